```python
import jax, jax.numpy as jnp
from jax import lax
import numpy as np

D_MODEL = 2048
BATCH = 8
SEQ = 4096
DEPTH = 2

CHUNK = 64
GLA_HEADS = 4
GLA_V = D_MODEL // 2
GLA_DV = GLA_V // GLA_HEADS
GLA_DK = GLA_DV // 2
GLA_QK = GLA_HEADS * GLA_DK
GLA_LOWRANK = 16
GLA_TAU = 16.0
CONV_WIDTH = D_MODEL // 2
CONV_K = 3
D_FF = 11 * D_MODEL // 4
EPS = 1e-6
SPLITS = (GLA_QK, GLA_QK, GLA_V, GLA_V, GLA_LOWRANK,
          CONV_WIDTH, CONV_WIDTH, CONV_WIDTH, D_MODEL, D_MODEL)
N_IN = sum(SPLITS)

kernel_name = "hybrid_gla_shortconv_convffn_adaln"


def rmsnorm(x, g):
    xf = x.astype(jnp.float32)
    y = xf * lax.rsqrt(jnp.mean(xf * xf, axis=-1, keepdims=True) + EPS)
    return (y * g.astype(jnp.float32)).astype(x.dtype)


def causal_dwconv(x, w):
    k = w.shape[0]
    s = x.shape[1]
    xp = jnp.pad(x, ((0, 0), (k - 1, 0), (0, 0)))
    y = w[0] * xp[:, 0:s]
    for i in range(1, k):
        y = y + w[i] * xp[:, i:i + s]
    return y


def gla_chunked(q, k, v, log_a):
    b, s, h, dk = q.shape
    dv = v.shape[-1]
    nc = s // CHUNK

    def to_chunks(t):
        return t.reshape(b, nc, CHUNK, h, t.shape[-1]).transpose(1, 0, 3, 2, 4)

    def step(state, inp):
        q_c, k_c, v_c, a_c = inp
        cum = jnp.cumsum(a_c, axis=2)
        cum_end = cum[:, :, -1:, :]
        k_dec = k_c * jnp.exp(cum_end - cum)
        state = (jnp.exp(cum_end[:, :, 0, :])[..., None] * state
                 + jnp.einsum('bhld,bhle->bhde', k_dec, v_c))
        o_c = jnp.einsum('bhld,bhde->bhle', q_c, state)
        return state, o_c

    s0 = jnp.zeros((b, h, dk, dv), jnp.float32)
    _, o = lax.scan(step, s0, (to_chunks(q), to_chunks(k), to_chunks(v), to_chunks(log_a)))
    return o.transpose(1, 0, 3, 2, 4).reshape(b, s, h, dv)


def token_mixer(h, w_in, w_a2, b_a2, gla_norm_g, w_out_gla, conv_mix_w, w_out_conv, w_o):
    bsz, s, _ = h.shape
    split_idx = [int(i) for i in np.cumsum(SPLITS)[:-1]]
    q, k, v, r, lr, cb, cc, cx, ga, gb = jnp.split(h @ w_in, split_idx, axis=-1)

    log_a = jax.nn.log_sigmoid((lr @ w_a2 + b_a2).astype(jnp.float32)) / GLA_TAU
    heads = lambda t, d: t.astype(jnp.float32).reshape(bsz, s, GLA_HEADS, d)
    o = gla_chunked(heads(q, GLA_DK) * (GLA_DK ** -0.5), heads(k, GLA_DK),
                    heads(v, GLA_DV), heads(log_a, GLA_DK))
    o = rmsnorm(o, gla_norm_g).astype(h.dtype).reshape(bsz, s, GLA_V)
    y_a = (o * jax.nn.silu(r)) @ w_out_gla

    y_b = (cb * causal_dwconv(cc * cx, conv_mix_w)) @ w_out_conv

    m = jax.nn.sigmoid(ga) * y_a + jax.nn.sigmoid(gb) * y_b
    return m @ w_o


def channel_mixer(h, w_up, ffn_conv_w, w_down):
    gate, up = jnp.split(h @ w_up, 2, axis=-1)
    return (jax.nn.gelu(causal_dwconv(gate, ffn_conv_w)) * up) @ w_down


def _fwd_setup_inputs(seed: int = 0) -> dict:
    key = jax.random.key(seed)
    ks = jax.random.split(key, 16)
    nrm = lambda k, shape, scale: jax.random.normal(k, shape, jnp.float32) * scale
    return {
        "x": nrm(ks[0], (BATCH, SEQ, D_MODEL), 1.0),
        "c": nrm(ks[1], (BATCH, D_MODEL), 1.0),
        "w_ada": nrm(ks[2], (DEPTH, D_MODEL, 6 * D_MODEL), 0.5 * D_MODEL ** -0.5),
        "b_ada": nrm(ks[3], (DEPTH, 6 * D_MODEL), 0.01),
        "norm_g": 1.0 + nrm(ks[4], (DEPTH, 4, D_MODEL), 0.05),
        "w_in": nrm(ks[5], (DEPTH, D_MODEL, N_IN), D_MODEL ** -0.5),
        "w_a2": nrm(ks[6], (DEPTH, GLA_LOWRANK, GLA_QK), GLA_LOWRANK ** -0.5),
        "b_a2": nrm(ks[7], (DEPTH, GLA_QK), 0.1),
        "gla_norm_g": 1.0 + nrm(ks[8], (DEPTH, GLA_DV), 0.05),
        "w_out_gla": nrm(ks[9], (DEPTH, GLA_V, D_MODEL), GLA_V ** -0.5),
        "conv_mix_w": nrm(ks[10], (DEPTH, CONV_K, CONV_WIDTH), 0.5),
        "w_out_conv": nrm(ks[11], (DEPTH, CONV_WIDTH, D_MODEL), CONV_WIDTH ** -0.5),
        "w_o": nrm(ks[12], (DEPTH, D_MODEL, D_MODEL), D_MODEL ** -0.5),
        "w_up": nrm(ks[13], (DEPTH, D_MODEL, 2 * D_FF), D_MODEL ** -0.5),
        "ffn_conv_w": nrm(ks[14], (DEPTH, CONV_K, D_FF), 0.5),
        "w_down": nrm(ks[15], (DEPTH, D_FF, D_MODEL), D_FF ** -0.5),
    }


def _fwd_reference(x, c, w_ada, b_ada, norm_g, w_in, w_a2, b_a2, gla_norm_g, w_out_gla,
              conv_mix_w, w_out_conv, w_o, w_up, ffn_conv_w, w_down):
    for l in range(DEPTH):
        mod = jax.nn.silu(c) @ w_ada[l] + b_ada[l]
        sh1, sc1, g1, sh2, sc2, g2 = [t[:, None, :] for t in jnp.split(mod, 6, axis=-1)]
        h = rmsnorm(x, norm_g[l, 0]) * (1.0 + sc1) + sh1
        y = token_mixer(h, w_in[l], w_a2[l], b_a2[l], gla_norm_g[l], w_out_gla[l],
                        conv_mix_w[l], w_out_conv[l], w_o[l])
        x = x + g1 * rmsnorm(y, norm_g[l, 1])
        h = rmsnorm(x, norm_g[l, 2]) * (1.0 + sc2) + sh2
        y = channel_mixer(h, w_up[l], ffn_conv_w[l], w_down[l])
        x = x + g2 * rmsnorm(y, norm_g[l, 3])
    return x


import jax as _jax
import jax.numpy as _jnp

TWIN_FORMAT = 'train_step'
FWD_PARAMS = ['x', 'c', 'w_ada', 'b_ada', 'norm_g', 'w_in', 'w_a2', 'b_a2', 'gla_norm_g', 'w_out_gla', 'conv_mix_w', 'w_out_conv', 'w_o', 'w_up', 'ffn_conv_w', 'w_down']
TWIN_WEIGHTS = ['w_ada', 'b_ada', 'norm_g', 'w_in', 'w_a2', 'b_a2', 'gla_norm_g', 'w_out_gla', 'conv_mix_w', 'w_out_conv', 'w_o', 'w_up', 'ffn_conv_w', 'w_down']
TWIN_DIFF_INPUT = 'x'
TWIN_INPUTS = ['x', 'c', 'w_ada', 'b_ada', 'norm_g', 'w_in', 'w_a2', 'b_a2', 'gla_norm_g', 'w_out_gla', 'conv_mix_w', 'w_out_conv', 'w_o', 'w_up', 'ffn_conv_w', 'w_down', 'loss_target', 'm_w_ada', 'm_b_ada', 'm_norm_g', 'm_w_in', 'm_w_a2', 'm_b_a2', 'm_gla_norm_g', 'm_w_out_gla', 'm_conv_mix_w', 'm_w_out_conv', 'm_w_o', 'm_w_up', 'm_ffn_conv_w', 'm_w_down', 'v_w_ada', 'v_b_ada', 'v_norm_g', 'v_w_in', 'v_w_a2', 'v_b_a2', 'v_gla_norm_g', 'v_w_out_gla', 'v_conv_mix_w', 'v_w_out_conv', 'v_w_o', 'v_w_up', 'v_ffn_conv_w', 'v_w_down']
TWIN_OUTPUTS = ['loss', 'grad_x', 'grad_w_ada', 'grad_b_ada', 'grad_norm_g', 'grad_w_in', 'grad_w_a2', 'grad_b_a2', 'grad_gla_norm_g', 'grad_w_out_gla', 'grad_conv_mix_w', 'grad_w_out_conv', 'grad_w_o', 'grad_w_up', 'grad_ffn_conv_w', 'grad_w_down', 'delta_w_ada', 'delta_b_ada', 'delta_norm_g', 'delta_w_in', 'delta_w_a2', 'delta_b_a2', 'delta_gla_norm_g', 'delta_w_out_gla', 'delta_conv_mix_w', 'delta_w_out_conv', 'delta_w_o', 'delta_w_up', 'delta_ffn_conv_w', 'delta_w_down', 'new_m_w_ada', 'new_m_b_ada', 'new_m_norm_g', 'new_m_w_in', 'new_m_w_a2', 'new_m_b_a2', 'new_m_gla_norm_g', 'new_m_w_out_gla', 'new_m_conv_mix_w', 'new_m_w_out_conv', 'new_m_w_o', 'new_m_w_up', 'new_m_ffn_conv_w', 'new_m_w_down', 'new_v_w_ada', 'new_v_b_ada', 'new_v_norm_g', 'new_v_w_in', 'new_v_w_a2', 'new_v_b_a2', 'new_v_gla_norm_g', 'new_v_w_out_gla', 'new_v_conv_mix_w', 'new_v_w_out_conv', 'new_v_w_o', 'new_v_w_up', 'new_v_ffn_conv_w', 'new_v_w_down']
TWIN_LEAF_KINDS = {'loss': 'loss', 'grad_x': 'grad_x', 'grad_w_ada': 'grad_w', 'grad_b_ada': 'grad_w', 'grad_norm_g': 'grad_w', 'grad_w_in': 'grad_w', 'grad_w_a2': 'grad_w', 'grad_b_a2': 'grad_w', 'grad_gla_norm_g': 'grad_w', 'grad_w_out_gla': 'grad_w', 'grad_conv_mix_w': 'grad_w', 'grad_w_out_conv': 'grad_w', 'grad_w_o': 'grad_w', 'grad_w_up': 'grad_w', 'grad_ffn_conv_w': 'grad_w', 'grad_w_down': 'grad_w', 'delta_w_ada': 'delta_w', 'delta_b_ada': 'delta_w', 'delta_norm_g': 'delta_w', 'delta_w_in': 'delta_w', 'delta_w_a2': 'delta_w', 'delta_b_a2': 'delta_w', 'delta_gla_norm_g': 'delta_w', 'delta_w_out_gla': 'delta_w', 'delta_conv_mix_w': 'delta_w', 'delta_w_out_conv': 'delta_w', 'delta_w_o': 'delta_w', 'delta_w_up': 'delta_w', 'delta_ffn_conv_w': 'delta_w', 'delta_w_down': 'delta_w', 'new_m_w_ada': 'new_m', 'new_m_b_ada': 'new_m', 'new_m_norm_g': 'new_m', 'new_m_w_in': 'new_m', 'new_m_w_a2': 'new_m', 'new_m_b_a2': 'new_m', 'new_m_gla_norm_g': 'new_m', 'new_m_w_out_gla': 'new_m', 'new_m_conv_mix_w': 'new_m', 'new_m_w_out_conv': 'new_m', 'new_m_w_o': 'new_m', 'new_m_w_up': 'new_m', 'new_m_ffn_conv_w': 'new_m', 'new_m_w_down': 'new_m', 'new_v_w_ada': 'new_v', 'new_v_b_ada': 'new_v', 'new_v_norm_g': 'new_v', 'new_v_w_in': 'new_v', 'new_v_w_a2': 'new_v', 'new_v_b_a2': 'new_v', 'new_v_gla_norm_g': 'new_v', 'new_v_w_out_gla': 'new_v', 'new_v_conv_mix_w': 'new_v', 'new_v_w_out_conv': 'new_v', 'new_v_w_o': 'new_v', 'new_v_w_up': 'new_v', 'new_v_ffn_conv_w': 'new_v', 'new_v_w_down': 'new_v'}


def _forward(args):
    return _fwd_reference(*[args[k] for k in FWD_PARAMS])


def _output_shape():
    def fwd():
        inp = _fwd_setup_inputs(0)
        return _fwd_reference(*[inp[k] for k in FWD_PARAMS])
    out = _jax.eval_shape(fwd)
    return out.shape, out.dtype

N_MICROBATCH = 1
ADAM_LR = 0.001
ADAM_B1 = 0.9
ADAM_B2 = 0.999
ADAM_EPS = 1e-08
ADAM_WD = 0.01
ADAM_STEP = 10
PER_EXAMPLE_BATCH_AXIS = {'x': 0, 'c': 0, 'loss_target': 0}
SHARED_INPUTS = []
_WEIGHT_DTYPES = {'w_ada': _jnp.float32, 'b_ada': _jnp.float32, 'norm_g': _jnp.float32, 'w_in': _jnp.float32, 'w_a2': _jnp.float32, 'b_a2': _jnp.float32, 'gla_norm_g': _jnp.float32, 'w_out_gla': _jnp.float32, 'conv_mix_w': _jnp.float32, 'w_out_conv': _jnp.float32, 'w_o': _jnp.float32, 'w_up': _jnp.float32, 'ffn_conv_w': _jnp.float32, 'w_down': _jnp.float32}
MOMENT_SCALE = {'w_ada': 5.965881e-01, 'b_ada': 1.298262e+00, 'norm_g': 1.140209e+00, 'w_in': 4.171536e-02, 'w_a2': 1.027065e-02, 'b_a2': 2.664295e-02, 'gla_norm_g': 7.617186e-02, 'w_out_gla': 2.608535e-02, 'conv_mix_w': 7.032832e-02, 'w_out_conv': 4.421178e-02, 'w_o': 5.270729e-02, 'w_up': 2.990202e-02, 'ffn_conv_w': 3.686741e-02, 'w_down': 5.158401e-02}


def _to_microbatches(a, axis):
    t = _jnp.moveaxis(a, axis, 0)
    t = t.reshape((N_MICROBATCH, t.shape[0] // N_MICROBATCH) + t.shape[1:])
    return _jnp.moveaxis(t, 1, axis + 1)


def setup_inputs(seed: int = 0) -> dict:
    inp = _fwd_setup_inputs(seed)
    key = _jax.random.fold_in(_jax.random.key(seed), 7919)
    shape, _ = _output_shape()
    out = dict(inp)
    out["loss_target"] = _jax.random.normal(_jax.random.fold_in(key, 0), shape, _jnp.float32)
    for i, name in enumerate(TWIN_WEIGHTS):
        w = inp[name].astype(_jnp.float32)
        if MOMENT_SCALE is None:
            s = _jnp.sqrt(_jnp.mean(_jnp.square(w)) + 1e-30)
        else:
            s = MOMENT_SCALE[name]
        km, kv = _jax.random.split(_jax.random.fold_in(key, i + 1))
        out[name] = w
        out["m_" + name] = s * _jax.random.normal(km, w.shape, _jnp.float32)
        out["v_" + name] = (s * s) * _jax.random.uniform(kv, w.shape, _jnp.float32, 0.5, 1.5)
    if N_MICROBATCH > 1:
        for name, axis in PER_EXAMPLE_BATCH_AXIS.items():
            out[name] = _to_microbatches(out[name], axis)
    return {'x': out['x'], 'c': out['c'], 'w_ada': out['w_ada'], 'b_ada': out['b_ada'], 'norm_g': out['norm_g'], 'w_in': out['w_in'], 'w_a2': out['w_a2'], 'b_a2': out['b_a2'], 'gla_norm_g': out['gla_norm_g'], 'w_out_gla': out['w_out_gla'], 'conv_mix_w': out['conv_mix_w'], 'w_out_conv': out['w_out_conv'], 'w_o': out['w_o'], 'w_up': out['w_up'], 'ffn_conv_w': out['ffn_conv_w'], 'w_down': out['w_down'], 'loss_target': out['loss_target'], 'm_w_ada': out['m_w_ada'], 'm_b_ada': out['m_b_ada'], 'm_norm_g': out['m_norm_g'], 'm_w_in': out['m_w_in'], 'm_w_a2': out['m_w_a2'], 'm_b_a2': out['m_b_a2'], 'm_gla_norm_g': out['m_gla_norm_g'], 'm_w_out_gla': out['m_w_out_gla'], 'm_conv_mix_w': out['m_conv_mix_w'], 'm_w_out_conv': out['m_w_out_conv'], 'm_w_o': out['m_w_o'], 'm_w_up': out['m_w_up'], 'm_ffn_conv_w': out['m_ffn_conv_w'], 'm_w_down': out['m_w_down'], 'v_w_ada': out['v_w_ada'], 'v_b_ada': out['v_b_ada'], 'v_norm_g': out['v_norm_g'], 'v_w_in': out['v_w_in'], 'v_w_a2': out['v_w_a2'], 'v_b_a2': out['v_b_a2'], 'v_gla_norm_g': out['v_gla_norm_g'], 'v_w_out_gla': out['v_w_out_gla'], 'v_conv_mix_w': out['v_conv_mix_w'], 'v_w_out_conv': out['v_w_out_conv'], 'v_w_o': out['v_w_o'], 'v_w_up': out['v_w_up'], 'v_ffn_conv_w': out['v_ffn_conv_w'], 'v_w_down': out['v_w_down']}


def _loss(weights, diff, rest, loss_target):
    with _jax.named_scope("forward"):
        args = {**rest, TWIN_DIFF_INPUT: diff, **{k: w.astype(_WEIGHT_DTYPES[k]) for k, w in weights.items()}}
        y = _forward(args)
    with _jax.named_scope("loss_head"):
        err = _jnp.square(y.astype(_jnp.float32) - loss_target)
        return 0.5 * _jnp.sum(_jnp.mean(err, axis=-1)) if err.ndim else 0.5 * err


def _adamw(w, g, m, v):
    m = ADAM_B1 * m + (1.0 - ADAM_B1) * g
    v = ADAM_B2 * v + (1.0 - ADAM_B2) * _jnp.square(g)
    m_hat = m / (1.0 - ADAM_B1 ** ADAM_STEP)
    v_hat = v / (1.0 - ADAM_B2 ** ADAM_STEP)
    delta = -ADAM_LR * (m_hat / (_jnp.sqrt(v_hat) + ADAM_EPS) + ADAM_WD * w)
    return delta, m, v


def reference(x, c, w_ada, b_ada, norm_g, w_in, w_a2, b_a2, gla_norm_g, w_out_gla, conv_mix_w, w_out_conv, w_o, w_up, ffn_conv_w, w_down, loss_target, m_w_ada, m_b_ada, m_norm_g, m_w_in, m_w_a2, m_b_a2, m_gla_norm_g, m_w_out_gla, m_conv_mix_w, m_w_out_conv, m_w_o, m_w_up, m_ffn_conv_w, m_w_down, v_w_ada, v_b_ada, v_norm_g, v_w_in, v_w_a2, v_b_a2, v_gla_norm_g, v_w_out_gla, v_conv_mix_w, v_w_out_conv, v_w_o, v_w_up, v_ffn_conv_w, v_w_down):
    given = dict(x=x, c=c, w_ada=w_ada, b_ada=b_ada, norm_g=norm_g, w_in=w_in, w_a2=w_a2, b_a2=b_a2, gla_norm_g=gla_norm_g, w_out_gla=w_out_gla, conv_mix_w=conv_mix_w, w_out_conv=w_out_conv, w_o=w_o, w_up=w_up, ffn_conv_w=ffn_conv_w, w_down=w_down, loss_target=loss_target, m_w_ada=m_w_ada, m_b_ada=m_b_ada, m_norm_g=m_norm_g, m_w_in=m_w_in, m_w_a2=m_w_a2, m_b_a2=m_b_a2, m_gla_norm_g=m_gla_norm_g, m_w_out_gla=m_w_out_gla, m_conv_mix_w=m_conv_mix_w, m_w_out_conv=m_w_out_conv, m_w_o=m_w_o, m_w_up=m_w_up, m_ffn_conv_w=m_ffn_conv_w, m_w_down=m_w_down, v_w_ada=v_w_ada, v_b_ada=v_b_ada, v_norm_g=v_norm_g, v_w_in=v_w_in, v_w_a2=v_w_a2, v_b_a2=v_b_a2, v_gla_norm_g=v_gla_norm_g, v_w_out_gla=v_w_out_gla, v_conv_mix_w=v_conv_mix_w, v_w_out_conv=v_w_out_conv, v_w_o=v_w_o, v_w_up=v_w_up, v_ffn_conv_w=v_ffn_conv_w, v_w_down=v_w_down)
    weights = {n: given[n] for n in TWIN_WEIGHTS}
    shared = {n: given[n] for n in SHARED_INPUTS}
    per_example = {n: given[n] for n in ['x', 'c']}
    grad_fn = _jax.value_and_grad(_loss, argnums=(0, 1))

    def one_microbatch(ex, loss_target):
        ex = dict(ex)
        diff = ex.pop(TWIN_DIFF_INPUT)
        return grad_fn(weights, diff, {**shared, **ex}, loss_target)

    if N_MICROBATCH == 1:
        loss, (grad_w, grad_x) = one_microbatch(per_example, given["loss_target"])
    else:
        def body(carry, xs):
            loss_sum, grad_sum = carry
            l_k, (gw_k, gx_k) = one_microbatch(xs[0], xs[1])
            with _jax.named_scope("update"):
                return (loss_sum + l_k, _jax.tree.map(_jnp.add, grad_sum, gw_k)), gx_k

        init = (_jnp.zeros((), _jnp.float32), _jax.tree.map(_jnp.zeros_like, weights))
        (loss, grad_w), grad_x = _jax.lax.scan(body, init, (per_example, given["loss_target"]))
    with _jax.named_scope("update"):
        delta_w, new_m, new_v = {}, {}, {}
        for n in TWIN_WEIGHTS:
            delta_w[n], new_m[n], new_v[n] = _adamw(weights[n], grad_w[n], given["m_" + n], given["v_" + n])
    return (loss, grad_x, *[grad_w[n] for n in TWIN_WEIGHTS], *[delta_w[n] for n in TWIN_WEIGHTS],
            *[new_m[n] for n in TWIN_WEIGHTS], *[new_v[n] for n in TWIN_WEIGHTS])
```

```python
import functools
import math

import jax
import jax.numpy as jnp
from jax import lax
from jax.experimental import pallas as pl
from jax.experimental.pallas import tpu as pltpu

F32 = jnp.float32
BF16 = jnp.bfloat16
MESH = pl.DeviceIdType.MESH

D_MODEL = 2048
DEPTH = 2
CHUNK = 64
GLA_HEADS = 4
GLA_DK = 128
GLA_DV = 256
GLA_QK = GLA_HEADS * GLA_DK
GLA_V = GLA_HEADS * GLA_DV
GLA_LOWRANK = 16
GLA_TAU = 16.0
CONV_WIDTH = 1024
D_FF = 5632
EPS = 1e-6
N_IN = 10256
LR_PAD = 128
N_IN_PAD = N_IN - GLA_LOWRANK + LR_PAD
OFF_Q, OFF_K, OFF_V, OFF_R = 0, 512, 1024, 2048
OFF_CB, OFF_CC, OFF_CX, OFF_GA, OFF_GB, OFF_LR = 3072, 4096, 5120, 6144, 8192, 10240

ADAM_LR = 0.001
ADAM_B1 = 0.9
ADAM_B2 = 0.999
ADAM_EPS = 1e-08
ADAM_WD = 0.01
ADAM_STEP = 10

N_CHIPS = 4
N_DEV = 8
VMEM_LIMIT = 56 * 1024 * 1024
TM_ROW = 256
TM_EW = 512
CW_EW = 512
GLA_ROWS = 256


def _params(sem=None):
    return pltpu.CompilerParams(dimension_semantics=sem, vmem_limit_bytes=VMEM_LIMIT)


def _sigmoid(v):
    return 1.0 / (1.0 + jnp.exp(-v))


def _log_sigmoid(v):
    return jnp.minimum(v, 0.0) - jnp.log(1.0 + jnp.exp(-jnp.abs(v)))


_GELU_C = math.sqrt(2.0 / math.pi)


def _gelu(v):
    return 0.5 * v * (1.0 + jnp.tanh(_GELU_C * (v + 0.044715 * v * v * v)))


def _gelu_grad(v):
    t = jnp.tanh(_GELU_C * (v + 0.044715 * v * v * v))
    return 0.5 * (1.0 + t) + 0.5 * v * (1.0 - t * t) * _GELU_C * (1.0 + 3.0 * 0.044715 * v * v)


def _flip(a, d):
    return a + d - 2 * a * d


def _unless(cond):
    return jnp.where(cond, 0.0, 1.0).astype(F32)


def _allgather8(xv, name):
    r, cdim = xv.shape

    def body(x_ref, out_ref, sum_ref, send_sems, recv_sems):
        xi, yi, ci = lax.axis_index("x"), lax.axis_index("y"), lax.axis_index("c")
        me = 4 * xi + 2 * yi + ci
        out_ref[pl.ds(me, 1)] = x_ref[...][None]
        started = []
        for k in range(1, N_DEV):
            px, py, pc = _flip(xi, (k >> 2) & 1), _flip(yi, (k >> 1) & 1), _flip(ci, k & 1)
            cp = pltpu.make_async_remote_copy(
                src_ref=x_ref, dst_ref=out_ref.at[me], send_sem=send_sems.at[k - 1], recv_sem=recv_sems.at[k - 1],
                device_id=(px, py, pc), device_id_type=MESH)
            cp.start()
            started.append((cp, 4 * px + 2 * py + pc, k, (px, py, pc)))
        for cp, peer, k, pid in started:
            cp.wait_send()
            pltpu.make_async_remote_copy(
                src_ref=x_ref, dst_ref=out_ref.at[peer], send_sem=send_sems.at[k - 1], recv_sem=recv_sems.at[k - 1],
                device_id=pid, device_id_type=MESH).wait_recv()
        acc = out_ref[0]
        for d in range(1, N_DEV):
            acc = acc + out_ref[d]
        sum_ref[...] = acc

    return pl.pallas_call(
        body, name=name,
        out_shape=(jax.ShapeDtypeStruct((N_DEV, r, cdim), F32), jax.ShapeDtypeStruct((r, cdim), F32)),
        in_specs=[pl.BlockSpec(memory_space=pltpu.VMEM)],
        out_specs=(pl.BlockSpec(memory_space=pltpu.VMEM), pl.BlockSpec(memory_space=pltpu.VMEM)),
        scratch_shapes=[pltpu.SemaphoreType.DMA((N_DEV - 1,)), pltpu.SemaphoreType.DMA((N_DEV - 1,))],
        compiler_params=pltpu.CompilerParams(vmem_limit_bytes=VMEM_LIMIT),
    )(xv)


def _chip_exchange(arrays, scatter, name):
    n = len(arrays)
    flips = ((1, 0), (0, 1), (1, 1))

    def body(*refs):
        ins, outs = refs[:n], refs[n:2 * n]
        send_sems, recv_sems, loc_sems = refs[2 * n:]
        xi, yi, ci = lax.axis_index("x"), lax.axis_index("y"), lax.axis_index("c")
        me = 2 * xi + yi
        local, remote = [], []
        for i in range(n):
            src = ins[i].at[me] if scatter else ins[i]
            cp = pltpu.make_async_copy(src, outs[i].at[me], loc_sems.at[i])
            cp.start()
            local.append(cp)
        for i in range(n):
            for k, (dx, dy) in enumerate(flips):
                px, py = _flip(xi, dx), _flip(yi, dy)
                peer = 2 * px + py
                src = ins[i].at[peer] if scatter else ins[i]
                cp = pltpu.make_async_remote_copy(
                    src_ref=src, dst_ref=outs[i].at[me], send_sem=send_sems.at[3 * i + k],
                    recv_sem=recv_sems.at[3 * i + k], device_id=(px, py, ci), device_id_type=MESH)
                cp.start()
                remote.append((cp, src, i, k, peer, (px, py, ci)))
        for cp in local:
            cp.wait()
        for cp, src, i, k, peer, pid in remote:
            cp.wait_send()
            pltpu.make_async_remote_copy(
                src_ref=src, dst_ref=outs[i].at[peer], send_sem=send_sems.at[3 * i + k],
                recv_sem=recv_sems.at[3 * i + k], device_id=pid, device_id_type=MESH).wait_recv()

    if scatter:
        out_shape = tuple(jax.ShapeDtypeStruct(a.shape, a.dtype) for a in arrays)
    else:
        out_shape = tuple(jax.ShapeDtypeStruct((N_CHIPS,) + a.shape, a.dtype) for a in arrays)
    return pl.pallas_call(
        body, name=name, out_shape=out_shape,
        in_specs=[pl.BlockSpec(memory_space=pl.ANY)] * n,
        out_specs=tuple(pl.BlockSpec(memory_space=pl.ANY) for _ in range(n)),
        scratch_shapes=[pltpu.SemaphoreType.DMA((3 * n,)), pltpu.SemaphoreType.DMA((3 * n,)),
                        pltpu.SemaphoreType.DMA((n,))],
    )(*arrays)


def _sibling_exchange(arrays, name):
    n = len(arrays)

    def body(*refs):
        ins, outs = refs[:n], refs[n:2 * n]
        send_sems, recv_sems = refs[2 * n:]
        xi, yi, ci = lax.axis_index("x"), lax.axis_index("y"), lax.axis_index("c")
        cps = []
        for i in range(n):
            cp = pltpu.make_async_remote_copy(
                src_ref=ins[i], dst_ref=outs[i], send_sem=send_sems.at[i], recv_sem=recv_sems.at[i],
                device_id=(xi, yi, 1 - ci), device_id_type=MESH)
            cp.start()
            cps.append(cp)
        for cp in cps:
            cp.wait()

    return pl.pallas_call(
        body, name=name, out_shape=tuple(jax.ShapeDtypeStruct(a.shape, a.dtype) for a in arrays),
        in_specs=[pl.BlockSpec(memory_space=pl.ANY)] * n,
        out_specs=tuple(pl.BlockSpec(memory_space=pl.ANY) for _ in range(n)),
        scratch_shapes=[pltpu.SemaphoreType.DMA((n,)), pltpu.SemaphoreType.DMA((n,))],
    )(*arrays)


def _pick(dim, pref):
    if dim <= pref:
        return dim
    t = (pref // 128) * 128
    while t >= 128:
        if dim % t == 0:
            return t
        t -= 128
    return dim


def _matmul(a, b, dims, out_dtype, name, tm=512, tn=1024, tk=2048):
    if dims == "nn":
        (m, kd), (_, n) = a.shape, b.shape
    elif dims == "nt":
        (m, kd), (n, _) = a.shape, b.shape
    else:
        (kd, m), (_, n) = a.shape, b.shape
    tm, tn, tk = _pick(m, tm), _pick(n, tn), _pick(kd, tk)
    nk = kd // tk
    if dims == "nn":
        a_spec = pl.BlockSpec((tm, tk), lambda j, i, k: (i, k))
        b_spec = pl.BlockSpec((tk, tn), lambda j, i, k: (k, j))
        dn = (((1,), (0,)), ((), ()))
    elif dims == "nt":
        a_spec = pl.BlockSpec((tm, tk), lambda j, i, k: (i, k))
        b_spec = pl.BlockSpec((tn, tk), lambda j, i, k: (j, k))
        dn = (((1,), (1,)), ((), ()))
    else:
        a_spec = pl.BlockSpec((tk, tm), lambda j, i, k: (k, i))
        b_spec = pl.BlockSpec((tk, tn), lambda j, i, k: (k, j))
        dn = (((0,), (0,)), ((), ()))

    def body(a_ref, b_ref, o_ref, acc_ref):
        part = lax.dot_general(a_ref[...].astype(BF16), b_ref[...].astype(BF16), dn, preferred_element_type=F32)
        if nk == 1:
            o_ref[...] = part.astype(o_ref.dtype)
        else:
            k = pl.program_id(2)

            @pl.when(k == 0)
            def _():
                acc_ref[...] = part

            @pl.when(k > 0)
            def _():
                acc_ref[...] += part

            @pl.when(k == nk - 1)
            def _():
                o_ref[...] = acc_ref[...].astype(o_ref.dtype)

    return pl.pallas_call(
        body, name=name, out_shape=jax.ShapeDtypeStruct((m, n), out_dtype),
        grid=(n // tn, m // tm, nk),
        in_specs=[a_spec, b_spec],
        out_specs=pl.BlockSpec((tm, tn), lambda j, i, k: (i, j)),
        scratch_shapes=[pltpu.VMEM((tm, tn), F32)],
        compiler_params=_params(("parallel", "parallel", "arbitrary")),
    )(a, b)


def _rstd(v):
    return lax.rsqrt(jnp.mean(v * v, axis=-1, keepdims=True) + EPS)


def _row(tm):
    return pl.BlockSpec((tm, D_MODEL), lambda i: (i, 0))


_VEC = pl.BlockSpec((1, D_MODEL), lambda i: (0, 0))


def _pre_norm(x, gn, sc, sh, name):
    t = x.shape[0]
    tm = min(TM_ROW, t)

    def body(x_ref, gn_ref, sc_ref, sh_ref, h_ref):
        xv = x_ref[...]
        h_ref[...] = ((xv * _rstd(xv) * gn_ref[...]) * (1.0 + sc_ref[...]) + sh_ref[...]).astype(BF16)

    return pl.pallas_call(
        body, name=name, out_shape=jax.ShapeDtypeStruct((t, D_MODEL), BF16), grid=(t // tm,),
        in_specs=[_row(tm), _VEC, _VEC, _VEC], out_specs=_row(tm),
        compiler_params=_params(("parallel",)),
    )(x, gn, sc, sh)


def _post_pre(x, y, g, gnp, gn, sc, sh, name):
    t = x.shape[0]
    tm = min(TM_ROW, t)

    def body(x_ref, y_ref, g_ref, gnp_ref, gn_ref, sc_ref, sh_ref, x1_ref, h_ref):
        yv = y_ref[...]
        x1 = x_ref[...] + g_ref[...] * (yv * _rstd(yv) * gnp_ref[...])
        x1_ref[...] = x1
        h_ref[...] = ((x1 * _rstd(x1) * gn_ref[...]) * (1.0 + sc_ref[...]) + sh_ref[...]).astype(BF16)

    return pl.pallas_call(
        body, name=name,
        out_shape=(jax.ShapeDtypeStruct((t, D_MODEL), F32), jax.ShapeDtypeStruct((t, D_MODEL), BF16)),
        grid=(t // tm,),
        in_specs=[_row(tm), _row(tm), _VEC, _VEC, _VEC, _VEC, _VEC], out_specs=(_row(tm), _row(tm)),
        compiler_params=_params(("parallel",)),
    )(x, y, g, gnp, gn, sc, sh)


def _post_loss(x, y, g, gnp, tgt, name):
    t = x.shape[0]
    tm = min(TM_ROW, t)

    def body(x_ref, y_ref, g_ref, gnp_ref, t_ref, dx_ref, loss_ref):
        yv = y_ref[...]
        diff = x_ref[...] + g_ref[...] * (yv * _rstd(yv) * gnp_ref[...]) - t_ref[...]
        dx_ref[...] = diff * (1.0 / D_MODEL)
        part = (0.5 / D_MODEL) * jnp.sum(jnp.sum(diff * diff, axis=-1, keepdims=True), axis=0, keepdims=True)

        @pl.when(pl.program_id(0) == 0)
        def _():
            loss_ref[...] = jnp.zeros_like(loss_ref)

        loss_ref[...] += jnp.broadcast_to(part, loss_ref.shape)

    return pl.pallas_call(
        body, name=name,
        out_shape=(jax.ShapeDtypeStruct((t, D_MODEL), F32), jax.ShapeDtypeStruct((8, 128), F32)),
        grid=(t // tm,),
        in_specs=[_row(tm), _row(tm), _VEC, _VEC, _row(tm)],
        out_specs=(_row(tm), pl.BlockSpec((8, 128), lambda i: (0, 0))),
        compiler_params=_params(("arbitrary",)),
    )(x, y, g, gnp, tgt)


def _acc_rows(ref, val):
    @pl.when(pl.program_id(0) == 0)
    def _():
        ref[...] = jnp.zeros_like(ref)

    ref[...] += jnp.sum(val, axis=0, keepdims=True)


def _post_bwd(dxn, y, g, gnp, name):
    t = y.shape[0]
    tm = min(TM_ROW, t)

    def body(dx_ref, y_ref, g_ref, gnp_ref, dy_ref, dg_ref, dgn_ref):
        yv, dxv = y_ref[...], dx_ref[...]
        r = _rstd(yv)
        yh = yv * r
        _acc_rows(dg_ref, dxv * (yh * gnp_ref[...]))
        dn = dxv * g_ref[...]
        _acc_rows(dgn_ref, dn * yh)
        dyh = dn * gnp_ref[...]
        dy_ref[...] = (r * (dyh - yh * jnp.mean(dyh * yh, axis=-1, keepdims=True))).astype(BF16)

    return pl.pallas_call(
        body, name=name,
        out_shape=(jax.ShapeDtypeStruct((t, D_MODEL), BF16), jax.ShapeDtypeStruct((1, D_MODEL), F32),
                   jax.ShapeDtypeStruct((1, D_MODEL), F32)),
        grid=(t // tm,),
        in_specs=[_row(tm), _row(tm), _VEC, _VEC], out_specs=(_row(tm), _VEC, _VEC),
        compiler_params=_params(("arbitrary",)),
    )(dxn, y, g, gnp)


def _pre_bwd(dh, xin, dres, gn, sc, name):
    t = xin.shape[0]
    tm = min(TM_ROW, t)

    def body(dh_ref, x_ref, dres_ref, gn_ref, sc_ref, dx_ref, dsh_ref, dsc_ref, dgn_ref):
        xv, dhv = x_ref[...], dh_ref[...]
        r = _rstd(xv)
        xh = xv * r
        _acc_rows(dsh_ref, dhv)
        _acc_rows(dsc_ref, dhv * (xh * gn_ref[...]))
        dn = dhv * (1.0 + sc_ref[...])
        _acc_rows(dgn_ref, dn * xh)
        dxh = dn * gn_ref[...]
        dx_ref[...] = dres_ref[...] + r * (dxh - xh * jnp.mean(dxh * xh, axis=-1, keepdims=True))

    vec = jax.ShapeDtypeStruct((1, D_MODEL), F32)
    return pl.pallas_call(
        body, name=name, out_shape=(jax.ShapeDtypeStruct((t, D_MODEL), F32), vec, vec, vec),
        grid=(t // tm,),
        in_specs=[_row(tm), _row(tm), _row(tm), _VEC, _VEC], out_specs=(_row(tm), _VEC, _VEC, _VEC),
        compiler_params=_params(("arbitrary",)),
    )(dh, xin, dres, gn, sc)


def _shift_down(v, halo, s):
    tm = v.shape[0]
    out = pltpu.roll(v, s, 0)
    row = lax.broadcasted_iota(jnp.int32, v.shape, 0)
    for j in range(s):
        out = jnp.where(row == j, jnp.broadcast_to(halo[8 - s + j:8 - s + j + 1, :], v.shape), out)
    return out


def _shift_up(v, halo, s):
    tm = v.shape[0]
    out = pltpu.roll(v, tm - s, 0)
    row = lax.broadcasted_iota(jnp.int32, v.shape, 0)
    for j in range(s):
        out = jnp.where(row == tm - s + j, jnp.broadcast_to(halo[j:j + 1, :], v.shape), out)
    return out


def _tile_specs(tm, cw, off, nrow):
    ob = off // cw
    r8 = tm // 8
    main = pl.BlockSpec((tm, cw), lambda j, i: (i, ob + j))
    prev = pl.BlockSpec((8, cw), lambda j, i: (jnp.maximum(i * r8 - 1, 0), ob + j))
    nxt = pl.BlockSpec((8, cw), lambda j, i: (jnp.minimum((i + 1) * r8, nrow * r8 - 1), ob + j))
    return main, prev, nxt


def _conv_fwd(p, w, name):
    t = p.shape[0]
    tm, cw = min(TM_EW, t), CW_EW
    nrow = t // tm
    cb_s, _, _ = _tile_specs(tm, cw, OFF_CB, nrow)
    cc_s, cc_p, _ = _tile_specs(tm, cw, OFF_CC, nrow)
    cx_s, cx_p, _ = _tile_specs(tm, cw, OFF_CX, nrow)

    def body(cb_ref, cc_ref, ccp_ref, cx_ref, cxp_ref, w_ref, z_ref):
        u = cc_ref[...] * cx_ref[...]
        uh = ccp_ref[...] * cxp_ref[...] * _unless(pl.program_id(1) == 0)
        wv = w_ref[...]
        conv = wv[2:3, :] * u + wv[1:2, :] * _shift_down(u, uh, 1) + wv[0:1, :] * _shift_down(u, uh, 2)
        z_ref[...] = (cb_ref[...] * conv).astype(BF16)

    return pl.pallas_call(
        body, name=name, out_shape=jax.ShapeDtypeStruct((t, CONV_WIDTH), BF16),
        grid=(CONV_WIDTH // cw, nrow),
        in_specs=[cb_s, cc_s, cc_p, cx_s, cx_p, pl.BlockSpec((8, cw), lambda j, i: (0, j))],
        out_specs=pl.BlockSpec((tm, cw), lambda j, i: (i, j)),
        compiler_params=_params(("parallel", "arbitrary")),
    )(p, p, p, p, p, w)


def _acc_w(ref, vals):
    @pl.when(pl.program_id(1) == 0)
    def _():
        ref[...] = jnp.zeros_like(ref)

    for j, v in enumerate(vals):
        ref[j:j + 1, :] += jnp.sum(v, axis=0, keepdims=True)


def _conv_bwd(dz, p, w, name):
    t = p.shape[0]
    tm, cw = min(TM_EW, t), CW_EW
    nrow = t // tm
    dz_s, _, dz_n = _tile_specs(tm, cw, 0, nrow)
    cb_s, _, cb_n = _tile_specs(tm, cw, OFF_CB, nrow)
    cc_s, cc_p, _ = _tile_specs(tm, cw, OFF_CC, nrow)
    cx_s, cx_p, _ = _tile_specs(tm, cw, OFF_CX, nrow)

    def body(dz_ref, dzn_ref, cb_ref, cbn_ref, cc_ref, ccp_ref, cx_ref, cxp_ref, w_ref,
             dcb_ref, dcc_ref, dcx_ref, dw_ref):
        i = pl.program_id(1)
        ccv, cxv, dzv = cc_ref[...], cx_ref[...], dz_ref[...]
        u = ccv * cxv
        uh = ccp_ref[...] * cxp_ref[...] * _unless(i == 0)
        wv = w_ref[...]
        u1, u2 = _shift_down(u, uh, 1), _shift_down(u, uh, 2)
        conv = wv[2:3, :] * u + wv[1:2, :] * u1 + wv[0:1, :] * u2
        dcb_ref[...] = (dzv * conv).astype(BF16)
        dconv = dzv * cb_ref[...]
        dch = dzn_ref[...] * cbn_ref[...] * _unless(i == nrow - 1)
        du = wv[2:3, :] * dconv + wv[1:2, :] * _shift_up(dconv, dch, 1) + wv[0:1, :] * _shift_up(dconv, dch, 2)
        dcc_ref[...] = (du * cxv).astype(BF16)
        dcx_ref[...] = (du * ccv).astype(BF16)
        _acc_w(dw_ref, (dconv * u2, dconv * u1, dconv * u))

    o_s = pl.BlockSpec((tm, cw), lambda j, i: (i, j))
    o_sh = jax.ShapeDtypeStruct((t, CONV_WIDTH), BF16)
    w_s = pl.BlockSpec((8, cw), lambda j, i: (0, j))
    return pl.pallas_call(
        body, name=name, out_shape=(o_sh, o_sh, o_sh, jax.ShapeDtypeStruct((8, CONV_WIDTH), F32)),
        grid=(CONV_WIDTH // cw, nrow),
        in_specs=[dz_s, dz_n, cb_s, cb_n, cc_s, cc_p, cx_s, cx_p, w_s],
        out_specs=(o_s, o_s, o_s, w_s),
        compiler_params=_params(("parallel", "arbitrary")),
    )(dz, dz, p, p, p, p, p, p, w)


def _ffn_fwd(u, w, name):
    t = u.shape[0]
    tm, cw = min(TM_EW, t), CW_EW
    nrow = t // tm
    g_s, g_p, _ = _tile_specs(tm, cw, 0, nrow)
    u_s, _, _ = _tile_specs(tm, cw, D_FF, nrow)

    def body(g_ref, gp_ref, u_ref, w_ref, f_ref):
        gv = g_ref[...]
        gh = gp_ref[...] * _unless(pl.program_id(1) == 0)
        wv = w_ref[...]
        gc = wv[2:3, :] * gv + wv[1:2, :] * _shift_down(gv, gh, 1) + wv[0:1, :] * _shift_down(gv, gh, 2)
        f_ref[...] = (_gelu(gc) * u_ref[...]).astype(BF16)

    return pl.pallas_call(
        body, name=name, out_shape=jax.ShapeDtypeStruct((t, D_FF), BF16),
        grid=(D_FF // cw, nrow),
        in_specs=[g_s, g_p, u_s, pl.BlockSpec((8, cw), lambda j, i: (0, j))],
        out_specs=pl.BlockSpec((tm, cw), lambda j, i: (i, j)),
        compiler_params=_params(("parallel", "arbitrary")),
    )(u, u, u, w)


def _ffn_bwd(df, u, w, name):
    t = u.shape[0]
    tm, cw = min(TM_EW, t), CW_EW
    nrow = t // tm
    df_s, _, df_n = _tile_specs(tm, cw, 0, nrow)
    g_s, g_p, g_n = _tile_specs(tm, cw, 0, nrow)
    u_s, _, u_n = _tile_specs(tm, cw, D_FF, nrow)
    r8 = tm // 8

    def body(df_ref, dfn_ref, g_ref, gp_ref, gn_ref, u_ref, un_ref, w_ref, dg_ref, du_ref, dw_ref):
        i = pl.program_id(1)
        gv, dfv, uv = g_ref[...], df_ref[...], u_ref[...]
        gh = gp_ref[...] * _unless(i == 0)
        wv = w_ref[...]
        g1, g2 = _shift_down(gv, gh, 1), _shift_down(gv, gh, 2)
        gc = wv[2:3, :] * gv + wv[1:2, :] * g1 + wv[0:1, :] * g2
        du_ref[...] = (dfv * _gelu(gc)).astype(BF16)
        dgc = dfv * uv * _gelu_grad(gc)
        gnv = gn_ref[...]
        gtail = gv[tm - 8:tm, :]
        gcn = (wv[2:3, :] * gnv + wv[1:2, :] * _shift_down(gnv, gtail, 1) + wv[0:1, :] * _shift_down(gnv, gtail, 2))
        dgcn = dfn_ref[...] * un_ref[...] * _gelu_grad(gcn) * _unless(i == nrow - 1)
        dg = wv[2:3, :] * dgc + wv[1:2, :] * _shift_up(dgc, dgcn, 1) + wv[0:1, :] * _shift_up(dgc, dgcn, 2)
        dg_ref[...] = dg.astype(BF16)
        _acc_w(dw_ref, (dgc * g2, dgc * g1, dgc * gv))

    o_s = pl.BlockSpec((tm, cw), lambda j, i: (i, j))
    o_sh = jax.ShapeDtypeStruct((t, D_FF), BF16)
    w_s = pl.BlockSpec((8, cw), lambda j, i: (0, j))
    return pl.pallas_call(
        body, name=name, out_shape=(o_sh, o_sh, jax.ShapeDtypeStruct((8, D_FF), F32)),
        grid=(D_FF // cw, nrow),
        in_specs=[df_s, df_n, g_s, g_p, g_n, u_s, u_n, w_s],
        out_specs=(o_s, o_s, w_s),
        compiler_params=_params(("parallel", "arbitrary")),
    )(df, df, u, u, u, u, u, w)


def _merge_fwd(ya, yb, p, name):
    t = ya.shape[0]
    tm, cw = min(TM_EW, t), CW_EW
    y_s = pl.BlockSpec((tm, cw), lambda i, j: (i, j))

    def body(ya_ref, yb_ref, ga_ref, gb_ref, m_ref):
        m_ref[...] = (_sigmoid(ga_ref[...]) * ya_ref[...] + _sigmoid(gb_ref[...]) * yb_ref[...]).astype(BF16)

    return pl.pallas_call(
        body, name=name, out_shape=jax.ShapeDtypeStruct((t, D_MODEL), BF16),
        grid=(t // tm, D_MODEL // cw),
        in_specs=[y_s, y_s, pl.BlockSpec((tm, cw), lambda i, j: (i, OFF_GA // cw + j)),
                  pl.BlockSpec((tm, cw), lambda i, j: (i, OFF_GB // cw + j))],
        out_specs=y_s, compiler_params=_params(("parallel", "parallel")),
    )(ya, yb, p, p)


def _merge_bwd(dm, ya, yb, p, name):
    t = ya.shape[0]
    tm, cw = min(TM_EW, t), CW_EW
    y_s = pl.BlockSpec((tm, cw), lambda i, j: (i, j))

    def body(dm_ref, ya_ref, yb_ref, ga_ref, gb_ref, dya_ref, dyb_ref, dga_ref, dgb_ref):
        dmv = dm_ref[...]
        sa, sb = _sigmoid(ga_ref[...]), _sigmoid(gb_ref[...])
        dya_ref[...] = (dmv * sa).astype(BF16)
        dyb_ref[...] = (dmv * sb).astype(BF16)
        dga_ref[...] = (dmv * ya_ref[...] * sa * (1.0 - sa)).astype(BF16)
        dgb_ref[...] = (dmv * yb_ref[...] * sb * (1.0 - sb)).astype(BF16)

    o_sh = jax.ShapeDtypeStruct((t, D_MODEL), BF16)
    return pl.pallas_call(
        body, name=name, out_shape=(o_sh, o_sh, o_sh, o_sh),
        grid=(t // tm, D_MODEL // cw),
        in_specs=[y_s, y_s, y_s, pl.BlockSpec((tm, cw), lambda i, j: (i, OFF_GA // cw + j)),
                  pl.BlockSpec((tm, cw), lambda i, j: (i, OFF_GB // cw + j))],
        out_specs=(y_s, y_s, y_s, y_s), compiler_params=_params(("parallel", "parallel")),
    )(dm, ya, yb, p, p)


def _tri(lower):
    r = lax.broadcasted_iota(jnp.int32, (CHUNK, CHUNK), 0)
    c = lax.broadcasted_iota(jnp.int32, (CHUNK, CHUNK), 1)
    return ((c <= r) if lower else (c >= r)).astype(F32)


def _eye_mask():
    r = lax.broadcasted_iota(jnp.int32, (GLA_DK, GLA_DK), 0)
    c = lax.broadcasted_iota(jnp.int32, (GLA_DK, GLA_DK), 1)
    return r == c


def _row_to_col(v):
    return jnp.sum(jnp.where(_eye_mask(), jnp.broadcast_to(v, (GLA_DK, GLA_DK)), 0.0), axis=1, keepdims=True)


def _col_to_row(v):
    return jnp.sum(jnp.where(_eye_mask(), jnp.broadcast_to(v, (GLA_DK, GLA_DK)), 0.0), axis=0, keepdims=True)


def _dot(a, b, dn):
    return lax.dot_general(a.astype(BF16), b.astype(BF16), (dn, ((), ())), preferred_element_type=F32)


_NN = ((1,), (0,))
_NT = ((1,), (1,))
_TN = ((0,), (0,))


def _gate_logits(lr_ref, wa_ref, ba_ref):
    return _dot(lr_ref[...], wa_ref[...], _NN) + ba_ref[...]


def _chunk_decay(la, tri):
    cum = lax.dot_general(tri, la, ((_NN), ((), ())), precision=lax.Precision.HIGHEST, preferred_element_type=F32)
    e = cum[CHUNK - 1:CHUNK, :]
    return cum, e, jnp.exp(e - cum)


def _gla_fwd(p, wa, ba, name):
    t = p.shape[0]
    rows = min(GLA_ROWS, t)
    cb = rows // CHUNK
    nc = t // CHUNK
    scale = GLA_DK ** -0.5

    def body(q_ref, k_ref, v_ref, lr_ref, wa_ref, ba_ref, o_ref, st_ref, s_scr):
        @pl.when(pl.program_id(0) == 0)
        def _():
            s_scr[...] = jnp.zeros_like(s_scr)

        la_all = _log_sigmoid(_gate_logits(lr_ref, wa_ref, ba_ref)) * (1.0 / GLA_TAU)
        tri = _tri(True)
        for ch in range(cb):
            rs = slice(ch * CHUNK, (ch + 1) * CHUNK)
            for h in range(GLA_HEADS):
                ks = slice(h * GLA_DK, (h + 1) * GLA_DK)
                vs = slice(h * GLA_DV, (h + 1) * GLA_DV)
                _, e, w = _chunk_decay(la_all[rs, ks], tri)
                kd = k_ref[rs, ks] * w
                s_new = _row_to_col(jnp.exp(e)) * s_scr[ks, :] + _dot(kd, v_ref[rs, vs], _TN)
                s_scr[ks, :] = s_new
                st_ref[ch, ks, :] = s_new
                o_ref[rs, vs] = _dot(q_ref[rs, ks] * scale, s_new, _NN)

    return pl.pallas_call(
        body, name=name,
        out_shape=(jax.ShapeDtypeStruct((t, GLA_V), F32), jax.ShapeDtypeStruct((nc, GLA_QK, GLA_DV), F32)),
        grid=(t // rows,),
        in_specs=[pl.BlockSpec((rows, GLA_QK), lambda i: (i, OFF_Q // GLA_QK)),
                  pl.BlockSpec((rows, GLA_QK), lambda i: (i, OFF_K // GLA_QK)),
                  pl.BlockSpec((rows, GLA_V), lambda i: (i, OFF_V // GLA_V)),
                  pl.BlockSpec((rows, LR_PAD), lambda i: (i, OFF_LR // LR_PAD)),
                  pl.BlockSpec((LR_PAD, GLA_QK), lambda i: (0, 0)),
                  pl.BlockSpec((1, GLA_QK), lambda i: (0, 0))],
        out_specs=(pl.BlockSpec((rows, GLA_V), lambda i: (i, 0)),
                   pl.BlockSpec((cb, GLA_QK, GLA_DV), lambda i: (i, 0, 0))),
        scratch_shapes=[pltpu.VMEM((GLA_QK, GLA_DV), F32)],
        compiler_params=_params(("arbitrary",)),
    )(p, p, p, p, wa, ba)


def _gla_bwd(do, p, st, wa, ba, name):
    t = p.shape[0]
    rows = min(GLA_ROWS, t)
    cb = rows // CHUNK
    nb = t // rows
    scale = GLA_DK ** -0.5

    def rev(i):
        return nb - 1 - i

    def body(do_ref, q_ref, k_ref, v_ref, lr_ref, st_ref, stp_ref, wa_ref, ba_ref,
             dq_ref, dk_ref, dv_ref, dlr_ref, dwa_ref, dba_ref, ds_scr, dz_scr):
        i = pl.program_id(0)

        @pl.when(i == 0)
        def _():
            ds_scr[...] = jnp.zeros_like(ds_scr)
            dwa_ref[...] = jnp.zeros_like(dwa_ref)
            dba_ref[...] = jnp.zeros_like(dba_ref)

        z_all = _gate_logits(lr_ref, wa_ref, ba_ref)
        la_all = _log_sigmoid(z_all) * (1.0 / GLA_TAU)
        tri, triu = _tri(True), _tri(False)
        last_row = lax.broadcasted_iota(jnp.int32, (CHUNK, GLA_DK), 0) == CHUNK - 1
        keep_prev = _unless(i == nb - 1)
        for ch in reversed(range(cb)):
            rs = slice(ch * CHUNK, (ch + 1) * CHUNK)
            for h in range(GLA_HEADS):
                ks = slice(h * GLA_DK, (h + 1) * GLA_DK)
                vs = slice(h * GLA_DV, (h + 1) * GLA_DV)
                _, e, w = _chunk_decay(la_all[rs, ks], tri)
                kd = k_ref[rs, ks] * w
                exp_e = jnp.exp(e)
                s_c = st_ref[ch, ks, :]
                if ch > 0:
                    s_p = st_ref[ch - 1, ks, :]
                else:
                    s_p = stp_ref[0, ks, :] * keep_prev
                do_c = do_ref[rs, vs]
                vv = v_ref[rs, vs]
                ds_tot = ds_scr[ks, :] + _dot(q_ref[rs, ks] * scale, do_c, _TN)
                dq_ref[rs, ks] = (_dot(do_c, s_c, _NT) * scale).astype(BF16)
                dkd = _dot(vv, ds_tot, _NT)
                dv_ref[rs, vs] = _dot(kd, ds_tot, _NN).astype(BF16)
                dexp_col = jnp.sum(ds_tot * s_p, axis=1, keepdims=True)
                ds_scr[ks, :] = _row_to_col(exp_e) * ds_tot
                dk_ref[rs, ks] = (dkd * w).astype(BF16)
                dwt = dkd * kd
                de = jnp.sum(dwt, axis=0, keepdims=True) + _col_to_row(dexp_col) * exp_e
                dcum = jnp.where(last_row, de - dwt, -dwt)
                da = lax.dot_general(triu, dcum, (_NN, ((), ())), precision=lax.Precision.HIGHEST,
                                     preferred_element_type=F32)
                dz_scr[rs, ks] = da * (1.0 / GLA_TAU) * _sigmoid(-z_all[rs, ks])
        dz = dz_scr[...]
        dlr_ref[...] = _dot(dz, wa_ref[...], _NT).astype(BF16)
        dwa_ref[...] += _dot(lr_ref[...], dz, _TN)
        dba_ref[...] += jnp.sum(dz, axis=0, keepdims=True)

    qk_sh = jax.ShapeDtypeStruct((t, GLA_QK), BF16)
    return pl.pallas_call(
        body, name=name,
        out_shape=(qk_sh, qk_sh, jax.ShapeDtypeStruct((t, GLA_V), BF16), jax.ShapeDtypeStruct((t, LR_PAD), BF16),
                   jax.ShapeDtypeStruct((LR_PAD, GLA_QK), F32), jax.ShapeDtypeStruct((1, GLA_QK), F32)),
        grid=(nb,),
        in_specs=[pl.BlockSpec((rows, GLA_V), lambda i: (rev(i), 0)),
                  pl.BlockSpec((rows, GLA_QK), lambda i: (rev(i), OFF_Q // GLA_QK)),
                  pl.BlockSpec((rows, GLA_QK), lambda i: (rev(i), OFF_K // GLA_QK)),
                  pl.BlockSpec((rows, GLA_V), lambda i: (rev(i), OFF_V // GLA_V)),
                  pl.BlockSpec((rows, LR_PAD), lambda i: (rev(i), OFF_LR // LR_PAD)),
                  pl.BlockSpec((cb, GLA_QK, GLA_DV), lambda i: (rev(i), 0, 0)),
                  pl.BlockSpec((1, GLA_QK, GLA_DV), lambda i: (jnp.maximum(rev(i) * cb - 1, 0), 0, 0)),
                  pl.BlockSpec((LR_PAD, GLA_QK), lambda i: (0, 0)),
                  pl.BlockSpec((1, GLA_QK), lambda i: (0, 0))],
        out_specs=(pl.BlockSpec((rows, GLA_QK), lambda i: (rev(i), 0)),
                   pl.BlockSpec((rows, GLA_QK), lambda i: (rev(i), 0)),
                   pl.BlockSpec((rows, GLA_V), lambda i: (rev(i), 0)),
                   pl.BlockSpec((rows, LR_PAD), lambda i: (rev(i), 0)),
                   pl.BlockSpec((LR_PAD, GLA_QK), lambda i: (0, 0)),
                   pl.BlockSpec((1, GLA_QK), lambda i: (0, 0))),
        scratch_shapes=[pltpu.VMEM((GLA_QK, GLA_DV), F32), pltpu.VMEM((rows, GLA_QK), F32)],
        compiler_params=_params(("arbitrary",)),
    )(do, p, p, p, p, st, st, wa, ba)


def _gla_out_fwd(o, p, gng, name):
    t = o.shape[0]
    tm = min(TM_EW, t)

    def body(o_ref, r_ref, g_ref, z_ref):
        gv = g_ref[...]
        for h in range(GLA_HEADS):
            vs = slice(h * GLA_DV, (h + 1) * GLA_DV)
            ov, rv = o_ref[:, vs], r_ref[:, vs]
            z_ref[:, vs] = ((ov * _rstd(ov) * gv) * (rv * _sigmoid(rv))).astype(BF16)

    return pl.pallas_call(
        body, name=name, out_shape=jax.ShapeDtypeStruct((t, GLA_V), BF16), grid=(t // tm,),
        in_specs=[pl.BlockSpec((tm, GLA_V), lambda i: (i, 0)),
                  pl.BlockSpec((tm, GLA_V), lambda i: (i, OFF_R // GLA_V)),
                  pl.BlockSpec((1, GLA_DV), lambda i: (0, 0))],
        out_specs=pl.BlockSpec((tm, GLA_V), lambda i: (i, 0)),
        compiler_params=_params(("parallel",)),
    )(o, p, gng)


def _gla_out_bwd(dz, o, p, gng, name):
    t = o.shape[0]
    tm = min(TM_EW, t)

    def body(dz_ref, o_ref, r_ref, g_ref, do_ref, dr_ref, dg_ref):
        @pl.when(pl.program_id(0) == 0)
        def _():
            dg_ref[...] = jnp.zeros_like(dg_ref)

        gv = g_ref[...]
        for h in range(GLA_HEADS):
            vs = slice(h * GLA_DV, (h + 1) * GLA_DV)
            ov, rv, dzv = o_ref[:, vs], r_ref[:, vs], dz_ref[:, vs]
            rs = _rstd(ov)
            oh = ov * rs
            sg = _sigmoid(rv)
            dr_ref[:, vs] = (dzv * (oh * gv) * (sg * (1.0 + rv * (1.0 - sg)))).astype(BF16)
            don = dzv * (rv * sg)
            dg_ref[...] += jnp.sum(don * oh, axis=0, keepdims=True)
            doh = don * gv
            do_ref[:, vs] = rs * (doh - oh * jnp.mean(doh * oh, axis=-1, keepdims=True))

    row = pl.BlockSpec((tm, GLA_V), lambda i: (i, 0))
    return pl.pallas_call(
        body, name=name,
        out_shape=(jax.ShapeDtypeStruct((t, GLA_V), F32), jax.ShapeDtypeStruct((t, GLA_V), BF16),
                   jax.ShapeDtypeStruct((1, GLA_DV), F32)),
        grid=(t // tm,),
        in_specs=[row, row, pl.BlockSpec((tm, GLA_V), lambda i: (i, OFF_R // GLA_V)),
                  pl.BlockSpec((1, GLA_DV), lambda i: (0, 0))],
        out_specs=(row, row, pl.BlockSpec((1, GLA_DV), lambda i: (0, 0))),
        compiler_params=_params(("arbitrary",)),
    )(dz, o, p, gng)


def _ada_fwd(c_all, w, b, layer, name):
    n = w.shape[2]
    tn = _pick(n, 512)

    def body(c_ref, w_ref, b_ref, o_ref):
        cv = c_ref[...]
        o_ref[...] = _dot(cv * _sigmoid(cv), w_ref[...], _NN) + b_ref[...]

    return pl.pallas_call(
        body, name=name, out_shape=jax.ShapeDtypeStruct((16, n), F32), grid=(n // tn,),
        in_specs=[pl.BlockSpec((16, D_MODEL), lambda j: (0, 0)),
                  pl.BlockSpec((None, D_MODEL, tn), lambda j: (layer, 0, j)),
                  pl.BlockSpec((1, tn), lambda j: (0, j))],
        out_specs=pl.BlockSpec((16, tn), lambda j: (0, j)),
        compiler_params=_params(("parallel",)),
    )(c_all, w, b)


def _ada_bwd(c_all, dmod, name):
    n = dmod.shape[1]
    tn = _pick(n, 512)

    def body(c_ref, d_ref, o_ref):
        cv = c_ref[...]
        o_ref[...] = _dot(cv * _sigmoid(cv), d_ref[...], _TN)

    return pl.pallas_call(
        body, name=name, out_shape=jax.ShapeDtypeStruct((D_MODEL, n), F32), grid=(n // tn,),
        in_specs=[pl.BlockSpec((16, D_MODEL), lambda j: (0, 0)), pl.BlockSpec((16, tn), lambda j: (0, j))],
        out_specs=pl.BlockSpec((D_MODEL, tn), lambda j: (0, j)),
        compiler_params=_params(("parallel",)),
    )(c_all, dmod)


def _rows_tile(nrows, ncols, target_bytes):
    want = max(16, target_bytes // (4 * ncols))
    if nrows <= want:
        return nrows
    t = (want // 16) * 16
    while t >= 16:
        if nrows % t == 0:
            return t
        t -= 16
    return nrows


def _sum_chips(r, name):
    _, nrows, ncols = r.shape
    tr = _rows_tile(nrows, ncols, 2 << 20)

    def body(r_ref, o_ref):
        acc = r_ref[0].astype(F32)
        for j in range(1, N_CHIPS):
            acc = acc + r_ref[j].astype(F32)
        o_ref[...] = acc

    return pl.pallas_call(
        body, name=name, out_shape=jax.ShapeDtypeStruct((nrows, ncols), F32), grid=(nrows // tr,),
        in_specs=[pl.BlockSpec((N_CHIPS, tr, ncols), lambda i: (0, i, 0))],
        out_specs=pl.BlockSpec((tr, ncols), lambda i: (i, 0)),
        compiler_params=_params(("parallel",)),
    )(r)


def _adamw(w, m, v, ga, gb, name):
    nrows, ncols = w.shape
    tr = _rows_tile(nrows, ncols, 1 << 20)
    two = gb is not None
    c1 = 1.0 - ADAM_B1 ** ADAM_STEP
    c2 = 1.0 - ADAM_B2 ** ADAM_STEP

    def body(*refs):
        if two:
            w_ref, m_ref, v_ref, ga_ref, gb_ref, g_ref, d_ref, nm_ref, nv_ref = refs
            g = ga_ref[...] + gb_ref[...]
        else:
            w_ref, m_ref, v_ref, ga_ref, g_ref, d_ref, nm_ref, nv_ref = refs
            g = ga_ref[...]
        g_ref[...] = g
        nm = ADAM_B1 * m_ref[...] + (1.0 - ADAM_B1) * g
        nv = ADAM_B2 * v_ref[...] + (1.0 - ADAM_B2) * (g * g)
        nm_ref[...] = nm
        nv_ref[...] = nv
        d_ref[...] = -ADAM_LR * ((nm / c1) / (jnp.sqrt(nv / c2) + ADAM_EPS) + ADAM_WD * w_ref[...])

    blk = pl.BlockSpec((tr, ncols), lambda i: (i, 0))
    sh = jax.ShapeDtypeStruct((nrows, ncols), F32)
    ins = [w, m, v, ga] + ([gb] if two else [])
    return pl.pallas_call(
        body, name=name, out_shape=(sh, sh, sh, sh), grid=(nrows // tr,),
        in_specs=[blk] * len(ins), out_specs=(blk, blk, blk, blk),
        compiler_params=_params(("parallel",)),
    )(*ins)


def _pad_rows(a, rows):
    return jnp.concatenate([a, jnp.zeros((rows - a.shape[0],) + a.shape[1:], a.dtype)], axis=0)


def _cols_from_chips(g, layer):
    a = g[:, layer]
    return jnp.transpose(a, (1, 0, 2)).reshape(a.shape[1], N_CHIPS * a.shape[2])


def _cols_to_chips(a):
    rows, n = a.shape
    return jnp.transpose(a.reshape(rows, N_CHIPS, n // N_CHIPS), (1, 0, 2))


def _pad_w_in(w):
    return jnp.concatenate([w[:, :3072], w[:, 3088:], w[:, 3072:3088],
                            jnp.zeros((w.shape[0], LR_PAD - GLA_LOWRANK), w.dtype)], axis=1)


def _unpad_w_in(w):
    return jnp.concatenate([w[:, :3072], w[:, OFF_LR:OFF_LR + GLA_LOWRANK], w[:, 3072:OFF_LR]], axis=1)


def kernel(x, c, w_ada, b_ada, norm_g, w_in, w_a2, b_a2, gla_norm_g, w_out_gla, conv_mix_w, w_out_conv, w_o, w_up, ffn_conv_w, w_down, loss_target, m_w_ada, m_b_ada, m_norm_g, m_w_in, m_w_a2, m_b_a2, m_gla_norm_g, m_w_out_gla, m_conv_mix_w, m_w_out_conv, m_w_o, m_w_up, m_ffn_conv_w, m_w_down, v_w_ada, v_b_ada, v_norm_g, v_w_in, v_w_a2, v_b_a2, v_gla_norm_g, v_w_out_gla, v_conv_mix_w, v_w_out_conv, v_w_o, v_w_up, v_ffn_conv_w, v_w_down):
    xi, yi, ci = lax.axis_index("x"), lax.axis_index("y"), lax.axis_index("c")
    chip = 2 * xi + yi
    dev = 2 * chip + ci
    xt = x[0]
    tgt = loss_target[0]
    t = xt.shape[0]

    big = [w_in, w_out_gla, w_out_conv, w_o, w_up, w_down]
    g_in, g_og, g_oc, g_o, g_up, g_dn = _chip_exchange([w.astype(BF16) for w in big], False, "gather_weights")

    c_all = _allgather8(jnp.broadcast_to(c, (8, D_MODEL)), "gather_c")[0][:, 0, :]
    c16 = _pad_rows(c_all, 16)
    sm_parts = [norm_g.reshape(-1), w_a2.reshape(-1), conv_mix_w.reshape(-1), ffn_conv_w.reshape(-1)]
    sm_sizes = [a.shape[0] for a in sm_parts]
    sm_flat = jnp.concatenate(sm_parts)
    sm_rows = -(-sm_flat.shape[0] // 128)
    sm_rows = -(-sm_rows // 8) * 8
    sm_flat = jnp.concatenate([sm_flat, jnp.zeros((sm_rows * 128 - sm_flat.shape[0],), F32)]).reshape(sm_rows, 128)
    sm_all = _allgather8(sm_flat, "gather_small")[0].reshape(N_DEV, -1)[0::2]
    offs = [0]
    for s in sm_sizes:
        offs.append(offs[-1] + s)

    def small_full(idx, shape):
        a = sm_all[:, offs[idx]:offs[idx + 1]].reshape((N_CHIPS,) + shape)
        a = jnp.moveaxis(a, 0, -2)
        return a.reshape(shape[:-1] + (N_CHIPS * shape[-1],))

    norm_g_f = small_full(0, (DEPTH, 4, 512))
    w_a2_f = small_full(1, (DEPTH, GLA_LOWRANK, 128))
    conv_w_f = small_full(2, (DEPTH, 3, 256))
    ffn_w_f = small_full(3, (DEPTH, 3, 1408))

    b_loc = lax.dynamic_slice(b_ada, (0, chip * 3072), (DEPTH, 3072))
    mod_loc = jnp.concatenate(
        [_ada_fwd(c16, w_ada, b_loc[l:l + 1], l, "ada_fwd")[:8] for l in range(DEPTH)], axis=0)
    mod_all = _allgather8(mod_loc, "gather_mod")[0][0::2]
    mods = []
    for l in range(DEPTH):
        row = lax.dynamic_slice(mod_all, (0, l * 8 + dev, 0), (N_CHIPS, 1, 3072)).reshape(1, 6 * D_MODEL)
        mods.append([row[:, k * D_MODEL:(k + 1) * D_MODEL] for k in range(6)])

    saved = []
    h = None
    xin = xt
    for l in range(DEPTH):
        sh1, sc1, g1, sh2, sc2, g2 = mods[l]
        gn = [norm_g_f[l, k][None] for k in range(4)]
        wi = _pad_w_in(_cols_from_chips(g_in, l))
        wog, woc = _cols_from_chips(g_og, l), _cols_from_chips(g_oc, l)
        wo = g_o[:, l].reshape(D_MODEL, D_MODEL)
        wup = _cols_from_chips(g_up, l)
        wdn = g_dn[:, l].reshape(D_FF, D_MODEL)
        wa = _pad_rows(w_a2_f[l], LR_PAD)
        ba = b_a2[l][None]
        gng = gla_norm_g[l][None]
        cw8 = _pad_rows(conv_w_f[l], 8)
        fw8 = _pad_rows(ffn_w_f[l], 8)
        if l == 0:
            h = _pre_norm(xin, gn[0], sc1, sh1, "pre_norm")
        p = _matmul(h, wi, "nn", F32, "mm_in", tn=1152)
        o, st = _gla_fwd(p, wa, ba, "gla_fwd")
        za = _gla_out_fwd(o, p, gng, "gla_out_fwd")
        zb = _conv_fwd(p, cw8, "conv_fwd")
        ya = _matmul(za, wog, "nn", F32, "mm_out_gla")
        yb = _matmul(zb, woc, "nn", F32, "mm_out_conv")
        mm = _merge_fwd(ya, yb, p, "merge_fwd")
        y = _matmul(mm, wo, "nn", F32, "mm_o")
        x1, h2 = _post_pre(xin, y, g1, gn[1], gn[2], sc2, sh2, "post_pre")
        u = _matmul(h2, wup, "nn", F32, "mm_up")
        f = _ffn_fwd(u, fw8, "ffn_fwd")
        y2 = _matmul(f, wdn, "nn", F32, "mm_down")
        saved.append(dict(xin=xin, h=h, p=p, o=o, st=st, za=za, zb=zb, ya=ya, yb=yb, mm=mm, y=y, x1=x1, h2=h2,
                          u=u, f=f, y2=y2, wi=wi, wog=wog, woc=woc, wo=wo, wup=wup, wdn=wdn, wa=wa, ba=ba,
                          gng=gng, cw8=cw8, fw8=fw8, gn=gn, mod=mods[l]))
        if l + 1 < DEPTH:
            nsh1, nsc1 = mods[l + 1][0], mods[l + 1][1]
            xin, h = _post_pre(x1, y2, g2, gn[3], norm_g_f[l + 1, 0][None], nsc1, nsh1, "post_pre")
        else:
            dx, loss_tile = _post_loss(x1, y2, g2, gn[3], tgt, "post_loss")
    loss = lax.psum(loss_tile[0, 0], ("x", "y", "c"))

    gr = {k: [None] * DEPTH for k in ("w_in", "w_og", "w_oc", "w_o", "w_up", "w_dn")}
    sm = {k: [None] * DEPTH for k in ("dmod", "norm_g", "w_a2", "b_a2", "gng", "conv_w", "ffn_w")}
    for l in reversed(range(DEPTH)):
        s = saved[l]
        sh1, sc1, g1, sh2, sc2, g2 = s["mod"]
        gn = s["gn"]
        dy2, dg2, dgn3 = _post_bwd(dx, s["y2"], g2, gn[3], "post_bwd")
        gr["w_dn"][l] = _matmul(s["f"], dy2, "tn", BF16, "mm_down_dw", tm=512, tn=1024, tk=2048)
        df = _matmul(dy2, s["wdn"], "nt", F32, "mm_down_dx")
        dgate, dup, dfw = _ffn_bwd(df, s["u"], s["fw8"], "ffn_bwd")
        du = jnp.concatenate([dgate, dup], axis=1)
        gr["w_up"][l] = _matmul(s["h2"], du, "tn", BF16, "mm_up_dw", tm=512, tn=1024, tk=2048)
        dh2 = _matmul(du, s["wup"], "nt", F32, "mm_up_dx", tn=1024, tk=2816)
        dx1, dsh2, dsc2, dgn2 = _pre_bwd(dh2, s["x1"], dx, gn[2], sc2, "pre_bwd")
        dy, dg1, dgn1 = _post_bwd(dx1, s["y"], g1, gn[1], "post_bwd")
        gr["w_o"][l] = _matmul(s["mm"], dy, "tn", BF16, "mm_o_dw")
        dm = _matmul(dy, s["wo"], "nt", F32, "mm_o_dx")
        dya, dyb, dga, dgb = _merge_bwd(dm, s["ya"], s["yb"], s["p"], "merge_bwd")
        gr["w_og"][l] = _matmul(s["za"], dya, "tn", BF16, "mm_out_gla_dw")
        dza = _matmul(dya, s["wog"], "nt", F32, "mm_out_gla_dx")
        do, dr, dgng = _gla_out_bwd(dza, s["o"], s["p"], s["gng"], "gla_out_bwd")
        gr["w_oc"][l] = _matmul(s["zb"], dyb, "tn", BF16, "mm_out_conv_dw")
        dzb = _matmul(dyb, s["woc"], "nt", F32, "mm_out_conv_dx")
        dcb, dcc, dcx, dcw = _conv_bwd(dzb, s["p"], s["cw8"], "conv_bwd")
        dq, dk, dv, dlr, dwa, dba = _gla_bwd(do, s["p"], s["st"], s["wa"], s["ba"], "gla_bwd")
        dp = jnp.concatenate([dq, dk, dv, dr, dcb, dcc, dcx, dga, dgb, dlr], axis=1)
        gr["w_in"][l] = _unpad_w_in(_matmul(s["h"], dp, "tn", BF16, "mm_in_dw", tm=512, tn=1152, tk=2048))
        dh = _matmul(dp, s["wi"], "nt", F32, "mm_in_dx", tn=1024, tk=1152)
        dx, dsh1, dsc1, dgn0 = _pre_bwd(dh, s["xin"], dx1, gn[0], sc1, "pre_bwd")
        sm["dmod"][l] = jnp.concatenate([dsh1, dsc1, dg1, dsh2, dsc2, dg2], axis=1)[0]
        sm["norm_g"][l] = jnp.concatenate([dgn0, dgn1, dgn2, dgn3], axis=0)
        sm["w_a2"][l] = dwa[:GLA_LOWRANK]
        sm["b_a2"][l] = dba[0]
        sm["gng"][l] = dgng[0]
        sm["conv_w"][l] = dcw[:3]
        sm["ffn_w"][l] = dfw[:3]
    grad_x = dx[None]

    names = ("dmod", "norm_g", "w_a2", "b_a2", "gng", "conv_w", "ffn_w")
    parts = [jnp.stack(sm[k]).reshape(-1) for k in names]
    shapes = [jnp.stack(sm[k]).shape for k in names]
    sizes = [a.shape[0] for a in parts]
    flat = jnp.concatenate(parts)
    rows = -(-flat.shape[0] // 1024) * 8
    flat = jnp.concatenate([flat, jnp.zeros((rows * 128 - flat.shape[0],), F32)]).reshape(rows, 128)
    gath, tot = _allgather8(flat, "reduce_small")
    po = [0]
    for s_ in sizes:
        po.append(po[-1] + s_)
    tot = tot.reshape(-1)
    tot_of = {k: tot[po[i]:po[i + 1]].reshape(shapes[i]) for i, k in enumerate(names)}
    dmod_all = gath.reshape(N_DEV, -1)[:, po[0]:po[1]].reshape(N_DEV, DEPTH, 6 * D_MODEL)

    def chip_cols(a, width):
        return lax.dynamic_slice_in_dim(a, chip * width, width, axis=a.ndim - 1)

    g_w_ada = []
    for l in range(DEPTH):
        dml = _pad_rows(chip_cols(dmod_all[:, l], 3072), 16)
        g_w_ada.append(_ada_bwd(c16, dml, "ada_bwd"))
    g_w_ada = jnp.stack(g_w_ada)

    send = [
        jnp.stack([_cols_to_chips(gr["w_in"][l]) for l in range(DEPTH)], axis=1),
        jnp.stack([_cols_to_chips(gr["w_og"][l]) for l in range(DEPTH)], axis=1),
        jnp.stack([_cols_to_chips(gr["w_oc"][l]) for l in range(DEPTH)], axis=1),
        jnp.stack([gr["w_o"][l].reshape(N_CHIPS, 512, D_MODEL) for l in range(DEPTH)], axis=1),
        jnp.stack([_cols_to_chips(gr["w_up"][l]) for l in range(DEPTH)], axis=1),
        jnp.stack([gr["w_dn"][l].reshape(N_CHIPS, 1408, D_MODEL) for l in range(DEPTH)], axis=1),
    ]
    recv = _chip_exchange(send, True, "scatter_grads")
    plane = [_sum_chips(r.reshape(N_CHIPS, DEPTH * r.shape[2], r.shape[3]), "sum_chips") for r in recv]
    other = _sibling_exchange(plane, "sibling_grads")

    def upd(w, m, v, ga, gb, name):
        sh = w.shape
        two_d = (-1, sh[-1])
        outs = _adamw(w.reshape(two_d), m.reshape(two_d), v.reshape(two_d), ga.reshape(two_d),
                      None if gb is None else gb.reshape(two_d), name)
        return [a.reshape(sh) for a in outs]

    res = {}
    res["w_ada"] = upd(w_ada, m_w_ada, v_w_ada, g_w_ada, None, "adamw")
    res["b_ada"] = upd(b_ada, m_b_ada, v_b_ada, tot_of["dmod"], None, "adamw")
    res["norm_g"] = upd(norm_g, m_norm_g, v_norm_g, chip_cols(tot_of["norm_g"], 512), None, "adamw")
    res["w_in"] = upd(w_in, m_w_in, v_w_in, plane[0], other[0], "adamw")
    res["w_a2"] = upd(w_a2, m_w_a2, v_w_a2, chip_cols(tot_of["w_a2"], 128), None, "adamw")
    res["b_a2"] = upd(b_a2, m_b_a2, v_b_a2, tot_of["b_a2"], None, "adamw")
    res["gla_norm_g"] = upd(gla_norm_g, m_gla_norm_g, v_gla_norm_g, tot_of["gng"], None, "adamw")
    res["w_out_gla"] = upd(w_out_gla, m_w_out_gla, v_w_out_gla, plane[1], other[1], "adamw")
    res["conv_mix_w"] = upd(conv_mix_w, m_conv_mix_w, v_conv_mix_w, chip_cols(tot_of["conv_w"], 256), None, "adamw")
    res["w_out_conv"] = upd(w_out_conv, m_w_out_conv, v_w_out_conv, plane[2], other[2], "adamw")
    res["w_o"] = upd(w_o, m_w_o, v_w_o, plane[3], other[3], "adamw")
    res["w_up"] = upd(w_up, m_w_up, v_w_up, plane[4], other[4], "adamw")
    res["ffn_conv_w"] = upd(ffn_conv_w, m_ffn_conv_w, v_ffn_conv_w, chip_cols(tot_of["ffn_w"], 1408), None, "adamw")
    res["w_down"] = upd(w_down, m_w_down, v_w_down, plane[5], other[5], "adamw")
    order = ("w_ada", "b_ada", "norm_g", "w_in", "w_a2", "b_a2", "gla_norm_g", "w_out_gla", "conv_mix_w",
             "w_out_conv", "w_o", "w_up", "ffn_conv_w", "w_down")
    return (loss, grad_x, *[res[k][0] for k in order], *[res[k][1] for k in order],
            *[res[k][2] for k in order], *[res[k][3] for k in order])
```

```python
import functools
import math

import jax
import jax.numpy as jnp
from jax import lax
from jax.experimental import pallas as pl
from jax.experimental.pallas import tpu as pltpu

F32 = jnp.float32
BF16 = jnp.bfloat16
MESH = pl.DeviceIdType.MESH

D_MODEL = 2048
DEPTH = 2
CHUNK = 64
GLA_HEADS = 4
GLA_DK = 128
GLA_DV = 256
GLA_QK = GLA_HEADS * GLA_DK
GLA_V = GLA_HEADS * GLA_DV
GLA_LOWRANK = 16
GLA_TAU = 16.0
CONV_WIDTH = 1024
D_FF = 5632
EPS = 1e-6
N_IN = 10256
LR_PAD = 128
N_IN_PAD = N_IN - GLA_LOWRANK + LR_PAD
OFF_Q, OFF_K, OFF_V, OFF_R = 0, 512, 1024, 2048
OFF_CB, OFF_CC, OFF_CX, OFF_GA, OFF_GB, OFF_LR = 3072, 4096, 5120, 6144, 8192, 10240

ADAM_LR = 0.001
ADAM_B1 = 0.9
ADAM_B2 = 0.999
ADAM_EPS = 1e-08
ADAM_WD = 0.01
ADAM_STEP = 10

N_CHIPS = 4
N_DEV = 8
VMEM_LIMIT = 56 * 1024 * 1024
TM_ROW = 256
TM_EW = 512
CW_EW = 512
GLA_ROWS = 256


def _params(sem=None):
    return pltpu.CompilerParams(dimension_semantics=sem, vmem_limit_bytes=VMEM_LIMIT)


def _sigmoid(v):
    return 1.0 / (1.0 + jnp.exp(-v))


def _log_sigmoid(v):
    return jnp.minimum(v, 0.0) - jnp.log(1.0 + jnp.exp(-jnp.abs(v)))


_GELU_C = math.sqrt(2.0 / math.pi)


def _gelu(v):
    return 0.5 * v * (1.0 + jnp.tanh(_GELU_C * (v + 0.044715 * v * v * v)))


def _gelu_grad(v):
    t = jnp.tanh(_GELU_C * (v + 0.044715 * v * v * v))
    return 0.5 * (1.0 + t) + 0.5 * v * (1.0 - t * t) * _GELU_C * (1.0 + 3.0 * 0.044715 * v * v)


def _flip(a, d):
    return a + d - 2 * a * d


def _unless(cond):
    return jnp.where(cond, 0.0, 1.0).astype(F32)


def _allgather8(xv, name):
    r, cdim = xv.shape

    def body(x_ref, out_ref, sum_ref, send_sems, recv_sems):
        xi, yi, ci = lax.axis_index("x"), lax.axis_index("y"), lax.axis_index("c")
        me = 4 * xi + 2 * yi + ci
        out_ref[pl.ds(me, 1)] = x_ref[...][None]
        started = []
        for k in range(1, N_DEV):
            px, py, pc = _flip(xi, (k >> 2) & 1), _flip(yi, (k >> 1) & 1), _flip(ci, k & 1)
            cp = pltpu.make_async_remote_copy(
                src_ref=x_ref, dst_ref=out_ref.at[me], send_sem=send_sems.at[k - 1], recv_sem=recv_sems.at[k - 1],
                device_id=(px, py, pc), device_id_type=MESH)
            cp.start()
            started.append((cp, 4 * px + 2 * py + pc, k, (px, py, pc)))
        for cp, peer, k, pid in started:
            cp.wait_send()
            pltpu.make_async_remote_copy(
                src_ref=x_ref, dst_ref=out_ref.at[peer], send_sem=send_sems.at[k - 1], recv_sem=recv_sems.at[k - 1],
                device_id=pid, device_id_type=MESH).wait_recv()
        acc = out_ref[0]
        for d in range(1, N_DEV):
            acc = acc + out_ref[d]
        sum_ref[...] = acc

    return pl.pallas_call(
        body, name=name,
        out_shape=(jax.ShapeDtypeStruct((N_DEV, r, cdim), F32), jax.ShapeDtypeStruct((r, cdim), F32)),
        in_specs=[pl.BlockSpec(memory_space=pltpu.VMEM)],
        out_specs=(pl.BlockSpec(memory_space=pltpu.VMEM), pl.BlockSpec(memory_space=pltpu.VMEM)),
        scratch_shapes=[pltpu.SemaphoreType.DMA((N_DEV - 1,)), pltpu.SemaphoreType.DMA((N_DEV - 1,))],
        compiler_params=pltpu.CompilerParams(vmem_limit_bytes=VMEM_LIMIT),
    )(xv)


_HBM = pl.BlockSpec(memory_space=pltpu.HBM)
_SEM = pl.BlockSpec(memory_space=pltpu.SEMAPHORE)
_EFFECT = pltpu.SideEffectType.DATAFLOW_SIDE_EFFECTING
_CHIP_FLIPS = ((1, 0), (0, 1), (1, 1))


def _chip_copies(src_ref, land_ref, send_sems, recv_sems, scatter):
    xi, yi, ci = lax.axis_index("x"), lax.axis_index("y"), lax.axis_index("c")
    me = 2 * xi + yi
    out = []
    for k, (dx, dy) in enumerate(_CHIP_FLIPS):
        px, py = _flip(xi, dx), _flip(yi, dy)
        peer = 2 * px + py
        src = src_ref.at[peer] if scatter else src_ref
        mk = functools.partial(pltpu.make_async_remote_copy, src_ref=src, send_sem=send_sems.at[k],
                               recv_sem=recv_sems.at[k], device_id=(px, py, ci), device_id_type=MESH)
        out.append((mk(dst_ref=land_ref.at[me]), mk(dst_ref=land_ref.at[peer])))
    return out


def _xchg_start(src, scatter, name):
    land_shape = src.shape if scatter else (N_CHIPS,) + src.shape

    def body(src_ref, land_ref, send_sems, recv_sems, src_thru, land_thru, token):
        for mine, _ in _chip_copies(src_ref, land_ref, send_sems, recv_sems, scatter):
            mine.start()
        token[...] = jnp.zeros_like(token)

    return pl.pallas_call(
        body, name=name,
        out_shape=(pltpu.SemaphoreType.DMA((3,)), pltpu.SemaphoreType.DMA((3,)), pltpu.HBM(src.shape, src.dtype),
                   pltpu.HBM(land_shape, src.dtype), jax.ShapeDtypeStruct((8, 128), F32)),
        in_specs=(_HBM, _HBM), out_specs=(_SEM, _SEM, _HBM, _HBM, pl.BlockSpec(memory_space=pltpu.VMEM)),
        input_output_aliases={0: 2, 1: 3},
        compiler_params=pltpu.CompilerParams(has_side_effects=_EFFECT),
    )(pltpu.with_memory_space_constraint(src, pltpu.HBM),
      pltpu.with_memory_space_constraint(lax.empty(land_shape, src.dtype), pltpu.HBM))


def _xchg_wait(handle, after, scatter, name):
    send, recv, src_thru, land_thru = handle

    def body(src_ref, land_ref, send_sems, recv_sems, after_ref, src_out, land_out):
        for mine, theirs in _chip_copies(src_ref, land_ref, send_sems, recv_sems, scatter):
            mine.wait_send()
            theirs.wait_recv()

    return pl.pallas_call(
        body, name=name,
        out_shape=(pltpu.HBM(src_thru.shape, src_thru.dtype), pltpu.HBM(land_thru.shape, land_thru.dtype)),
        in_specs=(_HBM, _HBM, _SEM, _SEM, pl.BlockSpec(memory_space=pl.ANY)), out_specs=(_HBM, _HBM),
        input_output_aliases={0: 0, 1: 1},
        compiler_params=pltpu.CompilerParams(has_side_effects=_EFFECT),
    )(src_thru, land_thru, send, recv, after)


def _sibling_exchange(arrays, name):
    n = len(arrays)

    def body(*refs):
        ins, outs = refs[:n], refs[n:2 * n]
        send_sems, recv_sems = refs[2 * n:]
        xi, yi, ci = lax.axis_index("x"), lax.axis_index("y"), lax.axis_index("c")
        cps = []
        for i in range(n):
            cp = pltpu.make_async_remote_copy(
                src_ref=ins[i], dst_ref=outs[i], send_sem=send_sems.at[i], recv_sem=recv_sems.at[i],
                device_id=(xi, yi, 1 - ci), device_id_type=MESH)
            cp.start()
            cps.append(cp)
        for cp in cps:
            cp.wait()

    return pl.pallas_call(
        body, name=name, out_shape=tuple(jax.ShapeDtypeStruct(a.shape, a.dtype) for a in arrays),
        in_specs=[pl.BlockSpec(memory_space=pl.ANY)] * n,
        out_specs=tuple(pl.BlockSpec(memory_space=pl.ANY) for _ in range(n)),
        scratch_shapes=[pltpu.SemaphoreType.DMA((n,)), pltpu.SemaphoreType.DMA((n,))],
    )(*arrays)


def _pick(dim, pref):
    if dim <= pref:
        return dim
    t = (pref // 128) * 128
    while t >= 128:
        if dim % t == 0:
            return t
        t -= 128
    return dim


def _matmul(a, b, dims, out_dtype, name, tm=512, tn=1024, tk=2048):
    if dims == "nn":
        (m, kd), (_, n) = a.shape, b.shape
    elif dims == "nt":
        (m, kd), (n, _) = a.shape, b.shape
    else:
        (kd, m), (_, n) = a.shape, b.shape
    tm, tn, tk = _pick(m, tm), _pick(n, tn), _pick(kd, tk)
    nk = kd // tk
    if dims == "nn":
        a_spec = pl.BlockSpec((tm, tk), lambda j, i, k: (i, k))
        b_spec = pl.BlockSpec((tk, tn), lambda j, i, k: (k, j))
        dn = (((1,), (0,)), ((), ()))
    elif dims == "nt":
        a_spec = pl.BlockSpec((tm, tk), lambda j, i, k: (i, k))
        b_spec = pl.BlockSpec((tn, tk), lambda j, i, k: (j, k))
        dn = (((1,), (1,)), ((), ()))
    else:
        a_spec = pl.BlockSpec((tk, tm), lambda j, i, k: (k, i))
        b_spec = pl.BlockSpec((tk, tn), lambda j, i, k: (k, j))
        dn = (((0,), (0,)), ((), ()))

    def body(a_ref, b_ref, o_ref, acc_ref):
        part = lax.dot_general(a_ref[...].astype(BF16), b_ref[...].astype(BF16), dn, preferred_element_type=F32)
        if nk == 1:
            o_ref[...] = part.astype(o_ref.dtype)
        else:
            k = pl.program_id(2)

            @pl.when(k == 0)
            def _():
                acc_ref[...] = part

            @pl.when(k > 0)
            def _():
                acc_ref[...] += part

            @pl.when(k == nk - 1)
            def _():
                o_ref[...] = acc_ref[...].astype(o_ref.dtype)

    return pl.pallas_call(
        body, name=name, out_shape=jax.ShapeDtypeStruct((m, n), out_dtype),
        grid=(n // tn, m // tm, nk),
        in_specs=[a_spec, b_spec],
        out_specs=pl.BlockSpec((tm, tn), lambda j, i, k: (i, j)),
        scratch_shapes=[pltpu.VMEM((tm, tn), F32)],
        compiler_params=_params(("parallel", "parallel", "arbitrary")),
    )(a, b)


def _rstd(v):
    return lax.rsqrt(jnp.mean(v * v, axis=-1, keepdims=True) + EPS)


def _row(tm):
    return pl.BlockSpec((tm, D_MODEL), lambda i: (i, 0))


_VEC = pl.BlockSpec((1, D_MODEL), lambda i: (0, 0))


def _pre_norm(x, gn, sc, sh, name):
    t = x.shape[0]
    tm = min(TM_ROW, t)

    def body(x_ref, gn_ref, sc_ref, sh_ref, h_ref):
        xv = x_ref[...]
        h_ref[...] = ((xv * _rstd(xv) * gn_ref[...]) * (1.0 + sc_ref[...]) + sh_ref[...]).astype(BF16)

    return pl.pallas_call(
        body, name=name, out_shape=jax.ShapeDtypeStruct((t, D_MODEL), BF16), grid=(t // tm,),
        in_specs=[_row(tm), _VEC, _VEC, _VEC], out_specs=_row(tm),
        compiler_params=_params(("parallel",)),
    )(x, gn, sc, sh)


def _post_pre(x, y, g, gnp, gn, sc, sh, name):
    t = x.shape[0]
    tm = min(TM_ROW, t)

    def body(x_ref, y_ref, g_ref, gnp_ref, gn_ref, sc_ref, sh_ref, x1_ref, h_ref):
        yv = y_ref[...]
        x1 = x_ref[...] + g_ref[...] * (yv * _rstd(yv) * gnp_ref[...])
        x1_ref[...] = x1
        h_ref[...] = ((x1 * _rstd(x1) * gn_ref[...]) * (1.0 + sc_ref[...]) + sh_ref[...]).astype(BF16)

    return pl.pallas_call(
        body, name=name,
        out_shape=(jax.ShapeDtypeStruct((t, D_MODEL), F32), jax.ShapeDtypeStruct((t, D_MODEL), BF16)),
        grid=(t // tm,),
        in_specs=[_row(tm), _row(tm), _VEC, _VEC, _VEC, _VEC, _VEC], out_specs=(_row(tm), _row(tm)),
        compiler_params=_params(("parallel",)),
    )(x, y, g, gnp, gn, sc, sh)


def _post_loss(x, y, g, gnp, tgt, name):
    t = x.shape[0]
    tm = min(TM_ROW, t)

    def body(x_ref, y_ref, g_ref, gnp_ref, t_ref, dx_ref, loss_ref):
        yv = y_ref[...]
        diff = x_ref[...] + g_ref[...] * (yv * _rstd(yv) * gnp_ref[...]) - t_ref[...]
        dx_ref[...] = diff * (1.0 / D_MODEL)
        part = (0.5 / D_MODEL) * jnp.sum(jnp.sum(diff * diff, axis=-1, keepdims=True), axis=0, keepdims=True)

        @pl.when(pl.program_id(0) == 0)
        def _():
            loss_ref[...] = jnp.zeros_like(loss_ref)

        loss_ref[...] += jnp.broadcast_to(part, loss_ref.shape)

    return pl.pallas_call(
        body, name=name,
        out_shape=(jax.ShapeDtypeStruct((t, D_MODEL), F32), jax.ShapeDtypeStruct((8, 128), F32)),
        grid=(t // tm,),
        in_specs=[_row(tm), _row(tm), _VEC, _VEC, _row(tm)],
        out_specs=(_row(tm), pl.BlockSpec((8, 128), lambda i: (0, 0))),
        compiler_params=_params(("arbitrary",)),
    )(x, y, g, gnp, tgt)


def _acc_rows(ref, val):
    @pl.when(pl.program_id(0) == 0)
    def _():
        ref[...] = jnp.zeros_like(ref)

    ref[...] += jnp.sum(val, axis=0, keepdims=True)


def _post_bwd(dxn, y, g, gnp, name):
    t = y.shape[0]
    tm = min(TM_ROW, t)

    def body(dx_ref, y_ref, g_ref, gnp_ref, dy_ref, dg_ref, dgn_ref):
        yv, dxv = y_ref[...], dx_ref[...]
        r = _rstd(yv)
        yh = yv * r
        _acc_rows(dg_ref, dxv * (yh * gnp_ref[...]))
        dn = dxv * g_ref[...]
        _acc_rows(dgn_ref, dn * yh)
        dyh = dn * gnp_ref[...]
        dy_ref[...] = (r * (dyh - yh * jnp.mean(dyh * yh, axis=-1, keepdims=True))).astype(BF16)

    return pl.pallas_call(
        body, name=name,
        out_shape=(jax.ShapeDtypeStruct((t, D_MODEL), BF16), jax.ShapeDtypeStruct((1, D_MODEL), F32),
                   jax.ShapeDtypeStruct((1, D_MODEL), F32)),
        grid=(t // tm,),
        in_specs=[_row(tm), _row(tm), _VEC, _VEC], out_specs=(_row(tm), _VEC, _VEC),
        compiler_params=_params(("arbitrary",)),
    )(dxn, y, g, gnp)


def _pre_bwd(dh, xin, dres, gn, sc, name):
    t = xin.shape[0]
    tm = min(TM_ROW, t)

    def body(dh_ref, x_ref, dres_ref, gn_ref, sc_ref, dx_ref, dsh_ref, dsc_ref, dgn_ref):
        xv, dhv = x_ref[...], dh_ref[...]
        r = _rstd(xv)
        xh = xv * r
        _acc_rows(dsh_ref, dhv)
        _acc_rows(dsc_ref, dhv * (xh * gn_ref[...]))
        dn = dhv * (1.0 + sc_ref[...])
        _acc_rows(dgn_ref, dn * xh)
        dxh = dn * gn_ref[...]
        dx_ref[...] = dres_ref[...] + r * (dxh - xh * jnp.mean(dxh * xh, axis=-1, keepdims=True))

    vec = jax.ShapeDtypeStruct((1, D_MODEL), F32)
    return pl.pallas_call(
        body, name=name, out_shape=(jax.ShapeDtypeStruct((t, D_MODEL), F32), vec, vec, vec),
        grid=(t // tm,),
        in_specs=[_row(tm), _row(tm), _row(tm), _VEC, _VEC], out_specs=(_row(tm), _VEC, _VEC, _VEC),
        compiler_params=_params(("arbitrary",)),
    )(dh, xin, dres, gn, sc)


def _shift_down(v, halo, s):
    tm = v.shape[0]
    out = pltpu.roll(v, s, 0)
    row = lax.broadcasted_iota(jnp.int32, v.shape, 0)
    for j in range(s):
        out = jnp.where(row == j, jnp.broadcast_to(halo[8 - s + j:8 - s + j + 1, :], v.shape), out)
    return out


def _shift_up(v, halo, s):
    tm = v.shape[0]
    out = pltpu.roll(v, tm - s, 0)
    row = lax.broadcasted_iota(jnp.int32, v.shape, 0)
    for j in range(s):
        out = jnp.where(row == tm - s + j, jnp.broadcast_to(halo[j:j + 1, :], v.shape), out)
    return out


def _tile_specs(tm, cw, off, nrow):
    ob = off // cw
    r8 = tm // 8
    main = pl.BlockSpec((tm, cw), lambda j, i: (i, ob + j))
    prev = pl.BlockSpec((8, cw), lambda j, i: (jnp.maximum(i * r8 - 1, 0), ob + j))
    nxt = pl.BlockSpec((8, cw), lambda j, i: (jnp.minimum((i + 1) * r8, nrow * r8 - 1), ob + j))
    return main, prev, nxt


def _conv_fwd(p, w, name):
    t = p.shape[0]
    tm, cw = min(TM_EW, t), CW_EW
    nrow = t // tm
    cb_s, _, _ = _tile_specs(tm, cw, OFF_CB, nrow)
    cc_s, cc_p, _ = _tile_specs(tm, cw, OFF_CC, nrow)
    cx_s, cx_p, _ = _tile_specs(tm, cw, OFF_CX, nrow)

    def body(cb_ref, cc_ref, ccp_ref, cx_ref, cxp_ref, w_ref, z_ref):
        u = cc_ref[...] * cx_ref[...]
        uh = ccp_ref[...] * cxp_ref[...] * _unless(pl.program_id(1) == 0)
        wv = w_ref[...]
        conv = wv[2:3, :] * u + wv[1:2, :] * _shift_down(u, uh, 1) + wv[0:1, :] * _shift_down(u, uh, 2)
        z_ref[...] = (cb_ref[...] * conv).astype(BF16)

    return pl.pallas_call(
        body, name=name, out_shape=jax.ShapeDtypeStruct((t, CONV_WIDTH), BF16),
        grid=(CONV_WIDTH // cw, nrow),
        in_specs=[cb_s, cc_s, cc_p, cx_s, cx_p, pl.BlockSpec((8, cw), lambda j, i: (0, j))],
        out_specs=pl.BlockSpec((tm, cw), lambda j, i: (i, j)),
        compiler_params=_params(("parallel", "arbitrary")),
    )(p, p, p, p, p, w)


def _acc_w(ref, vals):
    @pl.when(pl.program_id(1) == 0)
    def _():
        ref[...] = jnp.zeros_like(ref)

    for j, v in enumerate(vals):
        ref[j:j + 1, :] += jnp.sum(v, axis=0, keepdims=True)


def _conv_bwd(dz, p, w, name):
    t = p.shape[0]
    tm, cw = min(TM_EW, t), CW_EW
    nrow = t // tm
    dz_s, _, dz_n = _tile_specs(tm, cw, 0, nrow)
    cb_s, _, cb_n = _tile_specs(tm, cw, OFF_CB, nrow)
    cc_s, cc_p, _ = _tile_specs(tm, cw, OFF_CC, nrow)
    cx_s, cx_p, _ = _tile_specs(tm, cw, OFF_CX, nrow)

    def body(dz_ref, dzn_ref, cb_ref, cbn_ref, cc_ref, ccp_ref, cx_ref, cxp_ref, w_ref,
             dcb_ref, dcc_ref, dcx_ref, dw_ref):
        i = pl.program_id(1)
        ccv, cxv, dzv = cc_ref[...], cx_ref[...], dz_ref[...]
        u = ccv * cxv
        uh = ccp_ref[...] * cxp_ref[...] * _unless(i == 0)
        wv = w_ref[...]
        u1, u2 = _shift_down(u, uh, 1), _shift_down(u, uh, 2)
        conv = wv[2:3, :] * u + wv[1:2, :] * u1 + wv[0:1, :] * u2
        dcb_ref[...] = (dzv * conv).astype(BF16)
        dconv = dzv * cb_ref[...]
        dch = dzn_ref[...] * cbn_ref[...] * _unless(i == nrow - 1)
        du = wv[2:3, :] * dconv + wv[1:2, :] * _shift_up(dconv, dch, 1) + wv[0:1, :] * _shift_up(dconv, dch, 2)
        dcc_ref[...] = (du * cxv).astype(BF16)
        dcx_ref[...] = (du * ccv).astype(BF16)
        _acc_w(dw_ref, (dconv * u2, dconv * u1, dconv * u))

    o_s = pl.BlockSpec((tm, cw), lambda j, i: (i, j))
    o_sh = jax.ShapeDtypeStruct((t, CONV_WIDTH), BF16)
    w_s = pl.BlockSpec((8, cw), lambda j, i: (0, j))
    return pl.pallas_call(
        body, name=name, out_shape=(o_sh, o_sh, o_sh, jax.ShapeDtypeStruct((8, CONV_WIDTH), F32)),
        grid=(CONV_WIDTH // cw, nrow),
        in_specs=[dz_s, dz_n, cb_s, cb_n, cc_s, cc_p, cx_s, cx_p, w_s],
        out_specs=(o_s, o_s, o_s, w_s),
        compiler_params=_params(("parallel", "arbitrary")),
    )(dz, dz, p, p, p, p, p, p, w)


def _ffn_fwd(u, w, name):
    t = u.shape[0]
    tm, cw = min(TM_EW, t), CW_EW
    nrow = t // tm
    g_s, g_p, _ = _tile_specs(tm, cw, 0, nrow)
    u_s, _, _ = _tile_specs(tm, cw, D_FF, nrow)

    def body(g_ref, gp_ref, u_ref, w_ref, f_ref):
        gv = g_ref[...]
        gh = gp_ref[...] * _unless(pl.program_id(1) == 0)
        wv = w_ref[...]
        gc = wv[2:3, :] * gv + wv[1:2, :] * _shift_down(gv, gh, 1) + wv[0:1, :] * _shift_down(gv, gh, 2)
        f_ref[...] = (_gelu(gc) * u_ref[...]).astype(BF16)

    return pl.pallas_call(
        body, name=name, out_shape=jax.ShapeDtypeStruct((t, D_FF), BF16),
        grid=(D_FF // cw, nrow),
        in_specs=[g_s, g_p, u_s, pl.BlockSpec((8, cw), lambda j, i: (0, j))],
        out_specs=pl.BlockSpec((tm, cw), lambda j, i: (i, j)),
        compiler_params=_params(("parallel", "arbitrary")),
    )(u, u, u, w)


def _ffn_bwd(df, u, w, name):
    t = u.shape[0]
    tm, cw = min(TM_EW, t), CW_EW
    nrow = t // tm
    df_s, _, df_n = _tile_specs(tm, cw, 0, nrow)
    g_s, g_p, g_n = _tile_specs(tm, cw, 0, nrow)
    u_s, _, u_n = _tile_specs(tm, cw, D_FF, nrow)
    r8 = tm // 8

    def body(df_ref, dfn_ref, g_ref, gp_ref, gn_ref, u_ref, un_ref, w_ref, dg_ref, du_ref, dw_ref):
        i = pl.program_id(1)
        gv, dfv, uv = g_ref[...], df_ref[...], u_ref[...]
        gh = gp_ref[...] * _unless(i == 0)
        wv = w_ref[...]
        g1, g2 = _shift_down(gv, gh, 1), _shift_down(gv, gh, 2)
        gc = wv[2:3, :] * gv + wv[1:2, :] * g1 + wv[0:1, :] * g2
        du_ref[...] = (dfv * _gelu(gc)).astype(BF16)
        dgc = dfv * uv * _gelu_grad(gc)
        gnv = gn_ref[...]
        gtail = gv[tm - 8:tm, :]
        gcn = (wv[2:3, :] * gnv + wv[1:2, :] * _shift_down(gnv, gtail, 1) + wv[0:1, :] * _shift_down(gnv, gtail, 2))
        dgcn = dfn_ref[...] * un_ref[...] * _gelu_grad(gcn) * _unless(i == nrow - 1)
        dg = wv[2:3, :] * dgc + wv[1:2, :] * _shift_up(dgc, dgcn, 1) + wv[0:1, :] * _shift_up(dgc, dgcn, 2)
        dg_ref[...] = dg.astype(BF16)
        _acc_w(dw_ref, (dgc * g2, dgc * g1, dgc * gv))

    o_s = pl.BlockSpec((tm, cw), lambda j, i: (i, j))
    o_sh = jax.ShapeDtypeStruct((t, D_FF), BF16)
    w_s = pl.BlockSpec((8, cw), lambda j, i: (0, j))
    return pl.pallas_call(
        body, name=name, out_shape=(o_sh, o_sh, jax.ShapeDtypeStruct((8, D_FF), F32)),
        grid=(D_FF // cw, nrow),
        in_specs=[df_s, df_n, g_s, g_p, g_n, u_s, u_n, w_s],
        out_specs=(o_s, o_s, w_s),
        compiler_params=_params(("parallel", "arbitrary")),
    )(df, df, u, u, u, u, u, w)


def _merge_fwd(ya, yb, p, name):
    t = ya.shape[0]
    tm, cw = min(TM_EW, t), CW_EW
    y_s = pl.BlockSpec((tm, cw), lambda i, j: (i, j))

    def body(ya_ref, yb_ref, ga_ref, gb_ref, m_ref):
        m_ref[...] = (_sigmoid(ga_ref[...]) * ya_ref[...] + _sigmoid(gb_ref[...]) * yb_ref[...]).astype(BF16)

    return pl.pallas_call(
        body, name=name, out_shape=jax.ShapeDtypeStruct((t, D_MODEL), BF16),
        grid=(t // tm, D_MODEL // cw),
        in_specs=[y_s, y_s, pl.BlockSpec((tm, cw), lambda i, j: (i, OFF_GA // cw + j)),
                  pl.BlockSpec((tm, cw), lambda i, j: (i, OFF_GB // cw + j))],
        out_specs=y_s, compiler_params=_params(("parallel", "parallel")),
    )(ya, yb, p, p)


def _merge_bwd(dm, ya, yb, p, name):
    t = ya.shape[0]
    tm, cw = min(TM_EW, t), CW_EW
    y_s = pl.BlockSpec((tm, cw), lambda i, j: (i, j))

    def body(dm_ref, ya_ref, yb_ref, ga_ref, gb_ref, dya_ref, dyb_ref, dga_ref, dgb_ref):
        dmv = dm_ref[...]
        sa, sb = _sigmoid(ga_ref[...]), _sigmoid(gb_ref[...])
        dya_ref[...] = (dmv * sa).astype(BF16)
        dyb_ref[...] = (dmv * sb).astype(BF16)
        dga_ref[...] = (dmv * ya_ref[...] * sa * (1.0 - sa)).astype(BF16)
        dgb_ref[...] = (dmv * yb_ref[...] * sb * (1.0 - sb)).astype(BF16)

    o_sh = jax.ShapeDtypeStruct((t, D_MODEL), BF16)
    return pl.pallas_call(
        body, name=name, out_shape=(o_sh, o_sh, o_sh, o_sh),
        grid=(t // tm, D_MODEL // cw),
        in_specs=[y_s, y_s, y_s, pl.BlockSpec((tm, cw), lambda i, j: (i, OFF_GA // cw + j)),
                  pl.BlockSpec((tm, cw), lambda i, j: (i, OFF_GB // cw + j))],
        out_specs=(y_s, y_s, y_s, y_s), compiler_params=_params(("parallel", "parallel")),
    )(dm, ya, yb, p, p)


def _tri(lower):
    r = lax.broadcasted_iota(jnp.int32, (CHUNK, CHUNK), 0)
    c = lax.broadcasted_iota(jnp.int32, (CHUNK, CHUNK), 1)
    return ((c <= r) if lower else (c >= r)).astype(F32)


def _eye_mask():
    r = lax.broadcasted_iota(jnp.int32, (GLA_DK, GLA_DK), 0)
    c = lax.broadcasted_iota(jnp.int32, (GLA_DK, GLA_DK), 1)
    return r == c


def _row_to_col(v):
    return jnp.sum(jnp.where(_eye_mask(), jnp.broadcast_to(v, (GLA_DK, GLA_DK)), 0.0), axis=1, keepdims=True)


def _col_to_row(v):
    return jnp.sum(jnp.where(_eye_mask(), jnp.broadcast_to(v, (GLA_DK, GLA_DK)), 0.0), axis=0, keepdims=True)


def _dot(a, b, dn):
    return lax.dot_general(a.astype(BF16), b.astype(BF16), (dn, ((), ())), preferred_element_type=F32)


_NN = ((1,), (0,))
_NT = ((1,), (1,))
_TN = ((0,), (0,))


def _gate_logits(lr_ref, wa_ref, ba_ref):
    return _dot(lr_ref[...], wa_ref[...], _NN) + ba_ref[...]


def _chunk_decay(la, tri):
    cum = lax.dot_general(tri, la, ((_NN), ((), ())), precision=lax.Precision.HIGHEST, preferred_element_type=F32)
    e = cum[CHUNK - 1:CHUNK, :]
    return cum, e, jnp.exp(e - cum)


def _gla_fwd(p, wa, ba, name):
    t = p.shape[0]
    rows = min(GLA_ROWS, t)
    cb = rows // CHUNK
    nc = t // CHUNK
    scale = GLA_DK ** -0.5

    def body(q_ref, k_ref, v_ref, lr_ref, wa_ref, ba_ref, o_ref, st_ref, s_scr):
        @pl.when(pl.program_id(0) == 0)
        def _():
            s_scr[...] = jnp.zeros_like(s_scr)

        la_all = _log_sigmoid(_gate_logits(lr_ref, wa_ref, ba_ref)) * (1.0 / GLA_TAU)
        tri = _tri(True)
        for ch in range(cb):
            rs = slice(ch * CHUNK, (ch + 1) * CHUNK)
            for h in range(GLA_HEADS):
                ks = slice(h * GLA_DK, (h + 1) * GLA_DK)
                vs = slice(h * GLA_DV, (h + 1) * GLA_DV)
                _, e, w = _chunk_decay(la_all[rs, ks], tri)
                kd = k_ref[rs, ks] * w
                s_new = _row_to_col(jnp.exp(e)) * s_scr[ks, :] + _dot(kd, v_ref[rs, vs], _TN)
                s_scr[ks, :] = s_new
                st_ref[ch, ks, :] = s_new
                o_ref[rs, vs] = _dot(q_ref[rs, ks] * scale, s_new, _NN)

    return pl.pallas_call(
        body, name=name,
        out_shape=(jax.ShapeDtypeStruct((t, GLA_V), F32), jax.ShapeDtypeStruct((nc, GLA_QK, GLA_DV), F32)),
        grid=(t // rows,),
        in_specs=[pl.BlockSpec((rows, GLA_QK), lambda i: (i, OFF_Q // GLA_QK)),
                  pl.BlockSpec((rows, GLA_QK), lambda i: (i, OFF_K // GLA_QK)),
                  pl.BlockSpec((rows, GLA_V), lambda i: (i, OFF_V // GLA_V)),
                  pl.BlockSpec((rows, LR_PAD), lambda i: (i, OFF_LR // LR_PAD)),
                  pl.BlockSpec((LR_PAD, GLA_QK), lambda i: (0, 0)),
                  pl.BlockSpec((1, GLA_QK), lambda i: (0, 0))],
        out_specs=(pl.BlockSpec((rows, GLA_V), lambda i: (i, 0)),
                   pl.BlockSpec((cb, GLA_QK, GLA_DV), lambda i: (i, 0, 0))),
        scratch_shapes=[pltpu.VMEM((GLA_QK, GLA_DV), F32)],
        compiler_params=_params(("arbitrary",)),
    )(p, p, p, p, wa, ba)


def _gla_bwd(do, p, st, wa, ba, name):
    t = p.shape[0]
    rows = min(GLA_ROWS, t)
    cb = rows // CHUNK
    nb = t // rows
    scale = GLA_DK ** -0.5

    def rev(i):
        return nb - 1 - i

    def body(do_ref, q_ref, k_ref, v_ref, lr_ref, st_ref, stp_ref, wa_ref, ba_ref,
             dq_ref, dk_ref, dv_ref, dlr_ref, dwa_ref, dba_ref, ds_scr, dz_scr):
        i = pl.program_id(0)

        @pl.when(i == 0)
        def _():
            ds_scr[...] = jnp.zeros_like(ds_scr)
            dwa_ref[...] = jnp.zeros_like(dwa_ref)
            dba_ref[...] = jnp.zeros_like(dba_ref)

        z_all = _gate_logits(lr_ref, wa_ref, ba_ref)
        la_all = _log_sigmoid(z_all) * (1.0 / GLA_TAU)
        tri, triu = _tri(True), _tri(False)
        last_row = lax.broadcasted_iota(jnp.int32, (CHUNK, GLA_DK), 0) == CHUNK - 1
        keep_prev = _unless(i == nb - 1)
        for ch in reversed(range(cb)):
            rs = slice(ch * CHUNK, (ch + 1) * CHUNK)
            for h in range(GLA_HEADS):
                ks = slice(h * GLA_DK, (h + 1) * GLA_DK)
                vs = slice(h * GLA_DV, (h + 1) * GLA_DV)
                _, e, w = _chunk_decay(la_all[rs, ks], tri)
                kd = k_ref[rs, ks] * w
                exp_e = jnp.exp(e)
                s_c = st_ref[ch, ks, :]
                if ch > 0:
                    s_p = st_ref[ch - 1, ks, :]
                else:
                    s_p = stp_ref[0, ks, :] * keep_prev
                do_c = do_ref[rs, vs]
                vv = v_ref[rs, vs]
                ds_tot = ds_scr[ks, :] + _dot(q_ref[rs, ks] * scale, do_c, _TN)
                dq_ref[rs, ks] = (_dot(do_c, s_c, _NT) * scale).astype(BF16)
                dkd = _dot(vv, ds_tot, _NT)
                dv_ref[rs, vs] = _dot(kd, ds_tot, _NN).astype(BF16)
                dexp_col = jnp.sum(ds_tot * s_p, axis=1, keepdims=True)
                ds_scr[ks, :] = _row_to_col(exp_e) * ds_tot
                dk_ref[rs, ks] = (dkd * w).astype(BF16)
                dwt = dkd * kd
                de = jnp.sum(dwt, axis=0, keepdims=True) + _col_to_row(dexp_col) * exp_e
                dcum = jnp.where(last_row, de - dwt, -dwt)
                da = lax.dot_general(triu, dcum, (_NN, ((), ())), precision=lax.Precision.HIGHEST,
                                     preferred_element_type=F32)
                dz_scr[rs, ks] = da * (1.0 / GLA_TAU) * _sigmoid(-z_all[rs, ks])
        dz = dz_scr[...]
        dlr_ref[...] = _dot(dz, wa_ref[...], _NT).astype(BF16)
        dwa_ref[...] += _dot(lr_ref[...], dz, _TN)
        dba_ref[...] += jnp.sum(dz, axis=0, keepdims=True)

    qk_sh = jax.ShapeDtypeStruct((t, GLA_QK), BF16)
    return pl.pallas_call(
        body, name=name,
        out_shape=(qk_sh, qk_sh, jax.ShapeDtypeStruct((t, GLA_V), BF16), jax.ShapeDtypeStruct((t, LR_PAD), BF16),
                   jax.ShapeDtypeStruct((LR_PAD, GLA_QK), F32), jax.ShapeDtypeStruct((1, GLA_QK), F32)),
        grid=(nb,),
        in_specs=[pl.BlockSpec((rows, GLA_V), lambda i: (rev(i), 0)),
                  pl.BlockSpec((rows, GLA_QK), lambda i: (rev(i), OFF_Q // GLA_QK)),
                  pl.BlockSpec((rows, GLA_QK), lambda i: (rev(i), OFF_K // GLA_QK)),
                  pl.BlockSpec((rows, GLA_V), lambda i: (rev(i), OFF_V // GLA_V)),
                  pl.BlockSpec((rows, LR_PAD), lambda i: (rev(i), OFF_LR // LR_PAD)),
                  pl.BlockSpec((cb, GLA_QK, GLA_DV), lambda i: (rev(i), 0, 0)),
                  pl.BlockSpec((1, GLA_QK, GLA_DV), lambda i: (jnp.maximum(rev(i) * cb - 1, 0), 0, 0)),
                  pl.BlockSpec((LR_PAD, GLA_QK), lambda i: (0, 0)),
                  pl.BlockSpec((1, GLA_QK), lambda i: (0, 0))],
        out_specs=(pl.BlockSpec((rows, GLA_QK), lambda i: (rev(i), 0)),
                   pl.BlockSpec((rows, GLA_QK), lambda i: (rev(i), 0)),
                   pl.BlockSpec((rows, GLA_V), lambda i: (rev(i), 0)),
                   pl.BlockSpec((rows, LR_PAD), lambda i: (rev(i), 0)),
                   pl.BlockSpec((LR_PAD, GLA_QK), lambda i: (0, 0)),
                   pl.BlockSpec((1, GLA_QK), lambda i: (0, 0))),
        scratch_shapes=[pltpu.VMEM((GLA_QK, GLA_DV), F32), pltpu.VMEM((rows, GLA_QK), F32)],
        compiler_params=_params(("arbitrary",)),
    )(do, p, p, p, p, st, st, wa, ba)


def _gla_out_fwd(o, p, gng, name):
    t = o.shape[0]
    tm = min(TM_EW, t)

    def body(o_ref, r_ref, g_ref, z_ref):
        gv = g_ref[...]
        for h in range(GLA_HEADS):
            vs = slice(h * GLA_DV, (h + 1) * GLA_DV)
            ov, rv = o_ref[:, vs], r_ref[:, vs]
            z_ref[:, vs] = ((ov * _rstd(ov) * gv) * (rv * _sigmoid(rv))).astype(BF16)

    return pl.pallas_call(
        body, name=name, out_shape=jax.ShapeDtypeStruct((t, GLA_V), BF16), grid=(t // tm,),
        in_specs=[pl.BlockSpec((tm, GLA_V), lambda i: (i, 0)),
                  pl.BlockSpec((tm, GLA_V), lambda i: (i, OFF_R // GLA_V)),
                  pl.BlockSpec((1, GLA_DV), lambda i: (0, 0))],
        out_specs=pl.BlockSpec((tm, GLA_V), lambda i: (i, 0)),
        compiler_params=_params(("parallel",)),
    )(o, p, gng)


def _gla_out_bwd(dz, o, p, gng, name):
    t = o.shape[0]
    tm = min(TM_EW, t)

    def body(dz_ref, o_ref, r_ref, g_ref, do_ref, dr_ref, dg_ref):
        @pl.when(pl.program_id(0) == 0)
        def _():
            dg_ref[...] = jnp.zeros_like(dg_ref)

        gv = g_ref[...]
        for h in range(GLA_HEADS):
            vs = slice(h * GLA_DV, (h + 1) * GLA_DV)
            ov, rv, dzv = o_ref[:, vs], r_ref[:, vs], dz_ref[:, vs]
            rs = _rstd(ov)
            oh = ov * rs
            sg = _sigmoid(rv)
            dr_ref[:, vs] = (dzv * (oh * gv) * (sg * (1.0 + rv * (1.0 - sg)))).astype(BF16)
            don = dzv * (rv * sg)
            dg_ref[...] += jnp.sum(don * oh, axis=0, keepdims=True)
            doh = don * gv
            do_ref[:, vs] = rs * (doh - oh * jnp.mean(doh * oh, axis=-1, keepdims=True))

    row = pl.BlockSpec((tm, GLA_V), lambda i: (i, 0))
    return pl.pallas_call(
        body, name=name,
        out_shape=(jax.ShapeDtypeStruct((t, GLA_V), F32), jax.ShapeDtypeStruct((t, GLA_V), BF16),
                   jax.ShapeDtypeStruct((1, GLA_DV), F32)),
        grid=(t // tm,),
        in_specs=[row, row, pl.BlockSpec((tm, GLA_V), lambda i: (i, OFF_R // GLA_V)),
                  pl.BlockSpec((1, GLA_DV), lambda i: (0, 0))],
        out_specs=(row, row, pl.BlockSpec((1, GLA_DV), lambda i: (0, 0))),
        compiler_params=_params(("arbitrary",)),
    )(dz, o, p, gng)


def _ada_fwd(c_all, w, b, layer, name):
    n = w.shape[2]
    tn = _pick(n, 512)

    def body(c_ref, w_ref, b_ref, o_ref):
        cv = c_ref[...]
        o_ref[...] = _dot(cv * _sigmoid(cv), w_ref[...], _NN) + b_ref[...]

    return pl.pallas_call(
        body, name=name, out_shape=jax.ShapeDtypeStruct((16, n), F32), grid=(n // tn,),
        in_specs=[pl.BlockSpec((16, D_MODEL), lambda j: (0, 0)),
                  pl.BlockSpec((None, D_MODEL, tn), lambda j: (layer, 0, j)),
                  pl.BlockSpec((1, tn), lambda j: (0, j))],
        out_specs=pl.BlockSpec((16, tn), lambda j: (0, j)),
        compiler_params=_params(("parallel",)),
    )(c_all, w, b)


def _ada_bwd(c_all, dmod, name):
    n = dmod.shape[1]
    tn = _pick(n, 512)

    def body(c_ref, d_ref, o_ref):
        cv = c_ref[...]
        o_ref[...] = _dot(cv * _sigmoid(cv), d_ref[...], _TN)

    return pl.pallas_call(
        body, name=name, out_shape=jax.ShapeDtypeStruct((D_MODEL, n), F32), grid=(n // tn,),
        in_specs=[pl.BlockSpec((16, D_MODEL), lambda j: (0, 0)), pl.BlockSpec((16, tn), lambda j: (0, j))],
        out_specs=pl.BlockSpec((D_MODEL, tn), lambda j: (0, j)),
        compiler_params=_params(("parallel",)),
    )(c_all, dmod)


def _rows_tile(nrows, ncols, target_bytes):
    want = max(16, target_bytes // (4 * ncols))
    if nrows <= want:
        return nrows
    t = (want // 16) * 16
    while t >= 16:
        if nrows % t == 0:
            return t
        t -= 16
    return nrows


def _sum_chips(sent, landed, chip, name):
    _, nrows, ncols = sent[0].shape
    tr = _rows_tile(nrows, ncols, 2 << 20)
    nblk = nrows // tr

    def body(chip_ref, *refs):
        own, got, o_ref = refs[:DEPTH], refs[DEPTH:2 * DEPTH], refs[2 * DEPTH]
        me = chip_ref[0]
        for l in range(DEPTH):
            for j in range(N_CHIPS):
                def add(val):
                    if j == 0:
                        o_ref[...] = val.astype(F32)
                    else:
                        o_ref[...] += val.astype(F32)

                @pl.when(jnp.logical_and(pl.program_id(0) == l, me == j))
                def _():
                    add(own[l][...])

                @pl.when(jnp.logical_and(pl.program_id(0) == l, me != j))
                def _():
                    add(got[l][j])

    def rows_of(layer):
        return lambda l, i, chip_ref: jnp.where(l == layer, i, 0)

    own_specs = [pl.BlockSpec((None, tr, ncols), lambda l, i, chip_ref, r=rows_of(k): (chip_ref[0], r(l, i, chip_ref), 0))
                 for k in range(DEPTH)]
    got_specs = [pl.BlockSpec((N_CHIPS, tr, ncols), lambda l, i, chip_ref, r=rows_of(k): (0, r(l, i, chip_ref), 0))
                 for k in range(DEPTH)]
    return pl.pallas_call(
        body, name=name, out_shape=jax.ShapeDtypeStruct((DEPTH * nrows, ncols), F32),
        grid_spec=pltpu.PrefetchScalarGridSpec(
            num_scalar_prefetch=1, grid=(DEPTH, nblk), in_specs=own_specs + got_specs,
            out_specs=pl.BlockSpec((tr, ncols), lambda l, i, chip_ref: (l * nblk + i, 0))),
        compiler_params=_params(("arbitrary", "arbitrary")),
    )(chip, *sent, *landed)


def _adamw(w, m, v, ga, gb, name):
    nrows, ncols = w.shape
    tr = _rows_tile(nrows, ncols, 1 << 20)
    two = gb is not None
    c1 = 1.0 - ADAM_B1 ** ADAM_STEP
    c2 = 1.0 - ADAM_B2 ** ADAM_STEP

    def body(*refs):
        if two:
            w_ref, m_ref, v_ref, ga_ref, gb_ref, g_ref, d_ref, nm_ref, nv_ref = refs
            g = ga_ref[...] + gb_ref[...]
        else:
            w_ref, m_ref, v_ref, ga_ref, g_ref, d_ref, nm_ref, nv_ref = refs
            g = ga_ref[...]
        g_ref[...] = g
        nm = ADAM_B1 * m_ref[...] + (1.0 - ADAM_B1) * g
        nv = ADAM_B2 * v_ref[...] + (1.0 - ADAM_B2) * (g * g)
        nm_ref[...] = nm
        nv_ref[...] = nv
        d_ref[...] = -ADAM_LR * ((nm / c1) / (jnp.sqrt(nv / c2) + ADAM_EPS) + ADAM_WD * w_ref[...])

    blk = pl.BlockSpec((tr, ncols), lambda i: (i, 0))
    sh = jax.ShapeDtypeStruct((nrows, ncols), F32)
    ins = [w, m, v, ga] + ([gb] if two else [])
    return pl.pallas_call(
        body, name=name, out_shape=(sh, sh, sh, sh), grid=(nrows // tr,),
        in_specs=[blk] * len(ins), out_specs=(blk, blk, blk, blk),
        compiler_params=_params(("parallel",)),
    )(*ins)


def _pad_rows(a, rows):
    return jnp.concatenate([a, jnp.zeros((rows - a.shape[0],) + a.shape[1:], a.dtype)], axis=0)


def _cols_from_chips(a):
    return jnp.transpose(a, (1, 0, 2)).reshape(a.shape[1], N_CHIPS * a.shape[2])


def _cols_to_chips(a):
    rows, n = a.shape
    return jnp.transpose(a.reshape(rows, N_CHIPS, n // N_CHIPS), (1, 0, 2))


def _pad_w_in(w):
    return jnp.concatenate([w[:, :3072], w[:, 3088:], w[:, 3072:3088],
                            jnp.zeros((w.shape[0], LR_PAD - GLA_LOWRANK), w.dtype)], axis=1)


def _unpad_w_in(w):
    return jnp.concatenate([w[:, :3072], w[:, OFF_LR:OFF_LR + GLA_LOWRANK], w[:, 3072:OFF_LR]], axis=1)


_BIG = ("w_in", "w_og", "w_oc", "w_o", "w_up", "w_dn")
_ROW_SHARDED = ("w_o", "w_dn")


def kernel(x, c, w_ada, b_ada, norm_g, w_in, w_a2, b_a2, gla_norm_g, w_out_gla, conv_mix_w, w_out_conv, w_o, w_up, ffn_conv_w, w_down, loss_target, m_w_ada, m_b_ada, m_norm_g, m_w_in, m_w_a2, m_b_a2, m_gla_norm_g, m_w_out_gla, m_conv_mix_w, m_w_out_conv, m_w_o, m_w_up, m_ffn_conv_w, m_w_down, v_w_ada, v_b_ada, v_norm_g, v_w_in, v_w_a2, v_b_a2, v_gla_norm_g, v_w_out_gla, v_conv_mix_w, v_w_out_conv, v_w_o, v_w_up, v_ffn_conv_w, v_w_down):
    xi, yi, ci = lax.axis_index("x"), lax.axis_index("y"), lax.axis_index("c")
    chip = 2 * xi + yi
    dev = 2 * chip + ci
    chip_arr = jnp.reshape(chip, (1,)).astype(jnp.int32)
    xt = x[0]
    tgt = loss_target[0]

    big = dict(w_in=w_in, w_og=w_out_gla, w_oc=w_out_conv, w_o=w_o, w_up=w_up, w_dn=w_down)
    gathers = {}
    tok = jnp.zeros((), F32)
    for l in range(DEPTH):
        for k in _BIG:
            shard = (big[k][l] + tok).astype(BF16)
            *handle, token = _xchg_start(shard, False, "gather_start_%s_%d" % (k, l))
            gathers[k, l] = (shard, tuple(handle))
            tok = token[0, 0]

    def gathered(k, l, after):
        shard, handle = gathers[k, l]
        _, land = _xchg_wait(handle, after, False, "gather_wait_%s_%d" % (k, l))
        slot = lax.broadcasted_iota(jnp.int32, (N_CHIPS, 1, 1), 0)
        full = jnp.where(slot == chip, shard[None], land)
        if k in _ROW_SHARDED:
            return full.reshape(N_CHIPS * full.shape[1], full.shape[2])
        return _cols_from_chips(full)

    c_all = _allgather8(jnp.broadcast_to(c + tok, (8, D_MODEL)), "gather_c")[0][:, 0, :]
    c16 = _pad_rows(c_all, 16)
    sm_parts = [norm_g.reshape(-1), w_a2.reshape(-1), conv_mix_w.reshape(-1), ffn_conv_w.reshape(-1)]
    sm_sizes = [a.shape[0] for a in sm_parts]
    sm_flat = jnp.concatenate(sm_parts)
    sm_rows = -(-sm_flat.shape[0] // 128)
    sm_rows = -(-sm_rows // 8) * 8
    sm_flat = jnp.concatenate([sm_flat, jnp.zeros((sm_rows * 128 - sm_flat.shape[0],), F32)]).reshape(sm_rows, 128)
    sm_all = _allgather8(sm_flat, "gather_small")[0].reshape(N_DEV, -1)[0::2]
    offs = [0]
    for s in sm_sizes:
        offs.append(offs[-1] + s)

    def small_full(idx, shape):
        a = sm_all[:, offs[idx]:offs[idx + 1]].reshape((N_CHIPS,) + shape)
        a = jnp.moveaxis(a, 0, -2)
        return a.reshape(shape[:-1] + (N_CHIPS * shape[-1],))

    norm_g_f = small_full(0, (DEPTH, 4, 512))
    w_a2_f = small_full(1, (DEPTH, GLA_LOWRANK, 128))
    conv_w_f = small_full(2, (DEPTH, 3, 256))
    ffn_w_f = small_full(3, (DEPTH, 3, 1408))

    b_loc = lax.dynamic_slice(b_ada, (0, chip * 3072), (DEPTH, 3072))
    mod_loc = jnp.concatenate(
        [_ada_fwd(c16, w_ada, b_loc[l:l + 1], l, "ada_fwd")[:8] for l in range(DEPTH)], axis=0)
    mod_all = _allgather8(mod_loc, "gather_mod")[0][0::2]
    mods = []
    for l in range(DEPTH):
        row = lax.dynamic_slice(mod_all, (0, l * 8 + dev, 0), (N_CHIPS, 1, 3072)).reshape(1, 6 * D_MODEL)
        mods.append([row[:, k * D_MODEL:(k + 1) * D_MODEL] for k in range(6)])

    saved = []
    h = None
    xin = xt
    for l in range(DEPTH):
        sh1, sc1, g1, sh2, sc2, g2 = mods[l]
        gn = [norm_g_f[l, k][None] for k in range(4)]
        wa = _pad_rows(w_a2_f[l], LR_PAD)
        ba = b_a2[l][None]
        gng = gla_norm_g[l][None]
        cw8 = _pad_rows(conv_w_f[l], 8)
        fw8 = _pad_rows(ffn_w_f[l], 8)
        if l == 0:
            h = _pre_norm(xin, gn[0], sc1, sh1, "pre_norm")
        wi = _pad_w_in(gathered("w_in", l, h))
        p = _matmul(h, wi, "nn", F32, "mm_in", tn=1152)
        o, st = _gla_fwd(p, wa, ba, "gla_fwd")
        za = _gla_out_fwd(o, p, gng, "gla_out_fwd")
        zb = _conv_fwd(p, cw8, "conv_fwd")
        wog, woc = gathered("w_og", l, zb), gathered("w_oc", l, zb)
        ya = _matmul(za, wog, "nn", F32, "mm_out_gla")
        yb = _matmul(zb, woc, "nn", F32, "mm_out_conv")
        mm = _merge_fwd(ya, yb, p, "merge_fwd")
        wo = gathered("w_o", l, mm)
        y = _matmul(mm, wo, "nn", F32, "mm_o")
        x1, h2 = _post_pre(xin, y, g1, gn[1], gn[2], sc2, sh2, "post_pre")
        wup = gathered("w_up", l, h2)
        u = _matmul(h2, wup, "nn", F32, "mm_up")
        f = _ffn_fwd(u, fw8, "ffn_fwd")
        wdn = gathered("w_dn", l, f)
        y2 = _matmul(f, wdn, "nn", F32, "mm_down")
        saved.append(dict(xin=xin, h=h, p=p, o=o, st=st, za=za, zb=zb, ya=ya, yb=yb, mm=mm, y=y, x1=x1, h2=h2,
                          u=u, f=f, y2=y2, wi=wi, wog=wog, woc=woc, wo=wo, wup=wup, wdn=wdn, wa=wa, ba=ba,
                          gng=gng, cw8=cw8, fw8=fw8, gn=gn, mod=mods[l]))
        if l + 1 < DEPTH:
            nsh1, nsc1 = mods[l + 1][0], mods[l + 1][1]
            xin, h = _post_pre(x1, y2, g2, gn[3], norm_g_f[l + 1, 0][None], nsc1, nsh1, "post_pre")
        else:
            dx, loss_tile = _post_loss(x1, y2, g2, gn[3], tgt, "post_loss")
    loss = lax.psum(loss_tile[0, 0], ("x", "y", "c"))

    scatters = {}

    def scatter(k, l, dw):
        send = dw.reshape(N_CHIPS, dw.shape[0] // N_CHIPS, dw.shape[1]) if k in _ROW_SHARDED else _cols_to_chips(dw)
        *handle, token = _xchg_start(send, True, "scatter_start_%s_%d" % (k, l))
        scatters[k, l] = tuple(handle)
        return token[0, 0]

    sm = {k: [None] * DEPTH for k in ("dmod", "norm_g", "w_a2", "b_a2", "gng", "conv_w", "ffn_w")}
    for l in reversed(range(DEPTH)):
        s = saved[l]
        sh1, sc1, g1, sh2, sc2, g2 = s["mod"]
        gn = s["gn"]
        dy2, dg2, dgn3 = _post_bwd(dx, s["y2"], g2, gn[3], "post_bwd")
        tk = scatter("w_dn", l, _matmul(s["f"], dy2, "tn", BF16, "mm_down_dw", tm=512, tn=1024, tk=2048))
        df = _matmul(dy2, s["wdn"], "nt", F32, "mm_down_dx")
        dgate, dup, dfw = _ffn_bwd(df, s["u"], s["fw8"] + tk, "ffn_bwd")
        du = jnp.concatenate([dgate, dup], axis=1)
        tk = scatter("w_up", l, _matmul(s["h2"], du, "tn", BF16, "mm_up_dw", tm=512, tn=1024, tk=2048))
        dh2 = _matmul(du, s["wup"], "nt", F32, "mm_up_dx", tn=1024, tk=2816)
        dx1, dsh2, dsc2, dgn2 = _pre_bwd(dh2, s["x1"], dx, gn[2] + tk, sc2, "pre_bwd")
        dy, dg1, dgn1 = _post_bwd(dx1, s["y"], g1, gn[1], "post_bwd")
        tk = scatter("w_o", l, _matmul(s["mm"], dy, "tn", BF16, "mm_o_dw"))
        dm = _matmul(dy, s["wo"], "nt", F32, "mm_o_dx")
        dya, dyb, dga, dgb = _merge_bwd(dm, s["ya"], s["yb"], s["p"], "merge_bwd")
        tk = tk + scatter("w_og", l, _matmul(s["za"], dya, "tn", BF16, "mm_out_gla_dw"))
        dza = _matmul(dya, s["wog"], "nt", F32, "mm_out_gla_dx")
        do, dr, dgng = _gla_out_bwd(dza, s["o"], s["p"], s["gng"] + tk, "gla_out_bwd")
        tk = scatter("w_oc", l, _matmul(s["zb"], dyb, "tn", BF16, "mm_out_conv_dw"))
        dzb = _matmul(dyb, s["woc"], "nt", F32, "mm_out_conv_dx")
        dcb, dcc, dcx, dcw = _conv_bwd(dzb, s["p"], s["cw8"] + tk, "conv_bwd")
        dq, dk, dv, dlr, dwa, dba = _gla_bwd(do, s["p"], s["st"], s["wa"], s["ba"], "gla_bwd")
        dp = jnp.concatenate([dq, dk, dv, dr, dcb, dcc, dcx, dga, dgb, dlr], axis=1)
        tk = scatter("w_in", l, _unpad_w_in(_matmul(s["h"], dp, "tn", BF16, "mm_in_dw", tm=512, tn=1152, tk=2048)))
        dh = _matmul(dp, s["wi"], "nt", F32, "mm_in_dx", tn=1024, tk=1152)
        dx, dsh1, dsc1, dgn0 = _pre_bwd(dh, s["xin"], dx1, gn[0] + tk, sc1, "pre_bwd")
        sm["dmod"][l] = jnp.concatenate([dsh1, dsc1, dg1, dsh2, dsc2, dg2], axis=1)[0]
        sm["norm_g"][l] = jnp.concatenate([dgn0, dgn1, dgn2, dgn3], axis=0)
        sm["w_a2"][l] = dwa[:GLA_LOWRANK]
        sm["b_a2"][l] = dba[0]
        sm["gng"][l] = dgng[0]
        sm["conv_w"][l] = dcw[:3]
        sm["ffn_w"][l] = dfw[:3]
    grad_x = dx[None]

    names = ("dmod", "norm_g", "w_a2", "b_a2", "gng", "conv_w", "ffn_w")
    parts = [jnp.stack(sm[k]).reshape(-1) for k in names]
    shapes = [jnp.stack(sm[k]).shape for k in names]
    sizes = [a.shape[0] for a in parts]
    flat = jnp.concatenate(parts)
    rows = -(-flat.shape[0] // 1024) * 8
    flat = jnp.concatenate([flat, jnp.zeros((rows * 128 - flat.shape[0],), F32)]).reshape(rows, 128)
    gath, tot = _allgather8(flat, "reduce_small")
    po = [0]
    for s_ in sizes:
        po.append(po[-1] + s_)
    tot = tot.reshape(-1)
    tot_of = {k: tot[po[i]:po[i + 1]].reshape(shapes[i]) for i, k in enumerate(names)}
    dmod_all = gath.reshape(N_DEV, -1)[:, po[0]:po[1]].reshape(N_DEV, DEPTH, 6 * D_MODEL)

    def chip_cols(a, width):
        return lax.dynamic_slice_in_dim(a, chip * width, width, axis=a.ndim - 1)

    g_w_ada = []
    for l in range(DEPTH):
        dml = _pad_rows(chip_cols(dmod_all[:, l], 3072), 16)
        g_w_ada.append(_ada_bwd(c16, dml, "ada_bwd"))
    g_w_ada = jnp.stack(g_w_ada)

    plane = []
    for k in _BIG:
        done = [_xchg_wait(scatters[k, l], g_w_ada, True, "scatter_wait_%s_%d" % (k, l)) for l in range(DEPTH)]
        plane.append(_sum_chips([d[0] for d in done], [d[1] for d in done], chip_arr, "sum_chips"))
    other = _sibling_exchange(plane, "sibling_grads")

    def upd(w, m, v, ga, gb, name):
        sh = w.shape
        two_d = (-1, sh[-1])
        outs = _adamw(w.reshape(two_d), m.reshape(two_d), v.reshape(two_d), ga.reshape(two_d),
                      None if gb is None else gb.reshape(two_d), name)
        return [a.reshape(sh) for a in outs]

    res = {}
    res["w_ada"] = upd(w_ada, m_w_ada, v_w_ada, g_w_ada, None, "adamw")
    res["b_ada"] = upd(b_ada, m_b_ada, v_b_ada, tot_of["dmod"], None, "adamw")
    res["norm_g"] = upd(norm_g, m_norm_g, v_norm_g, chip_cols(tot_of["norm_g"], 512), None, "adamw")
    res["w_in"] = upd(w_in, m_w_in, v_w_in, plane[0], other[0], "adamw")
    res["w_a2"] = upd(w_a2, m_w_a2, v_w_a2, chip_cols(tot_of["w_a2"], 128), None, "adamw")
    res["b_a2"] = upd(b_a2, m_b_a2, v_b_a2, tot_of["b_a2"], None, "adamw")
    res["gla_norm_g"] = upd(gla_norm_g, m_gla_norm_g, v_gla_norm_g, tot_of["gng"], None, "adamw")
    res["w_out_gla"] = upd(w_out_gla, m_w_out_gla, v_w_out_gla, plane[1], other[1], "adamw")
    res["conv_mix_w"] = upd(conv_mix_w, m_conv_mix_w, v_conv_mix_w, chip_cols(tot_of["conv_w"], 256), None, "adamw")
    res["w_out_conv"] = upd(w_out_conv, m_w_out_conv, v_w_out_conv, plane[2], other[2], "adamw")
    res["w_o"] = upd(w_o, m_w_o, v_w_o, plane[3], other[3], "adamw")
    res["w_up"] = upd(w_up, m_w_up, v_w_up, plane[4], other[4], "adamw")
    res["ffn_conv_w"] = upd(ffn_conv_w, m_ffn_conv_w, v_ffn_conv_w, chip_cols(tot_of["ffn_w"], 1408), None, "adamw")
    res["w_down"] = upd(w_down, m_w_down, v_w_down, plane[5], other[5], "adamw")
    order = ("w_ada", "b_ada", "norm_g", "w_in", "w_a2", "b_a2", "gla_norm_g", "w_out_gla", "conv_mix_w",
             "w_out_conv", "w_o", "w_up", "ffn_conv_w", "w_down")
    return (loss, grad_x, *[res[k][0] for k in order], *[res[k][1] for k in order],
            *[res[k][2] for k in order], *[res[k][3] for k in order])
```

```python
import functools
import math

import jax
import jax.numpy as jnp
from jax import lax
from jax.experimental import pallas as pl
from jax.experimental.pallas import tpu as pltpu

F32 = jnp.float32
BF16 = jnp.bfloat16
MESH = pl.DeviceIdType.MESH

D_MODEL = 2048
DEPTH = 2
CHUNK = 64
GLA_HEADS = 4
GLA_DK = 128
GLA_DV = 256
GLA_QK = GLA_HEADS * GLA_DK
GLA_V = GLA_HEADS * GLA_DV
GLA_LOWRANK = 16
GLA_TAU = 16.0
CONV_WIDTH = 1024
D_FF = 5632
EPS = 1e-6
N_IN = 10256
LR_PAD = 128
N_IN_PAD = N_IN - GLA_LOWRANK + LR_PAD
OFF_Q, OFF_K, OFF_V, OFF_R = 0, 512, 1024, 2048
OFF_CB, OFF_CC, OFF_CX, OFF_GA, OFF_GB, OFF_LR = 3072, 4096, 5120, 6144, 8192, 10240

ADAM_LR = 0.001
ADAM_B1 = 0.9
ADAM_B2 = 0.999
ADAM_EPS = 1e-08
ADAM_WD = 0.01
ADAM_STEP = 10

N_CHIPS = 4
N_DEV = 8
VMEM_LIMIT = 56 * 1024 * 1024
TM_ROW = 256
TM_EW = 512
CW_EW = 512
GLA_ROWS = 256


def _params(sem=None):
    return pltpu.CompilerParams(dimension_semantics=sem, vmem_limit_bytes=VMEM_LIMIT)


def _sigmoid(v):
    return 1.0 / (1.0 + jnp.exp(-v))


def _log_sigmoid(v):
    return jnp.minimum(v, 0.0) - jnp.log(1.0 + jnp.exp(-jnp.abs(v)))


_GELU_C = math.sqrt(2.0 / math.pi)


def _gelu(v):
    return 0.5 * v * (1.0 + jnp.tanh(_GELU_C * (v + 0.044715 * v * v * v)))


def _gelu_grad(v):
    t = jnp.tanh(_GELU_C * (v + 0.044715 * v * v * v))
    return 0.5 * (1.0 + t) + 0.5 * v * (1.0 - t * t) * _GELU_C * (1.0 + 3.0 * 0.044715 * v * v)


def _flip(a, d):
    return a + d - 2 * a * d


def _unless(cond):
    return jnp.where(cond, 0.0, 1.0).astype(F32)


def _allgather8(xv, name):
    r, cdim = xv.shape

    def body(x_ref, out_ref, sum_ref, send_sems, recv_sems):
        xi, yi, ci = lax.axis_index("x"), lax.axis_index("y"), lax.axis_index("c")
        me = 4 * xi + 2 * yi + ci
        out_ref[pl.ds(me, 1)] = x_ref[...][None]
        started = []
        for k in range(1, N_DEV):
            px, py, pc = _flip(xi, (k >> 2) & 1), _flip(yi, (k >> 1) & 1), _flip(ci, k & 1)
            cp = pltpu.make_async_remote_copy(
                src_ref=x_ref, dst_ref=out_ref.at[me], send_sem=send_sems.at[k - 1], recv_sem=recv_sems.at[k - 1],
                device_id=(px, py, pc), device_id_type=MESH)
            cp.start()
            started.append((cp, 4 * px + 2 * py + pc, k, (px, py, pc)))
        for cp, peer, k, pid in started:
            cp.wait_send()
            pltpu.make_async_remote_copy(
                src_ref=x_ref, dst_ref=out_ref.at[peer], send_sem=send_sems.at[k - 1], recv_sem=recv_sems.at[k - 1],
                device_id=pid, device_id_type=MESH).wait_recv()
        acc = out_ref[0]
        for d in range(1, N_DEV):
            acc = acc + out_ref[d]
        sum_ref[...] = acc

    return pl.pallas_call(
        body, name=name,
        out_shape=(jax.ShapeDtypeStruct((N_DEV, r, cdim), F32), jax.ShapeDtypeStruct((r, cdim), F32)),
        in_specs=[pl.BlockSpec(memory_space=pltpu.VMEM)],
        out_specs=(pl.BlockSpec(memory_space=pltpu.VMEM), pl.BlockSpec(memory_space=pltpu.VMEM)),
        scratch_shapes=[pltpu.SemaphoreType.DMA((N_DEV - 1,)), pltpu.SemaphoreType.DMA((N_DEV - 1,))],
        compiler_params=pltpu.CompilerParams(vmem_limit_bytes=VMEM_LIMIT),
    )(xv)


_HBM = pl.BlockSpec(memory_space=pltpu.HBM)
_SEM = pl.BlockSpec(memory_space=pltpu.SEMAPHORE)
_EFFECT = pltpu.SideEffectType.DATAFLOW_SIDE_EFFECTING
_CHIP_FLIPS = ((1, 0), (0, 1), (1, 1))


def _chip_copies(src_ref, land_ref, send_sems, recv_sems, scatter):
    xi, yi, ci = lax.axis_index("x"), lax.axis_index("y"), lax.axis_index("c")
    me = 2 * xi + yi
    out = []
    for k, (dx, dy) in enumerate(_CHIP_FLIPS):
        px, py = _flip(xi, dx), _flip(yi, dy)
        peer = 2 * px + py
        src = src_ref.at[peer] if scatter else src_ref
        mk = functools.partial(pltpu.make_async_remote_copy, src_ref=src, send_sem=send_sems.at[k],
                               recv_sem=recv_sems.at[k], device_id=(px, py, ci), device_id_type=MESH)
        out.append((mk(dst_ref=land_ref.at[me]), mk(dst_ref=land_ref.at[peer])))
    return out


def _xchg_start(src, scatter, name):
    land_shape = src.shape if scatter else (N_CHIPS,) + src.shape

    def body(src_ref, land_ref, send_sems, recv_sems, src_thru, land_thru, token):
        for mine, _ in _chip_copies(src_ref, land_ref, send_sems, recv_sems, scatter):
            mine.start()
        token[...] = jnp.zeros_like(token)

    return pl.pallas_call(
        body, name=name,
        out_shape=(pltpu.SemaphoreType.DMA((3,)), pltpu.SemaphoreType.DMA((3,)), pltpu.HBM(src.shape, src.dtype),
                   pltpu.HBM(land_shape, src.dtype), jax.ShapeDtypeStruct((8, 128), F32)),
        in_specs=(_HBM, _HBM), out_specs=(_SEM, _SEM, _HBM, _HBM, pl.BlockSpec(memory_space=pltpu.VMEM)),
        input_output_aliases={0: 2, 1: 3},
        compiler_params=pltpu.CompilerParams(has_side_effects=_EFFECT),
    )(pltpu.with_memory_space_constraint(src, pltpu.HBM),
      pltpu.with_memory_space_constraint(lax.empty(land_shape, src.dtype), pltpu.HBM))


def _xchg_wait(handle, after, scatter, name):
    send, recv, src_thru, land_thru = handle

    def body(src_ref, land_ref, send_sems, recv_sems, after_ref, src_out, land_out):
        for mine, theirs in _chip_copies(src_ref, land_ref, send_sems, recv_sems, scatter):
            mine.wait_send()
            theirs.wait_recv()

    return pl.pallas_call(
        body, name=name,
        out_shape=(pltpu.HBM(src_thru.shape, src_thru.dtype), pltpu.HBM(land_thru.shape, land_thru.dtype)),
        in_specs=(_HBM, _HBM, _SEM, _SEM, pl.BlockSpec(memory_space=pl.ANY)), out_specs=(_HBM, _HBM),
        input_output_aliases={0: 0, 1: 1},
        compiler_params=pltpu.CompilerParams(has_side_effects=_EFFECT),
    )(src_thru, land_thru, send, recv, after)


def _sibling_exchange(arrays, name):
    n = len(arrays)

    def body(*refs):
        ins, outs = refs[:n], refs[n:2 * n]
        send_sems, recv_sems = refs[2 * n:]
        xi, yi, ci = lax.axis_index("x"), lax.axis_index("y"), lax.axis_index("c")
        cps = []
        for i in range(n):
            cp = pltpu.make_async_remote_copy(
                src_ref=ins[i], dst_ref=outs[i], send_sem=send_sems.at[i], recv_sem=recv_sems.at[i],
                device_id=(xi, yi, 1 - ci), device_id_type=MESH)
            cp.start()
            cps.append(cp)
        for cp in cps:
            cp.wait()

    return pl.pallas_call(
        body, name=name, out_shape=tuple(jax.ShapeDtypeStruct(a.shape, a.dtype) for a in arrays),
        in_specs=[pl.BlockSpec(memory_space=pl.ANY)] * n,
        out_specs=tuple(pl.BlockSpec(memory_space=pl.ANY) for _ in range(n)),
        scratch_shapes=[pltpu.SemaphoreType.DMA((n,)), pltpu.SemaphoreType.DMA((n,))],
    )(*arrays)


def _pick(dim, pref):
    if dim <= pref:
        return dim
    t = (pref // 128) * 128
    while t >= 128:
        if dim % t == 0:
            return t
        t -= 128
    return dim


def _matmul(a, b, dims, out_dtype, name, tm=512, tn=1024, tk=2048):
    if dims == "nn":
        (m, kd), (_, n) = a.shape, b.shape
    elif dims == "nt":
        (m, kd), (n, _) = a.shape, b.shape
    else:
        (kd, m), (_, n) = a.shape, b.shape
    tm, tn, tk = _pick(m, tm), _pick(n, tn), _pick(kd, tk)
    nk = kd // tk
    if dims == "nn":
        a_spec = pl.BlockSpec((tm, tk), lambda j, i, k: (i, k))
        b_spec = pl.BlockSpec((tk, tn), lambda j, i, k: (k, j))
        dn = (((1,), (0,)), ((), ()))
    elif dims == "nt":
        a_spec = pl.BlockSpec((tm, tk), lambda j, i, k: (i, k))
        b_spec = pl.BlockSpec((tn, tk), lambda j, i, k: (j, k))
        dn = (((1,), (1,)), ((), ()))
    else:
        a_spec = pl.BlockSpec((tk, tm), lambda j, i, k: (k, i))
        b_spec = pl.BlockSpec((tk, tn), lambda j, i, k: (k, j))
        dn = (((0,), (0,)), ((), ()))

    def body(a_ref, b_ref, o_ref, acc_ref):
        part = lax.dot_general(a_ref[...].astype(BF16), b_ref[...].astype(BF16), dn, preferred_element_type=F32)
        if nk == 1:
            o_ref[...] = part.astype(o_ref.dtype)
        else:
            k = pl.program_id(2)

            @pl.when(k == 0)
            def _():
                acc_ref[...] = part

            @pl.when(k > 0)
            def _():
                acc_ref[...] += part

            @pl.when(k == nk - 1)
            def _():
                o_ref[...] = acc_ref[...].astype(o_ref.dtype)

    return pl.pallas_call(
        body, name=name, out_shape=jax.ShapeDtypeStruct((m, n), out_dtype),
        grid=(n // tn, m // tm, nk),
        in_specs=[a_spec, b_spec],
        out_specs=pl.BlockSpec((tm, tn), lambda j, i, k: (i, j)),
        scratch_shapes=[pltpu.VMEM((tm, tn), F32)],
        compiler_params=_params(("parallel", "parallel", "arbitrary")),
    )(a, b)


def _rstd(v):
    return lax.rsqrt(jnp.mean(v * v, axis=-1, keepdims=True) + EPS)


def _row(tm):
    return pl.BlockSpec((tm, D_MODEL), lambda i: (i, 0))


_VEC = pl.BlockSpec((1, D_MODEL), lambda i: (0, 0))


def _pre_norm(x, gn, sc, sh, name):
    t = x.shape[0]
    tm = min(TM_ROW, t)

    def body(x_ref, gn_ref, sc_ref, sh_ref, h_ref):
        xv = x_ref[...]
        h_ref[...] = ((xv * _rstd(xv) * gn_ref[...]) * (1.0 + sc_ref[...]) + sh_ref[...]).astype(BF16)

    return pl.pallas_call(
        body, name=name, out_shape=jax.ShapeDtypeStruct((t, D_MODEL), BF16), grid=(t // tm,),
        in_specs=[_row(tm), _VEC, _VEC, _VEC], out_specs=_row(tm),
        compiler_params=_params(("parallel",)),
    )(x, gn, sc, sh)


def _post_pre(x, y, g, gnp, gn, sc, sh, name):
    t = x.shape[0]
    tm = min(TM_ROW, t)

    def body(x_ref, y_ref, g_ref, gnp_ref, gn_ref, sc_ref, sh_ref, x1_ref, h_ref):
        yv = y_ref[...]
        x1 = x_ref[...] + g_ref[...] * (yv * _rstd(yv) * gnp_ref[...])
        x1_ref[...] = x1
        h_ref[...] = ((x1 * _rstd(x1) * gn_ref[...]) * (1.0 + sc_ref[...]) + sh_ref[...]).astype(BF16)

    return pl.pallas_call(
        body, name=name,
        out_shape=(jax.ShapeDtypeStruct((t, D_MODEL), F32), jax.ShapeDtypeStruct((t, D_MODEL), BF16)),
        grid=(t // tm,),
        in_specs=[_row(tm), _row(tm), _VEC, _VEC, _VEC, _VEC, _VEC], out_specs=(_row(tm), _row(tm)),
        compiler_params=_params(("parallel",)),
    )(x, y, g, gnp, gn, sc, sh)


def _post_loss(x, y, g, gnp, tgt, name):
    t = x.shape[0]
    tm = min(TM_ROW, t)

    def body(x_ref, y_ref, g_ref, gnp_ref, t_ref, dx_ref, loss_ref):
        yv = y_ref[...]
        diff = x_ref[...] + g_ref[...] * (yv * _rstd(yv) * gnp_ref[...]) - t_ref[...]
        dx_ref[...] = diff * (1.0 / D_MODEL)
        part = (0.5 / D_MODEL) * jnp.sum(jnp.sum(diff * diff, axis=-1, keepdims=True), axis=0, keepdims=True)

        @pl.when(pl.program_id(0) == 0)
        def _():
            loss_ref[...] = jnp.zeros_like(loss_ref)

        loss_ref[...] += jnp.broadcast_to(part, loss_ref.shape)

    return pl.pallas_call(
        body, name=name,
        out_shape=(jax.ShapeDtypeStruct((t, D_MODEL), F32), jax.ShapeDtypeStruct((8, 128), F32)),
        grid=(t // tm,),
        in_specs=[_row(tm), _row(tm), _VEC, _VEC, _row(tm)],
        out_specs=(_row(tm), pl.BlockSpec((8, 128), lambda i: (0, 0))),
        compiler_params=_params(("arbitrary",)),
    )(x, y, g, gnp, tgt)


def _acc_rows(ref, val):
    @pl.when(pl.program_id(0) == 0)
    def _():
        ref[...] = jnp.zeros_like(ref)

    ref[...] += jnp.sum(val, axis=0, keepdims=True)


def _post_bwd(dxn, y, g, gnp, name):
    t = y.shape[0]
    tm = min(TM_ROW, t)

    def body(dx_ref, y_ref, g_ref, gnp_ref, dy_ref, dg_ref, dgn_ref):
        yv, dxv = y_ref[...], dx_ref[...]
        r = _rstd(yv)
        yh = yv * r
        _acc_rows(dg_ref, dxv * (yh * gnp_ref[...]))
        dn = dxv * g_ref[...]
        _acc_rows(dgn_ref, dn * yh)
        dyh = dn * gnp_ref[...]
        dy_ref[...] = (r * (dyh - yh * jnp.mean(dyh * yh, axis=-1, keepdims=True))).astype(BF16)

    return pl.pallas_call(
        body, name=name,
        out_shape=(jax.ShapeDtypeStruct((t, D_MODEL), BF16), jax.ShapeDtypeStruct((1, D_MODEL), F32),
                   jax.ShapeDtypeStruct((1, D_MODEL), F32)),
        grid=(t // tm,),
        in_specs=[_row(tm), _row(tm), _VEC, _VEC], out_specs=(_row(tm), _VEC, _VEC),
        compiler_params=_params(("arbitrary",)),
    )(dxn, y, g, gnp)


def _pre_bwd(dh, xin, dres, gn, sc, name):
    t = xin.shape[0]
    tm = min(TM_ROW, t)

    def body(dh_ref, x_ref, dres_ref, gn_ref, sc_ref, dx_ref, dsh_ref, dsc_ref, dgn_ref):
        xv, dhv = x_ref[...], dh_ref[...]
        r = _rstd(xv)
        xh = xv * r
        _acc_rows(dsh_ref, dhv)
        _acc_rows(dsc_ref, dhv * (xh * gn_ref[...]))
        dn = dhv * (1.0 + sc_ref[...])
        _acc_rows(dgn_ref, dn * xh)
        dxh = dn * gn_ref[...]
        dx_ref[...] = dres_ref[...] + r * (dxh - xh * jnp.mean(dxh * xh, axis=-1, keepdims=True))

    vec = jax.ShapeDtypeStruct((1, D_MODEL), F32)
    return pl.pallas_call(
        body, name=name, out_shape=(jax.ShapeDtypeStruct((t, D_MODEL), F32), vec, vec, vec),
        grid=(t // tm,),
        in_specs=[_row(tm), _row(tm), _row(tm), _VEC, _VEC], out_specs=(_row(tm), _VEC, _VEC, _VEC),
        compiler_params=_params(("arbitrary",)),
    )(dh, xin, dres, gn, sc)


def _shift_down(v, halo, s):
    tm = v.shape[0]
    out = pltpu.roll(v, s, 0)
    row = lax.broadcasted_iota(jnp.int32, v.shape, 0)
    for j in range(s):
        out = jnp.where(row == j, jnp.broadcast_to(halo[8 - s + j:8 - s + j + 1, :], v.shape), out)
    return out


def _shift_up(v, halo, s):
    tm = v.shape[0]
    out = pltpu.roll(v, tm - s, 0)
    row = lax.broadcasted_iota(jnp.int32, v.shape, 0)
    for j in range(s):
        out = jnp.where(row == tm - s + j, jnp.broadcast_to(halo[j:j + 1, :], v.shape), out)
    return out


def _tile_specs(tm, cw, off, nrow):
    ob = off // cw
    r8 = tm // 8
    main = pl.BlockSpec((tm, cw), lambda j, i: (i, ob + j))
    prev = pl.BlockSpec((8, cw), lambda j, i: (jnp.maximum(i * r8 - 1, 0), ob + j))
    nxt = pl.BlockSpec((8, cw), lambda j, i: (jnp.minimum((i + 1) * r8, nrow * r8 - 1), ob + j))
    return main, prev, nxt


def _conv_fwd(p, w, name):
    t = p.shape[0]
    tm, cw = min(TM_EW, t), CW_EW
    nrow = t // tm
    cb_s, _, _ = _tile_specs(tm, cw, OFF_CB, nrow)
    cc_s, cc_p, _ = _tile_specs(tm, cw, OFF_CC, nrow)
    cx_s, cx_p, _ = _tile_specs(tm, cw, OFF_CX, nrow)

    def body(cb_ref, cc_ref, ccp_ref, cx_ref, cxp_ref, w_ref, z_ref):
        u = cc_ref[...] * cx_ref[...]
        uh = ccp_ref[...] * cxp_ref[...] * _unless(pl.program_id(1) == 0)
        wv = w_ref[...]
        conv = wv[2:3, :] * u + wv[1:2, :] * _shift_down(u, uh, 1) + wv[0:1, :] * _shift_down(u, uh, 2)
        z_ref[...] = (cb_ref[...] * conv).astype(BF16)

    return pl.pallas_call(
        body, name=name, out_shape=jax.ShapeDtypeStruct((t, CONV_WIDTH), BF16),
        grid=(CONV_WIDTH // cw, nrow),
        in_specs=[cb_s, cc_s, cc_p, cx_s, cx_p, pl.BlockSpec((8, cw), lambda j, i: (0, j))],
        out_specs=pl.BlockSpec((tm, cw), lambda j, i: (i, j)),
        compiler_params=_params(("parallel", "arbitrary")),
    )(p, p, p, p, p, w)


def _acc_w(ref, vals):
    @pl.when(pl.program_id(1) == 0)
    def _():
        ref[...] = jnp.zeros_like(ref)

    for j, v in enumerate(vals):
        ref[j:j + 1, :] += jnp.sum(v, axis=0, keepdims=True)


def _conv_bwd(dz, p, w, name):
    t = p.shape[0]
    tm, cw = min(TM_EW, t), CW_EW
    nrow = t // tm
    dz_s, _, dz_n = _tile_specs(tm, cw, 0, nrow)
    cb_s, _, cb_n = _tile_specs(tm, cw, OFF_CB, nrow)
    cc_s, cc_p, _ = _tile_specs(tm, cw, OFF_CC, nrow)
    cx_s, cx_p, _ = _tile_specs(tm, cw, OFF_CX, nrow)

    def body(dz_ref, dzn_ref, cb_ref, cbn_ref, cc_ref, ccp_ref, cx_ref, cxp_ref, w_ref,
             dcb_ref, dcc_ref, dcx_ref, dw_ref):
        i = pl.program_id(1)
        ccv, cxv, dzv = cc_ref[...], cx_ref[...], dz_ref[...]
        u = ccv * cxv
        uh = ccp_ref[...] * cxp_ref[...] * _unless(i == 0)
        wv = w_ref[...]
        u1, u2 = _shift_down(u, uh, 1), _shift_down(u, uh, 2)
        conv = wv[2:3, :] * u + wv[1:2, :] * u1 + wv[0:1, :] * u2
        dcb_ref[...] = (dzv * conv).astype(BF16)
        dconv = dzv * cb_ref[...]
        dch = dzn_ref[...] * cbn_ref[...] * _unless(i == nrow - 1)
        du = wv[2:3, :] * dconv + wv[1:2, :] * _shift_up(dconv, dch, 1) + wv[0:1, :] * _shift_up(dconv, dch, 2)
        dcc_ref[...] = (du * cxv).astype(BF16)
        dcx_ref[...] = (du * ccv).astype(BF16)
        _acc_w(dw_ref, (dconv * u2, dconv * u1, dconv * u))

    o_s = pl.BlockSpec((tm, cw), lambda j, i: (i, j))
    o_sh = jax.ShapeDtypeStruct((t, CONV_WIDTH), BF16)
    w_s = pl.BlockSpec((8, cw), lambda j, i: (0, j))
    return pl.pallas_call(
        body, name=name, out_shape=(o_sh, o_sh, o_sh, jax.ShapeDtypeStruct((8, CONV_WIDTH), F32)),
        grid=(CONV_WIDTH // cw, nrow),
        in_specs=[dz_s, dz_n, cb_s, cb_n, cc_s, cc_p, cx_s, cx_p, w_s],
        out_specs=(o_s, o_s, o_s, w_s),
        compiler_params=_params(("parallel", "arbitrary")),
    )(dz, dz, p, p, p, p, p, p, w)


def _ffn_fwd(u, w, name):
    t = u.shape[0]
    tm, cw = min(TM_EW, t), CW_EW
    nrow = t // tm
    g_s, g_p, _ = _tile_specs(tm, cw, 0, nrow)
    u_s, _, _ = _tile_specs(tm, cw, D_FF, nrow)

    def body(g_ref, gp_ref, u_ref, w_ref, f_ref):
        gv = g_ref[...]
        gh = gp_ref[...] * _unless(pl.program_id(1) == 0)
        wv = w_ref[...]
        gc = wv[2:3, :] * gv + wv[1:2, :] * _shift_down(gv, gh, 1) + wv[0:1, :] * _shift_down(gv, gh, 2)
        f_ref[...] = (_gelu(gc) * u_ref[...]).astype(BF16)

    return pl.pallas_call(
        body, name=name, out_shape=jax.ShapeDtypeStruct((t, D_FF), BF16),
        grid=(D_FF // cw, nrow),
        in_specs=[g_s, g_p, u_s, pl.BlockSpec((8, cw), lambda j, i: (0, j))],
        out_specs=pl.BlockSpec((tm, cw), lambda j, i: (i, j)),
        compiler_params=_params(("parallel", "arbitrary")),
    )(u, u, u, w)


def _ffn_bwd(df, u, w, name):
    t = u.shape[0]
    tm, cw = min(TM_EW, t), CW_EW
    nrow = t // tm
    df_s, _, df_n = _tile_specs(tm, cw, 0, nrow)
    g_s, g_p, g_n = _tile_specs(tm, cw, 0, nrow)
    u_s, _, u_n = _tile_specs(tm, cw, D_FF, nrow)
    r8 = tm // 8

    def body(df_ref, dfn_ref, g_ref, gp_ref, gn_ref, u_ref, un_ref, w_ref, dg_ref, du_ref, dw_ref):
        i = pl.program_id(1)
        gv, dfv, uv = g_ref[...], df_ref[...], u_ref[...]
        gh = gp_ref[...] * _unless(i == 0)
        wv = w_ref[...]
        g1, g2 = _shift_down(gv, gh, 1), _shift_down(gv, gh, 2)
        gc = wv[2:3, :] * gv + wv[1:2, :] * g1 + wv[0:1, :] * g2
        du_ref[...] = (dfv * _gelu(gc)).astype(BF16)
        dgc = dfv * uv * _gelu_grad(gc)
        gnv = gn_ref[...]
        gtail = gv[tm - 8:tm, :]
        gcn = (wv[2:3, :] * gnv + wv[1:2, :] * _shift_down(gnv, gtail, 1) + wv[0:1, :] * _shift_down(gnv, gtail, 2))
        dgcn = dfn_ref[...] * un_ref[...] * _gelu_grad(gcn) * _unless(i == nrow - 1)
        dg = wv[2:3, :] * dgc + wv[1:2, :] * _shift_up(dgc, dgcn, 1) + wv[0:1, :] * _shift_up(dgc, dgcn, 2)
        dg_ref[...] = dg.astype(BF16)
        _acc_w(dw_ref, (dgc * g2, dgc * g1, dgc * gv))

    o_s = pl.BlockSpec((tm, cw), lambda j, i: (i, j))
    o_sh = jax.ShapeDtypeStruct((t, D_FF), BF16)
    w_s = pl.BlockSpec((8, cw), lambda j, i: (0, j))
    return pl.pallas_call(
        body, name=name, out_shape=(o_sh, o_sh, jax.ShapeDtypeStruct((8, D_FF), F32)),
        grid=(D_FF // cw, nrow),
        in_specs=[df_s, df_n, g_s, g_p, g_n, u_s, u_n, w_s],
        out_specs=(o_s, o_s, w_s),
        compiler_params=_params(("parallel", "arbitrary")),
    )(df, df, u, u, u, u, u, w)


def _merge_fwd(ya, yb, p, name):
    t = ya.shape[0]
    tm, cw = min(TM_EW, t), CW_EW
    y_s = pl.BlockSpec((tm, cw), lambda i, j: (i, j))

    def body(ya_ref, yb_ref, ga_ref, gb_ref, m_ref):
        m_ref[...] = (_sigmoid(ga_ref[...]) * ya_ref[...] + _sigmoid(gb_ref[...]) * yb_ref[...]).astype(BF16)

    return pl.pallas_call(
        body, name=name, out_shape=jax.ShapeDtypeStruct((t, D_MODEL), BF16),
        grid=(t // tm, D_MODEL // cw),
        in_specs=[y_s, y_s, pl.BlockSpec((tm, cw), lambda i, j: (i, OFF_GA // cw + j)),
                  pl.BlockSpec((tm, cw), lambda i, j: (i, OFF_GB // cw + j))],
        out_specs=y_s, compiler_params=_params(("parallel", "parallel")),
    )(ya, yb, p, p)


def _merge_bwd(dm, ya, yb, p, name):
    t = ya.shape[0]
    tm, cw = min(TM_EW, t), CW_EW
    y_s = pl.BlockSpec((tm, cw), lambda i, j: (i, j))

    def body(dm_ref, ya_ref, yb_ref, ga_ref, gb_ref, dya_ref, dyb_ref, dga_ref, dgb_ref):
        dmv = dm_ref[...]
        sa, sb = _sigmoid(ga_ref[...]), _sigmoid(gb_ref[...])
        dya_ref[...] = (dmv * sa).astype(BF16)
        dyb_ref[...] = (dmv * sb).astype(BF16)
        dga_ref[...] = (dmv * ya_ref[...] * sa * (1.0 - sa)).astype(BF16)
        dgb_ref[...] = (dmv * yb_ref[...] * sb * (1.0 - sb)).astype(BF16)

    o_sh = jax.ShapeDtypeStruct((t, D_MODEL), BF16)
    return pl.pallas_call(
        body, name=name, out_shape=(o_sh, o_sh, o_sh, o_sh),
        grid=(t // tm, D_MODEL // cw),
        in_specs=[y_s, y_s, y_s, pl.BlockSpec((tm, cw), lambda i, j: (i, OFF_GA // cw + j)),
                  pl.BlockSpec((tm, cw), lambda i, j: (i, OFF_GB // cw + j))],
        out_specs=(y_s, y_s, y_s, y_s), compiler_params=_params(("parallel", "parallel")),
    )(dm, ya, yb, p, p)


def _tri(lower):
    r = lax.broadcasted_iota(jnp.int32, (CHUNK, CHUNK), 0)
    c = lax.broadcasted_iota(jnp.int32, (CHUNK, CHUNK), 1)
    return ((c <= r) if lower else (c >= r)).astype(F32)


def _eye_mask():
    r = lax.broadcasted_iota(jnp.int32, (GLA_DK, GLA_DK), 0)
    c = lax.broadcasted_iota(jnp.int32, (GLA_DK, GLA_DK), 1)
    return r == c


def _row_to_col(v):
    return jnp.sum(jnp.where(_eye_mask(), jnp.broadcast_to(v, (GLA_DK, GLA_DK)), 0.0), axis=1, keepdims=True)


def _col_to_row(v):
    return jnp.sum(jnp.where(_eye_mask(), jnp.broadcast_to(v, (GLA_DK, GLA_DK)), 0.0), axis=0, keepdims=True)


def _dot(a, b, dn):
    return lax.dot_general(a.astype(BF16), b.astype(BF16), (dn, ((), ())), preferred_element_type=F32)


_NN = ((1,), (0,))
_NT = ((1,), (1,))
_TN = ((0,), (0,))


def _gate_logits(lr_ref, wa_ref, ba_ref):
    return _dot(lr_ref[...], wa_ref[...], _NN) + ba_ref[...]


def _chunk_decay(la, tri):
    cum = lax.dot_general(tri, la, ((_NN), ((), ())), precision=lax.Precision.HIGHEST, preferred_element_type=F32)
    e = cum[CHUNK - 1:CHUNK, :]
    return cum, e, jnp.exp(e - cum)


def _gla_fwd(p, wa, ba, name):
    t = p.shape[0]
    rows = min(GLA_ROWS, t)
    cb = rows // CHUNK
    nc = t // CHUNK
    scale = GLA_DK ** -0.5

    def body(q_ref, k_ref, v_ref, lr_ref, wa_ref, ba_ref, o_ref, st_ref, s_scr):
        @pl.when(pl.program_id(0) == 0)
        def _():
            s_scr[...] = jnp.zeros_like(s_scr)

        la_all = _log_sigmoid(_gate_logits(lr_ref, wa_ref, ba_ref)) * (1.0 / GLA_TAU)
        tri = _tri(True)
        for ch in range(cb):
            rs = slice(ch * CHUNK, (ch + 1) * CHUNK)
            for h in range(GLA_HEADS):
                ks = slice(h * GLA_DK, (h + 1) * GLA_DK)
                vs = slice(h * GLA_DV, (h + 1) * GLA_DV)
                _, e, w = _chunk_decay(la_all[rs, ks], tri)
                kd = k_ref[rs, ks] * w
                s_new = _row_to_col(jnp.exp(e)) * s_scr[ks, :] + _dot(kd, v_ref[rs, vs], _TN)
                s_scr[ks, :] = s_new
                st_ref[ch, ks, :] = s_new
                o_ref[rs, vs] = _dot(q_ref[rs, ks] * scale, s_new, _NN)

    return pl.pallas_call(
        body, name=name,
        out_shape=(jax.ShapeDtypeStruct((t, GLA_V), F32), jax.ShapeDtypeStruct((nc, GLA_QK, GLA_DV), F32)),
        grid=(t // rows,),
        in_specs=[pl.BlockSpec((rows, GLA_QK), lambda i: (i, OFF_Q // GLA_QK)),
                  pl.BlockSpec((rows, GLA_QK), lambda i: (i, OFF_K // GLA_QK)),
                  pl.BlockSpec((rows, GLA_V), lambda i: (i, OFF_V // GLA_V)),
                  pl.BlockSpec((rows, LR_PAD), lambda i: (i, OFF_LR // LR_PAD)),
                  pl.BlockSpec((LR_PAD, GLA_QK), lambda i: (0, 0)),
                  pl.BlockSpec((1, GLA_QK), lambda i: (0, 0))],
        out_specs=(pl.BlockSpec((rows, GLA_V), lambda i: (i, 0)),
                   pl.BlockSpec((cb, GLA_QK, GLA_DV), lambda i: (i, 0, 0))),
        scratch_shapes=[pltpu.VMEM((GLA_QK, GLA_DV), F32)],
        compiler_params=_params(("arbitrary",)),
    )(p, p, p, p, wa, ba)


def _gla_bwd(do, p, st, wa, ba, name):
    t = p.shape[0]
    rows = min(GLA_ROWS, t)
    cb = rows // CHUNK
    nb = t // rows
    scale = GLA_DK ** -0.5

    def rev(i):
        return nb - 1 - i

    def body(do_ref, q_ref, k_ref, v_ref, lr_ref, st_ref, stp_ref, wa_ref, ba_ref,
             dq_ref, dk_ref, dv_ref, dlr_ref, dwa_ref, dba_ref, ds_scr, dz_scr):
        i = pl.program_id(0)

        @pl.when(i == 0)
        def _():
            ds_scr[...] = jnp.zeros_like(ds_scr)
            dwa_ref[...] = jnp.zeros_like(dwa_ref)
            dba_ref[...] = jnp.zeros_like(dba_ref)

        z_all = _gate_logits(lr_ref, wa_ref, ba_ref)
        la_all = _log_sigmoid(z_all) * (1.0 / GLA_TAU)
        tri, triu = _tri(True), _tri(False)
        last_row = lax.broadcasted_iota(jnp.int32, (CHUNK, GLA_DK), 0) == CHUNK - 1
        keep_prev = _unless(i == nb - 1)
        for ch in reversed(range(cb)):
            rs = slice(ch * CHUNK, (ch + 1) * CHUNK)
            for h in range(GLA_HEADS):
                ks = slice(h * GLA_DK, (h + 1) * GLA_DK)
                vs = slice(h * GLA_DV, (h + 1) * GLA_DV)
                _, e, w = _chunk_decay(la_all[rs, ks], tri)
                kd = k_ref[rs, ks] * w
                exp_e = jnp.exp(e)
                s_c = st_ref[ch, ks, :]
                if ch > 0:
                    s_p = st_ref[ch - 1, ks, :]
                else:
                    s_p = stp_ref[0, ks, :] * keep_prev
                do_c = do_ref[rs, vs]
                vv = v_ref[rs, vs]
                ds_tot = ds_scr[ks, :] + _dot(q_ref[rs, ks] * scale, do_c, _TN)
                dq_ref[rs, ks] = (_dot(do_c, s_c, _NT) * scale).astype(BF16)
                dkd = _dot(vv, ds_tot, _NT)
                dv_ref[rs, vs] = _dot(kd, ds_tot, _NN).astype(BF16)
                dexp_col = jnp.sum(ds_tot * s_p, axis=1, keepdims=True)
                ds_scr[ks, :] = _row_to_col(exp_e) * ds_tot
                dk_ref[rs, ks] = (dkd * w).astype(BF16)
                dwt = dkd * kd
                de = jnp.sum(dwt, axis=0, keepdims=True) + _col_to_row(dexp_col) * exp_e
                dcum = jnp.where(last_row, de - dwt, -dwt)
                da = lax.dot_general(triu, dcum, (_NN, ((), ())), precision=lax.Precision.HIGHEST,
                                     preferred_element_type=F32)
                dz_scr[rs, ks] = da * (1.0 / GLA_TAU) * _sigmoid(-z_all[rs, ks])
        dz = dz_scr[...]
        dlr_ref[...] = _dot(dz, wa_ref[...], _NT).astype(BF16)
        dwa_ref[...] += _dot(lr_ref[...], dz, _TN)
        dba_ref[...] += jnp.sum(dz, axis=0, keepdims=True)

    qk_sh = jax.ShapeDtypeStruct((t, GLA_QK), BF16)
    return pl.pallas_call(
        body, name=name,
        out_shape=(qk_sh, qk_sh, jax.ShapeDtypeStruct((t, GLA_V), BF16), jax.ShapeDtypeStruct((t, LR_PAD), BF16),
                   jax.ShapeDtypeStruct((LR_PAD, GLA_QK), F32), jax.ShapeDtypeStruct((1, GLA_QK), F32)),
        grid=(nb,),
        in_specs=[pl.BlockSpec((rows, GLA_V), lambda i: (rev(i), 0)),
                  pl.BlockSpec((rows, GLA_QK), lambda i: (rev(i), OFF_Q // GLA_QK)),
                  pl.BlockSpec((rows, GLA_QK), lambda i: (rev(i), OFF_K // GLA_QK)),
                  pl.BlockSpec((rows, GLA_V), lambda i: (rev(i), OFF_V // GLA_V)),
                  pl.BlockSpec((rows, LR_PAD), lambda i: (rev(i), OFF_LR // LR_PAD)),
                  pl.BlockSpec((cb, GLA_QK, GLA_DV), lambda i: (rev(i), 0, 0)),
                  pl.BlockSpec((1, GLA_QK, GLA_DV), lambda i: (jnp.maximum(rev(i) * cb - 1, 0), 0, 0)),
                  pl.BlockSpec((LR_PAD, GLA_QK), lambda i: (0, 0)),
                  pl.BlockSpec((1, GLA_QK), lambda i: (0, 0))],
        out_specs=(pl.BlockSpec((rows, GLA_QK), lambda i: (rev(i), 0)),
                   pl.BlockSpec((rows, GLA_QK), lambda i: (rev(i), 0)),
                   pl.BlockSpec((rows, GLA_V), lambda i: (rev(i), 0)),
                   pl.BlockSpec((rows, LR_PAD), lambda i: (rev(i), 0)),
                   pl.BlockSpec((LR_PAD, GLA_QK), lambda i: (0, 0)),
                   pl.BlockSpec((1, GLA_QK), lambda i: (0, 0))),
        scratch_shapes=[pltpu.VMEM((GLA_QK, GLA_DV), F32), pltpu.VMEM((rows, GLA_QK), F32)],
        compiler_params=_params(("arbitrary",)),
    )(do, p, p, p, p, st, st, wa, ba)


def _gla_out_fwd(o, p, gng, name):
    t = o.shape[0]
    tm = min(TM_EW, t)

    def body(o_ref, r_ref, g_ref, z_ref):
        gv = g_ref[...]
        for h in range(GLA_HEADS):
            vs = slice(h * GLA_DV, (h + 1) * GLA_DV)
            ov, rv = o_ref[:, vs], r_ref[:, vs]
            z_ref[:, vs] = ((ov * _rstd(ov) * gv) * (rv * _sigmoid(rv))).astype(BF16)

    return pl.pallas_call(
        body, name=name, out_shape=jax.ShapeDtypeStruct((t, GLA_V), BF16), grid=(t // tm,),
        in_specs=[pl.BlockSpec((tm, GLA_V), lambda i: (i, 0)),
                  pl.BlockSpec((tm, GLA_V), lambda i: (i, OFF_R // GLA_V)),
                  pl.BlockSpec((1, GLA_DV), lambda i: (0, 0))],
        out_specs=pl.BlockSpec((tm, GLA_V), lambda i: (i, 0)),
        compiler_params=_params(("parallel",)),
    )(o, p, gng)


def _gla_out_bwd(dz, o, p, gng, name):
    t = o.shape[0]
    tm = min(TM_EW, t)

    def body(dz_ref, o_ref, r_ref, g_ref, do_ref, dr_ref, dg_ref):
        @pl.when(pl.program_id(0) == 0)
        def _():
            dg_ref[...] = jnp.zeros_like(dg_ref)

        gv = g_ref[...]
        for h in range(GLA_HEADS):
            vs = slice(h * GLA_DV, (h + 1) * GLA_DV)
            ov, rv, dzv = o_ref[:, vs], r_ref[:, vs], dz_ref[:, vs]
            rs = _rstd(ov)
            oh = ov * rs
            sg = _sigmoid(rv)
            dr_ref[:, vs] = (dzv * (oh * gv) * (sg * (1.0 + rv * (1.0 - sg)))).astype(BF16)
            don = dzv * (rv * sg)
            dg_ref[...] += jnp.sum(don * oh, axis=0, keepdims=True)
            doh = don * gv
            do_ref[:, vs] = rs * (doh - oh * jnp.mean(doh * oh, axis=-1, keepdims=True))

    row = pl.BlockSpec((tm, GLA_V), lambda i: (i, 0))
    return pl.pallas_call(
        body, name=name,
        out_shape=(jax.ShapeDtypeStruct((t, GLA_V), F32), jax.ShapeDtypeStruct((t, GLA_V), BF16),
                   jax.ShapeDtypeStruct((1, GLA_DV), F32)),
        grid=(t // tm,),
        in_specs=[row, row, pl.BlockSpec((tm, GLA_V), lambda i: (i, OFF_R // GLA_V)),
                  pl.BlockSpec((1, GLA_DV), lambda i: (0, 0))],
        out_specs=(row, row, pl.BlockSpec((1, GLA_DV), lambda i: (0, 0))),
        compiler_params=_params(("arbitrary",)),
    )(dz, o, p, gng)


def _ada_fwd(c_all, w, b, layer, name):
    n = w.shape[2]
    tn = _pick(n, 512)

    def body(c_ref, w_ref, b_ref, o_ref):
        cv = c_ref[...]
        o_ref[...] = _dot(cv * _sigmoid(cv), w_ref[...], _NN) + b_ref[...]

    return pl.pallas_call(
        body, name=name, out_shape=jax.ShapeDtypeStruct((16, n), F32), grid=(n // tn,),
        in_specs=[pl.BlockSpec((16, D_MODEL), lambda j: (0, 0)),
                  pl.BlockSpec((None, D_MODEL, tn), lambda j: (layer, 0, j)),
                  pl.BlockSpec((1, tn), lambda j: (0, j))],
        out_specs=pl.BlockSpec((16, tn), lambda j: (0, j)),
        compiler_params=_params(("parallel",)),
    )(c_all, w, b)


def _ada_bwd(c_all, dmod, name):
    n = dmod.shape[1]
    tn = _pick(n, 512)

    def body(c_ref, d_ref, o_ref):
        cv = c_ref[...]
        o_ref[...] = _dot(cv * _sigmoid(cv), d_ref[...], _TN)

    return pl.pallas_call(
        body, name=name, out_shape=jax.ShapeDtypeStruct((D_MODEL, n), F32), grid=(n // tn,),
        in_specs=[pl.BlockSpec((16, D_MODEL), lambda j: (0, 0)), pl.BlockSpec((16, tn), lambda j: (0, j))],
        out_specs=pl.BlockSpec((D_MODEL, tn), lambda j: (0, j)),
        compiler_params=_params(("parallel",)),
    )(c_all, dmod)


def _rows_tile(nrows, ncols, target_bytes):
    want = max(16, target_bytes // (4 * ncols))
    if nrows <= want:
        return nrows
    t = (want // 16) * 16
    while t >= 16:
        if nrows % t == 0:
            return t
        t -= 16
    return nrows


def _sum_chips(sent, landed, chip, name):
    _, nrows, ncols = sent[0].shape
    tr = _rows_tile(nrows, ncols, 2 << 20)
    nblk = nrows // tr

    def body(chip_ref, *refs):
        own, got, o_ref = refs[:DEPTH], refs[DEPTH:2 * DEPTH], refs[2 * DEPTH]
        me = chip_ref[0]
        for l in range(DEPTH):
            for j in range(N_CHIPS):
                def add(val):
                    if j == 0:
                        o_ref[...] = val.astype(F32)
                    else:
                        o_ref[...] += val.astype(F32)

                @pl.when(jnp.logical_and(pl.program_id(0) == l, me == j))
                def _():
                    add(own[l][...])

                @pl.when(jnp.logical_and(pl.program_id(0) == l, me != j))
                def _():
                    add(got[l][j])

    def rows_of(layer):
        return lambda l, i, chip_ref: jnp.where(l == layer, i, 0)

    own_specs = [pl.BlockSpec((None, tr, ncols), lambda l, i, chip_ref, r=rows_of(k): (chip_ref[0], r(l, i, chip_ref), 0))
                 for k in range(DEPTH)]
    got_specs = [pl.BlockSpec((N_CHIPS, tr, ncols), lambda l, i, chip_ref, r=rows_of(k): (0, r(l, i, chip_ref), 0))
                 for k in range(DEPTH)]
    return pl.pallas_call(
        body, name=name, out_shape=jax.ShapeDtypeStruct((DEPTH * nrows, ncols), F32),
        grid_spec=pltpu.PrefetchScalarGridSpec(
            num_scalar_prefetch=1, grid=(DEPTH, nblk), in_specs=own_specs + got_specs,
            out_specs=pl.BlockSpec((tr, ncols), lambda l, i, chip_ref: (l * nblk + i, 0))),
        compiler_params=_params(("arbitrary", "arbitrary")),
    )(chip, *sent, *landed)


def _adamw(w, m, v, ga, gb, name):
    nrows, ncols = w.shape
    tr = _rows_tile(nrows, ncols, 1 << 20)
    two = gb is not None
    c1 = 1.0 - ADAM_B1 ** ADAM_STEP
    c2 = 1.0 - ADAM_B2 ** ADAM_STEP

    def body(*refs):
        if two:
            w_ref, m_ref, v_ref, ga_ref, gb_ref, g_ref, d_ref, nm_ref, nv_ref = refs
            g = ga_ref[...] + gb_ref[...]
        else:
            w_ref, m_ref, v_ref, ga_ref, g_ref, d_ref, nm_ref, nv_ref = refs
            g = ga_ref[...]
        g_ref[...] = g
        nm = ADAM_B1 * m_ref[...] + (1.0 - ADAM_B1) * g
        nv = ADAM_B2 * v_ref[...] + (1.0 - ADAM_B2) * (g * g)
        nm_ref[...] = nm
        nv_ref[...] = nv
        d_ref[...] = -ADAM_LR * ((nm / c1) / (jnp.sqrt(nv / c2) + ADAM_EPS) + ADAM_WD * w_ref[...])

    blk = pl.BlockSpec((tr, ncols), lambda i: (i, 0))
    sh = jax.ShapeDtypeStruct((nrows, ncols), F32)
    ins = [w, m, v, ga] + ([gb] if two else [])
    return pl.pallas_call(
        body, name=name, out_shape=(sh, sh, sh, sh), grid=(nrows // tr,),
        in_specs=[blk] * len(ins), out_specs=(blk, blk, blk, blk),
        compiler_params=_params(("parallel",)),
    )(*ins)


def _pad_rows(a, rows):
    return jnp.concatenate([a, jnp.zeros((rows - a.shape[0],) + a.shape[1:], a.dtype)], axis=0)


def _cols_from_chips(a):
    return jnp.transpose(a, (1, 0, 2)).reshape(a.shape[1], N_CHIPS * a.shape[2])


def _cols_to_chips(a):
    rows, n = a.shape
    return jnp.transpose(a.reshape(rows, N_CHIPS, n // N_CHIPS), (1, 0, 2))


def _pad_w_in(w):
    return jnp.concatenate([w[:, :3072], w[:, 3088:], w[:, 3072:3088],
                            jnp.zeros((w.shape[0], LR_PAD - GLA_LOWRANK), w.dtype)], axis=1)


def _unpad_w_in(w):
    return jnp.concatenate([w[:, :3072], w[:, OFF_LR:OFF_LR + GLA_LOWRANK], w[:, 3072:OFF_LR]], axis=1)


_BIG = ("w_in", "w_og", "w_oc", "w_o", "w_up", "w_dn")
_ROW_SHARDED = ("w_o", "w_dn")


def kernel(x, c, w_ada, b_ada, norm_g, w_in, w_a2, b_a2, gla_norm_g, w_out_gla, conv_mix_w, w_out_conv, w_o, w_up, ffn_conv_w, w_down, loss_target, m_w_ada, m_b_ada, m_norm_g, m_w_in, m_w_a2, m_b_a2, m_gla_norm_g, m_w_out_gla, m_conv_mix_w, m_w_out_conv, m_w_o, m_w_up, m_ffn_conv_w, m_w_down, v_w_ada, v_b_ada, v_norm_g, v_w_in, v_w_a2, v_b_a2, v_gla_norm_g, v_w_out_gla, v_conv_mix_w, v_w_out_conv, v_w_o, v_w_up, v_ffn_conv_w, v_w_down):
    xi, yi, ci = lax.axis_index("x"), lax.axis_index("y"), lax.axis_index("c")
    chip = 2 * xi + yi
    dev = 2 * chip + ci
    chip_arr = jnp.reshape(chip, (1,)).astype(jnp.int32)
    xt = x[0]
    tgt = loss_target[0]

    c_all = _allgather8(jnp.broadcast_to(c, (8, D_MODEL)), "gather_c")[0][:, 0, :]
    c16 = _pad_rows(c_all, 16)
    sm_parts = [norm_g.reshape(-1), w_a2.reshape(-1), conv_mix_w.reshape(-1), ffn_conv_w.reshape(-1)]
    sm_sizes = [a.shape[0] for a in sm_parts]
    sm_flat = jnp.concatenate(sm_parts)
    sm_rows = -(-sm_flat.shape[0] // 128)
    sm_rows = -(-sm_rows // 8) * 8
    sm_flat = jnp.concatenate([sm_flat, jnp.zeros((sm_rows * 128 - sm_flat.shape[0],), F32)]).reshape(sm_rows, 128)
    sm_all = _allgather8(sm_flat, "gather_small")[0].reshape(N_DEV, -1)[0::2]
    offs = [0]
    for s in sm_sizes:
        offs.append(offs[-1] + s)

    def small_full(idx, shape):
        a = sm_all[:, offs[idx]:offs[idx + 1]].reshape((N_CHIPS,) + shape)
        a = jnp.moveaxis(a, 0, -2)
        return a.reshape(shape[:-1] + (N_CHIPS * shape[-1],))

    norm_g_f = small_full(0, (DEPTH, 4, 512))
    w_a2_f = small_full(1, (DEPTH, GLA_LOWRANK, 128))
    conv_w_f = small_full(2, (DEPTH, 3, 256))
    ffn_w_f = small_full(3, (DEPTH, 3, 1408))

    b_loc = lax.dynamic_slice(b_ada, (0, chip * 3072), (DEPTH, 3072))
    mod_loc = jnp.concatenate(
        [_ada_fwd(c16, w_ada, b_loc[l:l + 1], l, "ada_fwd")[:8] for l in range(DEPTH)], axis=0)
    mod_all = _allgather8(mod_loc, "gather_mod")[0][0::2]
    mods = []
    for l in range(DEPTH):
        row = lax.dynamic_slice(mod_all, (0, l * 8 + dev, 0), (N_CHIPS, 1, 3072)).reshape(1, 6 * D_MODEL)
        mods.append([row[:, k * D_MODEL:(k + 1) * D_MODEL] for k in range(6)])

    big = dict(w_in=w_in, w_og=w_out_gla, w_oc=w_out_conv, w_o=w_o, w_up=w_up, w_dn=w_down)
    gathers = {}
    tok = 0.0 * mod_all[0, 0, 0]
    for l in range(DEPTH):
        for k in _BIG:
            shard = (big[k][l] + tok).astype(BF16)
            *handle, token = _xchg_start(shard, False, "gather_start_%s_%d" % (k, l))
            gathers[k, l] = (shard, tuple(handle))
            tok = token[0, 0]

    def gathered(k, l, after):
        shard, handle = gathers[k, l]
        _, land = _xchg_wait(handle, after, False, "gather_wait_%s_%d" % (k, l))
        slot = lax.broadcasted_iota(jnp.int32, (N_CHIPS, 1, 1), 0)
        full = jnp.where(slot == chip, shard[None], land)
        if k in _ROW_SHARDED:
            return full.reshape(N_CHIPS * full.shape[1], full.shape[2])
        return _cols_from_chips(full)

    saved = []
    h = None
    xin = xt
    for l in range(DEPTH):
        sh1, sc1, g1, sh2, sc2, g2 = mods[l]
        gn = [norm_g_f[l, k][None] for k in range(4)]
        wa = _pad_rows(w_a2_f[l], LR_PAD)
        ba = b_a2[l][None]
        gng = gla_norm_g[l][None]
        cw8 = _pad_rows(conv_w_f[l], 8)
        fw8 = _pad_rows(ffn_w_f[l], 8)
        if l == 0:
            h = _pre_norm(xin, gn[0] + tok, sc1, sh1, "pre_norm")
        wi = _pad_w_in(gathered("w_in", l, h))
        p = _matmul(h, wi, "nn", F32, "mm_in", tn=1152)
        o, st = _gla_fwd(p, wa, ba, "gla_fwd")
        za = _gla_out_fwd(o, p, gng, "gla_out_fwd")
        zb = _conv_fwd(p, cw8, "conv_fwd")
        wog, woc = gathered("w_og", l, zb), gathered("w_oc", l, zb)
        ya = _matmul(za, wog, "nn", F32, "mm_out_gla")
        yb = _matmul(zb, woc, "nn", F32, "mm_out_conv")
        mm = _merge_fwd(ya, yb, p, "merge_fwd")
        wo = gathered("w_o", l, mm)
        y = _matmul(mm, wo, "nn", F32, "mm_o")
        x1, h2 = _post_pre(xin, y, g1, gn[1], gn[2], sc2, sh2, "post_pre")
        wup = gathered("w_up", l, h2)
        u = _matmul(h2, wup, "nn", F32, "mm_up")
        f = _ffn_fwd(u, fw8, "ffn_fwd")
        wdn = gathered("w_dn", l, f)
        y2 = _matmul(f, wdn, "nn", F32, "mm_down")
        saved.append(dict(xin=xin, h=h, p=p, o=o, st=st, za=za, zb=zb, ya=ya, yb=yb, mm=mm, y=y, x1=x1, h2=h2,
                          u=u, f=f, y2=y2, wi=wi, wog=wog, woc=woc, wo=wo, wup=wup, wdn=wdn, wa=wa, ba=ba,
                          gng=gng, cw8=cw8, fw8=fw8, gn=gn, mod=mods[l]))
        if l + 1 < DEPTH:
            nsh1, nsc1 = mods[l + 1][0], mods[l + 1][1]
            xin, h = _post_pre(x1, y2, g2, gn[3], norm_g_f[l + 1, 0][None], nsc1, nsh1, "post_pre")
        else:
            dx, loss_tile = _post_loss(x1, y2, g2, gn[3], tgt, "post_loss")
    loss = lax.psum(loss_tile[0, 0], ("x", "y", "c"))

    scatters = {}

    def scatter(k, l, dw):
        send = dw.reshape(N_CHIPS, dw.shape[0] // N_CHIPS, dw.shape[1]) if k in _ROW_SHARDED else _cols_to_chips(dw)
        *handle, token = _xchg_start(send, True, "scatter_start_%s_%d" % (k, l))
        scatters[k, l] = tuple(handle)
        return token[0, 0]

    sm = {k: [None] * DEPTH for k in ("dmod", "norm_g", "w_a2", "b_a2", "gng", "conv_w", "ffn_w")}
    for l in reversed(range(DEPTH)):
        s = saved[l]
        sh1, sc1, g1, sh2, sc2, g2 = s["mod"]
        gn = s["gn"]
        dy2, dg2, dgn3 = _post_bwd(dx, s["y2"], g2, gn[3], "post_bwd")
        tk = scatter("w_dn", l, _matmul(s["f"], dy2, "tn", BF16, "mm_down_dw", tm=512, tn=1024, tk=2048))
        df = _matmul(dy2, s["wdn"], "nt", F32, "mm_down_dx")
        dgate, dup, dfw = _ffn_bwd(df, s["u"], s["fw8"] + tk, "ffn_bwd")
        du = jnp.concatenate([dgate, dup], axis=1)
        tk = scatter("w_up", l, _matmul(s["h2"], du, "tn", BF16, "mm_up_dw", tm=512, tn=1024, tk=2048))
        dh2 = _matmul(du, s["wup"], "nt", F32, "mm_up_dx", tn=1024, tk=2816)
        dx1, dsh2, dsc2, dgn2 = _pre_bwd(dh2, s["x1"], dx, gn[2] + tk, sc2, "pre_bwd")
        dy, dg1, dgn1 = _post_bwd(dx1, s["y"], g1, gn[1], "post_bwd")
        tk = scatter("w_o", l, _matmul(s["mm"], dy, "tn", BF16, "mm_o_dw"))
        dm = _matmul(dy, s["wo"], "nt", F32, "mm_o_dx")
        dya, dyb, dga, dgb = _merge_bwd(dm, s["ya"], s["yb"], s["p"], "merge_bwd")
        tk = tk + scatter("w_og", l, _matmul(s["za"], dya, "tn", BF16, "mm_out_gla_dw"))
        dza = _matmul(dya, s["wog"], "nt", F32, "mm_out_gla_dx")
        do, dr, dgng = _gla_out_bwd(dza, s["o"], s["p"], s["gng"] + tk, "gla_out_bwd")
        tk = scatter("w_oc", l, _matmul(s["zb"], dyb, "tn", BF16, "mm_out_conv_dw"))
        dzb = _matmul(dyb, s["woc"], "nt", F32, "mm_out_conv_dx")
        dcb, dcc, dcx, dcw = _conv_bwd(dzb, s["p"], s["cw8"] + tk, "conv_bwd")
        dq, dk, dv, dlr, dwa, dba = _gla_bwd(do, s["p"], s["st"], s["wa"], s["ba"], "gla_bwd")
        dp = jnp.concatenate([dq, dk, dv, dr, dcb, dcc, dcx, dga, dgb, dlr], axis=1)
        dw_in = _unpad_w_in(_matmul(s["h"], dp, "tn", BF16, "mm_in_dw", tm=512, tn=1152, tk=2048))
        tk = scatter("w_in", l, dw_in) if l > 0 else 0.0
        dh =_matmul(dp, s["wi"], "nt", F32, "mm_in_dx", tn=1024, tk=1152)
        dx, dsh1, dsc1, dgn0 = _pre_bwd(dh, s["xin"], dx1, gn[0] + tk, sc1, "pre_bwd")
        sm["dmod"][l] = jnp.concatenate([dsh1, dsc1, dg1, dsh2, dsc2, dg2], axis=1)[0]
        sm["norm_g"][l] = jnp.concatenate([dgn0, dgn1, dgn2, dgn3], axis=0)
        sm["w_a2"][l] = dwa[:GLA_LOWRANK]
        sm["b_a2"][l] = dba[0]
        sm["gng"][l] = dgng[0]
        sm["conv_w"][l] = dcw[:3]
        sm["ffn_w"][l] = dfw[:3]
    grad_x = dx[None]

    names = ("dmod", "norm_g", "w_a2", "b_a2", "gng", "conv_w", "ffn_w")
    parts = [jnp.stack(sm[k]).reshape(-1) for k in names]
    shapes = [jnp.stack(sm[k]).shape for k in names]
    sizes = [a.shape[0] for a in parts]
    flat = jnp.concatenate(parts)
    rows = -(-flat.shape[0] // 1024) * 8
    flat = jnp.concatenate([flat, jnp.zeros((rows * 128 - flat.shape[0],), F32)]).reshape(rows, 128)
    gath, tot = _allgather8(flat, "reduce_small")
    tk = scatter("w_in", 0, dw_in + (0.0 * tot[0, 0]).astype(BF16))
    c16 = c16 + tk
    po = [0]
    for s_ in sizes:
        po.append(po[-1] + s_)
    tot = tot.reshape(-1)
    tot_of = {k: tot[po[i]:po[i + 1]].reshape(shapes[i]) for i, k in enumerate(names)}
    dmod_all = gath.reshape(N_DEV, -1)[:, po[0]:po[1]].reshape(N_DEV, DEPTH, 6 * D_MODEL)

    def chip_cols(a, width):
        return lax.dynamic_slice_in_dim(a, chip * width, width, axis=a.ndim - 1)

    g_w_ada = []
    for l in range(DEPTH):
        dml = _pad_rows(chip_cols(dmod_all[:, l], 3072), 16)
        g_w_ada.append(_ada_bwd(c16, dml, "ada_bwd"))
    g_w_ada = jnp.stack(g_w_ada)

    def upd(w, m, v, ga, gb, name):
        sh = w.shape
        two_d = (-1, sh[-1])
        outs = _adamw(w.reshape(two_d), m.reshape(two_d), v.reshape(two_d), ga.reshape(two_d),
                      None if gb is None else gb.reshape(two_d), name)
        return [a.reshape(sh) for a in outs]

    res = {}
    res["w_ada"] = upd(w_ada, m_w_ada, v_w_ada, g_w_ada, None, "adamw")
    res["b_ada"] = upd(b_ada, m_b_ada, v_b_ada, tot_of["dmod"], None, "adamw")
    res["norm_g"] = upd(norm_g, m_norm_g, v_norm_g, chip_cols(tot_of["norm_g"], 512), None, "adamw")
    res["w_a2"] = upd(w_a2, m_w_a2, v_w_a2, chip_cols(tot_of["w_a2"], 128), None, "adamw")
    res["b_a2"] = upd(b_a2, m_b_a2, v_b_a2, tot_of["b_a2"], None, "adamw")
    res["gla_norm_g"] = upd(gla_norm_g, m_gla_norm_g, v_gla_norm_g, tot_of["gng"], None, "adamw")
    res["conv_mix_w"] = upd(conv_mix_w, m_conv_mix_w, v_conv_mix_w, chip_cols(tot_of["conv_w"], 256), None, "adamw")
    res["ffn_conv_w"] = upd(ffn_conv_w, m_ffn_conv_w, v_ffn_conv_w, chip_cols(tot_of["ffn_w"], 1408), None, "adamw")

    full_name = dict(w_in="w_in", w_og="w_out_gla", w_oc="w_out_conv", w_o="w_o", w_up="w_up", w_dn="w_down")
    state = dict(w_in=(w_in, m_w_in, v_w_in), w_og=(w_out_gla, m_w_out_gla, v_w_out_gla),
                 w_oc=(w_out_conv, m_w_out_conv, v_w_out_conv), w_o=(w_o, m_w_o, v_w_o),
                 w_up=(w_up, m_w_up, v_w_up), w_dn=(w_down, m_w_down, v_w_down))
    after = res["w_ada"][3]
    for k in ("w_dn", "w_up", "w_o", "w_og", "w_oc", "w_in"):
        done = [_xchg_wait(scatters[k, l], after, True, "scatter_wait_%s_%d" % (k, l)) for l in range(DEPTH)]
        plane = _sum_chips([d[0] for d in done], [d[1] for d in done], chip_arr, "sum_chips")
        other = _sibling_exchange([plane], "sibling_" + k)[0]
        res[full_name[k]] = upd(*state[k], plane, other, "adamw")
        after = res[full_name[k]][3]
    order = ("w_ada", "b_ada", "norm_g", "w_in", "w_a2", "b_a2", "gla_norm_g", "w_out_gla", "conv_mix_w",
             "w_out_conv", "w_o", "w_up", "ffn_conv_w", "w_down")
    return (loss, grad_x, *[res[k][0] for k in order], *[res[k][1] for k in order],
            *[res[k][2] for k in order], *[res[k][3] for k in order])
```

```python
import functools
import math

import jax
import jax.numpy as jnp
from jax import lax
from jax.experimental import pallas as pl
from jax.experimental.pallas import tpu as pltpu

F32 = jnp.float32
BF16 = jnp.bfloat16
MESH = pl.DeviceIdType.MESH

D_MODEL = 2048
DEPTH = 2
CHUNK = 64
GLA_HEADS = 4
GLA_DK = 128
GLA_DV = 256
GLA_QK = GLA_HEADS * GLA_DK
GLA_V = GLA_HEADS * GLA_DV
GLA_LOWRANK = 16
GLA_TAU = 16.0
CONV_WIDTH = 1024
D_FF = 5632
EPS = 1e-6
N_IN = 10256
LR_PAD = 128
N_IN_PAD = N_IN - GLA_LOWRANK + LR_PAD
OFF_Q, OFF_K, OFF_V, OFF_R = 0, 512, 1024, 2048
OFF_CB, OFF_CC, OFF_CX, OFF_GA, OFF_GB, OFF_LR = 3072, 4096, 5120, 6144, 8192, 10240

ADAM_LR = 0.001
ADAM_B1 = 0.9
ADAM_B2 = 0.999
ADAM_EPS = 1e-08
ADAM_WD = 0.01
ADAM_STEP = 10

N_CHIPS = 4
N_DEV = 8
VMEM_LIMIT = 56 * 1024 * 1024
TM_ROW = 256
TM_EW = 512
CW_EW = 512
GLA_ROWS = 256


def _params(sem=None):
    return pltpu.CompilerParams(dimension_semantics=sem, vmem_limit_bytes=VMEM_LIMIT)


def _sigmoid(v):
    return 1.0 / (1.0 + jnp.exp(-v))


def _log_sigmoid(v):
    return jnp.minimum(v, 0.0) - jnp.log(1.0 + jnp.exp(-jnp.abs(v)))


_GELU_C = math.sqrt(2.0 / math.pi)


def _gelu(v):
    return 0.5 * v * (1.0 + jnp.tanh(_GELU_C * (v + 0.044715 * v * v * v)))


def _gelu_grad(v):
    t = jnp.tanh(_GELU_C * (v + 0.044715 * v * v * v))
    return 0.5 * (1.0 + t) + 0.5 * v * (1.0 - t * t) * _GELU_C * (1.0 + 3.0 * 0.044715 * v * v)


def _flip(a, d):
    return a + d - 2 * a * d


def _unless(cond):
    return jnp.where(cond, 0.0, 1.0).astype(F32)


def _allgather8(xv, name):
    r, cdim = xv.shape

    def body(x_ref, out_ref, sum_ref, send_sems, recv_sems):
        xi, yi, ci = lax.axis_index("x"), lax.axis_index("y"), lax.axis_index("c")
        me = 4 * xi + 2 * yi + ci
        out_ref[pl.ds(me, 1)] = x_ref[...][None]
        started = []
        for k in range(1, N_DEV):
            px, py, pc = _flip(xi, (k >> 2) & 1), _flip(yi, (k >> 1) & 1), _flip(ci, k & 1)
            cp = pltpu.make_async_remote_copy(
                src_ref=x_ref, dst_ref=out_ref.at[me], send_sem=send_sems.at[k - 1], recv_sem=recv_sems.at[k - 1],
                device_id=(px, py, pc), device_id_type=MESH)
            cp.start()
            started.append((cp, 4 * px + 2 * py + pc, k, (px, py, pc)))
        for cp, peer, k, pid in started:
            cp.wait_send()
            pltpu.make_async_remote_copy(
                src_ref=x_ref, dst_ref=out_ref.at[peer], send_sem=send_sems.at[k - 1], recv_sem=recv_sems.at[k - 1],
                device_id=pid, device_id_type=MESH).wait_recv()
        acc = out_ref[0]
        for d in range(1, N_DEV):
            acc = acc + out_ref[d]
        sum_ref[...] = acc

    return pl.pallas_call(
        body, name=name,
        out_shape=(jax.ShapeDtypeStruct((N_DEV, r, cdim), F32), jax.ShapeDtypeStruct((r, cdim), F32)),
        in_specs=[pl.BlockSpec(memory_space=pltpu.VMEM)],
        out_specs=(pl.BlockSpec(memory_space=pltpu.VMEM), pl.BlockSpec(memory_space=pltpu.VMEM)),
        scratch_shapes=[pltpu.SemaphoreType.DMA((N_DEV - 1,)), pltpu.SemaphoreType.DMA((N_DEV - 1,))],
        compiler_params=pltpu.CompilerParams(vmem_limit_bytes=VMEM_LIMIT),
    )(xv)


_HBM = pl.BlockSpec(memory_space=pltpu.HBM)
_SEM = pl.BlockSpec(memory_space=pltpu.SEMAPHORE)
_EFFECT = pltpu.SideEffectType.DATAFLOW_SIDE_EFFECTING
_CHIP_FLIPS = ((1, 0), (0, 1), (1, 1))


def _chip_copies(src_ref, land_ref, send_sems, recv_sems, scatter):
    xi, yi, ci = lax.axis_index("x"), lax.axis_index("y"), lax.axis_index("c")
    me = 2 * xi + yi
    out = []
    for k, (dx, dy) in enumerate(_CHIP_FLIPS):
        px, py = _flip(xi, dx), _flip(yi, dy)
        peer = 2 * px + py
        src = src_ref.at[peer] if scatter else src_ref
        mk = functools.partial(pltpu.make_async_remote_copy, src_ref=src, send_sem=send_sems.at[k],
                               recv_sem=recv_sems.at[k], device_id=(px, py, ci), device_id_type=MESH)
        out.append((mk(dst_ref=land_ref.at[me]), mk(dst_ref=land_ref.at[peer])))
    return out


def _xchg_start(src, scatter, name):
    land_shape = src.shape if scatter else (N_CHIPS,) + src.shape

    def body(src_ref, land_ref, send_sems, recv_sems, src_thru, land_thru, token):
        for mine, _ in _chip_copies(src_ref, land_ref, send_sems, recv_sems, scatter):
            mine.start()
        token[...] = jnp.zeros_like(token)

    return pl.pallas_call(
        body, name=name,
        out_shape=(pltpu.SemaphoreType.DMA((3,)), pltpu.SemaphoreType.DMA((3,)), pltpu.HBM(src.shape, src.dtype),
                   pltpu.HBM(land_shape, src.dtype), jax.ShapeDtypeStruct((8, 128), F32)),
        in_specs=(_HBM, _HBM), out_specs=(_SEM, _SEM, _HBM, _HBM, pl.BlockSpec(memory_space=pltpu.VMEM)),
        input_output_aliases={0: 2, 1: 3},
        compiler_params=pltpu.CompilerParams(has_side_effects=_EFFECT),
    )(pltpu.with_memory_space_constraint(src, pltpu.HBM),
      pltpu.with_memory_space_constraint(lax.empty(land_shape, src.dtype), pltpu.HBM))


def _xchg_wait(handle, after, scatter, name):
    send, recv, src_thru, land_thru = handle

    def body(src_ref, land_ref, send_sems, recv_sems, after_ref, src_out, land_out):
        for mine, theirs in _chip_copies(src_ref, land_ref, send_sems, recv_sems, scatter):
            mine.wait_send()
            theirs.wait_recv()

    return pl.pallas_call(
        body, name=name,
        out_shape=(pltpu.HBM(src_thru.shape, src_thru.dtype), pltpu.HBM(land_thru.shape, land_thru.dtype)),
        in_specs=(_HBM, _HBM, _SEM, _SEM, pl.BlockSpec(memory_space=pl.ANY)), out_specs=(_HBM, _HBM),
        input_output_aliases={0: 0, 1: 1},
        compiler_params=pltpu.CompilerParams(has_side_effects=_EFFECT),
    )(src_thru, land_thru, send, recv, after)


def _sibling_exchange(arrays, name):
    n = len(arrays)

    def body(*refs):
        ins, outs = refs[:n], refs[n:2 * n]
        send_sems, recv_sems = refs[2 * n:]
        xi, yi, ci = lax.axis_index("x"), lax.axis_index("y"), lax.axis_index("c")
        cps = []
        for i in range(n):
            cp = pltpu.make_async_remote_copy(
                src_ref=ins[i], dst_ref=outs[i], send_sem=send_sems.at[i], recv_sem=recv_sems.at[i],
                device_id=(xi, yi, 1 - ci), device_id_type=MESH)
            cp.start()
            cps.append(cp)
        for cp in cps:
            cp.wait()

    return pl.pallas_call(
        body, name=name, out_shape=tuple(jax.ShapeDtypeStruct(a.shape, a.dtype) for a in arrays),
        in_specs=[pl.BlockSpec(memory_space=pl.ANY)] * n,
        out_specs=tuple(pl.BlockSpec(memory_space=pl.ANY) for _ in range(n)),
        scratch_shapes=[pltpu.SemaphoreType.DMA((n,)), pltpu.SemaphoreType.DMA((n,))],
    )(*arrays)


def _pick(dim, pref):
    if dim <= pref:
        return dim
    t = (pref // 128) * 128
    while t >= 128:
        if dim % t == 0:
            return t
        t -= 128
    return dim


def _matmul(a, b, dims, out_dtype, name, tm=512, tn=1024, tk=2048):
    if dims == "nn":
        (m, kd), (_, n) = a.shape, b.shape
    elif dims == "nt":
        (m, kd), (n, _) = a.shape, b.shape
    else:
        (kd, m), (_, n) = a.shape, b.shape
    tm, tn, tk = _pick(m, tm), _pick(n, tn), _pick(kd, tk)
    nk = kd // tk
    if dims == "nn":
        a_spec = pl.BlockSpec((tm, tk), lambda j, i, k: (i, k))
        b_spec = pl.BlockSpec((tk, tn), lambda j, i, k: (k, j))
        dn = (((1,), (0,)), ((), ()))
    elif dims == "nt":
        a_spec = pl.BlockSpec((tm, tk), lambda j, i, k: (i, k))
        b_spec = pl.BlockSpec((tn, tk), lambda j, i, k: (j, k))
        dn = (((1,), (1,)), ((), ()))
    else:
        a_spec = pl.BlockSpec((tk, tm), lambda j, i, k: (k, i))
        b_spec = pl.BlockSpec((tk, tn), lambda j, i, k: (k, j))
        dn = (((0,), (0,)), ((), ()))

    def body(a_ref, b_ref, o_ref, acc_ref):
        part = lax.dot_general(a_ref[...].astype(BF16), b_ref[...].astype(BF16), dn, preferred_element_type=F32)
        if nk == 1:
            o_ref[...] = part.astype(o_ref.dtype)
        else:
            k = pl.program_id(2)

            @pl.when(k == 0)
            def _():
                acc_ref[...] = part

            @pl.when(k > 0)
            def _():
                acc_ref[...] += part

            @pl.when(k == nk - 1)
            def _():
                o_ref[...] = acc_ref[...].astype(o_ref.dtype)

    return pl.pallas_call(
        body, name=name, out_shape=jax.ShapeDtypeStruct((m, n), out_dtype),
        grid=(n // tn, m // tm, nk),
        in_specs=[a_spec, b_spec],
        out_specs=pl.BlockSpec((tm, tn), lambda j, i, k: (i, j)),
        scratch_shapes=[pltpu.VMEM((tm, tn), F32)],
        compiler_params=_params(("parallel", "parallel", "arbitrary")),
    )(a, b)


def _rstd(v):
    return lax.rsqrt(jnp.mean(v * v, axis=-1, keepdims=True) + EPS)


def _row(tm):
    return pl.BlockSpec((tm, D_MODEL), lambda i: (i, 0))


_VEC = pl.BlockSpec((1, D_MODEL), lambda i: (0, 0))


def _pre_norm(x, gn, sc, sh, name):
    t = x.shape[0]
    tm = min(TM_ROW, t)

    def body(x_ref, gn_ref, sc_ref, sh_ref, h_ref):
        xv = x_ref[...]
        h_ref[...] = ((xv * _rstd(xv) * gn_ref[...]) * (1.0 + sc_ref[...]) + sh_ref[...]).astype(BF16)

    return pl.pallas_call(
        body, name=name, out_shape=jax.ShapeDtypeStruct((t, D_MODEL), BF16), grid=(t // tm,),
        in_specs=[_row(tm), _VEC, _VEC, _VEC], out_specs=_row(tm),
        compiler_params=_params(("parallel",)),
    )(x, gn, sc, sh)


def _post_pre(x, y, g, gnp, gn, sc, sh, name):
    t = x.shape[0]
    tm = min(TM_ROW, t)

    def body(x_ref, y_ref, g_ref, gnp_ref, gn_ref, sc_ref, sh_ref, x1_ref, h_ref):
        yv = y_ref[...]
        x1 = x_ref[...] + g_ref[...] * (yv * _rstd(yv) * gnp_ref[...])
        x1_ref[...] = x1
        h_ref[...] = ((x1 * _rstd(x1) * gn_ref[...]) * (1.0 + sc_ref[...]) + sh_ref[...]).astype(BF16)

    return pl.pallas_call(
        body, name=name,
        out_shape=(jax.ShapeDtypeStruct((t, D_MODEL), F32), jax.ShapeDtypeStruct((t, D_MODEL), BF16)),
        grid=(t // tm,),
        in_specs=[_row(tm), _row(tm), _VEC, _VEC, _VEC, _VEC, _VEC], out_specs=(_row(tm), _row(tm)),
        compiler_params=_params(("parallel",)),
    )(x, y, g, gnp, gn, sc, sh)


def _post_loss(x, y, g, gnp, tgt, name):
    t = x.shape[0]
    tm = min(TM_ROW, t)

    def body(x_ref, y_ref, g_ref, gnp_ref, t_ref, dx_ref, loss_ref):
        yv = y_ref[...]
        diff = x_ref[...] + g_ref[...] * (yv * _rstd(yv) * gnp_ref[...]) - t_ref[...]
        dx_ref[...] = diff * (1.0 / D_MODEL)
        part = (0.5 / D_MODEL) * jnp.sum(jnp.sum(diff * diff, axis=-1, keepdims=True), axis=0, keepdims=True)

        @pl.when(pl.program_id(0) == 0)
        def _():
            loss_ref[...] = jnp.zeros_like(loss_ref)

        loss_ref[...] += jnp.broadcast_to(part, loss_ref.shape)

    return pl.pallas_call(
        body, name=name,
        out_shape=(jax.ShapeDtypeStruct((t, D_MODEL), F32), jax.ShapeDtypeStruct((8, 128), F32)),
        grid=(t // tm,),
        in_specs=[_row(tm), _row(tm), _VEC, _VEC, _row(tm)],
        out_specs=(_row(tm), pl.BlockSpec((8, 128), lambda i: (0, 0))),
        compiler_params=_params(("arbitrary",)),
    )(x, y, g, gnp, tgt)


def _acc_rows(ref, val):
    @pl.when(pl.program_id(0) == 0)
    def _():
        ref[...] = jnp.zeros_like(ref)

    ref[...] += jnp.sum(val, axis=0, keepdims=True)


def _post_bwd(dxn, y, g, gnp, name):
    t = y.shape[0]
    tm = min(TM_ROW, t)

    def body(dx_ref, y_ref, g_ref, gnp_ref, dy_ref, dg_ref, dgn_ref):
        yv, dxv = y_ref[...], dx_ref[...]
        r = _rstd(yv)
        yh = yv * r
        _acc_rows(dg_ref, dxv * (yh * gnp_ref[...]))
        dn = dxv * g_ref[...]
        _acc_rows(dgn_ref, dn * yh)
        dyh = dn * gnp_ref[...]
        dy_ref[...] = (r * (dyh - yh * jnp.mean(dyh * yh, axis=-1, keepdims=True))).astype(BF16)

    return pl.pallas_call(
        body, name=name,
        out_shape=(jax.ShapeDtypeStruct((t, D_MODEL), BF16), jax.ShapeDtypeStruct((1, D_MODEL), F32),
                   jax.ShapeDtypeStruct((1, D_MODEL), F32)),
        grid=(t // tm,),
        in_specs=[_row(tm), _row(tm), _VEC, _VEC], out_specs=(_row(tm), _VEC, _VEC),
        compiler_params=_params(("arbitrary",)),
    )(dxn, y, g, gnp)


def _pre_bwd(dh, xin, dres, gn, sc, name):
    t = xin.shape[0]
    tm = min(TM_ROW, t)

    def body(dh_ref, x_ref, dres_ref, gn_ref, sc_ref, dx_ref, dsh_ref, dsc_ref, dgn_ref):
        xv, dhv = x_ref[...], dh_ref[...]
        r = _rstd(xv)
        xh = xv * r
        _acc_rows(dsh_ref, dhv)
        _acc_rows(dsc_ref, dhv * (xh * gn_ref[...]))
        dn = dhv * (1.0 + sc_ref[...])
        _acc_rows(dgn_ref, dn * xh)
        dxh = dn * gn_ref[...]
        dx_ref[...] = dres_ref[...] + r * (dxh - xh * jnp.mean(dxh * xh, axis=-1, keepdims=True))

    vec = jax.ShapeDtypeStruct((1, D_MODEL), F32)
    return pl.pallas_call(
        body, name=name, out_shape=(jax.ShapeDtypeStruct((t, D_MODEL), F32), vec, vec, vec),
        grid=(t // tm,),
        in_specs=[_row(tm), _row(tm), _row(tm), _VEC, _VEC], out_specs=(_row(tm), _VEC, _VEC, _VEC),
        compiler_params=_params(("arbitrary",)),
    )(dh, xin, dres, gn, sc)


def _shift_down(v, halo, s):
    tm = v.shape[0]
    out = pltpu.roll(v, s, 0)
    row = lax.broadcasted_iota(jnp.int32, v.shape, 0)
    for j in range(s):
        out = jnp.where(row == j, jnp.broadcast_to(halo[8 - s + j:8 - s + j + 1, :], v.shape), out)
    return out


def _shift_up(v, halo, s):
    tm = v.shape[0]
    out = pltpu.roll(v, tm - s, 0)
    row = lax.broadcasted_iota(jnp.int32, v.shape, 0)
    for j in range(s):
        out = jnp.where(row == tm - s + j, jnp.broadcast_to(halo[j:j + 1, :], v.shape), out)
    return out


def _tile_specs(tm, cw, off, nrow):
    ob = off // cw
    r8 = tm // 8
    main = pl.BlockSpec((tm, cw), lambda j, i: (i, ob + j))
    prev = pl.BlockSpec((8, cw), lambda j, i: (jnp.maximum(i * r8 - 1, 0), ob + j))
    nxt = pl.BlockSpec((8, cw), lambda j, i: (jnp.minimum((i + 1) * r8, nrow * r8 - 1), ob + j))
    return main, prev, nxt


def _conv_fwd(p, w, name):
    t = p.shape[0]
    tm, cw = min(TM_EW, t), CW_EW
    nrow = t // tm
    cb_s, _, _ = _tile_specs(tm, cw, OFF_CB, nrow)
    cc_s, cc_p, _ = _tile_specs(tm, cw, OFF_CC, nrow)
    cx_s, cx_p, _ = _tile_specs(tm, cw, OFF_CX, nrow)

    def body(cb_ref, cc_ref, ccp_ref, cx_ref, cxp_ref, w_ref, z_ref):
        u = cc_ref[...] * cx_ref[...]
        uh = ccp_ref[...] * cxp_ref[...] * _unless(pl.program_id(1) == 0)
        wv = w_ref[...]
        conv = wv[2:3, :] * u + wv[1:2, :] * _shift_down(u, uh, 1) + wv[0:1, :] * _shift_down(u, uh, 2)
        z_ref[...] = (cb_ref[...] * conv).astype(BF16)

    return pl.pallas_call(
        body, name=name, out_shape=jax.ShapeDtypeStruct((t, CONV_WIDTH), BF16),
        grid=(CONV_WIDTH // cw, nrow),
        in_specs=[cb_s, cc_s, cc_p, cx_s, cx_p, pl.BlockSpec((8, cw), lambda j, i: (0, j))],
        out_specs=pl.BlockSpec((tm, cw), lambda j, i: (i, j)),
        compiler_params=_params(("parallel", "arbitrary")),
    )(p, p, p, p, p, w)


def _acc_w(ref, vals):
    @pl.when(pl.program_id(1) == 0)
    def _():
        ref[...] = jnp.zeros_like(ref)

    for j, v in enumerate(vals):
        ref[j:j + 1, :] += jnp.sum(v, axis=0, keepdims=True)


def _conv_bwd(dz, p, w, name):
    t = p.shape[0]
    tm, cw = min(TM_EW, t), CW_EW
    nrow = t // tm
    dz_s, _, dz_n = _tile_specs(tm, cw, 0, nrow)
    cb_s, _, cb_n = _tile_specs(tm, cw, OFF_CB, nrow)
    cc_s, cc_p, _ = _tile_specs(tm, cw, OFF_CC, nrow)
    cx_s, cx_p, _ = _tile_specs(tm, cw, OFF_CX, nrow)

    def body(dz_ref, dzn_ref, cb_ref, cbn_ref, cc_ref, ccp_ref, cx_ref, cxp_ref, w_ref,
             dcb_ref, dcc_ref, dcx_ref, dw_ref):
        i = pl.program_id(1)
        ccv, cxv, dzv = cc_ref[...], cx_ref[...], dz_ref[...]
        u = ccv * cxv
        uh = ccp_ref[...] * cxp_ref[...] * _unless(i == 0)
        wv = w_ref[...]
        u1, u2 = _shift_down(u, uh, 1), _shift_down(u, uh, 2)
        conv = wv[2:3, :] * u + wv[1:2, :] * u1 + wv[0:1, :] * u2
        dcb_ref[...] = (dzv * conv).astype(BF16)
        dconv = dzv * cb_ref[...]
        dch = dzn_ref[...] * cbn_ref[...] * _unless(i == nrow - 1)
        du = wv[2:3, :] * dconv + wv[1:2, :] * _shift_up(dconv, dch, 1) + wv[0:1, :] * _shift_up(dconv, dch, 2)
        dcc_ref[...] = (du * cxv).astype(BF16)
        dcx_ref[...] = (du * ccv).astype(BF16)
        _acc_w(dw_ref, (dconv * u2, dconv * u1, dconv * u))

    o_s = pl.BlockSpec((tm, cw), lambda j, i: (i, j))
    o_sh = jax.ShapeDtypeStruct((t, CONV_WIDTH), BF16)
    w_s = pl.BlockSpec((8, cw), lambda j, i: (0, j))
    return pl.pallas_call(
        body, name=name, out_shape=(o_sh, o_sh, o_sh, jax.ShapeDtypeStruct((8, CONV_WIDTH), F32)),
        grid=(CONV_WIDTH // cw, nrow),
        in_specs=[dz_s, dz_n, cb_s, cb_n, cc_s, cc_p, cx_s, cx_p, w_s],
        out_specs=(o_s, o_s, o_s, w_s),
        compiler_params=_params(("parallel", "arbitrary")),
    )(dz, dz, p, p, p, p, p, p, w)


def _ffn_fwd(u, w, name):
    t = u.shape[0]
    tm, cw = min(TM_EW, t), CW_EW
    nrow = t // tm
    g_s, g_p, _ = _tile_specs(tm, cw, 0, nrow)
    u_s, _, _ = _tile_specs(tm, cw, D_FF, nrow)

    def body(g_ref, gp_ref, u_ref, w_ref, f_ref):
        gv = g_ref[...]
        gh = gp_ref[...] * _unless(pl.program_id(1) == 0)
        wv = w_ref[...]
        gc = wv[2:3, :] * gv + wv[1:2, :] * _shift_down(gv, gh, 1) + wv[0:1, :] * _shift_down(gv, gh, 2)
        f_ref[...] = (_gelu(gc) * u_ref[...]).astype(BF16)

    return pl.pallas_call(
        body, name=name, out_shape=jax.ShapeDtypeStruct((t, D_FF), BF16),
        grid=(D_FF // cw, nrow),
        in_specs=[g_s, g_p, u_s, pl.BlockSpec((8, cw), lambda j, i: (0, j))],
        out_specs=pl.BlockSpec((tm, cw), lambda j, i: (i, j)),
        compiler_params=_params(("parallel", "arbitrary")),
    )(u, u, u, w)


def _ffn_bwd(df, u, w, name):
    t = u.shape[0]
    tm, cw = min(TM_EW, t), CW_EW
    nrow = t // tm
    df_s, _, df_n = _tile_specs(tm, cw, 0, nrow)
    g_s, g_p, g_n = _tile_specs(tm, cw, 0, nrow)
    u_s, _, u_n = _tile_specs(tm, cw, D_FF, nrow)
    r8 = tm // 8

    def body(df_ref, dfn_ref, g_ref, gp_ref, gn_ref, u_ref, un_ref, w_ref, dg_ref, du_ref, dw_ref):
        i = pl.program_id(1)
        gv, dfv, uv = g_ref[...], df_ref[...], u_ref[...]
        gh = gp_ref[...] * _unless(i == 0)
        wv = w_ref[...]
        g1, g2 = _shift_down(gv, gh, 1), _shift_down(gv, gh, 2)
        gc = wv[2:3, :] * gv + wv[1:2, :] * g1 + wv[0:1, :] * g2
        du_ref[...] = (dfv * _gelu(gc)).astype(BF16)
        dgc = dfv * uv * _gelu_grad(gc)
        gnv = gn_ref[...]
        gtail = gv[tm - 8:tm, :]
        gcn = (wv[2:3, :] * gnv + wv[1:2, :] * _shift_down(gnv, gtail, 1) + wv[0:1, :] * _shift_down(gnv, gtail, 2))
        dgcn = dfn_ref[...] * un_ref[...] * _gelu_grad(gcn) * _unless(i == nrow - 1)
        dg = wv[2:3, :] * dgc + wv[1:2, :] * _shift_up(dgc, dgcn, 1) + wv[0:1, :] * _shift_up(dgc, dgcn, 2)
        dg_ref[...] = dg.astype(BF16)
        _acc_w(dw_ref, (dgc * g2, dgc * g1, dgc * gv))

    o_s = pl.BlockSpec((tm, cw), lambda j, i: (i, j))
    o_sh = jax.ShapeDtypeStruct((t, D_FF), BF16)
    w_s = pl.BlockSpec((8, cw), lambda j, i: (0, j))
    return pl.pallas_call(
        body, name=name, out_shape=(o_sh, o_sh, jax.ShapeDtypeStruct((8, D_FF), F32)),
        grid=(D_FF // cw, nrow),
        in_specs=[df_s, df_n, g_s, g_p, g_n, u_s, u_n, w_s],
        out_specs=(o_s, o_s, w_s),
        compiler_params=_params(("parallel", "arbitrary")),
    )(df, df, u, u, u, u, u, w)


def _merge_fwd(ya, yb, p, name):
    t = ya.shape[0]
    tm, cw = min(TM_EW, t), CW_EW
    y_s = pl.BlockSpec((tm, cw), lambda i, j: (i, j))

    def body(ya_ref, yb_ref, ga_ref, gb_ref, m_ref):
        m_ref[...] = (_sigmoid(ga_ref[...]) * ya_ref[...] + _sigmoid(gb_ref[...]) * yb_ref[...]).astype(BF16)

    return pl.pallas_call(
        body, name=name, out_shape=jax.ShapeDtypeStruct((t, D_MODEL), BF16),
        grid=(t // tm, D_MODEL // cw),
        in_specs=[y_s, y_s, pl.BlockSpec((tm, cw), lambda i, j: (i, OFF_GA // cw + j)),
                  pl.BlockSpec((tm, cw), lambda i, j: (i, OFF_GB // cw + j))],
        out_specs=y_s, compiler_params=_params(("parallel", "parallel")),
    )(ya, yb, p, p)


def _merge_bwd(dm, ya, yb, p, name):
    t = ya.shape[0]
    tm, cw = min(TM_EW, t), CW_EW
    y_s = pl.BlockSpec((tm, cw), lambda i, j: (i, j))

    def body(dm_ref, ya_ref, yb_ref, ga_ref, gb_ref, dya_ref, dyb_ref, dga_ref, dgb_ref):
        dmv = dm_ref[...]
        sa, sb = _sigmoid(ga_ref[...]), _sigmoid(gb_ref[...])
        dya_ref[...] = (dmv * sa).astype(BF16)
        dyb_ref[...] = (dmv * sb).astype(BF16)
        dga_ref[...] = (dmv * ya_ref[...] * sa * (1.0 - sa)).astype(BF16)
        dgb_ref[...] = (dmv * yb_ref[...] * sb * (1.0 - sb)).astype(BF16)

    o_sh = jax.ShapeDtypeStruct((t, D_MODEL), BF16)
    return pl.pallas_call(
        body, name=name, out_shape=(o_sh, o_sh, o_sh, o_sh),
        grid=(t // tm, D_MODEL // cw),
        in_specs=[y_s, y_s, y_s, pl.BlockSpec((tm, cw), lambda i, j: (i, OFF_GA // cw + j)),
                  pl.BlockSpec((tm, cw), lambda i, j: (i, OFF_GB // cw + j))],
        out_specs=(y_s, y_s, y_s, y_s), compiler_params=_params(("parallel", "parallel")),
    )(dm, ya, yb, p, p)


def _tri(lower):
    r = lax.broadcasted_iota(jnp.int32, (CHUNK, CHUNK), 0)
    c = lax.broadcasted_iota(jnp.int32, (CHUNK, CHUNK), 1)
    return ((c <= r) if lower else (c >= r)).astype(F32)


def _eye_mask():
    r = lax.broadcasted_iota(jnp.int32, (GLA_DK, GLA_DK), 0)
    c = lax.broadcasted_iota(jnp.int32, (GLA_DK, GLA_DK), 1)
    return r == c


def _row_to_col(v):
    return jnp.sum(jnp.where(_eye_mask(), jnp.broadcast_to(v, (GLA_DK, GLA_DK)), 0.0), axis=1, keepdims=True)


def _col_to_row(v):
    return jnp.sum(jnp.where(_eye_mask(), jnp.broadcast_to(v, (GLA_DK, GLA_DK)), 0.0), axis=0, keepdims=True)


def _dot(a, b, dn):
    return lax.dot_general(a.astype(BF16), b.astype(BF16), (dn, ((), ())), preferred_element_type=F32)


_NN = ((1,), (0,))
_NT = ((1,), (1,))
_TN = ((0,), (0,))


def _gate_logits(lr_ref, wa_ref, ba_ref):
    return _dot(lr_ref[...], wa_ref[...], _NN) + ba_ref[...]


def _chunk_decay(la, tri):
    cum = lax.dot_general(tri, la, ((_NN), ((), ())), precision=lax.Precision.HIGHEST, preferred_element_type=F32)
    e = cum[CHUNK - 1:CHUNK, :]
    return cum, e, jnp.exp(e - cum)


def _gla_fwd(p, wa, ba, name):
    t = p.shape[0]
    rows = min(GLA_ROWS, t)
    cb = rows // CHUNK
    nc = t // CHUNK
    scale = GLA_DK ** -0.5

    def body(q_ref, k_ref, v_ref, lr_ref, wa_ref, ba_ref, o_ref, st_ref, s_scr):
        @pl.when(pl.program_id(0) == 0)
        def _():
            s_scr[...] = jnp.zeros_like(s_scr)

        la_all = _log_sigmoid(_gate_logits(lr_ref, wa_ref, ba_ref)) * (1.0 / GLA_TAU)
        tri = _tri(True)
        for ch in range(cb):
            rs = slice(ch * CHUNK, (ch + 1) * CHUNK)
            for h in range(GLA_HEADS):
                ks = slice(h * GLA_DK, (h + 1) * GLA_DK)
                vs = slice(h * GLA_DV, (h + 1) * GLA_DV)
                _, e, w = _chunk_decay(la_all[rs, ks], tri)
                kd = k_ref[rs, ks] * w
                s_new = _row_to_col(jnp.exp(e)) * s_scr[ks, :] + _dot(kd, v_ref[rs, vs], _TN)
                s_scr[ks, :] = s_new
                st_ref[ch, ks, :] = s_new
                o_ref[rs, vs] = _dot(q_ref[rs, ks] * scale, s_new, _NN)

    return pl.pallas_call(
        body, name=name,
        out_shape=(jax.ShapeDtypeStruct((t, GLA_V), F32), jax.ShapeDtypeStruct((nc, GLA_QK, GLA_DV), F32)),
        grid=(t // rows,),
        in_specs=[pl.BlockSpec((rows, GLA_QK), lambda i: (i, OFF_Q // GLA_QK)),
                  pl.BlockSpec((rows, GLA_QK), lambda i: (i, OFF_K // GLA_QK)),
                  pl.BlockSpec((rows, GLA_V), lambda i: (i, OFF_V // GLA_V)),
                  pl.BlockSpec((rows, LR_PAD), lambda i: (i, OFF_LR // LR_PAD)),
                  pl.BlockSpec((LR_PAD, GLA_QK), lambda i: (0, 0)),
                  pl.BlockSpec((1, GLA_QK), lambda i: (0, 0))],
        out_specs=(pl.BlockSpec((rows, GLA_V), lambda i: (i, 0)),
                   pl.BlockSpec((cb, GLA_QK, GLA_DV), lambda i: (i, 0, 0))),
        scratch_shapes=[pltpu.VMEM((GLA_QK, GLA_DV), F32)],
        compiler_params=_params(("arbitrary",)),
    )(p, p, p, p, wa, ba)


def _gla_bwd(do, p, st, wa, ba, name):
    t = p.shape[0]
    rows = min(GLA_ROWS, t)
    cb = rows // CHUNK
    nb = t // rows
    scale = GLA_DK ** -0.5

    def rev(i):
        return nb - 1 - i

    def body(do_ref, q_ref, k_ref, v_ref, lr_ref, st_ref, stp_ref, wa_ref, ba_ref,
             dq_ref, dk_ref, dv_ref, dlr_ref, dwa_ref, dba_ref, ds_scr, dz_scr):
        i = pl.program_id(0)

        @pl.when(i == 0)
        def _():
            ds_scr[...] = jnp.zeros_like(ds_scr)
            dwa_ref[...] = jnp.zeros_like(dwa_ref)
            dba_ref[...] = jnp.zeros_like(dba_ref)

        z_all = _gate_logits(lr_ref, wa_ref, ba_ref)
        la_all = _log_sigmoid(z_all) * (1.0 / GLA_TAU)
        tri, triu = _tri(True), _tri(False)
        last_row = lax.broadcasted_iota(jnp.int32, (CHUNK, GLA_DK), 0) == CHUNK - 1
        keep_prev = _unless(i == nb - 1)
        for ch in reversed(range(cb)):
            rs = slice(ch * CHUNK, (ch + 1) * CHUNK)
            for h in range(GLA_HEADS):
                ks = slice(h * GLA_DK, (h + 1) * GLA_DK)
                vs = slice(h * GLA_DV, (h + 1) * GLA_DV)
                _, e, w = _chunk_decay(la_all[rs, ks], tri)
                kd = k_ref[rs, ks] * w
                exp_e = jnp.exp(e)
                s_c = st_ref[ch, ks, :]
                if ch > 0:
                    s_p = st_ref[ch - 1, ks, :]
                else:
                    s_p = stp_ref[0, ks, :] * keep_prev
                do_c = do_ref[rs, vs]
                vv = v_ref[rs, vs]
                ds_tot = ds_scr[ks, :] + _dot(q_ref[rs, ks] * scale, do_c, _TN)
                dq_ref[rs, ks] = (_dot(do_c, s_c, _NT) * scale).astype(BF16)
                dkd = _dot(vv, ds_tot, _NT)
                dv_ref[rs, vs] = _dot(kd, ds_tot, _NN).astype(BF16)
                dexp_col = jnp.sum(ds_tot * s_p, axis=1, keepdims=True)
                ds_scr[ks, :] = _row_to_col(exp_e) * ds_tot
                dk_ref[rs, ks] = (dkd * w).astype(BF16)
                dwt = dkd * kd
                de = jnp.sum(dwt, axis=0, keepdims=True) + _col_to_row(dexp_col) * exp_e
                dcum = jnp.where(last_row, de - dwt, -dwt)
                da = lax.dot_general(triu, dcum, (_NN, ((), ())), precision=lax.Precision.HIGHEST,
                                     preferred_element_type=F32)
                dz_scr[rs, ks] = da * (1.0 / GLA_TAU) * _sigmoid(-z_all[rs, ks])
        dz = dz_scr[...]
        dlr_ref[...] = _dot(dz, wa_ref[...], _NT).astype(BF16)
        dwa_ref[...] += _dot(lr_ref[...], dz, _TN)
        dba_ref[...] += jnp.sum(dz, axis=0, keepdims=True)

    qk_sh = jax.ShapeDtypeStruct((t, GLA_QK), BF16)
    return pl.pallas_call(
        body, name=name,
        out_shape=(qk_sh, qk_sh, jax.ShapeDtypeStruct((t, GLA_V), BF16), jax.ShapeDtypeStruct((t, LR_PAD), BF16),
                   jax.ShapeDtypeStruct((LR_PAD, GLA_QK), F32), jax.ShapeDtypeStruct((1, GLA_QK), F32)),
        grid=(nb,),
        in_specs=[pl.BlockSpec((rows, GLA_V), lambda i: (rev(i), 0)),
                  pl.BlockSpec((rows, GLA_QK), lambda i: (rev(i), OFF_Q // GLA_QK)),
                  pl.BlockSpec((rows, GLA_QK), lambda i: (rev(i), OFF_K // GLA_QK)),
                  pl.BlockSpec((rows, GLA_V), lambda i: (rev(i), OFF_V // GLA_V)),
                  pl.BlockSpec((rows, LR_PAD), lambda i: (rev(i), OFF_LR // LR_PAD)),
                  pl.BlockSpec((cb, GLA_QK, GLA_DV), lambda i: (rev(i), 0, 0)),
                  pl.BlockSpec((1, GLA_QK, GLA_DV), lambda i: (jnp.maximum(rev(i) * cb - 1, 0), 0, 0)),
                  pl.BlockSpec((LR_PAD, GLA_QK), lambda i: (0, 0)),
                  pl.BlockSpec((1, GLA_QK), lambda i: (0, 0))],
        out_specs=(pl.BlockSpec((rows, GLA_QK), lambda i: (rev(i), 0)),
                   pl.BlockSpec((rows, GLA_QK), lambda i: (rev(i), 0)),
                   pl.BlockSpec((rows, GLA_V), lambda i: (rev(i), 0)),
                   pl.BlockSpec((rows, LR_PAD), lambda i: (rev(i), 0)),
                   pl.BlockSpec((LR_PAD, GLA_QK), lambda i: (0, 0)),
                   pl.BlockSpec((1, GLA_QK), lambda i: (0, 0))),
        scratch_shapes=[pltpu.VMEM((GLA_QK, GLA_DV), F32), pltpu.VMEM((rows, GLA_QK), F32)],
        compiler_params=_params(("arbitrary",)),
    )(do, p, p, p, p, st, st, wa, ba)


def _gla_out_fwd(o, p, gng, name):
    t = o.shape[0]
    tm = min(TM_EW, t)

    def body(o_ref, r_ref, g_ref, z_ref):
        gv = g_ref[...]
        for h in range(GLA_HEADS):
            vs = slice(h * GLA_DV, (h + 1) * GLA_DV)
            ov, rv = o_ref[:, vs], r_ref[:, vs]
            z_ref[:, vs] = ((ov * _rstd(ov) * gv) * (rv * _sigmoid(rv))).astype(BF16)

    return pl.pallas_call(
        body, name=name, out_shape=jax.ShapeDtypeStruct((t, GLA_V), BF16), grid=(t // tm,),
        in_specs=[pl.BlockSpec((tm, GLA_V), lambda i: (i, 0)),
                  pl.BlockSpec((tm, GLA_V), lambda i: (i, OFF_R // GLA_V)),
                  pl.BlockSpec((1, GLA_DV), lambda i: (0, 0))],
        out_specs=pl.BlockSpec((tm, GLA_V), lambda i: (i, 0)),
        compiler_params=_params(("parallel",)),
    )(o, p, gng)


def _gla_out_bwd(dz, o, p, gng, name):
    t = o.shape[0]
    tm = min(TM_EW, t)

    def body(dz_ref, o_ref, r_ref, g_ref, do_ref, dr_ref, dg_ref):
        @pl.when(pl.program_id(0) == 0)
        def _():
            dg_ref[...] = jnp.zeros_like(dg_ref)

        gv = g_ref[...]
        for h in range(GLA_HEADS):
            vs = slice(h * GLA_DV, (h + 1) * GLA_DV)
            ov, rv, dzv = o_ref[:, vs], r_ref[:, vs], dz_ref[:, vs]
            rs = _rstd(ov)
            oh = ov * rs
            sg = _sigmoid(rv)
            dr_ref[:, vs] = (dzv * (oh * gv) * (sg * (1.0 + rv * (1.0 - sg)))).astype(BF16)
            don = dzv * (rv * sg)
            dg_ref[...] += jnp.sum(don * oh, axis=0, keepdims=True)
            doh = don * gv
            do_ref[:, vs] = rs * (doh - oh * jnp.mean(doh * oh, axis=-1, keepdims=True))

    row = pl.BlockSpec((tm, GLA_V), lambda i: (i, 0))
    return pl.pallas_call(
        body, name=name,
        out_shape=(jax.ShapeDtypeStruct((t, GLA_V), F32), jax.ShapeDtypeStruct((t, GLA_V), BF16),
                   jax.ShapeDtypeStruct((1, GLA_DV), F32)),
        grid=(t // tm,),
        in_specs=[row, row, pl.BlockSpec((tm, GLA_V), lambda i: (i, OFF_R // GLA_V)),
                  pl.BlockSpec((1, GLA_DV), lambda i: (0, 0))],
        out_specs=(row, row, pl.BlockSpec((1, GLA_DV), lambda i: (0, 0))),
        compiler_params=_params(("arbitrary",)),
    )(dz, o, p, gng)


def _ada_fwd(c_all, w, b, layer, name):
    n = w.shape[2]
    tn = _pick(n, 512)

    def body(c_ref, w_ref, b_ref, o_ref):
        cv = c_ref[...]
        o_ref[...] = _dot(cv * _sigmoid(cv), w_ref[...], _NN) + b_ref[...]

    return pl.pallas_call(
        body, name=name, out_shape=jax.ShapeDtypeStruct((16, n), F32), grid=(n // tn,),
        in_specs=[pl.BlockSpec((16, D_MODEL), lambda j: (0, 0)),
                  pl.BlockSpec((None, D_MODEL, tn), lambda j: (layer, 0, j)),
                  pl.BlockSpec((1, tn), lambda j: (0, j))],
        out_specs=pl.BlockSpec((16, tn), lambda j: (0, j)),
        compiler_params=_params(("parallel",)),
    )(c_all, w, b)


def _ada_bwd(c_all, dmod, name):
    n = dmod.shape[2]
    tn = _pick(n, 512)

    def body(c_ref, d_ref, o_ref):
        cv = c_ref[...]
        o_ref[...] = _dot(cv * _sigmoid(cv), d_ref[...], _TN)

    return pl.pallas_call(
        body, name=name, out_shape=jax.ShapeDtypeStruct((DEPTH, D_MODEL, n), F32), grid=(DEPTH, n // tn),
        in_specs=[pl.BlockSpec((16, D_MODEL), lambda l, j: (0, 0)),
                  pl.BlockSpec((None, 16, tn), lambda l, j: (l, 0, j))],
        out_specs=pl.BlockSpec((None, D_MODEL, tn), lambda l, j: (l, 0, j)),
        compiler_params=_params(("parallel", "parallel")),
    )(c_all, dmod)


def _rows_tile(nrows, ncols, target_bytes):
    want = max(16, target_bytes // (4 * ncols))
    if nrows <= want:
        return nrows
    t = (want // 16) * 16
    while t >= 16:
        if nrows % t == 0:
            return t
        t -= 16
    return nrows


def _sum_chips(sent, landed, chip, name):
    _, nrows, ncols = sent[0].shape
    tr = _rows_tile(nrows, ncols, 2 << 20)
    nblk = nrows // tr

    def body(chip_ref, *refs):
        own, got, o_ref = refs[:DEPTH], refs[DEPTH:2 * DEPTH], refs[2 * DEPTH]
        me = chip_ref[0]
        for l in range(DEPTH):
            for j in range(N_CHIPS):
                def add(val):
                    if j == 0:
                        o_ref[...] = val.astype(F32)
                    else:
                        o_ref[...] += val.astype(F32)

                @pl.when(jnp.logical_and(pl.program_id(0) == l, me == j))
                def _():
                    add(own[l][...])

                @pl.when(jnp.logical_and(pl.program_id(0) == l, me != j))
                def _():
                    add(got[l][j])

    def rows_of(layer):
        return lambda l, i, chip_ref: jnp.where(l == layer, i, 0)

    own_specs = [pl.BlockSpec((None, tr, ncols), lambda l, i, chip_ref, r=rows_of(k): (chip_ref[0], r(l, i, chip_ref), 0))
                 for k in range(DEPTH)]
    got_specs = [pl.BlockSpec((N_CHIPS, tr, ncols), lambda l, i, chip_ref, r=rows_of(k): (0, r(l, i, chip_ref), 0))
                 for k in range(DEPTH)]
    return pl.pallas_call(
        body, name=name, out_shape=jax.ShapeDtypeStruct((DEPTH, nrows, ncols), F32),
        grid_spec=pltpu.PrefetchScalarGridSpec(
            num_scalar_prefetch=1, grid=(DEPTH, nblk), in_specs=own_specs + got_specs,
            out_specs=pl.BlockSpec((None, tr, ncols), lambda l, i, chip_ref: (l, i, 0))),
        compiler_params=_params(("arbitrary", "arbitrary")),
    )(chip, *sent, *landed)


def _adamw(w, m, v, ga, gb, name):
    nl, nrows, ncols = w.shape
    tr = _rows_tile(nrows, ncols, 1 << 20)
    two = gb is not None
    c1 = 1.0 - ADAM_B1 ** ADAM_STEP
    c2 = 1.0 - ADAM_B2 ** ADAM_STEP

    def body(*refs):
        if two:
            w_ref, m_ref, v_ref, ga_ref, gb_ref, g_ref, d_ref, nm_ref, nv_ref = refs
            g = ga_ref[...] + gb_ref[...]
        else:
            w_ref, m_ref, v_ref, ga_ref, g_ref, d_ref, nm_ref, nv_ref = refs
            g = ga_ref[...]
        g_ref[...] = g
        nm = ADAM_B1 * m_ref[...] + (1.0 - ADAM_B1) * g
        nv = ADAM_B2 * v_ref[...] + (1.0 - ADAM_B2) * (g * g)
        nm_ref[...] = nm
        nv_ref[...] = nv
        d_ref[...] = -ADAM_LR * ((nm / c1) / (jnp.sqrt(nv / c2) + ADAM_EPS) + ADAM_WD * w_ref[...])

    blk = pl.BlockSpec((None, tr, ncols), lambda l, i: (l, i, 0))
    sh = jax.ShapeDtypeStruct((nl, nrows, ncols), F32)
    ins = [w, m, v, ga] + ([gb] if two else [])
    return pl.pallas_call(
        body, name=name, out_shape=(sh, sh, sh, sh), grid=(nl, nrows // tr),
        in_specs=[blk] * len(ins), out_specs=(blk, blk, blk, blk),
        compiler_params=_params(("parallel", "parallel")),
    )(*ins)


def _pad_rows(a, rows):
    return jnp.concatenate([a, jnp.zeros((rows - a.shape[0],) + a.shape[1:], a.dtype)], axis=0)


def _cols_from_chips(a):
    return jnp.transpose(a, (1, 0, 2)).reshape(a.shape[1], N_CHIPS * a.shape[2])


def _cols_to_chips(a):
    rows, n = a.shape
    return jnp.transpose(a.reshape(rows, N_CHIPS, n // N_CHIPS), (1, 0, 2))


N_IN_CHIP = N_IN // N_CHIPS
_LR_LO = 3072 - N_IN_CHIP
_LR_HI = _LR_LO + GLA_LOWRANK


def _w_in_from_chips(a):
    return jnp.concatenate([a[0], a[1][:, :_LR_LO], a[1][:, _LR_HI:], a[2], a[3], a[1][:, _LR_LO:_LR_HI],
                            jnp.zeros((a.shape[1], LR_PAD - GLA_LOWRANK), a.dtype)], axis=1)


def _w_in_to_chips(w):
    s2 = 2 * N_IN_CHIP - GLA_LOWRANK
    s3 = s2 + N_IN_CHIP
    c1 = jnp.concatenate([w[:, N_IN_CHIP:3072], w[:, OFF_LR:OFF_LR + GLA_LOWRANK], w[:, 3072:s2]], axis=1)
    return jnp.stack([w[:, :N_IN_CHIP], c1, w[:, s2:s3], w[:, s3:OFF_LR]])


_BIG = ("w_in", "w_og", "w_oc", "w_o", "w_up", "w_dn")
_ROW_SHARDED = ("w_o", "w_dn")


def kernel(x, c, w_ada, b_ada, norm_g, w_in, w_a2, b_a2, gla_norm_g, w_out_gla, conv_mix_w, w_out_conv, w_o, w_up, ffn_conv_w, w_down, loss_target, m_w_ada, m_b_ada, m_norm_g, m_w_in, m_w_a2, m_b_a2, m_gla_norm_g, m_w_out_gla, m_conv_mix_w, m_w_out_conv, m_w_o, m_w_up, m_ffn_conv_w, m_w_down, v_w_ada, v_b_ada, v_norm_g, v_w_in, v_w_a2, v_b_a2, v_gla_norm_g, v_w_out_gla, v_conv_mix_w, v_w_out_conv, v_w_o, v_w_up, v_ffn_conv_w, v_w_down):
    xi, yi, ci = lax.axis_index("x"), lax.axis_index("y"), lax.axis_index("c")
    chip = 2 * xi + yi
    dev = 2 * chip + ci
    chip_arr = jnp.reshape(chip, (1,)).astype(jnp.int32)
    xt = x[0]
    tgt = loss_target[0]

    c_all = _allgather8(jnp.broadcast_to(c, (8, D_MODEL)), "gather_c")[0][:, 0, :]
    c16 = _pad_rows(c_all, 16)
    sm_parts = [norm_g.reshape(-1), w_a2.reshape(-1), conv_mix_w.reshape(-1), ffn_conv_w.reshape(-1)]
    sm_sizes = [a.shape[0] for a in sm_parts]
    sm_flat = jnp.concatenate(sm_parts)
    sm_rows = -(-sm_flat.shape[0] // 128)
    sm_rows = -(-sm_rows // 8) * 8
    sm_flat = jnp.concatenate([sm_flat, jnp.zeros((sm_rows * 128 - sm_flat.shape[0],), F32)]).reshape(sm_rows, 128)
    sm_all = _allgather8(sm_flat, "gather_small")[0].reshape(N_DEV, -1)[0::2]
    offs = [0]
    for s in sm_sizes:
        offs.append(offs[-1] + s)

    def small_full(idx, shape):
        a = sm_all[:, offs[idx]:offs[idx + 1]].reshape((N_CHIPS,) + shape)
        a = jnp.moveaxis(a, 0, -2)
        return a.reshape(shape[:-1] + (N_CHIPS * shape[-1],))

    norm_g_f = small_full(0, (DEPTH, 4, 512))
    w_a2_f = small_full(1, (DEPTH, GLA_LOWRANK, 128))
    conv_w_f = small_full(2, (DEPTH, 3, 256))
    ffn_w_f = small_full(3, (DEPTH, 3, 1408))

    b_loc = lax.dynamic_slice(b_ada, (0, chip * 3072), (DEPTH, 3072))
    mod_loc = jnp.concatenate(
        [_ada_fwd(c16, w_ada, b_loc[l:l + 1], l, "ada_fwd")[:8] for l in range(DEPTH)], axis=0)
    mod_all = _allgather8(mod_loc, "gather_mod")[0][0::2]
    mods = []
    for l in range(DEPTH):
        row = lax.dynamic_slice(mod_all, (0, l * 8 + dev, 0), (N_CHIPS, 1, 3072)).reshape(1, 6 * D_MODEL)
        mods.append([row[:, k * D_MODEL:(k + 1) * D_MODEL] for k in range(6)])

    big = dict(w_in=w_in, w_og=w_out_gla, w_oc=w_out_conv, w_o=w_o, w_up=w_up, w_dn=w_down)
    gathers = {}
    tok = 0.0 * (mod_all[0, 0, 0] + sm_all[0, 0])
    for l in range(DEPTH):
        for k in _BIG:
            shard = (big[k][l] + tok).astype(BF16)
            *handle, token = _xchg_start(shard, False, "gather_start_%s_%d" % (k, l))
            gathers[k, l] = (shard, tuple(handle))
            tok = token[0, 0]

    def gathered(k, l, after):
        shard, handle = gathers[k, l]
        _, land = _xchg_wait(handle, after, False, "gather_wait_%s_%d" % (k, l))
        slot = lax.broadcasted_iota(jnp.int32, (N_CHIPS, 1, 1), 0)
        full = jnp.where(slot == chip, shard[None], land)
        if k in _ROW_SHARDED:
            return full.reshape(N_CHIPS * full.shape[1], full.shape[2])
        return _w_in_from_chips(full) if k == "w_in" else _cols_from_chips(full)

    saved = []
    h = None
    xin = xt
    for l in range(DEPTH):
        sh1, sc1, g1, sh2, sc2, g2 = mods[l]
        gn = [norm_g_f[l, k][None] for k in range(4)]
        wa = _pad_rows(w_a2_f[l], LR_PAD)
        ba = b_a2[l][None]
        gng = gla_norm_g[l][None]
        cw8 = _pad_rows(conv_w_f[l], 8)
        fw8 = _pad_rows(ffn_w_f[l], 8)
        if l == 0:
            h = _pre_norm(xin, gn[0] + tok, sc1, sh1, "pre_norm")
        wi = gathered("w_in", l, h)
        p = _matmul(h, wi, "nn", F32, "mm_in", tn=1152)
        o, st = _gla_fwd(p, wa, ba, "gla_fwd")
        za = _gla_out_fwd(o, p, gng, "gla_out_fwd")
        zb = _conv_fwd(p, cw8, "conv_fwd")
        wog, woc = gathered("w_og", l, zb), gathered("w_oc", l, zb)
        ya = _matmul(za, wog, "nn", F32, "mm_out_gla")
        yb = _matmul(zb, woc, "nn", F32, "mm_out_conv")
        mm = _merge_fwd(ya, yb, p, "merge_fwd")
        wo = gathered("w_o", l, mm)
        y = _matmul(mm, wo, "nn", F32, "mm_o")
        x1, h2 = _post_pre(xin, y, g1, gn[1], gn[2], sc2, sh2, "post_pre")
        wup = gathered("w_up", l, h2)
        u = _matmul(h2, wup, "nn", F32, "mm_up")
        f = _ffn_fwd(u, fw8, "ffn_fwd")
        wdn = gathered("w_dn", l, f)
        y2 = _matmul(f, wdn, "nn", F32, "mm_down")
        saved.append(dict(xin=xin, h=h, p=p, o=o, st=st, za=za, zb=zb, ya=ya, yb=yb, mm=mm, y=y, x1=x1, h2=h2,
                          u=u, f=f, y2=y2, wi=wi, wog=wog, woc=woc, wo=wo, wup=wup, wdn=wdn, wa=wa, ba=ba,
                          gng=gng, cw8=cw8, fw8=fw8, gn=gn, mod=mods[l]))
        if l + 1 < DEPTH:
            nsh1, nsc1 = mods[l + 1][0], mods[l + 1][1]
            xin, h = _post_pre(x1, y2, g2, gn[3], norm_g_f[l + 1, 0][None], nsc1, nsh1, "post_pre")
        else:
            dx, loss_tile = _post_loss(x1, y2, g2, gn[3], tgt, "post_loss")
    loss = lax.psum(loss_tile[0, 0], ("x", "y", "c"))

    scatters = {}

    def scatter(k, l, dw):
        if k in _ROW_SHARDED:
            send = dw.reshape(N_CHIPS, dw.shape[0] // N_CHIPS, dw.shape[1])
        else:
            send = _w_in_to_chips(dw) if k == "w_in" else _cols_to_chips(dw)
        *handle, token = _xchg_start(send, True, "scatter_start_%s_%d" % (k, l))
        scatters[k, l] = tuple(handle)
        return token[0, 0]

    sm = {k: [None] * DEPTH for k in ("dmod", "norm_g", "w_a2", "b_a2", "gng", "conv_w", "ffn_w")}
    for l in reversed(range(DEPTH)):
        s = saved[l]
        sh1, sc1, g1, sh2, sc2, g2 = s["mod"]
        gn = s["gn"]
        dy2, dg2, dgn3 = _post_bwd(dx, s["y2"], g2, gn[3], "post_bwd")
        tk = scatter("w_dn", l, _matmul(s["f"], dy2, "tn", BF16, "mm_down_dw", tm=512, tn=1024, tk=2048))
        df = _matmul(dy2, s["wdn"], "nt", F32, "mm_down_dx")
        dgate, dup, dfw = _ffn_bwd(df, s["u"], s["fw8"] + tk, "ffn_bwd")
        du = jnp.concatenate([dgate, dup], axis=1)
        tk = scatter("w_up", l, _matmul(s["h2"], du, "tn", BF16, "mm_up_dw", tm=512, tn=1024, tk=2048))
        dh2 = _matmul(du, s["wup"], "nt", F32, "mm_up_dx", tn=1024, tk=2816)
        dx1, dsh2, dsc2, dgn2 = _pre_bwd(dh2, s["x1"], dx, gn[2] + tk, sc2, "pre_bwd")
        dy, dg1, dgn1 = _post_bwd(dx1, s["y"], g1, gn[1], "post_bwd")
        tk = scatter("w_o", l, _matmul(s["mm"], dy, "tn", BF16, "mm_o_dw"))
        dm = _matmul(dy, s["wo"], "nt", F32, "mm_o_dx")
        dya, dyb, dga, dgb = _merge_bwd(dm, s["ya"], s["yb"], s["p"], "merge_bwd")
        tk = tk + scatter("w_og", l, _matmul(s["za"], dya, "tn", BF16, "mm_out_gla_dw"))
        dza = _matmul(dya, s["wog"], "nt", F32, "mm_out_gla_dx")
        do, dr, dgng = _gla_out_bwd(dza, s["o"], s["p"], s["gng"] + tk, "gla_out_bwd")
        tk = scatter("w_oc", l, _matmul(s["zb"], dyb, "tn", BF16, "mm_out_conv_dw"))
        dzb = _matmul(dyb, s["woc"], "nt", F32, "mm_out_conv_dx")
        dcb, dcc, dcx, dcw = _conv_bwd(dzb, s["p"], s["cw8"] + tk, "conv_bwd")
        dq, dk, dv, dlr, dwa, dba = _gla_bwd(do, s["p"], s["st"], s["wa"], s["ba"], "gla_bwd")
        dp = jnp.concatenate([dq, dk, dv, dr, dcb, dcc, dcx, dga, dgb, dlr], axis=1)
        dw_in = _matmul(s["h"], dp, "tn", BF16, "mm_in_dw", tm=512, tn=1152, tk=2048)
        tk = scatter("w_in", l, dw_in) if l > 0 else 0.0
        dh =_matmul(dp, s["wi"], "nt", F32, "mm_in_dx", tn=1024, tk=1152)
        dx, dsh1, dsc1, dgn0 = _pre_bwd(dh, s["xin"], dx1, gn[0] + tk, sc1, "pre_bwd")
        sm["dmod"][l] = jnp.concatenate([dsh1, dsc1, dg1, dsh2, dsc2, dg2], axis=1)[0]
        sm["norm_g"][l] = jnp.concatenate([dgn0, dgn1, dgn2, dgn3], axis=0)
        sm["w_a2"][l] = dwa[:GLA_LOWRANK]
        sm["b_a2"][l] = dba[0]
        sm["gng"][l] = dgng[0]
        sm["conv_w"][l] = dcw[:3]
        sm["ffn_w"][l] = dfw[:3]
    grad_x = dx[None]

    names = ("dmod", "norm_g", "w_a2", "b_a2", "gng", "conv_w", "ffn_w")
    parts = [jnp.stack(sm[k]).reshape(-1) for k in names]
    shapes = [jnp.stack(sm[k]).shape for k in names]
    sizes = [a.shape[0] for a in parts]
    flat = jnp.concatenate(parts)
    rows = -(-flat.shape[0] // 1024) * 8
    flat = jnp.concatenate([flat, jnp.zeros((rows * 128 - flat.shape[0],), F32)]).reshape(rows, 128)
    gath, tot = _allgather8(flat, "reduce_small")
    tk = scatter("w_in", 0, dw_in + (0.0 * tot[0, 0]).astype(BF16))
    c16 = c16 + tk
    po = [0]
    for s_ in sizes:
        po.append(po[-1] + s_)
    tot = tot.reshape(-1)
    tot_of = {k: tot[po[i]:po[i + 1]].reshape(shapes[i]) for i, k in enumerate(names)}
    dmod_all = gath.reshape(N_DEV, -1)[:, po[0]:po[1]].reshape(N_DEV, DEPTH, 6 * D_MODEL)

    def chip_cols(a, width):
        return lax.dynamic_slice_in_dim(a, chip * width, width, axis=a.ndim - 1)

    dml = jnp.transpose(chip_cols(dmod_all, 3072), (1, 0, 2))
    dml = jnp.concatenate([dml, jnp.zeros_like(dml)], axis=1)
    g_w_ada = _ada_bwd(c16, dml, "ada_bwd")

    def upd(w, m, v, ga, gb, name):
        sh = w.shape
        as3 = sh if len(sh) == 3 else (1,) + sh
        outs = _adamw(w.reshape(as3), m.reshape(as3), v.reshape(as3), ga.reshape(as3),
                      None if gb is None else gb.reshape(as3), name)
        return [a.reshape(sh) for a in outs]

    res = {}
    res["w_ada"] = upd(w_ada, m_w_ada, v_w_ada, g_w_ada, None, "adamw")
    res["b_ada"] = upd(b_ada, m_b_ada, v_b_ada, tot_of["dmod"], None, "adamw")
    res["norm_g"] = upd(norm_g, m_norm_g, v_norm_g, chip_cols(tot_of["norm_g"], 512), None, "adamw")
    res["w_a2"] = upd(w_a2, m_w_a2, v_w_a2, chip_cols(tot_of["w_a2"], 128), None, "adamw")
    res["b_a2"] = upd(b_a2, m_b_a2, v_b_a2, tot_of["b_a2"], None, "adamw")
    res["gla_norm_g"] = upd(gla_norm_g, m_gla_norm_g, v_gla_norm_g, tot_of["gng"], None, "adamw")
    res["conv_mix_w"] = upd(conv_mix_w, m_conv_mix_w, v_conv_mix_w, chip_cols(tot_of["conv_w"], 256), None, "adamw")
    res["ffn_conv_w"] = upd(ffn_conv_w, m_ffn_conv_w, v_ffn_conv_w, chip_cols(tot_of["ffn_w"], 1408), None, "adamw")

    full_name = dict(w_in="w_in", w_og="w_out_gla", w_oc="w_out_conv", w_o="w_o", w_up="w_up", w_dn="w_down")
    state = dict(w_in=(w_in, m_w_in, v_w_in), w_og=(w_out_gla, m_w_out_gla, v_w_out_gla),
                 w_oc=(w_out_conv, m_w_out_conv, v_w_out_conv), w_o=(w_o, m_w_o, v_w_o),
                 w_up=(w_up, m_w_up, v_w_up), w_dn=(w_down, m_w_down, v_w_down))
    after = res["w_ada"][3]
    for k in ("w_dn", "w_up", "w_o", "w_og", "w_oc", "w_in"):
        done = [_xchg_wait(scatters[k, l], after, True, "scatter_wait_%s_%d" % (k, l)) for l in range(DEPTH)]
        plane = _sum_chips([d[0] for d in done], [d[1] for d in done], chip_arr, "sum_chips")
        other = _sibling_exchange([plane], "sibling_" + k)[0]
        res[full_name[k]] = upd(*state[k], plane, other, "adamw")
        after = res[full_name[k]][3]
    order = ("w_ada", "b_ada", "norm_g", "w_in", "w_a2", "b_a2", "gla_norm_g", "w_out_gla", "conv_mix_w",
             "w_out_conv", "w_o", "w_up", "ffn_conv_w", "w_down")
    return (loss, grad_x, *[res[k][0] for k in order], *[res[k][1] for k in order],
            *[res[k][2] for k in order], *[res[k][3] for k in order])
```

```python
import functools
import math

import jax
import jax.numpy as jnp
from jax import lax
from jax.experimental import pallas as pl
from jax.experimental.pallas import tpu as pltpu

F32 = jnp.float32
BF16 = jnp.bfloat16
MESH = pl.DeviceIdType.MESH

D_MODEL = 2048
DEPTH = 2
CHUNK = 64
GLA_HEADS = 4
GLA_DK = 128
GLA_DV = 256
GLA_QK = GLA_HEADS * GLA_DK
GLA_V = GLA_HEADS * GLA_DV
GLA_LOWRANK = 16
GLA_TAU = 16.0
CONV_WIDTH = 1024
D_FF = 5632
EPS = 1e-6
N_IN = 10256
LR_PAD = 128
N_IN_PAD = N_IN - GLA_LOWRANK + LR_PAD
OFF_Q, OFF_K, OFF_V, OFF_R = 0, 512, 1024, 2048
OFF_CB, OFF_CC, OFF_CX, OFF_GA, OFF_GB, OFF_LR = 3072, 4096, 5120, 6144, 8192, 10240

ADAM_LR = 0.001
ADAM_B1 = 0.9
ADAM_B2 = 0.999
ADAM_EPS = 1e-08
ADAM_WD = 0.01
ADAM_STEP = 10

N_CHIPS = 4
N_DEV = 8
VMEM_LIMIT = 56 * 1024 * 1024
TM_ROW = 256
TM_EW = 512
CW_EW = 512
GLA_ROWS = 256


def _params(sem=None):
    return pltpu.CompilerParams(dimension_semantics=sem, vmem_limit_bytes=VMEM_LIMIT)


def _sigmoid(v):
    return 1.0 / (1.0 + jnp.exp(-v))


def _log_sigmoid(v):
    return jnp.minimum(v, 0.0) - jnp.log(1.0 + jnp.exp(-jnp.abs(v)))


_GELU_C = math.sqrt(2.0 / math.pi)


def _gelu(v):
    return 0.5 * v * (1.0 + jnp.tanh(_GELU_C * (v + 0.044715 * v * v * v)))


def _gelu_grad(v):
    t = jnp.tanh(_GELU_C * (v + 0.044715 * v * v * v))
    return 0.5 * (1.0 + t) + 0.5 * v * (1.0 - t * t) * _GELU_C * (1.0 + 3.0 * 0.044715 * v * v)


def _flip(a, d):
    return a + d - 2 * a * d


def _unless(cond):
    return jnp.where(cond, 0.0, 1.0).astype(F32)


def _allgather8(xv, name):
    r, cdim = xv.shape

    def body(x_ref, out_ref, sum_ref, send_sems, recv_sems):
        xi, yi, ci = lax.axis_index("x"), lax.axis_index("y"), lax.axis_index("c")
        me = 4 * xi + 2 * yi + ci
        out_ref[pl.ds(me, 1)] = x_ref[...][None]
        started = []
        for k in range(1, N_DEV):
            px, py, pc = _flip(xi, (k >> 2) & 1), _flip(yi, (k >> 1) & 1), _flip(ci, k & 1)
            cp = pltpu.make_async_remote_copy(
                src_ref=x_ref, dst_ref=out_ref.at[me], send_sem=send_sems.at[k - 1], recv_sem=recv_sems.at[k - 1],
                device_id=(px, py, pc), device_id_type=MESH)
            cp.start()
            started.append((cp, 4 * px + 2 * py + pc, k, (px, py, pc)))
        for cp, peer, k, pid in started:
            cp.wait_send()
            pltpu.make_async_remote_copy(
                src_ref=x_ref, dst_ref=out_ref.at[peer], send_sem=send_sems.at[k - 1], recv_sem=recv_sems.at[k - 1],
                device_id=pid, device_id_type=MESH).wait_recv()
        acc = out_ref[0]
        for d in range(1, N_DEV):
            acc = acc + out_ref[d]
        sum_ref[...] = acc

    return pl.pallas_call(
        body, name=name,
        out_shape=(jax.ShapeDtypeStruct((N_DEV, r, cdim), F32), jax.ShapeDtypeStruct((r, cdim), F32)),
        in_specs=[pl.BlockSpec(memory_space=pltpu.VMEM)],
        out_specs=(pl.BlockSpec(memory_space=pltpu.VMEM), pl.BlockSpec(memory_space=pltpu.VMEM)),
        scratch_shapes=[pltpu.SemaphoreType.DMA((N_DEV - 1,)), pltpu.SemaphoreType.DMA((N_DEV - 1,))],
        compiler_params=pltpu.CompilerParams(vmem_limit_bytes=VMEM_LIMIT),
    )(xv)


_HBM = pl.BlockSpec(memory_space=pltpu.HBM)
_SEM = pl.BlockSpec(memory_space=pltpu.SEMAPHORE)
_EFFECT = pltpu.SideEffectType.DATAFLOW_SIDE_EFFECTING
_CHIP_FLIPS = ((1, 0), (0, 1), (1, 1))


def _chip_copies(src_ref, land_ref, send_sems, recv_sems, scatter):
    xi, yi, ci = lax.axis_index("x"), lax.axis_index("y"), lax.axis_index("c")
    me = 2 * xi + yi
    out = []
    for k, (dx, dy) in enumerate(_CHIP_FLIPS):
        px, py = _flip(xi, dx), _flip(yi, dy)
        peer = 2 * px + py
        src = src_ref.at[peer] if scatter else land_ref.at[me]
        mk = functools.partial(pltpu.make_async_remote_copy, src_ref=src, send_sem=send_sems.at[k],
                               recv_sem=recv_sems.at[k], device_id=(px, py, ci), device_id_type=MESH)
        out.append((mk(dst_ref=land_ref.at[me]), mk(dst_ref=land_ref.at[peer])))
    return out


def _gather_start(land, name):
    def body(land_ref, send_sems, recv_sems, land_thru, token):
        for mine, _ in _chip_copies(None, land_ref, send_sems, recv_sems, False):
            mine.start()
        token[...] = jnp.zeros_like(token)

    return pl.pallas_call(
        body, name=name,
        out_shape=(pltpu.SemaphoreType.DMA((3,)), pltpu.SemaphoreType.DMA((3,)), pltpu.HBM(land.shape, land.dtype),
                   jax.ShapeDtypeStruct((8, 128), F32)),
        in_specs=(_HBM,), out_specs=(_SEM, _SEM, _HBM, pl.BlockSpec(memory_space=pltpu.VMEM)),
        input_output_aliases={0: 2},
        compiler_params=pltpu.CompilerParams(has_side_effects=_EFFECT),
    )(pltpu.with_memory_space_constraint(land, pltpu.HBM))


def _gather_wait(handle, after, name):
    send, recv, land_thru = handle

    def body(land_ref, send_sems, recv_sems, after_ref, land_out):
        for mine, theirs in _chip_copies(None, land_ref, send_sems, recv_sems, False):
            mine.wait_send()
            theirs.wait_recv()

    return pl.pallas_call(
        body, name=name, out_shape=pltpu.HBM(land_thru.shape, land_thru.dtype),
        in_specs=(_HBM, _SEM, _SEM, pl.BlockSpec(memory_space=pl.ANY)), out_specs=_HBM,
        input_output_aliases={0: 0},
        compiler_params=pltpu.CompilerParams(has_side_effects=_EFFECT),
    )(land_thru, send, recv, after)


def _scatter_start(src, name):
    def body(src_ref, land_ref, send_sems, recv_sems, src_thru, land_thru, token):
        for mine, _ in _chip_copies(src_ref, land_ref, send_sems, recv_sems, True):
            mine.start()
        token[...] = jnp.zeros_like(token)

    return pl.pallas_call(
        body, name=name,
        out_shape=(pltpu.SemaphoreType.DMA((3,)), pltpu.SemaphoreType.DMA((3,)), pltpu.HBM(src.shape, src.dtype),
                   pltpu.HBM(src.shape, src.dtype), jax.ShapeDtypeStruct((8, 128), F32)),
        in_specs=(_HBM, _HBM), out_specs=(_SEM, _SEM, _HBM, _HBM, pl.BlockSpec(memory_space=pltpu.VMEM)),
        input_output_aliases={0: 2, 1: 3},
        compiler_params=pltpu.CompilerParams(has_side_effects=_EFFECT),
    )(pltpu.with_memory_space_constraint(src, pltpu.HBM),
      pltpu.with_memory_space_constraint(lax.empty(src.shape, src.dtype), pltpu.HBM))


def _scatter_wait(handle, after, name):
    send, recv, src_thru, land_thru = handle

    def body(src_ref, land_ref, send_sems, recv_sems, after_ref, src_out, land_out):
        for mine, theirs in _chip_copies(src_ref, land_ref, send_sems, recv_sems, True):
            mine.wait_send()
            theirs.wait_recv()

    return pl.pallas_call(
        body, name=name,
        out_shape=(pltpu.HBM(src_thru.shape, src_thru.dtype), pltpu.HBM(land_thru.shape, land_thru.dtype)),
        in_specs=(_HBM, _HBM, _SEM, _SEM, pl.BlockSpec(memory_space=pl.ANY)), out_specs=(_HBM, _HBM),
        input_output_aliases={0: 0, 1: 1},
        compiler_params=pltpu.CompilerParams(has_side_effects=_EFFECT),
    )(src_thru, land_thru, send, recv, after)


def _sibling_exchange(arrays, name):
    n = len(arrays)

    def body(*refs):
        ins, outs = refs[:n], refs[n:2 * n]
        send_sems, recv_sems = refs[2 * n:]
        xi, yi, ci = lax.axis_index("x"), lax.axis_index("y"), lax.axis_index("c")
        cps = []
        for i in range(n):
            cp = pltpu.make_async_remote_copy(
                src_ref=ins[i], dst_ref=outs[i], send_sem=send_sems.at[i], recv_sem=recv_sems.at[i],
                device_id=(xi, yi, 1 - ci), device_id_type=MESH)
            cp.start()
            cps.append(cp)
        for cp in cps:
            cp.wait()

    return pl.pallas_call(
        body, name=name, out_shape=tuple(jax.ShapeDtypeStruct(a.shape, a.dtype) for a in arrays),
        in_specs=[pl.BlockSpec(memory_space=pl.ANY)] * n,
        out_specs=tuple(pl.BlockSpec(memory_space=pl.ANY) for _ in range(n)),
        scratch_shapes=[pltpu.SemaphoreType.DMA((n,)), pltpu.SemaphoreType.DMA((n,))],
    )(*arrays)


def _pick(dim, pref):
    if dim <= pref:
        return dim
    t = (pref // 128) * 128
    while t >= 128:
        if dim % t == 0:
            return t
        t -= 128
    return dim


def _matmul(a, b, dims, out_dtype, name, tm=512, tn=1024, tk=2048, out_chips=False, b_chips=False):
    b_shape = (b.shape[1], N_CHIPS * b.shape[2]) if b_chips else b.shape
    if dims == "nn":
        (m, kd), (_, n) = a.shape, b_shape
    elif dims == "nt":
        (m, kd), (n, _) = a.shape, b_shape
    else:
        (kd, m), (_, n) = a.shape, b_shape
    tm = _pick(m, tm)
    tn = _pick(n // N_CHIPS, tn) if (out_chips or (b_chips and dims == "nn")) else _pick(n, tn)
    tk = _pick(kd // N_CHIPS, tk) if (b_chips and dims == "nt") else _pick(kd, tk)
    nk = kd // tk
    if out_chips:
        per_chip = n // N_CHIPS // tn
        out_shape = jax.ShapeDtypeStruct((N_CHIPS, m, n // N_CHIPS), out_dtype)
        out_spec = pl.BlockSpec((None, tm, tn), lambda j, i, k: (j // per_chip, i, j % per_chip))
    else:
        out_shape = jax.ShapeDtypeStruct((m, n), out_dtype)
        out_spec = pl.BlockSpec((tm, tn), lambda j, i, k: (i, j))
    if dims == "nn":
        a_spec = pl.BlockSpec((tm, tk), lambda j, i, k: (i, k))
        b_spec = pl.BlockSpec((tk, tn), lambda j, i, k: (k, j))
        dn = (((1,), (0,)), ((), ()))
    elif dims == "nt":
        a_spec = pl.BlockSpec((tm, tk), lambda j, i, k: (i, k))
        b_spec = pl.BlockSpec((tn, tk), lambda j, i, k: (j, k))
        dn = (((1,), (1,)), ((), ()))
    else:
        a_spec = pl.BlockSpec((tk, tm), lambda j, i, k: (k, i))
        b_spec = pl.BlockSpec((tk, tn), lambda j, i, k: (k, j))
        dn = (((0,), (0,)), ((), ()))
    if b_chips and dims == "nn":
        nper = n // N_CHIPS // tn
        b_spec = pl.BlockSpec((None, tk, tn), lambda j, i, k: (j // nper, k, j % nper))
    elif b_chips:
        kper = kd // N_CHIPS // tk
        b_spec = pl.BlockSpec((None, tn, tk), lambda j, i, k: (k // kper, j, k % kper))

    def body(a_ref, b_ref, o_ref, acc_ref):
        part = lax.dot_general(a_ref[...].astype(BF16), b_ref[...].astype(BF16), dn, preferred_element_type=F32)
        if nk == 1:
            o_ref[...] = part.astype(o_ref.dtype)
        else:
            k = pl.program_id(2)

            @pl.when(k == 0)
            def _():
                acc_ref[...] = part

            @pl.when(k > 0)
            def _():
                acc_ref[...] += part

            @pl.when(k == nk - 1)
            def _():
                o_ref[...] = acc_ref[...].astype(o_ref.dtype)

    return pl.pallas_call(
        body, name=name, out_shape=out_shape,
        grid=(n // tn, m // tm, nk),
        in_specs=[a_spec, b_spec],
        out_specs=out_spec,
        scratch_shapes=[pltpu.VMEM((tm, tn), F32)],
        compiler_params=_params(("parallel", "parallel", "arbitrary")),
    )(a, b)


def _rstd(v):
    return lax.rsqrt(jnp.mean(v * v, axis=-1, keepdims=True) + EPS)


def _row(tm):
    return pl.BlockSpec((tm, D_MODEL), lambda i: (i, 0))


_VEC = pl.BlockSpec((1, D_MODEL), lambda i: (0, 0))


def _pre_norm(x, gn, sc, sh, name):
    t = x.shape[0]
    tm = min(TM_ROW, t)

    def body(x_ref, gn_ref, sc_ref, sh_ref, h_ref):
        xv = x_ref[...]
        h_ref[...] = ((xv * _rstd(xv) * gn_ref[...]) * (1.0 + sc_ref[...]) + sh_ref[...]).astype(BF16)

    return pl.pallas_call(
        body, name=name, out_shape=jax.ShapeDtypeStruct((t, D_MODEL), BF16), grid=(t // tm,),
        in_specs=[_row(tm), _VEC, _VEC, _VEC], out_specs=_row(tm),
        compiler_params=_params(("parallel",)),
    )(x, gn, sc, sh)


def _post_pre(x, y, g, gnp, gn, sc, sh, name):
    t = x.shape[0]
    tm = min(TM_ROW, t)

    def body(x_ref, y_ref, g_ref, gnp_ref, gn_ref, sc_ref, sh_ref, x1_ref, h_ref):
        yv = y_ref[...]
        x1 = x_ref[...] + g_ref[...] * (yv * _rstd(yv) * gnp_ref[...])
        x1_ref[...] = x1
        h_ref[...] = ((x1 * _rstd(x1) * gn_ref[...]) * (1.0 + sc_ref[...]) + sh_ref[...]).astype(BF16)

    return pl.pallas_call(
        body, name=name,
        out_shape=(jax.ShapeDtypeStruct((t, D_MODEL), F32), jax.ShapeDtypeStruct((t, D_MODEL), BF16)),
        grid=(t // tm,),
        in_specs=[_row(tm), _row(tm), _VEC, _VEC, _VEC, _VEC, _VEC], out_specs=(_row(tm), _row(tm)),
        compiler_params=_params(("parallel",)),
    )(x, y, g, gnp, gn, sc, sh)


def _post_loss(x, y, g, gnp, tgt, name):
    t = x.shape[0]
    tm = min(TM_ROW, t)

    def body(x_ref, y_ref, g_ref, gnp_ref, t_ref, dx_ref, loss_ref):
        yv = y_ref[...]
        diff = x_ref[...] + g_ref[...] * (yv * _rstd(yv) * gnp_ref[...]) - t_ref[...]
        dx_ref[...] = diff * (1.0 / D_MODEL)
        part = (0.5 / D_MODEL) * jnp.sum(jnp.sum(diff * diff, axis=-1, keepdims=True), axis=0, keepdims=True)

        @pl.when(pl.program_id(0) == 0)
        def _():
            loss_ref[...] = jnp.zeros_like(loss_ref)

        loss_ref[...] += jnp.broadcast_to(part, loss_ref.shape)

    return pl.pallas_call(
        body, name=name,
        out_shape=(jax.ShapeDtypeStruct((t, D_MODEL), F32), jax.ShapeDtypeStruct((8, 128), F32)),
        grid=(t // tm,),
        in_specs=[_row(tm), _row(tm), _VEC, _VEC, _row(tm)],
        out_specs=(_row(tm), pl.BlockSpec((8, 128), lambda i: (0, 0))),
        compiler_params=_params(("arbitrary",)),
    )(x, y, g, gnp, tgt)


def _acc_rows(ref, val):
    @pl.when(pl.program_id(0) == 0)
    def _():
        ref[...] = jnp.zeros_like(ref)

    ref[...] += jnp.sum(val, axis=0, keepdims=True)


def _post_bwd(dxn, y, g, gnp, name):
    t = y.shape[0]
    tm = min(TM_ROW, t)

    def body(dx_ref, y_ref, g_ref, gnp_ref, dy_ref, dg_ref, dgn_ref):
        yv, dxv = y_ref[...], dx_ref[...]
        r = _rstd(yv)
        yh = yv * r
        _acc_rows(dg_ref, dxv * (yh * gnp_ref[...]))
        dn = dxv * g_ref[...]
        _acc_rows(dgn_ref, dn * yh)
        dyh = dn * gnp_ref[...]
        dy_ref[...] = (r * (dyh - yh * jnp.mean(dyh * yh, axis=-1, keepdims=True))).astype(BF16)

    return pl.pallas_call(
        body, name=name,
        out_shape=(jax.ShapeDtypeStruct((t, D_MODEL), BF16), jax.ShapeDtypeStruct((1, D_MODEL), F32),
                   jax.ShapeDtypeStruct((1, D_MODEL), F32)),
        grid=(t // tm,),
        in_specs=[_row(tm), _row(tm), _VEC, _VEC], out_specs=(_row(tm), _VEC, _VEC),
        compiler_params=_params(("arbitrary",)),
    )(dxn, y, g, gnp)


def _pre_bwd(dh, xin, dres, gn, sc, name):
    t = xin.shape[0]
    tm = min(TM_ROW, t)

    def body(dh_ref, x_ref, dres_ref, gn_ref, sc_ref, dx_ref, dsh_ref, dsc_ref, dgn_ref):
        xv, dhv = x_ref[...], dh_ref[...]
        r = _rstd(xv)
        xh = xv * r
        _acc_rows(dsh_ref, dhv)
        _acc_rows(dsc_ref, dhv * (xh * gn_ref[...]))
        dn = dhv * (1.0 + sc_ref[...])
        _acc_rows(dgn_ref, dn * xh)
        dxh = dn * gn_ref[...]
        dx_ref[...] = dres_ref[...] + r * (dxh - xh * jnp.mean(dxh * xh, axis=-1, keepdims=True))

    vec = jax.ShapeDtypeStruct((1, D_MODEL), F32)
    return pl.pallas_call(
        body, name=name, out_shape=(jax.ShapeDtypeStruct((t, D_MODEL), F32), vec, vec, vec),
        grid=(t // tm,),
        in_specs=[_row(tm), _row(tm), _row(tm), _VEC, _VEC], out_specs=(_row(tm), _VEC, _VEC, _VEC),
        compiler_params=_params(("arbitrary",)),
    )(dh, xin, dres, gn, sc)


def _shift_down(v, halo, s):
    tm = v.shape[0]
    out = pltpu.roll(v, s, 0)
    row = lax.broadcasted_iota(jnp.int32, v.shape, 0)
    for j in range(s):
        out = jnp.where(row == j, jnp.broadcast_to(halo[8 - s + j:8 - s + j + 1, :], v.shape), out)
    return out


def _shift_up(v, halo, s):
    tm = v.shape[0]
    out = pltpu.roll(v, tm - s, 0)
    row = lax.broadcasted_iota(jnp.int32, v.shape, 0)
    for j in range(s):
        out = jnp.where(row == tm - s + j, jnp.broadcast_to(halo[j:j + 1, :], v.shape), out)
    return out


def _tile_specs(tm, cw, off, nrow):
    ob = off // cw
    r8 = tm // 8
    main = pl.BlockSpec((tm, cw), lambda j, i: (i, ob + j))
    prev = pl.BlockSpec((8, cw), lambda j, i: (jnp.maximum(i * r8 - 1, 0), ob + j))
    nxt = pl.BlockSpec((8, cw), lambda j, i: (jnp.minimum((i + 1) * r8, nrow * r8 - 1), ob + j))
    return main, prev, nxt


def _conv_fwd(p, w, name):
    t = p.shape[0]
    tm, cw = min(TM_EW, t), CW_EW
    nrow = t // tm
    cb_s, _, _ = _tile_specs(tm, cw, OFF_CB, nrow)
    cc_s, cc_p, _ = _tile_specs(tm, cw, OFF_CC, nrow)
    cx_s, cx_p, _ = _tile_specs(tm, cw, OFF_CX, nrow)

    def body(cb_ref, cc_ref, ccp_ref, cx_ref, cxp_ref, w_ref, z_ref):
        u = cc_ref[...] * cx_ref[...]
        uh = ccp_ref[...] * cxp_ref[...] * _unless(pl.program_id(1) == 0)
        wv = w_ref[...]
        conv = wv[2:3, :] * u + wv[1:2, :] * _shift_down(u, uh, 1) + wv[0:1, :] * _shift_down(u, uh, 2)
        z_ref[...] = (cb_ref[...] * conv).astype(BF16)

    return pl.pallas_call(
        body, name=name, out_shape=jax.ShapeDtypeStruct((t, CONV_WIDTH), BF16),
        grid=(CONV_WIDTH // cw, nrow),
        in_specs=[cb_s, cc_s, cc_p, cx_s, cx_p, pl.BlockSpec((8, cw), lambda j, i: (0, j))],
        out_specs=pl.BlockSpec((tm, cw), lambda j, i: (i, j)),
        compiler_params=_params(("parallel", "arbitrary")),
    )(p, p, p, p, p, w)


def _acc_w(ref, vals):
    @pl.when(pl.program_id(1) == 0)
    def _():
        ref[...] = jnp.zeros_like(ref)

    for j, v in enumerate(vals):
        ref[j:j + 1, :] += jnp.sum(v, axis=0, keepdims=True)


def _conv_bwd(dz, p, w, name):
    t = p.shape[0]
    tm, cw = min(TM_EW, t), CW_EW
    nrow = t // tm
    dz_s, _, dz_n = _tile_specs(tm, cw, 0, nrow)
    cb_s, _, cb_n = _tile_specs(tm, cw, OFF_CB, nrow)
    cc_s, cc_p, _ = _tile_specs(tm, cw, OFF_CC, nrow)
    cx_s, cx_p, _ = _tile_specs(tm, cw, OFF_CX, nrow)

    def body(dz_ref, dzn_ref, cb_ref, cbn_ref, cc_ref, ccp_ref, cx_ref, cxp_ref, w_ref,
             dcb_ref, dcc_ref, dcx_ref, dw_ref):
        i = pl.program_id(1)
        ccv, cxv, dzv = cc_ref[...], cx_ref[...], dz_ref[...]
        u = ccv * cxv
        uh = ccp_ref[...] * cxp_ref[...] * _unless(i == 0)
        wv = w_ref[...]
        u1, u2 = _shift_down(u, uh, 1), _shift_down(u, uh, 2)
        conv = wv[2:3, :] * u + wv[1:2, :] * u1 + wv[0:1, :] * u2
        dcb_ref[...] = (dzv * conv).astype(BF16)
        dconv = dzv * cb_ref[...]
        dch = dzn_ref[...] * cbn_ref[...] * _unless(i == nrow - 1)
        du = wv[2:3, :] * dconv + wv[1:2, :] * _shift_up(dconv, dch, 1) + wv[0:1, :] * _shift_up(dconv, dch, 2)
        dcc_ref[...] = (du * cxv).astype(BF16)
        dcx_ref[...] = (du * ccv).astype(BF16)
        _acc_w(dw_ref, (dconv * u2, dconv * u1, dconv * u))

    o_s = pl.BlockSpec((tm, cw), lambda j, i: (i, j))
    o_sh = jax.ShapeDtypeStruct((t, CONV_WIDTH), BF16)
    w_s = pl.BlockSpec((8, cw), lambda j, i: (0, j))
    return pl.pallas_call(
        body, name=name, out_shape=(o_sh, o_sh, o_sh, jax.ShapeDtypeStruct((8, CONV_WIDTH), F32)),
        grid=(CONV_WIDTH // cw, nrow),
        in_specs=[dz_s, dz_n, cb_s, cb_n, cc_s, cc_p, cx_s, cx_p, w_s],
        out_specs=(o_s, o_s, o_s, w_s),
        compiler_params=_params(("parallel", "arbitrary")),
    )(dz, dz, p, p, p, p, p, p, w)


def _ffn_fwd(u, w, name):
    t = u.shape[0]
    tm, cw = min(TM_EW, t), CW_EW
    nrow = t // tm
    g_s, g_p, _ = _tile_specs(tm, cw, 0, nrow)
    u_s, _, _ = _tile_specs(tm, cw, D_FF, nrow)

    def body(g_ref, gp_ref, u_ref, w_ref, f_ref):
        gv = g_ref[...]
        gh = gp_ref[...] * _unless(pl.program_id(1) == 0)
        wv = w_ref[...]
        gc = wv[2:3, :] * gv + wv[1:2, :] * _shift_down(gv, gh, 1) + wv[0:1, :] * _shift_down(gv, gh, 2)
        f_ref[...] = (_gelu(gc) * u_ref[...]).astype(BF16)

    return pl.pallas_call(
        body, name=name, out_shape=jax.ShapeDtypeStruct((t, D_FF), BF16),
        grid=(D_FF // cw, nrow),
        in_specs=[g_s, g_p, u_s, pl.BlockSpec((8, cw), lambda j, i: (0, j))],
        out_specs=pl.BlockSpec((tm, cw), lambda j, i: (i, j)),
        compiler_params=_params(("parallel", "arbitrary")),
    )(u, u, u, w)


def _ffn_bwd(df, u, w, name):
    t = u.shape[0]
    tm, cw = min(TM_EW, t), CW_EW
    nrow = t // tm
    df_s, _, df_n = _tile_specs(tm, cw, 0, nrow)
    g_s, g_p, g_n = _tile_specs(tm, cw, 0, nrow)
    u_s, _, u_n = _tile_specs(tm, cw, D_FF, nrow)
    r8 = tm // 8

    def body(df_ref, dfn_ref, g_ref, gp_ref, gn_ref, u_ref, un_ref, w_ref, dg_ref, du_ref, dw_ref):
        i = pl.program_id(1)
        gv, dfv, uv = g_ref[...], df_ref[...], u_ref[...]
        gh = gp_ref[...] * _unless(i == 0)
        wv = w_ref[...]
        g1, g2 = _shift_down(gv, gh, 1), _shift_down(gv, gh, 2)
        gc = wv[2:3, :] * gv + wv[1:2, :] * g1 + wv[0:1, :] * g2
        du_ref[...] = (dfv * _gelu(gc)).astype(BF16)
        dgc = dfv * uv * _gelu_grad(gc)
        gnv = gn_ref[...]
        gtail = gv[tm - 8:tm, :]
        gcn = (wv[2:3, :] * gnv + wv[1:2, :] * _shift_down(gnv, gtail, 1) + wv[0:1, :] * _shift_down(gnv, gtail, 2))
        dgcn = dfn_ref[...] * un_ref[...] * _gelu_grad(gcn) * _unless(i == nrow - 1)
        dg = wv[2:3, :] * dgc + wv[1:2, :] * _shift_up(dgc, dgcn, 1) + wv[0:1, :] * _shift_up(dgc, dgcn, 2)
        dg_ref[...] = dg.astype(BF16)
        _acc_w(dw_ref, (dgc * g2, dgc * g1, dgc * gv))

    o_s = pl.BlockSpec((tm, cw), lambda j, i: (i, j))
    o_sh = jax.ShapeDtypeStruct((t, D_FF), BF16)
    w_s = pl.BlockSpec((8, cw), lambda j, i: (0, j))
    return pl.pallas_call(
        body, name=name, out_shape=(o_sh, o_sh, jax.ShapeDtypeStruct((8, D_FF), F32)),
        grid=(D_FF // cw, nrow),
        in_specs=[df_s, df_n, g_s, g_p, g_n, u_s, u_n, w_s],
        out_specs=(o_s, o_s, w_s),
        compiler_params=_params(("parallel", "arbitrary")),
    )(df, df, u, u, u, u, u, w)


def _merge_fwd(ya, yb, p, name):
    t = ya.shape[0]
    tm, cw = min(TM_EW, t), CW_EW
    y_s = pl.BlockSpec((tm, cw), lambda i, j: (i, j))

    def body(ya_ref, yb_ref, ga_ref, gb_ref, m_ref):
        m_ref[...] = (_sigmoid(ga_ref[...]) * ya_ref[...] + _sigmoid(gb_ref[...]) * yb_ref[...]).astype(BF16)

    return pl.pallas_call(
        body, name=name, out_shape=jax.ShapeDtypeStruct((t, D_MODEL), BF16),
        grid=(t // tm, D_MODEL // cw),
        in_specs=[y_s, y_s, pl.BlockSpec((tm, cw), lambda i, j: (i, OFF_GA // cw + j)),
                  pl.BlockSpec((tm, cw), lambda i, j: (i, OFF_GB // cw + j))],
        out_specs=y_s, compiler_params=_params(("parallel", "parallel")),
    )(ya, yb, p, p)


def _merge_bwd(dm, ya, yb, p, name):
    t = ya.shape[0]
    tm, cw = min(TM_EW, t), CW_EW
    y_s = pl.BlockSpec((tm, cw), lambda i, j: (i, j))

    def body(dm_ref, ya_ref, yb_ref, ga_ref, gb_ref, dya_ref, dyb_ref, dga_ref, dgb_ref):
        dmv = dm_ref[...]
        sa, sb = _sigmoid(ga_ref[...]), _sigmoid(gb_ref[...])
        dya_ref[...] = (dmv * sa).astype(BF16)
        dyb_ref[...] = (dmv * sb).astype(BF16)
        dga_ref[...] = (dmv * ya_ref[...] * sa * (1.0 - sa)).astype(BF16)
        dgb_ref[...] = (dmv * yb_ref[...] * sb * (1.0 - sb)).astype(BF16)

    o_sh = jax.ShapeDtypeStruct((t, D_MODEL), BF16)
    return pl.pallas_call(
        body, name=name, out_shape=(o_sh, o_sh, o_sh, o_sh),
        grid=(t // tm, D_MODEL // cw),
        in_specs=[y_s, y_s, y_s, pl.BlockSpec((tm, cw), lambda i, j: (i, OFF_GA // cw + j)),
                  pl.BlockSpec((tm, cw), lambda i, j: (i, OFF_GB // cw + j))],
        out_specs=(y_s, y_s, y_s, y_s), compiler_params=_params(("parallel", "parallel")),
    )(dm, ya, yb, p, p)


def _tri(lower):
    r = lax.broadcasted_iota(jnp.int32, (CHUNK, CHUNK), 0)
    c = lax.broadcasted_iota(jnp.int32, (CHUNK, CHUNK), 1)
    return ((c <= r) if lower else (c >= r)).astype(F32)


def _eye_mask():
    r = lax.broadcasted_iota(jnp.int32, (GLA_DK, GLA_DK), 0)
    c = lax.broadcasted_iota(jnp.int32, (GLA_DK, GLA_DK), 1)
    return r == c


def _row_to_col(v):
    return jnp.sum(jnp.where(_eye_mask(), jnp.broadcast_to(v, (GLA_DK, GLA_DK)), 0.0), axis=1, keepdims=True)


def _col_to_row(v):
    return jnp.sum(jnp.where(_eye_mask(), jnp.broadcast_to(v, (GLA_DK, GLA_DK)), 0.0), axis=0, keepdims=True)


def _dot(a, b, dn):
    return lax.dot_general(a.astype(BF16), b.astype(BF16), (dn, ((), ())), preferred_element_type=F32)


_NN = ((1,), (0,))
_NT = ((1,), (1,))
_TN = ((0,), (0,))


def _gate_logits(lr_ref, wa_ref, ba_ref):
    return _dot(lr_ref[...], wa_ref[...], _NN) + ba_ref[...]


def _chunk_decay(la, tri):
    cum = lax.dot_general(tri, la, ((_NN), ((), ())), precision=lax.Precision.HIGHEST, preferred_element_type=F32)
    e = cum[CHUNK - 1:CHUNK, :]
    return cum, e, jnp.exp(e - cum)


def _gla_fwd(p, wa, ba, name):
    t = p.shape[0]
    rows = min(GLA_ROWS, t)
    cb = rows // CHUNK
    nc = t // CHUNK
    scale = GLA_DK ** -0.5

    def body(q_ref, k_ref, v_ref, lr_ref, wa_ref, ba_ref, o_ref, st_ref, s_scr):
        @pl.when(pl.program_id(0) == 0)
        def _():
            s_scr[...] = jnp.zeros_like(s_scr)

        la_all = _log_sigmoid(_gate_logits(lr_ref, wa_ref, ba_ref)) * (1.0 / GLA_TAU)
        tri = _tri(True)
        for ch in range(cb):
            rs = slice(ch * CHUNK, (ch + 1) * CHUNK)
            for h in range(GLA_HEADS):
                ks = slice(h * GLA_DK, (h + 1) * GLA_DK)
                vs = slice(h * GLA_DV, (h + 1) * GLA_DV)
                _, e, w = _chunk_decay(la_all[rs, ks], tri)
                kd = k_ref[rs, ks] * w
                s_new = _row_to_col(jnp.exp(e)) * s_scr[ks, :] + _dot(kd, v_ref[rs, vs], _TN)
                s_scr[ks, :] = s_new
                st_ref[ch, ks, :] = s_new
                o_ref[rs, vs] = _dot(q_ref[rs, ks] * scale, s_new, _NN)

    return pl.pallas_call(
        body, name=name,
        out_shape=(jax.ShapeDtypeStruct((t, GLA_V), F32), jax.ShapeDtypeStruct((nc, GLA_QK, GLA_DV), F32)),
        grid=(t // rows,),
        in_specs=[pl.BlockSpec((rows, GLA_QK), lambda i: (i, OFF_Q // GLA_QK)),
                  pl.BlockSpec((rows, GLA_QK), lambda i: (i, OFF_K // GLA_QK)),
                  pl.BlockSpec((rows, GLA_V), lambda i: (i, OFF_V // GLA_V)),
                  pl.BlockSpec((rows, LR_PAD), lambda i: (i, OFF_LR // LR_PAD)),
                  pl.BlockSpec((LR_PAD, GLA_QK), lambda i: (0, 0)),
                  pl.BlockSpec((1, GLA_QK), lambda i: (0, 0))],
        out_specs=(pl.BlockSpec((rows, GLA_V), lambda i: (i, 0)),
                   pl.BlockSpec((cb, GLA_QK, GLA_DV), lambda i: (i, 0, 0))),
        scratch_shapes=[pltpu.VMEM((GLA_QK, GLA_DV), F32)],
        compiler_params=_params(("arbitrary",)),
    )(p, p, p, p, wa, ba)


def _gla_bwd(do, p, st, wa, ba, name):
    t = p.shape[0]
    rows = min(GLA_ROWS, t)
    cb = rows // CHUNK
    nb = t // rows
    scale = GLA_DK ** -0.5

    def rev(i):
        return nb - 1 - i

    def body(do_ref, q_ref, k_ref, v_ref, lr_ref, st_ref, stp_ref, wa_ref, ba_ref,
             dq_ref, dk_ref, dv_ref, dlr_ref, dwa_ref, dba_ref, ds_scr, dz_scr):
        i = pl.program_id(0)

        @pl.when(i == 0)
        def _():
            ds_scr[...] = jnp.zeros_like(ds_scr)
            dwa_ref[...] = jnp.zeros_like(dwa_ref)
            dba_ref[...] = jnp.zeros_like(dba_ref)

        z_all = _gate_logits(lr_ref, wa_ref, ba_ref)
        la_all = _log_sigmoid(z_all) * (1.0 / GLA_TAU)
        tri, triu = _tri(True), _tri(False)
        last_row = lax.broadcasted_iota(jnp.int32, (CHUNK, GLA_DK), 0) == CHUNK - 1
        keep_prev = _unless(i == nb - 1)
        for ch in reversed(range(cb)):
            rs = slice(ch * CHUNK, (ch + 1) * CHUNK)
            for h in range(GLA_HEADS):
                ks = slice(h * GLA_DK, (h + 1) * GLA_DK)
                vs = slice(h * GLA_DV, (h + 1) * GLA_DV)
                _, e, w = _chunk_decay(la_all[rs, ks], tri)
                kd = k_ref[rs, ks] * w
                exp_e = jnp.exp(e)
                s_c = st_ref[ch, ks, :]
                if ch > 0:
                    s_p = st_ref[ch - 1, ks, :]
                else:
                    s_p = stp_ref[0, ks, :] * keep_prev
                do_c = do_ref[rs, vs]
                vv = v_ref[rs, vs]
                ds_tot = ds_scr[ks, :] + _dot(q_ref[rs, ks] * scale, do_c, _TN)
                dq_ref[rs, ks] = (_dot(do_c, s_c, _NT) * scale).astype(BF16)
                dkd = _dot(vv, ds_tot, _NT)
                dv_ref[rs, vs] = _dot(kd, ds_tot, _NN).astype(BF16)
                dexp_col = jnp.sum(ds_tot * s_p, axis=1, keepdims=True)
                ds_scr[ks, :] = _row_to_col(exp_e) * ds_tot
                dk_ref[rs, ks] = (dkd * w).astype(BF16)
                dwt = dkd * kd
                de = jnp.sum(dwt, axis=0, keepdims=True) + _col_to_row(dexp_col) * exp_e
                dcum = jnp.where(last_row, de - dwt, -dwt)
                da = lax.dot_general(triu, dcum, (_NN, ((), ())), precision=lax.Precision.HIGHEST,
                                     preferred_element_type=F32)
                dz_scr[rs, ks] = da * (1.0 / GLA_TAU) * _sigmoid(-z_all[rs, ks])
        dz = dz_scr[...]
        dlr_ref[...] = _dot(dz, wa_ref[...], _NT).astype(BF16)
        dwa_ref[...] += _dot(lr_ref[...], dz, _TN)
        dba_ref[...] += jnp.sum(dz, axis=0, keepdims=True)

    qk_sh = jax.ShapeDtypeStruct((t, GLA_QK), BF16)
    return pl.pallas_call(
        body, name=name,
        out_shape=(qk_sh, qk_sh, jax.ShapeDtypeStruct((t, GLA_V), BF16), jax.ShapeDtypeStruct((t, LR_PAD), BF16),
                   jax.ShapeDtypeStruct((LR_PAD, GLA_QK), F32), jax.ShapeDtypeStruct((1, GLA_QK), F32)),
        grid=(nb,),
        in_specs=[pl.BlockSpec((rows, GLA_V), lambda i: (rev(i), 0)),
                  pl.BlockSpec((rows, GLA_QK), lambda i: (rev(i), OFF_Q // GLA_QK)),
                  pl.BlockSpec((rows, GLA_QK), lambda i: (rev(i), OFF_K // GLA_QK)),
                  pl.BlockSpec((rows, GLA_V), lambda i: (rev(i), OFF_V // GLA_V)),
                  pl.BlockSpec((rows, LR_PAD), lambda i: (rev(i), OFF_LR // LR_PAD)),
                  pl.BlockSpec((cb, GLA_QK, GLA_DV), lambda i: (rev(i), 0, 0)),
                  pl.BlockSpec((1, GLA_QK, GLA_DV), lambda i: (jnp.maximum(rev(i) * cb - 1, 0), 0, 0)),
                  pl.BlockSpec((LR_PAD, GLA_QK), lambda i: (0, 0)),
                  pl.BlockSpec((1, GLA_QK), lambda i: (0, 0))],
        out_specs=(pl.BlockSpec((rows, GLA_QK), lambda i: (rev(i), 0)),
                   pl.BlockSpec((rows, GLA_QK), lambda i: (rev(i), 0)),
                   pl.BlockSpec((rows, GLA_V), lambda i: (rev(i), 0)),
                   pl.BlockSpec((rows, LR_PAD), lambda i: (rev(i), 0)),
                   pl.BlockSpec((LR_PAD, GLA_QK), lambda i: (0, 0)),
                   pl.BlockSpec((1, GLA_QK), lambda i: (0, 0))),
        scratch_shapes=[pltpu.VMEM((GLA_QK, GLA_DV), F32), pltpu.VMEM((rows, GLA_QK), F32)],
        compiler_params=_params(("arbitrary",)),
    )(do, p, p, p, p, st, st, wa, ba)


def _gla_out_fwd(o, p, gng, name):
    t = o.shape[0]
    tm = min(TM_EW, t)

    def body(o_ref, r_ref, g_ref, z_ref):
        gv = g_ref[...]
        for h in range(GLA_HEADS):
            vs = slice(h * GLA_DV, (h + 1) * GLA_DV)
            ov, rv = o_ref[:, vs], r_ref[:, vs]
            z_ref[:, vs] = ((ov * _rstd(ov) * gv) * (rv * _sigmoid(rv))).astype(BF16)

    return pl.pallas_call(
        body, name=name, out_shape=jax.ShapeDtypeStruct((t, GLA_V), BF16), grid=(t // tm,),
        in_specs=[pl.BlockSpec((tm, GLA_V), lambda i: (i, 0)),
                  pl.BlockSpec((tm, GLA_V), lambda i: (i, OFF_R // GLA_V)),
                  pl.BlockSpec((1, GLA_DV), lambda i: (0, 0))],
        out_specs=pl.BlockSpec((tm, GLA_V), lambda i: (i, 0)),
        compiler_params=_params(("parallel",)),
    )(o, p, gng)


def _gla_out_bwd(dz, o, p, gng, name):
    t = o.shape[0]
    tm = min(TM_EW, t)

    def body(dz_ref, o_ref, r_ref, g_ref, do_ref, dr_ref, dg_ref):
        @pl.when(pl.program_id(0) == 0)
        def _():
            dg_ref[...] = jnp.zeros_like(dg_ref)

        gv = g_ref[...]
        for h in range(GLA_HEADS):
            vs = slice(h * GLA_DV, (h + 1) * GLA_DV)
            ov, rv, dzv = o_ref[:, vs], r_ref[:, vs], dz_ref[:, vs]
            rs = _rstd(ov)
            oh = ov * rs
            sg = _sigmoid(rv)
            dr_ref[:, vs] = (dzv * (oh * gv) * (sg * (1.0 + rv * (1.0 - sg)))).astype(BF16)
            don = dzv * (rv * sg)
            dg_ref[...] += jnp.sum(don * oh, axis=0, keepdims=True)
            doh = don * gv
            do_ref[:, vs] = rs * (doh - oh * jnp.mean(doh * oh, axis=-1, keepdims=True))

    row = pl.BlockSpec((tm, GLA_V), lambda i: (i, 0))
    return pl.pallas_call(
        body, name=name,
        out_shape=(jax.ShapeDtypeStruct((t, GLA_V), F32), jax.ShapeDtypeStruct((t, GLA_V), BF16),
                   jax.ShapeDtypeStruct((1, GLA_DV), F32)),
        grid=(t // tm,),
        in_specs=[row, row, pl.BlockSpec((tm, GLA_V), lambda i: (i, OFF_R // GLA_V)),
                  pl.BlockSpec((1, GLA_DV), lambda i: (0, 0))],
        out_specs=(row, row, pl.BlockSpec((1, GLA_DV), lambda i: (0, 0))),
        compiler_params=_params(("arbitrary",)),
    )(dz, o, p, gng)


def _ada_fwd(c_all, w, b, layer, name):
    n = w.shape[2]
    tn = _pick(n, 512)

    def body(c_ref, w_ref, b_ref, o_ref):
        cv = c_ref[...]
        o_ref[...] = _dot(cv * _sigmoid(cv), w_ref[...], _NN) + b_ref[...]

    return pl.pallas_call(
        body, name=name, out_shape=jax.ShapeDtypeStruct((16, n), F32), grid=(n // tn,),
        in_specs=[pl.BlockSpec((16, D_MODEL), lambda j: (0, 0)),
                  pl.BlockSpec((None, D_MODEL, tn), lambda j: (layer, 0, j)),
                  pl.BlockSpec((1, tn), lambda j: (0, j))],
        out_specs=pl.BlockSpec((16, tn), lambda j: (0, j)),
        compiler_params=_params(("parallel",)),
    )(c_all, w, b)


def _ada_bwd(c_all, dmod, name):
    n = dmod.shape[2]
    tn = _pick(n, 512)

    def body(c_ref, d_ref, o_ref):
        cv = c_ref[...]
        o_ref[...] = _dot(cv * _sigmoid(cv), d_ref[...], _TN)

    return pl.pallas_call(
        body, name=name, out_shape=jax.ShapeDtypeStruct((DEPTH, D_MODEL, n), F32), grid=(DEPTH, n // tn),
        in_specs=[pl.BlockSpec((16, D_MODEL), lambda l, j: (0, 0)),
                  pl.BlockSpec((None, 16, tn), lambda l, j: (l, 0, j))],
        out_specs=pl.BlockSpec((None, D_MODEL, tn), lambda l, j: (l, 0, j)),
        compiler_params=_params(("parallel", "parallel")),
    )(c_all, dmod)


def _rows_tile(nrows, ncols, target_bytes):
    want = max(16, target_bytes // (4 * ncols))
    if nrows <= want:
        return nrows
    t = (want // 16) * 16
    while t >= 16:
        if nrows % t == 0:
            return t
        t -= 16
    return nrows


def _sum_chips(sent, landed, chip, name):
    _, nrows, ncols = sent[0].shape
    tr = _rows_tile(nrows, ncols, 2 << 20)
    nblk = nrows // tr

    def body(chip_ref, *refs):
        own, got, o_ref = refs[:DEPTH], refs[DEPTH:2 * DEPTH], refs[2 * DEPTH]
        me = chip_ref[0]
        for l in range(DEPTH):
            for j in range(N_CHIPS):
                def add(val):
                    if j == 0:
                        o_ref[...] = val.astype(F32)
                    else:
                        o_ref[...] += val.astype(F32)

                @pl.when(jnp.logical_and(pl.program_id(0) == l, me == j))
                def _():
                    add(own[l][...])

                @pl.when(jnp.logical_and(pl.program_id(0) == l, me != j))
                def _():
                    add(got[l][j])

    def rows_of(layer):
        return lambda l, i, chip_ref: jnp.where(l == layer, i, 0)

    own_specs = [pl.BlockSpec((None, tr, ncols), lambda l, i, chip_ref, r=rows_of(k): (chip_ref[0], r(l, i, chip_ref), 0))
                 for k in range(DEPTH)]
    got_specs = [pl.BlockSpec((N_CHIPS, tr, ncols), lambda l, i, chip_ref, r=rows_of(k): (0, r(l, i, chip_ref), 0))
                 for k in range(DEPTH)]
    return pl.pallas_call(
        body, name=name, out_shape=jax.ShapeDtypeStruct((DEPTH, nrows, ncols), F32),
        grid_spec=pltpu.PrefetchScalarGridSpec(
            num_scalar_prefetch=1, grid=(DEPTH, nblk), in_specs=own_specs + got_specs,
            out_specs=pl.BlockSpec((None, tr, ncols), lambda l, i, chip_ref: (l, i, 0))),
        compiler_params=_params(("arbitrary", "arbitrary")),
    )(chip, *sent, *landed)


def _adamw(w, m, v, ga, gb, name, tile=None):
    two = gb is not None
    c1 = 1.0 - ADAM_B1 ** ADAM_STEP
    c2 = 1.0 - ADAM_B2 ** ADAM_STEP

    def body(*refs):
        if two:
            w_ref, m_ref, v_ref, ga_ref, gb_ref, g_ref, d_ref, nm_ref, nv_ref = refs
            g = ga_ref[...] + gb_ref[...]
        else:
            w_ref, m_ref, v_ref, ga_ref, g_ref, d_ref, nm_ref, nv_ref = refs
            g = ga_ref[...]
        g_ref[...] = g
        nm = ADAM_B1 * m_ref[...] + (1.0 - ADAM_B1) * g
        nv = ADAM_B2 * v_ref[...] + (1.0 - ADAM_B2) * (g * g)
        nm_ref[...] = nm
        nv_ref[...] = nv
        d_ref[...] = -ADAM_LR * ((nm / c1) / (jnp.sqrt(nv / c2) + ADAM_EPS) + ADAM_WD * w_ref[...])

    if tile is None:
        nl, nrows, ncols = w.shape
        tr = _rows_tile(nrows, ncols, 1 << 20)
        blk = pl.BlockSpec((None, tr, ncols), lambda l, i: (l, i, 0))
        grid = (nl, nrows // tr)
    else:
        nrows, nl, ncols = w.shape
        rb, cb = tile
        blk = pl.BlockSpec((rb, nl, cb), lambda i, j: (i, 0, j))
        grid = (nrows // rb, ncols // cb)
    sh = jax.ShapeDtypeStruct(w.shape, F32)
    ins = [w, m, v, ga] + ([gb] if two else [])
    return pl.pallas_call(
        body, name=name, out_shape=(sh, sh, sh, sh), grid=grid,
        in_specs=[blk] * len(ins), out_specs=(blk, blk, blk, blk),
        compiler_params=_params(("parallel", "parallel")),
    )(*ins)


def _pad_rows(a, rows):
    return jnp.concatenate([a, jnp.zeros((rows - a.shape[0],) + a.shape[1:], a.dtype)], axis=0)


N_IN_CHIP = N_IN // N_CHIPS
_LR_LO = 3072 - N_IN_CHIP
_LR_HI = _LR_LO + GLA_LOWRANK


def _w_in_from_chips(a):
    return jnp.concatenate([a[0], a[1][:, :_LR_LO], a[1][:, _LR_HI:], a[2], a[3], a[1][:, _LR_LO:_LR_HI],
                            jnp.zeros((a.shape[1], LR_PAD - GLA_LOWRANK), a.dtype)], axis=1)


def _w_in_to_chips(w):
    s2 = 2 * N_IN_CHIP - GLA_LOWRANK
    s3 = s2 + N_IN_CHIP
    c1 = jnp.concatenate([w[:, N_IN_CHIP:3072], w[:, OFF_LR:OFF_LR + GLA_LOWRANK], w[:, 3072:s2]], axis=1)
    return jnp.stack([w[:, :N_IN_CHIP], c1, w[:, s2:s3], w[:, s3:OFF_LR]])


_BIG = ("w_in", "w_og", "w_oc", "w_o", "w_up", "w_dn")
_ROW_SHARDED = ("w_o", "w_dn")


def kernel(x, c, w_ada, b_ada, norm_g, w_in, w_a2, b_a2, gla_norm_g, w_out_gla, conv_mix_w, w_out_conv, w_o, w_up, ffn_conv_w, w_down, loss_target, m_w_ada, m_b_ada, m_norm_g, m_w_in, m_w_a2, m_b_a2, m_gla_norm_g, m_w_out_gla, m_conv_mix_w, m_w_out_conv, m_w_o, m_w_up, m_ffn_conv_w, m_w_down, v_w_ada, v_b_ada, v_norm_g, v_w_in, v_w_a2, v_b_a2, v_gla_norm_g, v_w_out_gla, v_conv_mix_w, v_w_out_conv, v_w_o, v_w_up, v_ffn_conv_w, v_w_down):
    xi, yi, ci = lax.axis_index("x"), lax.axis_index("y"), lax.axis_index("c")
    chip = 2 * xi + yi
    dev = 2 * chip + ci
    chip_arr = jnp.reshape(chip, (1,)).astype(jnp.int32)
    xt = x[0]
    tgt = loss_target[0]

    c_all = _allgather8(jnp.broadcast_to(c, (8, D_MODEL)), "gather_c")[0][:, 0, :]
    c16 = _pad_rows(c_all, 16)
    sm_parts = [norm_g.reshape(-1), w_a2.reshape(-1), conv_mix_w.reshape(-1), ffn_conv_w.reshape(-1)]
    sm_sizes = [a.shape[0] for a in sm_parts]
    sm_flat = jnp.concatenate(sm_parts)
    sm_rows = -(-sm_flat.shape[0] // 128)
    sm_rows = -(-sm_rows // 8) * 8
    sm_flat = jnp.concatenate([sm_flat, jnp.zeros((sm_rows * 128 - sm_flat.shape[0],), F32)]).reshape(sm_rows, 128)
    sm_all = _allgather8(sm_flat, "gather_small")[0].reshape(N_DEV, -1)[0::2]
    offs = [0]
    for s in sm_sizes:
        offs.append(offs[-1] + s)

    def small_full(idx, shape):
        a = sm_all[:, offs[idx]:offs[idx + 1]].reshape((N_CHIPS,) + shape)
        a = jnp.moveaxis(a, 0, -2)
        return a.reshape(shape[:-1] + (N_CHIPS * shape[-1],))

    norm_g_f = small_full(0, (DEPTH, 4, 512))
    w_a2_f = small_full(1, (DEPTH, GLA_LOWRANK, 128))
    conv_w_f = small_full(2, (DEPTH, 3, 256))
    ffn_w_f = small_full(3, (DEPTH, 3, 1408))

    b_loc = lax.dynamic_slice(b_ada, (0, chip * 3072), (DEPTH, 3072))
    mod_loc = jnp.concatenate(
        [_ada_fwd(c16, w_ada, b_loc[l:l + 1], l, "ada_fwd")[:8] for l in range(DEPTH)], axis=0)
    mod_all = _allgather8(mod_loc, "gather_mod")[0][0::2]
    mods = []
    for l in range(DEPTH):
        row = lax.dynamic_slice(mod_all, (0, l * 8 + dev, 0), (N_CHIPS, 1, 3072)).reshape(1, 6 * D_MODEL)
        mods.append([row[:, k * D_MODEL:(k + 1) * D_MODEL] for k in range(6)])

    big = dict(w_in=w_in, w_og=w_out_gla, w_oc=w_out_conv, w_o=w_o, w_up=w_up, w_dn=w_down)
    gathers = {}
    tok = 0.0 * (mod_all[0, 0, 0] + sm_all[0, 0])
    for l in range(DEPTH):
        for k in _BIG:
            shard = (big[k][l] + tok).astype(BF16)
            land = lax.dynamic_update_slice(lax.empty((N_CHIPS,) + shard.shape, BF16), shard[None], (chip, 0, 0))
            *handle, token = _gather_start(land, "gather_start_%s_%d" % (k, l))
            gathers[k, l] = tuple(handle)
            tok = token[0, 0]

    def gathered(k, l, after):
        full = _gather_wait(gathers[k, l], after, "gather_wait_%s_%d" % (k, l))
        if k in _ROW_SHARDED:
            return full.reshape(N_CHIPS * full.shape[1], full.shape[2])
        return _w_in_from_chips(full) if k == "w_in" else full

    saved = []
    h = None
    xin = xt
    for l in range(DEPTH):
        sh1, sc1, g1, sh2, sc2, g2 = mods[l]
        gn = [norm_g_f[l, k][None] for k in range(4)]
        wa = _pad_rows(w_a2_f[l], LR_PAD)
        ba = b_a2[l][None]
        gng = gla_norm_g[l][None]
        cw8 = _pad_rows(conv_w_f[l], 8)
        fw8 = _pad_rows(ffn_w_f[l], 8)
        if l == 0:
            h = _pre_norm(xin, gn[0] + tok, sc1, sh1, "pre_norm")
        wi = gathered("w_in", l, h)
        p = _matmul(h, wi, "nn", F32, "mm_in", tn=1152)
        o, st = _gla_fwd(p, wa, ba, "gla_fwd")
        za = _gla_out_fwd(o, p, gng, "gla_out_fwd")
        zb = _conv_fwd(p, cw8, "conv_fwd")
        wog, woc = gathered("w_og", l, zb), gathered("w_oc", l, zb)
        ya = _matmul(za, wog, "nn", F32, "mm_out_gla", b_chips=True)
        yb = _matmul(zb, woc, "nn", F32, "mm_out_conv", b_chips=True)
        mm = _merge_fwd(ya, yb, p, "merge_fwd")
        wo = gathered("w_o", l, mm)
        y = _matmul(mm, wo, "nn", F32, "mm_o")
        x1, h2 = _post_pre(xin, y, g1, gn[1], gn[2], sc2, sh2, "post_pre")
        wup = gathered("w_up", l, h2)
        u = _matmul(h2, wup, "nn", F32, "mm_up", tn=1408, b_chips=True)
        f = _ffn_fwd(u, fw8, "ffn_fwd")
        wdn = gathered("w_dn", l, f)
        y2 = _matmul(f, wdn, "nn", F32, "mm_down")
        saved.append(dict(xin=xin, h=h, p=p, o=o, st=st, za=za, zb=zb, ya=ya, yb=yb, mm=mm, y=y, x1=x1, h2=h2,
                          u=u, f=f, y2=y2, wi=wi, wog=wog, woc=woc, wo=wo, wup=wup, wdn=wdn, wa=wa, ba=ba,
                          gng=gng, cw8=cw8, fw8=fw8, gn=gn, mod=mods[l]))
        if l + 1 < DEPTH:
            nsh1, nsc1 = mods[l + 1][0], mods[l + 1][1]
            xin, h = _post_pre(x1, y2, g2, gn[3], norm_g_f[l + 1, 0][None], nsc1, nsh1, "post_pre")
        else:
            dx, loss_tile = _post_loss(x1, y2, g2, gn[3], tgt, "post_loss")
    loss = lax.psum(loss_tile[0, 0], ("x", "y", "c"))

    scatters = {}

    def scatter(k, l, dw):
        if k in _ROW_SHARDED:
            send = dw.reshape(N_CHIPS, dw.shape[0] // N_CHIPS, dw.shape[1])
        else:
            send = _w_in_to_chips(dw) if k == "w_in" else dw
        *handle, token = _scatter_start(send, "scatter_start_%s_%d" % (k, l))
        scatters[k, l] = tuple(handle)
        return token[0, 0]

    sm = {k: [None] * DEPTH for k in ("dmod", "norm_g", "w_a2", "b_a2", "gng", "conv_w", "ffn_w")}
    for l in reversed(range(DEPTH)):
        s = saved[l]
        sh1, sc1, g1, sh2, sc2, g2 = s["mod"]
        gn = s["gn"]
        dy2, dg2, dgn3 = _post_bwd(dx, s["y2"], g2, gn[3], "post_bwd")
        tk = scatter("w_dn", l, _matmul(s["f"], dy2, "tn", BF16, "mm_down_dw", tm=512, tn=1024, tk=2048))
        df = _matmul(dy2, s["wdn"], "nt", F32, "mm_down_dx")
        dgate, dup, dfw = _ffn_bwd(df, s["u"], s["fw8"] + tk, "ffn_bwd")
        du = jnp.concatenate([dgate, dup], axis=1)
        tk = scatter("w_up", l, _matmul(s["h2"], du, "tn", BF16, "mm_up_dw", tm=512, tn=1408, tk=2048, out_chips=True))
        dh2 = _matmul(du, s["wup"], "nt", F32, "mm_up_dx", tn=1024, tk=1408, b_chips=True)
        dx1, dsh2, dsc2, dgn2 = _pre_bwd(dh2, s["x1"], dx, gn[2] + tk, sc2, "pre_bwd")
        dy, dg1, dgn1 = _post_bwd(dx1, s["y"], g1, gn[1], "post_bwd")
        tk = scatter("w_o", l, _matmul(s["mm"], dy, "tn", BF16, "mm_o_dw"))
        dm = _matmul(dy, s["wo"], "nt", F32, "mm_o_dx")
        dya, dyb, dga, dgb = _merge_bwd(dm, s["ya"], s["yb"], s["p"], "merge_bwd")
        tk = tk + scatter("w_og", l, _matmul(s["za"], dya, "tn", BF16, "mm_out_gla_dw", out_chips=True))
        dza = _matmul(dya, s["wog"], "nt", F32, "mm_out_gla_dx", b_chips=True)
        do, dr, dgng = _gla_out_bwd(dza, s["o"], s["p"], s["gng"] + tk, "gla_out_bwd")
        tk = scatter("w_oc", l, _matmul(s["zb"], dyb, "tn", BF16, "mm_out_conv_dw", out_chips=True))
        dzb = _matmul(dyb, s["woc"], "nt", F32, "mm_out_conv_dx", b_chips=True)
        dcb, dcc, dcx, dcw = _conv_bwd(dzb, s["p"], s["cw8"] + tk, "conv_bwd")
        dq, dk, dv, dlr, dwa, dba = _gla_bwd(do, s["p"], s["st"], s["wa"], s["ba"], "gla_bwd")
        dp = jnp.concatenate([dq, dk, dv, dr, dcb, dcc, dcx, dga, dgb, dlr], axis=1)
        dw_in = _matmul(s["h"], dp, "tn", BF16, "mm_in_dw", tm=512, tn=1152, tk=2048)
        tk = scatter("w_in", l, dw_in) if l > 0 else 0.0
        dh =_matmul(dp, s["wi"], "nt", F32, "mm_in_dx", tn=1024, tk=1152)
        dx, dsh1, dsc1, dgn0 = _pre_bwd(dh, s["xin"], dx1, gn[0] + tk, sc1, "pre_bwd")
        sm["dmod"][l] = jnp.concatenate([dsh1, dsc1, dg1, dsh2, dsc2, dg2], axis=1)[0]
        sm["norm_g"][l] = jnp.concatenate([dgn0, dgn1, dgn2, dgn3], axis=0)
        sm["w_a2"][l] = dwa[:GLA_LOWRANK]
        sm["b_a2"][l] = dba[0]
        sm["gng"][l] = dgng[0]
        sm["conv_w"][l] = dcw[:3]
        sm["ffn_w"][l] = dfw[:3]
    grad_x = dx[None]

    names = ("dmod", "norm_g", "w_a2", "b_a2", "gng", "conv_w", "ffn_w")
    parts = [jnp.stack(sm[k]).reshape(-1) for k in names]
    shapes = [jnp.stack(sm[k]).shape for k in names]
    sizes = [a.shape[0] for a in parts]
    flat = jnp.concatenate(parts)
    rows = -(-flat.shape[0] // 1024) * 8
    flat = jnp.concatenate([flat, jnp.zeros((rows * 128 - flat.shape[0],), F32)]).reshape(rows, 128)
    gath, tot = _allgather8(flat, "reduce_small")
    tk = scatter("w_in", 0, dw_in + (0.0 * tot[0, 0]).astype(BF16))
    c16 = c16 + tk
    po = [0]
    for s_ in sizes:
        po.append(po[-1] + s_)
    tot = tot.reshape(-1)
    tot_of = {k: tot[po[i]:po[i + 1]].reshape(shapes[i]) for i, k in enumerate(names)}
    dmod_all = gath.reshape(N_DEV, -1)[:, po[0]:po[1]].reshape(N_DEV, DEPTH, 6 * D_MODEL)

    def chip_cols(a, width):
        return lax.dynamic_slice_in_dim(a, chip * width, width, axis=a.ndim - 1)

    dml = jnp.transpose(chip_cols(dmod_all, 3072), (1, 0, 2))
    dml = jnp.concatenate([dml, jnp.zeros_like(dml)], axis=1)
    g_w_ada = _ada_bwd(c16, dml, "ada_bwd")

    def upd(w, m, v, ga, gb, name):
        sh = w.shape
        as3 = sh if len(sh) == 3 else (1,) + sh
        outs = _adamw(w.reshape(as3), m.reshape(as3), v.reshape(as3), ga.reshape(as3),
                      None if gb is None else gb.reshape(as3), name)
        return [a.reshape(sh) for a in outs]

    res = {}
    res["w_ada"] = upd(w_ada, m_w_ada, v_w_ada, g_w_ada, None, "adamw")
    res["b_ada"] = upd(b_ada, m_b_ada, v_b_ada, tot_of["dmod"], None, "adamw")
    res["norm_g"] = upd(norm_g, m_norm_g, v_norm_g, chip_cols(tot_of["norm_g"], 512), None, "adamw")
    res["w_a2"] = upd(w_a2, m_w_a2, v_w_a2, chip_cols(tot_of["w_a2"], 128), None, "adamw")
    res["b_a2"] = upd(b_a2, m_b_a2, v_b_a2, tot_of["b_a2"], None, "adamw")
    res["gla_norm_g"] = upd(gla_norm_g, m_gla_norm_g, v_gla_norm_g, tot_of["gng"], None, "adamw")
    res["conv_mix_w"] = upd(conv_mix_w, m_conv_mix_w, v_conv_mix_w, chip_cols(tot_of["conv_w"], 256), None, "adamw")
    res["ffn_conv_w"] = upd(ffn_conv_w, m_ffn_conv_w, v_ffn_conv_w, chip_cols(tot_of["ffn_w"], 1408), None, "adamw")

    full_name = dict(w_in="w_in", w_og="w_out_gla", w_oc="w_out_conv", w_o="w_o", w_up="w_up", w_dn="w_down")
    state = dict(w_in=(w_in, m_w_in, v_w_in), w_og=(w_out_gla, m_w_out_gla, v_w_out_gla),
                 w_oc=(w_out_conv, m_w_out_conv, v_w_out_conv), w_o=(w_o, m_w_o, v_w_o),
                 w_up=(w_up, m_w_up, v_w_up), w_dn=(w_down, m_w_down, v_w_down))
    after = res["w_ada"][3]
    for k in ("w_dn", "w_up", "w_o", "w_og", "w_oc", "w_in"):
        done = [_scatter_wait(scatters[k, l], after, "scatter_wait_%s_%d" % (k, l)) for l in range(DEPTH)]
        plane = _sum_chips([d[0] for d in done], [d[1] for d in done], chip_arr, "sum_chips")
        if k == "w_in":
            plane = jnp.transpose(plane, (2, 0, 1))
            other = _sibling_exchange([plane], "sibling_" + k)[0]
            outs = _adamw(*[jnp.transpose(a, (2, 0, 1)) for a in state[k]], plane, other, "adamw_w_in",
                          tile=(N_IN_CHIP // 4, D_MODEL // 8))
            res[full_name[k]] = [jnp.transpose(a, (1, 2, 0)) for a in outs]
        else:
            other = _sibling_exchange([plane], "sibling_" + k)[0]
            res[full_name[k]] = upd(*state[k], plane, other, "adamw")
        after = res[full_name[k]][3]
    order = ("w_ada", "b_ada", "norm_g", "w_in", "w_a2", "b_a2", "gla_norm_g", "w_out_gla", "conv_mix_w",
             "w_out_conv", "w_o", "w_up", "ffn_conv_w", "w_down")
    return (loss, grad_x, *[res[k][0] for k in order], *[res[k][1] for k in order],
            *[res[k][2] for k in order], *[res[k][3] for k in order])
```

```python
import functools
import math

import jax
import jax.numpy as jnp
from jax import lax
from jax.experimental import pallas as pl
from jax.experimental.pallas import tpu as pltpu

F32 = jnp.float32
BF16 = jnp.bfloat16
MESH = pl.DeviceIdType.MESH

D_MODEL = 2048
DEPTH = 2
CHUNK = 64
GLA_HEADS = 4
GLA_DK = 128
GLA_DV = 256
GLA_QK = GLA_HEADS * GLA_DK
GLA_V = GLA_HEADS * GLA_DV
GLA_LOWRANK = 16
GLA_TAU = 16.0
CONV_WIDTH = 1024
D_FF = 5632
EPS = 1e-6
N_IN = 10256
LR_PAD = 128
N_IN_PAD = N_IN - GLA_LOWRANK + LR_PAD
OFF_Q, OFF_K, OFF_V, OFF_R = 0, 512, 1024, 2048
OFF_CB, OFF_CC, OFF_CX, OFF_GA, OFF_GB, OFF_LR = 3072, 4096, 5120, 6144, 8192, 10240

ADAM_LR = 0.001
ADAM_B1 = 0.9
ADAM_B2 = 0.999
ADAM_EPS = 1e-08
ADAM_WD = 0.01
ADAM_STEP = 10

N_CHIPS = 4
N_DEV = 8
VMEM_LIMIT = 56 * 1024 * 1024
TM_ROW = 256
TM_EW = 512
CW_EW = 512
GLA_ROWS = 256


def _params(sem=None):
    return pltpu.CompilerParams(dimension_semantics=sem, vmem_limit_bytes=VMEM_LIMIT)


def _sigmoid(v):
    return 1.0 / (1.0 + jnp.exp(-v))


def _log_sigmoid(v):
    return jnp.minimum(v, 0.0) - jnp.log(1.0 + jnp.exp(-jnp.abs(v)))


_GELU_C = math.sqrt(2.0 / math.pi)


def _gelu(v):
    return 0.5 * v * (1.0 + jnp.tanh(_GELU_C * (v + 0.044715 * v * v * v)))


def _gelu_grad(v):
    t = jnp.tanh(_GELU_C * (v + 0.044715 * v * v * v))
    return 0.5 * (1.0 + t) + 0.5 * v * (1.0 - t * t) * _GELU_C * (1.0 + 3.0 * 0.044715 * v * v)


def _flip(a, d):
    return a + d - 2 * a * d


def _unless(cond):
    return jnp.where(cond, 0.0, 1.0).astype(F32)


def _allgather8(xv, name):
    r, cdim = xv.shape

    def body(x_ref, out_ref, sum_ref, send_sems, recv_sems):
        xi, yi, ci = lax.axis_index("x"), lax.axis_index("y"), lax.axis_index("c")
        me = 4 * xi + 2 * yi + ci
        out_ref[pl.ds(me, 1)] = x_ref[...][None]
        started = []
        for k in range(1, N_DEV):
            px, py, pc = _flip(xi, (k >> 2) & 1), _flip(yi, (k >> 1) & 1), _flip(ci, k & 1)
            cp = pltpu.make_async_remote_copy(
                src_ref=x_ref, dst_ref=out_ref.at[me], send_sem=send_sems.at[k - 1], recv_sem=recv_sems.at[k - 1],
                device_id=(px, py, pc), device_id_type=MESH)
            cp.start()
            started.append((cp, 4 * px + 2 * py + pc, k, (px, py, pc)))
        for cp, peer, k, pid in started:
            cp.wait_send()
            pltpu.make_async_remote_copy(
                src_ref=x_ref, dst_ref=out_ref.at[peer], send_sem=send_sems.at[k - 1], recv_sem=recv_sems.at[k - 1],
                device_id=pid, device_id_type=MESH).wait_recv()
        acc = out_ref[0]
        for d in range(1, N_DEV):
            acc = acc + out_ref[d]
        sum_ref[...] = acc

    return pl.pallas_call(
        body, name=name,
        out_shape=(jax.ShapeDtypeStruct((N_DEV, r, cdim), F32), jax.ShapeDtypeStruct((r, cdim), F32)),
        in_specs=[pl.BlockSpec(memory_space=pltpu.VMEM)],
        out_specs=(pl.BlockSpec(memory_space=pltpu.VMEM), pl.BlockSpec(memory_space=pltpu.VMEM)),
        scratch_shapes=[pltpu.SemaphoreType.DMA((N_DEV - 1,)), pltpu.SemaphoreType.DMA((N_DEV - 1,))],
        compiler_params=pltpu.CompilerParams(vmem_limit_bytes=VMEM_LIMIT),
    )(xv)


_HBM = pl.BlockSpec(memory_space=pltpu.HBM)
_SEM = pl.BlockSpec(memory_space=pltpu.SEMAPHORE)
_EFFECT = pltpu.SideEffectType.DATAFLOW_SIDE_EFFECTING
_CHIP_FLIPS = ((1, 0), (0, 1), (1, 1))


def _chip_copies(src_ref, land_ref, send_sems, recv_sems, scatter):
    xi, yi, ci = lax.axis_index("x"), lax.axis_index("y"), lax.axis_index("c")
    me = 2 * xi + yi
    out = []
    for k, (dx, dy) in enumerate(_CHIP_FLIPS):
        px, py = _flip(xi, dx), _flip(yi, dy)
        peer = 2 * px + py
        src = src_ref.at[peer] if scatter else land_ref.at[me]
        mk = functools.partial(pltpu.make_async_remote_copy, src_ref=src, send_sem=send_sems.at[k],
                               recv_sem=recv_sems.at[k], device_id=(px, py, ci), device_id_type=MESH)
        out.append((mk(dst_ref=land_ref.at[me]), mk(dst_ref=land_ref.at[peer])))
    return out


def _gather_start(land, name):
    def body(land_ref, send_sems, recv_sems, land_thru, token):
        for mine, _ in _chip_copies(None, land_ref, send_sems, recv_sems, False):
            mine.start()
        token[...] = jnp.zeros_like(token)

    return pl.pallas_call(
        body, name=name,
        out_shape=(pltpu.SemaphoreType.DMA((3,)), pltpu.SemaphoreType.DMA((3,)), pltpu.HBM(land.shape, land.dtype),
                   jax.ShapeDtypeStruct((8, 128), F32)),
        in_specs=(_HBM,), out_specs=(_SEM, _SEM, _HBM, pl.BlockSpec(memory_space=pltpu.VMEM)),
        input_output_aliases={0: 2},
        compiler_params=pltpu.CompilerParams(has_side_effects=_EFFECT),
    )(pltpu.with_memory_space_constraint(land, pltpu.HBM))


def _gather_wait(handle, after, name):
    send, recv, land_thru = handle

    def body(land_ref, send_sems, recv_sems, after_ref, land_out):
        for mine, theirs in _chip_copies(None, land_ref, send_sems, recv_sems, False):
            mine.wait_send()
            theirs.wait_recv()

    return pl.pallas_call(
        body, name=name, out_shape=pltpu.HBM(land_thru.shape, land_thru.dtype),
        in_specs=(_HBM, _SEM, _SEM, pl.BlockSpec(memory_space=pl.ANY)), out_specs=_HBM,
        input_output_aliases={0: 0},
        compiler_params=pltpu.CompilerParams(has_side_effects=_EFFECT),
    )(land_thru, send, recv, after)


def _scatter_start(src, name):
    def body(src_ref, land_ref, send_sems, recv_sems, src_thru, land_thru, token):
        for mine, _ in _chip_copies(src_ref, land_ref, send_sems, recv_sems, True):
            mine.start()
        token[...] = jnp.zeros_like(token)

    return pl.pallas_call(
        body, name=name,
        out_shape=(pltpu.SemaphoreType.DMA((3,)), pltpu.SemaphoreType.DMA((3,)), pltpu.HBM(src.shape, src.dtype),
                   pltpu.HBM(src.shape, src.dtype), jax.ShapeDtypeStruct((8, 128), F32)),
        in_specs=(_HBM, _HBM), out_specs=(_SEM, _SEM, _HBM, _HBM, pl.BlockSpec(memory_space=pltpu.VMEM)),
        input_output_aliases={0: 2, 1: 3},
        compiler_params=pltpu.CompilerParams(has_side_effects=_EFFECT),
    )(pltpu.with_memory_space_constraint(src, pltpu.HBM),
      pltpu.with_memory_space_constraint(lax.empty(src.shape, src.dtype), pltpu.HBM))


def _scatter_wait(handle, after, name):
    send, recv, src_thru, land_thru = handle

    def body(src_ref, land_ref, send_sems, recv_sems, after_ref, src_out, land_out):
        for mine, theirs in _chip_copies(src_ref, land_ref, send_sems, recv_sems, True):
            mine.wait_send()
            theirs.wait_recv()

    return pl.pallas_call(
        body, name=name,
        out_shape=(pltpu.HBM(src_thru.shape, src_thru.dtype), pltpu.HBM(land_thru.shape, land_thru.dtype)),
        in_specs=(_HBM, _HBM, _SEM, _SEM, pl.BlockSpec(memory_space=pl.ANY)), out_specs=(_HBM, _HBM),
        input_output_aliases={0: 0, 1: 1},
        compiler_params=pltpu.CompilerParams(has_side_effects=_EFFECT),
    )(src_thru, land_thru, send, recv, after)


def _sibling_exchange(arrays, name):
    n = len(arrays)

    def body(*refs):
        ins, outs = refs[:n], refs[n:2 * n]
        send_sems, recv_sems = refs[2 * n:]
        xi, yi, ci = lax.axis_index("x"), lax.axis_index("y"), lax.axis_index("c")
        cps = []
        for i in range(n):
            cp = pltpu.make_async_remote_copy(
                src_ref=ins[i], dst_ref=outs[i], send_sem=send_sems.at[i], recv_sem=recv_sems.at[i],
                device_id=(xi, yi, 1 - ci), device_id_type=MESH)
            cp.start()
            cps.append(cp)
        for cp in cps:
            cp.wait()

    return pl.pallas_call(
        body, name=name, out_shape=tuple(jax.ShapeDtypeStruct(a.shape, a.dtype) for a in arrays),
        in_specs=[pl.BlockSpec(memory_space=pl.ANY)] * n,
        out_specs=tuple(pl.BlockSpec(memory_space=pl.ANY) for _ in range(n)),
        scratch_shapes=[pltpu.SemaphoreType.DMA((n,)), pltpu.SemaphoreType.DMA((n,))],
    )(*arrays)


def _pick(dim, pref):
    if dim <= pref:
        return dim
    t = (pref // 128) * 128
    while t >= 128:
        if dim % t == 0:
            return t
        t -= 128
    return dim


def _matmul(a, b, dims, out_dtype, name, tm=512, tn=1024, tk=2048, out_chips=False, b_chips=False):
    a_parts = a if isinstance(a, tuple) else (a,)
    b_parts = b if isinstance(b, tuple) else (b,)
    na, nb = len(a_parts), len(b_parts)
    assert (na == 1 or dims == "nt") and (nb == 1 or dims == "tn")
    b_shape = (b_parts[0].shape[1], N_CHIPS * b_parts[0].shape[2]) if b_chips else b_parts[0].shape
    if dims == "nn":
        (m, kd), (_, n) = a_parts[0].shape, b_shape
    elif dims == "nt":
        (m, kd), (n, _) = a_parts[0].shape, b_shape
        kd = na * kd
    else:
        (kd, m), (_, n) = a_parts[0].shape, b_shape
        n = nb * n
    tm = _pick(m, tm)
    tn = _pick(n // N_CHIPS, tn) if (out_chips or (b_chips and dims == "nn")) else _pick(n // nb, tn)
    tk = _pick(kd // N_CHIPS, tk) if (b_chips and dims == "nt") else _pick(kd // na, tk)
    nk, nj = kd // tk, n // tn
    ka, jb = nk // na, nj // nb
    if out_chips:
        per_chip = n // N_CHIPS // tn
        out_shape = jax.ShapeDtypeStruct((N_CHIPS, m, n // N_CHIPS), out_dtype)
        out_spec = pl.BlockSpec((None, tm, tn), lambda j, i, k: (j // per_chip, i, j % per_chip))
    else:
        out_shape = jax.ShapeDtypeStruct((m, n), out_dtype)
        out_spec = pl.BlockSpec((tm, tn), lambda j, i, k: (i, j))
    def part_of(idx, first, count):
        return jnp.clip(idx - first, 0, count - 1)

    if dims == "nn":
        a_specs = [pl.BlockSpec((tm, tk), lambda j, i, k: (i, k))]
        b_specs = [pl.BlockSpec((tk, tn), lambda j, i, k: (k, j))]
        dn = (((1,), (0,)), ((), ()))
    elif dims == "nt":
        a_specs = [pl.BlockSpec((tm, tk), lambda j, i, k, p=p: (i, part_of(k, p * ka, ka))) for p in range(na)]
        b_specs = [pl.BlockSpec((tn, tk), lambda j, i, k: (j, k))]
        dn = (((1,), (1,)), ((), ()))
    else:
        a_specs = [pl.BlockSpec((tk, tm), lambda j, i, k: (k, i))]
        b_specs = [pl.BlockSpec((tk, tn), lambda j, i, k, p=p: (k, part_of(j, p * jb, jb))) for p in range(nb)]
        dn = (((0,), (0,)), ((), ()))
    if b_chips and dims == "nn":
        nper = n // N_CHIPS // tn
        b_specs = [pl.BlockSpec((None, tk, tn), lambda j, i, k: (j // nper, k, j % nper))]
    elif b_chips:
        kper = kd // N_CHIPS // tk
        b_specs = [pl.BlockSpec((None, tn, tk), lambda j, i, k: (k // kper, j, k % kper))]
    direct = nk == 1 or out_dtype == F32

    def body(*refs):
        a_refs, b_refs, o_ref = refs[:na], refs[na:na + nb], refs[na + nb]
        acc_ref = o_ref if direct else refs[na + nb + 1]
        j, k = pl.program_id(0), pl.program_id(2)

        def step(a_ref, b_ref):
            part = lax.dot_general(a_ref[...].astype(BF16), b_ref[...].astype(BF16), dn, preferred_element_type=F32)
            if nk == 1:
                o_ref[...] = part.astype(o_ref.dtype)
                return

            @pl.when(k == 0)
            def _():
                acc_ref[...] = part

            @pl.when(k > 0)
            def _():
                acc_ref[...] += part

            if not direct:
                @pl.when(k == nk - 1)
                def _():
                    o_ref[...] = acc_ref[...].astype(o_ref.dtype)

        if na == 1 and nb == 1:
            step(a_refs[0], b_refs[0])
        for p in range(na if na > 1 else 0):
            pl.when(jnp.logical_and(k >= p * ka, k < (p + 1) * ka))(functools.partial(step, a_refs[p], b_refs[0]))
        for p in range(nb if nb > 1 else 0):
            pl.when(jnp.logical_and(j >= p * jb, j < (p + 1) * jb))(functools.partial(step, a_refs[0], b_refs[p]))

    return pl.pallas_call(
        body, name=name, out_shape=out_shape,
        grid=(nj, m // tm, nk),
        in_specs=a_specs + b_specs,
        out_specs=out_spec,
        scratch_shapes=[] if direct else [pltpu.VMEM((tm, tn), F32)],
        compiler_params=_params(("parallel", "parallel", "arbitrary")),
    )(*a_parts, *b_parts)


def _rstd(v):
    return lax.rsqrt(jnp.mean(v * v, axis=-1, keepdims=True) + EPS)


def _row(tm):
    return pl.BlockSpec((tm, D_MODEL), lambda i: (i, 0))


_VEC = pl.BlockSpec((1, D_MODEL), lambda i: (0, 0))


def _pre_norm(x, gn, sc, sh, name):
    t = x.shape[0]
    tm = min(TM_ROW, t)

    def body(x_ref, gn_ref, sc_ref, sh_ref, h_ref):
        xv = x_ref[...]
        h_ref[...] = ((xv * _rstd(xv) * gn_ref[...]) * (1.0 + sc_ref[...]) + sh_ref[...]).astype(BF16)

    return pl.pallas_call(
        body, name=name, out_shape=jax.ShapeDtypeStruct((t, D_MODEL), BF16), grid=(t // tm,),
        in_specs=[_row(tm), _VEC, _VEC, _VEC], out_specs=_row(tm),
        compiler_params=_params(("parallel",)),
    )(x, gn, sc, sh)


def _post_pre(x, y, g, gnp, gn, sc, sh, name):
    t = x.shape[0]
    tm = min(TM_ROW, t)

    def body(x_ref, y_ref, g_ref, gnp_ref, gn_ref, sc_ref, sh_ref, x1_ref, h_ref):
        yv = y_ref[...]
        x1 = x_ref[...] + g_ref[...] * (yv * _rstd(yv) * gnp_ref[...])
        x1_ref[...] = x1
        h_ref[...] = ((x1 * _rstd(x1) * gn_ref[...]) * (1.0 + sc_ref[...]) + sh_ref[...]).astype(BF16)

    return pl.pallas_call(
        body, name=name,
        out_shape=(jax.ShapeDtypeStruct((t, D_MODEL), F32), jax.ShapeDtypeStruct((t, D_MODEL), BF16)),
        grid=(t // tm,),
        in_specs=[_row(tm), _row(tm), _VEC, _VEC, _VEC, _VEC, _VEC], out_specs=(_row(tm), _row(tm)),
        compiler_params=_params(("parallel",)),
    )(x, y, g, gnp, gn, sc, sh)


def _post_loss(x, y, g, gnp, tgt, name):
    t = x.shape[0]
    tm = min(TM_ROW, t)

    def body(x_ref, y_ref, g_ref, gnp_ref, t_ref, dx_ref, loss_ref):
        yv = y_ref[...]
        diff = x_ref[...] + g_ref[...] * (yv * _rstd(yv) * gnp_ref[...]) - t_ref[...]
        dx_ref[...] = diff * (1.0 / D_MODEL)
        part = (0.5 / D_MODEL) * jnp.sum(jnp.sum(diff * diff, axis=-1, keepdims=True), axis=0, keepdims=True)

        @pl.when(pl.program_id(0) == 0)
        def _():
            loss_ref[...] = jnp.zeros_like(loss_ref)

        loss_ref[...] += jnp.broadcast_to(part, loss_ref.shape)

    return pl.pallas_call(
        body, name=name,
        out_shape=(jax.ShapeDtypeStruct((t, D_MODEL), F32), jax.ShapeDtypeStruct((8, 128), F32)),
        grid=(t // tm,),
        in_specs=[_row(tm), _row(tm), _VEC, _VEC, _row(tm)],
        out_specs=(_row(tm), pl.BlockSpec((8, 128), lambda i: (0, 0))),
        compiler_params=_params(("arbitrary",)),
    )(x, y, g, gnp, tgt)


def _acc_rows(ref, val):
    @pl.when(pl.program_id(0) == 0)
    def _():
        ref[...] = jnp.zeros_like(ref)

    ref[...] += jnp.sum(val, axis=0, keepdims=True)


def _post_bwd(dxn, y, g, gnp, name):
    t = y.shape[0]
    tm = min(TM_ROW, t)

    def body(dx_ref, y_ref, g_ref, gnp_ref, dy_ref, dg_ref, dgn_ref):
        yv, dxv = y_ref[...], dx_ref[...]
        r = _rstd(yv)
        yh = yv * r
        _acc_rows(dg_ref, dxv * (yh * gnp_ref[...]))
        dn = dxv * g_ref[...]
        _acc_rows(dgn_ref, dn * yh)
        dyh = dn * gnp_ref[...]
        dy_ref[...] = (r * (dyh - yh * jnp.mean(dyh * yh, axis=-1, keepdims=True))).astype(BF16)

    return pl.pallas_call(
        body, name=name,
        out_shape=(jax.ShapeDtypeStruct((t, D_MODEL), BF16), jax.ShapeDtypeStruct((1, D_MODEL), F32),
                   jax.ShapeDtypeStruct((1, D_MODEL), F32)),
        grid=(t // tm,),
        in_specs=[_row(tm), _row(tm), _VEC, _VEC], out_specs=(_row(tm), _VEC, _VEC),
        compiler_params=_params(("arbitrary",)),
    )(dxn, y, g, gnp)


def _pre_bwd(dh, xin, dres, gn, sc, name):
    t = xin.shape[0]
    tm = min(TM_ROW, t)

    def body(dh_ref, x_ref, dres_ref, gn_ref, sc_ref, dx_ref, dsh_ref, dsc_ref, dgn_ref):
        xv, dhv = x_ref[...], dh_ref[...]
        r = _rstd(xv)
        xh = xv * r
        _acc_rows(dsh_ref, dhv)
        _acc_rows(dsc_ref, dhv * (xh * gn_ref[...]))
        dn = dhv * (1.0 + sc_ref[...])
        _acc_rows(dgn_ref, dn * xh)
        dxh = dn * gn_ref[...]
        dx_ref[...] = dres_ref[...] + r * (dxh - xh * jnp.mean(dxh * xh, axis=-1, keepdims=True))

    vec = jax.ShapeDtypeStruct((1, D_MODEL), F32)
    return pl.pallas_call(
        body, name=name, out_shape=(jax.ShapeDtypeStruct((t, D_MODEL), F32), vec, vec, vec),
        grid=(t // tm,),
        in_specs=[_row(tm), _row(tm), _row(tm), _VEC, _VEC], out_specs=(_row(tm), _VEC, _VEC, _VEC),
        compiler_params=_params(("arbitrary",)),
    )(dh, xin, dres, gn, sc)


def _shift_down(v, halo, s):
    tm = v.shape[0]
    out = pltpu.roll(v, s, 0)
    row = lax.broadcasted_iota(jnp.int32, v.shape, 0)
    for j in range(s):
        out = jnp.where(row == j, jnp.broadcast_to(halo[8 - s + j:8 - s + j + 1, :], v.shape), out)
    return out


def _shift_up(v, halo, s):
    tm = v.shape[0]
    out = pltpu.roll(v, tm - s, 0)
    row = lax.broadcasted_iota(jnp.int32, v.shape, 0)
    for j in range(s):
        out = jnp.where(row == tm - s + j, jnp.broadcast_to(halo[j:j + 1, :], v.shape), out)
    return out


def _tile_specs(tm, cw, off, nrow):
    ob = off // cw
    r8 = tm // 8
    main = pl.BlockSpec((tm, cw), lambda j, i: (i, ob + j))
    prev = pl.BlockSpec((8, cw), lambda j, i: (jnp.maximum(i * r8 - 1, 0), ob + j))
    nxt = pl.BlockSpec((8, cw), lambda j, i: (jnp.minimum((i + 1) * r8, nrow * r8 - 1), ob + j))
    return main, prev, nxt


def _conv_fwd(p, w, name):
    t = p.shape[0]
    tm, cw = min(TM_EW, t), CW_EW
    nrow = t // tm
    cb_s, _, _ = _tile_specs(tm, cw, OFF_CB, nrow)
    cc_s, cc_p, _ = _tile_specs(tm, cw, OFF_CC, nrow)
    cx_s, cx_p, _ = _tile_specs(tm, cw, OFF_CX, nrow)

    def body(cb_ref, cc_ref, ccp_ref, cx_ref, cxp_ref, w_ref, z_ref):
        u = cc_ref[...] * cx_ref[...]
        uh = ccp_ref[...] * cxp_ref[...] * _unless(pl.program_id(1) == 0)
        wv = w_ref[...]
        conv = wv[2:3, :] * u + wv[1:2, :] * _shift_down(u, uh, 1) + wv[0:1, :] * _shift_down(u, uh, 2)
        z_ref[...] = (cb_ref[...] * conv).astype(BF16)

    return pl.pallas_call(
        body, name=name, out_shape=jax.ShapeDtypeStruct((t, CONV_WIDTH), BF16),
        grid=(CONV_WIDTH // cw, nrow),
        in_specs=[cb_s, cc_s, cc_p, cx_s, cx_p, pl.BlockSpec((8, cw), lambda j, i: (0, j))],
        out_specs=pl.BlockSpec((tm, cw), lambda j, i: (i, j)),
        compiler_params=_params(("parallel", "arbitrary")),
    )(p, p, p, p, p, w)


def _acc_w(ref, vals):
    @pl.when(pl.program_id(1) == 0)
    def _():
        ref[...] = jnp.zeros_like(ref)

    for j, v in enumerate(vals):
        ref[j:j + 1, :] += jnp.sum(v, axis=0, keepdims=True)


def _conv_bwd(dz, p, w, dp, name):
    t = p.shape[0]
    tm, cw = min(TM_EW // 2, t), CONV_WIDTH
    nrow = t // tm
    dz_s, _, dz_n = _tile_specs(tm, cw, 0, nrow)
    cb_s, _, cb_n = _tile_specs(tm, cw, OFF_CB, nrow)
    cc_s, cc_p, _ = _tile_specs(tm, cw, OFF_CC, nrow)
    cx_s, cx_p, _ = _tile_specs(tm, cw, OFF_CX, nrow)

    def body(dz_ref, dzn_ref, cb_ref, cbn_ref, cc_ref, ccp_ref, cx_ref, cxp_ref, w_ref, dp_in, dp_ref, dw_ref):
        dcb_ref = dp_ref.at[:, 0:cw]
        dcc_ref = dp_ref.at[:, cw:2 * cw]
        dcx_ref = dp_ref.at[:, 2 * cw:3 * cw]
        i = pl.program_id(1)
        ccv, cxv, dzv = cc_ref[...], cx_ref[...], dz_ref[...]
        u = ccv * cxv
        uh = ccp_ref[...] * cxp_ref[...] * _unless(i == 0)
        wv = w_ref[...]
        u1, u2 = _shift_down(u, uh, 1), _shift_down(u, uh, 2)
        conv = wv[2:3, :] * u + wv[1:2, :] * u1 + wv[0:1, :] * u2
        dcb_ref[...] = (dzv * conv).astype(BF16)
        dconv = dzv * cb_ref[...]
        dch = dzn_ref[...] * cbn_ref[...] * _unless(i == nrow - 1)
        du = wv[2:3, :] * dconv + wv[1:2, :] * _shift_up(dconv, dch, 1) + wv[0:1, :] * _shift_up(dconv, dch, 2)
        dcc_ref[...] = (du * cxv).astype(BF16)
        dcx_ref[...] = (du * ccv).astype(BF16)
        _acc_w(dw_ref, (dconv * u2, dconv * u1, dconv * u))

    w_s = pl.BlockSpec((8, cw), lambda j, i: (0, j))
    return pl.pallas_call(
        body, name=name, out_shape=(_dp_shape(t), jax.ShapeDtypeStruct((8, CONV_WIDTH), F32)),
        grid=(1, nrow),
        in_specs=[dz_s, dz_n, cb_s, cb_n, cc_s, cc_p, cx_s, cx_p, w_s, _ANY],
        out_specs=(pl.BlockSpec((tm, 3 * cw), lambda j, i: (i, OFF_CB // (3 * cw))), w_s),
        input_output_aliases={9: 0},
        compiler_params=_params(("parallel", "arbitrary")),
    )(dz, dz, p, p, p, p, p, p, w, dp)


def _ffn_fwd(u, w, name):
    t = u.shape[0]
    tm, cw = min(TM_EW, t), CW_EW
    nrow = t // tm
    g_s, g_p, _ = _tile_specs(tm, cw, 0, nrow)
    u_s, _, _ = _tile_specs(tm, cw, D_FF, nrow)

    def body(g_ref, gp_ref, u_ref, w_ref, f_ref):
        gv = g_ref[...]
        gh = gp_ref[...] * _unless(pl.program_id(1) == 0)
        wv = w_ref[...]
        gc = wv[2:3, :] * gv + wv[1:2, :] * _shift_down(gv, gh, 1) + wv[0:1, :] * _shift_down(gv, gh, 2)
        f_ref[...] = (_gelu(gc) * u_ref[...]).astype(BF16)

    return pl.pallas_call(
        body, name=name, out_shape=jax.ShapeDtypeStruct((t, D_FF), BF16),
        grid=(D_FF // cw, nrow),
        in_specs=[g_s, g_p, u_s, pl.BlockSpec((8, cw), lambda j, i: (0, j))],
        out_specs=pl.BlockSpec((tm, cw), lambda j, i: (i, j)),
        compiler_params=_params(("parallel", "arbitrary")),
    )(u, u, u, w)


def _ffn_bwd(df, u, w, name):
    t = u.shape[0]
    tm, cw = min(TM_EW, t), CW_EW
    nrow = t // tm
    df_s, _, df_n = _tile_specs(tm, cw, 0, nrow)
    g_s, g_p, g_n = _tile_specs(tm, cw, 0, nrow)
    u_s, _, u_n = _tile_specs(tm, cw, D_FF, nrow)
    r8 = tm // 8

    def body(df_ref, dfn_ref, g_ref, gp_ref, gn_ref, u_ref, un_ref, w_ref, dg_ref, du_ref, dw_ref):
        i = pl.program_id(1)
        gv, dfv, uv = g_ref[...], df_ref[...], u_ref[...]
        gh = gp_ref[...] * _unless(i == 0)
        wv = w_ref[...]
        g1, g2 = _shift_down(gv, gh, 1), _shift_down(gv, gh, 2)
        gc = wv[2:3, :] * gv + wv[1:2, :] * g1 + wv[0:1, :] * g2
        du_ref[...] = (dfv * _gelu(gc)).astype(BF16)
        dgc = dfv * uv * _gelu_grad(gc)
        gnv = gn_ref[...]
        gtail = gv[tm - 8:tm, :]
        gcn = (wv[2:3, :] * gnv + wv[1:2, :] * _shift_down(gnv, gtail, 1) + wv[0:1, :] * _shift_down(gnv, gtail, 2))
        dgcn = dfn_ref[...] * un_ref[...] * _gelu_grad(gcn) * _unless(i == nrow - 1)
        dg = wv[2:3, :] * dgc + wv[1:2, :] * _shift_up(dgc, dgcn, 1) + wv[0:1, :] * _shift_up(dgc, dgcn, 2)
        dg_ref[...] = dg.astype(BF16)
        _acc_w(dw_ref, (dgc * g2, dgc * g1, dgc * gv))

    o_s = pl.BlockSpec((tm, cw), lambda j, i: (i, j))
    o_sh = jax.ShapeDtypeStruct((t, D_FF), BF16)
    w_s = pl.BlockSpec((8, cw), lambda j, i: (0, j))
    return pl.pallas_call(
        body, name=name, out_shape=(o_sh, o_sh, jax.ShapeDtypeStruct((8, D_FF), F32)),
        grid=(D_FF // cw, nrow),
        in_specs=[df_s, df_n, g_s, g_p, g_n, u_s, u_n, w_s],
        out_specs=(o_s, o_s, w_s),
        compiler_params=_params(("parallel", "arbitrary")),
    )(df, df, u, u, u, u, u, w)


def _merge_fwd(ya, yb, p, name):
    t = ya.shape[0]
    tm, cw = min(TM_EW, t), CW_EW
    y_s = pl.BlockSpec((tm, cw), lambda i, j: (i, j))

    def body(ya_ref, yb_ref, ga_ref, gb_ref, m_ref):
        m_ref[...] = (_sigmoid(ga_ref[...]) * ya_ref[...] + _sigmoid(gb_ref[...]) * yb_ref[...]).astype(BF16)

    return pl.pallas_call(
        body, name=name, out_shape=jax.ShapeDtypeStruct((t, D_MODEL), BF16),
        grid=(t // tm, D_MODEL // cw),
        in_specs=[y_s, y_s, pl.BlockSpec((tm, cw), lambda i, j: (i, OFF_GA // cw + j)),
                  pl.BlockSpec((tm, cw), lambda i, j: (i, OFF_GB // cw + j))],
        out_specs=y_s, compiler_params=_params(("parallel", "parallel")),
    )(ya, yb, p, p)


_ANY = pl.BlockSpec(memory_space=pl.ANY)


def _dp_shape(t):
    return jax.ShapeDtypeStruct((t, N_IN_PAD), BF16)


def _merge_bwd(dm, y, p, gate_off, dp, name):
    t = y.shape[0]
    tm, cw = min(TM_EW, t), CW_EW
    y_s = pl.BlockSpec((tm, cw), lambda i, j: (i, j))
    g_s = pl.BlockSpec((tm, cw), lambda i, j: (i, gate_off // cw + j))

    def body(dm_ref, y_ref, g_ref, *rest):
        dy_ref, dp_ref = rest[-2:]
        dmv = dm_ref[...]
        sg = _sigmoid(g_ref[...])
        dy_ref[...] = (dmv * sg).astype(BF16)
        dp_ref[...] = (dmv * y_ref[...] * sg * (1.0 - sg)).astype(BF16)

    extra = [] if dp is None else [dp]
    return pl.pallas_call(
        body, name=name, out_shape=(jax.ShapeDtypeStruct((t, D_MODEL), BF16), _dp_shape(t)),
        grid=(t // tm, D_MODEL // cw),
        in_specs=[y_s, y_s, g_s] + [_ANY] * len(extra),
        out_specs=(y_s, g_s), input_output_aliases={} if dp is None else {3: 1},
        compiler_params=_params(("parallel", "parallel")),
    )(dm, y, p, *extra)


def _tri(lower):
    r = lax.broadcasted_iota(jnp.int32, (CHUNK, CHUNK), 0)
    c = lax.broadcasted_iota(jnp.int32, (CHUNK, CHUNK), 1)
    return ((c <= r) if lower else (c >= r)).astype(F32)


def _eye_mask():
    r = lax.broadcasted_iota(jnp.int32, (GLA_DK, GLA_DK), 0)
    c = lax.broadcasted_iota(jnp.int32, (GLA_DK, GLA_DK), 1)
    return r == c


def _row_to_col(v):
    return jnp.sum(jnp.where(_eye_mask(), jnp.broadcast_to(v, (GLA_DK, GLA_DK)), 0.0), axis=1, keepdims=True)


def _col_to_row(v):
    return jnp.sum(jnp.where(_eye_mask(), jnp.broadcast_to(v, (GLA_DK, GLA_DK)), 0.0), axis=0, keepdims=True)


def _dot(a, b, dn):
    return lax.dot_general(a.astype(BF16), b.astype(BF16), (dn, ((), ())), preferred_element_type=F32)


_NN = ((1,), (0,))
_NT = ((1,), (1,))
_TN = ((0,), (0,))


def _gate_logits(lr_ref, wa_ref, ba_ref):
    return _dot(lr_ref[...], wa_ref[...], _NN) + ba_ref[...]


def _chunk_decay(la, tri):
    cum = lax.dot_general(tri, la, ((_NN), ((), ())), precision=lax.Precision.HIGHEST, preferred_element_type=F32)
    e = cum[CHUNK - 1:CHUNK, :]
    return cum, e, jnp.exp(e - cum)


def _gla_fwd(p, wa, ba, name):
    t = p.shape[0]
    rows = min(GLA_ROWS, t)
    cb = rows // CHUNK
    nc = t // CHUNK
    scale = GLA_DK ** -0.5

    def body(q_ref, k_ref, v_ref, lr_ref, wa_ref, ba_ref, o_ref, st_ref, s_scr):
        @pl.when(pl.program_id(0) == 0)
        def _():
            s_scr[...] = jnp.zeros_like(s_scr)

        la_all = _log_sigmoid(_gate_logits(lr_ref, wa_ref, ba_ref)) * (1.0 / GLA_TAU)
        tri = _tri(True)
        for ch in range(cb):
            rs = slice(ch * CHUNK, (ch + 1) * CHUNK)
            for h in range(GLA_HEADS):
                ks = slice(h * GLA_DK, (h + 1) * GLA_DK)
                vs = slice(h * GLA_DV, (h + 1) * GLA_DV)
                _, e, w = _chunk_decay(la_all[rs, ks], tri)
                kd = k_ref[rs, ks] * w
                s_new = _row_to_col(jnp.exp(e)) * s_scr[ks, :] + _dot(kd, v_ref[rs, vs], _TN)
                s_scr[ks, :] = s_new
                st_ref[ch, ks, :] = s_new
                o_ref[rs, vs] = _dot(q_ref[rs, ks] * scale, s_new, _NN)

    return pl.pallas_call(
        body, name=name,
        out_shape=(jax.ShapeDtypeStruct((t, GLA_V), F32), jax.ShapeDtypeStruct((nc, GLA_QK, GLA_DV), F32)),
        grid=(t // rows,),
        in_specs=[pl.BlockSpec((rows, GLA_QK), lambda i: (i, OFF_Q // GLA_QK)),
                  pl.BlockSpec((rows, GLA_QK), lambda i: (i, OFF_K // GLA_QK)),
                  pl.BlockSpec((rows, GLA_V), lambda i: (i, OFF_V // GLA_V)),
                  pl.BlockSpec((rows, LR_PAD), lambda i: (i, OFF_LR // LR_PAD)),
                  pl.BlockSpec((LR_PAD, GLA_QK), lambda i: (0, 0)),
                  pl.BlockSpec((1, GLA_QK), lambda i: (0, 0))],
        out_specs=(pl.BlockSpec((rows, GLA_V), lambda i: (i, 0)),
                   pl.BlockSpec((cb, GLA_QK, GLA_DV), lambda i: (i, 0, 0))),
        scratch_shapes=[pltpu.VMEM((GLA_QK, GLA_DV), F32)],
        compiler_params=_params(("arbitrary",)),
    )(p, p, p, p, wa, ba)


def _gla_bwd(do, p, st, wa, ba, dp, name):
    t = p.shape[0]
    rows = min(GLA_ROWS, t)
    cb = rows // CHUNK
    nb = t // rows
    scale = GLA_DK ** -0.5

    def rev(i):
        return nb - 1 - i

    def body(do_ref, q_ref, k_ref, v_ref, lr_ref, st_ref, stp_ref, wa_ref, ba_ref, dp_in,
             dp_ref, dlr_ref, dwa_ref, dba_ref, ds_scr, dz_scr):
        dq_ref = dp_ref.at[:, OFF_Q:OFF_Q + GLA_QK]
        dk_ref = dp_ref.at[:, OFF_K:OFF_K + GLA_QK]
        dv_ref = dp_ref.at[:, OFF_V:OFF_V + GLA_V]
        i = pl.program_id(0)

        @pl.when(i == 0)
        def _():
            ds_scr[...] = jnp.zeros_like(ds_scr)
            dwa_ref[...] = jnp.zeros_like(dwa_ref)
            dba_ref[...] = jnp.zeros_like(dba_ref)

        z_all = _gate_logits(lr_ref, wa_ref, ba_ref)
        la_all = _log_sigmoid(z_all) * (1.0 / GLA_TAU)
        tri, triu = _tri(True), _tri(False)
        last_row = lax.broadcasted_iota(jnp.int32, (CHUNK, GLA_DK), 0) == CHUNK - 1
        keep_prev = _unless(i == nb - 1)
        for ch in reversed(range(cb)):
            rs = slice(ch * CHUNK, (ch + 1) * CHUNK)
            for h in range(GLA_HEADS):
                ks = slice(h * GLA_DK, (h + 1) * GLA_DK)
                vs = slice(h * GLA_DV, (h + 1) * GLA_DV)
                _, e, w = _chunk_decay(la_all[rs, ks], tri)
                kd = k_ref[rs, ks] * w
                exp_e = jnp.exp(e)
                s_c = st_ref[ch, ks, :]
                if ch > 0:
                    s_p = st_ref[ch - 1, ks, :]
                else:
                    s_p = stp_ref[0, ks, :] * keep_prev
                do_c = do_ref[rs, vs]
                vv = v_ref[rs, vs]
                ds_tot = ds_scr[ks, :] + _dot(q_ref[rs, ks] * scale, do_c, _TN)
                dq_ref[rs, ks] = (_dot(do_c, s_c, _NT) * scale).astype(BF16)
                dkd = _dot(vv, ds_tot, _NT)
                dv_ref[rs, vs] = _dot(kd, ds_tot, _NN).astype(BF16)
                dexp_col = jnp.sum(ds_tot * s_p, axis=1, keepdims=True)
                ds_scr[ks, :] = _row_to_col(exp_e) * ds_tot
                dk_ref[rs, ks] = (dkd * w).astype(BF16)
                dwt = dkd * kd
                de = jnp.sum(dwt, axis=0, keepdims=True) + _col_to_row(dexp_col) * exp_e
                dcum = jnp.where(last_row, de - dwt, -dwt)
                da = lax.dot_general(triu, dcum, (_NN, ((), ())), precision=lax.Precision.HIGHEST,
                                     preferred_element_type=F32)
                dz_scr[rs, ks] = da * (1.0 / GLA_TAU) * _sigmoid(-z_all[rs, ks])
        dz = dz_scr[...]
        dlr_ref[...] = _dot(dz, wa_ref[...], _NT).astype(BF16)
        dwa_ref[...] += _dot(lr_ref[...], dz, _TN)
        dba_ref[...] += jnp.sum(dz, axis=0, keepdims=True)

    qkv = OFF_V + GLA_V
    return pl.pallas_call(
        body, name=name,
        out_shape=(_dp_shape(t), jax.ShapeDtypeStruct((t, LR_PAD), BF16),
                   jax.ShapeDtypeStruct((LR_PAD, GLA_QK), F32), jax.ShapeDtypeStruct((1, GLA_QK), F32)),
        grid=(nb,),
        in_specs=[pl.BlockSpec((rows, GLA_V), lambda i: (rev(i), 0)),
                  pl.BlockSpec((rows, GLA_QK), lambda i: (rev(i), OFF_Q // GLA_QK)),
                  pl.BlockSpec((rows, GLA_QK), lambda i: (rev(i), OFF_K // GLA_QK)),
                  pl.BlockSpec((rows, GLA_V), lambda i: (rev(i), OFF_V // GLA_V)),
                  pl.BlockSpec((rows, LR_PAD), lambda i: (rev(i), OFF_LR // LR_PAD)),
                  pl.BlockSpec((cb, GLA_QK, GLA_DV), lambda i: (rev(i), 0, 0)),
                  pl.BlockSpec((1, GLA_QK, GLA_DV), lambda i: (jnp.maximum(rev(i) * cb - 1, 0), 0, 0)),
                  pl.BlockSpec((LR_PAD, GLA_QK), lambda i: (0, 0)),
                  pl.BlockSpec((1, GLA_QK), lambda i: (0, 0)), _ANY],
        out_specs=(pl.BlockSpec((rows, qkv), lambda i: (rev(i), 0)),
                   pl.BlockSpec((rows, LR_PAD), lambda i: (rev(i), 0)),
                   pl.BlockSpec((LR_PAD, GLA_QK), lambda i: (0, 0)),
                   pl.BlockSpec((1, GLA_QK), lambda i: (0, 0))),
        input_output_aliases={9: 0},
        scratch_shapes=[pltpu.VMEM((GLA_QK, GLA_DV), F32), pltpu.VMEM((rows, GLA_QK), F32)],
        compiler_params=_params(("arbitrary",)),
    )(do, p, p, p, p, st, st, wa, ba, dp)


def _gla_out_fwd(o, p, gng, name):
    t = o.shape[0]
    tm = min(TM_EW, t)

    def body(o_ref, r_ref, g_ref, z_ref):
        gv = g_ref[...]
        for h in range(GLA_HEADS):
            vs = slice(h * GLA_DV, (h + 1) * GLA_DV)
            ov, rv = o_ref[:, vs], r_ref[:, vs]
            z_ref[:, vs] = ((ov * _rstd(ov) * gv) * (rv * _sigmoid(rv))).astype(BF16)

    return pl.pallas_call(
        body, name=name, out_shape=jax.ShapeDtypeStruct((t, GLA_V), BF16), grid=(t // tm,),
        in_specs=[pl.BlockSpec((tm, GLA_V), lambda i: (i, 0)),
                  pl.BlockSpec((tm, GLA_V), lambda i: (i, OFF_R // GLA_V)),
                  pl.BlockSpec((1, GLA_DV), lambda i: (0, 0))],
        out_specs=pl.BlockSpec((tm, GLA_V), lambda i: (i, 0)),
        compiler_params=_params(("parallel",)),
    )(o, p, gng)


def _gla_out_bwd(dz, o, p, gng, dp, name):
    t = o.shape[0]
    tm = min(TM_EW, t)

    def body(dz_ref, o_ref, r_ref, g_ref, dp_in, do_ref, dr_ref, dg_ref):
        @pl.when(pl.program_id(0) == 0)
        def _():
            dg_ref[...] = jnp.zeros_like(dg_ref)

        gv = g_ref[...]
        for h in range(GLA_HEADS):
            vs = slice(h * GLA_DV, (h + 1) * GLA_DV)
            ov, rv, dzv = o_ref[:, vs], r_ref[:, vs], dz_ref[:, vs]
            rs = _rstd(ov)
            oh = ov * rs
            sg = _sigmoid(rv)
            dr_ref[:, vs] = (dzv * (oh * gv) * (sg * (1.0 + rv * (1.0 - sg)))).astype(BF16)
            don = dzv * (rv * sg)
            dg_ref[...] += jnp.sum(don * oh, axis=0, keepdims=True)
            doh = don * gv
            do_ref[:, vs] = rs * (doh - oh * jnp.mean(doh * oh, axis=-1, keepdims=True))

    row = pl.BlockSpec((tm, GLA_V), lambda i: (i, 0))
    r_s = pl.BlockSpec((tm, GLA_V), lambda i: (i, OFF_R // GLA_V))
    return pl.pallas_call(
        body, name=name,
        out_shape=(jax.ShapeDtypeStruct((t, GLA_V), F32), _dp_shape(t), jax.ShapeDtypeStruct((1, GLA_DV), F32)),
        grid=(t // tm,),
        in_specs=[row, row, r_s, pl.BlockSpec((1, GLA_DV), lambda i: (0, 0)), _ANY],
        out_specs=(row, r_s, pl.BlockSpec((1, GLA_DV), lambda i: (0, 0))),
        input_output_aliases={4: 1},
        compiler_params=_params(("arbitrary",)),
    )(dz, o, p, gng, dp)


def _ada_fwd(c_all, w, b, layer, name):
    n = w.shape[2]
    tn = _pick(n, 512)

    def body(c_ref, w_ref, b_ref, o_ref):
        cv = c_ref[...]
        o_ref[...] = _dot(cv * _sigmoid(cv), w_ref[...], _NN) + b_ref[...]

    return pl.pallas_call(
        body, name=name, out_shape=jax.ShapeDtypeStruct((16, n), F32), grid=(n // tn,),
        in_specs=[pl.BlockSpec((16, D_MODEL), lambda j: (0, 0)),
                  pl.BlockSpec((None, D_MODEL, tn), lambda j: (layer, 0, j)),
                  pl.BlockSpec((1, tn), lambda j: (0, j))],
        out_specs=pl.BlockSpec((16, tn), lambda j: (0, j)),
        compiler_params=_params(("parallel",)),
    )(c_all, w, b)


def _ada_bwd(c_all, dmod, name):
    n = dmod.shape[2]
    tn = _pick(n, 512)

    def body(c_ref, d_ref, o_ref):
        cv = c_ref[...]
        o_ref[...] = _dot(cv * _sigmoid(cv), d_ref[...], _TN)

    return pl.pallas_call(
        body, name=name, out_shape=jax.ShapeDtypeStruct((DEPTH, D_MODEL, n), F32), grid=(DEPTH, n // tn),
        in_specs=[pl.BlockSpec((16, D_MODEL), lambda l, j: (0, 0)),
                  pl.BlockSpec((None, 16, tn), lambda l, j: (l, 0, j))],
        out_specs=pl.BlockSpec((None, D_MODEL, tn), lambda l, j: (l, 0, j)),
        compiler_params=_params(("parallel", "parallel")),
    )(c_all, dmod)


def _rows_tile(nrows, ncols, target_bytes):
    want = max(16, target_bytes // (4 * ncols))
    if nrows <= want:
        return nrows
    t = (want // 16) * 16
    while t >= 16:
        if nrows % t == 0:
            return t
        t -= 16
    return nrows


def _sum_chips(sent, landed, chip, name):
    _, nrows, ncols = sent[0].shape
    tr = _rows_tile(nrows, ncols, 2 << 20)
    nblk = nrows // tr

    def body(chip_ref, *refs):
        own, got, o_ref = refs[:DEPTH], refs[DEPTH:2 * DEPTH], refs[2 * DEPTH]
        me = chip_ref[0]
        for l in range(DEPTH):
            for j in range(N_CHIPS):
                def add(val):
                    if j == 0:
                        o_ref[...] = val.astype(F32)
                    else:
                        o_ref[...] += val.astype(F32)

                @pl.when(jnp.logical_and(pl.program_id(0) == l, me == j))
                def _():
                    add(own[l][...])

                @pl.when(jnp.logical_and(pl.program_id(0) == l, me != j))
                def _():
                    add(got[l][j])

    def rows_of(layer):
        return lambda l, i, chip_ref: jnp.where(l == layer, i, 0)

    own_specs = [pl.BlockSpec((None, tr, ncols), lambda l, i, chip_ref, r=rows_of(k): (chip_ref[0], r(l, i, chip_ref), 0))
                 for k in range(DEPTH)]
    got_specs = [pl.BlockSpec((N_CHIPS, tr, ncols), lambda l, i, chip_ref, r=rows_of(k): (0, r(l, i, chip_ref), 0))
                 for k in range(DEPTH)]
    return pl.pallas_call(
        body, name=name, out_shape=jax.ShapeDtypeStruct((DEPTH, nrows, ncols), F32),
        grid_spec=pltpu.PrefetchScalarGridSpec(
            num_scalar_prefetch=1, grid=(DEPTH, nblk), in_specs=own_specs + got_specs,
            out_specs=pl.BlockSpec((None, tr, ncols), lambda l, i, chip_ref: (l, i, 0))),
        compiler_params=_params(("arbitrary", "arbitrary")),
    )(chip, *sent, *landed)


def _adamw(w, m, v, ga, gb, name, tile=None):
    two = gb is not None
    c1 = 1.0 - ADAM_B1 ** ADAM_STEP
    c2 = 1.0 - ADAM_B2 ** ADAM_STEP

    def body(*refs):
        if two:
            w_ref, m_ref, v_ref, ga_ref, gb_ref, g_ref, d_ref, nm_ref, nv_ref = refs
            g = ga_ref[...] + gb_ref[...]
        else:
            w_ref, m_ref, v_ref, ga_ref, g_ref, d_ref, nm_ref, nv_ref = refs
            g = ga_ref[...]
        g_ref[...] = g
        nm = ADAM_B1 * m_ref[...] + (1.0 - ADAM_B1) * g
        nv = ADAM_B2 * v_ref[...] + (1.0 - ADAM_B2) * (g * g)
        nm_ref[...] = nm
        nv_ref[...] = nv
        d_ref[...] = -ADAM_LR * ((nm / c1) / (jnp.sqrt(nv / c2) + ADAM_EPS) + ADAM_WD * w_ref[...])

    if tile is None:
        nl, nrows, ncols = w.shape
        tr = _rows_tile(nrows, ncols, 1 << 20)
        blk = pl.BlockSpec((None, tr, ncols), lambda l, i: (l, i, 0))
        grid = (nl, nrows // tr)
    else:
        nrows, nl, ncols = w.shape
        rb, cb = tile
        blk = pl.BlockSpec((rb, nl, cb), lambda i, j: (i, 0, j))
        grid = (nrows // rb, ncols // cb)
    sh = jax.ShapeDtypeStruct(w.shape, F32)
    ins = [w, m, v, ga] + ([gb] if two else [])
    return pl.pallas_call(
        body, name=name, out_shape=(sh, sh, sh, sh), grid=grid,
        in_specs=[blk] * len(ins), out_specs=(blk, blk, blk, blk),
        compiler_params=_params(("parallel", "parallel")),
    )(*ins)


def _pad_rows(a, rows):
    return jnp.concatenate([a, jnp.zeros((rows - a.shape[0],) + a.shape[1:], a.dtype)], axis=0)


N_IN_CHIP = N_IN // N_CHIPS
_LR_LO = 3072 - N_IN_CHIP
_LR_HI = _LR_LO + GLA_LOWRANK


def _w_in_from_chips(a):
    return jnp.concatenate([a[0], a[1][:, :_LR_LO], a[1][:, _LR_HI:], a[2], a[3], a[1][:, _LR_LO:_LR_HI],
                            jnp.zeros((a.shape[1], LR_PAD - GLA_LOWRANK), a.dtype)], axis=1)


def _w_in_to_chips(w):
    s2 = 2 * N_IN_CHIP - GLA_LOWRANK
    s3 = s2 + N_IN_CHIP
    c1 = jnp.concatenate([w[:, N_IN_CHIP:3072], w[:, OFF_LR:OFF_LR + GLA_LOWRANK], w[:, 3072:s2]], axis=1)
    return jnp.stack([w[:, :N_IN_CHIP], c1, w[:, s2:s3], w[:, s3:OFF_LR]])


_BIG = ("w_in", "w_og", "w_oc", "w_o", "w_up", "w_dn")
_ROW_SHARDED = ("w_o", "w_dn")


def kernel(x, c, w_ada, b_ada, norm_g, w_in, w_a2, b_a2, gla_norm_g, w_out_gla, conv_mix_w, w_out_conv, w_o, w_up, ffn_conv_w, w_down, loss_target, m_w_ada, m_b_ada, m_norm_g, m_w_in, m_w_a2, m_b_a2, m_gla_norm_g, m_w_out_gla, m_conv_mix_w, m_w_out_conv, m_w_o, m_w_up, m_ffn_conv_w, m_w_down, v_w_ada, v_b_ada, v_norm_g, v_w_in, v_w_a2, v_b_a2, v_gla_norm_g, v_w_out_gla, v_conv_mix_w, v_w_out_conv, v_w_o, v_w_up, v_ffn_conv_w, v_w_down):
    xi, yi, ci = lax.axis_index("x"), lax.axis_index("y"), lax.axis_index("c")
    chip = 2 * xi + yi
    dev = 2 * chip + ci
    chip_arr = jnp.reshape(chip, (1,)).astype(jnp.int32)
    xt = x[0]
    tgt = loss_target[0]

    c_all = _allgather8(jnp.broadcast_to(c, (8, D_MODEL)), "gather_c")[0][:, 0, :]
    c16 = _pad_rows(c_all, 16)
    sm_parts = [norm_g.reshape(-1), w_a2.reshape(-1), conv_mix_w.reshape(-1), ffn_conv_w.reshape(-1)]
    sm_sizes = [a.shape[0] for a in sm_parts]
    sm_flat = jnp.concatenate(sm_parts)
    sm_rows = -(-sm_flat.shape[0] // 128)
    sm_rows = -(-sm_rows // 8) * 8
    sm_flat = jnp.concatenate([sm_flat, jnp.zeros((sm_rows * 128 - sm_flat.shape[0],), F32)]).reshape(sm_rows, 128)
    sm_all = _allgather8(sm_flat, "gather_small")[0].reshape(N_DEV, -1)[0::2]
    offs = [0]
    for s in sm_sizes:
        offs.append(offs[-1] + s)

    def small_full(idx, shape):
        a = sm_all[:, offs[idx]:offs[idx + 1]].reshape((N_CHIPS,) + shape)
        a = jnp.moveaxis(a, 0, -2)
        return a.reshape(shape[:-1] + (N_CHIPS * shape[-1],))

    norm_g_f = small_full(0, (DEPTH, 4, 512))
    w_a2_f = small_full(1, (DEPTH, GLA_LOWRANK, 128))
    conv_w_f = small_full(2, (DEPTH, 3, 256))
    ffn_w_f = small_full(3, (DEPTH, 3, 1408))

    b_loc = lax.dynamic_slice(b_ada, (0, chip * 3072), (DEPTH, 3072))
    mod_loc = jnp.concatenate(
        [_ada_fwd(c16, w_ada, b_loc[l:l + 1], l, "ada_fwd")[:8] for l in range(DEPTH)], axis=0)
    mod_all = _allgather8(mod_loc, "gather_mod")[0][0::2]
    mods = []
    for l in range(DEPTH):
        row = lax.dynamic_slice(mod_all, (0, l * 8 + dev, 0), (N_CHIPS, 1, 3072)).reshape(1, 6 * D_MODEL)
        mods.append([row[:, k * D_MODEL:(k + 1) * D_MODEL] for k in range(6)])

    big = dict(w_in=w_in, w_og=w_out_gla, w_oc=w_out_conv, w_o=w_o, w_up=w_up, w_dn=w_down)
    gathers = {}
    tok = 0.0 * (mod_all[0, 0, 0] + sm_all[0, 0])
    for l in range(DEPTH):
        for k in _BIG:
            shard = (big[k][l] + tok).astype(BF16)
            land = lax.dynamic_update_slice(lax.empty((N_CHIPS,) + shard.shape, BF16), shard[None], (chip, 0, 0))
            *handle, token = _gather_start(land, "gather_start_%s_%d" % (k, l))
            gathers[k, l] = tuple(handle)
            tok = token[0, 0]

    def gathered(k, l, after):
        full = _gather_wait(gathers[k, l], after, "gather_wait_%s_%d" % (k, l))
        if k in _ROW_SHARDED:
            return full.reshape(N_CHIPS * full.shape[1], full.shape[2])
        return _w_in_from_chips(full) if k == "w_in" else full

    saved = []
    h = None
    xin = xt
    for l in range(DEPTH):
        sh1, sc1, g1, sh2, sc2, g2 = mods[l]
        gn = [norm_g_f[l, k][None] for k in range(4)]
        wa = _pad_rows(w_a2_f[l], LR_PAD)
        ba = b_a2[l][None]
        gng = gla_norm_g[l][None]
        cw8 = _pad_rows(conv_w_f[l], 8)
        fw8 = _pad_rows(ffn_w_f[l], 8)
        if l == 0:
            h = _pre_norm(xin, gn[0] + tok, sc1, sh1, "pre_norm")
        wi = gathered("w_in", l, h)
        p = _matmul(h, wi, "nn", F32, "mm_in", tn=1152)
        o, st = _gla_fwd(p, wa, ba, "gla_fwd")
        za = _gla_out_fwd(o, p, gng, "gla_out_fwd")
        zb = _conv_fwd(p, cw8, "conv_fwd")
        wog, woc = gathered("w_og", l, zb), gathered("w_oc", l, zb)
        ya = _matmul(za, wog, "nn", F32, "mm_out_gla", b_chips=True)
        yb = _matmul(zb, woc, "nn", F32, "mm_out_conv", b_chips=True)
        mm = _merge_fwd(ya, yb, p, "merge_fwd")
        wo = gathered("w_o", l, mm)
        y = _matmul(mm, wo, "nn", F32, "mm_o")
        x1, h2 = _post_pre(xin, y, g1, gn[1], gn[2], sc2, sh2, "post_pre")
        wup = gathered("w_up", l, h2)
        u = _matmul(h2, wup, "nn", F32, "mm_up", tn=1408, b_chips=True)
        f = _ffn_fwd(u, fw8, "ffn_fwd")
        wdn = gathered("w_dn", l, f)
        y2 = _matmul(f, wdn, "nn", F32, "mm_down", tm=1024, tn=2048, tk=1408)
        saved.append(dict(xin=xin, h=h, p=p, o=o, st=st, za=za, zb=zb, ya=ya, yb=yb, mm=mm, y=y, x1=x1, h2=h2,
                          u=u, f=f, y2=y2, wi=wi, wog=wog, woc=woc, wo=wo, wup=wup, wdn=wdn, wa=wa, ba=ba,
                          gng=gng, cw8=cw8, fw8=fw8, gn=gn, mod=mods[l]))
        if l + 1 < DEPTH:
            nsh1, nsc1 = mods[l + 1][0], mods[l + 1][1]
            xin, h = _post_pre(x1, y2, g2, gn[3], norm_g_f[l + 1, 0][None], nsc1, nsh1, "post_pre")
        else:
            dx, loss_tile = _post_loss(x1, y2, g2, gn[3], tgt, "post_loss")
    loss = lax.psum(loss_tile[0, 0], ("x", "y", "c"))

    scatters = {}

    def scatter(k, l, dw):
        if k in _ROW_SHARDED:
            send = dw.reshape(N_CHIPS, dw.shape[0] // N_CHIPS, dw.shape[1])
        else:
            send = _w_in_to_chips(dw) if k == "w_in" else dw
        *handle, token = _scatter_start(send, "scatter_start_%s_%d" % (k, l))
        scatters[k, l] = tuple(handle)
        return token[0, 0]

    sm = {k: [None] * DEPTH for k in ("dmod", "norm_g", "w_a2", "b_a2", "gng", "conv_w", "ffn_w")}
    for l in reversed(range(DEPTH)):
        s = saved[l]
        sh1, sc1, g1, sh2, sc2, g2 = s["mod"]
        gn = s["gn"]
        dy2, dg2, dgn3 = _post_bwd(dx, s["y2"], g2, gn[3], "post_bwd")
        tk = scatter("w_dn", l, _matmul(s["f"], dy2, "tn", BF16, "mm_down_dw", tm=512, tn=1024, tk=4096))
        df = _matmul(dy2, s["wdn"], "nt", F32, "mm_down_dx", tn=1408)
        dgate, dup, dfw = _ffn_bwd(df, s["u"], s["fw8"] + tk, "ffn_bwd")
        du = (dgate, dup)
        tk = scatter("w_up", l, _matmul(s["h2"], du, "tn", BF16, "mm_up_dw", tm=1024, tn=1408, tk=2048, out_chips=True))
        dh2 = _matmul(du, s["wup"], "nt", F32, "mm_up_dx", tm=1024, tn=2048, tk=1408, b_chips=True)
        dx1, dsh2, dsc2, dgn2 = _pre_bwd(dh2, s["x1"], dx, gn[2] + tk, sc2, "pre_bwd")
        dy, dg1, dgn1 = _post_bwd(dx1, s["y"], g1, gn[1], "post_bwd")
        tk = scatter("w_o", l, _matmul(s["mm"], dy, "tn", BF16, "mm_o_dw", tk=4096))
        dm = _matmul(dy, s["wo"], "nt", F32, "mm_o_dx")
        dya, dp = _merge_bwd(dm, s["ya"], s["p"], OFF_GA, None, "merge_bwd_a")
        dyb, dp = _merge_bwd(dm, s["yb"], s["p"], OFF_GB, dp, "merge_bwd_b")
        tk = tk + scatter("w_og", l, _matmul(s["za"], dya, "tn", BF16, "mm_out_gla_dw", tk=4096, out_chips=True))
        dza = _matmul(dya, s["wog"], "nt", F32, "mm_out_gla_dx", b_chips=True)
        do, dp, dgng = _gla_out_bwd(dza, s["o"], s["p"], s["gng"] + tk, dp, "gla_out_bwd")
        tk = scatter("w_oc", l, _matmul(s["zb"], dyb, "tn", BF16, "mm_out_conv_dw", tk=4096, out_chips=True))
        dzb = _matmul(dyb, s["woc"], "nt", F32, "mm_out_conv_dx", b_chips=True)
        dp, dcw = _conv_bwd(dzb, s["p"], s["cw8"] + tk, dp, "conv_bwd")
        dp, dlr, dwa, dba = _gla_bwd(do, s["p"], s["st"], s["wa"], s["ba"], dp, "gla_bwd")
        dp = lax.dynamic_update_slice(dp, dlr, (0, OFF_LR))
        dw_in = _matmul(s["h"], dp, "tn", BF16, "mm_in_dw", tm=512, tn=1152, tk=4096)
        tk = scatter("w_in", l, dw_in) if l > 0 else 0.0
        dh = _matmul(dp, s["wi"], "nt", F32, "mm_in_dx", tm=1024, tn=2048, tk=1152)
        dx, dsh1, dsc1, dgn0 = _pre_bwd(dh, s["xin"], dx1, gn[0] + tk, sc1, "pre_bwd")
        sm["dmod"][l] = jnp.concatenate([dsh1, dsc1, dg1, dsh2, dsc2, dg2], axis=1)[0]
        sm["norm_g"][l] = jnp.concatenate([dgn0, dgn1, dgn2, dgn3], axis=0)
        sm["w_a2"][l] = dwa[:GLA_LOWRANK]
        sm["b_a2"][l] = dba[0]
        sm["gng"][l] = dgng[0]
        sm["conv_w"][l] = dcw[:3]
        sm["ffn_w"][l] = dfw[:3]
    grad_x = dx[None]

    names = ("dmod", "norm_g", "w_a2", "b_a2", "gng", "conv_w", "ffn_w")
    parts = [jnp.stack(sm[k]).reshape(-1) for k in names]
    shapes = [jnp.stack(sm[k]).shape for k in names]
    sizes = [a.shape[0] for a in parts]
    flat = jnp.concatenate(parts)
    rows = -(-flat.shape[0] // 1024) * 8
    flat = jnp.concatenate([flat, jnp.zeros((rows * 128 - flat.shape[0],), F32)]).reshape(rows, 128)
    gath, tot = _allgather8(flat, "reduce_small")
    tk = scatter("w_in", 0, dw_in + (0.0 * tot[0, 0]).astype(BF16))
    c16 = c16 + tk
    po = [0]
    for s_ in sizes:
        po.append(po[-1] + s_)
    tot = tot.reshape(-1)
    tot_of = {k: tot[po[i]:po[i + 1]].reshape(shapes[i]) for i, k in enumerate(names)}
    dmod_all = gath.reshape(N_DEV, -1)[:, po[0]:po[1]].reshape(N_DEV, DEPTH, 6 * D_MODEL)

    def chip_cols(a, width):
        return lax.dynamic_slice_in_dim(a, chip * width, width, axis=a.ndim - 1)

    dml = jnp.transpose(chip_cols(dmod_all, 3072), (1, 0, 2))
    dml = jnp.concatenate([dml, jnp.zeros_like(dml)], axis=1)
    g_w_ada = _ada_bwd(c16, dml, "ada_bwd")

    def upd(w, m, v, ga, gb, name):
        sh = w.shape
        as3 = sh if len(sh) == 3 else (1,) + sh
        outs = _adamw(w.reshape(as3), m.reshape(as3), v.reshape(as3), ga.reshape(as3),
                      None if gb is None else gb.reshape(as3), name)
        return [a.reshape(sh) for a in outs]

    res = {}
    res["w_ada"] = upd(w_ada, m_w_ada, v_w_ada, g_w_ada, None, "adamw")
    res["b_ada"] = upd(b_ada, m_b_ada, v_b_ada, tot_of["dmod"], None, "adamw")
    res["norm_g"] = upd(norm_g, m_norm_g, v_norm_g, chip_cols(tot_of["norm_g"], 512), None, "adamw")
    res["w_a2"] = upd(w_a2, m_w_a2, v_w_a2, chip_cols(tot_of["w_a2"], 128), None, "adamw")
    res["b_a2"] = upd(b_a2, m_b_a2, v_b_a2, tot_of["b_a2"], None, "adamw")
    res["gla_norm_g"] = upd(gla_norm_g, m_gla_norm_g, v_gla_norm_g, tot_of["gng"], None, "adamw")
    res["conv_mix_w"] = upd(conv_mix_w, m_conv_mix_w, v_conv_mix_w, chip_cols(tot_of["conv_w"], 256), None, "adamw")
    res["ffn_conv_w"] = upd(ffn_conv_w, m_ffn_conv_w, v_ffn_conv_w, chip_cols(tot_of["ffn_w"], 1408), None, "adamw")

    full_name = dict(w_in="w_in", w_og="w_out_gla", w_oc="w_out_conv", w_o="w_o", w_up="w_up", w_dn="w_down")
    state = dict(w_in=(w_in, m_w_in, v_w_in), w_og=(w_out_gla, m_w_out_gla, v_w_out_gla),
                 w_oc=(w_out_conv, m_w_out_conv, v_w_out_conv), w_o=(w_o, m_w_o, v_w_o),
                 w_up=(w_up, m_w_up, v_w_up), w_dn=(w_down, m_w_down, v_w_down))
    after = res["w_ada"][3]
    for k in ("w_dn", "w_up", "w_o", "w_og", "w_oc", "w_in"):
        done = [_scatter_wait(scatters[k, l], after, "scatter_wait_%s_%d" % (k, l)) for l in range(DEPTH)]
        plane = _sum_chips([d[0] for d in done], [d[1] for d in done], chip_arr, "sum_chips")
        if k == "w_in":
            plane = jnp.transpose(plane, (2, 0, 1))
            other = _sibling_exchange([plane], "sibling_" + k)[0]
            outs = _adamw(*[jnp.transpose(a, (2, 0, 1)) for a in state[k]], plane, other, "adamw_w_in",
                          tile=(N_IN_CHIP // 4, D_MODEL // 8))
            res[full_name[k]] = [jnp.transpose(a, (1, 2, 0)) for a in outs]
        else:
            other = _sibling_exchange([plane], "sibling_" + k)[0]
            res[full_name[k]] = upd(*state[k], plane, other, "adamw")
        after = res[full_name[k]][3]
    order = ("w_ada", "b_ada", "norm_g", "w_in", "w_a2", "b_a2", "gla_norm_g", "w_out_gla", "conv_mix_w",
             "w_out_conv", "w_o", "w_up", "ffn_conv_w", "w_down")
    return (loss, grad_x, *[res[k][0] for k in order], *[res[k][1] for k in order],
            *[res[k][2] for k in order], *[res[k][3] for k in order])
```

```python
import functools
import math

import jax
import jax.numpy as jnp
from jax import lax
from jax.experimental import pallas as pl
from jax.experimental.pallas import tpu as pltpu

F32 = jnp.float32
BF16 = jnp.bfloat16
MESH = pl.DeviceIdType.MESH

D_MODEL = 2048
DEPTH = 2
CHUNK = 64
GLA_HEADS = 4
GLA_DK = 128
GLA_DV = 256
GLA_QK = GLA_HEADS * GLA_DK
GLA_V = GLA_HEADS * GLA_DV
GLA_LOWRANK = 16
GLA_TAU = 16.0
CONV_WIDTH = 1024
D_FF = 5632
EPS = 1e-6
N_IN = 10256
LR_PAD = 128
N_IN_PAD = N_IN - GLA_LOWRANK + LR_PAD
OFF_Q, OFF_K, OFF_V, OFF_R = 0, 512, 1024, 2048
OFF_CB, OFF_CC, OFF_CX, OFF_GA, OFF_GB, OFF_LR = 3072, 4096, 5120, 6144, 8192, 10240

ADAM_LR = 0.001
ADAM_B1 = 0.9
ADAM_B2 = 0.999
ADAM_EPS = 1e-08
ADAM_WD = 0.01
ADAM_STEP = 10

N_CHIPS = 4
N_DEV = 8
VMEM_LIMIT = 56 * 1024 * 1024
TM_ROW = 256
TM_EW = 512
CW_EW = 512
GLA_ROWS = 256


def _params(sem=None):
    return pltpu.CompilerParams(dimension_semantics=sem, vmem_limit_bytes=VMEM_LIMIT)


def _sigmoid(v):
    return 1.0 / (1.0 + jnp.exp(-v))


def _log_sigmoid(v):
    return jnp.minimum(v, 0.0) - jnp.log(1.0 + jnp.exp(-jnp.abs(v)))


_GELU_C = math.sqrt(2.0 / math.pi)


def _gelu_and_grad(v):
    v2 = v * v
    t = jnp.tanh(_GELU_C * v * (1.0 + 0.044715 * v2))
    half = 0.5 * (1.0 + t)
    return v * half, half + (0.5 * _GELU_C) * v * (1.0 - t * t) * (1.0 + (3.0 * 0.044715) * v2)


def _ld(ref):
    return ref[...].astype(F32)


def _flip(a, d):
    return a + d - 2 * a * d


def _unless(cond):
    return jnp.where(cond, 0.0, 1.0).astype(F32)


def _allgather8(xv, name):
    r, cdim = xv.shape

    def body(x_ref, out_ref, sum_ref, send_sems, recv_sems):
        xi, yi, ci = lax.axis_index("x"), lax.axis_index("y"), lax.axis_index("c")
        me = 4 * xi + 2 * yi + ci
        out_ref[pl.ds(me, 1)] = x_ref[...][None]
        started = []
        for k in range(1, N_DEV):
            px, py, pc = _flip(xi, (k >> 2) & 1), _flip(yi, (k >> 1) & 1), _flip(ci, k & 1)
            cp = pltpu.make_async_remote_copy(
                src_ref=x_ref, dst_ref=out_ref.at[me], send_sem=send_sems.at[k - 1], recv_sem=recv_sems.at[k - 1],
                device_id=(px, py, pc), device_id_type=MESH)
            cp.start()
            started.append((cp, 4 * px + 2 * py + pc, k, (px, py, pc)))
        for cp, peer, k, pid in started:
            cp.wait_send()
            pltpu.make_async_remote_copy(
                src_ref=x_ref, dst_ref=out_ref.at[peer], send_sem=send_sems.at[k - 1], recv_sem=recv_sems.at[k - 1],
                device_id=pid, device_id_type=MESH).wait_recv()
        acc = out_ref[0]
        for d in range(1, N_DEV):
            acc = acc + out_ref[d]
        sum_ref[...] = acc

    return pl.pallas_call(
        body, name=name,
        out_shape=(jax.ShapeDtypeStruct((N_DEV, r, cdim), F32), jax.ShapeDtypeStruct((r, cdim), F32)),
        in_specs=[pl.BlockSpec(memory_space=pltpu.VMEM)],
        out_specs=(pl.BlockSpec(memory_space=pltpu.VMEM), pl.BlockSpec(memory_space=pltpu.VMEM)),
        scratch_shapes=[pltpu.SemaphoreType.DMA((N_DEV - 1,)), pltpu.SemaphoreType.DMA((N_DEV - 1,))],
        compiler_params=pltpu.CompilerParams(vmem_limit_bytes=VMEM_LIMIT),
    )(xv)


_HBM = pl.BlockSpec(memory_space=pltpu.HBM)
_SEM = pl.BlockSpec(memory_space=pltpu.SEMAPHORE)
_EFFECT = pltpu.SideEffectType.DATAFLOW_SIDE_EFFECTING
_CHIP_FLIPS = ((1, 0), (0, 1), (1, 1))


def _chip_copies(src_ref, land_ref, send_sems, recv_sems, scatter):
    xi, yi, ci = lax.axis_index("x"), lax.axis_index("y"), lax.axis_index("c")
    me = 2 * xi + yi
    out = []
    for k, (dx, dy) in enumerate(_CHIP_FLIPS):
        px, py = _flip(xi, dx), _flip(yi, dy)
        peer = 2 * px + py
        src = src_ref.at[peer] if scatter else land_ref.at[me]
        mk = functools.partial(pltpu.make_async_remote_copy, src_ref=src, send_sem=send_sems.at[k],
                               recv_sem=recv_sems.at[k], device_id=(px, py, ci), device_id_type=MESH)
        out.append((mk(dst_ref=land_ref.at[me]), mk(dst_ref=land_ref.at[peer])))
    return out


def _gather_start(land, name):
    def body(land_ref, send_sems, recv_sems, land_thru, token):
        for mine, _ in _chip_copies(None, land_ref, send_sems, recv_sems, False):
            mine.start()
        token[...] = jnp.zeros_like(token)

    return pl.pallas_call(
        body, name=name,
        out_shape=(pltpu.SemaphoreType.DMA((3,)), pltpu.SemaphoreType.DMA((3,)), pltpu.HBM(land.shape, land.dtype),
                   jax.ShapeDtypeStruct((8, 128), F32)),
        in_specs=(_HBM,), out_specs=(_SEM, _SEM, _HBM, pl.BlockSpec(memory_space=pltpu.VMEM)),
        input_output_aliases={0: 2},
        compiler_params=pltpu.CompilerParams(has_side_effects=_EFFECT),
    )(pltpu.with_memory_space_constraint(land, pltpu.HBM))


def _gather_wait(handle, after, name):
    send, recv, land_thru = handle

    def body(land_ref, send_sems, recv_sems, after_ref, land_out):
        for mine, theirs in _chip_copies(None, land_ref, send_sems, recv_sems, False):
            mine.wait_send()
            theirs.wait_recv()

    return pl.pallas_call(
        body, name=name, out_shape=pltpu.HBM(land_thru.shape, land_thru.dtype),
        in_specs=(_HBM, _SEM, _SEM, pl.BlockSpec(memory_space=pl.ANY)), out_specs=_HBM,
        input_output_aliases={0: 0},
        compiler_params=pltpu.CompilerParams(has_side_effects=_EFFECT),
    )(land_thru, send, recv, after)


def _scatter_start(src, name):
    def body(src_ref, land_ref, send_sems, recv_sems, src_thru, land_thru, token):
        for mine, _ in _chip_copies(src_ref, land_ref, send_sems, recv_sems, True):
            mine.start()
        token[...] = jnp.zeros_like(token)

    return pl.pallas_call(
        body, name=name,
        out_shape=(pltpu.SemaphoreType.DMA((3,)), pltpu.SemaphoreType.DMA((3,)), pltpu.HBM(src.shape, src.dtype),
                   pltpu.HBM(src.shape, src.dtype), jax.ShapeDtypeStruct((8, 128), F32)),
        in_specs=(_HBM, _HBM), out_specs=(_SEM, _SEM, _HBM, _HBM, pl.BlockSpec(memory_space=pltpu.VMEM)),
        input_output_aliases={0: 2, 1: 3},
        compiler_params=pltpu.CompilerParams(has_side_effects=_EFFECT),
    )(pltpu.with_memory_space_constraint(src, pltpu.HBM),
      pltpu.with_memory_space_constraint(lax.empty(src.shape, src.dtype), pltpu.HBM))


def _scatter_wait(handle, after, name):
    send, recv, src_thru, land_thru = handle

    def body(src_ref, land_ref, send_sems, recv_sems, after_ref, src_out, land_out):
        for mine, theirs in _chip_copies(src_ref, land_ref, send_sems, recv_sems, True):
            mine.wait_send()
            theirs.wait_recv()

    return pl.pallas_call(
        body, name=name,
        out_shape=(pltpu.HBM(src_thru.shape, src_thru.dtype), pltpu.HBM(land_thru.shape, land_thru.dtype)),
        in_specs=(_HBM, _HBM, _SEM, _SEM, pl.BlockSpec(memory_space=pl.ANY)), out_specs=(_HBM, _HBM),
        input_output_aliases={0: 0, 1: 1},
        compiler_params=pltpu.CompilerParams(has_side_effects=_EFFECT),
    )(src_thru, land_thru, send, recv, after)


def _sibling_exchange(arrays, name):
    n = len(arrays)

    def body(*refs):
        ins, outs = refs[:n], refs[n:2 * n]
        send_sems, recv_sems = refs[2 * n:]
        xi, yi, ci = lax.axis_index("x"), lax.axis_index("y"), lax.axis_index("c")
        cps = []
        for i in range(n):
            cp = pltpu.make_async_remote_copy(
                src_ref=ins[i], dst_ref=outs[i], send_sem=send_sems.at[i], recv_sem=recv_sems.at[i],
                device_id=(xi, yi, 1 - ci), device_id_type=MESH)
            cp.start()
            cps.append(cp)
        for cp in cps:
            cp.wait()

    return pl.pallas_call(
        body, name=name, out_shape=tuple(jax.ShapeDtypeStruct(a.shape, a.dtype) for a in arrays),
        in_specs=[pl.BlockSpec(memory_space=pl.ANY)] * n,
        out_specs=tuple(pl.BlockSpec(memory_space=pl.ANY) for _ in range(n)),
        scratch_shapes=[pltpu.SemaphoreType.DMA((n,)), pltpu.SemaphoreType.DMA((n,))],
    )(*arrays)


def _pick(dim, pref):
    if dim <= pref:
        return dim
    t = (pref // 128) * 128
    while t >= 128:
        if dim % t == 0:
            return t
        t -= 128
    return dim


def _matmul(a, b, dims, out_dtype, name, tm=512, tn=1024, tk=2048, out_chips=False, b_chips=False):
    a_parts = a if isinstance(a, tuple) else (a,)
    b_parts = b if isinstance(b, tuple) else (b,)
    na, nb = len(a_parts), len(b_parts)
    assert (na == 1 or dims == "nt") and (nb == 1 or dims == "tn")
    b_shape = (b_parts[0].shape[1], N_CHIPS * b_parts[0].shape[2]) if b_chips else b_parts[0].shape
    if dims == "nn":
        (m, kd), (_, n) = a_parts[0].shape, b_shape
    elif dims == "nt":
        (m, kd), (n, _) = a_parts[0].shape, b_shape
        kd = na * kd
    else:
        (kd, m), (_, n) = a_parts[0].shape, b_shape
        n = nb * n
    tm = _pick(m, tm)
    tn = _pick(n // N_CHIPS, tn) if (out_chips or (b_chips and dims == "nn")) else _pick(n // nb, tn)
    tk = _pick(kd // N_CHIPS, tk) if (b_chips and dims == "nt") else _pick(kd // na, tk)
    nk, nj = kd // tk, n // tn
    ka, jb = nk // na, nj // nb
    if out_chips:
        per_chip = n // N_CHIPS // tn
        out_shape = jax.ShapeDtypeStruct((N_CHIPS, m, n // N_CHIPS), out_dtype)
        out_spec = pl.BlockSpec((None, tm, tn), lambda j, i, k: (j // per_chip, i, j % per_chip))
    else:
        out_shape = jax.ShapeDtypeStruct((m, n), out_dtype)
        out_spec = pl.BlockSpec((tm, tn), lambda j, i, k: (i, j))
    def part_of(idx, first, count):
        return jnp.clip(idx - first, 0, count - 1)

    if dims == "nn":
        a_specs = [pl.BlockSpec((tm, tk), lambda j, i, k: (i, k))]
        b_specs = [pl.BlockSpec((tk, tn), lambda j, i, k: (k, j))]
        dn = (((1,), (0,)), ((), ()))
    elif dims == "nt":
        a_specs = [pl.BlockSpec((tm, tk), lambda j, i, k, p=p: (i, part_of(k, p * ka, ka))) for p in range(na)]
        b_specs = [pl.BlockSpec((tn, tk), lambda j, i, k: (j, k))]
        dn = (((1,), (1,)), ((), ()))
    else:
        a_specs = [pl.BlockSpec((tk, tm), lambda j, i, k: (k, i))]
        b_specs = [pl.BlockSpec((tk, tn), lambda j, i, k, p=p: (k, part_of(j, p * jb, jb))) for p in range(nb)]
        dn = (((0,), (0,)), ((), ()))
    if b_chips and dims == "nn":
        nper = n // N_CHIPS // tn
        b_specs = [pl.BlockSpec((None, tk, tn), lambda j, i, k: (j // nper, k, j % nper))]
    elif b_chips:
        kper = kd // N_CHIPS // tk
        b_specs = [pl.BlockSpec((None, tn, tk), lambda j, i, k: (k // kper, j, k % kper))]
    direct = nk == 1 or out_dtype == F32

    def body(*refs):
        a_refs, b_refs, o_ref = refs[:na], refs[na:na + nb], refs[na + nb]
        acc_ref = o_ref if direct else refs[na + nb + 1]
        j, k = pl.program_id(0), pl.program_id(2)

        def step(a_ref, b_ref):
            part = lax.dot_general(a_ref[...].astype(BF16), b_ref[...].astype(BF16), dn, preferred_element_type=F32)
            if nk == 1:
                o_ref[...] = part.astype(o_ref.dtype)
                return

            @pl.when(k == 0)
            def _():
                acc_ref[...] = part

            @pl.when(k > 0)
            def _():
                acc_ref[...] += part

            if not direct:
                @pl.when(k == nk - 1)
                def _():
                    o_ref[...] = acc_ref[...].astype(o_ref.dtype)

        if na == 1 and nb == 1:
            step(a_refs[0], b_refs[0])
        for p in range(na if na > 1 else 0):
            pl.when(jnp.logical_and(k >= p * ka, k < (p + 1) * ka))(functools.partial(step, a_refs[p], b_refs[0]))
        for p in range(nb if nb > 1 else 0):
            pl.when(jnp.logical_and(j >= p * jb, j < (p + 1) * jb))(functools.partial(step, a_refs[0], b_refs[p]))

    return pl.pallas_call(
        body, name=name, out_shape=out_shape,
        grid=(nj, m // tm, nk),
        in_specs=a_specs + b_specs,
        out_specs=out_spec,
        scratch_shapes=[] if direct else [pltpu.VMEM((tm, tn), F32)],
        compiler_params=_params(("parallel", "parallel", "arbitrary")),
    )(*a_parts, *b_parts)


def _rstd(v):
    return lax.rsqrt(jnp.mean(v * v, axis=-1, keepdims=True) + EPS)


def _row(tm):
    return pl.BlockSpec((tm, D_MODEL), lambda i: (i, 0))


_VEC = pl.BlockSpec((1, D_MODEL), lambda i: (0, 0))


def _pre_norm(x, gn, sc, sh, name):
    t = x.shape[0]
    tm = min(TM_ROW, t)

    def body(x_ref, gn_ref, sc_ref, sh_ref, h_ref):
        xv = x_ref[...]
        h_ref[...] = ((xv * _rstd(xv) * gn_ref[...]) * (1.0 + sc_ref[...]) + sh_ref[...]).astype(BF16)

    return pl.pallas_call(
        body, name=name, out_shape=jax.ShapeDtypeStruct((t, D_MODEL), BF16), grid=(t // tm,),
        in_specs=[_row(tm), _VEC, _VEC, _VEC], out_specs=_row(tm),
        compiler_params=_params(("parallel",)),
    )(x, gn, sc, sh)


def _post_pre(x, y, g, gnp, gn, sc, sh, name):
    t = x.shape[0]
    tm = min(TM_ROW, t)

    def body(x_ref, y_ref, g_ref, gnp_ref, gn_ref, sc_ref, sh_ref, x1_ref, h_ref):
        yv = y_ref[...]
        x1 = x_ref[...] + g_ref[...] * (yv * _rstd(yv) * gnp_ref[...])
        x1_ref[...] = x1
        h_ref[...] = ((x1 * _rstd(x1) * gn_ref[...]) * (1.0 + sc_ref[...]) + sh_ref[...]).astype(BF16)

    return pl.pallas_call(
        body, name=name,
        out_shape=(jax.ShapeDtypeStruct((t, D_MODEL), F32), jax.ShapeDtypeStruct((t, D_MODEL), BF16)),
        grid=(t // tm,),
        in_specs=[_row(tm), _row(tm), _VEC, _VEC, _VEC, _VEC, _VEC], out_specs=(_row(tm), _row(tm)),
        compiler_params=_params(("parallel",)),
    )(x, y, g, gnp, gn, sc, sh)


def _post_loss(x, y, g, gnp, tgt, name):
    t = x.shape[0]
    tm = min(TM_ROW, t)

    def body(x_ref, y_ref, g_ref, gnp_ref, t_ref, dx_ref, loss_ref):
        yv = y_ref[...]
        diff = x_ref[...] + g_ref[...] * (yv * _rstd(yv) * gnp_ref[...]) - t_ref[...]
        dx_ref[...] = diff * (1.0 / D_MODEL)
        part = (0.5 / D_MODEL) * jnp.sum(jnp.sum(diff * diff, axis=-1, keepdims=True), axis=0, keepdims=True)

        @pl.when(pl.program_id(0) == 0)
        def _():
            loss_ref[...] = jnp.zeros_like(loss_ref)

        loss_ref[...] += jnp.broadcast_to(part, loss_ref.shape)

    return pl.pallas_call(
        body, name=name,
        out_shape=(jax.ShapeDtypeStruct((t, D_MODEL), F32), jax.ShapeDtypeStruct((8, 128), F32)),
        grid=(t // tm,),
        in_specs=[_row(tm), _row(tm), _VEC, _VEC, _row(tm)],
        out_specs=(_row(tm), pl.BlockSpec((8, 128), lambda i: (0, 0))),
        compiler_params=_params(("arbitrary",)),
    )(x, y, g, gnp, tgt)


def _acc_rows(ref, val):
    @pl.when(pl.program_id(0) == 0)
    def _():
        ref[...] = jnp.zeros_like(ref)

    ref[...] += jnp.sum(val, axis=0, keepdims=True)


def _post_bwd(dxn, y, g, gnp, name):
    t = y.shape[0]
    tm = min(TM_ROW, t)

    def body(dx_ref, y_ref, g_ref, gnp_ref, dy_ref, dg_ref, dgn_ref):
        yv, dxv = y_ref[...], dx_ref[...]
        r = _rstd(yv)
        yh = yv * r
        _acc_rows(dg_ref, dxv * (yh * gnp_ref[...]))
        dn = dxv * g_ref[...]
        _acc_rows(dgn_ref, dn * yh)
        dyh = dn * gnp_ref[...]
        dy_ref[...] = (r * (dyh - yh * jnp.mean(dyh * yh, axis=-1, keepdims=True))).astype(BF16)

    return pl.pallas_call(
        body, name=name,
        out_shape=(jax.ShapeDtypeStruct((t, D_MODEL), BF16), jax.ShapeDtypeStruct((1, D_MODEL), F32),
                   jax.ShapeDtypeStruct((1, D_MODEL), F32)),
        grid=(t // tm,),
        in_specs=[_row(tm), _row(tm), _VEC, _VEC], out_specs=(_row(tm), _VEC, _VEC),
        compiler_params=_params(("arbitrary",)),
    )(dxn, y, g, gnp)


def _pre_bwd(dh, xin, dres, gn, sc, name):
    t = xin.shape[0]
    tm = min(TM_ROW, t)

    def body(dh_ref, x_ref, dres_ref, gn_ref, sc_ref, dx_ref, dsh_ref, dsc_ref, dgn_ref):
        xv, dhv = x_ref[...], dh_ref[...]
        r = _rstd(xv)
        xh = xv * r
        _acc_rows(dsh_ref, dhv)
        _acc_rows(dsc_ref, dhv * (xh * gn_ref[...]))
        dn = dhv * (1.0 + sc_ref[...])
        _acc_rows(dgn_ref, dn * xh)
        dxh = dn * gn_ref[...]
        dx_ref[...] = dres_ref[...] + r * (dxh - xh * jnp.mean(dxh * xh, axis=-1, keepdims=True))

    vec = jax.ShapeDtypeStruct((1, D_MODEL), F32)
    return pl.pallas_call(
        body, name=name, out_shape=(jax.ShapeDtypeStruct((t, D_MODEL), F32), vec, vec, vec),
        grid=(t // tm,),
        in_specs=[_row(tm), _row(tm), _row(tm), _VEC, _VEC], out_specs=(_row(tm), _VEC, _VEC, _VEC),
        compiler_params=_params(("arbitrary",)),
    )(dh, xin, dres, gn, sc)


def _shift_down(v, halo, s):
    hr = halo.shape[0]
    out = pltpu.roll(v, s, 0)
    row = lax.broadcasted_iota(jnp.int32, v.shape, 0)
    for j in range(s):
        out = jnp.where(row == j, jnp.broadcast_to(halo[hr - s + j:hr - s + j + 1, :], v.shape), out)
    return out


def _shift_up(v, halo, s):
    tm = v.shape[0]
    out = pltpu.roll(v, tm - s, 0)
    row = lax.broadcasted_iota(jnp.int32, v.shape, 0)
    for j in range(s):
        out = jnp.where(row == tm - s + j, jnp.broadcast_to(halo[j:j + 1, :], v.shape), out)
    return out


def _tile_specs(tm, cw, off, nrow, hr=8):
    ob = off // cw
    per = tm // hr
    main = pl.BlockSpec((tm, cw), lambda j, i: (i, ob + j))
    prev = pl.BlockSpec((hr, cw), lambda j, i: (jnp.maximum(i * per - 1, 0), ob + j))
    nxt = pl.BlockSpec((hr, cw), lambda j, i: (jnp.minimum((i + 1) * per, nrow * per - 1), ob + j))
    return main, prev, nxt


def _conv_fwd(p, w, name):
    t = p.shape[0]
    tm, cw = min(TM_EW, t), CW_EW
    nrow = t // tm
    cb_s, _, _ = _tile_specs(tm, cw, OFF_CB, nrow, 16)
    cc_s, cc_p, _ = _tile_specs(tm, cw, OFF_CC, nrow, 16)
    cx_s, cx_p, _ = _tile_specs(tm, cw, OFF_CX, nrow, 16)

    def body(cb_ref, cc_ref, ccp_ref, cx_ref, cxp_ref, w_ref, z_ref):
        u = _ld(cc_ref) * _ld(cx_ref)
        uh = _ld(ccp_ref) * _ld(cxp_ref) * _unless(pl.program_id(1) == 0)
        wv = w_ref[...]
        conv = wv[2:3, :] * u + wv[1:2, :] * _shift_down(u, uh, 1) + wv[0:1, :] * _shift_down(u, uh, 2)
        z_ref[...] = (_ld(cb_ref) * conv).astype(BF16)

    return pl.pallas_call(
        body, name=name, out_shape=jax.ShapeDtypeStruct((t, CONV_WIDTH), BF16),
        grid=(CONV_WIDTH // cw, nrow),
        in_specs=[cb_s, cc_s, cc_p, cx_s, cx_p, pl.BlockSpec((8, cw), lambda j, i: (0, j))],
        out_specs=pl.BlockSpec((tm, cw), lambda j, i: (i, j)),
        compiler_params=_params(("parallel", "arbitrary")),
    )(p, p, p, p, p, w)


def _acc_w(ref, vals):
    @pl.when(pl.program_id(1) == 0)
    def _():
        ref[...] = jnp.zeros_like(ref)

    for j, v in enumerate(vals):
        ref[j:j + 1, :] += jnp.sum(v, axis=0, keepdims=True)


def _conv_bwd(dz, p, w, dp, name):
    t = p.shape[0]
    tm, cw = min(TM_EW // 2, t), CONV_WIDTH
    nrow = t // tm
    dz_s, _, dz_n = _tile_specs(tm, cw, 0, nrow)
    cb_s, _, cb_n = _tile_specs(tm, cw, OFF_CB, nrow, 16)
    cc_s, cc_p, _ = _tile_specs(tm, cw, OFF_CC, nrow, 16)
    cx_s, cx_p, _ = _tile_specs(tm, cw, OFF_CX, nrow, 16)

    def body(dz_ref, dzn_ref, cb_ref, cbn_ref, cc_ref, ccp_ref, cx_ref, cxp_ref, w_ref, dp_in, dp_ref, dw_ref):
        dcb_ref = dp_ref.at[:, 0:cw]
        dcc_ref = dp_ref.at[:, cw:2 * cw]
        dcx_ref = dp_ref.at[:, 2 * cw:3 * cw]
        i = pl.program_id(1)
        ccv, cxv, dzv = _ld(cc_ref), _ld(cx_ref), dz_ref[...]
        u = ccv * cxv
        uh = _ld(ccp_ref) * _ld(cxp_ref) * _unless(i == 0)
        wv = w_ref[...]
        u1, u2 = _shift_down(u, uh, 1), _shift_down(u, uh, 2)
        conv = wv[2:3, :] * u + wv[1:2, :] * u1 + wv[0:1, :] * u2
        dcb_ref[...] = (dzv * conv).astype(BF16)
        dconv = dzv * _ld(cb_ref)
        dch = dzn_ref[...] * _ld(cbn_ref)[0:8, :] * _unless(i == nrow - 1)
        du = wv[2:3, :] * dconv + wv[1:2, :] * _shift_up(dconv, dch, 1) + wv[0:1, :] * _shift_up(dconv, dch, 2)
        dcc_ref[...] = (du * cxv).astype(BF16)
        dcx_ref[...] = (du * ccv).astype(BF16)
        _acc_w(dw_ref, (dconv * u2, dconv * u1, dconv * u))

    w_s = pl.BlockSpec((8, cw), lambda j, i: (0, j))
    return pl.pallas_call(
        body, name=name, out_shape=(_dp_shape(t), jax.ShapeDtypeStruct((8, CONV_WIDTH), F32)),
        grid=(1, nrow),
        in_specs=[dz_s, dz_n, cb_s, cb_n, cc_s, cc_p, cx_s, cx_p, w_s, _ANY],
        out_specs=(pl.BlockSpec((tm, 3 * cw), lambda j, i: (i, OFF_CB // (3 * cw))), w_s),
        input_output_aliases={9: 0},
        compiler_params=_params(("parallel", "arbitrary")),
    )(dz, dz, p, p, p, p, p, p, w, dp)


def _ffn_fwd(u, w, name):
    t = u.shape[0]
    tm, cw = min(TM_EW, t), CW_EW
    nrow = t // tm
    g_s, g_p, _ = _tile_specs(tm, cw, 0, nrow, 16)
    u_s, _, _ = _tile_specs(tm, cw, D_FF, nrow, 16)

    def body(g_ref, gp_ref, u_ref, w_ref, f_ref):
        gv = _ld(g_ref)
        gh = _ld(gp_ref) * _unless(pl.program_id(1) == 0)
        wv = w_ref[...]
        gc = wv[2:3, :] * gv + wv[1:2, :] * _shift_down(gv, gh, 1) + wv[0:1, :] * _shift_down(gv, gh, 2)
        f_ref[...] = (_gelu_and_grad(gc)[0] * _ld(u_ref)).astype(BF16)

    return pl.pallas_call(
        body, name=name, out_shape=jax.ShapeDtypeStruct((t, D_FF), BF16),
        grid=(D_FF // cw, nrow),
        in_specs=[g_s, g_p, u_s, pl.BlockSpec((8, cw), lambda j, i: (0, j))],
        out_specs=pl.BlockSpec((tm, cw), lambda j, i: (i, j)),
        compiler_params=_params(("parallel", "arbitrary")),
    )(u, u, u, w)


def _ffn_bwd(df, u, w, name):
    t = u.shape[0]
    tm, cw = min(TM_EW, t), CW_EW
    nrow = t // tm
    df_s, _, df_n = _tile_specs(tm, cw, 0, nrow)
    g_s, g_p, g_n = _tile_specs(tm, cw, 0, nrow, 16)
    u_s, _, u_n = _tile_specs(tm, cw, D_FF, nrow, 16)

    def body(df_ref, dfn_ref, g_ref, gp_ref, gn_ref, u_ref, un_ref, w_ref, dg_ref, du_ref, dw_ref):
        i = pl.program_id(1)
        gv, dfv, uv = _ld(g_ref), df_ref[...], _ld(u_ref)
        gh = _ld(gp_ref) * _unless(i == 0)
        wv = w_ref[...]
        g1, g2 = _shift_down(gv, gh, 1), _shift_down(gv, gh, 2)
        gc = wv[2:3, :] * gv + wv[1:2, :] * g1 + wv[0:1, :] * g2
        act, act_grad = _gelu_and_grad(gc)
        du_ref[...] = (dfv * act).astype(BF16)
        dgc = dfv * uv * act_grad
        gnv = _ld(gn_ref)[0:8, :]
        gtail = gv[tm - 8:tm, :]
        gcn = (wv[2:3, :] * gnv + wv[1:2, :] * _shift_down(gnv, gtail, 1) + wv[0:1, :] * _shift_down(gnv, gtail, 2))
        dgcn = dfn_ref[...] * _ld(un_ref)[0:8, :] * _gelu_and_grad(gcn)[1] * _unless(i == nrow - 1)
        dg = wv[2:3, :] * dgc + wv[1:2, :] * _shift_up(dgc, dgcn, 1) + wv[0:1, :] * _shift_up(dgc, dgcn, 2)
        dg_ref[...] = dg.astype(BF16)
        _acc_w(dw_ref, (dgc * g2, dgc * g1, dgc * gv))

    o_s = pl.BlockSpec((tm, cw), lambda j, i: (i, j))
    o_sh = jax.ShapeDtypeStruct((t, D_FF), BF16)
    w_s = pl.BlockSpec((8, cw), lambda j, i: (0, j))
    return pl.pallas_call(
        body, name=name, out_shape=(o_sh, o_sh, jax.ShapeDtypeStruct((8, D_FF), F32)),
        grid=(D_FF // cw, nrow),
        in_specs=[df_s, df_n, g_s, g_p, g_n, u_s, u_n, w_s],
        out_specs=(o_s, o_s, w_s),
        compiler_params=_params(("parallel", "arbitrary")),
    )(df, df, u, u, u, u, u, w)


def _merge_fwd(ya, yb, p, name):
    t = ya.shape[0]
    tm, cw = min(TM_EW, t), CW_EW
    y_s = pl.BlockSpec((tm, cw), lambda i, j: (i, j))

    def body(ya_ref, yb_ref, ga_ref, gb_ref, m_ref):
        m_ref[...] = (_sigmoid(_ld(ga_ref)) * ya_ref[...] + _sigmoid(_ld(gb_ref)) * yb_ref[...]).astype(BF16)

    return pl.pallas_call(
        body, name=name, out_shape=jax.ShapeDtypeStruct((t, D_MODEL), BF16),
        grid=(t // tm, D_MODEL // cw),
        in_specs=[y_s, y_s, pl.BlockSpec((tm, cw), lambda i, j: (i, OFF_GA // cw + j)),
                  pl.BlockSpec((tm, cw), lambda i, j: (i, OFF_GB // cw + j))],
        out_specs=y_s, compiler_params=_params(("parallel", "parallel")),
    )(ya, yb, p, p)


_ANY = pl.BlockSpec(memory_space=pl.ANY)


def _dp_shape(t):
    return jax.ShapeDtypeStruct((t, N_IN_PAD), BF16)


def _merge_bwd(dm, y, p, gate_off, dp, name):
    t = y.shape[0]
    tm, cw = min(TM_EW, t), CW_EW
    y_s = pl.BlockSpec((tm, cw), lambda i, j: (i, j))
    g_s = pl.BlockSpec((tm, cw), lambda i, j: (i, gate_off // cw + j))

    def body(dm_ref, y_ref, g_ref, *rest):
        dy_ref, dp_ref = rest[-2:]
        dmv = dm_ref[...]
        sg = _sigmoid(_ld(g_ref))
        dy_ref[...] = (dmv * sg).astype(BF16)
        dp_ref[...] = (dmv * y_ref[...] * sg * (1.0 - sg)).astype(BF16)

    extra = [] if dp is None else [dp]
    return pl.pallas_call(
        body, name=name, out_shape=(jax.ShapeDtypeStruct((t, D_MODEL), BF16), _dp_shape(t)),
        grid=(t // tm, D_MODEL // cw),
        in_specs=[y_s, y_s, g_s] + [_ANY] * len(extra),
        out_specs=(y_s, g_s), input_output_aliases={} if dp is None else {3: 1},
        compiler_params=_params(("parallel", "parallel")),
    )(dm, y, p, *extra)


def _tri(lower):
    r = lax.broadcasted_iota(jnp.int32, (CHUNK, CHUNK), 0)
    c = lax.broadcasted_iota(jnp.int32, (CHUNK, CHUNK), 1)
    return ((c <= r) if lower else (c >= r)).astype(F32)


def _eye_mask():
    r = lax.broadcasted_iota(jnp.int32, (GLA_DK, GLA_DK), 0)
    c = lax.broadcasted_iota(jnp.int32, (GLA_DK, GLA_DK), 1)
    return r == c


def _row_to_col(v):
    return jnp.sum(jnp.where(_eye_mask(), jnp.broadcast_to(v, (GLA_DK, GLA_DK)), 0.0), axis=1, keepdims=True)


def _col_to_row(v):
    return jnp.sum(jnp.where(_eye_mask(), jnp.broadcast_to(v, (GLA_DK, GLA_DK)), 0.0), axis=0, keepdims=True)


def _dot(a, b, dn):
    return lax.dot_general(a.astype(BF16), b.astype(BF16), (dn, ((), ())), preferred_element_type=F32)


_NN = ((1,), (0,))
_NT = ((1,), (1,))
_TN = ((0,), (0,))


def _gate_logits(lr_ref, wa_ref, ba_ref):
    return _dot(lr_ref[...], wa_ref[...], _NN) + ba_ref[...]


def _chunk_decay(la, tri):
    cum = lax.dot_general(tri, la, ((_NN), ((), ())), precision=lax.Precision.HIGHEST, preferred_element_type=F32)
    e = cum[CHUNK - 1:CHUNK, :]
    return cum, e, jnp.exp(e - cum)


def _gla_fwd(p, wa, ba, name):
    t = p.shape[0]
    rows = min(GLA_ROWS, t)
    cb = rows // CHUNK
    nc = t // CHUNK
    scale = GLA_DK ** -0.5

    def body(q_ref, k_ref, v_ref, lr_ref, wa_ref, ba_ref, o_ref, st_ref, s_scr):
        @pl.when(pl.program_id(0) == 0)
        def _():
            s_scr[...] = jnp.zeros_like(s_scr)

        la_all = _log_sigmoid(_gate_logits(lr_ref, wa_ref, ba_ref)) * (1.0 / GLA_TAU)
        tri = _tri(True)
        for ch in range(cb):
            rs = slice(ch * CHUNK, (ch + 1) * CHUNK)
            for h in range(GLA_HEADS):
                ks = slice(h * GLA_DK, (h + 1) * GLA_DK)
                vs = slice(h * GLA_DV, (h + 1) * GLA_DV)
                _, e, w = _chunk_decay(la_all[rs, ks], tri)
                kd = k_ref[rs, ks].astype(F32) * w
                s_new = _row_to_col(jnp.exp(e)) * s_scr[ks, :] + _dot(kd, v_ref[rs, vs], _TN)
                s_scr[ks, :] = s_new
                st_ref[ch, ks, :] = s_new
                o_ref[rs, vs] = _dot(q_ref[rs, ks].astype(F32) * scale, s_new, _NN)

    return pl.pallas_call(
        body, name=name,
        out_shape=(jax.ShapeDtypeStruct((t, GLA_V), F32), jax.ShapeDtypeStruct((nc, GLA_QK, GLA_DV), F32)),
        grid=(t // rows,),
        in_specs=[pl.BlockSpec((rows, GLA_QK), lambda i: (i, OFF_Q // GLA_QK)),
                  pl.BlockSpec((rows, GLA_QK), lambda i: (i, OFF_K // GLA_QK)),
                  pl.BlockSpec((rows, GLA_V), lambda i: (i, OFF_V // GLA_V)),
                  pl.BlockSpec((rows, LR_PAD), lambda i: (i, OFF_LR // LR_PAD)),
                  pl.BlockSpec((LR_PAD, GLA_QK), lambda i: (0, 0)),
                  pl.BlockSpec((1, GLA_QK), lambda i: (0, 0))],
        out_specs=(pl.BlockSpec((rows, GLA_V), lambda i: (i, 0)),
                   pl.BlockSpec((cb, GLA_QK, GLA_DV), lambda i: (i, 0, 0))),
        scratch_shapes=[pltpu.VMEM((GLA_QK, GLA_DV), F32)],
        compiler_params=_params(("arbitrary",)),
    )(p, p, p, p, wa, ba)


def _gla_bwd(do, p, st, wa, ba, dp, name):
    t = p.shape[0]
    rows = min(GLA_ROWS, t)
    cb = rows // CHUNK
    nb = t // rows
    scale = GLA_DK ** -0.5

    def rev(i):
        return nb - 1 - i

    def body(do_ref, q_ref, k_ref, v_ref, lr_ref, st_ref, stp_ref, wa_ref, ba_ref, dp_in,
             dp_ref, dlr_ref, dwa_ref, dba_ref, ds_scr, dz_scr):
        dq_ref = dp_ref.at[:, OFF_Q:OFF_Q + GLA_QK]
        dk_ref = dp_ref.at[:, OFF_K:OFF_K + GLA_QK]
        dv_ref = dp_ref.at[:, OFF_V:OFF_V + GLA_V]
        i = pl.program_id(0)

        @pl.when(i == 0)
        def _():
            ds_scr[...] = jnp.zeros_like(ds_scr)
            dwa_ref[...] = jnp.zeros_like(dwa_ref)
            dba_ref[...] = jnp.zeros_like(dba_ref)

        z_all = _gate_logits(lr_ref, wa_ref, ba_ref)
        la_all = _log_sigmoid(z_all) * (1.0 / GLA_TAU)
        tri, triu = _tri(True), _tri(False)
        last_row = lax.broadcasted_iota(jnp.int32, (CHUNK, GLA_DK), 0) == CHUNK - 1
        keep_prev = _unless(i == nb - 1)
        for ch in reversed(range(cb)):
            rs = slice(ch * CHUNK, (ch + 1) * CHUNK)
            for h in range(GLA_HEADS):
                ks = slice(h * GLA_DK, (h + 1) * GLA_DK)
                vs = slice(h * GLA_DV, (h + 1) * GLA_DV)
                _, e, w = _chunk_decay(la_all[rs, ks], tri)
                kd = k_ref[rs, ks].astype(F32) * w
                exp_e = jnp.exp(e)
                s_c = st_ref[ch, ks, :]
                if ch > 0:
                    s_p = st_ref[ch - 1, ks, :]
                else:
                    s_p = stp_ref[0, ks, :] * keep_prev
                do_c = do_ref[rs, vs]
                vv = v_ref[rs, vs]
                ds_tot = ds_scr[ks, :] + _dot(q_ref[rs, ks].astype(F32) * scale, do_c, _TN)
                dq_ref[rs, ks] = (_dot(do_c, s_c, _NT) * scale).astype(BF16)
                dkd = _dot(vv, ds_tot, _NT)
                dv_ref[rs, vs] = _dot(kd, ds_tot, _NN).astype(BF16)
                dexp_col = jnp.sum(ds_tot * s_p, axis=1, keepdims=True)
                ds_scr[ks, :] = _row_to_col(exp_e) * ds_tot
                dk_ref[rs, ks] = (dkd * w).astype(BF16)
                dwt = dkd * kd
                de = jnp.sum(dwt, axis=0, keepdims=True) + _col_to_row(dexp_col) * exp_e
                dcum = jnp.where(last_row, de - dwt, -dwt)
                da = lax.dot_general(triu, dcum, (_NN, ((), ())), precision=lax.Precision.HIGHEST,
                                     preferred_element_type=F32)
                dz_scr[rs, ks] = da * (1.0 / GLA_TAU) * _sigmoid(-z_all[rs, ks])
        dz = dz_scr[...]
        dlr_ref[...] = _dot(dz, wa_ref[...], _NT).astype(BF16)
        dwa_ref[...] += _dot(lr_ref[...], dz, _TN)
        dba_ref[...] += jnp.sum(dz, axis=0, keepdims=True)

    qkv = OFF_V + GLA_V
    return pl.pallas_call(
        body, name=name,
        out_shape=(_dp_shape(t), jax.ShapeDtypeStruct((t, LR_PAD), BF16),
                   jax.ShapeDtypeStruct((LR_PAD, GLA_QK), F32), jax.ShapeDtypeStruct((1, GLA_QK), F32)),
        grid=(nb,),
        in_specs=[pl.BlockSpec((rows, GLA_V), lambda i: (rev(i), 0)),
                  pl.BlockSpec((rows, GLA_QK), lambda i: (rev(i), OFF_Q // GLA_QK)),
                  pl.BlockSpec((rows, GLA_QK), lambda i: (rev(i), OFF_K // GLA_QK)),
                  pl.BlockSpec((rows, GLA_V), lambda i: (rev(i), OFF_V // GLA_V)),
                  pl.BlockSpec((rows, LR_PAD), lambda i: (rev(i), OFF_LR // LR_PAD)),
                  pl.BlockSpec((cb, GLA_QK, GLA_DV), lambda i: (rev(i), 0, 0)),
                  pl.BlockSpec((1, GLA_QK, GLA_DV), lambda i: (jnp.maximum(rev(i) * cb - 1, 0), 0, 0)),
                  pl.BlockSpec((LR_PAD, GLA_QK), lambda i: (0, 0)),
                  pl.BlockSpec((1, GLA_QK), lambda i: (0, 0)), _ANY],
        out_specs=(pl.BlockSpec((rows, qkv), lambda i: (rev(i), 0)),
                   pl.BlockSpec((rows, LR_PAD), lambda i: (rev(i), 0)),
                   pl.BlockSpec((LR_PAD, GLA_QK), lambda i: (0, 0)),
                   pl.BlockSpec((1, GLA_QK), lambda i: (0, 0))),
        input_output_aliases={9: 0},
        scratch_shapes=[pltpu.VMEM((GLA_QK, GLA_DV), F32), pltpu.VMEM((rows, GLA_QK), F32)],
        compiler_params=_params(("arbitrary",)),
    )(do, p, p, p, p, st, st, wa, ba, dp)


def _gla_out_fwd(o, p, gng, name):
    t = o.shape[0]
    tm = min(TM_EW, t)

    def body(o_ref, r_ref, g_ref, z_ref):
        gv = g_ref[...]
        for h in range(GLA_HEADS):
            vs = slice(h * GLA_DV, (h + 1) * GLA_DV)
            ov, rv = o_ref[:, vs], r_ref[:, vs].astype(F32)
            z_ref[:, vs] = ((ov * _rstd(ov) * gv) * (rv * _sigmoid(rv))).astype(BF16)

    return pl.pallas_call(
        body, name=name, out_shape=jax.ShapeDtypeStruct((t, GLA_V), BF16), grid=(t // tm,),
        in_specs=[pl.BlockSpec((tm, GLA_V), lambda i: (i, 0)),
                  pl.BlockSpec((tm, GLA_V), lambda i: (i, OFF_R // GLA_V)),
                  pl.BlockSpec((1, GLA_DV), lambda i: (0, 0))],
        out_specs=pl.BlockSpec((tm, GLA_V), lambda i: (i, 0)),
        compiler_params=_params(("parallel",)),
    )(o, p, gng)


def _gla_out_bwd(dz, o, p, gng, dp, name):
    t = o.shape[0]
    tm = min(TM_EW, t)

    def body(dz_ref, o_ref, r_ref, g_ref, dp_in, do_ref, dr_ref, dg_ref):
        @pl.when(pl.program_id(0) == 0)
        def _():
            dg_ref[...] = jnp.zeros_like(dg_ref)

        gv = g_ref[...]
        for h in range(GLA_HEADS):
            vs = slice(h * GLA_DV, (h + 1) * GLA_DV)
            ov, rv, dzv = o_ref[:, vs], r_ref[:, vs].astype(F32), dz_ref[:, vs]
            rs = _rstd(ov)
            oh = ov * rs
            sg = _sigmoid(rv)
            dr_ref[:, vs] = (dzv * (oh * gv) * (sg * (1.0 + rv * (1.0 - sg)))).astype(BF16)
            don = dzv * (rv * sg)
            dg_ref[...] += jnp.sum(don * oh, axis=0, keepdims=True)
            doh = don * gv
            do_ref[:, vs] = rs * (doh - oh * jnp.mean(doh * oh, axis=-1, keepdims=True))

    row = pl.BlockSpec((tm, GLA_V), lambda i: (i, 0))
    r_s = pl.BlockSpec((tm, GLA_V), lambda i: (i, OFF_R // GLA_V))
    return pl.pallas_call(
        body, name=name,
        out_shape=(jax.ShapeDtypeStruct((t, GLA_V), F32), _dp_shape(t), jax.ShapeDtypeStruct((1, GLA_DV), F32)),
        grid=(t // tm,),
        in_specs=[row, row, r_s, pl.BlockSpec((1, GLA_DV), lambda i: (0, 0)), _ANY],
        out_specs=(row, r_s, pl.BlockSpec((1, GLA_DV), lambda i: (0, 0))),
        input_output_aliases={4: 1},
        compiler_params=_params(("arbitrary",)),
    )(dz, o, p, gng, dp)


def _ada_fwd(c_all, w, b, layer, name):
    n = w.shape[2]
    tn = _pick(n, 512)

    def body(c_ref, w_ref, b_ref, o_ref):
        cv = c_ref[...]
        o_ref[...] = _dot(cv * _sigmoid(cv), w_ref[...], _NN) + b_ref[...]

    return pl.pallas_call(
        body, name=name, out_shape=jax.ShapeDtypeStruct((16, n), F32), grid=(n // tn,),
        in_specs=[pl.BlockSpec((16, D_MODEL), lambda j: (0, 0)),
                  pl.BlockSpec((None, D_MODEL, tn), lambda j: (layer, 0, j)),
                  pl.BlockSpec((1, tn), lambda j: (0, j))],
        out_specs=pl.BlockSpec((16, tn), lambda j: (0, j)),
        compiler_params=_params(("parallel",)),
    )(c_all, w, b)


def _ada_bwd(c_all, dmod, name):
    n = dmod.shape[2]
    tn = _pick(n, 512)

    def body(c_ref, d_ref, o_ref):
        cv = c_ref[...]
        o_ref[...] = _dot(cv * _sigmoid(cv), d_ref[...], _TN)

    return pl.pallas_call(
        body, name=name, out_shape=jax.ShapeDtypeStruct((DEPTH, D_MODEL, n), F32), grid=(DEPTH, n // tn),
        in_specs=[pl.BlockSpec((16, D_MODEL), lambda l, j: (0, 0)),
                  pl.BlockSpec((None, 16, tn), lambda l, j: (l, 0, j))],
        out_specs=pl.BlockSpec((None, D_MODEL, tn), lambda l, j: (l, 0, j)),
        compiler_params=_params(("parallel", "parallel")),
    )(c_all, dmod)


def _rows_tile(nrows, ncols, target_bytes):
    want = max(16, target_bytes // (4 * ncols))
    if nrows <= want:
        return nrows
    t = (want // 16) * 16
    while t >= 16:
        if nrows % t == 0:
            return t
        t -= 16
    return nrows


def _sum_chips(sent, landed, chip, name):
    _, nrows, ncols = sent[0].shape
    tr = _rows_tile(nrows, ncols, 2 << 20)
    nblk = nrows // tr

    def body(chip_ref, *refs):
        own, got, o_ref = refs[:DEPTH], refs[DEPTH:2 * DEPTH], refs[2 * DEPTH]
        me = chip_ref[0]
        for l in range(DEPTH):
            for j in range(N_CHIPS):
                def add(val):
                    if j == 0:
                        o_ref[...] = val.astype(F32)
                    else:
                        o_ref[...] += val.astype(F32)

                @pl.when(jnp.logical_and(pl.program_id(0) == l, me == j))
                def _():
                    add(own[l][...])

                @pl.when(jnp.logical_and(pl.program_id(0) == l, me != j))
                def _():
                    add(got[l][j])

    def rows_of(layer):
        return lambda l, i, chip_ref: jnp.where(l == layer, i, 0)

    own_specs = [pl.BlockSpec((None, tr, ncols), lambda l, i, chip_ref, r=rows_of(k): (chip_ref[0], r(l, i, chip_ref), 0))
                 for k in range(DEPTH)]
    got_specs = [pl.BlockSpec((N_CHIPS, tr, ncols), lambda l, i, chip_ref, r=rows_of(k): (0, r(l, i, chip_ref), 0))
                 for k in range(DEPTH)]
    return pl.pallas_call(
        body, name=name, out_shape=jax.ShapeDtypeStruct((DEPTH, nrows, ncols), F32),
        grid_spec=pltpu.PrefetchScalarGridSpec(
            num_scalar_prefetch=1, grid=(DEPTH, nblk), in_specs=own_specs + got_specs,
            out_specs=pl.BlockSpec((None, tr, ncols), lambda l, i, chip_ref: (l, i, 0))),
        compiler_params=_params(("arbitrary", "arbitrary")),
    )(chip, *sent, *landed)


def _adamw(w, m, v, ga, gb, name, tile=None):
    two = gb is not None
    c1 = 1.0 - ADAM_B1 ** ADAM_STEP
    c2 = 1.0 - ADAM_B2 ** ADAM_STEP

    def body(*refs):
        if two:
            w_ref, m_ref, v_ref, ga_ref, gb_ref, g_ref, d_ref, nm_ref, nv_ref = refs
            g = ga_ref[...] + gb_ref[...]
        else:
            w_ref, m_ref, v_ref, ga_ref, g_ref, d_ref, nm_ref, nv_ref = refs
            g = ga_ref[...]
        g_ref[...] = g
        nm = ADAM_B1 * m_ref[...] + (1.0 - ADAM_B1) * g
        nv = ADAM_B2 * v_ref[...] + (1.0 - ADAM_B2) * (g * g)
        nm_ref[...] = nm
        nv_ref[...] = nv
        d_ref[...] = -ADAM_LR * ((nm / c1) / (jnp.sqrt(nv / c2) + ADAM_EPS) + ADAM_WD * w_ref[...])

    if tile is None:
        nl, nrows, ncols = w.shape
        tr = _rows_tile(nrows, ncols, 1 << 20)
        blk = pl.BlockSpec((None, tr, ncols), lambda l, i: (l, i, 0))
        grid = (nl, nrows // tr)
    else:
        nrows, nl, ncols = w.shape
        rb, cb = tile
        blk = pl.BlockSpec((rb, nl, cb), lambda i, j: (i, 0, j))
        grid = (nrows // rb, ncols // cb)
    sh = jax.ShapeDtypeStruct(w.shape, F32)
    ins = [w, m, v, ga] + ([gb] if two else [])
    return pl.pallas_call(
        body, name=name, out_shape=(sh, sh, sh, sh), grid=grid,
        in_specs=[blk] * len(ins), out_specs=(blk, blk, blk, blk),
        compiler_params=_params(("parallel", "parallel")),
    )(*ins)


def _pad_rows(a, rows):
    return jnp.concatenate([a, jnp.zeros((rows - a.shape[0],) + a.shape[1:], a.dtype)], axis=0)


N_IN_CHIP = N_IN // N_CHIPS
_LR_LO = 3072 - N_IN_CHIP
_LR_HI = _LR_LO + GLA_LOWRANK


def _w_in_from_chips(a):
    return jnp.concatenate([a[0], a[1][:, :_LR_LO], a[1][:, _LR_HI:], a[2], a[3], a[1][:, _LR_LO:_LR_HI],
                            jnp.zeros((a.shape[1], LR_PAD - GLA_LOWRANK), a.dtype)], axis=1)


def _w_in_to_chips(w):
    s2 = 2 * N_IN_CHIP - GLA_LOWRANK
    s3 = s2 + N_IN_CHIP
    c1 = jnp.concatenate([w[:, N_IN_CHIP:3072], w[:, OFF_LR:OFF_LR + GLA_LOWRANK], w[:, 3072:s2]], axis=1)
    return jnp.stack([w[:, :N_IN_CHIP], c1, w[:, s2:s3], w[:, s3:OFF_LR]])


_BIG = ("w_in", "w_og", "w_oc", "w_o", "w_up", "w_dn")
_ROW_SHARDED = ("w_o", "w_dn")


def kernel(x, c, w_ada, b_ada, norm_g, w_in, w_a2, b_a2, gla_norm_g, w_out_gla, conv_mix_w, w_out_conv, w_o, w_up, ffn_conv_w, w_down, loss_target, m_w_ada, m_b_ada, m_norm_g, m_w_in, m_w_a2, m_b_a2, m_gla_norm_g, m_w_out_gla, m_conv_mix_w, m_w_out_conv, m_w_o, m_w_up, m_ffn_conv_w, m_w_down, v_w_ada, v_b_ada, v_norm_g, v_w_in, v_w_a2, v_b_a2, v_gla_norm_g, v_w_out_gla, v_conv_mix_w, v_w_out_conv, v_w_o, v_w_up, v_ffn_conv_w, v_w_down):
    xi, yi, ci = lax.axis_index("x"), lax.axis_index("y"), lax.axis_index("c")
    chip = 2 * xi + yi
    dev = 2 * chip + ci
    chip_arr = jnp.reshape(chip, (1,)).astype(jnp.int32)
    xt = x[0]
    tgt = loss_target[0]

    c_all = _allgather8(jnp.broadcast_to(c, (8, D_MODEL)), "gather_c")[0][:, 0, :]
    c16 = _pad_rows(c_all, 16)
    sm_parts = [norm_g.reshape(-1), w_a2.reshape(-1), conv_mix_w.reshape(-1), ffn_conv_w.reshape(-1)]
    sm_sizes = [a.shape[0] for a in sm_parts]
    sm_flat = jnp.concatenate(sm_parts)
    sm_rows = -(-sm_flat.shape[0] // 128)
    sm_rows = -(-sm_rows // 8) * 8
    sm_flat = jnp.concatenate([sm_flat, jnp.zeros((sm_rows * 128 - sm_flat.shape[0],), F32)]).reshape(sm_rows, 128)
    sm_all = _allgather8(sm_flat, "gather_small")[0].reshape(N_DEV, -1)[0::2]
    offs = [0]
    for s in sm_sizes:
        offs.append(offs[-1] + s)

    def small_full(idx, shape):
        a = sm_all[:, offs[idx]:offs[idx + 1]].reshape((N_CHIPS,) + shape)
        a = jnp.moveaxis(a, 0, -2)
        return a.reshape(shape[:-1] + (N_CHIPS * shape[-1],))

    norm_g_f = small_full(0, (DEPTH, 4, 512))
    w_a2_f = small_full(1, (DEPTH, GLA_LOWRANK, 128))
    conv_w_f = small_full(2, (DEPTH, 3, 256))
    ffn_w_f = small_full(3, (DEPTH, 3, 1408))

    b_loc = lax.dynamic_slice(b_ada, (0, chip * 3072), (DEPTH, 3072))
    mod_loc = jnp.concatenate(
        [_ada_fwd(c16, w_ada, b_loc[l:l + 1], l, "ada_fwd")[:8] for l in range(DEPTH)], axis=0)
    mod_all = _allgather8(mod_loc, "gather_mod")[0][0::2]
    mods = []
    for l in range(DEPTH):
        row = lax.dynamic_slice(mod_all, (0, l * 8 + dev, 0), (N_CHIPS, 1, 3072)).reshape(1, 6 * D_MODEL)
        mods.append([row[:, k * D_MODEL:(k + 1) * D_MODEL] for k in range(6)])

    big = dict(w_in=w_in, w_og=w_out_gla, w_oc=w_out_conv, w_o=w_o, w_up=w_up, w_dn=w_down)
    gathers = {}
    tok = 0.0 * (mod_all[0, 0, 0] + sm_all[0, 0])
    for l in range(DEPTH):
        for k in _BIG:
            shard = (big[k][l] + tok).astype(BF16)
            land = lax.dynamic_update_slice(lax.empty((N_CHIPS,) + shard.shape, BF16), shard[None], (chip, 0, 0))
            *handle, token = _gather_start(land, "gather_start_%s_%d" % (k, l))
            gathers[k, l] = tuple(handle)
            tok = token[0, 0]

    def gathered(k, l, after):
        full = _gather_wait(gathers[k, l], after, "gather_wait_%s_%d" % (k, l))
        if k in _ROW_SHARDED:
            return full.reshape(N_CHIPS * full.shape[1], full.shape[2])
        return _w_in_from_chips(full) if k == "w_in" else full

    saved = []
    h = None
    xin = xt
    for l in range(DEPTH):
        sh1, sc1, g1, sh2, sc2, g2 = mods[l]
        gn = [norm_g_f[l, k][None] for k in range(4)]
        wa = _pad_rows(w_a2_f[l], LR_PAD)
        ba = b_a2[l][None]
        gng = gla_norm_g[l][None]
        cw8 = _pad_rows(conv_w_f[l], 8)
        fw8 = _pad_rows(ffn_w_f[l], 8)
        if l == 0:
            h = _pre_norm(xin, gn[0] + tok, sc1, sh1, "pre_norm")
        wi = gathered("w_in", l, h)
        p = _matmul(h, wi, "nn", BF16, "mm_in", tn=1152)
        o, st = _gla_fwd(p, wa, ba, "gla_fwd")
        za = _gla_out_fwd(o, p, gng, "gla_out_fwd")
        zb = _conv_fwd(p, cw8, "conv_fwd")
        wog, woc = gathered("w_og", l, zb), gathered("w_oc", l, zb)
        ya = _matmul(za, wog, "nn", F32, "mm_out_gla", b_chips=True)
        yb = _matmul(zb, woc, "nn", F32, "mm_out_conv", b_chips=True)
        mm = _merge_fwd(ya, yb, p, "merge_fwd")
        wo = gathered("w_o", l, mm)
        y = _matmul(mm, wo, "nn", F32, "mm_o")
        x1, h2 = _post_pre(xin, y, g1, gn[1], gn[2], sc2, sh2, "post_pre")
        wup = gathered("w_up", l, h2)
        u = _matmul(h2, wup, "nn", BF16, "mm_up", tn=1408, b_chips=True)
        f = _ffn_fwd(u, fw8, "ffn_fwd")
        wdn = gathered("w_dn", l, f)
        y2 = _matmul(f, wdn, "nn", F32, "mm_down", tm=1024, tn=2048, tk=1408)
        saved.append(dict(xin=xin, h=h, p=p, o=o, st=st, za=za, zb=zb, ya=ya, yb=yb, mm=mm, y=y, x1=x1, h2=h2,
                          u=u, f=f, y2=y2, wi=wi, wog=wog, woc=woc, wo=wo, wup=wup, wdn=wdn, wa=wa, ba=ba,
                          gng=gng, cw8=cw8, fw8=fw8, gn=gn, mod=mods[l]))
        if l + 1 < DEPTH:
            nsh1, nsc1 = mods[l + 1][0], mods[l + 1][1]
            xin, h = _post_pre(x1, y2, g2, gn[3], norm_g_f[l + 1, 0][None], nsc1, nsh1, "post_pre")
        else:
            dx, loss_tile = _post_loss(x1, y2, g2, gn[3], tgt, "post_loss")
    loss = lax.psum(loss_tile[0, 0], ("x", "y", "c"))

    scatters = {}

    def scatter(k, l, dw):
        if k in _ROW_SHARDED:
            send = dw.reshape(N_CHIPS, dw.shape[0] // N_CHIPS, dw.shape[1])
        else:
            send = _w_in_to_chips(dw) if k == "w_in" else dw
        *handle, token = _scatter_start(send, "scatter_start_%s_%d" % (k, l))
        scatters[k, l] = tuple(handle)
        return token[0, 0]

    sm = {k: [None] * DEPTH for k in ("dmod", "norm_g", "w_a2", "b_a2", "gng", "conv_w", "ffn_w")}
    for l in reversed(range(DEPTH)):
        s = saved[l]
        sh1, sc1, g1, sh2, sc2, g2 = s["mod"]
        gn = s["gn"]
        dy2, dg2, dgn3 = _post_bwd(dx, s["y2"], g2, gn[3], "post_bwd")
        tk = scatter("w_dn", l, _matmul(s["f"], dy2, "tn", BF16, "mm_down_dw", tm=512, tn=1024, tk=4096))
        df = _matmul(dy2, s["wdn"], "nt", F32, "mm_down_dx", tn=1408)
        dgate, dup, dfw = _ffn_bwd(df, s["u"], s["fw8"] + tk, "ffn_bwd")
        du = (dgate, dup)
        tk = scatter("w_up", l, _matmul(s["h2"], du, "tn", BF16, "mm_up_dw", tm=1024, tn=1408, tk=2048, out_chips=True))
        dh2 = _matmul(du, s["wup"], "nt", F32, "mm_up_dx", tm=1024, tn=2048, tk=1408, b_chips=True)
        dx1, dsh2, dsc2, dgn2 = _pre_bwd(dh2, s["x1"], dx, gn[2] + tk, sc2, "pre_bwd")
        dy, dg1, dgn1 = _post_bwd(dx1, s["y"], g1, gn[1], "post_bwd")
        tk = scatter("w_o", l, _matmul(s["mm"], dy, "tn", BF16, "mm_o_dw", tk=4096))
        dm = _matmul(dy, s["wo"], "nt", F32, "mm_o_dx")
        dya, dp = _merge_bwd(dm, s["ya"], s["p"], OFF_GA, None, "merge_bwd_a")
        dyb, dp = _merge_bwd(dm, s["yb"], s["p"], OFF_GB, dp, "merge_bwd_b")
        tk = tk + scatter("w_og", l, _matmul(s["za"], dya, "tn", BF16, "mm_out_gla_dw", tk=4096, out_chips=True))
        dza = _matmul(dya, s["wog"], "nt", F32, "mm_out_gla_dx", b_chips=True)
        do, dp, dgng = _gla_out_bwd(dza, s["o"], s["p"], s["gng"] + tk, dp, "gla_out_bwd")
        tk = scatter("w_oc", l, _matmul(s["zb"], dyb, "tn", BF16, "mm_out_conv_dw", tk=4096, out_chips=True))
        dzb = _matmul(dyb, s["woc"], "nt", F32, "mm_out_conv_dx", b_chips=True)
        dp, dcw = _conv_bwd(dzb, s["p"], s["cw8"] + tk, dp, "conv_bwd")
        dp, dlr, dwa, dba = _gla_bwd(do, s["p"], s["st"], s["wa"], s["ba"], dp, "gla_bwd")
        dp = lax.dynamic_update_slice(dp, dlr, (0, OFF_LR))
        dw_in = _matmul(s["h"], dp, "tn", BF16, "mm_in_dw", tm=512, tn=1152, tk=4096)
        tk = scatter("w_in", l, dw_in) if l > 0 else 0.0
        dh = _matmul(dp, s["wi"], "nt", F32, "mm_in_dx", tm=1024, tn=2048, tk=1152)
        dx, dsh1, dsc1, dgn0 = _pre_bwd(dh, s["xin"], dx1, gn[0] + tk, sc1, "pre_bwd")
        sm["dmod"][l] = jnp.concatenate([dsh1, dsc1, dg1, dsh2, dsc2, dg2], axis=1)[0]
        sm["norm_g"][l] = jnp.concatenate([dgn0, dgn1, dgn2, dgn3], axis=0)
        sm["w_a2"][l] = dwa[:GLA_LOWRANK]
        sm["b_a2"][l] = dba[0]
        sm["gng"][l] = dgng[0]
        sm["conv_w"][l] = dcw[:3]
        sm["ffn_w"][l] = dfw[:3]
    grad_x = dx[None]

    names = ("dmod", "norm_g", "w_a2", "b_a2", "gng", "conv_w", "ffn_w")
    parts = [jnp.stack(sm[k]).reshape(-1) for k in names]
    shapes = [jnp.stack(sm[k]).shape for k in names]
    sizes = [a.shape[0] for a in parts]
    flat = jnp.concatenate(parts)
    rows = -(-flat.shape[0] // 1024) * 8
    flat = jnp.concatenate([flat, jnp.zeros((rows * 128 - flat.shape[0],), F32)]).reshape(rows, 128)
    gath, tot = _allgather8(flat, "reduce_small")
    tk = scatter("w_in", 0, dw_in + (0.0 * tot[0, 0]).astype(BF16))
    c16 = c16 + tk
    po = [0]
    for s_ in sizes:
        po.append(po[-1] + s_)
    tot = tot.reshape(-1)
    tot_of = {k: tot[po[i]:po[i + 1]].reshape(shapes[i]) for i, k in enumerate(names)}
    dmod_all = gath.reshape(N_DEV, -1)[:, po[0]:po[1]].reshape(N_DEV, DEPTH, 6 * D_MODEL)

    def chip_cols(a, width):
        return lax.dynamic_slice_in_dim(a, chip * width, width, axis=a.ndim - 1)

    dml = jnp.transpose(chip_cols(dmod_all, 3072), (1, 0, 2))
    dml = jnp.concatenate([dml, jnp.zeros_like(dml)], axis=1)
    g_w_ada = _ada_bwd(c16, dml, "ada_bwd")

    def upd(w, m, v, ga, gb, name):
        sh = w.shape
        as3 = sh if len(sh) == 3 else (1,) + sh
        outs = _adamw(w.reshape(as3), m.reshape(as3), v.reshape(as3), ga.reshape(as3),
                      None if gb is None else gb.reshape(as3), name)
        return [a.reshape(sh) for a in outs]

    res = {}
    res["w_ada"] = upd(w_ada, m_w_ada, v_w_ada, g_w_ada, None, "adamw")
    res["b_ada"] = upd(b_ada, m_b_ada, v_b_ada, tot_of["dmod"], None, "adamw")
    res["norm_g"] = upd(norm_g, m_norm_g, v_norm_g, chip_cols(tot_of["norm_g"], 512), None, "adamw")
    res["w_a2"] = upd(w_a2, m_w_a2, v_w_a2, chip_cols(tot_of["w_a2"], 128), None, "adamw")
    res["b_a2"] = upd(b_a2, m_b_a2, v_b_a2, tot_of["b_a2"], None, "adamw")
    res["gla_norm_g"] = upd(gla_norm_g, m_gla_norm_g, v_gla_norm_g, tot_of["gng"], None, "adamw")
    res["conv_mix_w"] = upd(conv_mix_w, m_conv_mix_w, v_conv_mix_w, chip_cols(tot_of["conv_w"], 256), None, "adamw")
    res["ffn_conv_w"] = upd(ffn_conv_w, m_ffn_conv_w, v_ffn_conv_w, chip_cols(tot_of["ffn_w"], 1408), None, "adamw")

    full_name = dict(w_in="w_in", w_og="w_out_gla", w_oc="w_out_conv", w_o="w_o", w_up="w_up", w_dn="w_down")
    state = dict(w_in=(w_in, m_w_in, v_w_in), w_og=(w_out_gla, m_w_out_gla, v_w_out_gla),
                 w_oc=(w_out_conv, m_w_out_conv, v_w_out_conv), w_o=(w_o, m_w_o, v_w_o),
                 w_up=(w_up, m_w_up, v_w_up), w_dn=(w_down, m_w_down, v_w_down))
    after = res["w_ada"][3]
    for k in ("w_dn", "w_up", "w_o", "w_og", "w_oc", "w_in"):
        done = [_scatter_wait(scatters[k, l], after, "scatter_wait_%s_%d" % (k, l)) for l in range(DEPTH)]
        plane = _sum_chips([d[0] for d in done], [d[1] for d in done], chip_arr, "sum_chips")
        if k == "w_in":
            plane = jnp.transpose(plane, (2, 0, 1))
            other = _sibling_exchange([plane], "sibling_" + k)[0]
            outs = _adamw(*[jnp.transpose(a, (2, 0, 1)) for a in state[k]], plane, other, "adamw_w_in",
                          tile=(N_IN_CHIP // 4, D_MODEL // 8))
            res[full_name[k]] = [jnp.transpose(a, (1, 2, 0)) for a in outs]
        else:
            other = _sibling_exchange([plane], "sibling_" + k)[0]
            res[full_name[k]] = upd(*state[k], plane, other, "adamw")
        after = res[full_name[k]][3]
    order = ("w_ada", "b_ada", "norm_g", "w_in", "w_a2", "b_a2", "gla_norm_g", "w_out_gla", "conv_mix_w",
             "w_out_conv", "w_o", "w_up", "ffn_conv_w", "w_down")
    return (loss, grad_x, *[res[k][0] for k in order], *[res[k][1] for k in order],
            *[res[k][2] for k in order], *[res[k][3] for k in order])
```

```python
import functools
import math

import jax
import jax.numpy as jnp
from jax import lax
from jax.experimental import pallas as pl
from jax.experimental.pallas import tpu as pltpu

F32 = jnp.float32
BF16 = jnp.bfloat16
MESH = pl.DeviceIdType.MESH

D_MODEL = 2048
DEPTH = 2
CHUNK = 64
GLA_HEADS = 4
GLA_DK = 128
GLA_DV = 256
GLA_QK = GLA_HEADS * GLA_DK
GLA_V = GLA_HEADS * GLA_DV
GLA_LOWRANK = 16
GLA_TAU = 16.0
CONV_WIDTH = 1024
D_FF = 5632
EPS = 1e-6
N_IN = 10256
LR_PAD = 128
N_IN_PAD = N_IN - GLA_LOWRANK + LR_PAD
OFF_Q, OFF_K, OFF_V, OFF_R = 0, 512, 1024, 2048
OFF_CB, OFF_CC, OFF_CX, OFF_GA, OFF_GB, OFF_LR = 3072, 4096, 5120, 6144, 8192, 10240

ADAM_LR = 0.001
ADAM_B1 = 0.9
ADAM_B2 = 0.999
ADAM_EPS = 1e-08
ADAM_WD = 0.01
ADAM_STEP = 10

N_CHIPS = 4
N_DEV = 8
VMEM_LIMIT = 56 * 1024 * 1024
TM_ROW = 256
TM_EW = 512
CW_EW = 512
GLA_ROWS = 256


def _params(sem=None):
    return pltpu.CompilerParams(dimension_semantics=sem, vmem_limit_bytes=VMEM_LIMIT)


def _sigmoid(v):
    return 1.0 / (1.0 + jnp.exp(-v))


def _log_sigmoid(v):
    return jnp.minimum(v, 0.0) - jnp.log(1.0 + jnp.exp(-jnp.abs(v)))


_GELU_C = math.sqrt(2.0 / math.pi)


def _gelu_and_grad(v):
    v2 = v * v
    t = jnp.tanh(_GELU_C * v * (1.0 + 0.044715 * v2))
    half = 0.5 * (1.0 + t)
    return v * half, half + (0.5 * _GELU_C) * v * (1.0 - t * t) * (1.0 + (3.0 * 0.044715) * v2)


def _ld(ref):
    return ref[...].astype(F32)


def _flip(a, d):
    return a + d - 2 * a * d


def _unless(cond):
    return jnp.where(cond, 0.0, 1.0).astype(F32)


def _allgather8(xv, name):
    r, cdim = xv.shape

    def body(x_ref, out_ref, sum_ref, send_sems, recv_sems):
        xi, yi, ci = lax.axis_index("x"), lax.axis_index("y"), lax.axis_index("c")
        me = 4 * xi + 2 * yi + ci
        out_ref[pl.ds(me, 1)] = x_ref[...][None]
        started = []
        for k in range(1, N_DEV):
            px, py, pc = _flip(xi, (k >> 2) & 1), _flip(yi, (k >> 1) & 1), _flip(ci, k & 1)
            cp = pltpu.make_async_remote_copy(
                src_ref=x_ref, dst_ref=out_ref.at[me], send_sem=send_sems.at[k - 1], recv_sem=recv_sems.at[k - 1],
                device_id=(px, py, pc), device_id_type=MESH)
            cp.start()
            started.append((cp, 4 * px + 2 * py + pc, k, (px, py, pc)))
        for cp, peer, k, pid in started:
            cp.wait_send()
            pltpu.make_async_remote_copy(
                src_ref=x_ref, dst_ref=out_ref.at[peer], send_sem=send_sems.at[k - 1], recv_sem=recv_sems.at[k - 1],
                device_id=pid, device_id_type=MESH).wait_recv()
        acc = out_ref[0]
        for d in range(1, N_DEV):
            acc = acc + out_ref[d]
        sum_ref[...] = acc

    return pl.pallas_call(
        body, name=name,
        out_shape=(jax.ShapeDtypeStruct((N_DEV, r, cdim), F32), jax.ShapeDtypeStruct((r, cdim), F32)),
        in_specs=[pl.BlockSpec(memory_space=pltpu.VMEM)],
        out_specs=(pl.BlockSpec(memory_space=pltpu.VMEM), pl.BlockSpec(memory_space=pltpu.VMEM)),
        scratch_shapes=[pltpu.SemaphoreType.DMA((N_DEV - 1,)), pltpu.SemaphoreType.DMA((N_DEV - 1,))],
        compiler_params=pltpu.CompilerParams(vmem_limit_bytes=VMEM_LIMIT),
    )(xv)


_HBM = pl.BlockSpec(memory_space=pltpu.HBM)
_SEM = pl.BlockSpec(memory_space=pltpu.SEMAPHORE)
_EFFECT = pltpu.SideEffectType.DATAFLOW_SIDE_EFFECTING
_CHIP_FLIPS = ((1, 0), (0, 1), (1, 1))


def _chip_copies(src_ref, land_ref, send_sems, recv_sems, scatter):
    xi, yi, ci = lax.axis_index("x"), lax.axis_index("y"), lax.axis_index("c")
    me = 2 * xi + yi
    out = []
    for k, (dx, dy) in enumerate(_CHIP_FLIPS):
        px, py = _flip(xi, dx), _flip(yi, dy)
        peer = 2 * px + py
        src = src_ref.at[peer] if scatter else land_ref.at[me]
        mk = functools.partial(pltpu.make_async_remote_copy, src_ref=src, send_sem=send_sems.at[k],
                               recv_sem=recv_sems.at[k], device_id=(px, py, ci), device_id_type=MESH)
        out.append((mk(dst_ref=land_ref.at[me]), mk(dst_ref=land_ref.at[peer])))
    return out


def _gather_start(land, name):
    def body(land_ref, send_sems, recv_sems, land_thru, token):
        for mine, _ in _chip_copies(None, land_ref, send_sems, recv_sems, False):
            mine.start()
        token[...] = jnp.zeros_like(token)

    return pl.pallas_call(
        body, name=name,
        out_shape=(pltpu.SemaphoreType.DMA((3,)), pltpu.SemaphoreType.DMA((3,)), pltpu.HBM(land.shape, land.dtype),
                   jax.ShapeDtypeStruct((8, 128), F32)),
        in_specs=(_HBM,), out_specs=(_SEM, _SEM, _HBM, pl.BlockSpec(memory_space=pltpu.VMEM)),
        input_output_aliases={0: 2},
        compiler_params=pltpu.CompilerParams(has_side_effects=_EFFECT),
    )(pltpu.with_memory_space_constraint(land, pltpu.HBM))


def _gather_wait(handle, after, name):
    send, recv, land_thru = handle

    def body(land_ref, send_sems, recv_sems, after_ref, land_out):
        for mine, theirs in _chip_copies(None, land_ref, send_sems, recv_sems, False):
            mine.wait_send()
            theirs.wait_recv()

    return pl.pallas_call(
        body, name=name, out_shape=pltpu.HBM(land_thru.shape, land_thru.dtype),
        in_specs=(_HBM, _SEM, _SEM, pl.BlockSpec(memory_space=pl.ANY)), out_specs=_HBM,
        input_output_aliases={0: 0},
        compiler_params=pltpu.CompilerParams(has_side_effects=_EFFECT),
    )(land_thru, send, recv, after)


def _scatter_start(src, name, after=None):
    extra = [] if after is None else [after]

    def body(src_ref, land_ref, *rest):
        send_sems, recv_sems, src_thru, land_thru, token = rest[len(extra):]
        for mine, _ in _chip_copies(src_ref, land_ref, send_sems, recv_sems, True):
            mine.start()
        token[...] = jnp.zeros_like(token)

    return pl.pallas_call(
        body, name=name,
        out_shape=(pltpu.SemaphoreType.DMA((3,)), pltpu.SemaphoreType.DMA((3,)), pltpu.HBM(src.shape, src.dtype),
                   pltpu.HBM(src.shape, src.dtype), jax.ShapeDtypeStruct((8, 128), F32)),
        in_specs=(_HBM, _HBM) + (pl.BlockSpec(memory_space=pl.ANY),) * len(extra),
        out_specs=(_SEM, _SEM, _HBM, _HBM, pl.BlockSpec(memory_space=pltpu.VMEM)),
        input_output_aliases={0: 2, 1: 3},
        compiler_params=pltpu.CompilerParams(has_side_effects=_EFFECT),
    )(pltpu.with_memory_space_constraint(src, pltpu.HBM),
      pltpu.with_memory_space_constraint(lax.empty(src.shape, src.dtype), pltpu.HBM), *extra)


def _scatter_wait(handle, after, name):
    send, recv, src_thru, land_thru = handle

    def body(src_ref, land_ref, send_sems, recv_sems, after_ref, src_out, land_out):
        for mine, theirs in _chip_copies(src_ref, land_ref, send_sems, recv_sems, True):
            mine.wait_send()
            theirs.wait_recv()

    return pl.pallas_call(
        body, name=name,
        out_shape=(pltpu.HBM(src_thru.shape, src_thru.dtype), pltpu.HBM(land_thru.shape, land_thru.dtype)),
        in_specs=(_HBM, _HBM, _SEM, _SEM, pl.BlockSpec(memory_space=pl.ANY)), out_specs=(_HBM, _HBM),
        input_output_aliases={0: 0, 1: 1},
        compiler_params=pltpu.CompilerParams(has_side_effects=_EFFECT),
    )(src_thru, land_thru, send, recv, after)


def _sibling_exchange(arrays, name):
    n = len(arrays)

    def body(*refs):
        ins, outs = refs[:n], refs[n:2 * n]
        send_sems, recv_sems = refs[2 * n:]
        xi, yi, ci = lax.axis_index("x"), lax.axis_index("y"), lax.axis_index("c")
        cps = []
        for i in range(n):
            cp = pltpu.make_async_remote_copy(
                src_ref=ins[i], dst_ref=outs[i], send_sem=send_sems.at[i], recv_sem=recv_sems.at[i],
                device_id=(xi, yi, 1 - ci), device_id_type=MESH)
            cp.start()
            cps.append(cp)
        for cp in cps:
            cp.wait()

    return pl.pallas_call(
        body, name=name, out_shape=tuple(jax.ShapeDtypeStruct(a.shape, a.dtype) for a in arrays),
        in_specs=[pl.BlockSpec(memory_space=pl.ANY)] * n,
        out_specs=tuple(pl.BlockSpec(memory_space=pl.ANY) for _ in range(n)),
        scratch_shapes=[pltpu.SemaphoreType.DMA((n,)), pltpu.SemaphoreType.DMA((n,))],
    )(*arrays)


def _pick(dim, pref):
    if dim <= pref:
        return dim
    t = (pref // 128) * 128
    while t >= 128:
        if dim % t == 0:
            return t
        t -= 128
    return dim


def _matmul(a, b, dims, out_dtype, name, tm=512, tn=1024, tk=2048, out_chips=False, b_chips=False):
    a_parts = a if isinstance(a, tuple) else (a,)
    b_parts = b if isinstance(b, tuple) else (b,)
    na, nb = len(a_parts), len(b_parts)
    assert (na == 1 or dims == "nt") and (nb == 1 or dims == "tn")
    b_shape = (b_parts[0].shape[1], N_CHIPS * b_parts[0].shape[2]) if b_chips else b_parts[0].shape
    if dims == "nn":
        (m, kd), (_, n) = a_parts[0].shape, b_shape
    elif dims == "nt":
        (m, kd), (n, _) = a_parts[0].shape, b_shape
        kd = na * kd
    else:
        (kd, m), (_, n) = a_parts[0].shape, b_shape
        n = nb * n
    tm = _pick(m, tm)
    tn = _pick(n // N_CHIPS, tn) if (out_chips or (b_chips and dims == "nn")) else _pick(n // nb, tn)
    tk = _pick(kd // N_CHIPS, tk) if (b_chips and dims == "nt") else _pick(kd // na, tk)
    nk, nj = kd // tk, n // tn
    ka, jb = nk // na, nj // nb
    if out_chips:
        per_chip = n // N_CHIPS // tn
        out_shape = jax.ShapeDtypeStruct((N_CHIPS, m, n // N_CHIPS), out_dtype)
        out_spec = pl.BlockSpec((None, tm, tn), lambda j, i, k: (j // per_chip, i, j % per_chip))
    else:
        out_shape = jax.ShapeDtypeStruct((m, n), out_dtype)
        out_spec = pl.BlockSpec((tm, tn), lambda j, i, k: (i, j))
    def part_of(idx, first, count):
        return jnp.clip(idx - first, 0, count - 1)

    if dims == "nn":
        a_specs = [pl.BlockSpec((tm, tk), lambda j, i, k: (i, k))]
        b_specs = [pl.BlockSpec((tk, tn), lambda j, i, k: (k, j))]
        dn = (((1,), (0,)), ((), ()))
    elif dims == "nt":
        a_specs = [pl.BlockSpec((tm, tk), lambda j, i, k, p=p: (i, part_of(k, p * ka, ka))) for p in range(na)]
        b_specs = [pl.BlockSpec((tn, tk), lambda j, i, k: (j, k))]
        dn = (((1,), (1,)), ((), ()))
    else:
        a_specs = [pl.BlockSpec((tk, tm), lambda j, i, k: (k, i))]
        b_specs = [pl.BlockSpec((tk, tn), lambda j, i, k, p=p: (k, part_of(j, p * jb, jb))) for p in range(nb)]
        dn = (((0,), (0,)), ((), ()))
    if b_chips and dims == "nn":
        nper = n // N_CHIPS // tn
        b_specs = [pl.BlockSpec((None, tk, tn), lambda j, i, k: (j // nper, k, j % nper))]
    elif b_chips:
        kper = kd // N_CHIPS // tk
        b_specs = [pl.BlockSpec((None, tn, tk), lambda j, i, k: (k // kper, j, k % kper))]
    direct = nk == 1 or out_dtype == F32

    def body(*refs):
        a_refs, b_refs, o_ref = refs[:na], refs[na:na + nb], refs[na + nb]
        acc_ref = o_ref if direct else refs[na + nb + 1]
        j, k = pl.program_id(0), pl.program_id(2)

        def step(a_ref, b_ref):
            part = lax.dot_general(a_ref[...].astype(BF16), b_ref[...].astype(BF16), dn, preferred_element_type=F32)
            if nk == 1:
                o_ref[...] = part.astype(o_ref.dtype)
                return

            @pl.when(k == 0)
            def _():
                acc_ref[...] = part

            @pl.when(k > 0)
            def _():
                acc_ref[...] += part

            if not direct:
                @pl.when(k == nk - 1)
                def _():
                    o_ref[...] = acc_ref[...].astype(o_ref.dtype)

        if na == 1 and nb == 1:
            step(a_refs[0], b_refs[0])
        for p in range(na if na > 1 else 0):
            pl.when(jnp.logical_and(k >= p * ka, k < (p + 1) * ka))(functools.partial(step, a_refs[p], b_refs[0]))
        for p in range(nb if nb > 1 else 0):
            pl.when(jnp.logical_and(j >= p * jb, j < (p + 1) * jb))(functools.partial(step, a_refs[0], b_refs[p]))

    return pl.pallas_call(
        body, name=name, out_shape=out_shape,
        grid=(nj, m // tm, nk),
        in_specs=a_specs + b_specs,
        out_specs=out_spec,
        scratch_shapes=[] if direct else [pltpu.VMEM((tm, tn), F32)],
        compiler_params=_params(("parallel", "parallel", "arbitrary")),
    )(*a_parts, *b_parts)


def _rstd(v):
    return lax.rsqrt(jnp.mean(v * v, axis=-1, keepdims=True) + EPS)


def _row(tm):
    return pl.BlockSpec((tm, D_MODEL), lambda i: (i, 0))


_VEC = pl.BlockSpec((1, D_MODEL), lambda i: (0, 0))


def _pre_norm(x, gn, sc, sh, name):
    t = x.shape[0]
    tm = min(TM_ROW, t)

    def body(x_ref, gn_ref, sc_ref, sh_ref, h_ref):
        xv = x_ref[...]
        h_ref[...] = ((xv * _rstd(xv) * gn_ref[...]) * (1.0 + sc_ref[...]) + sh_ref[...]).astype(BF16)

    return pl.pallas_call(
        body, name=name, out_shape=jax.ShapeDtypeStruct((t, D_MODEL), BF16), grid=(t // tm,),
        in_specs=[_row(tm), _VEC, _VEC, _VEC], out_specs=_row(tm),
        compiler_params=_params(("parallel",)),
    )(x, gn, sc, sh)


def _post_pre(x, y, g, gnp, gn, sc, sh, name):
    t = x.shape[0]
    tm = min(TM_ROW, t)

    def body(x_ref, y_ref, g_ref, gnp_ref, gn_ref, sc_ref, sh_ref, x1_ref, h_ref):
        yv = y_ref[...]
        x1 = x_ref[...] + g_ref[...] * (yv * _rstd(yv) * gnp_ref[...])
        x1_ref[...] = x1
        h_ref[...] = ((x1 * _rstd(x1) * gn_ref[...]) * (1.0 + sc_ref[...]) + sh_ref[...]).astype(BF16)

    return pl.pallas_call(
        body, name=name,
        out_shape=(jax.ShapeDtypeStruct((t, D_MODEL), F32), jax.ShapeDtypeStruct((t, D_MODEL), BF16)),
        grid=(t // tm,),
        in_specs=[_row(tm), _row(tm), _VEC, _VEC, _VEC, _VEC, _VEC], out_specs=(_row(tm), _row(tm)),
        compiler_params=_params(("parallel",)),
    )(x, y, g, gnp, gn, sc, sh)


def _post_loss(x, y, g, gnp, tgt, name):
    t = x.shape[0]
    tm = min(TM_ROW, t)

    def body(x_ref, y_ref, g_ref, gnp_ref, t_ref, dx_ref, loss_ref):
        yv = y_ref[...]
        diff = x_ref[...] + g_ref[...] * (yv * _rstd(yv) * gnp_ref[...]) - t_ref[...]
        dx_ref[...] = diff * (1.0 / D_MODEL)
        part = (0.5 / D_MODEL) * jnp.sum(jnp.sum(diff * diff, axis=-1, keepdims=True), axis=0, keepdims=True)

        @pl.when(pl.program_id(0) == 0)
        def _():
            loss_ref[...] = jnp.zeros_like(loss_ref)

        loss_ref[...] += jnp.broadcast_to(part, loss_ref.shape)

    return pl.pallas_call(
        body, name=name,
        out_shape=(jax.ShapeDtypeStruct((t, D_MODEL), F32), jax.ShapeDtypeStruct((8, 128), F32)),
        grid=(t // tm,),
        in_specs=[_row(tm), _row(tm), _VEC, _VEC, _row(tm)],
        out_specs=(_row(tm), pl.BlockSpec((8, 128), lambda i: (0, 0))),
        compiler_params=_params(("arbitrary",)),
    )(x, y, g, gnp, tgt)


def _acc_rows(ref, val):
    @pl.when(pl.program_id(0) == 0)
    def _():
        ref[...] = jnp.zeros_like(ref)

    ref[...] += jnp.sum(val, axis=0, keepdims=True)


def _post_bwd(dxn, y, g, gnp, name):
    t = y.shape[0]
    tm = min(TM_ROW, t)

    def body(dx_ref, y_ref, g_ref, gnp_ref, dy_ref, dg_ref, dgn_ref):
        yv, dxv = y_ref[...], dx_ref[...]
        r = _rstd(yv)
        yh = yv * r
        _acc_rows(dg_ref, dxv * (yh * gnp_ref[...]))
        dn = dxv * g_ref[...]
        _acc_rows(dgn_ref, dn * yh)
        dyh = dn * gnp_ref[...]
        dy_ref[...] = (r * (dyh - yh * jnp.mean(dyh * yh, axis=-1, keepdims=True))).astype(BF16)

    return pl.pallas_call(
        body, name=name,
        out_shape=(jax.ShapeDtypeStruct((t, D_MODEL), BF16), jax.ShapeDtypeStruct((1, D_MODEL), F32),
                   jax.ShapeDtypeStruct((1, D_MODEL), F32)),
        grid=(t // tm,),
        in_specs=[_row(tm), _row(tm), _VEC, _VEC], out_specs=(_row(tm), _VEC, _VEC),
        compiler_params=_params(("arbitrary",)),
    )(dxn, y, g, gnp)


def _pre_bwd(dh, xin, dres, gn, sc, name):
    t = xin.shape[0]
    tm = min(TM_ROW, t)

    def body(dh_ref, x_ref, dres_ref, gn_ref, sc_ref, dx_ref, dsh_ref, dsc_ref, dgn_ref):
        xv, dhv = x_ref[...], dh_ref[...]
        r = _rstd(xv)
        xh = xv * r
        _acc_rows(dsh_ref, dhv)
        _acc_rows(dsc_ref, dhv * (xh * gn_ref[...]))
        dn = dhv * (1.0 + sc_ref[...])
        _acc_rows(dgn_ref, dn * xh)
        dxh = dn * gn_ref[...]
        dx_ref[...] = dres_ref[...] + r * (dxh - xh * jnp.mean(dxh * xh, axis=-1, keepdims=True))

    vec = jax.ShapeDtypeStruct((1, D_MODEL), F32)
    return pl.pallas_call(
        body, name=name, out_shape=(jax.ShapeDtypeStruct((t, D_MODEL), F32), vec, vec, vec),
        grid=(t // tm,),
        in_specs=[_row(tm), _row(tm), _row(tm), _VEC, _VEC], out_specs=(_row(tm), _VEC, _VEC, _VEC),
        compiler_params=_params(("arbitrary",)),
    )(dh, xin, dres, gn, sc)


def _fix_rows(v8, rows):
    idx = lax.broadcasted_iota(jnp.int32, v8.shape, 0)
    for j, val in rows:
        v8 = jnp.where(idx == j, jnp.broadcast_to(val, v8.shape), v8)
    return v8


def _shift_down(v, halo, s):
    hr, tm = halo.shape[0], v.shape[0]
    out = pltpu.roll(v, s, 0)
    if tm == 8:
        return _fix_rows(out, [(j, halo[hr - s + j:hr - s + j + 1, :]) for j in range(s)])
    head = _fix_rows(out[0:8, :], [(j, halo[hr - s + j:hr - s + j + 1, :]) for j in range(s)])
    return jnp.concatenate([head, out[8:, :]], axis=0)


def _shift_up(v, halo, s):
    tm = v.shape[0]
    out = pltpu.roll(v, tm - s, 0)
    tail = _fix_rows(out[tm - 8:, :], [(8 - s + j, halo[j:j + 1, :]) for j in range(s)])
    return jnp.concatenate([out[:tm - 8, :], tail], axis=0)


def _tile_specs(tm, cw, off, nrow, hr=8):
    ob = off // cw
    per = tm // hr
    main = pl.BlockSpec((tm, cw), lambda j, i: (i, ob + j))
    prev = pl.BlockSpec((hr, cw), lambda j, i: (jnp.maximum(i * per - 1, 0), ob + j))
    nxt = pl.BlockSpec((hr, cw), lambda j, i: (jnp.minimum((i + 1) * per, nrow * per - 1), ob + j))
    return main, prev, nxt


def _conv_fwd(p, w, name):
    t = p.shape[0]
    tm, cw = min(TM_EW, t), CW_EW
    nrow = t // tm
    cb_s, _, _ = _tile_specs(tm, cw, OFF_CB, nrow, 16)
    cc_s, cc_p, _ = _tile_specs(tm, cw, OFF_CC, nrow, 16)
    cx_s, cx_p, _ = _tile_specs(tm, cw, OFF_CX, nrow, 16)

    def body(cb_ref, cc_ref, ccp_ref, cx_ref, cxp_ref, w_ref, z_ref):
        u = _ld(cc_ref) * _ld(cx_ref)
        uh = _ld(ccp_ref) * _ld(cxp_ref) * _unless(pl.program_id(1) == 0)
        wv = w_ref[...]
        conv = wv[2:3, :] * u + wv[1:2, :] * _shift_down(u, uh, 1) + wv[0:1, :] * _shift_down(u, uh, 2)
        z_ref[...] = (_ld(cb_ref) * conv).astype(BF16)

    return pl.pallas_call(
        body, name=name, out_shape=jax.ShapeDtypeStruct((t, CONV_WIDTH), BF16),
        grid=(CONV_WIDTH // cw, nrow),
        in_specs=[cb_s, cc_s, cc_p, cx_s, cx_p, pl.BlockSpec((8, cw), lambda j, i: (0, j))],
        out_specs=pl.BlockSpec((tm, cw), lambda j, i: (i, j)),
        compiler_params=_params(("parallel", "arbitrary")),
    )(p, p, p, p, p, w)


def _acc_w(ref, vals):
    @pl.when(pl.program_id(1) == 0)
    def _():
        ref[...] = jnp.zeros_like(ref)

    for j, v in enumerate(vals):
        ref[j:j + 1, :] += jnp.sum(v, axis=0, keepdims=True)


def _conv_bwd(dz, p, w, dp, name):
    t = p.shape[0]
    tm, cw = min(TM_EW // 2, t), CONV_WIDTH
    nrow = t // tm
    dz_s, _, dz_n = _tile_specs(tm, cw, 0, nrow)
    cb_s, _, cb_n = _tile_specs(tm, cw, OFF_CB, nrow, 16)
    cc_s, cc_p, _ = _tile_specs(tm, cw, OFF_CC, nrow, 16)
    cx_s, cx_p, _ = _tile_specs(tm, cw, OFF_CX, nrow, 16)

    def body(dz_ref, dzn_ref, cb_ref, cbn_ref, cc_ref, ccp_ref, cx_ref, cxp_ref, w_ref, dp_in, dp_ref, dw_ref):
        dcb_ref = dp_ref.at[:, 0:cw]
        dcc_ref = dp_ref.at[:, cw:2 * cw]
        dcx_ref = dp_ref.at[:, 2 * cw:3 * cw]
        i = pl.program_id(1)
        ccv, cxv, dzv = _ld(cc_ref), _ld(cx_ref), dz_ref[...]
        u = ccv * cxv
        uh = _ld(ccp_ref) * _ld(cxp_ref) * _unless(i == 0)
        wv = w_ref[...]
        u1, u2 = _shift_down(u, uh, 1), _shift_down(u, uh, 2)
        conv = wv[2:3, :] * u + wv[1:2, :] * u1 + wv[0:1, :] * u2
        dcb_ref[...] = (dzv * conv).astype(BF16)
        dconv = dzv * _ld(cb_ref)
        dch = dzn_ref[...] * _ld(cbn_ref)[0:8, :] * _unless(i == nrow - 1)
        du = wv[2:3, :] * dconv + wv[1:2, :] * _shift_up(dconv, dch, 1) + wv[0:1, :] * _shift_up(dconv, dch, 2)
        dcc_ref[...] = (du * cxv).astype(BF16)
        dcx_ref[...] = (du * ccv).astype(BF16)
        _acc_w(dw_ref, (dconv * u2, dconv * u1, dconv * u))

    w_s = pl.BlockSpec((8, cw), lambda j, i: (0, j))
    return pl.pallas_call(
        body, name=name, out_shape=(_dp_shape(t), jax.ShapeDtypeStruct((8, CONV_WIDTH), F32)),
        grid=(1, nrow),
        in_specs=[dz_s, dz_n, cb_s, cb_n, cc_s, cc_p, cx_s, cx_p, w_s, _ANY],
        out_specs=(pl.BlockSpec((tm, 3 * cw), lambda j, i: (i, OFF_CB // (3 * cw))), w_s),
        input_output_aliases={9: 0},
        compiler_params=_params(("parallel", "arbitrary")),
    )(dz, dz, p, p, p, p, p, p, w, dp)


def _ffn_fwd(u, w, name):
    t = u.shape[0]
    tm, cw = min(TM_EW, t), CW_EW
    nrow = t // tm
    g_s, g_p, _ = _tile_specs(tm, cw, 0, nrow, 16)
    u_s, _, _ = _tile_specs(tm, cw, D_FF, nrow, 16)

    def body(g_ref, gp_ref, u_ref, w_ref, f_ref):
        gv = _ld(g_ref)
        gh = _ld(gp_ref) * _unless(pl.program_id(1) == 0)
        wv = w_ref[...]
        gc = wv[2:3, :] * gv + wv[1:2, :] * _shift_down(gv, gh, 1) + wv[0:1, :] * _shift_down(gv, gh, 2)
        f_ref[...] = (_gelu_and_grad(gc)[0] * _ld(u_ref)).astype(BF16)

    return pl.pallas_call(
        body, name=name, out_shape=jax.ShapeDtypeStruct((t, D_FF), BF16),
        grid=(D_FF // cw, nrow),
        in_specs=[g_s, g_p, u_s, pl.BlockSpec((8, cw), lambda j, i: (0, j))],
        out_specs=pl.BlockSpec((tm, cw), lambda j, i: (i, j)),
        compiler_params=_params(("parallel", "arbitrary")),
    )(u, u, u, w)


def _ffn_bwd(df, u, w, name):
    t = u.shape[0]
    tm, cw = min(TM_EW, t), CW_EW
    nrow = t // tm
    df_s, _, df_n = _tile_specs(tm, cw, 0, nrow)
    g_s, g_p, g_n = _tile_specs(tm, cw, 0, nrow, 16)
    u_s, _, u_n = _tile_specs(tm, cw, D_FF, nrow, 16)

    def body(df_ref, dfn_ref, g_ref, gp_ref, gn_ref, u_ref, un_ref, w_ref, dg_ref, du_ref, dw_ref):
        i = pl.program_id(1)
        gv, dfv, uv = _ld(g_ref), df_ref[...], _ld(u_ref)
        gh = _ld(gp_ref) * _unless(i == 0)
        wv = w_ref[...]
        g1, g2 = _shift_down(gv, gh, 1), _shift_down(gv, gh, 2)
        gc = wv[2:3, :] * gv + wv[1:2, :] * g1 + wv[0:1, :] * g2
        act, act_grad = _gelu_and_grad(gc)
        du_ref[...] = (dfv * act).astype(BF16)
        dgc = dfv * uv * act_grad
        gnv = _ld(gn_ref)[0:8, :]
        gtail = gv[tm - 8:tm, :]
        gcn = (wv[2:3, :] * gnv + wv[1:2, :] * _shift_down(gnv, gtail, 1) + wv[0:1, :] * _shift_down(gnv, gtail, 2))
        dgcn = dfn_ref[...] * _ld(un_ref)[0:8, :] * _gelu_and_grad(gcn)[1] * _unless(i == nrow - 1)
        dg = wv[2:3, :] * dgc + wv[1:2, :] * _shift_up(dgc, dgcn, 1) + wv[0:1, :] * _shift_up(dgc, dgcn, 2)
        dg_ref[...] = dg.astype(BF16)
        _acc_w(dw_ref, (dgc * g2, dgc * g1, dgc * gv))

    o_s = pl.BlockSpec((tm, cw), lambda j, i: (i, j))
    o_sh = jax.ShapeDtypeStruct((t, D_FF), BF16)
    w_s = pl.BlockSpec((8, cw), lambda j, i: (0, j))
    return pl.pallas_call(
        body, name=name, out_shape=(o_sh, o_sh, jax.ShapeDtypeStruct((8, D_FF), F32)),
        grid=(D_FF // cw, nrow),
        in_specs=[df_s, df_n, g_s, g_p, g_n, u_s, u_n, w_s],
        out_specs=(o_s, o_s, w_s),
        compiler_params=_params(("parallel", "arbitrary")),
    )(df, df, u, u, u, u, u, w)


def _merge_fwd(ya, yb, p, name):
    t = ya.shape[0]
    tm, cw = min(TM_EW, t), CW_EW
    y_s = pl.BlockSpec((tm, cw), lambda i, j: (i, j))

    def body(ya_ref, yb_ref, ga_ref, gb_ref, m_ref):
        m_ref[...] = (_sigmoid(_ld(ga_ref)) * ya_ref[...] + _sigmoid(_ld(gb_ref)) * yb_ref[...]).astype(BF16)

    return pl.pallas_call(
        body, name=name, out_shape=jax.ShapeDtypeStruct((t, D_MODEL), BF16),
        grid=(t // tm, D_MODEL // cw),
        in_specs=[y_s, y_s, pl.BlockSpec((tm, cw), lambda i, j: (i, OFF_GA // cw + j)),
                  pl.BlockSpec((tm, cw), lambda i, j: (i, OFF_GB // cw + j))],
        out_specs=y_s, compiler_params=_params(("parallel", "parallel")),
    )(ya, yb, p, p)


_ANY = pl.BlockSpec(memory_space=pl.ANY)


def _dp_shape(t):
    return jax.ShapeDtypeStruct((t, N_IN_PAD), BF16)


def _merge_bwd(dm, y, p, gate_off, dp, name):
    t = y.shape[0]
    tm, cw = min(TM_EW, t), CW_EW
    y_s = pl.BlockSpec((tm, cw), lambda i, j: (i, j))
    g_s = pl.BlockSpec((tm, cw), lambda i, j: (i, gate_off // cw + j))

    def body(dm_ref, y_ref, g_ref, *rest):
        dy_ref, dp_ref = rest[-2:]
        dmv = dm_ref[...]
        sg = _sigmoid(_ld(g_ref))
        dy_ref[...] = (dmv * sg).astype(BF16)
        dp_ref[...] = (dmv * y_ref[...] * sg * (1.0 - sg)).astype(BF16)

    extra = [] if dp is None else [dp]
    return pl.pallas_call(
        body, name=name, out_shape=(jax.ShapeDtypeStruct((t, D_MODEL), BF16), _dp_shape(t)),
        grid=(t // tm, D_MODEL // cw),
        in_specs=[y_s, y_s, g_s] + [_ANY] * len(extra),
        out_specs=(y_s, g_s), input_output_aliases={} if dp is None else {3: 1},
        compiler_params=_params(("parallel", "parallel")),
    )(dm, y, p, *extra)


def _tri(lower):
    r = lax.broadcasted_iota(jnp.int32, (CHUNK, CHUNK), 0)
    c = lax.broadcasted_iota(jnp.int32, (CHUNK, CHUNK), 1)
    return ((c <= r) if lower else (c >= r)).astype(F32)


def _eye_mask():
    r = lax.broadcasted_iota(jnp.int32, (GLA_DK, GLA_DK), 0)
    c = lax.broadcasted_iota(jnp.int32, (GLA_DK, GLA_DK), 1)
    return r == c


def _row_to_col(v):
    return jnp.sum(jnp.where(_eye_mask(), jnp.broadcast_to(v, (GLA_DK, GLA_DK)), 0.0), axis=1, keepdims=True)


def _col_to_row(v):
    return jnp.sum(jnp.where(_eye_mask(), jnp.broadcast_to(v, (GLA_DK, GLA_DK)), 0.0), axis=0, keepdims=True)


def _dot(a, b, dn):
    return lax.dot_general(a.astype(BF16), b.astype(BF16), (dn, ((), ())), preferred_element_type=F32)


_NN = ((1,), (0,))
_NT = ((1,), (1,))
_TN = ((0,), (0,))


def _gate_logits(lr_ref, wa_ref, ba_ref):
    return _dot(lr_ref[...], wa_ref[...], _NN) + ba_ref[...]


def _chunk_decay(la, tri):
    cum = lax.dot_general(tri, la, ((_NN), ((), ())), precision=lax.Precision.HIGHEST, preferred_element_type=F32)
    e = cum[CHUNK - 1:CHUNK, :]
    return cum, e, jnp.exp(e - cum)


def _gla_fwd(p, wa, ba, name):
    t = p.shape[0]
    rows = min(GLA_ROWS, t)
    cb = rows // CHUNK
    nc = t // CHUNK
    scale = GLA_DK ** -0.5

    def body(q_ref, k_ref, v_ref, lr_ref, wa_ref, ba_ref, o_ref, st_ref, s_scr):
        @pl.when(pl.program_id(0) == 0)
        def _():
            s_scr[...] = jnp.zeros_like(s_scr)

        la_all = _log_sigmoid(_gate_logits(lr_ref, wa_ref, ba_ref)) * (1.0 / GLA_TAU)
        tri = _tri(True)
        for ch in range(cb):
            rs = slice(ch * CHUNK, (ch + 1) * CHUNK)
            for h in range(GLA_HEADS):
                ks = slice(h * GLA_DK, (h + 1) * GLA_DK)
                vs = slice(h * GLA_DV, (h + 1) * GLA_DV)
                _, e, w = _chunk_decay(la_all[rs, ks], tri)
                kd = k_ref[rs, ks].astype(F32) * w
                s_new = _row_to_col(jnp.exp(e)) * s_scr[ks, :] + _dot(kd, v_ref[rs, vs], _TN)
                s_scr[ks, :] = s_new
                st_ref[ch, ks, :] = s_new
                o_ref[rs, vs] = _dot(q_ref[rs, ks].astype(F32) * scale, s_new, _NN)

    return pl.pallas_call(
        body, name=name,
        out_shape=(jax.ShapeDtypeStruct((t, GLA_V), F32), jax.ShapeDtypeStruct((nc, GLA_QK, GLA_DV), F32)),
        grid=(t // rows,),
        in_specs=[pl.BlockSpec((rows, GLA_QK), lambda i: (i, OFF_Q // GLA_QK)),
                  pl.BlockSpec((rows, GLA_QK), lambda i: (i, OFF_K // GLA_QK)),
                  pl.BlockSpec((rows, GLA_V), lambda i: (i, OFF_V // GLA_V)),
                  pl.BlockSpec((rows, LR_PAD), lambda i: (i, OFF_LR // LR_PAD)),
                  pl.BlockSpec((LR_PAD, GLA_QK), lambda i: (0, 0)),
                  pl.BlockSpec((1, GLA_QK), lambda i: (0, 0))],
        out_specs=(pl.BlockSpec((rows, GLA_V), lambda i: (i, 0)),
                   pl.BlockSpec((cb, GLA_QK, GLA_DV), lambda i: (i, 0, 0))),
        scratch_shapes=[pltpu.VMEM((GLA_QK, GLA_DV), F32)],
        compiler_params=_params(("arbitrary",)),
    )(p, p, p, p, wa, ba)


def _gla_bwd(do, p, st, wa, ba, dp, name):
    t = p.shape[0]
    rows = min(GLA_ROWS, t)
    cb = rows // CHUNK
    nb = t // rows
    scale = GLA_DK ** -0.5

    def rev(i):
        return nb - 1 - i

    def body(do_ref, q_ref, k_ref, v_ref, lr_ref, st_ref, stp_ref, wa_ref, ba_ref, dp_in,
             dp_ref, dlr_ref, dwa_ref, dba_ref, ds_scr, dz_scr):
        dq_ref = dp_ref.at[:, OFF_Q:OFF_Q + GLA_QK]
        dk_ref = dp_ref.at[:, OFF_K:OFF_K + GLA_QK]
        dv_ref = dp_ref.at[:, OFF_V:OFF_V + GLA_V]
        i = pl.program_id(0)

        @pl.when(i == 0)
        def _():
            ds_scr[...] = jnp.zeros_like(ds_scr)
            dwa_ref[...] = jnp.zeros_like(dwa_ref)
            dba_ref[...] = jnp.zeros_like(dba_ref)

        z_all = _gate_logits(lr_ref, wa_ref, ba_ref)
        la_all = _log_sigmoid(z_all) * (1.0 / GLA_TAU)
        tri, triu = _tri(True), _tri(False)
        last_row = lax.broadcasted_iota(jnp.int32, (CHUNK, GLA_DK), 0) == CHUNK - 1
        keep_prev = _unless(i == nb - 1)
        for ch in reversed(range(cb)):
            rs = slice(ch * CHUNK, (ch + 1) * CHUNK)
            for h in range(GLA_HEADS):
                ks = slice(h * GLA_DK, (h + 1) * GLA_DK)
                vs = slice(h * GLA_DV, (h + 1) * GLA_DV)
                _, e, w = _chunk_decay(la_all[rs, ks], tri)
                kd = k_ref[rs, ks].astype(F32) * w
                exp_e = jnp.exp(e)
                s_c = st_ref[ch, ks, :]
                if ch > 0:
                    s_p = st_ref[ch - 1, ks, :]
                else:
                    s_p = stp_ref[0, ks, :] * keep_prev
                do_c = do_ref[rs, vs]
                vv = v_ref[rs, vs]
                ds_tot = ds_scr[ks, :] + _dot(q_ref[rs, ks].astype(F32) * scale, do_c, _TN)
                dq_ref[rs, ks] = (_dot(do_c, s_c, _NT) * scale).astype(BF16)
                dkd = _dot(vv, ds_tot, _NT)
                dv_ref[rs, vs] = _dot(kd, ds_tot, _NN).astype(BF16)
                dexp_col = jnp.sum(ds_tot * s_p, axis=1, keepdims=True)
                ds_scr[ks, :] = _row_to_col(exp_e) * ds_tot
                dk_ref[rs, ks] = (dkd * w).astype(BF16)
                dwt = dkd * kd
                de = jnp.sum(dwt, axis=0, keepdims=True) + _col_to_row(dexp_col) * exp_e
                dcum = jnp.where(last_row, de - dwt, -dwt)
                da = lax.dot_general(triu, dcum, (_NN, ((), ())), precision=lax.Precision.HIGHEST,
                                     preferred_element_type=F32)
                dz_scr[rs, ks] = da * (1.0 / GLA_TAU) * _sigmoid(-z_all[rs, ks])
        dz = dz_scr[...]
        dlr_ref[...] = _dot(dz, wa_ref[...], _NT).astype(BF16)
        dwa_ref[...] += _dot(lr_ref[...], dz, _TN)
        dba_ref[...] += jnp.sum(dz, axis=0, keepdims=True)

    qkv = OFF_V + GLA_V
    return pl.pallas_call(
        body, name=name,
        out_shape=(_dp_shape(t), jax.ShapeDtypeStruct((t, LR_PAD), BF16),
                   jax.ShapeDtypeStruct((LR_PAD, GLA_QK), F32), jax.ShapeDtypeStruct((1, GLA_QK), F32)),
        grid=(nb,),
        in_specs=[pl.BlockSpec((rows, GLA_V), lambda i: (rev(i), 0)),
                  pl.BlockSpec((rows, GLA_QK), lambda i: (rev(i), OFF_Q // GLA_QK)),
                  pl.BlockSpec((rows, GLA_QK), lambda i: (rev(i), OFF_K // GLA_QK)),
                  pl.BlockSpec((rows, GLA_V), lambda i: (rev(i), OFF_V // GLA_V)),
                  pl.BlockSpec((rows, LR_PAD), lambda i: (rev(i), OFF_LR // LR_PAD)),
                  pl.BlockSpec((cb, GLA_QK, GLA_DV), lambda i: (rev(i), 0, 0)),
                  pl.BlockSpec((1, GLA_QK, GLA_DV), lambda i: (jnp.maximum(rev(i) * cb - 1, 0), 0, 0)),
                  pl.BlockSpec((LR_PAD, GLA_QK), lambda i: (0, 0)),
                  pl.BlockSpec((1, GLA_QK), lambda i: (0, 0)), _ANY],
        out_specs=(pl.BlockSpec((rows, qkv), lambda i: (rev(i), 0)),
                   pl.BlockSpec((rows, LR_PAD), lambda i: (rev(i), 0)),
                   pl.BlockSpec((LR_PAD, GLA_QK), lambda i: (0, 0)),
                   pl.BlockSpec((1, GLA_QK), lambda i: (0, 0))),
        input_output_aliases={9: 0},
        scratch_shapes=[pltpu.VMEM((GLA_QK, GLA_DV), F32), pltpu.VMEM((rows, GLA_QK), F32)],
        compiler_params=_params(("arbitrary",)),
    )(do, p, p, p, p, st, st, wa, ba, dp)


def _gla_out_fwd(o, p, gng, name):
    t = o.shape[0]
    tm = min(TM_EW, t)

    def body(o_ref, r_ref, g_ref, z_ref):
        gv = g_ref[...]
        for h in range(GLA_HEADS):
            vs = slice(h * GLA_DV, (h + 1) * GLA_DV)
            ov, rv = o_ref[:, vs], r_ref[:, vs].astype(F32)
            z_ref[:, vs] = ((ov * _rstd(ov) * gv) * (rv * _sigmoid(rv))).astype(BF16)

    return pl.pallas_call(
        body, name=name, out_shape=jax.ShapeDtypeStruct((t, GLA_V), BF16), grid=(t // tm,),
        in_specs=[pl.BlockSpec((tm, GLA_V), lambda i: (i, 0)),
                  pl.BlockSpec((tm, GLA_V), lambda i: (i, OFF_R // GLA_V)),
                  pl.BlockSpec((1, GLA_DV), lambda i: (0, 0))],
        out_specs=pl.BlockSpec((tm, GLA_V), lambda i: (i, 0)),
        compiler_params=_params(("parallel",)),
    )(o, p, gng)


def _gla_out_bwd(dz, o, p, gng, dp, name):
    t = o.shape[0]
    tm = min(TM_EW, t)

    def body(dz_ref, o_ref, r_ref, g_ref, dp_in, do_ref, dr_ref, dg_ref):
        @pl.when(pl.program_id(0) == 0)
        def _():
            dg_ref[...] = jnp.zeros_like(dg_ref)

        gv = g_ref[...]
        for h in range(GLA_HEADS):
            vs = slice(h * GLA_DV, (h + 1) * GLA_DV)
            ov, rv, dzv = o_ref[:, vs], r_ref[:, vs].astype(F32), dz_ref[:, vs]
            rs = _rstd(ov)
            oh = ov * rs
            sg = _sigmoid(rv)
            dr_ref[:, vs] = (dzv * (oh * gv) * (sg * (1.0 + rv * (1.0 - sg)))).astype(BF16)
            don = dzv * (rv * sg)
            dg_ref[...] += jnp.sum(don * oh, axis=0, keepdims=True)
            doh = don * gv
            do_ref[:, vs] = rs * (doh - oh * jnp.mean(doh * oh, axis=-1, keepdims=True))

    row = pl.BlockSpec((tm, GLA_V), lambda i: (i, 0))
    r_s = pl.BlockSpec((tm, GLA_V), lambda i: (i, OFF_R // GLA_V))
    return pl.pallas_call(
        body, name=name,
        out_shape=(jax.ShapeDtypeStruct((t, GLA_V), F32), _dp_shape(t), jax.ShapeDtypeStruct((1, GLA_DV), F32)),
        grid=(t // tm,),
        in_specs=[row, row, r_s, pl.BlockSpec((1, GLA_DV), lambda i: (0, 0)), _ANY],
        out_specs=(row, r_s, pl.BlockSpec((1, GLA_DV), lambda i: (0, 0))),
        input_output_aliases={4: 1},
        compiler_params=_params(("arbitrary",)),
    )(dz, o, p, gng, dp)


def _ada_fwd(c_all, w, b, layer, name):
    n = w.shape[2]
    tn = _pick(n, 512)

    def body(c_ref, w_ref, b_ref, o_ref):
        cv = c_ref[...]
        o_ref[...] = _dot(cv * _sigmoid(cv), w_ref[...], _NN) + b_ref[...]

    return pl.pallas_call(
        body, name=name, out_shape=jax.ShapeDtypeStruct((16, n), F32), grid=(n // tn,),
        in_specs=[pl.BlockSpec((16, D_MODEL), lambda j: (0, 0)),
                  pl.BlockSpec((None, D_MODEL, tn), lambda j: (layer, 0, j)),
                  pl.BlockSpec((1, tn), lambda j: (0, j))],
        out_specs=pl.BlockSpec((16, tn), lambda j: (0, j)),
        compiler_params=_params(("parallel",)),
    )(c_all, w, b)


def _ada_bwd(c_all, dmod, name):
    n = dmod.shape[2]
    tn = _pick(n, 512)

    def body(c_ref, d_ref, o_ref):
        cv = c_ref[...]
        o_ref[...] = _dot(cv * _sigmoid(cv), d_ref[...], _TN)

    return pl.pallas_call(
        body, name=name, out_shape=jax.ShapeDtypeStruct((DEPTH, D_MODEL, n), F32), grid=(DEPTH, n // tn),
        in_specs=[pl.BlockSpec((16, D_MODEL), lambda l, j: (0, 0)),
                  pl.BlockSpec((None, 16, tn), lambda l, j: (l, 0, j))],
        out_specs=pl.BlockSpec((None, D_MODEL, tn), lambda l, j: (l, 0, j)),
        compiler_params=_params(("parallel", "parallel")),
    )(c_all, dmod)


def _rows_tile(nrows, ncols, target_bytes):
    want = max(16, target_bytes // (4 * ncols))
    if nrows <= want:
        return nrows
    t = (want // 16) * 16
    while t >= 16:
        if nrows % t == 0:
            return t
        t -= 16
    return nrows


def _sum_chips(sent, landed, chip, name):
    _, nrows, ncols = sent[0].shape
    tr = _rows_tile(nrows, ncols, 2 << 20)
    nblk = nrows // tr

    def body(chip_ref, *refs):
        own, got, o_ref = refs[:DEPTH], refs[DEPTH:2 * DEPTH], refs[2 * DEPTH]
        me = chip_ref[0]
        for l in range(DEPTH):
            for j in range(N_CHIPS):
                def add(val):
                    if j == 0:
                        o_ref[...] = val.astype(F32)
                    else:
                        o_ref[...] += val.astype(F32)

                @pl.when(jnp.logical_and(pl.program_id(0) == l, me == j))
                def _():
                    add(own[l][...])

                @pl.when(jnp.logical_and(pl.program_id(0) == l, me != j))
                def _():
                    add(got[l][j])

    def rows_of(layer):
        return lambda l, i, chip_ref: jnp.where(l == layer, i, 0)

    own_specs = [pl.BlockSpec((None, tr, ncols), lambda l, i, chip_ref, r=rows_of(k): (chip_ref[0], r(l, i, chip_ref), 0))
                 for k in range(DEPTH)]
    got_specs = [pl.BlockSpec((N_CHIPS, tr, ncols), lambda l, i, chip_ref, r=rows_of(k): (0, r(l, i, chip_ref), 0))
                 for k in range(DEPTH)]
    return pl.pallas_call(
        body, name=name, out_shape=jax.ShapeDtypeStruct((DEPTH, nrows, ncols), F32),
        grid_spec=pltpu.PrefetchScalarGridSpec(
            num_scalar_prefetch=1, grid=(DEPTH, nblk), in_specs=own_specs + got_specs,
            out_specs=pl.BlockSpec((None, tr, ncols), lambda l, i, chip_ref: (l, i, 0))),
        compiler_params=_params(("arbitrary", "arbitrary")),
    )(chip, *sent, *landed)


def _adamw(w, m, v, ga, gb, name, tile=None):
    two = gb is not None
    c1 = 1.0 - ADAM_B1 ** ADAM_STEP
    c2 = 1.0 - ADAM_B2 ** ADAM_STEP

    def body(*refs):
        if two:
            w_ref, m_ref, v_ref, ga_ref, gb_ref, g_ref, d_ref, nm_ref, nv_ref = refs
            g = ga_ref[...] + gb_ref[...]
        else:
            w_ref, m_ref, v_ref, ga_ref, g_ref, d_ref, nm_ref, nv_ref = refs
            g = ga_ref[...]
        g_ref[...] = g
        nm = ADAM_B1 * m_ref[...] + (1.0 - ADAM_B1) * g
        nv = ADAM_B2 * v_ref[...] + (1.0 - ADAM_B2) * (g * g)
        nm_ref[...] = nm
        nv_ref[...] = nv
        d_ref[...] = -ADAM_LR * ((nm / c1) / (jnp.sqrt(nv / c2) + ADAM_EPS) + ADAM_WD * w_ref[...])

    if tile is None:
        nl, nrows, ncols = w.shape
        tr = _rows_tile(nrows, ncols, 1 << 20)
        blk = pl.BlockSpec((None, tr, ncols), lambda l, i: (l, i, 0))
        grid = (nl, nrows // tr)
    else:
        nrows, nl, ncols = w.shape
        rb, cb = tile
        blk = pl.BlockSpec((rb, nl, cb), lambda i, j: (i, 0, j))
        grid = (nrows // rb, ncols // cb)
    sh = jax.ShapeDtypeStruct(w.shape, F32)
    ins = [w, m, v, ga] + ([gb] if two else [])
    return pl.pallas_call(
        body, name=name, out_shape=(sh, sh, sh, sh), grid=grid,
        in_specs=[blk] * len(ins), out_specs=(blk, blk, blk, blk),
        compiler_params=_params(("parallel", "parallel")),
    )(*ins)


def _pad_rows(a, rows):
    return jnp.concatenate([a, jnp.zeros((rows - a.shape[0],) + a.shape[1:], a.dtype)], axis=0)


N_IN_CHIP = N_IN // N_CHIPS
_LR_LO = 3072 - N_IN_CHIP
_LR_HI = _LR_LO + GLA_LOWRANK


def _w_in_from_chips(a):
    return jnp.concatenate([a[0], a[1][:, :_LR_LO], a[1][:, _LR_HI:], a[2], a[3], a[1][:, _LR_LO:_LR_HI],
                            jnp.zeros((a.shape[1], LR_PAD - GLA_LOWRANK), a.dtype)], axis=1)


def _w_in_to_chips(w):
    s2 = 2 * N_IN_CHIP - GLA_LOWRANK
    s3 = s2 + N_IN_CHIP
    c1 = jnp.concatenate([w[:, N_IN_CHIP:3072], w[:, OFF_LR:OFF_LR + GLA_LOWRANK], w[:, 3072:s2]], axis=1)
    return jnp.stack([w[:, :N_IN_CHIP], c1, w[:, s2:s3], w[:, s3:OFF_LR]])


_BIG = ("w_in", "w_og", "w_oc", "w_o", "w_up", "w_dn")
_ROW_SHARDED = ("w_o", "w_dn")


def kernel(x, c, w_ada, b_ada, norm_g, w_in, w_a2, b_a2, gla_norm_g, w_out_gla, conv_mix_w, w_out_conv, w_o, w_up, ffn_conv_w, w_down, loss_target, m_w_ada, m_b_ada, m_norm_g, m_w_in, m_w_a2, m_b_a2, m_gla_norm_g, m_w_out_gla, m_conv_mix_w, m_w_out_conv, m_w_o, m_w_up, m_ffn_conv_w, m_w_down, v_w_ada, v_b_ada, v_norm_g, v_w_in, v_w_a2, v_b_a2, v_gla_norm_g, v_w_out_gla, v_conv_mix_w, v_w_out_conv, v_w_o, v_w_up, v_ffn_conv_w, v_w_down):
    xi, yi, ci = lax.axis_index("x"), lax.axis_index("y"), lax.axis_index("c")
    chip = 2 * xi + yi
    dev = 2 * chip + ci
    chip_arr = jnp.reshape(chip, (1,)).astype(jnp.int32)
    xt = x[0]
    tgt = loss_target[0]

    c_all = _allgather8(jnp.broadcast_to(c, (8, D_MODEL)), "gather_c")[0][:, 0, :]
    c16 = _pad_rows(c_all, 16)
    sm_parts = [norm_g.reshape(-1), w_a2.reshape(-1), conv_mix_w.reshape(-1), ffn_conv_w.reshape(-1)]
    sm_sizes = [a.shape[0] for a in sm_parts]
    sm_flat = jnp.concatenate(sm_parts)
    sm_rows = -(-sm_flat.shape[0] // 128)
    sm_rows = -(-sm_rows // 8) * 8
    sm_flat = jnp.concatenate([sm_flat, jnp.zeros((sm_rows * 128 - sm_flat.shape[0],), F32)]).reshape(sm_rows, 128)
    sm_all = _allgather8(sm_flat, "gather_small")[0].reshape(N_DEV, -1)[0::2]
    offs = [0]
    for s in sm_sizes:
        offs.append(offs[-1] + s)

    def small_full(idx, shape):
        a = sm_all[:, offs[idx]:offs[idx + 1]].reshape((N_CHIPS,) + shape)
        a = jnp.moveaxis(a, 0, -2)
        return a.reshape(shape[:-1] + (N_CHIPS * shape[-1],))

    norm_g_f = small_full(0, (DEPTH, 4, 512))
    w_a2_f = small_full(1, (DEPTH, GLA_LOWRANK, 128))
    conv_w_f = small_full(2, (DEPTH, 3, 256))
    ffn_w_f = small_full(3, (DEPTH, 3, 1408))

    b_loc = lax.dynamic_slice(b_ada, (0, chip * 3072), (DEPTH, 3072))
    mod_loc = jnp.concatenate(
        [_ada_fwd(c16, w_ada, b_loc[l:l + 1], l, "ada_fwd")[:8] for l in range(DEPTH)], axis=0)
    mod_all = _allgather8(mod_loc, "gather_mod")[0][0::2]
    mods = []
    for l in range(DEPTH):
        row = lax.dynamic_slice(mod_all, (0, l * 8 + dev, 0), (N_CHIPS, 1, 3072)).reshape(1, 6 * D_MODEL)
        mods.append([row[:, k * D_MODEL:(k + 1) * D_MODEL] for k in range(6)])

    big = dict(w_in=w_in, w_og=w_out_gla, w_oc=w_out_conv, w_o=w_o, w_up=w_up, w_dn=w_down)
    gathers = {}
    tok = 0.0 * (mod_all[0, 0, 0] + sm_all[0, 0])
    for l in range(DEPTH):
        for k in _BIG:
            shard = (big[k][l] + tok).astype(BF16)
            land = lax.dynamic_update_slice(lax.empty((N_CHIPS,) + shard.shape, BF16), shard[None], (chip, 0, 0))
            *handle, token = _gather_start(land, "gather_start_%s_%d" % (k, l))
            gathers[k, l] = tuple(handle)
            tok = token[0, 0]

    def gathered(k, l, after):
        full = _gather_wait(gathers[k, l], after, "gather_wait_%s_%d" % (k, l))
        if k in _ROW_SHARDED:
            return full.reshape(N_CHIPS * full.shape[1], full.shape[2])
        return _w_in_from_chips(full) if k == "w_in" else full

    saved = []
    h = None
    xin = xt
    for l in range(DEPTH):
        sh1, sc1, g1, sh2, sc2, g2 = mods[l]
        gn = [norm_g_f[l, k][None] for k in range(4)]
        wa = _pad_rows(w_a2_f[l], LR_PAD)
        ba = b_a2[l][None]
        gng = gla_norm_g[l][None]
        cw8 = _pad_rows(conv_w_f[l], 8)
        fw8 = _pad_rows(ffn_w_f[l], 8)
        if l == 0:
            h = _pre_norm(xin, gn[0] + tok, sc1, sh1, "pre_norm")
        wi = gathered("w_in", l, h)
        p = _matmul(h, wi, "nn", BF16, "mm_in", tn=1152)
        o, st = _gla_fwd(p, wa, ba, "gla_fwd")
        za = _gla_out_fwd(o, p, gng, "gla_out_fwd")
        zb = _conv_fwd(p, cw8, "conv_fwd")
        wog, woc = gathered("w_og", l, zb), gathered("w_oc", l, zb)
        ya = _matmul(za, wog, "nn", F32, "mm_out_gla", tm=2048, b_chips=True)
        yb = _matmul(zb, woc, "nn", F32, "mm_out_conv", tm=2048, b_chips=True)
        mm = _merge_fwd(ya, yb, p, "merge_fwd")
        wo = gathered("w_o", l, mm)
        y = _matmul(mm, wo, "nn", F32, "mm_o")
        x1, h2 = _post_pre(xin, y, g1, gn[1], gn[2], sc2, sh2, "post_pre")
        wup = gathered("w_up", l, h2)
        u = _matmul(h2, wup, "nn", BF16, "mm_up", tn=1408, b_chips=True)
        f = _ffn_fwd(u, fw8, "ffn_fwd")
        wdn = gathered("w_dn", l, f)
        y2 = _matmul(f, wdn, "nn", F32, "mm_down", tm=1024, tn=2048, tk=1408)
        saved.append(dict(xin=xin, h=h, p=p, o=o, st=st, za=za, zb=zb, ya=ya, yb=yb, mm=mm, y=y, x1=x1, h2=h2,
                          u=u, f=f, y2=y2, wi=wi, wog=wog, woc=woc, wo=wo, wup=wup, wdn=wdn, wa=wa, ba=ba,
                          gng=gng, cw8=cw8, fw8=fw8, gn=gn, mod=mods[l]))
        if l + 1 < DEPTH:
            nsh1, nsc1 = mods[l + 1][0], mods[l + 1][1]
            xin, h = _post_pre(x1, y2, g2, gn[3], norm_g_f[l + 1, 0][None], nsc1, nsh1, "post_pre")
        else:
            dx, loss_tile = _post_loss(x1, y2, g2, gn[3], tgt, "post_loss")
    loss = lax.psum(loss_tile[0, 0], ("x", "y", "c"))

    scatters = {}

    def scatter(k, l, dw, after=None):
        if k in _ROW_SHARDED:
            send = dw.reshape(N_CHIPS, dw.shape[0] // N_CHIPS, dw.shape[1])
        else:
            send = _w_in_to_chips(dw) if k == "w_in" else dw
        *handle, token = _scatter_start(send, "scatter_start_%s_%d" % (k, l), after)
        scatters[k, l] = tuple(handle)
        return token[0, 0]

    sm = {k: [None] * DEPTH for k in ("dmod", "norm_g", "w_a2", "b_a2", "gng", "conv_w", "ffn_w")}
    for l in reversed(range(DEPTH)):
        s = saved[l]
        sh1, sc1, g1, sh2, sc2, g2 = s["mod"]
        gn = s["gn"]
        dy2, dg2, dgn3 = _post_bwd(dx, s["y2"], g2, gn[3], "post_bwd")
        tk = scatter("w_dn", l, _matmul(s["f"], dy2, "tn", BF16, "mm_down_dw", tm=512, tn=1024, tk=4096))
        df = _matmul(dy2, s["wdn"], "nt", F32, "mm_down_dx", tn=1408)
        dgate, dup, dfw = _ffn_bwd(df, s["u"], s["fw8"] + tk, "ffn_bwd")
        du = (dgate, dup)
        tk = scatter("w_up", l, _matmul(s["h2"], du, "tn", BF16, "mm_up_dw", tm=1024, tn=1408, tk=2048, out_chips=True))
        dh2 = _matmul(du, s["wup"], "nt", F32, "mm_up_dx", tm=1024, tn=2048, tk=1408, b_chips=True)
        dx1, dsh2, dsc2, dgn2 = _pre_bwd(dh2, s["x1"], dx, gn[2] + tk, sc2, "pre_bwd")
        dy, dg1, dgn1 = _post_bwd(dx1, s["y"], g1, gn[1], "post_bwd")
        tk = scatter("w_o", l, _matmul(s["mm"], dy, "tn", BF16, "mm_o_dw", tk=4096))
        dm = _matmul(dy, s["wo"], "nt", F32, "mm_o_dx")
        dya, dp = _merge_bwd(dm, s["ya"], s["p"], OFF_GA, None, "merge_bwd_a")
        dyb, dp = _merge_bwd(dm, s["yb"], s["p"], OFF_GB, dp, "merge_bwd_b")
        tk = tk + scatter("w_og", l, _matmul(s["za"], dya, "tn", BF16, "mm_out_gla_dw", tk=4096, out_chips=True))
        dza = _matmul(dya, s["wog"], "nt", F32, "mm_out_gla_dx", tm=2048, b_chips=True)
        do, dp, dgng = _gla_out_bwd(dza, s["o"], s["p"], s["gng"] + tk, dp, "gla_out_bwd")
        tk = scatter("w_oc", l, _matmul(s["zb"], dyb, "tn", BF16, "mm_out_conv_dw", tk=4096, out_chips=True))
        dzb = _matmul(dyb, s["woc"], "nt", F32, "mm_out_conv_dx", tm=2048, b_chips=True)
        dp, dcw = _conv_bwd(dzb, s["p"], s["cw8"] + tk, dp, "conv_bwd")
        dp, dlr, dwa, dba = _gla_bwd(do, s["p"], s["st"], s["wa"], s["ba"], dp, "gla_bwd")
        dp = lax.dynamic_update_slice(dp, dlr, (0, OFF_LR))
        dw_in = _matmul(s["h"], dp, "tn", BF16, "mm_in_dw", tm=512, tn=1152, tk=4096)
        tk = scatter("w_in", l, dw_in) if l > 0 else 0.0
        dh = _matmul(dp, s["wi"], "nt", F32, "mm_in_dx", tm=1024, tn=2048, tk=1152)
        dx, dsh1, dsc1, dgn0 = _pre_bwd(dh, s["xin"], dx1, gn[0] + tk, sc1, "pre_bwd")
        sm["dmod"][l] = jnp.concatenate([dsh1, dsc1, dg1, dsh2, dsc2, dg2], axis=1)[0]
        sm["norm_g"][l] = jnp.concatenate([dgn0, dgn1, dgn2, dgn3], axis=0)
        sm["w_a2"][l] = dwa[:GLA_LOWRANK]
        sm["b_a2"][l] = dba[0]
        sm["gng"][l] = dgng[0]
        sm["conv_w"][l] = dcw[:3]
        sm["ffn_w"][l] = dfw[:3]
    grad_x = dx[None]

    names = ("dmod", "norm_g", "w_a2", "b_a2", "gng", "conv_w", "ffn_w")
    parts = [jnp.stack(sm[k]).reshape(-1) for k in names]
    shapes = [jnp.stack(sm[k]).shape for k in names]
    sizes = [a.shape[0] for a in parts]
    flat = jnp.concatenate(parts)
    rows = -(-flat.shape[0] // 1024) * 8
    flat = jnp.concatenate([flat, jnp.zeros((rows * 128 - flat.shape[0],), F32)]).reshape(rows, 128)
    gath, tot = _allgather8(flat, "reduce_small")
    tk = scatter("w_in", 0, dw_in, after=tot)
    c16 = c16 + tk
    po = [0]
    for s_ in sizes:
        po.append(po[-1] + s_)
    tot = tot.reshape(-1)
    tot_of = {k: tot[po[i]:po[i + 1]].reshape(shapes[i]) for i, k in enumerate(names)}
    dmod_all = gath.reshape(N_DEV, -1)[:, po[0]:po[1]].reshape(N_DEV, DEPTH, 6 * D_MODEL)

    def chip_cols(a, width):
        return lax.dynamic_slice_in_dim(a, chip * width, width, axis=a.ndim - 1)

    dml = jnp.transpose(chip_cols(dmod_all, 3072), (1, 0, 2))
    dml = jnp.concatenate([dml, jnp.zeros_like(dml)], axis=1)
    g_w_ada = _ada_bwd(c16, dml, "ada_bwd")

    def upd(w, m, v, ga, gb, name):
        sh = w.shape
        as3 = sh if len(sh) == 3 else (1,) + sh
        outs = _adamw(w.reshape(as3), m.reshape(as3), v.reshape(as3), ga.reshape(as3),
                      None if gb is None else gb.reshape(as3), name)
        return [a.reshape(sh) for a in outs]

    res = {}
    res["w_ada"] = upd(w_ada, m_w_ada, v_w_ada, g_w_ada, None, "adamw")
    res["b_ada"] = upd(b_ada, m_b_ada, v_b_ada, tot_of["dmod"], None, "adamw")
    res["norm_g"] = upd(norm_g, m_norm_g, v_norm_g, chip_cols(tot_of["norm_g"], 512), None, "adamw")
    res["w_a2"] = upd(w_a2, m_w_a2, v_w_a2, chip_cols(tot_of["w_a2"], 128), None, "adamw")
    res["b_a2"] = upd(b_a2, m_b_a2, v_b_a2, tot_of["b_a2"], None, "adamw")
    res["gla_norm_g"] = upd(gla_norm_g, m_gla_norm_g, v_gla_norm_g, tot_of["gng"], None, "adamw")
    res["conv_mix_w"] = upd(conv_mix_w, m_conv_mix_w, v_conv_mix_w, chip_cols(tot_of["conv_w"], 256), None, "adamw")
    res["ffn_conv_w"] = upd(ffn_conv_w, m_ffn_conv_w, v_ffn_conv_w, chip_cols(tot_of["ffn_w"], 1408), None, "adamw")

    full_name = dict(w_in="w_in", w_og="w_out_gla", w_oc="w_out_conv", w_o="w_o", w_up="w_up", w_dn="w_down")
    state = dict(w_in=(w_in, m_w_in, v_w_in), w_og=(w_out_gla, m_w_out_gla, v_w_out_gla),
                 w_oc=(w_out_conv, m_w_out_conv, v_w_out_conv), w_o=(w_o, m_w_o, v_w_o),
                 w_up=(w_up, m_w_up, v_w_up), w_dn=(w_down, m_w_down, v_w_down))
    after = res["w_ada"][3]
    for k in ("w_dn", "w_up", "w_o", "w_og", "w_oc", "w_in"):
        done = [_scatter_wait(scatters[k, l], after, "scatter_wait_%s_%d" % (k, l)) for l in range(DEPTH)]
        plane = _sum_chips([d[0] for d in done], [d[1] for d in done], chip_arr, "sum_chips")
        if k == "w_in":
            plane = jnp.transpose(plane, (2, 0, 1))
            other = _sibling_exchange([plane], "sibling_" + k)[0]
            outs = _adamw(*[jnp.transpose(a, (2, 0, 1)) for a in state[k]], plane, other, "adamw_w_in",
                          tile=(N_IN_CHIP // 4, D_MODEL // 8))
            res[full_name[k]] = [jnp.transpose(a, (1, 2, 0)) for a in outs]
        else:
            other = _sibling_exchange([plane], "sibling_" + k)[0]
            res[full_name[k]] = upd(*state[k], plane, other, "adamw")
        after = res[full_name[k]][3]
    order = ("w_ada", "b_ada", "norm_g", "w_in", "w_a2", "b_a2", "gla_norm_g", "w_out_gla", "conv_mix_w",
             "w_out_conv", "w_o", "w_up", "ffn_conv_w", "w_down")
    return (loss, grad_x, *[res[k][0] for k in order], *[res[k][1] for k in order],
            *[res[k][2] for k in order], *[res[k][3] for k in order])
```

```python
import functools
import math

import jax
import jax.numpy as jnp
from jax import lax
from jax.experimental import pallas as pl
from jax.experimental.pallas import tpu as pltpu

F32 = jnp.float32
BF16 = jnp.bfloat16
MESH = pl.DeviceIdType.MESH

D_MODEL = 2048
DEPTH = 2
CHUNK = 64
GLA_HEADS = 4
GLA_DK = 128
GLA_DV = 256
GLA_QK = GLA_HEADS * GLA_DK
GLA_V = GLA_HEADS * GLA_DV
GLA_LOWRANK = 16
GLA_TAU = 16.0
CONV_WIDTH = 1024
D_FF = 5632
EPS = 1e-6
N_IN = 10256
LR_PAD = 128
N_IN_PAD = N_IN - GLA_LOWRANK + LR_PAD
OFF_Q, OFF_K, OFF_V, OFF_R = 0, 512, 1024, 2048
OFF_CB, OFF_CC, OFF_CX, OFF_GA, OFF_GB, OFF_LR = 3072, 4096, 5120, 6144, 8192, 10240

ADAM_LR = 0.001
ADAM_B1 = 0.9
ADAM_B2 = 0.999
ADAM_EPS = 1e-08
ADAM_WD = 0.01
ADAM_STEP = 10

N_CHIPS = 4
N_DEV = 8
VMEM_LIMIT = 56 * 1024 * 1024
TM_ROW = 256
TM_EW = 512
CW_EW = 512
GLA_ROWS = 256


def _params(sem=None):
    return pltpu.CompilerParams(dimension_semantics=sem, vmem_limit_bytes=VMEM_LIMIT)


def _sigmoid(v):
    return 1.0 / (1.0 + jnp.exp(-v))


def _log_sigmoid(v):
    return jnp.minimum(v, 0.0) - jnp.log(1.0 + jnp.exp(-jnp.abs(v)))


_GELU_C = math.sqrt(2.0 / math.pi)


def _gelu_and_grad(v):
    v2 = v * v
    t = jnp.tanh(_GELU_C * v * (1.0 + 0.044715 * v2))
    half = 0.5 * (1.0 + t)
    return v * half, half + (0.5 * _GELU_C) * v * (1.0 - t * t) * (1.0 + (3.0 * 0.044715) * v2)


def _ld(ref):
    return ref[...].astype(F32)


def _flip(a, d):
    return a + d - 2 * a * d


def _unless(cond):
    return jnp.where(cond, 0.0, 1.0).astype(F32)


def _allgather8(xv, name):
    r, cdim = xv.shape

    def body(x_ref, out_ref, sum_ref, send_sems, recv_sems):
        xi, yi, ci = lax.axis_index("x"), lax.axis_index("y"), lax.axis_index("c")
        me = 4 * xi + 2 * yi + ci
        out_ref[pl.ds(me, 1)] = x_ref[...][None]
        started = []
        for k in range(1, N_DEV):
            px, py, pc = _flip(xi, (k >> 2) & 1), _flip(yi, (k >> 1) & 1), _flip(ci, k & 1)
            cp = pltpu.make_async_remote_copy(
                src_ref=x_ref, dst_ref=out_ref.at[me], send_sem=send_sems.at[k - 1], recv_sem=recv_sems.at[k - 1],
                device_id=(px, py, pc), device_id_type=MESH)
            cp.start()
            started.append((cp, 4 * px + 2 * py + pc, k, (px, py, pc)))
        for cp, peer, k, pid in started:
            cp.wait_send()
            pltpu.make_async_remote_copy(
                src_ref=x_ref, dst_ref=out_ref.at[peer], send_sem=send_sems.at[k - 1], recv_sem=recv_sems.at[k - 1],
                device_id=pid, device_id_type=MESH).wait_recv()
        acc = out_ref[0]
        for d in range(1, N_DEV):
            acc = acc + out_ref[d]
        sum_ref[...] = acc

    return pl.pallas_call(
        body, name=name,
        out_shape=(jax.ShapeDtypeStruct((N_DEV, r, cdim), F32), jax.ShapeDtypeStruct((r, cdim), F32)),
        in_specs=[pl.BlockSpec(memory_space=pltpu.VMEM)],
        out_specs=(pl.BlockSpec(memory_space=pltpu.VMEM), pl.BlockSpec(memory_space=pltpu.VMEM)),
        scratch_shapes=[pltpu.SemaphoreType.DMA((N_DEV - 1,)), pltpu.SemaphoreType.DMA((N_DEV - 1,))],
        compiler_params=pltpu.CompilerParams(vmem_limit_bytes=VMEM_LIMIT),
    )(xv)


_HBM = pl.BlockSpec(memory_space=pltpu.HBM)
_SEM = pl.BlockSpec(memory_space=pltpu.SEMAPHORE)
_EFFECT = pltpu.SideEffectType.DATAFLOW_SIDE_EFFECTING
_CHIP_FLIPS = ((1, 0), (0, 1), (1, 1))


def _chip_copies(src_ref, land_ref, send_sems, recv_sems, scatter, halves=False):
    xi, yi, ci = lax.axis_index("x"), lax.axis_index("y"), lax.axis_index("c")
    me = 2 * xi + yi
    out = []

    def slot(j):
        return land_ref.at[j, ci] if halves else land_ref.at[j]

    for k, (dx, dy) in enumerate(_CHIP_FLIPS):
        px, py = _flip(xi, dx), _flip(yi, dy)
        peer = 2 * px + py
        src = src_ref.at[peer] if scatter else slot(me)
        mk = functools.partial(pltpu.make_async_remote_copy, src_ref=src, send_sem=send_sems.at[k],
                               recv_sem=recv_sems.at[k], device_id=(px, py, ci), device_id_type=MESH)
        out.append((mk(dst_ref=slot(me)), mk(dst_ref=slot(peer))))
    return out


def _sibling_fill(land, name):
    def body(land_in, land_out, send_sems, recv_sems):
        xi, yi, ci = lax.axis_index("x"), lax.axis_index("y"), lax.axis_index("c")
        copies = []
        for k, (dx, dy) in enumerate(_CHIP_FLIPS):
            peer = 2 * _flip(xi, dx) + _flip(yi, dy)
            mk = functools.partial(pltpu.make_async_remote_copy, send_sem=send_sems.at[k], recv_sem=recv_sems.at[k],
                                   device_id=(xi, yi, 1 - ci), device_id_type=MESH)
            mine = mk(src_ref=land_in.at[peer, ci], dst_ref=land_out.at[peer, ci])
            theirs = mk(src_ref=land_in.at[peer, 1 - ci], dst_ref=land_out.at[peer, 1 - ci])
            mine.start()
            copies.append((mine, theirs))
        for mine, theirs in copies:
            mine.wait_send()
            theirs.wait_recv()

    return pl.pallas_call(
        body, name=name, out_shape=jax.ShapeDtypeStruct(land.shape, land.dtype),
        in_specs=[pl.BlockSpec(memory_space=pl.ANY)], out_specs=pl.BlockSpec(memory_space=pl.ANY),
        scratch_shapes=[pltpu.SemaphoreType.DMA((3,)), pltpu.SemaphoreType.DMA((3,))],
        input_output_aliases={0: 0},
    )(land)


def _gather_start(land, name, halves=False):
    def body(land_ref, send_sems, recv_sems, land_thru, token):
        for mine, _ in _chip_copies(None, land_ref, send_sems, recv_sems, False, halves):
            mine.start()
        token[...] = jnp.zeros_like(token)

    return pl.pallas_call(
        body, name=name,
        out_shape=(pltpu.SemaphoreType.DMA((3,)), pltpu.SemaphoreType.DMA((3,)), pltpu.HBM(land.shape, land.dtype),
                   jax.ShapeDtypeStruct((8, 128), F32)),
        in_specs=(_HBM,), out_specs=(_SEM, _SEM, _HBM, pl.BlockSpec(memory_space=pltpu.VMEM)),
        input_output_aliases={0: 2},
        compiler_params=pltpu.CompilerParams(has_side_effects=_EFFECT),
    )(pltpu.with_memory_space_constraint(land, pltpu.HBM))


def _gather_wait(handle, after, name, halves=False):
    send, recv, land_thru = handle

    def body(land_ref, send_sems, recv_sems, after_ref, land_out):
        for mine, theirs in _chip_copies(None, land_ref, send_sems, recv_sems, False, halves):
            mine.wait_send()
            theirs.wait_recv()

    return pl.pallas_call(
        body, name=name, out_shape=pltpu.HBM(land_thru.shape, land_thru.dtype),
        in_specs=(_HBM, _SEM, _SEM, pl.BlockSpec(memory_space=pl.ANY)), out_specs=_HBM,
        input_output_aliases={0: 0},
        compiler_params=pltpu.CompilerParams(has_side_effects=_EFFECT),
    )(land_thru, send, recv, after)


def _scatter_start(src, name, after=None):
    extra = [] if after is None else [after]

    def body(src_ref, land_ref, *rest):
        send_sems, recv_sems, src_thru, land_thru, token = rest[len(extra):]
        for mine, _ in _chip_copies(src_ref, land_ref, send_sems, recv_sems, True):
            mine.start()
        token[...] = jnp.zeros_like(token)

    return pl.pallas_call(
        body, name=name,
        out_shape=(pltpu.SemaphoreType.DMA((3,)), pltpu.SemaphoreType.DMA((3,)), pltpu.HBM(src.shape, src.dtype),
                   pltpu.HBM(src.shape, src.dtype), jax.ShapeDtypeStruct((8, 128), F32)),
        in_specs=(_HBM, _HBM) + (pl.BlockSpec(memory_space=pl.ANY),) * len(extra),
        out_specs=(_SEM, _SEM, _HBM, _HBM, pl.BlockSpec(memory_space=pltpu.VMEM)),
        input_output_aliases={0: 2, 1: 3},
        compiler_params=pltpu.CompilerParams(has_side_effects=_EFFECT),
    )(pltpu.with_memory_space_constraint(src, pltpu.HBM),
      pltpu.with_memory_space_constraint(lax.empty(src.shape, src.dtype), pltpu.HBM), *extra)


def _scatter_wait(handle, after, name):
    send, recv, src_thru, land_thru = handle

    def body(src_ref, land_ref, send_sems, recv_sems, after_ref, src_out, land_out):
        for mine, theirs in _chip_copies(src_ref, land_ref, send_sems, recv_sems, True):
            mine.wait_send()
            theirs.wait_recv()

    return pl.pallas_call(
        body, name=name,
        out_shape=(pltpu.HBM(src_thru.shape, src_thru.dtype), pltpu.HBM(land_thru.shape, land_thru.dtype)),
        in_specs=(_HBM, _HBM, _SEM, _SEM, pl.BlockSpec(memory_space=pl.ANY)), out_specs=(_HBM, _HBM),
        input_output_aliases={0: 0, 1: 1},
        compiler_params=pltpu.CompilerParams(has_side_effects=_EFFECT),
    )(src_thru, land_thru, send, recv, after)


def _sibling_exchange(arrays, name):
    n = len(arrays)

    def body(*refs):
        ins, outs = refs[:n], refs[n:2 * n]
        send_sems, recv_sems = refs[2 * n:]
        xi, yi, ci = lax.axis_index("x"), lax.axis_index("y"), lax.axis_index("c")
        cps = []
        for i in range(n):
            cp = pltpu.make_async_remote_copy(
                src_ref=ins[i], dst_ref=outs[i], send_sem=send_sems.at[i], recv_sem=recv_sems.at[i],
                device_id=(xi, yi, 1 - ci), device_id_type=MESH)
            cp.start()
            cps.append(cp)
        for cp in cps:
            cp.wait()

    return pl.pallas_call(
        body, name=name, out_shape=tuple(jax.ShapeDtypeStruct(a.shape, a.dtype) for a in arrays),
        in_specs=[pl.BlockSpec(memory_space=pl.ANY)] * n,
        out_specs=tuple(pl.BlockSpec(memory_space=pl.ANY) for _ in range(n)),
        scratch_shapes=[pltpu.SemaphoreType.DMA((n,)), pltpu.SemaphoreType.DMA((n,))],
    )(*arrays)


def _pick(dim, pref):
    if dim <= pref:
        return dim
    t = (pref // 128) * 128
    while t >= 128:
        if dim % t == 0:
            return t
        t -= 128
    return dim


def _matmul(a, b, dims, out_dtype, name, tm=512, tn=1024, tk=2048, out_chips=False, b_chips=False):
    a_parts = a if isinstance(a, tuple) else (a,)
    b_parts = b if isinstance(b, tuple) else (b,)
    na, nb = len(a_parts), len(b_parts)
    assert (na == 1 or dims == "nt") and (nb == 1 or dims == "tn")
    b_shape = (b_parts[0].shape[1], N_CHIPS * b_parts[0].shape[2]) if b_chips else b_parts[0].shape
    if dims == "nn":
        (m, kd), (_, n) = a_parts[0].shape, b_shape
    elif dims == "nt":
        (m, kd), (n, _) = a_parts[0].shape, b_shape
        kd = na * kd
    else:
        (kd, m), (_, n) = a_parts[0].shape, b_shape
        n = nb * n
    tm = _pick(m, tm)
    tn = _pick(n // N_CHIPS, tn) if (out_chips or (b_chips and dims == "nn")) else _pick(n // nb, tn)
    tk = _pick(kd // N_CHIPS, tk) if (b_chips and dims == "nt") else _pick(kd // na, tk)
    nk, nj = kd // tk, n // tn
    ka, jb = nk // na, nj // nb
    if out_chips:
        per_chip = n // N_CHIPS // tn
        out_shape = jax.ShapeDtypeStruct((N_CHIPS, m, n // N_CHIPS), out_dtype)
        out_spec = pl.BlockSpec((None, tm, tn), lambda j, i, k: (j // per_chip, i, j % per_chip))
    else:
        out_shape = jax.ShapeDtypeStruct((m, n), out_dtype)
        out_spec = pl.BlockSpec((tm, tn), lambda j, i, k: (i, j))
    def part_of(idx, first, count):
        return jnp.clip(idx - first, 0, count - 1)

    if dims == "nn":
        a_specs = [pl.BlockSpec((tm, tk), lambda j, i, k: (i, k))]
        b_specs = [pl.BlockSpec((tk, tn), lambda j, i, k: (k, j))]
        dn = (((1,), (0,)), ((), ()))
    elif dims == "nt":
        a_specs = [pl.BlockSpec((tm, tk), lambda j, i, k, p=p: (i, part_of(k, p * ka, ka))) for p in range(na)]
        b_specs = [pl.BlockSpec((tn, tk), lambda j, i, k: (j, k))]
        dn = (((1,), (1,)), ((), ()))
    else:
        a_specs = [pl.BlockSpec((tk, tm), lambda j, i, k: (k, i))]
        b_specs = [pl.BlockSpec((tk, tn), lambda j, i, k, p=p: (k, part_of(j, p * jb, jb))) for p in range(nb)]
        dn = (((0,), (0,)), ((), ()))
    if b_chips and dims == "nn":
        nper = n // N_CHIPS // tn
        b_specs = [pl.BlockSpec((None, tk, tn), lambda j, i, k: (j // nper, k, j % nper))]
    elif b_chips:
        kper = kd // N_CHIPS // tk
        b_specs = [pl.BlockSpec((None, tn, tk), lambda j, i, k: (k // kper, j, k % kper))]
    direct = nk == 1 or out_dtype == F32

    def body(*refs):
        a_refs, b_refs, o_ref = refs[:na], refs[na:na + nb], refs[na + nb]
        acc_ref = o_ref if direct else refs[na + nb + 1]
        j, k = pl.program_id(0), pl.program_id(2)

        def step(a_ref, b_ref):
            part = lax.dot_general(a_ref[...].astype(BF16), b_ref[...].astype(BF16), dn, preferred_element_type=F32)
            if nk == 1:
                o_ref[...] = part.astype(o_ref.dtype)
                return

            @pl.when(k == 0)
            def _():
                acc_ref[...] = part

            @pl.when(k > 0)
            def _():
                acc_ref[...] += part

            if not direct:
                @pl.when(k == nk - 1)
                def _():
                    o_ref[...] = acc_ref[...].astype(o_ref.dtype)

        if na == 1 and nb == 1:
            step(a_refs[0], b_refs[0])
        for p in range(na if na > 1 else 0):
            pl.when(jnp.logical_and(k >= p * ka, k < (p + 1) * ka))(functools.partial(step, a_refs[p], b_refs[0]))
        for p in range(nb if nb > 1 else 0):
            pl.when(jnp.logical_and(j >= p * jb, j < (p + 1) * jb))(functools.partial(step, a_refs[0], b_refs[p]))

    return pl.pallas_call(
        body, name=name, out_shape=out_shape,
        grid=(nj, m // tm, nk),
        in_specs=a_specs + b_specs,
        out_specs=out_spec,
        scratch_shapes=[] if direct else [pltpu.VMEM((tm, tn), F32)],
        compiler_params=_params(("parallel", "parallel", "arbitrary")),
    )(*a_parts, *b_parts)


def _rstd(v):
    return lax.rsqrt(jnp.mean(v * v, axis=-1, keepdims=True) + EPS)


def _row(tm):
    return pl.BlockSpec((tm, D_MODEL), lambda i: (i, 0))


_VEC = pl.BlockSpec((1, D_MODEL), lambda i: (0, 0))


def _pre_norm(x, gn, sc, sh, name):
    t = x.shape[0]
    tm = min(TM_ROW, t)

    def body(x_ref, gn_ref, sc_ref, sh_ref, h_ref):
        xv = x_ref[...]
        h_ref[...] = ((xv * _rstd(xv) * gn_ref[...]) * (1.0 + sc_ref[...]) + sh_ref[...]).astype(BF16)

    return pl.pallas_call(
        body, name=name, out_shape=jax.ShapeDtypeStruct((t, D_MODEL), BF16), grid=(t // tm,),
        in_specs=[_row(tm), _VEC, _VEC, _VEC], out_specs=_row(tm),
        compiler_params=_params(("parallel",)),
    )(x, gn, sc, sh)


def _post_pre(x, y, g, gnp, gn, sc, sh, name):
    t = x.shape[0]
    tm = min(TM_ROW, t)

    def body(x_ref, y_ref, g_ref, gnp_ref, gn_ref, sc_ref, sh_ref, x1_ref, h_ref):
        yv = y_ref[...]
        x1 = x_ref[...] + g_ref[...] * (yv * _rstd(yv) * gnp_ref[...])
        x1_ref[...] = x1
        h_ref[...] = ((x1 * _rstd(x1) * gn_ref[...]) * (1.0 + sc_ref[...]) + sh_ref[...]).astype(BF16)

    return pl.pallas_call(
        body, name=name,
        out_shape=(jax.ShapeDtypeStruct((t, D_MODEL), F32), jax.ShapeDtypeStruct((t, D_MODEL), BF16)),
        grid=(t // tm,),
        in_specs=[_row(tm), _row(tm), _VEC, _VEC, _VEC, _VEC, _VEC], out_specs=(_row(tm), _row(tm)),
        compiler_params=_params(("parallel",)),
    )(x, y, g, gnp, gn, sc, sh)


def _post_loss(x, y, g, gnp, tgt, name):
    t = x.shape[0]
    tm = min(TM_ROW, t)

    def body(x_ref, y_ref, g_ref, gnp_ref, t_ref, dx_ref, loss_ref):
        yv = y_ref[...]
        diff = x_ref[...] + g_ref[...] * (yv * _rstd(yv) * gnp_ref[...]) - t_ref[...]
        dx_ref[...] = diff * (1.0 / D_MODEL)
        part = (0.5 / D_MODEL) * jnp.sum(jnp.sum(diff * diff, axis=-1, keepdims=True), axis=0, keepdims=True)

        @pl.when(pl.program_id(0) == 0)
        def _():
            loss_ref[...] = jnp.zeros_like(loss_ref)

        loss_ref[...] += jnp.broadcast_to(part, loss_ref.shape)

    return pl.pallas_call(
        body, name=name,
        out_shape=(jax.ShapeDtypeStruct((t, D_MODEL), F32), jax.ShapeDtypeStruct((8, 128), F32)),
        grid=(t // tm,),
        in_specs=[_row(tm), _row(tm), _VEC, _VEC, _row(tm)],
        out_specs=(_row(tm), pl.BlockSpec((8, 128), lambda i: (0, 0))),
        compiler_params=_params(("arbitrary",)),
    )(x, y, g, gnp, tgt)


def _acc_rows(ref, val):
    @pl.when(pl.program_id(0) == 0)
    def _():
        ref[...] = jnp.zeros_like(ref)

    ref[...] += jnp.sum(val, axis=0, keepdims=True)


def _post_bwd(dxn, y, g, gnp, name):
    t = y.shape[0]
    tm = min(TM_ROW, t)

    def body(dx_ref, y_ref, g_ref, gnp_ref, dy_ref, dg_ref, dgn_ref):
        yv, dxv = y_ref[...], dx_ref[...]
        r = _rstd(yv)
        yh = yv * r
        _acc_rows(dg_ref, dxv * (yh * gnp_ref[...]))
        dn = dxv * g_ref[...]
        _acc_rows(dgn_ref, dn * yh)
        dyh = dn * gnp_ref[...]
        dy_ref[...] = (r * (dyh - yh * jnp.mean(dyh * yh, axis=-1, keepdims=True))).astype(BF16)

    return pl.pallas_call(
        body, name=name,
        out_shape=(jax.ShapeDtypeStruct((t, D_MODEL), BF16), jax.ShapeDtypeStruct((1, D_MODEL), F32),
                   jax.ShapeDtypeStruct((1, D_MODEL), F32)),
        grid=(t // tm,),
        in_specs=[_row(tm), _row(tm), _VEC, _VEC], out_specs=(_row(tm), _VEC, _VEC),
        compiler_params=_params(("arbitrary",)),
    )(dxn, y, g, gnp)


def _pre_bwd(dh, xin, dres, gn, sc, name):
    t = xin.shape[0]
    tm = min(TM_ROW, t)

    def body(dh_ref, x_ref, dres_ref, gn_ref, sc_ref, dx_ref, dsh_ref, dsc_ref, dgn_ref):
        xv, dhv = x_ref[...], dh_ref[...]
        r = _rstd(xv)
        xh = xv * r
        _acc_rows(dsh_ref, dhv)
        _acc_rows(dsc_ref, dhv * (xh * gn_ref[...]))
        dn = dhv * (1.0 + sc_ref[...])
        _acc_rows(dgn_ref, dn * xh)
        dxh = dn * gn_ref[...]
        dx_ref[...] = dres_ref[...] + r * (dxh - xh * jnp.mean(dxh * xh, axis=-1, keepdims=True))

    vec = jax.ShapeDtypeStruct((1, D_MODEL), F32)
    return pl.pallas_call(
        body, name=name, out_shape=(jax.ShapeDtypeStruct((t, D_MODEL), F32), vec, vec, vec),
        grid=(t // tm,),
        in_specs=[_row(tm), _row(tm), _row(tm), _VEC, _VEC], out_specs=(_row(tm), _VEC, _VEC, _VEC),
        compiler_params=_params(("arbitrary",)),
    )(dh, xin, dres, gn, sc)


def _fix_rows(v8, rows):
    idx = lax.broadcasted_iota(jnp.int32, v8.shape, 0)
    for j, val in rows:
        v8 = jnp.where(idx == j, jnp.broadcast_to(val, v8.shape), v8)
    return v8


def _shift_down(v, halo, s):
    hr, tm = halo.shape[0], v.shape[0]
    out = pltpu.roll(v, s, 0)
    if tm == 8:
        return _fix_rows(out, [(j, halo[hr - s + j:hr - s + j + 1, :]) for j in range(s)])
    head = _fix_rows(out[0:8, :], [(j, halo[hr - s + j:hr - s + j + 1, :]) for j in range(s)])
    return jnp.concatenate([head, out[8:, :]], axis=0)


def _shift_up(v, halo, s):
    tm = v.shape[0]
    out = pltpu.roll(v, tm - s, 0)
    tail = _fix_rows(out[tm - 8:, :], [(8 - s + j, halo[j:j + 1, :]) for j in range(s)])
    return jnp.concatenate([out[:tm - 8, :], tail], axis=0)


def _tile_specs(tm, cw, off, nrow, hr=8):
    ob = off // cw
    per = tm // hr
    main = pl.BlockSpec((tm, cw), lambda j, i: (i, ob + j))
    prev = pl.BlockSpec((hr, cw), lambda j, i: (jnp.maximum(i * per - 1, 0), ob + j))
    nxt = pl.BlockSpec((hr, cw), lambda j, i: (jnp.minimum((i + 1) * per, nrow * per - 1), ob + j))
    return main, prev, nxt


def _conv_fwd(p, w, name):
    t = p.shape[0]
    tm, cw = min(TM_EW, t), CW_EW
    nrow = t // tm
    cb_s, _, _ = _tile_specs(tm, cw, OFF_CB, nrow, 16)
    cc_s, cc_p, _ = _tile_specs(tm, cw, OFF_CC, nrow, 16)
    cx_s, cx_p, _ = _tile_specs(tm, cw, OFF_CX, nrow, 16)

    def body(cb_ref, cc_ref, ccp_ref, cx_ref, cxp_ref, w_ref, z_ref):
        u = _ld(cc_ref) * _ld(cx_ref)
        uh = _ld(ccp_ref) * _ld(cxp_ref) * _unless(pl.program_id(1) == 0)
        wv = w_ref[...]
        conv = wv[2:3, :] * u + wv[1:2, :] * _shift_down(u, uh, 1) + wv[0:1, :] * _shift_down(u, uh, 2)
        z_ref[...] = (_ld(cb_ref) * conv).astype(BF16)

    return pl.pallas_call(
        body, name=name, out_shape=jax.ShapeDtypeStruct((t, CONV_WIDTH), BF16),
        grid=(CONV_WIDTH // cw, nrow),
        in_specs=[cb_s, cc_s, cc_p, cx_s, cx_p, pl.BlockSpec((8, cw), lambda j, i: (0, j))],
        out_specs=pl.BlockSpec((tm, cw), lambda j, i: (i, j)),
        compiler_params=_params(("parallel", "arbitrary")),
    )(p, p, p, p, p, w)


def _acc_w(ref, vals):
    @pl.when(pl.program_id(1) == 0)
    def _():
        ref[...] = jnp.zeros_like(ref)

    for j, v in enumerate(vals):
        ref[j:j + 1, :] += jnp.sum(v, axis=0, keepdims=True)


def _conv_bwd(dz, p, w, dp, name):
    t = p.shape[0]
    tm, cw = min(TM_EW // 2, t), CONV_WIDTH
    nrow = t // tm
    dz_s, _, dz_n = _tile_specs(tm, cw, 0, nrow)
    cb_s, _, cb_n = _tile_specs(tm, cw, OFF_CB, nrow, 16)
    cc_s, cc_p, _ = _tile_specs(tm, cw, OFF_CC, nrow, 16)
    cx_s, cx_p, _ = _tile_specs(tm, cw, OFF_CX, nrow, 16)

    def body(dz_ref, dzn_ref, cb_ref, cbn_ref, cc_ref, ccp_ref, cx_ref, cxp_ref, w_ref, dp_in, dp_ref, dw_ref):
        dcb_ref = dp_ref.at[:, 0:cw]
        dcc_ref = dp_ref.at[:, cw:2 * cw]
        dcx_ref = dp_ref.at[:, 2 * cw:3 * cw]
        i = pl.program_id(1)
        ccv, cxv, dzv = _ld(cc_ref), _ld(cx_ref), dz_ref[...]
        u = ccv * cxv
        uh = _ld(ccp_ref) * _ld(cxp_ref) * _unless(i == 0)
        wv = w_ref[...]
        u1, u2 = _shift_down(u, uh, 1), _shift_down(u, uh, 2)
        conv = wv[2:3, :] * u + wv[1:2, :] * u1 + wv[0:1, :] * u2
        dcb_ref[...] = (dzv * conv).astype(BF16)
        dconv = dzv * _ld(cb_ref)
        dch = dzn_ref[...] * _ld(cbn_ref)[0:8, :] * _unless(i == nrow - 1)
        du = wv[2:3, :] * dconv + wv[1:2, :] * _shift_up(dconv, dch, 1) + wv[0:1, :] * _shift_up(dconv, dch, 2)
        dcc_ref[...] = (du * cxv).astype(BF16)
        dcx_ref[...] = (du * ccv).astype(BF16)
        _acc_w(dw_ref, (dconv * u2, dconv * u1, dconv * u))

    w_s = pl.BlockSpec((8, cw), lambda j, i: (0, j))
    return pl.pallas_call(
        body, name=name, out_shape=(_dp_shape(t), jax.ShapeDtypeStruct((8, CONV_WIDTH), F32)),
        grid=(1, nrow),
        in_specs=[dz_s, dz_n, cb_s, cb_n, cc_s, cc_p, cx_s, cx_p, w_s, _ANY],
        out_specs=(pl.BlockSpec((tm, 3 * cw), lambda j, i: (i, OFF_CB // (3 * cw))), w_s),
        input_output_aliases={9: 0},
        compiler_params=_params(("parallel", "arbitrary")),
    )(dz, dz, p, p, p, p, p, p, w, dp)


def _ffn_fwd(u, w, name):
    t = u.shape[0]
    tm, cw = min(TM_EW, t), CW_EW
    nrow = t // tm
    g_s, g_p, _ = _tile_specs(tm, cw, 0, nrow, 16)
    u_s, _, _ = _tile_specs(tm, cw, D_FF, nrow, 16)

    def body(g_ref, gp_ref, u_ref, w_ref, f_ref):
        gv = _ld(g_ref)
        gh = _ld(gp_ref) * _unless(pl.program_id(1) == 0)
        wv = w_ref[...]
        gc = wv[2:3, :] * gv + wv[1:2, :] * _shift_down(gv, gh, 1) + wv[0:1, :] * _shift_down(gv, gh, 2)
        f_ref[...] = (_gelu_and_grad(gc)[0] * _ld(u_ref)).astype(BF16)

    return pl.pallas_call(
        body, name=name, out_shape=jax.ShapeDtypeStruct((t, D_FF), BF16),
        grid=(D_FF // cw, nrow),
        in_specs=[g_s, g_p, u_s, pl.BlockSpec((8, cw), lambda j, i: (0, j))],
        out_specs=pl.BlockSpec((tm, cw), lambda j, i: (i, j)),
        compiler_params=_params(("parallel", "arbitrary")),
    )(u, u, u, w)


def _ffn_bwd(df, u, w, name):
    t = u.shape[0]
    tm, cw = min(TM_EW, t), CW_EW
    nrow = t // tm
    df_s, _, df_n = _tile_specs(tm, cw, 0, nrow)
    g_s, g_p, g_n = _tile_specs(tm, cw, 0, nrow, 16)
    u_s, _, u_n = _tile_specs(tm, cw, D_FF, nrow, 16)

    def body(df_ref, dfn_ref, g_ref, gp_ref, gn_ref, u_ref, un_ref, w_ref, dg_ref, du_ref, dw_ref):
        i = pl.program_id(1)
        gv, dfv, uv = _ld(g_ref), df_ref[...], _ld(u_ref)
        gh = _ld(gp_ref) * _unless(i == 0)
        wv = w_ref[...]
        g1, g2 = _shift_down(gv, gh, 1), _shift_down(gv, gh, 2)
        gc = wv[2:3, :] * gv + wv[1:2, :] * g1 + wv[0:1, :] * g2
        act, act_grad = _gelu_and_grad(gc)
        du_ref[...] = (dfv * act).astype(BF16)
        dgc = dfv * uv * act_grad
        gnv = _ld(gn_ref)[0:8, :]
        gtail = gv[tm - 8:tm, :]
        gcn = (wv[2:3, :] * gnv + wv[1:2, :] * _shift_down(gnv, gtail, 1) + wv[0:1, :] * _shift_down(gnv, gtail, 2))
        dgcn = dfn_ref[...] * _ld(un_ref)[0:8, :] * _gelu_and_grad(gcn)[1] * _unless(i == nrow - 1)
        dg = wv[2:3, :] * dgc + wv[1:2, :] * _shift_up(dgc, dgcn, 1) + wv[0:1, :] * _shift_up(dgc, dgcn, 2)
        dg_ref[...] = dg.astype(BF16)
        _acc_w(dw_ref, (dgc * g2, dgc * g1, dgc * gv))

    o_s = pl.BlockSpec((tm, cw), lambda j, i: (i, j))
    o_sh = jax.ShapeDtypeStruct((t, D_FF), BF16)
    w_s = pl.BlockSpec((8, cw), lambda j, i: (0, j))
    return pl.pallas_call(
        body, name=name, out_shape=(o_sh, o_sh, jax.ShapeDtypeStruct((8, D_FF), F32)),
        grid=(D_FF // cw, nrow),
        in_specs=[df_s, df_n, g_s, g_p, g_n, u_s, u_n, w_s],
        out_specs=(o_s, o_s, w_s),
        compiler_params=_params(("parallel", "arbitrary")),
    )(df, df, u, u, u, u, u, w)


def _merge_fwd(ya, yb, p, name):
    t = ya.shape[0]
    tm, cw = min(TM_EW, t), CW_EW
    y_s = pl.BlockSpec((tm, cw), lambda i, j: (i, j))

    def body(ya_ref, yb_ref, ga_ref, gb_ref, m_ref):
        m_ref[...] = (_sigmoid(_ld(ga_ref)) * ya_ref[...] + _sigmoid(_ld(gb_ref)) * yb_ref[...]).astype(BF16)

    return pl.pallas_call(
        body, name=name, out_shape=jax.ShapeDtypeStruct((t, D_MODEL), BF16),
        grid=(t // tm, D_MODEL // cw),
        in_specs=[y_s, y_s, pl.BlockSpec((tm, cw), lambda i, j: (i, OFF_GA // cw + j)),
                  pl.BlockSpec((tm, cw), lambda i, j: (i, OFF_GB // cw + j))],
        out_specs=y_s, compiler_params=_params(("parallel", "parallel")),
    )(ya, yb, p, p)


_ANY = pl.BlockSpec(memory_space=pl.ANY)


def _dp_shape(t):
    return jax.ShapeDtypeStruct((t, N_IN_PAD), BF16)


def _merge_bwd(dm, y, p, gate_off, dp, name):
    t = y.shape[0]
    tm, cw = min(TM_EW, t), CW_EW
    y_s = pl.BlockSpec((tm, cw), lambda i, j: (i, j))
    g_s = pl.BlockSpec((tm, cw), lambda i, j: (i, gate_off // cw + j))

    def body(dm_ref, y_ref, g_ref, *rest):
        dy_ref, dp_ref = rest[-2:]
        dmv = dm_ref[...]
        sg = _sigmoid(_ld(g_ref))
        dy_ref[...] = (dmv * sg).astype(BF16)
        dp_ref[...] = (dmv * y_ref[...] * sg * (1.0 - sg)).astype(BF16)

    extra = [] if dp is None else [dp]
    return pl.pallas_call(
        body, name=name, out_shape=(jax.ShapeDtypeStruct((t, D_MODEL), BF16), _dp_shape(t)),
        grid=(t // tm, D_MODEL // cw),
        in_specs=[y_s, y_s, g_s] + [_ANY] * len(extra),
        out_specs=(y_s, g_s), input_output_aliases={} if dp is None else {3: 1},
        compiler_params=_params(("parallel", "parallel")),
    )(dm, y, p, *extra)


def _tri(lower):
    r = lax.broadcasted_iota(jnp.int32, (CHUNK, CHUNK), 0)
    c = lax.broadcasted_iota(jnp.int32, (CHUNK, CHUNK), 1)
    return ((c <= r) if lower else (c >= r)).astype(F32)


def _eye_mask():
    r = lax.broadcasted_iota(jnp.int32, (GLA_DK, GLA_DK), 0)
    c = lax.broadcasted_iota(jnp.int32, (GLA_DK, GLA_DK), 1)
    return r == c


def _row_to_col(v):
    return jnp.sum(jnp.where(_eye_mask(), jnp.broadcast_to(v, (GLA_DK, GLA_DK)), 0.0), axis=1, keepdims=True)


def _col_to_row(v):
    return jnp.sum(jnp.where(_eye_mask(), jnp.broadcast_to(v, (GLA_DK, GLA_DK)), 0.0), axis=0, keepdims=True)


def _dot(a, b, dn):
    return lax.dot_general(a.astype(BF16), b.astype(BF16), (dn, ((), ())), preferred_element_type=F32)


_NN = ((1,), (0,))
_NT = ((1,), (1,))
_TN = ((0,), (0,))


def _gate_logits(lr_ref, wa_ref, ba_ref):
    return _dot(lr_ref[...], wa_ref[...], _NN) + ba_ref[...]


def _chunk_decay(la, tri):
    cum = lax.dot_general(tri, la, ((_NN), ((), ())), precision=lax.Precision.HIGHEST, preferred_element_type=F32)
    e = cum[CHUNK - 1:CHUNK, :]
    return cum, e, jnp.exp(e - cum)


def _gla_fwd(p, wa, ba, name):
    t = p.shape[0]
    rows = min(GLA_ROWS, t)
    cb = rows // CHUNK
    nc = t // CHUNK
    scale = GLA_DK ** -0.5

    def body(q_ref, k_ref, v_ref, lr_ref, wa_ref, ba_ref, o_ref, st_ref, s_scr):
        @pl.when(pl.program_id(0) == 0)
        def _():
            s_scr[...] = jnp.zeros_like(s_scr)

        la_all = _log_sigmoid(_gate_logits(lr_ref, wa_ref, ba_ref)) * (1.0 / GLA_TAU)
        tri = _tri(True)
        for ch in range(cb):
            rs = slice(ch * CHUNK, (ch + 1) * CHUNK)
            for h in range(GLA_HEADS):
                ks = slice(h * GLA_DK, (h + 1) * GLA_DK)
                vs = slice(h * GLA_DV, (h + 1) * GLA_DV)
                _, e, w = _chunk_decay(la_all[rs, ks], tri)
                kd = k_ref[rs, ks].astype(F32) * w
                s_new = _row_to_col(jnp.exp(e)) * s_scr[ks, :] + _dot(kd, v_ref[rs, vs], _TN)
                s_scr[ks, :] = s_new
                st_ref[ch, ks, :] = s_new
                o_ref[rs, vs] = _dot(q_ref[rs, ks].astype(F32) * scale, s_new, _NN)

    return pl.pallas_call(
        body, name=name,
        out_shape=(jax.ShapeDtypeStruct((t, GLA_V), F32), jax.ShapeDtypeStruct((nc, GLA_QK, GLA_DV), F32)),
        grid=(t // rows,),
        in_specs=[pl.BlockSpec((rows, GLA_QK), lambda i: (i, OFF_Q // GLA_QK)),
                  pl.BlockSpec((rows, GLA_QK), lambda i: (i, OFF_K // GLA_QK)),
                  pl.BlockSpec((rows, GLA_V), lambda i: (i, OFF_V // GLA_V)),
                  pl.BlockSpec((rows, LR_PAD), lambda i: (i, OFF_LR // LR_PAD)),
                  pl.BlockSpec((LR_PAD, GLA_QK), lambda i: (0, 0)),
                  pl.BlockSpec((1, GLA_QK), lambda i: (0, 0))],
        out_specs=(pl.BlockSpec((rows, GLA_V), lambda i: (i, 0)),
                   pl.BlockSpec((cb, GLA_QK, GLA_DV), lambda i: (i, 0, 0))),
        scratch_shapes=[pltpu.VMEM((GLA_QK, GLA_DV), F32)],
        compiler_params=_params(("arbitrary",)),
    )(p, p, p, p, wa, ba)


def _gla_bwd(do, p, st, wa, ba, dp, name):
    t = p.shape[0]
    rows = min(GLA_ROWS, t)
    cb = rows // CHUNK
    nb = t // rows
    scale = GLA_DK ** -0.5

    def rev(i):
        return nb - 1 - i

    def body(do_ref, q_ref, k_ref, v_ref, lr_ref, st_ref, stp_ref, wa_ref, ba_ref, dp_in,
             dp_ref, dlr_ref, dwa_ref, dba_ref, ds_scr, dz_scr):
        dq_ref = dp_ref.at[:, OFF_Q:OFF_Q + GLA_QK]
        dk_ref = dp_ref.at[:, OFF_K:OFF_K + GLA_QK]
        dv_ref = dp_ref.at[:, OFF_V:OFF_V + GLA_V]
        i = pl.program_id(0)

        @pl.when(i == 0)
        def _():
            ds_scr[...] = jnp.zeros_like(ds_scr)
            dwa_ref[...] = jnp.zeros_like(dwa_ref)
            dba_ref[...] = jnp.zeros_like(dba_ref)

        z_all = _gate_logits(lr_ref, wa_ref, ba_ref)
        la_all = _log_sigmoid(z_all) * (1.0 / GLA_TAU)
        tri, triu = _tri(True), _tri(False)
        last_row = lax.broadcasted_iota(jnp.int32, (CHUNK, GLA_DK), 0) == CHUNK - 1
        keep_prev = _unless(i == nb - 1)
        for ch in reversed(range(cb)):
            rs = slice(ch * CHUNK, (ch + 1) * CHUNK)
            for h in range(GLA_HEADS):
                ks = slice(h * GLA_DK, (h + 1) * GLA_DK)
                vs = slice(h * GLA_DV, (h + 1) * GLA_DV)
                _, e, w = _chunk_decay(la_all[rs, ks], tri)
                kd = k_ref[rs, ks].astype(F32) * w
                exp_e = jnp.exp(e)
                s_c = st_ref[ch, ks, :]
                if ch > 0:
                    s_p = st_ref[ch - 1, ks, :]
                else:
                    s_p = stp_ref[0, ks, :] * keep_prev
                do_c = do_ref[rs, vs]
                vv = v_ref[rs, vs]
                ds_tot = ds_scr[ks, :] + _dot(q_ref[rs, ks].astype(F32) * scale, do_c, _TN)
                dq_ref[rs, ks] = (_dot(do_c, s_c, _NT) * scale).astype(BF16)
                dkd = _dot(vv, ds_tot, _NT)
                dv_ref[rs, vs] = _dot(kd, ds_tot, _NN).astype(BF16)
                dexp_col = jnp.sum(ds_tot * s_p, axis=1, keepdims=True)
                ds_scr[ks, :] = _row_to_col(exp_e) * ds_tot
                dk_ref[rs, ks] = (dkd * w).astype(BF16)
                dwt = dkd * kd
                de = jnp.sum(dwt, axis=0, keepdims=True) + _col_to_row(dexp_col) * exp_e
                dcum = jnp.where(last_row, de - dwt, -dwt)
                da = lax.dot_general(triu, dcum, (_NN, ((), ())), precision=lax.Precision.HIGHEST,
                                     preferred_element_type=F32)
                dz_scr[rs, ks] = da * (1.0 / GLA_TAU) * _sigmoid(-z_all[rs, ks])
        dz = dz_scr[...]
        dlr_ref[...] = _dot(dz, wa_ref[...], _NT).astype(BF16)
        dwa_ref[...] += _dot(lr_ref[...], dz, _TN)
        dba_ref[...] += jnp.sum(dz, axis=0, keepdims=True)

    qkv = OFF_V + GLA_V
    return pl.pallas_call(
        body, name=name,
        out_shape=(_dp_shape(t), jax.ShapeDtypeStruct((t, LR_PAD), BF16),
                   jax.ShapeDtypeStruct((LR_PAD, GLA_QK), F32), jax.ShapeDtypeStruct((1, GLA_QK), F32)),
        grid=(nb,),
        in_specs=[pl.BlockSpec((rows, GLA_V), lambda i: (rev(i), 0)),
                  pl.BlockSpec((rows, GLA_QK), lambda i: (rev(i), OFF_Q // GLA_QK)),
                  pl.BlockSpec((rows, GLA_QK), lambda i: (rev(i), OFF_K // GLA_QK)),
                  pl.BlockSpec((rows, GLA_V), lambda i: (rev(i), OFF_V // GLA_V)),
                  pl.BlockSpec((rows, LR_PAD), lambda i: (rev(i), OFF_LR // LR_PAD)),
                  pl.BlockSpec((cb, GLA_QK, GLA_DV), lambda i: (rev(i), 0, 0)),
                  pl.BlockSpec((1, GLA_QK, GLA_DV), lambda i: (jnp.maximum(rev(i) * cb - 1, 0), 0, 0)),
                  pl.BlockSpec((LR_PAD, GLA_QK), lambda i: (0, 0)),
                  pl.BlockSpec((1, GLA_QK), lambda i: (0, 0)), _ANY],
        out_specs=(pl.BlockSpec((rows, qkv), lambda i: (rev(i), 0)),
                   pl.BlockSpec((rows, LR_PAD), lambda i: (rev(i), 0)),
                   pl.BlockSpec((LR_PAD, GLA_QK), lambda i: (0, 0)),
                   pl.BlockSpec((1, GLA_QK), lambda i: (0, 0))),
        input_output_aliases={9: 0},
        scratch_shapes=[pltpu.VMEM((GLA_QK, GLA_DV), F32), pltpu.VMEM((rows, GLA_QK), F32)],
        compiler_params=_params(("arbitrary",)),
    )(do, p, p, p, p, st, st, wa, ba, dp)


def _gla_out_fwd(o, p, gng, name):
    t = o.shape[0]
    tm = min(TM_EW, t)

    def body(o_ref, r_ref, g_ref, z_ref):
        gv = g_ref[...]
        for h in range(GLA_HEADS):
            vs = slice(h * GLA_DV, (h + 1) * GLA_DV)
            ov, rv = o_ref[:, vs], r_ref[:, vs].astype(F32)
            z_ref[:, vs] = ((ov * _rstd(ov) * gv) * (rv * _sigmoid(rv))).astype(BF16)

    return pl.pallas_call(
        body, name=name, out_shape=jax.ShapeDtypeStruct((t, GLA_V), BF16), grid=(t // tm,),
        in_specs=[pl.BlockSpec((tm, GLA_V), lambda i: (i, 0)),
                  pl.BlockSpec((tm, GLA_V), lambda i: (i, OFF_R // GLA_V)),
                  pl.BlockSpec((1, GLA_DV), lambda i: (0, 0))],
        out_specs=pl.BlockSpec((tm, GLA_V), lambda i: (i, 0)),
        compiler_params=_params(("parallel",)),
    )(o, p, gng)


def _gla_out_bwd(dz, o, p, gng, dp, name):
    t = o.shape[0]
    tm = min(TM_EW, t)

    def body(dz_ref, o_ref, r_ref, g_ref, dp_in, do_ref, dr_ref, dg_ref):
        @pl.when(pl.program_id(0) == 0)
        def _():
            dg_ref[...] = jnp.zeros_like(dg_ref)

        gv = g_ref[...]
        for h in range(GLA_HEADS):
            vs = slice(h * GLA_DV, (h + 1) * GLA_DV)
            ov, rv, dzv = o_ref[:, vs], r_ref[:, vs].astype(F32), dz_ref[:, vs]
            rs = _rstd(ov)
            oh = ov * rs
            sg = _sigmoid(rv)
            dr_ref[:, vs] = (dzv * (oh * gv) * (sg * (1.0 + rv * (1.0 - sg)))).astype(BF16)
            don = dzv * (rv * sg)
            dg_ref[...] += jnp.sum(don * oh, axis=0, keepdims=True)
            doh = don * gv
            do_ref[:, vs] = rs * (doh - oh * jnp.mean(doh * oh, axis=-1, keepdims=True))

    row = pl.BlockSpec((tm, GLA_V), lambda i: (i, 0))
    r_s = pl.BlockSpec((tm, GLA_V), lambda i: (i, OFF_R // GLA_V))
    return pl.pallas_call(
        body, name=name,
        out_shape=(jax.ShapeDtypeStruct((t, GLA_V), F32), _dp_shape(t), jax.ShapeDtypeStruct((1, GLA_DV), F32)),
        grid=(t // tm,),
        in_specs=[row, row, r_s, pl.BlockSpec((1, GLA_DV), lambda i: (0, 0)), _ANY],
        out_specs=(row, r_s, pl.BlockSpec((1, GLA_DV), lambda i: (0, 0))),
        input_output_aliases={4: 1},
        compiler_params=_params(("arbitrary",)),
    )(dz, o, p, gng, dp)


def _ada_fwd(c_all, w, b, layer, name):
    n = w.shape[2]
    tn = _pick(n, 512)

    def body(c_ref, w_ref, b_ref, o_ref):
        cv = c_ref[...]
        o_ref[...] = _dot(cv * _sigmoid(cv), w_ref[...], _NN) + b_ref[...]

    return pl.pallas_call(
        body, name=name, out_shape=jax.ShapeDtypeStruct((16, n), F32), grid=(n // tn,),
        in_specs=[pl.BlockSpec((16, D_MODEL), lambda j: (0, 0)),
                  pl.BlockSpec((None, D_MODEL, tn), lambda j: (layer, 0, j)),
                  pl.BlockSpec((1, tn), lambda j: (0, j))],
        out_specs=pl.BlockSpec((16, tn), lambda j: (0, j)),
        compiler_params=_params(("parallel",)),
    )(c_all, w, b)


def _ada_bwd(c_all, dmod, name):
    n = dmod.shape[2]
    tn = _pick(n, 512)

    def body(c_ref, d_ref, o_ref):
        cv = c_ref[...]
        o_ref[...] = _dot(cv * _sigmoid(cv), d_ref[...], _TN)

    return pl.pallas_call(
        body, name=name, out_shape=jax.ShapeDtypeStruct((DEPTH, D_MODEL, n), F32), grid=(DEPTH, n // tn),
        in_specs=[pl.BlockSpec((16, D_MODEL), lambda l, j: (0, 0)),
                  pl.BlockSpec((None, 16, tn), lambda l, j: (l, 0, j))],
        out_specs=pl.BlockSpec((None, D_MODEL, tn), lambda l, j: (l, 0, j)),
        compiler_params=_params(("parallel", "parallel")),
    )(c_all, dmod)


def _rows_tile(nrows, ncols, target_bytes):
    want = max(16, target_bytes // (4 * ncols))
    if nrows <= want:
        return nrows
    t = (want // 16) * 16
    while t >= 16:
        if nrows % t == 0:
            return t
        t -= 16
    return nrows


def _sum_chips(sent, landed, chip, name):
    _, nrows, ncols = sent[0].shape
    tr = _rows_tile(nrows, ncols, 2 << 20)
    nblk = nrows // tr

    def body(chip_ref, *refs):
        own, got, o_ref = refs[:DEPTH], refs[DEPTH:2 * DEPTH], refs[2 * DEPTH]
        me = chip_ref[0]
        for l in range(DEPTH):
            for j in range(N_CHIPS):
                def add(val):
                    if j == 0:
                        o_ref[...] = val.astype(F32)
                    else:
                        o_ref[...] += val.astype(F32)

                @pl.when(jnp.logical_and(pl.program_id(0) == l, me == j))
                def _():
                    add(own[l][...])

                @pl.when(jnp.logical_and(pl.program_id(0) == l, me != j))
                def _():
                    add(got[l][j])

    def rows_of(layer):
        return lambda l, i, chip_ref: jnp.where(l == layer, i, 0)

    own_specs = [pl.BlockSpec((None, tr, ncols), lambda l, i, chip_ref, r=rows_of(k): (chip_ref[0], r(l, i, chip_ref), 0))
                 for k in range(DEPTH)]
    got_specs = [pl.BlockSpec((N_CHIPS, tr, ncols), lambda l, i, chip_ref, r=rows_of(k): (0, r(l, i, chip_ref), 0))
                 for k in range(DEPTH)]
    return pl.pallas_call(
        body, name=name, out_shape=jax.ShapeDtypeStruct((DEPTH, nrows, ncols), F32),
        grid_spec=pltpu.PrefetchScalarGridSpec(
            num_scalar_prefetch=1, grid=(DEPTH, nblk), in_specs=own_specs + got_specs,
            out_specs=pl.BlockSpec((None, tr, ncols), lambda l, i, chip_ref: (l, i, 0))),
        compiler_params=_params(("arbitrary", "arbitrary")),
    )(chip, *sent, *landed)


def _adamw(w, m, v, ga, gb, name, tile=None):
    two = gb is not None
    c1 = 1.0 - ADAM_B1 ** ADAM_STEP
    c2 = 1.0 - ADAM_B2 ** ADAM_STEP

    def body(*refs):
        if two:
            w_ref, m_ref, v_ref, ga_ref, gb_ref, g_ref, d_ref, nm_ref, nv_ref = refs
            g = ga_ref[...] + gb_ref[...]
        else:
            w_ref, m_ref, v_ref, ga_ref, g_ref, d_ref, nm_ref, nv_ref = refs
            g = ga_ref[...]
        g_ref[...] = g
        nm = ADAM_B1 * m_ref[...] + (1.0 - ADAM_B1) * g
        nv = ADAM_B2 * v_ref[...] + (1.0 - ADAM_B2) * (g * g)
        nm_ref[...] = nm
        nv_ref[...] = nv
        d_ref[...] = -ADAM_LR * ((nm / c1) / (jnp.sqrt(nv / c2) + ADAM_EPS) + ADAM_WD * w_ref[...])

    if tile is None:
        nl, nrows, ncols = w.shape
        tr = _rows_tile(nrows, ncols, 1 << 20)
        blk = pl.BlockSpec((None, tr, ncols), lambda l, i: (l, i, 0))
        grid = (nl, nrows // tr)
    else:
        nrows, nl, ncols = w.shape
        rb, cb = tile
        blk = pl.BlockSpec((rb, nl, cb), lambda i, j: (i, 0, j))
        grid = (nrows // rb, ncols // cb)
    sh = jax.ShapeDtypeStruct(w.shape, F32)
    ins = [w, m, v, ga] + ([gb] if two else [])
    return pl.pallas_call(
        body, name=name, out_shape=(sh, sh, sh, sh), grid=grid,
        in_specs=[blk] * len(ins), out_specs=(blk, blk, blk, blk),
        compiler_params=_params(("parallel", "parallel")),
    )(*ins)


def _pad_rows(a, rows):
    return jnp.concatenate([a, jnp.zeros((rows - a.shape[0],) + a.shape[1:], a.dtype)], axis=0)


N_IN_CHIP = N_IN // N_CHIPS
_LR_LO = 3072 - N_IN_CHIP
_LR_HI = _LR_LO + GLA_LOWRANK


def _w_in_from_chips(a):
    return jnp.concatenate([a[0], a[1][:, :_LR_LO], a[1][:, _LR_HI:], a[2], a[3], a[1][:, _LR_LO:_LR_HI],
                            jnp.zeros((a.shape[1], LR_PAD - GLA_LOWRANK), a.dtype)], axis=1)


def _w_in_to_chips(w):
    s2 = 2 * N_IN_CHIP - GLA_LOWRANK
    s3 = s2 + N_IN_CHIP
    c1 = jnp.concatenate([w[:, N_IN_CHIP:3072], w[:, OFF_LR:OFF_LR + GLA_LOWRANK], w[:, 3072:s2]], axis=1)
    return jnp.stack([w[:, :N_IN_CHIP], c1, w[:, s2:s3], w[:, s3:OFF_LR]])


_BIG = ("w_in", "w_og", "w_oc", "w_o", "w_up", "w_dn")
_ROW_SHARDED = ("w_o", "w_dn")
_TWO_LEVEL = ("w_in", "w_up")


def kernel(x, c, w_ada, b_ada, norm_g, w_in, w_a2, b_a2, gla_norm_g, w_out_gla, conv_mix_w, w_out_conv, w_o, w_up, ffn_conv_w, w_down, loss_target, m_w_ada, m_b_ada, m_norm_g, m_w_in, m_w_a2, m_b_a2, m_gla_norm_g, m_w_out_gla, m_conv_mix_w, m_w_out_conv, m_w_o, m_w_up, m_ffn_conv_w, m_w_down, v_w_ada, v_b_ada, v_norm_g, v_w_in, v_w_a2, v_b_a2, v_gla_norm_g, v_w_out_gla, v_conv_mix_w, v_w_out_conv, v_w_o, v_w_up, v_ffn_conv_w, v_w_down):
    xi, yi, ci = lax.axis_index("x"), lax.axis_index("y"), lax.axis_index("c")
    chip = 2 * xi + yi
    dev = 2 * chip + ci
    chip_arr = jnp.reshape(chip, (1,)).astype(jnp.int32)
    xt = x[0]
    tgt = loss_target[0]

    c_all = _allgather8(jnp.broadcast_to(c, (8, D_MODEL)), "gather_c")[0][:, 0, :]
    c16 = _pad_rows(c_all, 16)
    sm_parts = [norm_g.reshape(-1), w_a2.reshape(-1), conv_mix_w.reshape(-1), ffn_conv_w.reshape(-1)]
    sm_sizes = [a.shape[0] for a in sm_parts]
    sm_flat = jnp.concatenate(sm_parts)
    sm_rows = -(-sm_flat.shape[0] // 128)
    sm_rows = -(-sm_rows // 8) * 8
    sm_flat = jnp.concatenate([sm_flat, jnp.zeros((sm_rows * 128 - sm_flat.shape[0],), F32)]).reshape(sm_rows, 128)
    sm_all = _allgather8(sm_flat, "gather_small")[0].reshape(N_DEV, -1)[0::2]
    offs = [0]
    for s in sm_sizes:
        offs.append(offs[-1] + s)

    def small_full(idx, shape):
        a = sm_all[:, offs[idx]:offs[idx + 1]].reshape((N_CHIPS,) + shape)
        a = jnp.moveaxis(a, 0, -2)
        return a.reshape(shape[:-1] + (N_CHIPS * shape[-1],))

    norm_g_f = small_full(0, (DEPTH, 4, 512))
    w_a2_f = small_full(1, (DEPTH, GLA_LOWRANK, 128))
    conv_w_f = small_full(2, (DEPTH, 3, 256))
    ffn_w_f = small_full(3, (DEPTH, 3, 1408))

    b_loc = lax.dynamic_slice(b_ada, (0, chip * 3072), (DEPTH, 3072))
    mod_loc = jnp.concatenate(
        [_ada_fwd(c16, w_ada, b_loc[l:l + 1], l, "ada_fwd")[:8] for l in range(DEPTH)], axis=0)
    mod_all = _allgather8(mod_loc, "gather_mod")[0][0::2]
    mods = []
    for l in range(DEPTH):
        row = lax.dynamic_slice(mod_all, (0, l * 8 + dev, 0), (N_CHIPS, 1, 3072)).reshape(1, 6 * D_MODEL)
        mods.append([row[:, k * D_MODEL:(k + 1) * D_MODEL] for k in range(6)])

    big = dict(w_in=w_in, w_og=w_out_gla, w_oc=w_out_conv, w_o=w_o, w_up=w_up, w_dn=w_down)
    gathers = {}
    tok = 0.0 * (mod_all[0, 0, 0] + sm_all[0, 0])
    for l in range(DEPTH):
        for k in _BIG:
            shard = (big[k][l] + tok).astype(BF16)
            if k in _TWO_LEVEL:
                shard = shard.reshape(2, shard.shape[0] // 2, shard.shape[1])
            land = lax.dynamic_update_slice(lax.empty((N_CHIPS,) + shard.shape, BF16), shard[None],
                                            (chip,) + (0,) * shard.ndim)
            *handle, token = _gather_start(land, "gather_start_%s_%d" % (k, l), k in _TWO_LEVEL)
            gathers[k, l] = tuple(handle)
            tok = token[0, 0]

    def gathered(k, l, after):
        full = _gather_wait(gathers[k, l], after, "gather_wait_%s_%d" % (k, l), k in _TWO_LEVEL)
        if k in _TWO_LEVEL:
            full = _sibling_fill(full, "sibling_fill_%s_%d" % (k, l))
            full = full.reshape(N_CHIPS, 2 * full.shape[2], full.shape[3])
        if k in _ROW_SHARDED:
            return full.reshape(N_CHIPS * full.shape[1], full.shape[2])
        return _w_in_from_chips(full) if k == "w_in" else full

    saved = []
    h = None
    xin = xt
    for l in range(DEPTH):
        sh1, sc1, g1, sh2, sc2, g2 = mods[l]
        gn = [norm_g_f[l, k][None] for k in range(4)]
        wa = _pad_rows(w_a2_f[l], LR_PAD)
        ba = b_a2[l][None]
        gng = gla_norm_g[l][None]
        cw8 = _pad_rows(conv_w_f[l], 8)
        fw8 = _pad_rows(ffn_w_f[l], 8)
        if l == 0:
            h = _pre_norm(xin, gn[0] + tok, sc1, sh1, "pre_norm")
        wi = gathered("w_in", l, h)
        p = _matmul(h, wi, "nn", BF16, "mm_in", tn=1152)
        o, st = _gla_fwd(p, wa, ba, "gla_fwd")
        za = _gla_out_fwd(o, p, gng, "gla_out_fwd")
        zb = _conv_fwd(p, cw8, "conv_fwd")
        wog, woc = gathered("w_og", l, zb), gathered("w_oc", l, zb)
        ya = _matmul(za, wog, "nn", F32, "mm_out_gla", tm=2048, b_chips=True)
        yb = _matmul(zb, woc, "nn", F32, "mm_out_conv", tm=2048, b_chips=True)
        mm = _merge_fwd(ya, yb, p, "merge_fwd")
        wo = gathered("w_o", l, mm)
        y = _matmul(mm, wo, "nn", F32, "mm_o")
        x1, h2 = _post_pre(xin, y, g1, gn[1], gn[2], sc2, sh2, "post_pre")
        wup = gathered("w_up", l, h2)
        u = _matmul(h2, wup, "nn", BF16, "mm_up", tn=1408, b_chips=True)
        f = _ffn_fwd(u, fw8, "ffn_fwd")
        wdn = gathered("w_dn", l, f)
        y2 = _matmul(f, wdn, "nn", F32, "mm_down", tm=1024, tn=2048, tk=1408)
        saved.append(dict(xin=xin, h=h, p=p, o=o, st=st, za=za, zb=zb, ya=ya, yb=yb, mm=mm, y=y, x1=x1, h2=h2,
                          u=u, f=f, y2=y2, wi=wi, wog=wog, woc=woc, wo=wo, wup=wup, wdn=wdn, wa=wa, ba=ba,
                          gng=gng, cw8=cw8, fw8=fw8, gn=gn, mod=mods[l]))
        if l + 1 < DEPTH:
            nsh1, nsc1 = mods[l + 1][0], mods[l + 1][1]
            xin, h = _post_pre(x1, y2, g2, gn[3], norm_g_f[l + 1, 0][None], nsc1, nsh1, "post_pre")
        else:
            dx, loss_tile = _post_loss(x1, y2, g2, gn[3], tgt, "post_loss")
    loss = lax.psum(loss_tile[0, 0], ("x", "y", "c"))

    scatters = {}

    def scatter(k, l, dw, after=None):
        if k in _ROW_SHARDED:
            send = dw.reshape(N_CHIPS, dw.shape[0] // N_CHIPS, dw.shape[1])
        else:
            send = _w_in_to_chips(dw) if k == "w_in" else dw
        *handle, token = _scatter_start(send, "scatter_start_%s_%d" % (k, l), after)
        scatters[k, l] = tuple(handle)
        return token[0, 0]

    sm = {k: [None] * DEPTH for k in ("dmod", "norm_g", "w_a2", "b_a2", "gng", "conv_w", "ffn_w")}
    for l in reversed(range(DEPTH)):
        s = saved[l]
        sh1, sc1, g1, sh2, sc2, g2 = s["mod"]
        gn = s["gn"]
        dy2, dg2, dgn3 = _post_bwd(dx, s["y2"], g2, gn[3], "post_bwd")
        tk = scatter("w_dn", l, _matmul(s["f"], dy2, "tn", BF16, "mm_down_dw", tm=512, tn=1024, tk=4096))
        df = _matmul(dy2, s["wdn"], "nt", F32, "mm_down_dx", tn=1408)
        dgate, dup, dfw = _ffn_bwd(df, s["u"], s["fw8"] + tk, "ffn_bwd")
        du = (dgate, dup)
        tk = scatter("w_up", l, _matmul(s["h2"], du, "tn", BF16, "mm_up_dw", tm=1024, tn=1408, tk=2048, out_chips=True))
        dh2 = _matmul(du, s["wup"], "nt", F32, "mm_up_dx", tm=1024, tn=2048, tk=1408, b_chips=True)
        dx1, dsh2, dsc2, dgn2 = _pre_bwd(dh2, s["x1"], dx, gn[2] + tk, sc2, "pre_bwd")
        dy, dg1, dgn1 = _post_bwd(dx1, s["y"], g1, gn[1], "post_bwd")
        tk = scatter("w_o", l, _matmul(s["mm"], dy, "tn", BF16, "mm_o_dw", tk=4096))
        dm = _matmul(dy, s["wo"], "nt", F32, "mm_o_dx")
        dya, dp = _merge_bwd(dm, s["ya"], s["p"], OFF_GA, None, "merge_bwd_a")
        dyb, dp = _merge_bwd(dm, s["yb"], s["p"], OFF_GB, dp, "merge_bwd_b")
        tk = tk + scatter("w_og", l, _matmul(s["za"], dya, "tn", BF16, "mm_out_gla_dw", tk=4096, out_chips=True))
        dza = _matmul(dya, s["wog"], "nt", F32, "mm_out_gla_dx", tm=2048, b_chips=True)
        do, dp, dgng = _gla_out_bwd(dza, s["o"], s["p"], s["gng"] + tk, dp, "gla_out_bwd")
        tk = scatter("w_oc", l, _matmul(s["zb"], dyb, "tn", BF16, "mm_out_conv_dw", tk=4096, out_chips=True))
        dzb = _matmul(dyb, s["woc"], "nt", F32, "mm_out_conv_dx", tm=2048, b_chips=True)
        dp, dcw = _conv_bwd(dzb, s["p"], s["cw8"] + tk, dp, "conv_bwd")
        dp, dlr, dwa, dba = _gla_bwd(do, s["p"], s["st"], s["wa"], s["ba"], dp, "gla_bwd")
        dp = lax.dynamic_update_slice(dp, dlr, (0, OFF_LR))
        dw_in = _matmul(s["h"], dp, "tn", BF16, "mm_in_dw", tm=512, tn=1152, tk=4096)
        tk = scatter("w_in", l, dw_in) if l > 0 else 0.0
        dh = _matmul(dp, s["wi"], "nt", F32, "mm_in_dx", tm=1024, tn=2048, tk=1152)
        dx, dsh1, dsc1, dgn0 = _pre_bwd(dh, s["xin"], dx1, gn[0] + tk, sc1, "pre_bwd")
        sm["dmod"][l] = jnp.concatenate([dsh1, dsc1, dg1, dsh2, dsc2, dg2], axis=1)[0]
        sm["norm_g"][l] = jnp.concatenate([dgn0, dgn1, dgn2, dgn3], axis=0)
        sm["w_a2"][l] = dwa[:GLA_LOWRANK]
        sm["b_a2"][l] = dba[0]
        sm["gng"][l] = dgng[0]
        sm["conv_w"][l] = dcw[:3]
        sm["ffn_w"][l] = dfw[:3]
    grad_x = dx[None]

    names = ("dmod", "norm_g", "w_a2", "b_a2", "gng", "conv_w", "ffn_w")
    parts = [jnp.stack(sm[k]).reshape(-1) for k in names]
    shapes = [jnp.stack(sm[k]).shape for k in names]
    sizes = [a.shape[0] for a in parts]
    flat = jnp.concatenate(parts)
    rows = -(-flat.shape[0] // 1024) * 8
    flat = jnp.concatenate([flat, jnp.zeros((rows * 128 - flat.shape[0],), F32)]).reshape(rows, 128)
    gath, tot = _allgather8(flat, "reduce_small")
    tk = scatter("w_in", 0, dw_in, after=tot)
    c16 = c16 + tk
    po = [0]
    for s_ in sizes:
        po.append(po[-1] + s_)
    tot = tot.reshape(-1)
    tot_of = {k: tot[po[i]:po[i + 1]].reshape(shapes[i]) for i, k in enumerate(names)}
    dmod_all = gath.reshape(N_DEV, -1)[:, po[0]:po[1]].reshape(N_DEV, DEPTH, 6 * D_MODEL)

    def chip_cols(a, width):
        return lax.dynamic_slice_in_dim(a, chip * width, width, axis=a.ndim - 1)

    dml = jnp.transpose(chip_cols(dmod_all, 3072), (1, 0, 2))
    dml = jnp.concatenate([dml, jnp.zeros_like(dml)], axis=1)
    g_w_ada = _ada_bwd(c16, dml, "ada_bwd")

    def upd(w, m, v, ga, gb, name):
        sh = w.shape
        as3 = sh if len(sh) == 3 else (1,) + sh
        outs = _adamw(w.reshape(as3), m.reshape(as3), v.reshape(as3), ga.reshape(as3),
                      None if gb is None else gb.reshape(as3), name)
        return [a.reshape(sh) for a in outs]

    res = {}
    res["w_ada"] = upd(w_ada, m_w_ada, v_w_ada, g_w_ada, None, "adamw")
    res["b_ada"] = upd(b_ada, m_b_ada, v_b_ada, tot_of["dmod"], None, "adamw")
    res["norm_g"] = upd(norm_g, m_norm_g, v_norm_g, chip_cols(tot_of["norm_g"], 512), None, "adamw")
    res["w_a2"] = upd(w_a2, m_w_a2, v_w_a2, chip_cols(tot_of["w_a2"], 128), None, "adamw")
    res["b_a2"] = upd(b_a2, m_b_a2, v_b_a2, tot_of["b_a2"], None, "adamw")
    res["gla_norm_g"] = upd(gla_norm_g, m_gla_norm_g, v_gla_norm_g, tot_of["gng"], None, "adamw")
    res["conv_mix_w"] = upd(conv_mix_w, m_conv_mix_w, v_conv_mix_w, chip_cols(tot_of["conv_w"], 256), None, "adamw")
    res["ffn_conv_w"] = upd(ffn_conv_w, m_ffn_conv_w, v_ffn_conv_w, chip_cols(tot_of["ffn_w"], 1408), None, "adamw")

    full_name = dict(w_in="w_in", w_og="w_out_gla", w_oc="w_out_conv", w_o="w_o", w_up="w_up", w_dn="w_down")
    state = dict(w_in=(w_in, m_w_in, v_w_in), w_og=(w_out_gla, m_w_out_gla, v_w_out_gla),
                 w_oc=(w_out_conv, m_w_out_conv, v_w_out_conv), w_o=(w_o, m_w_o, v_w_o),
                 w_up=(w_up, m_w_up, v_w_up), w_dn=(w_down, m_w_down, v_w_down))
    after = res["w_ada"][3]
    for k in ("w_dn", "w_up", "w_o", "w_og", "w_oc", "w_in"):
        done = [_scatter_wait(scatters[k, l], after, "scatter_wait_%s_%d" % (k, l)) for l in range(DEPTH)]
        plane = _sum_chips([d[0] for d in done], [d[1] for d in done], chip_arr, "sum_chips")
        if k == "w_in":
            plane = jnp.transpose(plane, (2, 0, 1))
            other = _sibling_exchange([plane], "sibling_" + k)[0]
            outs = _adamw(*[jnp.transpose(a, (2, 0, 1)) for a in state[k]], plane, other, "adamw_w_in",
                          tile=(N_IN_CHIP // 4, D_MODEL // 8))
            res[full_name[k]] = [jnp.transpose(a, (1, 2, 0)) for a in outs]
        else:
            other = _sibling_exchange([plane], "sibling_" + k)[0]
            res[full_name[k]] = upd(*state[k], plane, other, "adamw")
        after = res[full_name[k]][3]
    order = ("w_ada", "b_ada", "norm_g", "w_in", "w_a2", "b_a2", "gla_norm_g", "w_out_gla", "conv_mix_w",
             "w_out_conv", "w_o", "w_up", "ffn_conv_w", "w_down")
    return (loss, grad_x, *[res[k][0] for k in order], *[res[k][1] for k in order],
            *[res[k][2] for k in order], *[res[k][3] for k in order])
```

```python
import functools
import math

import jax
import jax.numpy as jnp
from jax import lax
from jax.experimental import pallas as pl
from jax.experimental.pallas import tpu as pltpu

F32 = jnp.float32
BF16 = jnp.bfloat16
MESH = pl.DeviceIdType.MESH

D_MODEL = 2048
DEPTH = 2
CHUNK = 64
GLA_HEADS = 4
GLA_DK = 128
GLA_DV = 256
GLA_QK = GLA_HEADS * GLA_DK
GLA_V = GLA_HEADS * GLA_DV
GLA_LOWRANK = 16
GLA_TAU = 16.0
CONV_WIDTH = 1024
D_FF = 5632
EPS = 1e-6
N_IN = 10256
LR_PAD = 128
N_IN_PAD = N_IN - GLA_LOWRANK + LR_PAD
OFF_Q, OFF_K, OFF_V, OFF_R = 0, 512, 1024, 2048
OFF_CB, OFF_CC, OFF_CX, OFF_GA, OFF_GB, OFF_LR = 3072, 4096, 5120, 6144, 8192, 10240

ADAM_LR = 0.001
ADAM_B1 = 0.9
ADAM_B2 = 0.999
ADAM_EPS = 1e-08
ADAM_WD = 0.01
ADAM_STEP = 10

N_CHIPS = 4
N_DEV = 8
VMEM_LIMIT = 56 * 1024 * 1024
TM_ROW = 256
TM_EW = 512
CW_EW = 512
GLA_ROWS = 256


def _params(sem=None):
    return pltpu.CompilerParams(dimension_semantics=sem, vmem_limit_bytes=VMEM_LIMIT)


def _sigmoid(v):
    return 1.0 / (1.0 + jnp.exp(-v))


def _log_sigmoid(v):
    return jnp.minimum(v, 0.0) - jnp.log(1.0 + jnp.exp(-jnp.abs(v)))


_GELU_C = math.sqrt(2.0 / math.pi)


def _gelu_and_grad(v):
    v2 = v * v
    t = jnp.tanh(_GELU_C * v * (1.0 + 0.044715 * v2))
    half = 0.5 * (1.0 + t)
    return v * half, half + (0.5 * _GELU_C) * v * (1.0 - t * t) * (1.0 + (3.0 * 0.044715) * v2)


def _ld(ref):
    return ref[...].astype(F32)


def _flip(a, d):
    return a + d - 2 * a * d


def _unless(cond):
    return jnp.where(cond, 0.0, 1.0).astype(F32)


def _allgather8(xv, name):
    r, cdim = xv.shape

    def body(x_ref, out_ref, sum_ref, send_sems, recv_sems):
        xi, yi, ci = lax.axis_index("x"), lax.axis_index("y"), lax.axis_index("c")
        me = 4 * xi + 2 * yi + ci
        out_ref[pl.ds(me, 1)] = x_ref[...][None]
        started = []
        for k in range(1, N_DEV):
            px, py, pc = _flip(xi, (k >> 2) & 1), _flip(yi, (k >> 1) & 1), _flip(ci, k & 1)
            cp = pltpu.make_async_remote_copy(
                src_ref=x_ref, dst_ref=out_ref.at[me], send_sem=send_sems.at[k - 1], recv_sem=recv_sems.at[k - 1],
                device_id=(px, py, pc), device_id_type=MESH)
            cp.start()
            started.append((cp, 4 * px + 2 * py + pc, k, (px, py, pc)))
        for cp, peer, k, pid in started:
            cp.wait_send()
            pltpu.make_async_remote_copy(
                src_ref=x_ref, dst_ref=out_ref.at[peer], send_sem=send_sems.at[k - 1], recv_sem=recv_sems.at[k - 1],
                device_id=pid, device_id_type=MESH).wait_recv()
        acc = out_ref[0]
        for d in range(1, N_DEV):
            acc = acc + out_ref[d]
        sum_ref[...] = acc

    return pl.pallas_call(
        body, name=name,
        out_shape=(jax.ShapeDtypeStruct((N_DEV, r, cdim), F32), jax.ShapeDtypeStruct((r, cdim), F32)),
        in_specs=[pl.BlockSpec(memory_space=pltpu.VMEM)],
        out_specs=(pl.BlockSpec(memory_space=pltpu.VMEM), pl.BlockSpec(memory_space=pltpu.VMEM)),
        scratch_shapes=[pltpu.SemaphoreType.DMA((N_DEV - 1,)), pltpu.SemaphoreType.DMA((N_DEV - 1,))],
        compiler_params=pltpu.CompilerParams(vmem_limit_bytes=VMEM_LIMIT),
    )(xv)


_HBM = pl.BlockSpec(memory_space=pltpu.HBM)
_SEM = pl.BlockSpec(memory_space=pltpu.SEMAPHORE)
_EFFECT = pltpu.SideEffectType.DATAFLOW_SIDE_EFFECTING
_CHIP_FLIPS = ((1, 0), (0, 1), (1, 1))


def _chip_copies(src_ref, land_ref, send_sems, recv_sems, scatter, halves=False):
    xi, yi, ci = lax.axis_index("x"), lax.axis_index("y"), lax.axis_index("c")
    me = 2 * xi + yi
    out = []

    def slot(j):
        return land_ref.at[j, ci] if halves else land_ref.at[j]

    for k, (dx, dy) in enumerate(_CHIP_FLIPS):
        px, py = _flip(xi, dx), _flip(yi, dy)
        peer = 2 * px + py
        src = src_ref.at[peer] if scatter else slot(me)
        mk = functools.partial(pltpu.make_async_remote_copy, src_ref=src, send_sem=send_sems.at[k],
                               recv_sem=recv_sems.at[k], device_id=(px, py, ci), device_id_type=MESH)
        out.append((mk(dst_ref=slot(me)), mk(dst_ref=slot(peer))))
    return out


def _sibling_fill(land, name):
    def body(land_in, land_out, send_sems, recv_sems):
        xi, yi, ci = lax.axis_index("x"), lax.axis_index("y"), lax.axis_index("c")
        copies = []
        for k, (dx, dy) in enumerate(_CHIP_FLIPS):
            peer = 2 * _flip(xi, dx) + _flip(yi, dy)
            mk = functools.partial(pltpu.make_async_remote_copy, send_sem=send_sems.at[k], recv_sem=recv_sems.at[k],
                                   device_id=(xi, yi, 1 - ci), device_id_type=MESH)
            mine = mk(src_ref=land_in.at[peer, ci], dst_ref=land_out.at[peer, ci])
            theirs = mk(src_ref=land_in.at[peer, 1 - ci], dst_ref=land_out.at[peer, 1 - ci])
            mine.start()
            copies.append((mine, theirs))
        for mine, theirs in copies:
            mine.wait_send()
            theirs.wait_recv()

    return pl.pallas_call(
        body, name=name, out_shape=jax.ShapeDtypeStruct(land.shape, land.dtype),
        in_specs=[pl.BlockSpec(memory_space=pl.ANY)], out_specs=pl.BlockSpec(memory_space=pl.ANY),
        scratch_shapes=[pltpu.SemaphoreType.DMA((3,)), pltpu.SemaphoreType.DMA((3,))],
        input_output_aliases={0: 0},
    )(land)


def _gather_start(land, name, halves=False):
    def body(land_ref, send_sems, recv_sems, land_thru, token):
        for mine, _ in _chip_copies(None, land_ref, send_sems, recv_sems, False, halves):
            mine.start()
        token[...] = jnp.zeros_like(token)

    return pl.pallas_call(
        body, name=name,
        out_shape=(pltpu.SemaphoreType.DMA((3,)), pltpu.SemaphoreType.DMA((3,)), pltpu.HBM(land.shape, land.dtype),
                   jax.ShapeDtypeStruct((8, 128), F32)),
        in_specs=(_HBM,), out_specs=(_SEM, _SEM, _HBM, pl.BlockSpec(memory_space=pltpu.VMEM)),
        input_output_aliases={0: 2},
        compiler_params=pltpu.CompilerParams(has_side_effects=_EFFECT),
    )(pltpu.with_memory_space_constraint(land, pltpu.HBM))


def _gather_wait(handle, after, name, halves=False):
    send, recv, land_thru = handle

    def body(land_ref, send_sems, recv_sems, after_ref, land_out):
        for mine, theirs in _chip_copies(None, land_ref, send_sems, recv_sems, False, halves):
            mine.wait_send()
            theirs.wait_recv()

    return pl.pallas_call(
        body, name=name, out_shape=pltpu.HBM(land_thru.shape, land_thru.dtype),
        in_specs=(_HBM, _SEM, _SEM, pl.BlockSpec(memory_space=pl.ANY)), out_specs=_HBM,
        input_output_aliases={0: 0},
        compiler_params=pltpu.CompilerParams(has_side_effects=_EFFECT),
    )(land_thru, send, recv, after)


def _scatter_start(src, name, after=None):
    extra = [] if after is None else [after]

    def body(src_ref, land_ref, *rest):
        send_sems, recv_sems, src_thru, land_thru, token = rest[len(extra):]
        for mine, _ in _chip_copies(src_ref, land_ref, send_sems, recv_sems, True):
            mine.start()
        token[...] = jnp.zeros_like(token)

    return pl.pallas_call(
        body, name=name,
        out_shape=(pltpu.SemaphoreType.DMA((3,)), pltpu.SemaphoreType.DMA((3,)), pltpu.HBM(src.shape, src.dtype),
                   pltpu.HBM(src.shape, src.dtype), jax.ShapeDtypeStruct((8, 128), F32)),
        in_specs=(_HBM, _HBM) + (pl.BlockSpec(memory_space=pl.ANY),) * len(extra),
        out_specs=(_SEM, _SEM, _HBM, _HBM, pl.BlockSpec(memory_space=pltpu.VMEM)),
        input_output_aliases={0: 2, 1: 3},
        compiler_params=pltpu.CompilerParams(has_side_effects=_EFFECT),
    )(pltpu.with_memory_space_constraint(src, pltpu.HBM),
      pltpu.with_memory_space_constraint(lax.empty(src.shape, src.dtype), pltpu.HBM), *extra)


def _scatter_wait(handle, after, name):
    send, recv, src_thru, land_thru = handle

    def body(src_ref, land_ref, send_sems, recv_sems, after_ref, src_out, land_out):
        for mine, theirs in _chip_copies(src_ref, land_ref, send_sems, recv_sems, True):
            mine.wait_send()
            theirs.wait_recv()

    return pl.pallas_call(
        body, name=name,
        out_shape=(pltpu.HBM(src_thru.shape, src_thru.dtype), pltpu.HBM(land_thru.shape, land_thru.dtype)),
        in_specs=(_HBM, _HBM, _SEM, _SEM, pl.BlockSpec(memory_space=pl.ANY)), out_specs=(_HBM, _HBM),
        input_output_aliases={0: 0, 1: 1},
        compiler_params=pltpu.CompilerParams(has_side_effects=_EFFECT),
    )(src_thru, land_thru, send, recv, after)


def _sibling_copy(src_ref, land_ref, send_sems, recv_sems):
    xi, yi, ci = lax.axis_index("x"), lax.axis_index("y"), lax.axis_index("c")
    return pltpu.make_async_remote_copy(src_ref=src_ref, dst_ref=land_ref, send_sem=send_sems.at[0],
                                        recv_sem=recv_sems.at[0], device_id=(xi, yi, 1 - ci), device_id_type=MESH)


def _sibling_start(src, name):
    def body(src_ref, land_ref, send_sems, recv_sems, src_thru, land_thru, token):
        _sibling_copy(src_ref, land_ref, send_sems, recv_sems).start()
        token[...] = jnp.zeros_like(token)

    return pl.pallas_call(
        body, name=name,
        out_shape=(pltpu.SemaphoreType.DMA((1,)), pltpu.SemaphoreType.DMA((1,)), pltpu.HBM(src.shape, src.dtype),
                   pltpu.HBM(src.shape, src.dtype), jax.ShapeDtypeStruct((8, 128), F32)),
        in_specs=(_HBM, _HBM), out_specs=(_SEM, _SEM, _HBM, _HBM, pl.BlockSpec(memory_space=pltpu.VMEM)),
        input_output_aliases={0: 2, 1: 3},
        compiler_params=pltpu.CompilerParams(has_side_effects=_EFFECT),
    )(pltpu.with_memory_space_constraint(src, pltpu.HBM),
      pltpu.with_memory_space_constraint(lax.empty(src.shape, src.dtype), pltpu.HBM))


def _sibling_wait(handle, after, name):
    send, recv, src_thru, land_thru = handle

    def body(src_ref, land_ref, send_sems, recv_sems, after_ref, src_out, land_out):
        cp = _sibling_copy(src_ref, land_ref, send_sems, recv_sems)
        cp.wait_send()
        cp.wait_recv()

    return pl.pallas_call(
        body, name=name,
        out_shape=(pltpu.HBM(src_thru.shape, src_thru.dtype), pltpu.HBM(land_thru.shape, land_thru.dtype)),
        in_specs=(_HBM, _HBM, _SEM, _SEM, pl.BlockSpec(memory_space=pl.ANY)), out_specs=(_HBM, _HBM),
        input_output_aliases={0: 0, 1: 1},
        compiler_params=pltpu.CompilerParams(has_side_effects=_EFFECT),
    )(src_thru, land_thru, send, recv, after)


def _pick(dim, pref):
    if dim <= pref:
        return dim
    t = (pref // 128) * 128
    while t >= 128:
        if dim % t == 0:
            return t
        t -= 128
    return dim


def _matmul(a, b, dims, out_dtype, name, tm=512, tn=1024, tk=2048, out_chips=False, b_chips=False):
    a_parts = a if isinstance(a, tuple) else (a,)
    b_parts = b if isinstance(b, tuple) else (b,)
    na, nb = len(a_parts), len(b_parts)
    assert (na == 1 or dims == "nt") and (nb == 1 or dims == "tn")
    b_shape = (b_parts[0].shape[1], N_CHIPS * b_parts[0].shape[2]) if b_chips else b_parts[0].shape
    if dims == "nn":
        (m, kd), (_, n) = a_parts[0].shape, b_shape
    elif dims == "nt":
        (m, kd), (n, _) = a_parts[0].shape, b_shape
        kd = na * kd
    else:
        (kd, m), (_, n) = a_parts[0].shape, b_shape
        n = nb * n
    tm = _pick(m, tm)
    tn = _pick(n // N_CHIPS, tn) if (out_chips or (b_chips and dims == "nn")) else _pick(n // nb, tn)
    tk = _pick(kd // N_CHIPS, tk) if (b_chips and dims == "nt") else _pick(kd // na, tk)
    nk, nj = kd // tk, n // tn
    ka, jb = nk // na, nj // nb
    if out_chips:
        per_chip = n // N_CHIPS // tn
        out_shape = jax.ShapeDtypeStruct((N_CHIPS, m, n // N_CHIPS), out_dtype)
        out_spec = pl.BlockSpec((None, tm, tn), lambda j, i, k: (j // per_chip, i, j % per_chip))
    else:
        out_shape = jax.ShapeDtypeStruct((m, n), out_dtype)
        out_spec = pl.BlockSpec((tm, tn), lambda j, i, k: (i, j))
    def part_of(idx, first, count):
        return jnp.clip(idx - first, 0, count - 1)

    if dims == "nn":
        a_specs = [pl.BlockSpec((tm, tk), lambda j, i, k: (i, k))]
        b_specs = [pl.BlockSpec((tk, tn), lambda j, i, k: (k, j))]
        dn = (((1,), (0,)), ((), ()))
    elif dims == "nt":
        a_specs = [pl.BlockSpec((tm, tk), lambda j, i, k, p=p: (i, part_of(k, p * ka, ka))) for p in range(na)]
        b_specs = [pl.BlockSpec((tn, tk), lambda j, i, k: (j, k))]
        dn = (((1,), (1,)), ((), ()))
    else:
        a_specs = [pl.BlockSpec((tk, tm), lambda j, i, k: (k, i))]
        b_specs = [pl.BlockSpec((tk, tn), lambda j, i, k, p=p: (k, part_of(j, p * jb, jb))) for p in range(nb)]
        dn = (((0,), (0,)), ((), ()))
    if b_chips and dims == "nn":
        nper = n // N_CHIPS // tn
        b_specs = [pl.BlockSpec((None, tk, tn), lambda j, i, k: (j // nper, k, j % nper))]
    elif b_chips:
        kper = kd // N_CHIPS // tk
        b_specs = [pl.BlockSpec((None, tn, tk), lambda j, i, k: (k // kper, j, k % kper))]
    direct = nk == 1 or out_dtype == F32

    def body(*refs):
        a_refs, b_refs, o_ref = refs[:na], refs[na:na + nb], refs[na + nb]
        acc_ref = o_ref if direct else refs[na + nb + 1]
        j, k = pl.program_id(0), pl.program_id(2)

        def step(a_ref, b_ref):
            part = lax.dot_general(a_ref[...].astype(BF16), b_ref[...].astype(BF16), dn, preferred_element_type=F32)
            if nk == 1:
                o_ref[...] = part.astype(o_ref.dtype)
                return

            @pl.when(k == 0)
            def _():
                acc_ref[...] = part

            @pl.when(k > 0)
            def _():
                acc_ref[...] += part

            if not direct:
                @pl.when(k == nk - 1)
                def _():
                    o_ref[...] = acc_ref[...].astype(o_ref.dtype)

        if na == 1 and nb == 1:
            step(a_refs[0], b_refs[0])
        for p in range(na if na > 1 else 0):
            pl.when(jnp.logical_and(k >= p * ka, k < (p + 1) * ka))(functools.partial(step, a_refs[p], b_refs[0]))
        for p in range(nb if nb > 1 else 0):
            pl.when(jnp.logical_and(j >= p * jb, j < (p + 1) * jb))(functools.partial(step, a_refs[0], b_refs[p]))

    return pl.pallas_call(
        body, name=name, out_shape=out_shape,
        grid=(nj, m // tm, nk),
        in_specs=a_specs + b_specs,
        out_specs=out_spec,
        scratch_shapes=[] if direct else [pltpu.VMEM((tm, tn), F32)],
        compiler_params=_params(("parallel", "parallel", "arbitrary")),
    )(*a_parts, *b_parts)


def _rstd(v):
    return lax.rsqrt(jnp.mean(v * v, axis=-1, keepdims=True) + EPS)


def _row(tm):
    return pl.BlockSpec((tm, D_MODEL), lambda i: (i, 0))


_VEC = pl.BlockSpec((1, D_MODEL), lambda i: (0, 0))


def _pre_norm(x, gn, sc, sh, name):
    t = x.shape[0]
    tm = min(TM_ROW, t)

    def body(x_ref, gn_ref, sc_ref, sh_ref, h_ref):
        xv = x_ref[...]
        h_ref[...] = ((xv * _rstd(xv) * gn_ref[...]) * (1.0 + sc_ref[...]) + sh_ref[...]).astype(BF16)

    return pl.pallas_call(
        body, name=name, out_shape=jax.ShapeDtypeStruct((t, D_MODEL), BF16), grid=(t // tm,),
        in_specs=[_row(tm), _VEC, _VEC, _VEC], out_specs=_row(tm),
        compiler_params=_params(("parallel",)),
    )(x, gn, sc, sh)


def _post_pre(x, y, g, gnp, gn, sc, sh, name):
    t = x.shape[0]
    tm = min(TM_ROW, t)

    def body(x_ref, y_ref, g_ref, gnp_ref, gn_ref, sc_ref, sh_ref, x1_ref, h_ref):
        yv = y_ref[...]
        x1 = x_ref[...] + g_ref[...] * (yv * _rstd(yv) * gnp_ref[...])
        x1_ref[...] = x1
        h_ref[...] = ((x1 * _rstd(x1) * gn_ref[...]) * (1.0 + sc_ref[...]) + sh_ref[...]).astype(BF16)

    return pl.pallas_call(
        body, name=name,
        out_shape=(jax.ShapeDtypeStruct((t, D_MODEL), F32), jax.ShapeDtypeStruct((t, D_MODEL), BF16)),
        grid=(t // tm,),
        in_specs=[_row(tm), _row(tm), _VEC, _VEC, _VEC, _VEC, _VEC], out_specs=(_row(tm), _row(tm)),
        compiler_params=_params(("parallel",)),
    )(x, y, g, gnp, gn, sc, sh)


def _post_loss(x, y, g, gnp, tgt, name):
    t = x.shape[0]
    tm = min(TM_ROW, t)

    def body(x_ref, y_ref, g_ref, gnp_ref, t_ref, dx_ref, loss_ref):
        yv = y_ref[...]
        diff = x_ref[...] + g_ref[...] * (yv * _rstd(yv) * gnp_ref[...]) - t_ref[...]
        dx_ref[...] = diff * (1.0 / D_MODEL)
        part = (0.5 / D_MODEL) * jnp.sum(jnp.sum(diff * diff, axis=-1, keepdims=True), axis=0, keepdims=True)

        @pl.when(pl.program_id(0) == 0)
        def _():
            loss_ref[...] = jnp.zeros_like(loss_ref)

        loss_ref[...] += jnp.broadcast_to(part, loss_ref.shape)

    return pl.pallas_call(
        body, name=name,
        out_shape=(jax.ShapeDtypeStruct((t, D_MODEL), F32), jax.ShapeDtypeStruct((8, 128), F32)),
        grid=(t // tm,),
        in_specs=[_row(tm), _row(tm), _VEC, _VEC, _row(tm)],
        out_specs=(_row(tm), pl.BlockSpec((8, 128), lambda i: (0, 0))),
        compiler_params=_params(("arbitrary",)),
    )(x, y, g, gnp, tgt)


def _acc_rows(ref, val):
    @pl.when(pl.program_id(0) == 0)
    def _():
        ref[...] = jnp.zeros_like(ref)

    ref[...] += jnp.sum(val, axis=0, keepdims=True)


def _post_bwd(dxn, y, g, gnp, name):
    t = y.shape[0]
    tm = min(TM_ROW, t)

    def body(dx_ref, y_ref, g_ref, gnp_ref, dy_ref, dg_ref, dgn_ref):
        yv, dxv = y_ref[...], dx_ref[...]
        r = _rstd(yv)
        yh = yv * r
        _acc_rows(dg_ref, dxv * (yh * gnp_ref[...]))
        dn = dxv * g_ref[...]
        _acc_rows(dgn_ref, dn * yh)
        dyh = dn * gnp_ref[...]
        dy_ref[...] = (r * (dyh - yh * jnp.mean(dyh * yh, axis=-1, keepdims=True))).astype(BF16)

    return pl.pallas_call(
        body, name=name,
        out_shape=(jax.ShapeDtypeStruct((t, D_MODEL), BF16), jax.ShapeDtypeStruct((1, D_MODEL), F32),
                   jax.ShapeDtypeStruct((1, D_MODEL), F32)),
        grid=(t // tm,),
        in_specs=[_row(tm), _row(tm), _VEC, _VEC], out_specs=(_row(tm), _VEC, _VEC),
        compiler_params=_params(("arbitrary",)),
    )(dxn, y, g, gnp)


def _pre_bwd(dh, xin, dres, gn, sc, name):
    t = xin.shape[0]
    tm = min(TM_ROW, t)

    def body(dh_ref, x_ref, dres_ref, gn_ref, sc_ref, dx_ref, dsh_ref, dsc_ref, dgn_ref):
        xv, dhv = x_ref[...], dh_ref[...]
        r = _rstd(xv)
        xh = xv * r
        _acc_rows(dsh_ref, dhv)
        _acc_rows(dsc_ref, dhv * (xh * gn_ref[...]))
        dn = dhv * (1.0 + sc_ref[...])
        _acc_rows(dgn_ref, dn * xh)
        dxh = dn * gn_ref[...]
        dx_ref[...] = dres_ref[...] + r * (dxh - xh * jnp.mean(dxh * xh, axis=-1, keepdims=True))

    vec = jax.ShapeDtypeStruct((1, D_MODEL), F32)
    return pl.pallas_call(
        body, name=name, out_shape=(jax.ShapeDtypeStruct((t, D_MODEL), F32), vec, vec, vec),
        grid=(t // tm,),
        in_specs=[_row(tm), _row(tm), _row(tm), _VEC, _VEC], out_specs=(_row(tm), _VEC, _VEC, _VEC),
        compiler_params=_params(("arbitrary",)),
    )(dh, xin, dres, gn, sc)


def _fix_rows(v8, rows):
    idx = lax.broadcasted_iota(jnp.int32, v8.shape, 0)
    for j, val in rows:
        v8 = jnp.where(idx == j, jnp.broadcast_to(val, v8.shape), v8)
    return v8


def _shift_down(v, halo, s):
    hr, tm = halo.shape[0], v.shape[0]
    out = pltpu.roll(v, s, 0)
    if tm == 8:
        return _fix_rows(out, [(j, halo[hr - s + j:hr - s + j + 1, :]) for j in range(s)])
    head = _fix_rows(out[0:8, :], [(j, halo[hr - s + j:hr - s + j + 1, :]) for j in range(s)])
    return jnp.concatenate([head, out[8:, :]], axis=0)


def _shift_up(v, halo, s):
    tm = v.shape[0]
    out = pltpu.roll(v, tm - s, 0)
    tail = _fix_rows(out[tm - 8:, :], [(8 - s + j, halo[j:j + 1, :]) for j in range(s)])
    return jnp.concatenate([out[:tm - 8, :], tail], axis=0)


def _tile_specs(tm, cw, off, nrow, hr=8):
    ob = off // cw
    per = tm // hr
    main = pl.BlockSpec((tm, cw), lambda j, i: (i, ob + j))
    prev = pl.BlockSpec((hr, cw), lambda j, i: (jnp.maximum(i * per - 1, 0), ob + j))
    nxt = pl.BlockSpec((hr, cw), lambda j, i: (jnp.minimum((i + 1) * per, nrow * per - 1), ob + j))
    return main, prev, nxt


def _conv_fwd(p, w, name):
    t = p.shape[0]
    tm, cw = min(TM_EW, t), CW_EW
    nrow = t // tm
    cb_s, _, _ = _tile_specs(tm, cw, OFF_CB, nrow, 16)
    cc_s, cc_p, _ = _tile_specs(tm, cw, OFF_CC, nrow, 16)
    cx_s, cx_p, _ = _tile_specs(tm, cw, OFF_CX, nrow, 16)

    def body(cb_ref, cc_ref, ccp_ref, cx_ref, cxp_ref, w_ref, z_ref):
        u = _ld(cc_ref) * _ld(cx_ref)
        uh = _ld(ccp_ref) * _ld(cxp_ref) * _unless(pl.program_id(1) == 0)
        wv = w_ref[...]
        conv = wv[2:3, :] * u + wv[1:2, :] * _shift_down(u, uh, 1) + wv[0:1, :] * _shift_down(u, uh, 2)
        z_ref[...] = (_ld(cb_ref) * conv).astype(BF16)

    return pl.pallas_call(
        body, name=name, out_shape=jax.ShapeDtypeStruct((t, CONV_WIDTH), BF16),
        grid=(CONV_WIDTH // cw, nrow),
        in_specs=[cb_s, cc_s, cc_p, cx_s, cx_p, pl.BlockSpec((8, cw), lambda j, i: (0, j))],
        out_specs=pl.BlockSpec((tm, cw), lambda j, i: (i, j)),
        compiler_params=_params(("parallel", "arbitrary")),
    )(p, p, p, p, p, w)


def _acc_w(ref, vals):
    @pl.when(pl.program_id(1) == 0)
    def _():
        ref[...] = jnp.zeros_like(ref)

    for j, v in enumerate(vals):
        ref[j:j + 1, :] += jnp.sum(v, axis=0, keepdims=True)


def _conv_bwd(dz, p, w, dp, name):
    t = p.shape[0]
    tm, cw = min(TM_EW // 2, t), CONV_WIDTH
    nrow = t // tm
    dz_s, _, dz_n = _tile_specs(tm, cw, 0, nrow)
    cb_s, _, cb_n = _tile_specs(tm, cw, OFF_CB, nrow, 16)
    cc_s, cc_p, _ = _tile_specs(tm, cw, OFF_CC, nrow, 16)
    cx_s, cx_p, _ = _tile_specs(tm, cw, OFF_CX, nrow, 16)

    def body(dz_ref, dzn_ref, cb_ref, cbn_ref, cc_ref, ccp_ref, cx_ref, cxp_ref, w_ref, dp_in, dp_ref, dw_ref):
        dcb_ref = dp_ref.at[:, 0:cw]
        dcc_ref = dp_ref.at[:, cw:2 * cw]
        dcx_ref = dp_ref.at[:, 2 * cw:3 * cw]
        i = pl.program_id(1)
        ccv, cxv, dzv = _ld(cc_ref), _ld(cx_ref), dz_ref[...]
        u = ccv * cxv
        uh = _ld(ccp_ref) * _ld(cxp_ref) * _unless(i == 0)
        wv = w_ref[...]
        u1, u2 = _shift_down(u, uh, 1), _shift_down(u, uh, 2)
        conv = wv[2:3, :] * u + wv[1:2, :] * u1 + wv[0:1, :] * u2
        dcb_ref[...] = (dzv * conv).astype(BF16)
        dconv = dzv * _ld(cb_ref)
        dch = dzn_ref[...] * _ld(cbn_ref)[0:8, :] * _unless(i == nrow - 1)
        du = wv[2:3, :] * dconv + wv[1:2, :] * _shift_up(dconv, dch, 1) + wv[0:1, :] * _shift_up(dconv, dch, 2)
        dcc_ref[...] = (du * cxv).astype(BF16)
        dcx_ref[...] = (du * ccv).astype(BF16)
        _acc_w(dw_ref, (dconv * u2, dconv * u1, dconv * u))

    w_s = pl.BlockSpec((8, cw), lambda j, i: (0, j))
    return pl.pallas_call(
        body, name=name, out_shape=(_dp_shape(t), jax.ShapeDtypeStruct((8, CONV_WIDTH), F32)),
        grid=(1, nrow),
        in_specs=[dz_s, dz_n, cb_s, cb_n, cc_s, cc_p, cx_s, cx_p, w_s, _ANY],
        out_specs=(pl.BlockSpec((tm, 3 * cw), lambda j, i: (i, OFF_CB // (3 * cw))), w_s),
        input_output_aliases={9: 0},
        compiler_params=_params(("parallel", "arbitrary")),
    )(dz, dz, p, p, p, p, p, p, w, dp)


def _ffn_fwd(u, w, name):
    t = u.shape[0]
    tm, cw = min(TM_EW, t), CW_EW
    nrow = t // tm
    g_s, g_p, _ = _tile_specs(tm, cw, 0, nrow, 16)
    u_s, _, _ = _tile_specs(tm, cw, D_FF, nrow, 16)

    def body(g_ref, gp_ref, u_ref, w_ref, f_ref):
        gv = _ld(g_ref)
        gh = _ld(gp_ref) * _unless(pl.program_id(1) == 0)
        wv = w_ref[...]
        gc = wv[2:3, :] * gv + wv[1:2, :] * _shift_down(gv, gh, 1) + wv[0:1, :] * _shift_down(gv, gh, 2)
        f_ref[...] = (_gelu_and_grad(gc)[0] * _ld(u_ref)).astype(BF16)

    return pl.pallas_call(
        body, name=name, out_shape=jax.ShapeDtypeStruct((t, D_FF), BF16),
        grid=(D_FF // cw, nrow),
        in_specs=[g_s, g_p, u_s, pl.BlockSpec((8, cw), lambda j, i: (0, j))],
        out_specs=pl.BlockSpec((tm, cw), lambda j, i: (i, j)),
        compiler_params=_params(("parallel", "arbitrary")),
    )(u, u, u, w)


def _ffn_bwd(df, u, w, name):
    t = u.shape[0]
    tm, cw = min(TM_EW, t), CW_EW
    nrow = t // tm
    df_s, _, df_n = _tile_specs(tm, cw, 0, nrow)
    g_s, g_p, g_n = _tile_specs(tm, cw, 0, nrow, 16)
    u_s, _, u_n = _tile_specs(tm, cw, D_FF, nrow, 16)

    def body(df_ref, dfn_ref, g_ref, gp_ref, gn_ref, u_ref, un_ref, w_ref, dg_ref, du_ref, dw_ref):
        i = pl.program_id(1)
        gv, dfv, uv = _ld(g_ref), df_ref[...], _ld(u_ref)
        gh = _ld(gp_ref) * _unless(i == 0)
        wv = w_ref[...]
        g1, g2 = _shift_down(gv, gh, 1), _shift_down(gv, gh, 2)
        gc = wv[2:3, :] * gv + wv[1:2, :] * g1 + wv[0:1, :] * g2
        act, act_grad = _gelu_and_grad(gc)
        du_ref[...] = (dfv * act).astype(BF16)
        dgc = dfv * uv * act_grad
        gnv = _ld(gn_ref)[0:8, :]
        gtail = gv[tm - 8:tm, :]
        gcn = (wv[2:3, :] * gnv + wv[1:2, :] * _shift_down(gnv, gtail, 1) + wv[0:1, :] * _shift_down(gnv, gtail, 2))
        dgcn = dfn_ref[...] * _ld(un_ref)[0:8, :] * _gelu_and_grad(gcn)[1] * _unless(i == nrow - 1)
        dg = wv[2:3, :] * dgc + wv[1:2, :] * _shift_up(dgc, dgcn, 1) + wv[0:1, :] * _shift_up(dgc, dgcn, 2)
        dg_ref[...] = dg.astype(BF16)
        _acc_w(dw_ref, (dgc * g2, dgc * g1, dgc * gv))

    o_s = pl.BlockSpec((tm, cw), lambda j, i: (i, j))
    o_sh = jax.ShapeDtypeStruct((t, D_FF), BF16)
    w_s = pl.BlockSpec((8, cw), lambda j, i: (0, j))
    return pl.pallas_call(
        body, name=name, out_shape=(o_sh, o_sh, jax.ShapeDtypeStruct((8, D_FF), F32)),
        grid=(D_FF // cw, nrow),
        in_specs=[df_s, df_n, g_s, g_p, g_n, u_s, u_n, w_s],
        out_specs=(o_s, o_s, w_s),
        compiler_params=_params(("parallel", "arbitrary")),
    )(df, df, u, u, u, u, u, w)


def _merge_fwd(ya, yb, p, name):
    t = ya.shape[0]
    tm, cw = min(TM_EW, t), CW_EW
    y_s = pl.BlockSpec((tm, cw), lambda i, j: (i, j))

    def body(ya_ref, yb_ref, ga_ref, gb_ref, m_ref):
        m_ref[...] = (_sigmoid(_ld(ga_ref)) * ya_ref[...] + _sigmoid(_ld(gb_ref)) * yb_ref[...]).astype(BF16)

    return pl.pallas_call(
        body, name=name, out_shape=jax.ShapeDtypeStruct((t, D_MODEL), BF16),
        grid=(t // tm, D_MODEL // cw),
        in_specs=[y_s, y_s, pl.BlockSpec((tm, cw), lambda i, j: (i, OFF_GA // cw + j)),
                  pl.BlockSpec((tm, cw), lambda i, j: (i, OFF_GB // cw + j))],
        out_specs=y_s, compiler_params=_params(("parallel", "parallel")),
    )(ya, yb, p, p)


_ANY = pl.BlockSpec(memory_space=pl.ANY)


def _dp_shape(t):
    return jax.ShapeDtypeStruct((t, N_IN_PAD), BF16)


def _merge_bwd(dm, y, p, gate_off, dp, name):
    t = y.shape[0]
    tm, cw = min(TM_EW, t), CW_EW
    y_s = pl.BlockSpec((tm, cw), lambda i, j: (i, j))
    g_s = pl.BlockSpec((tm, cw), lambda i, j: (i, gate_off // cw + j))

    def body(dm_ref, y_ref, g_ref, *rest):
        dy_ref, dp_ref = rest[-2:]
        dmv = dm_ref[...]
        sg = _sigmoid(_ld(g_ref))
        dy_ref[...] = (dmv * sg).astype(BF16)
        dp_ref[...] = (dmv * y_ref[...] * sg * (1.0 - sg)).astype(BF16)

    extra = [] if dp is None else [dp]
    return pl.pallas_call(
        body, name=name, out_shape=(jax.ShapeDtypeStruct((t, D_MODEL), BF16), _dp_shape(t)),
        grid=(t // tm, D_MODEL // cw),
        in_specs=[y_s, y_s, g_s] + [_ANY] * len(extra),
        out_specs=(y_s, g_s), input_output_aliases={} if dp is None else {3: 1},
        compiler_params=_params(("parallel", "parallel")),
    )(dm, y, p, *extra)


def _tri(lower):
    r = lax.broadcasted_iota(jnp.int32, (CHUNK, CHUNK), 0)
    c = lax.broadcasted_iota(jnp.int32, (CHUNK, CHUNK), 1)
    return ((c <= r) if lower else (c >= r)).astype(F32)


def _eye_mask():
    r = lax.broadcasted_iota(jnp.int32, (GLA_DK, GLA_DK), 0)
    c = lax.broadcasted_iota(jnp.int32, (GLA_DK, GLA_DK), 1)
    return r == c


def _row_to_col(v):
    return jnp.sum(jnp.where(_eye_mask(), jnp.broadcast_to(v, (GLA_DK, GLA_DK)), 0.0), axis=1, keepdims=True)


def _col_to_row(v):
    return jnp.sum(jnp.where(_eye_mask(), jnp.broadcast_to(v, (GLA_DK, GLA_DK)), 0.0), axis=0, keepdims=True)


def _dot(a, b, dn):
    return lax.dot_general(a.astype(BF16), b.astype(BF16), (dn, ((), ())), preferred_element_type=F32)


_NN = ((1,), (0,))
_NT = ((1,), (1,))
_TN = ((0,), (0,))


def _gate_logits(lr_ref, wa_ref, ba_ref):
    return _dot(lr_ref[...], wa_ref[...], _NN) + ba_ref[...]


def _chunk_decay(la, tri):
    cum = lax.dot_general(tri, la, ((_NN), ((), ())), precision=lax.Precision.HIGHEST, preferred_element_type=F32)
    e = cum[CHUNK - 1:CHUNK, :]
    return cum, e, jnp.exp(e - cum)


def _gla_fwd(p, wa, ba, name):
    t = p.shape[0]
    rows = min(GLA_ROWS, t)
    cb = rows // CHUNK
    nc = t // CHUNK
    scale = GLA_DK ** -0.5

    def body(q_ref, k_ref, v_ref, lr_ref, wa_ref, ba_ref, o_ref, st_ref, s_scr):
        @pl.when(pl.program_id(0) == 0)
        def _():
            s_scr[...] = jnp.zeros_like(s_scr)

        la_all = _log_sigmoid(_gate_logits(lr_ref, wa_ref, ba_ref)) * (1.0 / GLA_TAU)
        tri = _tri(True)
        for ch in range(cb):
            rs = slice(ch * CHUNK, (ch + 1) * CHUNK)
            for h in range(GLA_HEADS):
                ks = slice(h * GLA_DK, (h + 1) * GLA_DK)
                vs = slice(h * GLA_DV, (h + 1) * GLA_DV)
                _, e, w = _chunk_decay(la_all[rs, ks], tri)
                kd = k_ref[rs, ks].astype(F32) * w
                s_new = _row_to_col(jnp.exp(e)) * s_scr[ks, :] + _dot(kd, v_ref[rs, vs], _TN)
                s_scr[ks, :] = s_new
                st_ref[ch, ks, :] = s_new
                o_ref[rs, vs] = _dot(q_ref[rs, ks].astype(F32) * scale, s_new, _NN)

    return pl.pallas_call(
        body, name=name,
        out_shape=(jax.ShapeDtypeStruct((t, GLA_V), F32), jax.ShapeDtypeStruct((nc, GLA_QK, GLA_DV), F32)),
        grid=(t // rows,),
        in_specs=[pl.BlockSpec((rows, GLA_QK), lambda i: (i, OFF_Q // GLA_QK)),
                  pl.BlockSpec((rows, GLA_QK), lambda i: (i, OFF_K // GLA_QK)),
                  pl.BlockSpec((rows, GLA_V), lambda i: (i, OFF_V // GLA_V)),
                  pl.BlockSpec((rows, LR_PAD), lambda i: (i, OFF_LR // LR_PAD)),
                  pl.BlockSpec((LR_PAD, GLA_QK), lambda i: (0, 0)),
                  pl.BlockSpec((1, GLA_QK), lambda i: (0, 0))],
        out_specs=(pl.BlockSpec((rows, GLA_V), lambda i: (i, 0)),
                   pl.BlockSpec((cb, GLA_QK, GLA_DV), lambda i: (i, 0, 0))),
        scratch_shapes=[pltpu.VMEM((GLA_QK, GLA_DV), F32)],
        compiler_params=_params(("arbitrary",)),
    )(p, p, p, p, wa, ba)


def _gla_bwd(do, p, st, wa, ba, dp, name):
    t = p.shape[0]
    rows = min(GLA_ROWS, t)
    cb = rows // CHUNK
    nb = t // rows
    scale = GLA_DK ** -0.5

    def rev(i):
        return nb - 1 - i

    def body(do_ref, q_ref, k_ref, v_ref, lr_ref, st_ref, stp_ref, wa_ref, ba_ref, dp_in,
             dp_ref, dlr_ref, dwa_ref, dba_ref, ds_scr, dz_scr):
        dq_ref = dp_ref.at[:, OFF_Q:OFF_Q + GLA_QK]
        dk_ref = dp_ref.at[:, OFF_K:OFF_K + GLA_QK]
        dv_ref = dp_ref.at[:, OFF_V:OFF_V + GLA_V]
        i = pl.program_id(0)

        @pl.when(i == 0)
        def _():
            ds_scr[...] = jnp.zeros_like(ds_scr)
            dwa_ref[...] = jnp.zeros_like(dwa_ref)
            dba_ref[...] = jnp.zeros_like(dba_ref)

        z_all = _gate_logits(lr_ref, wa_ref, ba_ref)
        la_all = _log_sigmoid(z_all) * (1.0 / GLA_TAU)
        tri, triu = _tri(True), _tri(False)
        last_row = lax.broadcasted_iota(jnp.int32, (CHUNK, GLA_DK), 0) == CHUNK - 1
        keep_prev = _unless(i == nb - 1)
        for ch in reversed(range(cb)):
            rs = slice(ch * CHUNK, (ch + 1) * CHUNK)
            for h in range(GLA_HEADS):
                ks = slice(h * GLA_DK, (h + 1) * GLA_DK)
                vs = slice(h * GLA_DV, (h + 1) * GLA_DV)
                _, e, w = _chunk_decay(la_all[rs, ks], tri)
                kd = k_ref[rs, ks].astype(F32) * w
                exp_e = jnp.exp(e)
                s_c = st_ref[ch, ks, :]
                if ch > 0:
                    s_p = st_ref[ch - 1, ks, :]
                else:
                    s_p = stp_ref[0, ks, :] * keep_prev
                do_c = do_ref[rs, vs]
                vv = v_ref[rs, vs]
                ds_tot = ds_scr[ks, :] + _dot(q_ref[rs, ks].astype(F32) * scale, do_c, _TN)
                dq_ref[rs, ks] = (_dot(do_c, s_c, _NT) * scale).astype(BF16)
                dkd = _dot(vv, ds_tot, _NT)
                dv_ref[rs, vs] = _dot(kd, ds_tot, _NN).astype(BF16)
                dexp_col = jnp.sum(ds_tot * s_p, axis=1, keepdims=True)
                ds_scr[ks, :] = _row_to_col(exp_e) * ds_tot
                dk_ref[rs, ks] = (dkd * w).astype(BF16)
                dwt = dkd * kd
                de = jnp.sum(dwt, axis=0, keepdims=True) + _col_to_row(dexp_col) * exp_e
                dcum = jnp.where(last_row, de - dwt, -dwt)
                da = lax.dot_general(triu, dcum, (_NN, ((), ())), precision=lax.Precision.HIGHEST,
                                     preferred_element_type=F32)
                dz_scr[rs, ks] = da * (1.0 / GLA_TAU) * _sigmoid(-z_all[rs, ks])
        dz = dz_scr[...]
        dlr_ref[...] = _dot(dz, wa_ref[...], _NT).astype(BF16)
        dwa_ref[...] += _dot(lr_ref[...], dz, _TN)
        dba_ref[...] += jnp.sum(dz, axis=0, keepdims=True)

    qkv = OFF_V + GLA_V
    return pl.pallas_call(
        body, name=name,
        out_shape=(_dp_shape(t), jax.ShapeDtypeStruct((t, LR_PAD), BF16),
                   jax.ShapeDtypeStruct((LR_PAD, GLA_QK), F32), jax.ShapeDtypeStruct((1, GLA_QK), F32)),
        grid=(nb,),
        in_specs=[pl.BlockSpec((rows, GLA_V), lambda i: (rev(i), 0)),
                  pl.BlockSpec((rows, GLA_QK), lambda i: (rev(i), OFF_Q // GLA_QK)),
                  pl.BlockSpec((rows, GLA_QK), lambda i: (rev(i), OFF_K // GLA_QK)),
                  pl.BlockSpec((rows, GLA_V), lambda i: (rev(i), OFF_V // GLA_V)),
                  pl.BlockSpec((rows, LR_PAD), lambda i: (rev(i), OFF_LR // LR_PAD)),
                  pl.BlockSpec((cb, GLA_QK, GLA_DV), lambda i: (rev(i), 0, 0)),
                  pl.BlockSpec((1, GLA_QK, GLA_DV), lambda i: (jnp.maximum(rev(i) * cb - 1, 0), 0, 0)),
                  pl.BlockSpec((LR_PAD, GLA_QK), lambda i: (0, 0)),
                  pl.BlockSpec((1, GLA_QK), lambda i: (0, 0)), _ANY],
        out_specs=(pl.BlockSpec((rows, qkv), lambda i: (rev(i), 0)),
                   pl.BlockSpec((rows, LR_PAD), lambda i: (rev(i), 0)),
                   pl.BlockSpec((LR_PAD, GLA_QK), lambda i: (0, 0)),
                   pl.BlockSpec((1, GLA_QK), lambda i: (0, 0))),
        input_output_aliases={9: 0},
        scratch_shapes=[pltpu.VMEM((GLA_QK, GLA_DV), F32), pltpu.VMEM((rows, GLA_QK), F32)],
        compiler_params=_params(("arbitrary",)),
    )(do, p, p, p, p, st, st, wa, ba, dp)


def _gla_out_fwd(o, p, gng, name):
    t = o.shape[0]
    tm = min(TM_EW, t)

    def body(o_ref, r_ref, g_ref, z_ref):
        gv = g_ref[...]
        for h in range(GLA_HEADS):
            vs = slice(h * GLA_DV, (h + 1) * GLA_DV)
            ov, rv = o_ref[:, vs], r_ref[:, vs].astype(F32)
            z_ref[:, vs] = ((ov * _rstd(ov) * gv) * (rv * _sigmoid(rv))).astype(BF16)

    return pl.pallas_call(
        body, name=name, out_shape=jax.ShapeDtypeStruct((t, GLA_V), BF16), grid=(t // tm,),
        in_specs=[pl.BlockSpec((tm, GLA_V), lambda i: (i, 0)),
                  pl.BlockSpec((tm, GLA_V), lambda i: (i, OFF_R // GLA_V)),
                  pl.BlockSpec((1, GLA_DV), lambda i: (0, 0))],
        out_specs=pl.BlockSpec((tm, GLA_V), lambda i: (i, 0)),
        compiler_params=_params(("parallel",)),
    )(o, p, gng)


def _gla_out_bwd(dz, o, p, gng, dp, name):
    t = o.shape[0]
    tm = min(TM_EW, t)

    def body(dz_ref, o_ref, r_ref, g_ref, dp_in, do_ref, dr_ref, dg_ref):
        @pl.when(pl.program_id(0) == 0)
        def _():
            dg_ref[...] = jnp.zeros_like(dg_ref)

        gv = g_ref[...]
        for h in range(GLA_HEADS):
            vs = slice(h * GLA_DV, (h + 1) * GLA_DV)
            ov, rv, dzv = o_ref[:, vs], r_ref[:, vs].astype(F32), dz_ref[:, vs]
            rs = _rstd(ov)
            oh = ov * rs
            sg = _sigmoid(rv)
            dr_ref[:, vs] = (dzv * (oh * gv) * (sg * (1.0 + rv * (1.0 - sg)))).astype(BF16)
            don = dzv * (rv * sg)
            dg_ref[...] += jnp.sum(don * oh, axis=0, keepdims=True)
            doh = don * gv
            do_ref[:, vs] = rs * (doh - oh * jnp.mean(doh * oh, axis=-1, keepdims=True))

    row = pl.BlockSpec((tm, GLA_V), lambda i: (i, 0))
    r_s = pl.BlockSpec((tm, GLA_V), lambda i: (i, OFF_R // GLA_V))
    return pl.pallas_call(
        body, name=name,
        out_shape=(jax.ShapeDtypeStruct((t, GLA_V), F32), _dp_shape(t), jax.ShapeDtypeStruct((1, GLA_DV), F32)),
        grid=(t // tm,),
        in_specs=[row, row, r_s, pl.BlockSpec((1, GLA_DV), lambda i: (0, 0)), _ANY],
        out_specs=(row, r_s, pl.BlockSpec((1, GLA_DV), lambda i: (0, 0))),
        input_output_aliases={4: 1},
        compiler_params=_params(("arbitrary",)),
    )(dz, o, p, gng, dp)


def _ada_fwd(c_all, w, b, layer, name):
    n = w.shape[2]
    tn = _pick(n, 512)

    def body(c_ref, w_ref, b_ref, o_ref):
        cv = c_ref[...]
        o_ref[...] = _dot(cv * _sigmoid(cv), w_ref[...], _NN) + b_ref[...]

    return pl.pallas_call(
        body, name=name, out_shape=jax.ShapeDtypeStruct((16, n), F32), grid=(n // tn,),
        in_specs=[pl.BlockSpec((16, D_MODEL), lambda j: (0, 0)),
                  pl.BlockSpec((None, D_MODEL, tn), lambda j: (layer, 0, j)),
                  pl.BlockSpec((1, tn), lambda j: (0, j))],
        out_specs=pl.BlockSpec((16, tn), lambda j: (0, j)),
        compiler_params=_params(("parallel",)),
    )(c_all, w, b)


def _ada_bwd(c_all, dmod, name):
    n = dmod.shape[2]
    tn = _pick(n, 512)

    def body(c_ref, d_ref, o_ref):
        cv = c_ref[...]
        o_ref[...] = _dot(cv * _sigmoid(cv), d_ref[...], _TN)

    return pl.pallas_call(
        body, name=name, out_shape=jax.ShapeDtypeStruct((DEPTH, D_MODEL, n), F32), grid=(DEPTH, n // tn),
        in_specs=[pl.BlockSpec((16, D_MODEL), lambda l, j: (0, 0)),
                  pl.BlockSpec((None, 16, tn), lambda l, j: (l, 0, j))],
        out_specs=pl.BlockSpec((None, D_MODEL, tn), lambda l, j: (l, 0, j)),
        compiler_params=_params(("parallel", "parallel")),
    )(c_all, dmod)


def _rows_tile(nrows, ncols, target_bytes):
    want = max(16, target_bytes // (4 * ncols))
    if nrows <= want:
        return nrows
    t = (want // 16) * 16
    while t >= 16:
        if nrows % t == 0:
            return t
        t -= 16
    return nrows


def _sum_chips(sent, landed, chip, name):
    _, nrows, ncols = sent[0].shape
    tr = _rows_tile(nrows, ncols, 2 << 20)
    nblk = nrows // tr

    def body(chip_ref, *refs):
        own, got, o_ref = refs[:DEPTH], refs[DEPTH:2 * DEPTH], refs[2 * DEPTH]
        me = chip_ref[0]
        for l in range(DEPTH):
            for j in range(N_CHIPS):
                def add(val):
                    if j == 0:
                        o_ref[...] = val.astype(F32)
                    else:
                        o_ref[...] += val.astype(F32)

                @pl.when(jnp.logical_and(pl.program_id(0) == l, me == j))
                def _():
                    add(own[l][...])

                @pl.when(jnp.logical_and(pl.program_id(0) == l, me != j))
                def _():
                    add(got[l][j])

    def rows_of(layer):
        return lambda l, i, chip_ref: jnp.where(l == layer, i, 0)

    own_specs = [pl.BlockSpec((None, tr, ncols), lambda l, i, chip_ref, r=rows_of(k): (chip_ref[0], r(l, i, chip_ref), 0))
                 for k in range(DEPTH)]
    got_specs = [pl.BlockSpec((N_CHIPS, tr, ncols), lambda l, i, chip_ref, r=rows_of(k): (0, r(l, i, chip_ref), 0))
                 for k in range(DEPTH)]
    return pl.pallas_call(
        body, name=name, out_shape=jax.ShapeDtypeStruct((DEPTH, nrows, ncols), F32),
        grid_spec=pltpu.PrefetchScalarGridSpec(
            num_scalar_prefetch=1, grid=(DEPTH, nblk), in_specs=own_specs + got_specs,
            out_specs=pl.BlockSpec((None, tr, ncols), lambda l, i, chip_ref: (l, i, 0))),
        compiler_params=_params(("arbitrary", "arbitrary")),
    )(chip, *sent, *landed)


def _adamw(w, m, v, ga, gb, name, tile=None):
    two = gb is not None
    c1 = 1.0 - ADAM_B1 ** ADAM_STEP
    c2 = 1.0 - ADAM_B2 ** ADAM_STEP

    def body(*refs):
        if two:
            w_ref, m_ref, v_ref, ga_ref, gb_ref, g_ref, d_ref, nm_ref, nv_ref = refs
            g = ga_ref[...] + gb_ref[...]
        else:
            w_ref, m_ref, v_ref, ga_ref, g_ref, d_ref, nm_ref, nv_ref = refs
            g = ga_ref[...]
        g_ref[...] = g
        nm = ADAM_B1 * m_ref[...] + (1.0 - ADAM_B1) * g
        nv = ADAM_B2 * v_ref[...] + (1.0 - ADAM_B2) * (g * g)
        nm_ref[...] = nm
        nv_ref[...] = nv
        d_ref[...] = -ADAM_LR * ((nm / c1) / (jnp.sqrt(nv / c2) + ADAM_EPS) + ADAM_WD * w_ref[...])

    if tile is None:
        nl, nrows, ncols = w.shape
        tr = _rows_tile(nrows, ncols, 1 << 20)
        blk = pl.BlockSpec((None, tr, ncols), lambda l, i: (l, i, 0))
        grid = (nl, nrows // tr)
    else:
        nrows, nl, ncols = w.shape
        rb, cb = tile
        blk = pl.BlockSpec((rb, nl, cb), lambda i, j: (i, 0, j))
        grid = (nrows // rb, ncols // cb)
    sh = jax.ShapeDtypeStruct(w.shape, F32)
    ins = [w, m, v, ga] + ([gb] if two else [])
    return pl.pallas_call(
        body, name=name, out_shape=(sh, sh, sh, sh), grid=grid,
        in_specs=[blk] * len(ins), out_specs=(blk, blk, blk, blk),
        compiler_params=_params(("parallel", "parallel")),
    )(*ins)


def _pad_rows(a, rows):
    return jnp.concatenate([a, jnp.zeros((rows - a.shape[0],) + a.shape[1:], a.dtype)], axis=0)


N_IN_CHIP = N_IN // N_CHIPS
_LR_LO = 3072 - N_IN_CHIP
_LR_HI = _LR_LO + GLA_LOWRANK


def _w_in_from_chips(a):
    return jnp.concatenate([a[0], a[1][:, :_LR_LO], a[1][:, _LR_HI:], a[2], a[3], a[1][:, _LR_LO:_LR_HI],
                            jnp.zeros((a.shape[1], LR_PAD - GLA_LOWRANK), a.dtype)], axis=1)


def _w_in_to_chips(w):
    s2 = 2 * N_IN_CHIP - GLA_LOWRANK
    s3 = s2 + N_IN_CHIP
    c1 = jnp.concatenate([w[:, N_IN_CHIP:3072], w[:, OFF_LR:OFF_LR + GLA_LOWRANK], w[:, 3072:s2]], axis=1)
    return jnp.stack([w[:, :N_IN_CHIP], c1, w[:, s2:s3], w[:, s3:OFF_LR]])


_BIG = ("w_in", "w_og", "w_oc", "w_o", "w_up", "w_dn")
_ROW_SHARDED = ("w_o", "w_dn")
_TWO_LEVEL = ("w_in", "w_up")


def kernel(x, c, w_ada, b_ada, norm_g, w_in, w_a2, b_a2, gla_norm_g, w_out_gla, conv_mix_w, w_out_conv, w_o, w_up, ffn_conv_w, w_down, loss_target, m_w_ada, m_b_ada, m_norm_g, m_w_in, m_w_a2, m_b_a2, m_gla_norm_g, m_w_out_gla, m_conv_mix_w, m_w_out_conv, m_w_o, m_w_up, m_ffn_conv_w, m_w_down, v_w_ada, v_b_ada, v_norm_g, v_w_in, v_w_a2, v_b_a2, v_gla_norm_g, v_w_out_gla, v_conv_mix_w, v_w_out_conv, v_w_o, v_w_up, v_ffn_conv_w, v_w_down):
    xi, yi, ci = lax.axis_index("x"), lax.axis_index("y"), lax.axis_index("c")
    chip = 2 * xi + yi
    dev = 2 * chip + ci
    chip_arr = jnp.reshape(chip, (1,)).astype(jnp.int32)
    xt = x[0]
    tgt = loss_target[0]

    c_all = _allgather8(jnp.broadcast_to(c, (8, D_MODEL)), "gather_c")[0][:, 0, :]
    c16 = _pad_rows(c_all, 16)
    sm_parts = [norm_g.reshape(-1), w_a2.reshape(-1), conv_mix_w.reshape(-1), ffn_conv_w.reshape(-1)]
    sm_sizes = [a.shape[0] for a in sm_parts]
    sm_flat = jnp.concatenate(sm_parts)
    sm_rows = -(-sm_flat.shape[0] // 128)
    sm_rows = -(-sm_rows // 8) * 8
    sm_flat = jnp.concatenate([sm_flat, jnp.zeros((sm_rows * 128 - sm_flat.shape[0],), F32)]).reshape(sm_rows, 128)
    sm_all = _allgather8(sm_flat, "gather_small")[0].reshape(N_DEV, -1)[0::2]
    offs = [0]
    for s in sm_sizes:
        offs.append(offs[-1] + s)

    def small_full(idx, shape):
        a = sm_all[:, offs[idx]:offs[idx + 1]].reshape((N_CHIPS,) + shape)
        a = jnp.moveaxis(a, 0, -2)
        return a.reshape(shape[:-1] + (N_CHIPS * shape[-1],))

    norm_g_f = small_full(0, (DEPTH, 4, 512))
    w_a2_f = small_full(1, (DEPTH, GLA_LOWRANK, 128))
    conv_w_f = small_full(2, (DEPTH, 3, 256))
    ffn_w_f = small_full(3, (DEPTH, 3, 1408))

    b_loc = lax.dynamic_slice(b_ada, (0, chip * 3072), (DEPTH, 3072))
    mod_loc = jnp.concatenate(
        [_ada_fwd(c16, w_ada, b_loc[l:l + 1], l, "ada_fwd")[:8] for l in range(DEPTH)], axis=0)
    mod_all = _allgather8(mod_loc, "gather_mod")[0][0::2]
    mods = []
    for l in range(DEPTH):
        row = lax.dynamic_slice(mod_all, (0, l * 8 + dev, 0), (N_CHIPS, 1, 3072)).reshape(1, 6 * D_MODEL)
        mods.append([row[:, k * D_MODEL:(k + 1) * D_MODEL] for k in range(6)])

    big = dict(w_in=w_in, w_og=w_out_gla, w_oc=w_out_conv, w_o=w_o, w_up=w_up, w_dn=w_down)
    gathers = {}
    tok = 0.0 * (mod_all[0, 0, 0] + sm_all[0, 0])
    for l in range(DEPTH):
        for k in _BIG:
            shard = (big[k][l] + tok).astype(BF16)
            if k in _TWO_LEVEL:
                shard = shard.reshape(2, shard.shape[0] // 2, shard.shape[1])
            land = lax.dynamic_update_slice(lax.empty((N_CHIPS,) + shard.shape, BF16), shard[None],
                                            (chip,) + (0,) * shard.ndim)
            *handle, token = _gather_start(land, "gather_start_%s_%d" % (k, l), k in _TWO_LEVEL)
            gathers[k, l] = tuple(handle)
            tok = token[0, 0]

    def gathered(k, l, after):
        full = _gather_wait(gathers[k, l], after, "gather_wait_%s_%d" % (k, l), k in _TWO_LEVEL)
        if k in _TWO_LEVEL:
            full = _sibling_fill(full, "sibling_fill_%s_%d" % (k, l))
            full = full.reshape(N_CHIPS, 2 * full.shape[2], full.shape[3])
        if k in _ROW_SHARDED:
            return full.reshape(N_CHIPS * full.shape[1], full.shape[2])
        return _w_in_from_chips(full) if k == "w_in" else full

    saved = []
    h = None
    xin = xt
    for l in range(DEPTH):
        sh1, sc1, g1, sh2, sc2, g2 = mods[l]
        gn = [norm_g_f[l, k][None] for k in range(4)]
        wa = _pad_rows(w_a2_f[l], LR_PAD)
        ba = b_a2[l][None]
        gng = gla_norm_g[l][None]
        cw8 = _pad_rows(conv_w_f[l], 8)
        fw8 = _pad_rows(ffn_w_f[l], 8)
        if l == 0:
            h = _pre_norm(xin, gn[0] + tok, sc1, sh1, "pre_norm")
        wi = gathered("w_in", l, h)
        p = _matmul(h, wi, "nn", BF16, "mm_in", tn=1152)
        o, st = _gla_fwd(p, wa, ba, "gla_fwd")
        za = _gla_out_fwd(o, p, gng, "gla_out_fwd")
        zb = _conv_fwd(p, cw8, "conv_fwd")
        wog, woc = gathered("w_og", l, zb), gathered("w_oc", l, zb)
        ya = _matmul(za, wog, "nn", F32, "mm_out_gla", tm=2048, b_chips=True)
        yb = _matmul(zb, woc, "nn", F32, "mm_out_conv", tm=2048, b_chips=True)
        mm = _merge_fwd(ya, yb, p, "merge_fwd")
        wo = gathered("w_o", l, mm)
        y = _matmul(mm, wo, "nn", F32, "mm_o")
        x1, h2 = _post_pre(xin, y, g1, gn[1], gn[2], sc2, sh2, "post_pre")
        wup = gathered("w_up", l, h2)
        u = _matmul(h2, wup, "nn", BF16, "mm_up", tn=1408, b_chips=True)
        f = _ffn_fwd(u, fw8, "ffn_fwd")
        wdn = gathered("w_dn", l, f)
        y2 = _matmul(f, wdn, "nn", F32, "mm_down", tm=1024, tn=2048, tk=1408)
        saved.append(dict(xin=xin, h=h, p=p, o=o, st=st, za=za, zb=zb, ya=ya, yb=yb, mm=mm, y=y, x1=x1, h2=h2,
                          u=u, f=f, y2=y2, wi=wi, wog=wog, woc=woc, wo=wo, wup=wup, wdn=wdn, wa=wa, ba=ba,
                          gng=gng, cw8=cw8, fw8=fw8, gn=gn, mod=mods[l]))
        if l + 1 < DEPTH:
            nsh1, nsc1 = mods[l + 1][0], mods[l + 1][1]
            xin, h = _post_pre(x1, y2, g2, gn[3], norm_g_f[l + 1, 0][None], nsc1, nsh1, "post_pre")
        else:
            dx, loss_tile = _post_loss(x1, y2, g2, gn[3], tgt, "post_loss")
    loss = lax.psum(loss_tile[0, 0], ("x", "y", "c"))

    scatters = {}

    def scatter(k, l, dw, after=None):
        if k in _ROW_SHARDED:
            send = dw.reshape(N_CHIPS, dw.shape[0] // N_CHIPS, dw.shape[1])
        else:
            send = _w_in_to_chips(dw) if k == "w_in" else dw
        *handle, token = _scatter_start(send, "scatter_start_%s_%d" % (k, l), after)
        scatters[k, l] = tuple(handle)
        return token[0, 0]

    sm = {k: [None] * DEPTH for k in ("dmod", "norm_g", "w_a2", "b_a2", "gng", "conv_w", "ffn_w")}
    for l in reversed(range(DEPTH)):
        s = saved[l]
        sh1, sc1, g1, sh2, sc2, g2 = s["mod"]
        gn = s["gn"]
        dy2, dg2, dgn3 = _post_bwd(dx, s["y2"], g2, gn[3], "post_bwd")
        tk = scatter("w_dn", l, _matmul(s["f"], dy2, "tn", BF16, "mm_down_dw", tm=512, tn=1024, tk=4096))
        df = _matmul(dy2, s["wdn"], "nt", F32, "mm_down_dx", tn=1408)
        dgate, dup, dfw = _ffn_bwd(df, s["u"], s["fw8"] + tk, "ffn_bwd")
        du = (dgate, dup)
        tk = scatter("w_up", l, _matmul(s["h2"], du, "tn", BF16, "mm_up_dw", tm=1024, tn=1408, tk=2048, out_chips=True))
        dh2 = _matmul(du, s["wup"], "nt", F32, "mm_up_dx", tm=1024, tn=2048, tk=1408, b_chips=True)
        dx1, dsh2, dsc2, dgn2 = _pre_bwd(dh2, s["x1"], dx, gn[2] + tk, sc2, "pre_bwd")
        dy, dg1, dgn1 = _post_bwd(dx1, s["y"], g1, gn[1], "post_bwd")
        tk = scatter("w_o", l, _matmul(s["mm"], dy, "tn", BF16, "mm_o_dw", tk=4096))
        dm = _matmul(dy, s["wo"], "nt", F32, "mm_o_dx")
        dya, dp = _merge_bwd(dm, s["ya"], s["p"], OFF_GA, None, "merge_bwd_a")
        dyb, dp = _merge_bwd(dm, s["yb"], s["p"], OFF_GB, dp, "merge_bwd_b")
        tk = tk + scatter("w_og", l, _matmul(s["za"], dya, "tn", BF16, "mm_out_gla_dw", tk=4096, out_chips=True))
        dza = _matmul(dya, s["wog"], "nt", F32, "mm_out_gla_dx", tm=2048, b_chips=True)
        do, dp, dgng = _gla_out_bwd(dza, s["o"], s["p"], s["gng"] + tk, dp, "gla_out_bwd")
        tk = scatter("w_oc", l, _matmul(s["zb"], dyb, "tn", BF16, "mm_out_conv_dw", tk=4096, out_chips=True))
        dzb = _matmul(dyb, s["woc"], "nt", F32, "mm_out_conv_dx", tm=2048, b_chips=True)
        dp, dcw = _conv_bwd(dzb, s["p"], s["cw8"] + tk, dp, "conv_bwd")
        dp, dlr, dwa, dba = _gla_bwd(do, s["p"], s["st"], s["wa"], s["ba"], dp, "gla_bwd")
        dp = lax.dynamic_update_slice(dp, dlr, (0, OFF_LR))
        dw_in = _matmul(s["h"], dp, "tn", BF16, "mm_in_dw", tm=512, tn=1152, tk=4096)
        tk = scatter("w_in", l, dw_in) if l > 0 else 0.0
        dh = _matmul(dp, s["wi"], "nt", F32, "mm_in_dx", tm=1024, tn=2048, tk=1152)
        dx, dsh1, dsc1, dgn0 = _pre_bwd(dh, s["xin"], dx1, gn[0] + tk, sc1, "pre_bwd")
        sm["dmod"][l] = jnp.concatenate([dsh1, dsc1, dg1, dsh2, dsc2, dg2], axis=1)[0]
        sm["norm_g"][l] = jnp.concatenate([dgn0, dgn1, dgn2, dgn3], axis=0)
        sm["w_a2"][l] = dwa[:GLA_LOWRANK]
        sm["b_a2"][l] = dba[0]
        sm["gng"][l] = dgng[0]
        sm["conv_w"][l] = dcw[:3]
        sm["ffn_w"][l] = dfw[:3]
    grad_x = dx[None]

    names = ("dmod", "norm_g", "w_a2", "b_a2", "gng", "conv_w", "ffn_w")
    parts = [jnp.stack(sm[k]).reshape(-1) for k in names]
    shapes = [jnp.stack(sm[k]).shape for k in names]
    sizes = [a.shape[0] for a in parts]
    flat = jnp.concatenate(parts)
    rows = -(-flat.shape[0] // 1024) * 8
    flat = jnp.concatenate([flat, jnp.zeros((rows * 128 - flat.shape[0],), F32)]).reshape(rows, 128)
    gath, tot = _allgather8(flat, "reduce_small")
    tk = scatter("w_in", 0, dw_in, after=tot)
    c16 = c16 + tk
    po = [0]
    for s_ in sizes:
        po.append(po[-1] + s_)
    tot = tot.reshape(-1)
    tot_of = {k: tot[po[i]:po[i + 1]].reshape(shapes[i]) for i, k in enumerate(names)}
    dmod_all = gath.reshape(N_DEV, -1)[:, po[0]:po[1]].reshape(N_DEV, DEPTH, 6 * D_MODEL)

    def chip_cols(a, width):
        return lax.dynamic_slice_in_dim(a, chip * width, width, axis=a.ndim - 1)

    dml = jnp.transpose(chip_cols(dmod_all, 3072), (1, 0, 2))
    dml = jnp.concatenate([dml, jnp.zeros_like(dml)], axis=1)
    g_w_ada = _ada_bwd(c16, dml, "ada_bwd")

    def upd(w, m, v, ga, gb, name):
        sh = w.shape
        as3 = sh if len(sh) == 3 else (1,) + sh
        outs = _adamw(w.reshape(as3), m.reshape(as3), v.reshape(as3), ga.reshape(as3),
                      None if gb is None else gb.reshape(as3), name)
        return [a.reshape(sh) for a in outs]

    res = {}
    res["w_ada"] = upd(w_ada, m_w_ada, v_w_ada, g_w_ada, None, "adamw")
    res["b_ada"] = upd(b_ada, m_b_ada, v_b_ada, tot_of["dmod"], None, "adamw")
    res["norm_g"] = upd(norm_g, m_norm_g, v_norm_g, chip_cols(tot_of["norm_g"], 512), None, "adamw")
    res["w_a2"] = upd(w_a2, m_w_a2, v_w_a2, chip_cols(tot_of["w_a2"], 128), None, "adamw")
    res["b_a2"] = upd(b_a2, m_b_a2, v_b_a2, tot_of["b_a2"], None, "adamw")
    res["gla_norm_g"] = upd(gla_norm_g, m_gla_norm_g, v_gla_norm_g, tot_of["gng"], None, "adamw")
    res["conv_mix_w"] = upd(conv_mix_w, m_conv_mix_w, v_conv_mix_w, chip_cols(tot_of["conv_w"], 256), None, "adamw")
    res["ffn_conv_w"] = upd(ffn_conv_w, m_ffn_conv_w, v_ffn_conv_w, chip_cols(tot_of["ffn_w"], 1408), None, "adamw")

    full_name = dict(w_in="w_in", w_og="w_out_gla", w_oc="w_out_conv", w_o="w_o", w_up="w_up", w_dn="w_down")
    state = dict(w_in=(w_in, m_w_in, v_w_in), w_og=(w_out_gla, m_w_out_gla, v_w_out_gla),
                 w_oc=(w_out_conv, m_w_out_conv, v_w_out_conv), w_o=(w_o, m_w_o, v_w_o),
                 w_up=(w_up, m_w_up, v_w_up), w_dn=(w_down, m_w_down, v_w_down))
    def finish(k, handle, after):
        plane, other = _sibling_wait(handle, after, "sibling_wait_" + k)
        if k == "w_in":
            outs = _adamw(*[jnp.transpose(a, (2, 0, 1)) for a in state[k]], plane, other, "adamw_w_in",
                          tile=(N_IN_CHIP // 4, D_MODEL // 8))
            res[full_name[k]] = [jnp.transpose(a, (1, 2, 0)) for a in outs]
        else:
            res[full_name[k]] = upd(*state[k], plane, other, "adamw")

    after = res["w_ada"][3]
    pending = None
    for k in ("w_dn", "w_up", "w_o", "w_og", "w_oc", "w_in"):
        done = [_scatter_wait(scatters[k, l], after, "scatter_wait_%s_%d" % (k, l)) for l in range(DEPTH)]
        plane = _sum_chips([d[0] for d in done], [d[1] for d in done], chip_arr, "sum_chips")
        if k == "w_in":
            plane = jnp.transpose(plane, (2, 0, 1))
        *handle, after = _sibling_start(plane, "sibling_start_" + k)
        if pending is not None:
            finish(*pending, plane)
        pending = (k, tuple(handle))
    finish(*pending, after)
    order = ("w_ada", "b_ada", "norm_g", "w_in", "w_a2", "b_a2", "gla_norm_g", "w_out_gla", "conv_mix_w",
             "w_out_conv", "w_o", "w_up", "ffn_conv_w", "w_down")
    return (loss, grad_x, *[res[k][0] for k in order], *[res[k][1] for k in order],
            *[res[k][2] for k in order], *[res[k][3] for k in order])
```

```python
import functools
import math

import jax
import jax.numpy as jnp
from jax import lax
from jax.experimental import pallas as pl
from jax.experimental.pallas import tpu as pltpu

F32 = jnp.float32
BF16 = jnp.bfloat16
MESH = pl.DeviceIdType.MESH

D_MODEL = 2048
DEPTH = 2
CHUNK = 64
GLA_HEADS = 4
GLA_DK = 128
GLA_DV = 256
GLA_QK = GLA_HEADS * GLA_DK
GLA_V = GLA_HEADS * GLA_DV
GLA_LOWRANK = 16
GLA_TAU = 16.0
CONV_WIDTH = 1024
D_FF = 5632
EPS = 1e-6
N_IN = 10256
LR_PAD = 128
N_IN_PAD = N_IN - GLA_LOWRANK + LR_PAD
OFF_Q, OFF_K, OFF_V, OFF_R = 0, 512, 1024, 2048
OFF_CB, OFF_CC, OFF_CX, OFF_GA, OFF_GB, OFF_LR = 3072, 4096, 5120, 6144, 8192, 10240

ADAM_LR = 0.001
ADAM_B1 = 0.9
ADAM_B2 = 0.999
ADAM_EPS = 1e-08
ADAM_WD = 0.01
ADAM_STEP = 10

N_CHIPS = 4
N_DEV = 8
VMEM_LIMIT = 56 * 1024 * 1024
TM_ROW = 256
TM_EW = 512
CW_EW = 512
GLA_ROWS = 256


def _params(sem=None):
    return pltpu.CompilerParams(dimension_semantics=sem, vmem_limit_bytes=VMEM_LIMIT)


def _sigmoid(v):
    return 1.0 / (1.0 + jnp.exp(-v))


def _log_sigmoid(v):
    return jnp.minimum(v, 0.0) - jnp.log(1.0 + jnp.exp(-jnp.abs(v)))


_GELU_C = math.sqrt(2.0 / math.pi)


def _gelu_and_grad(v):
    v2 = v * v
    t = jnp.tanh(_GELU_C * v * (1.0 + 0.044715 * v2))
    half = 0.5 * (1.0 + t)
    return v * half, half + (0.5 * _GELU_C) * v * (1.0 - t * t) * (1.0 + (3.0 * 0.044715) * v2)


def _ld(ref):
    return ref[...].astype(F32)


def _flip(a, d):
    return a + d - 2 * a * d


def _unless(cond):
    return jnp.where(cond, 0.0, 1.0).astype(F32)


def _allgather8(xv, name):
    r, cdim = xv.shape

    def body(x_ref, out_ref, sum_ref, send_sems, recv_sems):
        xi, yi, ci = lax.axis_index("x"), lax.axis_index("y"), lax.axis_index("c")
        me = 4 * xi + 2 * yi + ci
        out_ref[pl.ds(me, 1)] = x_ref[...][None]
        started = []
        for k in range(1, N_DEV):
            px, py, pc = _flip(xi, (k >> 2) & 1), _flip(yi, (k >> 1) & 1), _flip(ci, k & 1)
            cp = pltpu.make_async_remote_copy(
                src_ref=x_ref, dst_ref=out_ref.at[me], send_sem=send_sems.at[k - 1], recv_sem=recv_sems.at[k - 1],
                device_id=(px, py, pc), device_id_type=MESH)
            cp.start()
            started.append((cp, 4 * px + 2 * py + pc, k, (px, py, pc)))
        for cp, peer, k, pid in started:
            cp.wait_send()
            pltpu.make_async_remote_copy(
                src_ref=x_ref, dst_ref=out_ref.at[peer], send_sem=send_sems.at[k - 1], recv_sem=recv_sems.at[k - 1],
                device_id=pid, device_id_type=MESH).wait_recv()
        acc = out_ref[0]
        for d in range(1, N_DEV):
            acc = acc + out_ref[d]
        sum_ref[...] = acc

    return pl.pallas_call(
        body, name=name,
        out_shape=(jax.ShapeDtypeStruct((N_DEV, r, cdim), F32), jax.ShapeDtypeStruct((r, cdim), F32)),
        in_specs=[pl.BlockSpec(memory_space=pltpu.VMEM)],
        out_specs=(pl.BlockSpec(memory_space=pltpu.VMEM), pl.BlockSpec(memory_space=pltpu.VMEM)),
        scratch_shapes=[pltpu.SemaphoreType.DMA((N_DEV - 1,)), pltpu.SemaphoreType.DMA((N_DEV - 1,))],
        compiler_params=pltpu.CompilerParams(vmem_limit_bytes=VMEM_LIMIT),
    )(xv)


_HBM = pl.BlockSpec(memory_space=pltpu.HBM)
_SEM = pl.BlockSpec(memory_space=pltpu.SEMAPHORE)
_EFFECT = pltpu.SideEffectType.DATAFLOW_SIDE_EFFECTING
_CHIP_FLIPS = ((1, 0), (0, 1), (1, 1))


def _chip_copies(src_ref, land_ref, send_sems, recv_sems, scatter, halves=False):
    xi, yi, ci = lax.axis_index("x"), lax.axis_index("y"), lax.axis_index("c")
    me = 2 * xi + yi
    out = []

    def slot(j):
        return land_ref.at[j, ci] if halves else land_ref.at[j]

    for k, (dx, dy) in enumerate(_CHIP_FLIPS):
        px, py = _flip(xi, dx), _flip(yi, dy)
        peer = 2 * px + py
        src = src_ref.at[peer] if scatter else slot(me)
        mk = functools.partial(pltpu.make_async_remote_copy, src_ref=src, send_sem=send_sems.at[k],
                               recv_sem=recv_sems.at[k], device_id=(px, py, ci), device_id_type=MESH)
        out.append((mk(dst_ref=slot(me)), mk(dst_ref=slot(peer))))
    return out


def _fill_copies(land_ref, send_sems, recv_sems):
    xi, yi, ci = lax.axis_index("x"), lax.axis_index("y"), lax.axis_index("c")
    out = []
    for k, (dx, dy) in enumerate(_CHIP_FLIPS):
        peer = 2 * _flip(xi, dx) + _flip(yi, dy)
        mk = functools.partial(pltpu.make_async_remote_copy, src_ref=land_ref.at[peer, ci], send_sem=send_sems.at[k],
                               recv_sem=recv_sems.at[k], device_id=(xi, yi, 1 - ci), device_id_type=MESH)
        out.append((mk(dst_ref=land_ref.at[peer, ci]), mk(dst_ref=land_ref.at[peer, 1 - ci])))
    return out


def _fill_start(land, name):
    def body(land_ref, send_sems, recv_sems, land_thru, token):
        for mine, _ in _fill_copies(land_ref, send_sems, recv_sems):
            mine.start()
        token[...] = jnp.zeros_like(token)

    return pl.pallas_call(
        body, name=name,
        out_shape=(pltpu.SemaphoreType.DMA((3,)), pltpu.SemaphoreType.DMA((3,)), pltpu.HBM(land.shape, land.dtype),
                   jax.ShapeDtypeStruct((8, 128), F32)),
        in_specs=(_HBM,), out_specs=(_SEM, _SEM, _HBM, pl.BlockSpec(memory_space=pltpu.VMEM)),
        input_output_aliases={0: 2},
        compiler_params=pltpu.CompilerParams(has_side_effects=_EFFECT),
    )(land)


def _fill_wait(handle, after, name):
    send, recv, land_thru = handle

    def body(land_ref, send_sems, recv_sems, after_ref, land_out):
        for mine, theirs in _fill_copies(land_ref, send_sems, recv_sems):
            mine.wait_send()
            theirs.wait_recv()

    return pl.pallas_call(
        body, name=name, out_shape=pltpu.HBM(land_thru.shape, land_thru.dtype),
        in_specs=(_HBM, _SEM, _SEM, pl.BlockSpec(memory_space=pl.ANY)), out_specs=_HBM,
        input_output_aliases={0: 0},
        compiler_params=pltpu.CompilerParams(has_side_effects=_EFFECT),
    )(land_thru, send, recv, after)


def _gather_start(land, name, halves=False):
    def body(land_ref, send_sems, recv_sems, land_thru, token):
        for mine, _ in _chip_copies(None, land_ref, send_sems, recv_sems, False, halves):
            mine.start()
        token[...] = jnp.zeros_like(token)

    return pl.pallas_call(
        body, name=name,
        out_shape=(pltpu.SemaphoreType.DMA((3,)), pltpu.SemaphoreType.DMA((3,)), pltpu.HBM(land.shape, land.dtype),
                   jax.ShapeDtypeStruct((8, 128), F32)),
        in_specs=(_HBM,), out_specs=(_SEM, _SEM, _HBM, pl.BlockSpec(memory_space=pltpu.VMEM)),
        input_output_aliases={0: 2},
        compiler_params=pltpu.CompilerParams(has_side_effects=_EFFECT),
    )(pltpu.with_memory_space_constraint(land, pltpu.HBM))


def _gather_wait(handle, after, name, halves=False):
    send, recv, land_thru = handle

    def body(land_ref, send_sems, recv_sems, after_ref, land_out):
        for mine, theirs in _chip_copies(None, land_ref, send_sems, recv_sems, False, halves):
            mine.wait_send()
            theirs.wait_recv()

    return pl.pallas_call(
        body, name=name, out_shape=pltpu.HBM(land_thru.shape, land_thru.dtype),
        in_specs=(_HBM, _SEM, _SEM, pl.BlockSpec(memory_space=pl.ANY)), out_specs=_HBM,
        input_output_aliases={0: 0},
        compiler_params=pltpu.CompilerParams(has_side_effects=_EFFECT),
    )(land_thru, send, recv, after)


def _scatter_start(src, name, after=None):
    extra = [] if after is None else [after]

    def body(src_ref, land_ref, *rest):
        send_sems, recv_sems, src_thru, land_thru, token = rest[len(extra):]
        for mine, _ in _chip_copies(src_ref, land_ref, send_sems, recv_sems, True):
            mine.start()
        token[...] = jnp.zeros_like(token)

    return pl.pallas_call(
        body, name=name,
        out_shape=(pltpu.SemaphoreType.DMA((3,)), pltpu.SemaphoreType.DMA((3,)), pltpu.HBM(src.shape, src.dtype),
                   pltpu.HBM(src.shape, src.dtype), jax.ShapeDtypeStruct((8, 128), F32)),
        in_specs=(_HBM, _HBM) + (pl.BlockSpec(memory_space=pl.ANY),) * len(extra),
        out_specs=(_SEM, _SEM, _HBM, _HBM, pl.BlockSpec(memory_space=pltpu.VMEM)),
        input_output_aliases={0: 2, 1: 3},
        compiler_params=pltpu.CompilerParams(has_side_effects=_EFFECT),
    )(pltpu.with_memory_space_constraint(src, pltpu.HBM),
      pltpu.with_memory_space_constraint(lax.empty(src.shape, src.dtype), pltpu.HBM), *extra)


def _scatter_wait(handle, after, name):
    send, recv, src_thru, land_thru = handle

    def body(src_ref, land_ref, send_sems, recv_sems, after_ref, src_out, land_out):
        for mine, theirs in _chip_copies(src_ref, land_ref, send_sems, recv_sems, True):
            mine.wait_send()
            theirs.wait_recv()

    return pl.pallas_call(
        body, name=name,
        out_shape=(pltpu.HBM(src_thru.shape, src_thru.dtype), pltpu.HBM(land_thru.shape, land_thru.dtype)),
        in_specs=(_HBM, _HBM, _SEM, _SEM, pl.BlockSpec(memory_space=pl.ANY)), out_specs=(_HBM, _HBM),
        input_output_aliases={0: 0, 1: 1},
        compiler_params=pltpu.CompilerParams(has_side_effects=_EFFECT),
    )(src_thru, land_thru, send, recv, after)


def _sibling_copy(src_ref, land_ref, send_sems, recv_sems):
    xi, yi, ci = lax.axis_index("x"), lax.axis_index("y"), lax.axis_index("c")
    return pltpu.make_async_remote_copy(src_ref=src_ref, dst_ref=land_ref, send_sem=send_sems.at[0],
                                        recv_sem=recv_sems.at[0], device_id=(xi, yi, 1 - ci), device_id_type=MESH)


def _sibling_start(src, name):
    def body(src_ref, land_ref, send_sems, recv_sems, src_thru, land_thru, token):
        _sibling_copy(src_ref, land_ref, send_sems, recv_sems).start()
        token[...] = jnp.zeros_like(token)

    return pl.pallas_call(
        body, name=name,
        out_shape=(pltpu.SemaphoreType.DMA((1,)), pltpu.SemaphoreType.DMA((1,)), pltpu.HBM(src.shape, src.dtype),
                   pltpu.HBM(src.shape, src.dtype), jax.ShapeDtypeStruct((8, 128), F32)),
        in_specs=(_HBM, _HBM), out_specs=(_SEM, _SEM, _HBM, _HBM, pl.BlockSpec(memory_space=pltpu.VMEM)),
        input_output_aliases={0: 2, 1: 3},
        compiler_params=pltpu.CompilerParams(has_side_effects=_EFFECT),
    )(pltpu.with_memory_space_constraint(src, pltpu.HBM),
      pltpu.with_memory_space_constraint(lax.empty(src.shape, src.dtype), pltpu.HBM))


def _sibling_wait(handle, after, name):
    send, recv, src_thru, land_thru = handle

    def body(src_ref, land_ref, send_sems, recv_sems, after_ref, src_out, land_out):
        cp = _sibling_copy(src_ref, land_ref, send_sems, recv_sems)
        cp.wait_send()
        cp.wait_recv()

    return pl.pallas_call(
        body, name=name,
        out_shape=(pltpu.HBM(src_thru.shape, src_thru.dtype), pltpu.HBM(land_thru.shape, land_thru.dtype)),
        in_specs=(_HBM, _HBM, _SEM, _SEM, pl.BlockSpec(memory_space=pl.ANY)), out_specs=(_HBM, _HBM),
        input_output_aliases={0: 0, 1: 1},
        compiler_params=pltpu.CompilerParams(has_side_effects=_EFFECT),
    )(src_thru, land_thru, send, recv, after)


def _pick(dim, pref):
    if dim <= pref:
        return dim
    t = (pref // 128) * 128
    while t >= 128:
        if dim % t == 0:
            return t
        t -= 128
    return dim


def _matmul(a, b, dims, out_dtype, name, tm=512, tn=1024, tk=2048, out_chips=False, b_chips=False):
    a_parts = a if isinstance(a, tuple) else (a,)
    b_parts = b if isinstance(b, tuple) else (b,)
    na, nb = len(a_parts), len(b_parts)
    assert (na == 1 or dims == "nt") and (nb == 1 or dims == "tn")
    b_shape = (b_parts[0].shape[1], N_CHIPS * b_parts[0].shape[2]) if b_chips else b_parts[0].shape
    if dims == "nn":
        (m, kd), (_, n) = a_parts[0].shape, b_shape
    elif dims == "nt":
        (m, kd), (n, _) = a_parts[0].shape, b_shape
        kd = na * kd
    else:
        (kd, m), (_, n) = a_parts[0].shape, b_shape
        n = nb * n
    tm = _pick(m, tm)
    tn = _pick(n // N_CHIPS, tn) if (out_chips or (b_chips and dims == "nn")) else _pick(n // nb, tn)
    tk = _pick(kd // N_CHIPS, tk) if (b_chips and dims == "nt") else _pick(kd // na, tk)
    nk, nj = kd // tk, n // tn
    ka, jb = nk // na, nj // nb
    if out_chips:
        per_chip = n // N_CHIPS // tn
        out_shape = jax.ShapeDtypeStruct((N_CHIPS, m, n // N_CHIPS), out_dtype)
        out_spec = pl.BlockSpec((None, tm, tn), lambda j, i, k: (j // per_chip, i, j % per_chip))
    else:
        out_shape = jax.ShapeDtypeStruct((m, n), out_dtype)
        out_spec = pl.BlockSpec((tm, tn), lambda j, i, k: (i, j))
    def part_of(idx, first, count):
        return jnp.clip(idx - first, 0, count - 1)

    if dims == "nn":
        a_specs = [pl.BlockSpec((tm, tk), lambda j, i, k: (i, k))]
        b_specs = [pl.BlockSpec((tk, tn), lambda j, i, k: (k, j))]
        dn = (((1,), (0,)), ((), ()))
    elif dims == "nt":
        a_specs = [pl.BlockSpec((tm, tk), lambda j, i, k, p=p: (i, part_of(k, p * ka, ka))) for p in range(na)]
        b_specs = [pl.BlockSpec((tn, tk), lambda j, i, k: (j, k))]
        dn = (((1,), (1,)), ((), ()))
    else:
        a_specs = [pl.BlockSpec((tk, tm), lambda j, i, k: (k, i))]
        b_specs = [pl.BlockSpec((tk, tn), lambda j, i, k, p=p: (k, part_of(j, p * jb, jb))) for p in range(nb)]
        dn = (((0,), (0,)), ((), ()))
    if b_chips and dims == "nn":
        nper = n // N_CHIPS // tn
        b_specs = [pl.BlockSpec((None, tk, tn), lambda j, i, k: (j // nper, k, j % nper))]
    elif b_chips:
        kper = kd // N_CHIPS // tk
        b_specs = [pl.BlockSpec((None, tn, tk), lambda j, i, k: (k // kper, j, k % kper))]
    direct = nk == 1 or out_dtype == F32

    def body(*refs):
        a_refs, b_refs, o_ref = refs[:na], refs[na:na + nb], refs[na + nb]
        acc_ref = o_ref if direct else refs[na + nb + 1]
        j, k = pl.program_id(0), pl.program_id(2)

        def step(a_ref, b_ref):
            part = lax.dot_general(a_ref[...].astype(BF16), b_ref[...].astype(BF16), dn, preferred_element_type=F32)
            if nk == 1:
                o_ref[...] = part.astype(o_ref.dtype)
                return

            @pl.when(k == 0)
            def _():
                acc_ref[...] = part

            @pl.when(k > 0)
            def _():
                acc_ref[...] += part

            if not direct:
                @pl.when(k == nk - 1)
                def _():
                    o_ref[...] = acc_ref[...].astype(o_ref.dtype)

        if na == 1 and nb == 1:
            step(a_refs[0], b_refs[0])
        for p in range(na if na > 1 else 0):
            pl.when(jnp.logical_and(k >= p * ka, k < (p + 1) * ka))(functools.partial(step, a_refs[p], b_refs[0]))
        for p in range(nb if nb > 1 else 0):
            pl.when(jnp.logical_and(j >= p * jb, j < (p + 1) * jb))(functools.partial(step, a_refs[0], b_refs[p]))

    return pl.pallas_call(
        body, name=name, out_shape=out_shape,
        grid=(nj, m // tm, nk),
        in_specs=a_specs + b_specs,
        out_specs=out_spec,
        scratch_shapes=[] if direct else [pltpu.VMEM((tm, tn), F32)],
        compiler_params=_params(("parallel", "parallel", "arbitrary")),
    )(*a_parts, *b_parts)


def _rstd(v):
    return lax.rsqrt(jnp.mean(v * v, axis=-1, keepdims=True) + EPS)


def _row(tm):
    return pl.BlockSpec((tm, D_MODEL), lambda i: (i, 0))


_VEC = pl.BlockSpec((1, D_MODEL), lambda i: (0, 0))


def _pre_norm(x, gn, sc, sh, name):
    t = x.shape[0]
    tm = min(TM_ROW, t)

    def body(x_ref, gn_ref, sc_ref, sh_ref, h_ref):
        xv = x_ref[...]
        h_ref[...] = ((xv * _rstd(xv) * gn_ref[...]) * (1.0 + sc_ref[...]) + sh_ref[...]).astype(BF16)

    return pl.pallas_call(
        body, name=name, out_shape=jax.ShapeDtypeStruct((t, D_MODEL), BF16), grid=(t // tm,),
        in_specs=[_row(tm), _VEC, _VEC, _VEC], out_specs=_row(tm),
        compiler_params=_params(("parallel",)),
    )(x, gn, sc, sh)


def _post_pre(x, y, g, gnp, gn, sc, sh, name):
    t = x.shape[0]
    tm = min(TM_ROW, t)

    def body(x_ref, y_ref, g_ref, gnp_ref, gn_ref, sc_ref, sh_ref, x1_ref, h_ref):
        yv = y_ref[...]
        x1 = x_ref[...] + g_ref[...] * (yv * _rstd(yv) * gnp_ref[...])
        x1_ref[...] = x1
        h_ref[...] = ((x1 * _rstd(x1) * gn_ref[...]) * (1.0 + sc_ref[...]) + sh_ref[...]).astype(BF16)

    return pl.pallas_call(
        body, name=name,
        out_shape=(jax.ShapeDtypeStruct((t, D_MODEL), F32), jax.ShapeDtypeStruct((t, D_MODEL), BF16)),
        grid=(t // tm,),
        in_specs=[_row(tm), _row(tm), _VEC, _VEC, _VEC, _VEC, _VEC], out_specs=(_row(tm), _row(tm)),
        compiler_params=_params(("parallel",)),
    )(x, y, g, gnp, gn, sc, sh)


def _post_loss(x, y, g, gnp, tgt, name):
    t = x.shape[0]
    tm = min(TM_ROW, t)

    def body(x_ref, y_ref, g_ref, gnp_ref, t_ref, dx_ref, loss_ref):
        yv = y_ref[...]
        diff = x_ref[...] + g_ref[...] * (yv * _rstd(yv) * gnp_ref[...]) - t_ref[...]
        dx_ref[...] = diff * (1.0 / D_MODEL)
        part = (0.5 / D_MODEL) * jnp.sum(jnp.sum(diff * diff, axis=-1, keepdims=True), axis=0, keepdims=True)

        @pl.when(pl.program_id(0) == 0)
        def _():
            loss_ref[...] = jnp.zeros_like(loss_ref)

        loss_ref[...] += jnp.broadcast_to(part, loss_ref.shape)

    return pl.pallas_call(
        body, name=name,
        out_shape=(jax.ShapeDtypeStruct((t, D_MODEL), F32), jax.ShapeDtypeStruct((8, 128), F32)),
        grid=(t // tm,),
        in_specs=[_row(tm), _row(tm), _VEC, _VEC, _row(tm)],
        out_specs=(_row(tm), pl.BlockSpec((8, 128), lambda i: (0, 0))),
        compiler_params=_params(("arbitrary",)),
    )(x, y, g, gnp, tgt)


def _acc_rows(ref, val):
    @pl.when(pl.program_id(0) == 0)
    def _():
        ref[...] = jnp.zeros_like(ref)

    ref[...] += jnp.sum(val, axis=0, keepdims=True)


def _post_bwd(dxn, y, g, gnp, name):
    t = y.shape[0]
    tm = min(TM_ROW, t)

    def body(dx_ref, y_ref, g_ref, gnp_ref, dy_ref, dg_ref, dgn_ref):
        yv, dxv = y_ref[...], dx_ref[...]
        r = _rstd(yv)
        yh = yv * r
        _acc_rows(dg_ref, dxv * (yh * gnp_ref[...]))
        dn = dxv * g_ref[...]
        _acc_rows(dgn_ref, dn * yh)
        dyh = dn * gnp_ref[...]
        dy_ref[...] = (r * (dyh - yh * jnp.mean(dyh * yh, axis=-1, keepdims=True))).astype(BF16)

    return pl.pallas_call(
        body, name=name,
        out_shape=(jax.ShapeDtypeStruct((t, D_MODEL), BF16), jax.ShapeDtypeStruct((1, D_MODEL), F32),
                   jax.ShapeDtypeStruct((1, D_MODEL), F32)),
        grid=(t // tm,),
        in_specs=[_row(tm), _row(tm), _VEC, _VEC], out_specs=(_row(tm), _VEC, _VEC),
        compiler_params=_params(("arbitrary",)),
    )(dxn, y, g, gnp)


def _pre_bwd(dh, xin, dres, gn, sc, name):
    t = xin.shape[0]
    tm = min(TM_ROW, t)

    def body(dh_ref, x_ref, dres_ref, gn_ref, sc_ref, dx_ref, dsh_ref, dsc_ref, dgn_ref):
        xv, dhv = x_ref[...], dh_ref[...]
        r = _rstd(xv)
        xh = xv * r
        _acc_rows(dsh_ref, dhv)
        _acc_rows(dsc_ref, dhv * (xh * gn_ref[...]))
        dn = dhv * (1.0 + sc_ref[...])
        _acc_rows(dgn_ref, dn * xh)
        dxh = dn * gn_ref[...]
        dx_ref[...] = dres_ref[...] + r * (dxh - xh * jnp.mean(dxh * xh, axis=-1, keepdims=True))

    vec = jax.ShapeDtypeStruct((1, D_MODEL), F32)
    return pl.pallas_call(
        body, name=name, out_shape=(jax.ShapeDtypeStruct((t, D_MODEL), F32), vec, vec, vec),
        grid=(t // tm,),
        in_specs=[_row(tm), _row(tm), _row(tm), _VEC, _VEC], out_specs=(_row(tm), _VEC, _VEC, _VEC),
        compiler_params=_params(("arbitrary",)),
    )(dh, xin, dres, gn, sc)


def _fix_rows(v8, rows):
    idx = lax.broadcasted_iota(jnp.int32, v8.shape, 0)
    for j, val in rows:
        v8 = jnp.where(idx == j, jnp.broadcast_to(val, v8.shape), v8)
    return v8


def _shift_down(v, halo, s):
    hr, tm = halo.shape[0], v.shape[0]
    out = pltpu.roll(v, s, 0)
    if tm == 8:
        return _fix_rows(out, [(j, halo[hr - s + j:hr - s + j + 1, :]) for j in range(s)])
    head = _fix_rows(out[0:8, :], [(j, halo[hr - s + j:hr - s + j + 1, :]) for j in range(s)])
    return jnp.concatenate([head, out[8:, :]], axis=0)


def _shift_up(v, halo, s):
    tm = v.shape[0]
    out = pltpu.roll(v, tm - s, 0)
    tail = _fix_rows(out[tm - 8:, :], [(8 - s + j, halo[j:j + 1, :]) for j in range(s)])
    return jnp.concatenate([out[:tm - 8, :], tail], axis=0)


def _tile_specs(tm, cw, off, nrow, hr=8):
    ob = off // cw
    per = tm // hr
    main = pl.BlockSpec((tm, cw), lambda j, i: (i, ob + j))
    prev = pl.BlockSpec((hr, cw), lambda j, i: (jnp.maximum(i * per - 1, 0), ob + j))
    nxt = pl.BlockSpec((hr, cw), lambda j, i: (jnp.minimum((i + 1) * per, nrow * per - 1), ob + j))
    return main, prev, nxt


def _conv_fwd(p, w, name):
    t = p.shape[0]
    tm, cw = min(TM_EW, t), CW_EW
    nrow = t // tm
    cb_s, _, _ = _tile_specs(tm, cw, OFF_CB, nrow, 16)
    cc_s, cc_p, _ = _tile_specs(tm, cw, OFF_CC, nrow, 16)
    cx_s, cx_p, _ = _tile_specs(tm, cw, OFF_CX, nrow, 16)

    def body(cb_ref, cc_ref, ccp_ref, cx_ref, cxp_ref, w_ref, z_ref):
        u = _ld(cc_ref) * _ld(cx_ref)
        uh = _ld(ccp_ref) * _ld(cxp_ref) * _unless(pl.program_id(1) == 0)
        wv = w_ref[...]
        conv = wv[2:3, :] * u + wv[1:2, :] * _shift_down(u, uh, 1) + wv[0:1, :] * _shift_down(u, uh, 2)
        z_ref[...] = (_ld(cb_ref) * conv).astype(BF16)

    return pl.pallas_call(
        body, name=name, out_shape=jax.ShapeDtypeStruct((t, CONV_WIDTH), BF16),
        grid=(CONV_WIDTH // cw, nrow),
        in_specs=[cb_s, cc_s, cc_p, cx_s, cx_p, pl.BlockSpec((8, cw), lambda j, i: (0, j))],
        out_specs=pl.BlockSpec((tm, cw), lambda j, i: (i, j)),
        compiler_params=_params(("parallel", "arbitrary")),
    )(p, p, p, p, p, w)


def _acc_w(ref, vals):
    @pl.when(pl.program_id(1) == 0)
    def _():
        ref[...] = jnp.zeros_like(ref)

    for j, v in enumerate(vals):
        ref[j:j + 1, :] += jnp.sum(v, axis=0, keepdims=True)


def _conv_bwd(dz, p, w, dp, name):
    t = p.shape[0]
    tm, cw = min(TM_EW // 2, t), CONV_WIDTH
    nrow = t // tm
    dz_s, _, dz_n = _tile_specs(tm, cw, 0, nrow)
    cb_s, _, cb_n = _tile_specs(tm, cw, OFF_CB, nrow, 16)
    cc_s, cc_p, _ = _tile_specs(tm, cw, OFF_CC, nrow, 16)
    cx_s, cx_p, _ = _tile_specs(tm, cw, OFF_CX, nrow, 16)

    def body(dz_ref, dzn_ref, cb_ref, cbn_ref, cc_ref, ccp_ref, cx_ref, cxp_ref, w_ref, dp_in, dp_ref, dw_ref):
        dcb_ref = dp_ref.at[:, 0:cw]
        dcc_ref = dp_ref.at[:, cw:2 * cw]
        dcx_ref = dp_ref.at[:, 2 * cw:3 * cw]
        i = pl.program_id(1)
        ccv, cxv, dzv = _ld(cc_ref), _ld(cx_ref), dz_ref[...]
        u = ccv * cxv
        uh = _ld(ccp_ref) * _ld(cxp_ref) * _unless(i == 0)
        wv = w_ref[...]
        u1, u2 = _shift_down(u, uh, 1), _shift_down(u, uh, 2)
        conv = wv[2:3, :] * u + wv[1:2, :] * u1 + wv[0:1, :] * u2
        dcb_ref[...] = (dzv * conv).astype(BF16)
        dconv = dzv * _ld(cb_ref)
        dch = dzn_ref[...] * _ld(cbn_ref)[0:8, :] * _unless(i == nrow - 1)
        du = wv[2:3, :] * dconv + wv[1:2, :] * _shift_up(dconv, dch, 1) + wv[0:1, :] * _shift_up(dconv, dch, 2)
        dcc_ref[...] = (du * cxv).astype(BF16)
        dcx_ref[...] = (du * ccv).astype(BF16)
        _acc_w(dw_ref, (dconv * u2, dconv * u1, dconv * u))

    w_s = pl.BlockSpec((8, cw), lambda j, i: (0, j))
    return pl.pallas_call(
        body, name=name, out_shape=(_dp_shape(t), jax.ShapeDtypeStruct((8, CONV_WIDTH), F32)),
        grid=(1, nrow),
        in_specs=[dz_s, dz_n, cb_s, cb_n, cc_s, cc_p, cx_s, cx_p, w_s, _ANY],
        out_specs=(pl.BlockSpec((tm, 3 * cw), lambda j, i: (i, OFF_CB // (3 * cw))), w_s),
        input_output_aliases={9: 0},
        compiler_params=_params(("parallel", "arbitrary")),
    )(dz, dz, p, p, p, p, p, p, w, dp)


def _ffn_fwd(u, w, name):
    t = u.shape[0]
    tm, cw = min(TM_EW, t), CW_EW
    nrow = t // tm
    g_s, g_p, _ = _tile_specs(tm, cw, 0, nrow, 16)
    u_s, _, _ = _tile_specs(tm, cw, D_FF, nrow, 16)

    def body(g_ref, gp_ref, u_ref, w_ref, f_ref):
        gv = _ld(g_ref)
        gh = _ld(gp_ref) * _unless(pl.program_id(1) == 0)
        wv = w_ref[...]
        gc = wv[2:3, :] * gv + wv[1:2, :] * _shift_down(gv, gh, 1) + wv[0:1, :] * _shift_down(gv, gh, 2)
        f_ref[...] = (_gelu_and_grad(gc)[0] * _ld(u_ref)).astype(BF16)

    return pl.pallas_call(
        body, name=name, out_shape=jax.ShapeDtypeStruct((t, D_FF), BF16),
        grid=(D_FF // cw, nrow),
        in_specs=[g_s, g_p, u_s, pl.BlockSpec((8, cw), lambda j, i: (0, j))],
        out_specs=pl.BlockSpec((tm, cw), lambda j, i: (i, j)),
        compiler_params=_params(("parallel", "arbitrary")),
    )(u, u, u, w)


def _ffn_bwd(df, u, w, name):
    t = u.shape[0]
    tm, cw = min(TM_EW, t), CW_EW
    nrow = t // tm
    df_s, _, df_n = _tile_specs(tm, cw, 0, nrow)
    g_s, g_p, g_n = _tile_specs(tm, cw, 0, nrow, 16)
    u_s, _, u_n = _tile_specs(tm, cw, D_FF, nrow, 16)

    def body(df_ref, dfn_ref, g_ref, gp_ref, gn_ref, u_ref, un_ref, w_ref, dg_ref, du_ref, dw_ref):
        i = pl.program_id(1)
        gv, dfv, uv = _ld(g_ref), df_ref[...], _ld(u_ref)
        gh = _ld(gp_ref) * _unless(i == 0)
        wv = w_ref[...]
        g1, g2 = _shift_down(gv, gh, 1), _shift_down(gv, gh, 2)
        gc = wv[2:3, :] * gv + wv[1:2, :] * g1 + wv[0:1, :] * g2
        act, act_grad = _gelu_and_grad(gc)
        du_ref[...] = (dfv * act).astype(BF16)
        dgc = dfv * uv * act_grad
        gnv = _ld(gn_ref)[0:8, :]
        gtail = gv[tm - 8:tm, :]
        gcn = (wv[2:3, :] * gnv + wv[1:2, :] * _shift_down(gnv, gtail, 1) + wv[0:1, :] * _shift_down(gnv, gtail, 2))
        dgcn = dfn_ref[...] * _ld(un_ref)[0:8, :] * _gelu_and_grad(gcn)[1] * _unless(i == nrow - 1)
        dg = wv[2:3, :] * dgc + wv[1:2, :] * _shift_up(dgc, dgcn, 1) + wv[0:1, :] * _shift_up(dgc, dgcn, 2)
        dg_ref[...] = dg.astype(BF16)
        _acc_w(dw_ref, (dgc * g2, dgc * g1, dgc * gv))

    o_s = pl.BlockSpec((tm, cw), lambda j, i: (i, j))
    o_sh = jax.ShapeDtypeStruct((t, D_FF), BF16)
    w_s = pl.BlockSpec((8, cw), lambda j, i: (0, j))
    return pl.pallas_call(
        body, name=name, out_shape=(o_sh, o_sh, jax.ShapeDtypeStruct((8, D_FF), F32)),
        grid=(D_FF // cw, nrow),
        in_specs=[df_s, df_n, g_s, g_p, g_n, u_s, u_n, w_s],
        out_specs=(o_s, o_s, w_s),
        compiler_params=_params(("parallel", "arbitrary")),
    )(df, df, u, u, u, u, u, w)


def _merge_fwd(ya, yb, p, name):
    t = ya.shape[0]
    tm, cw = min(TM_EW, t), CW_EW
    y_s = pl.BlockSpec((tm, cw), lambda i, j: (i, j))

    def body(ya_ref, yb_ref, ga_ref, gb_ref, m_ref):
        m_ref[...] = (_sigmoid(_ld(ga_ref)) * ya_ref[...] + _sigmoid(_ld(gb_ref)) * yb_ref[...]).astype(BF16)

    return pl.pallas_call(
        body, name=name, out_shape=jax.ShapeDtypeStruct((t, D_MODEL), BF16),
        grid=(t // tm, D_MODEL // cw),
        in_specs=[y_s, y_s, pl.BlockSpec((tm, cw), lambda i, j: (i, OFF_GA // cw + j)),
                  pl.BlockSpec((tm, cw), lambda i, j: (i, OFF_GB // cw + j))],
        out_specs=y_s, compiler_params=_params(("parallel", "parallel")),
    )(ya, yb, p, p)


_ANY = pl.BlockSpec(memory_space=pl.ANY)


def _dp_shape(t):
    return jax.ShapeDtypeStruct((t, N_IN_PAD), BF16)


def _merge_bwd(dm, y, p, gate_off, dp, name):
    t = y.shape[0]
    tm, cw = min(TM_EW, t), CW_EW
    y_s = pl.BlockSpec((tm, cw), lambda i, j: (i, j))
    g_s = pl.BlockSpec((tm, cw), lambda i, j: (i, gate_off // cw + j))

    def body(dm_ref, y_ref, g_ref, *rest):
        dy_ref, dp_ref = rest[-2:]
        dmv = dm_ref[...]
        sg = _sigmoid(_ld(g_ref))
        dy_ref[...] = (dmv * sg).astype(BF16)
        dp_ref[...] = (dmv * y_ref[...] * sg * (1.0 - sg)).astype(BF16)

    extra = [] if dp is None else [dp]
    return pl.pallas_call(
        body, name=name, out_shape=(jax.ShapeDtypeStruct((t, D_MODEL), BF16), _dp_shape(t)),
        grid=(t // tm, D_MODEL // cw),
        in_specs=[y_s, y_s, g_s] + [_ANY] * len(extra),
        out_specs=(y_s, g_s), input_output_aliases={} if dp is None else {3: 1},
        compiler_params=_params(("parallel", "parallel")),
    )(dm, y, p, *extra)


def _tri(lower):
    r = lax.broadcasted_iota(jnp.int32, (CHUNK, CHUNK), 0)
    c = lax.broadcasted_iota(jnp.int32, (CHUNK, CHUNK), 1)
    return ((c <= r) if lower else (c >= r)).astype(F32)


def _eye_mask():
    r = lax.broadcasted_iota(jnp.int32, (GLA_DK, GLA_DK), 0)
    c = lax.broadcasted_iota(jnp.int32, (GLA_DK, GLA_DK), 1)
    return r == c


def _row_to_col(v):
    return jnp.sum(jnp.where(_eye_mask(), jnp.broadcast_to(v, (GLA_DK, GLA_DK)), 0.0), axis=1, keepdims=True)


def _col_to_row(v):
    return jnp.sum(jnp.where(_eye_mask(), jnp.broadcast_to(v, (GLA_DK, GLA_DK)), 0.0), axis=0, keepdims=True)


def _dot(a, b, dn):
    return lax.dot_general(a.astype(BF16), b.astype(BF16), (dn, ((), ())), preferred_element_type=F32)


_NN = ((1,), (0,))
_NT = ((1,), (1,))
_TN = ((0,), (0,))


def _gate_logits(lr_ref, wa_ref, ba_ref):
    return _dot(lr_ref[...], wa_ref[...], _NN) + ba_ref[...]


def _chunk_decay(la, tri):
    cum = lax.dot_general(tri, la, ((_NN), ((), ())), precision=lax.Precision.HIGHEST, preferred_element_type=F32)
    e = cum[CHUNK - 1:CHUNK, :]
    return cum, e, jnp.exp(e - cum)


def _gla_fwd(p, wa, ba, name):
    t = p.shape[0]
    rows = min(GLA_ROWS, t)
    cb = rows // CHUNK
    nc = t // CHUNK
    scale = GLA_DK ** -0.5

    def body(q_ref, k_ref, v_ref, lr_ref, wa_ref, ba_ref, o_ref, st_ref, s_scr):
        @pl.when(pl.program_id(0) == 0)
        def _():
            s_scr[...] = jnp.zeros_like(s_scr)

        la_all = _log_sigmoid(_gate_logits(lr_ref, wa_ref, ba_ref)) * (1.0 / GLA_TAU)
        tri = _tri(True)
        for ch in range(cb):
            rs = slice(ch * CHUNK, (ch + 1) * CHUNK)
            for h in range(GLA_HEADS):
                ks = slice(h * GLA_DK, (h + 1) * GLA_DK)
                vs = slice(h * GLA_DV, (h + 1) * GLA_DV)
                _, e, w = _chunk_decay(la_all[rs, ks], tri)
                kd = k_ref[rs, ks].astype(F32) * w
                s_new = _row_to_col(jnp.exp(e)) * s_scr[ks, :] + _dot(kd, v_ref[rs, vs], _TN)
                s_scr[ks, :] = s_new
                st_ref[ch, ks, :] = s_new
                o_ref[rs, vs] = _dot(q_ref[rs, ks].astype(F32) * scale, s_new, _NN)

    return pl.pallas_call(
        body, name=name,
        out_shape=(jax.ShapeDtypeStruct((t, GLA_V), F32), jax.ShapeDtypeStruct((nc, GLA_QK, GLA_DV), F32)),
        grid=(t // rows,),
        in_specs=[pl.BlockSpec((rows, GLA_QK), lambda i: (i, OFF_Q // GLA_QK)),
                  pl.BlockSpec((rows, GLA_QK), lambda i: (i, OFF_K // GLA_QK)),
                  pl.BlockSpec((rows, GLA_V), lambda i: (i, OFF_V // GLA_V)),
                  pl.BlockSpec((rows, LR_PAD), lambda i: (i, OFF_LR // LR_PAD)),
                  pl.BlockSpec((LR_PAD, GLA_QK), lambda i: (0, 0)),
                  pl.BlockSpec((1, GLA_QK), lambda i: (0, 0))],
        out_specs=(pl.BlockSpec((rows, GLA_V), lambda i: (i, 0)),
                   pl.BlockSpec((cb, GLA_QK, GLA_DV), lambda i: (i, 0, 0))),
        scratch_shapes=[pltpu.VMEM((GLA_QK, GLA_DV), F32)],
        compiler_params=_params(("arbitrary",)),
    )(p, p, p, p, wa, ba)


def _gla_bwd(do, p, st, wa, ba, dp, name):
    t = p.shape[0]
    rows = min(GLA_ROWS, t)
    cb = rows // CHUNK
    nb = t // rows
    scale = GLA_DK ** -0.5

    def rev(i):
        return nb - 1 - i

    def body(do_ref, q_ref, k_ref, v_ref, lr_ref, st_ref, stp_ref, wa_ref, ba_ref, dp_in,
             dp_ref, dlr_ref, dwa_ref, dba_ref, ds_scr, dz_scr):
        dq_ref = dp_ref.at[:, OFF_Q:OFF_Q + GLA_QK]
        dk_ref = dp_ref.at[:, OFF_K:OFF_K + GLA_QK]
        dv_ref = dp_ref.at[:, OFF_V:OFF_V + GLA_V]
        i = pl.program_id(0)

        @pl.when(i == 0)
        def _():
            ds_scr[...] = jnp.zeros_like(ds_scr)
            dwa_ref[...] = jnp.zeros_like(dwa_ref)
            dba_ref[...] = jnp.zeros_like(dba_ref)

        z_all = _gate_logits(lr_ref, wa_ref, ba_ref)
        la_all = _log_sigmoid(z_all) * (1.0 / GLA_TAU)
        tri, triu = _tri(True), _tri(False)
        last_row = lax.broadcasted_iota(jnp.int32, (CHUNK, GLA_DK), 0) == CHUNK - 1
        keep_prev = _unless(i == nb - 1)
        for ch in reversed(range(cb)):
            rs = slice(ch * CHUNK, (ch + 1) * CHUNK)
            for h in range(GLA_HEADS):
                ks = slice(h * GLA_DK, (h + 1) * GLA_DK)
                vs = slice(h * GLA_DV, (h + 1) * GLA_DV)
                _, e, w = _chunk_decay(la_all[rs, ks], tri)
                kd = k_ref[rs, ks].astype(F32) * w
                exp_e = jnp.exp(e)
                s_c = st_ref[ch, ks, :]
                if ch > 0:
                    s_p = st_ref[ch - 1, ks, :]
                else:
                    s_p = stp_ref[0, ks, :] * keep_prev
                do_c = do_ref[rs, vs]
                vv = v_ref[rs, vs]
                ds_tot = ds_scr[ks, :] + _dot(q_ref[rs, ks].astype(F32) * scale, do_c, _TN)
                dq_ref[rs, ks] = (_dot(do_c, s_c, _NT) * scale).astype(BF16)
                dkd = _dot(vv, ds_tot, _NT)
                dv_ref[rs, vs] = _dot(kd, ds_tot, _NN).astype(BF16)
                dexp_col = jnp.sum(ds_tot * s_p, axis=1, keepdims=True)
                ds_scr[ks, :] = _row_to_col(exp_e) * ds_tot
                dk_ref[rs, ks] = (dkd * w).astype(BF16)
                dwt = dkd * kd
                de = jnp.sum(dwt, axis=0, keepdims=True) + _col_to_row(dexp_col) * exp_e
                dcum = jnp.where(last_row, de - dwt, -dwt)
                da = lax.dot_general(triu, dcum, (_NN, ((), ())), precision=lax.Precision.HIGHEST,
                                     preferred_element_type=F32)
                dz_scr[rs, ks] = da * (1.0 / GLA_TAU) * _sigmoid(-z_all[rs, ks])
        dz = dz_scr[...]
        dlr_ref[...] = _dot(dz, wa_ref[...], _NT).astype(BF16)
        dwa_ref[...] += _dot(lr_ref[...], dz, _TN)
        dba_ref[...] += jnp.sum(dz, axis=0, keepdims=True)

    qkv = OFF_V + GLA_V
    return pl.pallas_call(
        body, name=name,
        out_shape=(_dp_shape(t), jax.ShapeDtypeStruct((t, LR_PAD), BF16),
                   jax.ShapeDtypeStruct((LR_PAD, GLA_QK), F32), jax.ShapeDtypeStruct((1, GLA_QK), F32)),
        grid=(nb,),
        in_specs=[pl.BlockSpec((rows, GLA_V), lambda i: (rev(i), 0)),
                  pl.BlockSpec((rows, GLA_QK), lambda i: (rev(i), OFF_Q // GLA_QK)),
                  pl.BlockSpec((rows, GLA_QK), lambda i: (rev(i), OFF_K // GLA_QK)),
                  pl.BlockSpec((rows, GLA_V), lambda i: (rev(i), OFF_V // GLA_V)),
                  pl.BlockSpec((rows, LR_PAD), lambda i: (rev(i), OFF_LR // LR_PAD)),
                  pl.BlockSpec((cb, GLA_QK, GLA_DV), lambda i: (rev(i), 0, 0)),
                  pl.BlockSpec((1, GLA_QK, GLA_DV), lambda i: (jnp.maximum(rev(i) * cb - 1, 0), 0, 0)),
                  pl.BlockSpec((LR_PAD, GLA_QK), lambda i: (0, 0)),
                  pl.BlockSpec((1, GLA_QK), lambda i: (0, 0)), _ANY],
        out_specs=(pl.BlockSpec((rows, qkv), lambda i: (rev(i), 0)),
                   pl.BlockSpec((rows, LR_PAD), lambda i: (rev(i), 0)),
                   pl.BlockSpec((LR_PAD, GLA_QK), lambda i: (0, 0)),
                   pl.BlockSpec((1, GLA_QK), lambda i: (0, 0))),
        input_output_aliases={9: 0},
        scratch_shapes=[pltpu.VMEM((GLA_QK, GLA_DV), F32), pltpu.VMEM((rows, GLA_QK), F32)],
        compiler_params=_params(("arbitrary",)),
    )(do, p, p, p, p, st, st, wa, ba, dp)


def _gla_out_fwd(o, p, gng, name):
    t = o.shape[0]
    tm = min(TM_EW, t)

    def body(o_ref, r_ref, g_ref, z_ref):
        gv = g_ref[...]
        for h in range(GLA_HEADS):
            vs = slice(h * GLA_DV, (h + 1) * GLA_DV)
            ov, rv = o_ref[:, vs], r_ref[:, vs].astype(F32)
            z_ref[:, vs] = ((ov * _rstd(ov) * gv) * (rv * _sigmoid(rv))).astype(BF16)

    return pl.pallas_call(
        body, name=name, out_shape=jax.ShapeDtypeStruct((t, GLA_V), BF16), grid=(t // tm,),
        in_specs=[pl.BlockSpec((tm, GLA_V), lambda i: (i, 0)),
                  pl.BlockSpec((tm, GLA_V), lambda i: (i, OFF_R // GLA_V)),
                  pl.BlockSpec((1, GLA_DV), lambda i: (0, 0))],
        out_specs=pl.BlockSpec((tm, GLA_V), lambda i: (i, 0)),
        compiler_params=_params(("parallel",)),
    )(o, p, gng)


def _gla_out_bwd(dz, o, p, gng, dp, name):
    t = o.shape[0]
    tm = min(TM_EW, t)

    def body(dz_ref, o_ref, r_ref, g_ref, dp_in, do_ref, dr_ref, dg_ref):
        @pl.when(pl.program_id(0) == 0)
        def _():
            dg_ref[...] = jnp.zeros_like(dg_ref)

        gv = g_ref[...]
        for h in range(GLA_HEADS):
            vs = slice(h * GLA_DV, (h + 1) * GLA_DV)
            ov, rv, dzv = o_ref[:, vs], r_ref[:, vs].astype(F32), dz_ref[:, vs]
            rs = _rstd(ov)
            oh = ov * rs
            sg = _sigmoid(rv)
            dr_ref[:, vs] = (dzv * (oh * gv) * (sg * (1.0 + rv * (1.0 - sg)))).astype(BF16)
            don = dzv * (rv * sg)
            dg_ref[...] += jnp.sum(don * oh, axis=0, keepdims=True)
            doh = don * gv
            do_ref[:, vs] = rs * (doh - oh * jnp.mean(doh * oh, axis=-1, keepdims=True))

    row = pl.BlockSpec((tm, GLA_V), lambda i: (i, 0))
    r_s = pl.BlockSpec((tm, GLA_V), lambda i: (i, OFF_R // GLA_V))
    return pl.pallas_call(
        body, name=name,
        out_shape=(jax.ShapeDtypeStruct((t, GLA_V), F32), _dp_shape(t), jax.ShapeDtypeStruct((1, GLA_DV), F32)),
        grid=(t // tm,),
        in_specs=[row, row, r_s, pl.BlockSpec((1, GLA_DV), lambda i: (0, 0)), _ANY],
        out_specs=(row, r_s, pl.BlockSpec((1, GLA_DV), lambda i: (0, 0))),
        input_output_aliases={4: 1},
        compiler_params=_params(("arbitrary",)),
    )(dz, o, p, gng, dp)


def _ada_fwd(c_all, w, b, layer, name):
    n = w.shape[2]
    tn = _pick(n, 512)

    def body(c_ref, w_ref, b_ref, o_ref):
        cv = c_ref[...]
        o_ref[...] = _dot(cv * _sigmoid(cv), w_ref[...], _NN) + b_ref[...]

    return pl.pallas_call(
        body, name=name, out_shape=jax.ShapeDtypeStruct((16, n), F32), grid=(n // tn,),
        in_specs=[pl.BlockSpec((16, D_MODEL), lambda j: (0, 0)),
                  pl.BlockSpec((None, D_MODEL, tn), lambda j: (layer, 0, j)),
                  pl.BlockSpec((1, tn), lambda j: (0, j))],
        out_specs=pl.BlockSpec((16, tn), lambda j: (0, j)),
        compiler_params=_params(("parallel",)),
    )(c_all, w, b)


def _ada_bwd(c_all, dmod, name):
    n = dmod.shape[2]
    tn = _pick(n, 512)

    def body(c_ref, d_ref, o_ref):
        cv = c_ref[...]
        o_ref[...] = _dot(cv * _sigmoid(cv), d_ref[...], _TN)

    return pl.pallas_call(
        body, name=name, out_shape=jax.ShapeDtypeStruct((DEPTH, D_MODEL, n), F32), grid=(DEPTH, n // tn),
        in_specs=[pl.BlockSpec((16, D_MODEL), lambda l, j: (0, 0)),
                  pl.BlockSpec((None, 16, tn), lambda l, j: (l, 0, j))],
        out_specs=pl.BlockSpec((None, D_MODEL, tn), lambda l, j: (l, 0, j)),
        compiler_params=_params(("parallel", "parallel")),
    )(c_all, dmod)


def _rows_tile(nrows, ncols, target_bytes):
    want = max(16, target_bytes // (4 * ncols))
    if nrows <= want:
        return nrows
    t = (want // 16) * 16
    while t >= 16:
        if nrows % t == 0:
            return t
        t -= 16
    return nrows


def _sum_chips(sent, landed, chip, name):
    _, nrows, ncols = sent[0].shape
    tr = _rows_tile(nrows, ncols, 2 << 20)
    nblk = nrows // tr

    def body(chip_ref, *refs):
        own, got, o_ref = refs[:DEPTH], refs[DEPTH:2 * DEPTH], refs[2 * DEPTH]
        me = chip_ref[0]
        for l in range(DEPTH):
            for j in range(N_CHIPS):
                def add(val):
                    if j == 0:
                        o_ref[...] = val.astype(F32)
                    else:
                        o_ref[...] += val.astype(F32)

                @pl.when(jnp.logical_and(pl.program_id(0) == l, me == j))
                def _():
                    add(own[l][...])

                @pl.when(jnp.logical_and(pl.program_id(0) == l, me != j))
                def _():
                    add(got[l][j])

    def rows_of(layer):
        return lambda l, i, chip_ref: jnp.where(l == layer, i, 0)

    own_specs = [pl.BlockSpec((None, tr, ncols), lambda l, i, chip_ref, r=rows_of(k): (chip_ref[0], r(l, i, chip_ref), 0))
                 for k in range(DEPTH)]
    got_specs = [pl.BlockSpec((N_CHIPS, tr, ncols), lambda l, i, chip_ref, r=rows_of(k): (0, r(l, i, chip_ref), 0))
                 for k in range(DEPTH)]
    return pl.pallas_call(
        body, name=name, out_shape=jax.ShapeDtypeStruct((DEPTH, nrows, ncols), F32),
        grid_spec=pltpu.PrefetchScalarGridSpec(
            num_scalar_prefetch=1, grid=(DEPTH, nblk), in_specs=own_specs + got_specs,
            out_specs=pl.BlockSpec((None, tr, ncols), lambda l, i, chip_ref: (l, i, 0))),
        compiler_params=_params(("arbitrary", "arbitrary")),
    )(chip, *sent, *landed)


def _adamw(w, m, v, ga, gb, name, tile=None):
    two = gb is not None
    c1 = 1.0 - ADAM_B1 ** ADAM_STEP
    c2 = 1.0 - ADAM_B2 ** ADAM_STEP

    def body(*refs):
        if two:
            w_ref, m_ref, v_ref, ga_ref, gb_ref, g_ref, d_ref, nm_ref, nv_ref = refs
            g = ga_ref[...] + gb_ref[...]
        else:
            w_ref, m_ref, v_ref, ga_ref, g_ref, d_ref, nm_ref, nv_ref = refs
            g = ga_ref[...]
        g_ref[...] = g
        nm = ADAM_B1 * m_ref[...] + (1.0 - ADAM_B1) * g
        nv = ADAM_B2 * v_ref[...] + (1.0 - ADAM_B2) * (g * g)
        nm_ref[...] = nm
        nv_ref[...] = nv
        d_ref[...] = -ADAM_LR * ((nm / c1) / (jnp.sqrt(nv / c2) + ADAM_EPS) + ADAM_WD * w_ref[...])

    if tile is None:
        nl, nrows, ncols = w.shape
        tr = _rows_tile(nrows, ncols, 1 << 20)
        blk = pl.BlockSpec((None, tr, ncols), lambda l, i: (l, i, 0))
        grid = (nl, nrows // tr)
    else:
        nrows, nl, ncols = w.shape
        rb, cb = tile
        blk = pl.BlockSpec((rb, nl, cb), lambda i, j: (i, 0, j))
        grid = (nrows // rb, ncols // cb)
    sh = jax.ShapeDtypeStruct(w.shape, F32)
    ins = [w, m, v, ga] + ([gb] if two else [])
    return pl.pallas_call(
        body, name=name, out_shape=(sh, sh, sh, sh), grid=grid,
        in_specs=[blk] * len(ins), out_specs=(blk, blk, blk, blk),
        compiler_params=_params(("parallel", "parallel")),
    )(*ins)


def _pad_rows(a, rows):
    return jnp.concatenate([a, jnp.zeros((rows - a.shape[0],) + a.shape[1:], a.dtype)], axis=0)


N_IN_CHIP = N_IN // N_CHIPS
_LR_LO = 3072 - N_IN_CHIP
_LR_HI = _LR_LO + GLA_LOWRANK


def _w_in_from_chips(a):
    return jnp.concatenate([a[0], a[1][:, :_LR_LO], a[1][:, _LR_HI:], a[2], a[3], a[1][:, _LR_LO:_LR_HI],
                            jnp.zeros((a.shape[1], LR_PAD - GLA_LOWRANK), a.dtype)], axis=1)


def _w_in_to_chips(w):
    s2 = 2 * N_IN_CHIP - GLA_LOWRANK
    s3 = s2 + N_IN_CHIP
    c1 = jnp.concatenate([w[:, N_IN_CHIP:3072], w[:, OFF_LR:OFF_LR + GLA_LOWRANK], w[:, 3072:s2]], axis=1)
    return jnp.stack([w[:, :N_IN_CHIP], c1, w[:, s2:s3], w[:, s3:OFF_LR]])


_BIG = ("w_in", "w_og", "w_oc", "w_o", "w_up", "w_dn")
_ROW_SHARDED = ("w_o", "w_dn")
_TWO_LEVEL = ("w_in", "w_up")


def kernel(x, c, w_ada, b_ada, norm_g, w_in, w_a2, b_a2, gla_norm_g, w_out_gla, conv_mix_w, w_out_conv, w_o, w_up, ffn_conv_w, w_down, loss_target, m_w_ada, m_b_ada, m_norm_g, m_w_in, m_w_a2, m_b_a2, m_gla_norm_g, m_w_out_gla, m_conv_mix_w, m_w_out_conv, m_w_o, m_w_up, m_ffn_conv_w, m_w_down, v_w_ada, v_b_ada, v_norm_g, v_w_in, v_w_a2, v_b_a2, v_gla_norm_g, v_w_out_gla, v_conv_mix_w, v_w_out_conv, v_w_o, v_w_up, v_ffn_conv_w, v_w_down):
    xi, yi, ci = lax.axis_index("x"), lax.axis_index("y"), lax.axis_index("c")
    chip = 2 * xi + yi
    dev = 2 * chip + ci
    chip_arr = jnp.reshape(chip, (1,)).astype(jnp.int32)
    xt = x[0]
    tgt = loss_target[0]

    c_all = _allgather8(jnp.broadcast_to(c, (8, D_MODEL)), "gather_c")[0][:, 0, :]
    c16 = _pad_rows(c_all, 16)
    sm_parts = [norm_g.reshape(-1), w_a2.reshape(-1), conv_mix_w.reshape(-1), ffn_conv_w.reshape(-1)]
    sm_sizes = [a.shape[0] for a in sm_parts]
    sm_flat = jnp.concatenate(sm_parts)
    sm_rows = -(-sm_flat.shape[0] // 128)
    sm_rows = -(-sm_rows // 8) * 8
    sm_flat = jnp.concatenate([sm_flat, jnp.zeros((sm_rows * 128 - sm_flat.shape[0],), F32)]).reshape(sm_rows, 128)
    sm_all = _allgather8(sm_flat, "gather_small")[0].reshape(N_DEV, -1)[0::2]
    offs = [0]
    for s in sm_sizes:
        offs.append(offs[-1] + s)

    def small_full(idx, shape):
        a = sm_all[:, offs[idx]:offs[idx + 1]].reshape((N_CHIPS,) + shape)
        a = jnp.moveaxis(a, 0, -2)
        return a.reshape(shape[:-1] + (N_CHIPS * shape[-1],))

    norm_g_f = small_full(0, (DEPTH, 4, 512))
    w_a2_f = small_full(1, (DEPTH, GLA_LOWRANK, 128))
    conv_w_f = small_full(2, (DEPTH, 3, 256))
    ffn_w_f = small_full(3, (DEPTH, 3, 1408))

    b_loc = lax.dynamic_slice(b_ada, (0, chip * 3072), (DEPTH, 3072))
    mod_loc = jnp.concatenate(
        [_ada_fwd(c16, w_ada, b_loc[l:l + 1], l, "ada_fwd")[:8] for l in range(DEPTH)], axis=0)
    mod_all = _allgather8(mod_loc, "gather_mod")[0][0::2]
    mods = []
    for l in range(DEPTH):
        row = lax.dynamic_slice(mod_all, (0, l * 8 + dev, 0), (N_CHIPS, 1, 3072)).reshape(1, 6 * D_MODEL)
        mods.append([row[:, k * D_MODEL:(k + 1) * D_MODEL] for k in range(6)])

    big = dict(w_in=w_in, w_og=w_out_gla, w_oc=w_out_conv, w_o=w_o, w_up=w_up, w_dn=w_down)
    gathers = {}
    tok = 0.0 * (mod_all[0, 0, 0] + sm_all[0, 0])
    for l in range(DEPTH):
        for k in _BIG:
            shard = (big[k][l] + tok).astype(BF16)
            if k in _TWO_LEVEL:
                shard = shard.reshape(2, shard.shape[0] // 2, shard.shape[1])
            land = lax.dynamic_update_slice(lax.empty((N_CHIPS,) + shard.shape, BF16), shard[None],
                                            (chip,) + (0,) * shard.ndim)
            *handle, token = _gather_start(land, "gather_start_%s_%d" % (k, l), k in _TWO_LEVEL)
            gathers[k, l] = tuple(handle)
            tok = token[0, 0]

    fills = {}

    def fill_early(k, l, after):
        land = _gather_wait(gathers[k, l], after, "gather_wait_%s_%d" % (k, l), True)
        *handle, token = _fill_start(land, "fill_start_%s_%d" % (k, l))
        fills[k, l] = tuple(handle)
        return token

    def gathered(k, l, after):
        if k in _TWO_LEVEL:
            if (k, l) not in fills:
                fill_early(k, l, after)
            full = _fill_wait(fills[k, l], after, "fill_wait_%s_%d" % (k, l))
            full = full.reshape(N_CHIPS, 2 * full.shape[2], full.shape[3])
        else:
            full = _gather_wait(gathers[k, l], after, "gather_wait_%s_%d" % (k, l))
        if k in _ROW_SHARDED:
            return full.reshape(N_CHIPS * full.shape[1], full.shape[2])
        return _w_in_from_chips(full) if k == "w_in" else full

    saved = []
    h = None
    xin = xt
    for l in range(DEPTH):
        sh1, sc1, g1, sh2, sc2, g2 = mods[l]
        gn = [norm_g_f[l, k][None] for k in range(4)]
        wa = _pad_rows(w_a2_f[l], LR_PAD)
        ba = b_a2[l][None]
        gng = gla_norm_g[l][None]
        cw8 = _pad_rows(conv_w_f[l], 8)
        fw8 = _pad_rows(ffn_w_f[l], 8)
        if l == 0:
            h = _pre_norm(xin, gn[0] + tok, sc1, sh1, "pre_norm")
        wi = gathered("w_in", l, h)
        p = _matmul(h, wi, "nn", BF16, "mm_in", tn=1152)
        o, st = _gla_fwd(p, wa, ba, "gla_fwd")
        za = _gla_out_fwd(o, p, gng, "gla_out_fwd")
        zb = _conv_fwd(p, cw8, "conv_fwd")
        wog, woc = gathered("w_og", l, zb), gathered("w_oc", l, zb)
        ya = _matmul(za, wog, "nn", F32, "mm_out_gla", tm=2048, b_chips=True)
        yb = _matmul(zb, woc, "nn", F32, "mm_out_conv", tm=2048, b_chips=True)
        mm = _merge_fwd(ya, yb, p, "merge_fwd")
        wo = gathered("w_o", l, fill_early("w_up", l, mm))
        y = _matmul(mm, wo, "nn", F32, "mm_o")
        x1, h2 = _post_pre(xin, y, g1, gn[1], gn[2], sc2, sh2, "post_pre")
        wup = gathered("w_up", l, h2)
        u = _matmul(h2, wup, "nn", BF16, "mm_up", tn=1408, b_chips=True)
        f = _ffn_fwd(u, fw8, "ffn_fwd")
        wdn = gathered("w_dn", l, fill_early("w_in", l + 1, f) if l + 1 < DEPTH else f)
        y2 = _matmul(f, wdn, "nn", F32, "mm_down", tm=1024, tn=2048, tk=1408)
        saved.append(dict(xin=xin, h=h, p=p, o=o, st=st, za=za, zb=zb, ya=ya, yb=yb, mm=mm, y=y, x1=x1, h2=h2,
                          u=u, f=f, y2=y2, wi=wi, wog=wog, woc=woc, wo=wo, wup=wup, wdn=wdn, wa=wa, ba=ba,
                          gng=gng, cw8=cw8, fw8=fw8, gn=gn, mod=mods[l]))
        if l + 1 < DEPTH:
            nsh1, nsc1 = mods[l + 1][0], mods[l + 1][1]
            xin, h = _post_pre(x1, y2, g2, gn[3], norm_g_f[l + 1, 0][None], nsc1, nsh1, "post_pre")
        else:
            dx, loss_tile = _post_loss(x1, y2, g2, gn[3], tgt, "post_loss")
    loss = lax.psum(loss_tile[0, 0], ("x", "y", "c"))

    scatters = {}

    def scatter(k, l, dw, after=None):
        if k in _ROW_SHARDED:
            send = dw.reshape(N_CHIPS, dw.shape[0] // N_CHIPS, dw.shape[1])
        else:
            send = _w_in_to_chips(dw) if k == "w_in" else dw
        *handle, token = _scatter_start(send, "scatter_start_%s_%d" % (k, l), after)
        scatters[k, l] = tuple(handle)
        return token[0, 0]

    sm = {k: [None] * DEPTH for k in ("dmod", "norm_g", "w_a2", "b_a2", "gng", "conv_w", "ffn_w")}
    for l in reversed(range(DEPTH)):
        s = saved[l]
        sh1, sc1, g1, sh2, sc2, g2 = s["mod"]
        gn = s["gn"]
        dy2, dg2, dgn3 = _post_bwd(dx, s["y2"], g2, gn[3], "post_bwd")
        tk = scatter("w_dn", l, _matmul(s["f"], dy2, "tn", BF16, "mm_down_dw", tm=512, tn=1024, tk=4096))
        df = _matmul(dy2, s["wdn"], "nt", F32, "mm_down_dx", tn=1408)
        dgate, dup, dfw = _ffn_bwd(df, s["u"], s["fw8"] + tk, "ffn_bwd")
        du = (dgate, dup)
        tk = scatter("w_up", l, _matmul(s["h2"], du, "tn", BF16, "mm_up_dw", tm=1024, tn=1408, tk=2048, out_chips=True))
        dh2 = _matmul(du, s["wup"], "nt", F32, "mm_up_dx", tm=1024, tn=2048, tk=1408, b_chips=True)
        dx1, dsh2, dsc2, dgn2 = _pre_bwd(dh2, s["x1"], dx, gn[2] + tk, sc2, "pre_bwd")
        dy, dg1, dgn1 = _post_bwd(dx1, s["y"], g1, gn[1], "post_bwd")
        tk = scatter("w_o", l, _matmul(s["mm"], dy, "tn", BF16, "mm_o_dw", tk=4096))
        dm = _matmul(dy, s["wo"], "nt", F32, "mm_o_dx")
        dya, dp = _merge_bwd(dm, s["ya"], s["p"], OFF_GA, None, "merge_bwd_a")
        dyb, dp = _merge_bwd(dm, s["yb"], s["p"], OFF_GB, dp, "merge_bwd_b")
        tk = tk + scatter("w_og", l, _matmul(s["za"], dya, "tn", BF16, "mm_out_gla_dw", tk=4096, out_chips=True))
        dza = _matmul(dya, s["wog"], "nt", F32, "mm_out_gla_dx", tm=2048, b_chips=True)
        do, dp, dgng = _gla_out_bwd(dza, s["o"], s["p"], s["gng"] + tk, dp, "gla_out_bwd")
        tk = scatter("w_oc", l, _matmul(s["zb"], dyb, "tn", BF16, "mm_out_conv_dw", tk=4096, out_chips=True))
        dzb = _matmul(dyb, s["woc"], "nt", F32, "mm_out_conv_dx", tm=2048, b_chips=True)
        dp, dcw = _conv_bwd(dzb, s["p"], s["cw8"] + tk, dp, "conv_bwd")
        dp, dlr, dwa, dba = _gla_bwd(do, s["p"], s["st"], s["wa"], s["ba"], dp, "gla_bwd")
        dp = lax.dynamic_update_slice(dp, dlr, (0, OFF_LR))
        dw_in = _matmul(s["h"], dp, "tn", BF16, "mm_in_dw", tm=512, tn=1152, tk=4096)
        tk = scatter("w_in", l, dw_in) if l > 0 else 0.0
        dh = _matmul(dp, s["wi"], "nt", F32, "mm_in_dx", tm=1024, tn=2048, tk=1152)
        dx, dsh1, dsc1, dgn0 = _pre_bwd(dh, s["xin"], dx1, gn[0] + tk, sc1, "pre_bwd")
        sm["dmod"][l] = jnp.concatenate([dsh1, dsc1, dg1, dsh2, dsc2, dg2], axis=1)[0]
        sm["norm_g"][l] = jnp.concatenate([dgn0, dgn1, dgn2, dgn3], axis=0)
        sm["w_a2"][l] = dwa[:GLA_LOWRANK]
        sm["b_a2"][l] = dba[0]
        sm["gng"][l] = dgng[0]
        sm["conv_w"][l] = dcw[:3]
        sm["ffn_w"][l] = dfw[:3]
    grad_x = dx[None]

    names = ("dmod", "norm_g", "w_a2", "b_a2", "gng", "conv_w", "ffn_w")
    parts = [jnp.stack(sm[k]).reshape(-1) for k in names]
    shapes = [jnp.stack(sm[k]).shape for k in names]
    sizes = [a.shape[0] for a in parts]
    flat = jnp.concatenate(parts)
    rows = -(-flat.shape[0] // 1024) * 8
    flat = jnp.concatenate([flat, jnp.zeros((rows * 128 - flat.shape[0],), F32)]).reshape(rows, 128)
    gath, tot = _allgather8(flat, "reduce_small")
    tk = scatter("w_in", 0, dw_in, after=tot)
    c16 = c16 + tk
    po = [0]
    for s_ in sizes:
        po.append(po[-1] + s_)
    tot = tot.reshape(-1)
    tot_of = {k: tot[po[i]:po[i + 1]].reshape(shapes[i]) for i, k in enumerate(names)}
    dmod_all = gath.reshape(N_DEV, -1)[:, po[0]:po[1]].reshape(N_DEV, DEPTH, 6 * D_MODEL)

    def chip_cols(a, width):
        return lax.dynamic_slice_in_dim(a, chip * width, width, axis=a.ndim - 1)

    dml = jnp.transpose(chip_cols(dmod_all, 3072), (1, 0, 2))
    dml = jnp.concatenate([dml, jnp.zeros_like(dml)], axis=1)
    g_w_ada = _ada_bwd(c16, dml, "ada_bwd")

    def upd(w, m, v, ga, gb, name):
        sh = w.shape
        as3 = sh if len(sh) == 3 else (1,) + sh
        outs = _adamw(w.reshape(as3), m.reshape(as3), v.reshape(as3), ga.reshape(as3),
                      None if gb is None else gb.reshape(as3), name)
        return [a.reshape(sh) for a in outs]

    res = {}
    res["w_ada"] = upd(w_ada, m_w_ada, v_w_ada, g_w_ada, None, "adamw")
    res["b_ada"] = upd(b_ada, m_b_ada, v_b_ada, tot_of["dmod"], None, "adamw")
    res["norm_g"] = upd(norm_g, m_norm_g, v_norm_g, chip_cols(tot_of["norm_g"], 512), None, "adamw")
    res["w_a2"] = upd(w_a2, m_w_a2, v_w_a2, chip_cols(tot_of["w_a2"], 128), None, "adamw")
    res["b_a2"] = upd(b_a2, m_b_a2, v_b_a2, tot_of["b_a2"], None, "adamw")
    res["gla_norm_g"] = upd(gla_norm_g, m_gla_norm_g, v_gla_norm_g, tot_of["gng"], None, "adamw")
    res["conv_mix_w"] = upd(conv_mix_w, m_conv_mix_w, v_conv_mix_w, chip_cols(tot_of["conv_w"], 256), None, "adamw")
    res["ffn_conv_w"] = upd(ffn_conv_w, m_ffn_conv_w, v_ffn_conv_w, chip_cols(tot_of["ffn_w"], 1408), None, "adamw")

    full_name = dict(w_in="w_in", w_og="w_out_gla", w_oc="w_out_conv", w_o="w_o", w_up="w_up", w_dn="w_down")
    state = dict(w_in=(w_in, m_w_in, v_w_in), w_og=(w_out_gla, m_w_out_gla, v_w_out_gla),
                 w_oc=(w_out_conv, m_w_out_conv, v_w_out_conv), w_o=(w_o, m_w_o, v_w_o),
                 w_up=(w_up, m_w_up, v_w_up), w_dn=(w_down, m_w_down, v_w_down))
    def finish(k, handle, after):
        plane, other = _sibling_wait(handle, after, "sibling_wait_" + k)
        if k == "w_in":
            outs = _adamw(*[jnp.transpose(a, (2, 0, 1)) for a in state[k]], plane, other, "adamw_w_in",
                          tile=(N_IN_CHIP // 4, D_MODEL // 8))
            res[full_name[k]] = [jnp.transpose(a, (1, 2, 0)) for a in outs]
        else:
            res[full_name[k]] = upd(*state[k], plane, other, "adamw")

    after = res["w_ada"][3]
    pending = None
    for k in ("w_dn", "w_up", "w_o", "w_og", "w_oc", "w_in"):
        done = [_scatter_wait(scatters[k, l], after, "scatter_wait_%s_%d" % (k, l)) for l in range(DEPTH)]
        plane = _sum_chips([d[0] for d in done], [d[1] for d in done], chip_arr, "sum_chips")
        if k == "w_in":
            plane = jnp.transpose(plane, (2, 0, 1))
        *handle, after = _sibling_start(plane, "sibling_start_" + k)
        if pending is not None:
            finish(*pending, plane)
        pending = (k, tuple(handle))
    finish(*pending, after)
    order = ("w_ada", "b_ada", "norm_g", "w_in", "w_a2", "b_a2", "gla_norm_g", "w_out_gla", "conv_mix_w",
             "w_out_conv", "w_o", "w_up", "ffn_conv_w", "w_down")
    return (loss, grad_x, *[res[k][0] for k in order], *[res[k][1] for k in order],
            *[res[k][2] for k in order], *[res[k][3] for k in order])
```

```python
import functools
import math

import jax
import jax.numpy as jnp
from jax import lax
from jax.experimental import pallas as pl
from jax.experimental.pallas import tpu as pltpu

F32 = jnp.float32
BF16 = jnp.bfloat16
MESH = pl.DeviceIdType.MESH

D_MODEL = 2048
DEPTH = 2
CHUNK = 64
GLA_HEADS = 4
GLA_DK = 128
GLA_DV = 256
GLA_QK = GLA_HEADS * GLA_DK
GLA_V = GLA_HEADS * GLA_DV
GLA_LOWRANK = 16
GLA_TAU = 16.0
CONV_WIDTH = 1024
D_FF = 5632
EPS = 1e-6
N_IN = 10256
LR_PAD = 128
N_IN_PAD = N_IN - GLA_LOWRANK + LR_PAD
OFF_Q, OFF_K, OFF_V, OFF_R = 0, 512, 1024, 2048
OFF_CB, OFF_CC, OFF_CX, OFF_GA, OFF_GB, OFF_LR = 3072, 4096, 5120, 6144, 8192, 10240

ADAM_LR = 0.001
ADAM_B1 = 0.9
ADAM_B2 = 0.999
ADAM_EPS = 1e-08
ADAM_WD = 0.01
ADAM_STEP = 10

N_CHIPS = 4
N_DEV = 8
VMEM_LIMIT = 56 * 1024 * 1024
TM_ROW = 256
TM_EW = 512
CW_EW = 512
GLA_ROWS = 256


def _params(sem=None):
    return pltpu.CompilerParams(dimension_semantics=sem, vmem_limit_bytes=VMEM_LIMIT)


def _sigmoid(v):
    return 1.0 / (1.0 + jnp.exp(-v))


def _log_sigmoid(v):
    return jnp.minimum(v, 0.0) - jnp.log(1.0 + jnp.exp(-jnp.abs(v)))


_GELU_C = math.sqrt(2.0 / math.pi)


def _gelu_and_grad(v):
    v2 = v * v
    t = jnp.tanh(_GELU_C * v * (1.0 + 0.044715 * v2))
    half = 0.5 * (1.0 + t)
    return v * half, half + (0.5 * _GELU_C) * v * (1.0 - t * t) * (1.0 + (3.0 * 0.044715) * v2)


def _ld(ref):
    return ref[...].astype(F32)


def _flip(a, d):
    return a + d - 2 * a * d


def _unless(cond):
    return jnp.where(cond, 0.0, 1.0).astype(F32)


def _allgather8(xv, name):
    r, cdim = xv.shape

    def body(x_ref, out_ref, sum_ref, send_sems, recv_sems):
        xi, yi, ci = lax.axis_index("x"), lax.axis_index("y"), lax.axis_index("c")
        me = 4 * xi + 2 * yi + ci
        out_ref[pl.ds(me, 1)] = x_ref[...][None]
        started = []
        for k in range(1, N_DEV):
            px, py, pc = _flip(xi, (k >> 2) & 1), _flip(yi, (k >> 1) & 1), _flip(ci, k & 1)
            cp = pltpu.make_async_remote_copy(
                src_ref=x_ref, dst_ref=out_ref.at[me], send_sem=send_sems.at[k - 1], recv_sem=recv_sems.at[k - 1],
                device_id=(px, py, pc), device_id_type=MESH)
            cp.start()
            started.append((cp, 4 * px + 2 * py + pc, k, (px, py, pc)))
        for cp, peer, k, pid in started:
            cp.wait_send()
            pltpu.make_async_remote_copy(
                src_ref=x_ref, dst_ref=out_ref.at[peer], send_sem=send_sems.at[k - 1], recv_sem=recv_sems.at[k - 1],
                device_id=pid, device_id_type=MESH).wait_recv()
        acc = out_ref[0]
        for d in range(1, N_DEV):
            acc = acc + out_ref[d]
        sum_ref[...] = acc

    return pl.pallas_call(
        body, name=name,
        out_shape=(jax.ShapeDtypeStruct((N_DEV, r, cdim), F32), jax.ShapeDtypeStruct((r, cdim), F32)),
        in_specs=[pl.BlockSpec(memory_space=pltpu.VMEM)],
        out_specs=(pl.BlockSpec(memory_space=pltpu.VMEM), pl.BlockSpec(memory_space=pltpu.VMEM)),
        scratch_shapes=[pltpu.SemaphoreType.DMA((N_DEV - 1,)), pltpu.SemaphoreType.DMA((N_DEV - 1,))],
        compiler_params=pltpu.CompilerParams(vmem_limit_bytes=VMEM_LIMIT),
    )(xv)


_HBM = pl.BlockSpec(memory_space=pltpu.HBM)
_SEM = pl.BlockSpec(memory_space=pltpu.SEMAPHORE)
_EFFECT = pltpu.SideEffectType.DATAFLOW_SIDE_EFFECTING
_CHIP_FLIPS = ((1, 0), (0, 1), (1, 1))


def _chip_copies(src_ref, land_ref, send_sems, recv_sems, scatter, halves=False):
    xi, yi, ci = lax.axis_index("x"), lax.axis_index("y"), lax.axis_index("c")
    me = 2 * xi + yi
    out = []

    def slot(j):
        return land_ref.at[j, ci] if halves else land_ref.at[j]

    for k, (dx, dy) in enumerate(_CHIP_FLIPS):
        px, py = _flip(xi, dx), _flip(yi, dy)
        peer = 2 * px + py
        src = src_ref.at[peer] if scatter else slot(me)
        mk = functools.partial(pltpu.make_async_remote_copy, src_ref=src, send_sem=send_sems.at[k],
                               recv_sem=recv_sems.at[k], device_id=(px, py, ci), device_id_type=MESH)
        out.append((mk(dst_ref=slot(me)), mk(dst_ref=slot(peer))))
    return out


def _fill_copies(land_ref, send_sems, recv_sems):
    xi, yi, ci = lax.axis_index("x"), lax.axis_index("y"), lax.axis_index("c")
    out = []
    for k, (dx, dy) in enumerate(_CHIP_FLIPS):
        peer = 2 * _flip(xi, dx) + _flip(yi, dy)
        mk = functools.partial(pltpu.make_async_remote_copy, src_ref=land_ref.at[peer, ci], send_sem=send_sems.at[k],
                               recv_sem=recv_sems.at[k], device_id=(xi, yi, 1 - ci), device_id_type=MESH)
        out.append((mk(dst_ref=land_ref.at[peer, ci]), mk(dst_ref=land_ref.at[peer, 1 - ci])))
    return out


def _fill_start(land, name):
    def body(land_ref, send_sems, recv_sems, land_thru, token):
        for mine, _ in _fill_copies(land_ref, send_sems, recv_sems):
            mine.start()
        token[...] = jnp.zeros_like(token)

    return pl.pallas_call(
        body, name=name,
        out_shape=(pltpu.SemaphoreType.DMA((3,)), pltpu.SemaphoreType.DMA((3,)), pltpu.HBM(land.shape, land.dtype),
                   jax.ShapeDtypeStruct((8, 128), F32)),
        in_specs=(_HBM,), out_specs=(_SEM, _SEM, _HBM, pl.BlockSpec(memory_space=pltpu.VMEM)),
        input_output_aliases={0: 2},
        compiler_params=pltpu.CompilerParams(has_side_effects=_EFFECT),
    )(land)


def _fill_wait(handle, after, name):
    send, recv, land_thru = handle

    def body(land_ref, send_sems, recv_sems, after_ref, land_out):
        for mine, theirs in _fill_copies(land_ref, send_sems, recv_sems):
            mine.wait_send()
            theirs.wait_recv()

    return pl.pallas_call(
        body, name=name, out_shape=pltpu.HBM(land_thru.shape, land_thru.dtype),
        in_specs=(_HBM, _SEM, _SEM, pl.BlockSpec(memory_space=pl.ANY)), out_specs=_HBM,
        input_output_aliases={0: 0},
        compiler_params=pltpu.CompilerParams(has_side_effects=_EFFECT),
    )(land_thru, send, recv, after)


def _gather_start(land, name, halves=False):
    def body(land_ref, send_sems, recv_sems, land_thru, token):
        for mine, _ in _chip_copies(None, land_ref, send_sems, recv_sems, False, halves):
            mine.start()
        token[...] = jnp.zeros_like(token)

    return pl.pallas_call(
        body, name=name,
        out_shape=(pltpu.SemaphoreType.DMA((3,)), pltpu.SemaphoreType.DMA((3,)), pltpu.HBM(land.shape, land.dtype),
                   jax.ShapeDtypeStruct((8, 128), F32)),
        in_specs=(_HBM,), out_specs=(_SEM, _SEM, _HBM, pl.BlockSpec(memory_space=pltpu.VMEM)),
        input_output_aliases={0: 2},
        compiler_params=pltpu.CompilerParams(has_side_effects=_EFFECT),
    )(pltpu.with_memory_space_constraint(land, pltpu.HBM))


def _gather_wait(handle, after, name, halves=False):
    send, recv, land_thru = handle

    def body(land_ref, send_sems, recv_sems, after_ref, land_out):
        for mine, theirs in _chip_copies(None, land_ref, send_sems, recv_sems, False, halves):
            mine.wait_send()
            theirs.wait_recv()

    return pl.pallas_call(
        body, name=name, out_shape=pltpu.HBM(land_thru.shape, land_thru.dtype),
        in_specs=(_HBM, _SEM, _SEM, pl.BlockSpec(memory_space=pl.ANY)), out_specs=_HBM,
        input_output_aliases={0: 0},
        compiler_params=pltpu.CompilerParams(has_side_effects=_EFFECT),
    )(land_thru, send, recv, after)


def _scatter_start(src, name, after=None):
    extra = [] if after is None else [after]

    def body(src_ref, land_ref, *rest):
        send_sems, recv_sems, src_thru, land_thru, token = rest[len(extra):]
        for mine, _ in _chip_copies(src_ref, land_ref, send_sems, recv_sems, True):
            mine.start()
        token[...] = jnp.zeros_like(token)

    return pl.pallas_call(
        body, name=name,
        out_shape=(pltpu.SemaphoreType.DMA((3,)), pltpu.SemaphoreType.DMA((3,)), pltpu.HBM(src.shape, src.dtype),
                   pltpu.HBM(src.shape, src.dtype), jax.ShapeDtypeStruct((8, 128), F32)),
        in_specs=(_HBM, _HBM) + (pl.BlockSpec(memory_space=pl.ANY),) * len(extra),
        out_specs=(_SEM, _SEM, _HBM, _HBM, pl.BlockSpec(memory_space=pltpu.VMEM)),
        input_output_aliases={0: 2, 1: 3},
        compiler_params=pltpu.CompilerParams(has_side_effects=_EFFECT),
    )(pltpu.with_memory_space_constraint(src, pltpu.HBM),
      pltpu.with_memory_space_constraint(lax.empty(src.shape, src.dtype), pltpu.HBM), *extra)


def _scatter_wait(handle, after, name):
    send, recv, src_thru, land_thru = handle

    def body(src_ref, land_ref, send_sems, recv_sems, after_ref, src_out, land_out):
        for mine, theirs in _chip_copies(src_ref, land_ref, send_sems, recv_sems, True):
            mine.wait_send()
            theirs.wait_recv()

    return pl.pallas_call(
        body, name=name,
        out_shape=(pltpu.HBM(src_thru.shape, src_thru.dtype), pltpu.HBM(land_thru.shape, land_thru.dtype)),
        in_specs=(_HBM, _HBM, _SEM, _SEM, pl.BlockSpec(memory_space=pl.ANY)), out_specs=(_HBM, _HBM),
        input_output_aliases={0: 0, 1: 1},
        compiler_params=pltpu.CompilerParams(has_side_effects=_EFFECT),
    )(src_thru, land_thru, send, recv, after)


def _sibling_copy(src_ref, land_ref, send_sems, recv_sems):
    xi, yi, ci = lax.axis_index("x"), lax.axis_index("y"), lax.axis_index("c")
    return pltpu.make_async_remote_copy(src_ref=src_ref, dst_ref=land_ref, send_sem=send_sems.at[0],
                                        recv_sem=recv_sems.at[0], device_id=(xi, yi, 1 - ci), device_id_type=MESH)


def _sibling_start(src, name):
    def body(src_ref, land_ref, send_sems, recv_sems, src_thru, land_thru, token):
        _sibling_copy(src_ref, land_ref, send_sems, recv_sems).start()
        token[...] = jnp.zeros_like(token)

    return pl.pallas_call(
        body, name=name,
        out_shape=(pltpu.SemaphoreType.DMA((1,)), pltpu.SemaphoreType.DMA((1,)), pltpu.HBM(src.shape, src.dtype),
                   pltpu.HBM(src.shape, src.dtype), jax.ShapeDtypeStruct((8, 128), F32)),
        in_specs=(_HBM, _HBM), out_specs=(_SEM, _SEM, _HBM, _HBM, pl.BlockSpec(memory_space=pltpu.VMEM)),
        input_output_aliases={0: 2, 1: 3},
        compiler_params=pltpu.CompilerParams(has_side_effects=_EFFECT),
    )(pltpu.with_memory_space_constraint(src, pltpu.HBM),
      pltpu.with_memory_space_constraint(lax.empty(src.shape, src.dtype), pltpu.HBM))


def _sibling_wait(handle, after, name):
    send, recv, src_thru, land_thru = handle

    def body(src_ref, land_ref, send_sems, recv_sems, after_ref, src_out, land_out):
        cp = _sibling_copy(src_ref, land_ref, send_sems, recv_sems)
        cp.wait_send()
        cp.wait_recv()

    return pl.pallas_call(
        body, name=name,
        out_shape=(pltpu.HBM(src_thru.shape, src_thru.dtype), pltpu.HBM(land_thru.shape, land_thru.dtype)),
        in_specs=(_HBM, _HBM, _SEM, _SEM, pl.BlockSpec(memory_space=pl.ANY)), out_specs=(_HBM, _HBM),
        input_output_aliases={0: 0, 1: 1},
        compiler_params=pltpu.CompilerParams(has_side_effects=_EFFECT),
    )(src_thru, land_thru, send, recv, after)


def _pick(dim, pref):
    if dim <= pref:
        return dim
    t = (pref // 128) * 128
    while t >= 128:
        if dim % t == 0:
            return t
        t -= 128
    return dim


def _matmul(a, b, dims, out_dtype, name, tm=512, tn=1024, tk=2048, out_chips=False, b_chips=False):
    a_parts = a if isinstance(a, tuple) else (a,)
    b_parts = b if isinstance(b, tuple) else (b,)
    na, nb = len(a_parts), len(b_parts)
    assert (na == 1 or dims == "nt") and (nb == 1 or dims == "tn")
    b_shape = (b_parts[0].shape[1], N_CHIPS * b_parts[0].shape[2]) if b_chips else b_parts[0].shape
    if dims == "nn":
        (m, kd), (_, n) = a_parts[0].shape, b_shape
    elif dims == "nt":
        (m, kd), (n, _) = a_parts[0].shape, b_shape
        kd = na * kd
    else:
        (kd, m), (_, n) = a_parts[0].shape, b_shape
        n = nb * n
    tm = _pick(m, tm)
    tn = _pick(n // N_CHIPS, tn) if (out_chips or (b_chips and dims == "nn")) else _pick(n // nb, tn)
    tk = _pick(kd // N_CHIPS, tk) if (b_chips and dims == "nt") else _pick(kd // na, tk)
    nk, nj = kd // tk, n // tn
    ka, jb = nk // na, nj // nb
    if out_chips:
        per_chip = n // N_CHIPS // tn
        out_shape = jax.ShapeDtypeStruct((N_CHIPS, m, n // N_CHIPS), out_dtype)
        out_spec = pl.BlockSpec((None, tm, tn), lambda j, i, k: (j // per_chip, i, j % per_chip))
    else:
        out_shape = jax.ShapeDtypeStruct((m, n), out_dtype)
        out_spec = pl.BlockSpec((tm, tn), lambda j, i, k: (i, j))
    def part_of(idx, first, count):
        return jnp.clip(idx - first, 0, count - 1)

    if dims == "nn":
        a_specs = [pl.BlockSpec((tm, tk), lambda j, i, k: (i, k))]
        b_specs = [pl.BlockSpec((tk, tn), lambda j, i, k: (k, j))]
        dn = (((1,), (0,)), ((), ()))
    elif dims == "nt":
        a_specs = [pl.BlockSpec((tm, tk), lambda j, i, k, p=p: (i, part_of(k, p * ka, ka))) for p in range(na)]
        b_specs = [pl.BlockSpec((tn, tk), lambda j, i, k: (j, k))]
        dn = (((1,), (1,)), ((), ()))
    else:
        a_specs = [pl.BlockSpec((tk, tm), lambda j, i, k: (k, i))]
        b_specs = [pl.BlockSpec((tk, tn), lambda j, i, k, p=p: (k, part_of(j, p * jb, jb))) for p in range(nb)]
        dn = (((0,), (0,)), ((), ()))
    if b_chips and dims == "nn":
        nper = n // N_CHIPS // tn
        b_specs = [pl.BlockSpec((None, tk, tn), lambda j, i, k: (j // nper, k, j % nper))]
    elif b_chips:
        kper = kd // N_CHIPS // tk
        b_specs = [pl.BlockSpec((None, tn, tk), lambda j, i, k: (k // kper, j, k % kper))]
    direct = nk == 1 or out_dtype == F32

    def body(*refs):
        a_refs, b_refs, o_ref = refs[:na], refs[na:na + nb], refs[na + nb]
        acc_ref = o_ref if direct else refs[na + nb + 1]
        j, k = pl.program_id(0), pl.program_id(2)

        def step(a_ref, b_ref):
            part = lax.dot_general(a_ref[...].astype(BF16), b_ref[...].astype(BF16), dn, preferred_element_type=F32)
            if nk == 1:
                o_ref[...] = part.astype(o_ref.dtype)
                return

            @pl.when(k == 0)
            def _():
                acc_ref[...] = part

            @pl.when(k > 0)
            def _():
                acc_ref[...] += part

            if not direct:
                @pl.when(k == nk - 1)
                def _():
                    o_ref[...] = acc_ref[...].astype(o_ref.dtype)

        if na == 1 and nb == 1:
            step(a_refs[0], b_refs[0])
        for p in range(na if na > 1 else 0):
            pl.when(jnp.logical_and(k >= p * ka, k < (p + 1) * ka))(functools.partial(step, a_refs[p], b_refs[0]))
        for p in range(nb if nb > 1 else 0):
            pl.when(jnp.logical_and(j >= p * jb, j < (p + 1) * jb))(functools.partial(step, a_refs[0], b_refs[p]))

    return pl.pallas_call(
        body, name=name, out_shape=out_shape,
        grid=(nj, m // tm, nk),
        in_specs=a_specs + b_specs,
        out_specs=out_spec,
        scratch_shapes=[] if direct else [pltpu.VMEM((tm, tn), F32)],
        compiler_params=_params(("parallel", "parallel", "arbitrary")),
    )(*a_parts, *b_parts)


def _rstd(v):
    return lax.rsqrt(jnp.mean(v * v, axis=-1, keepdims=True) + EPS)


def _row(tm):
    return pl.BlockSpec((tm, D_MODEL), lambda i: (i, 0))


_VEC = pl.BlockSpec((1, D_MODEL), lambda i: (0, 0))


def _pre_norm(x, gn, sc, sh, name):
    t = x.shape[0]
    tm = min(TM_ROW, t)

    def body(x_ref, gn_ref, sc_ref, sh_ref, h_ref):
        xv = x_ref[...]
        h_ref[...] = ((xv * _rstd(xv) * gn_ref[...]) * (1.0 + sc_ref[...]) + sh_ref[...]).astype(BF16)

    return pl.pallas_call(
        body, name=name, out_shape=jax.ShapeDtypeStruct((t, D_MODEL), BF16), grid=(t // tm,),
        in_specs=[_row(tm), _VEC, _VEC, _VEC], out_specs=_row(tm),
        compiler_params=_params(("parallel",)),
    )(x, gn, sc, sh)


def _post_pre(x, y, g, gnp, gn, sc, sh, name):
    t = x.shape[0]
    tm = min(TM_ROW, t)

    def body(x_ref, y_ref, g_ref, gnp_ref, gn_ref, sc_ref, sh_ref, x1_ref, h_ref):
        yv = y_ref[...]
        x1 = x_ref[...] + g_ref[...] * (yv * _rstd(yv) * gnp_ref[...])
        x1_ref[...] = x1
        h_ref[...] = ((x1 * _rstd(x1) * gn_ref[...]) * (1.0 + sc_ref[...]) + sh_ref[...]).astype(BF16)

    return pl.pallas_call(
        body, name=name,
        out_shape=(jax.ShapeDtypeStruct((t, D_MODEL), F32), jax.ShapeDtypeStruct((t, D_MODEL), BF16)),
        grid=(t // tm,),
        in_specs=[_row(tm), _row(tm), _VEC, _VEC, _VEC, _VEC, _VEC], out_specs=(_row(tm), _row(tm)),
        compiler_params=_params(("parallel",)),
    )(x, y, g, gnp, gn, sc, sh)


def _post_loss(x, y, g, gnp, tgt, name):
    t = x.shape[0]
    tm = min(TM_ROW, t)

    def body(x_ref, y_ref, g_ref, gnp_ref, t_ref, dx_ref, loss_ref):
        yv = y_ref[...]
        diff = x_ref[...] + g_ref[...] * (yv * _rstd(yv) * gnp_ref[...]) - t_ref[...]
        dx_ref[...] = diff * (1.0 / D_MODEL)
        part = (0.5 / D_MODEL) * jnp.sum(jnp.sum(diff * diff, axis=-1, keepdims=True), axis=0, keepdims=True)

        @pl.when(pl.program_id(0) == 0)
        def _():
            loss_ref[...] = jnp.zeros_like(loss_ref)

        loss_ref[...] += jnp.broadcast_to(part, loss_ref.shape)

    return pl.pallas_call(
        body, name=name,
        out_shape=(jax.ShapeDtypeStruct((t, D_MODEL), F32), jax.ShapeDtypeStruct((8, 128), F32)),
        grid=(t // tm,),
        in_specs=[_row(tm), _row(tm), _VEC, _VEC, _row(tm)],
        out_specs=(_row(tm), pl.BlockSpec((8, 128), lambda i: (0, 0))),
        compiler_params=_params(("arbitrary",)),
    )(x, y, g, gnp, tgt)


def _acc_rows(ref, val):
    @pl.when(pl.program_id(0) == 0)
    def _():
        ref[...] = jnp.zeros_like(ref)

    ref[...] += jnp.sum(val, axis=0, keepdims=True)


def _post_bwd(dxn, y, g, gnp, name):
    t = y.shape[0]
    tm = min(TM_ROW, t)

    def body(dx_ref, y_ref, g_ref, gnp_ref, dy_ref, dg_ref, dgn_ref):
        yv, dxv = y_ref[...], dx_ref[...]
        r = _rstd(yv)
        yh = yv * r
        _acc_rows(dg_ref, dxv * (yh * gnp_ref[...]))
        dn = dxv * g_ref[...]
        _acc_rows(dgn_ref, dn * yh)
        dyh = dn * gnp_ref[...]
        dy_ref[...] = (r * (dyh - yh * jnp.mean(dyh * yh, axis=-1, keepdims=True))).astype(BF16)

    return pl.pallas_call(
        body, name=name,
        out_shape=(jax.ShapeDtypeStruct((t, D_MODEL), BF16), jax.ShapeDtypeStruct((1, D_MODEL), F32),
                   jax.ShapeDtypeStruct((1, D_MODEL), F32)),
        grid=(t // tm,),
        in_specs=[_row(tm), _row(tm), _VEC, _VEC], out_specs=(_row(tm), _VEC, _VEC),
        compiler_params=_params(("arbitrary",)),
    )(dxn, y, g, gnp)


def _pre_bwd(dh, xin, dres, gn, sc, name):
    t = xin.shape[0]
    tm = min(TM_ROW, t)

    def body(dh_ref, x_ref, dres_ref, gn_ref, sc_ref, dx_ref, dsh_ref, dsc_ref, dgn_ref):
        xv, dhv = x_ref[...], dh_ref[...]
        r = _rstd(xv)
        xh = xv * r
        _acc_rows(dsh_ref, dhv)
        _acc_rows(dsc_ref, dhv * (xh * gn_ref[...]))
        dn = dhv * (1.0 + sc_ref[...])
        _acc_rows(dgn_ref, dn * xh)
        dxh = dn * gn_ref[...]
        dx_ref[...] = dres_ref[...] + r * (dxh - xh * jnp.mean(dxh * xh, axis=-1, keepdims=True))

    vec = jax.ShapeDtypeStruct((1, D_MODEL), F32)
    return pl.pallas_call(
        body, name=name, out_shape=(jax.ShapeDtypeStruct((t, D_MODEL), F32), vec, vec, vec),
        grid=(t // tm,),
        in_specs=[_row(tm), _row(tm), _row(tm), _VEC, _VEC], out_specs=(_row(tm), _VEC, _VEC, _VEC),
        compiler_params=_params(("arbitrary",)),
    )(dh, xin, dres, gn, sc)


def _fix_rows(v8, rows):
    idx = lax.broadcasted_iota(jnp.int32, v8.shape, 0)
    for j, val in rows:
        v8 = jnp.where(idx == j, jnp.broadcast_to(val, v8.shape), v8)
    return v8


def _shift_down(v, halo, s):
    hr, tm = halo.shape[0], v.shape[0]
    out = pltpu.roll(v, s, 0)
    if tm == 8:
        return _fix_rows(out, [(j, halo[hr - s + j:hr - s + j + 1, :]) for j in range(s)])
    head = _fix_rows(out[0:8, :], [(j, halo[hr - s + j:hr - s + j + 1, :]) for j in range(s)])
    return jnp.concatenate([head, out[8:, :]], axis=0)


def _shift_up(v, halo, s):
    tm = v.shape[0]
    out = pltpu.roll(v, tm - s, 0)
    tail = _fix_rows(out[tm - 8:, :], [(8 - s + j, halo[j:j + 1, :]) for j in range(s)])
    return jnp.concatenate([out[:tm - 8, :], tail], axis=0)


def _tile_specs(tm, cw, off, nrow, hr=8):
    ob = off // cw
    per = tm // hr
    main = pl.BlockSpec((tm, cw), lambda j, i: (i, ob + j))
    prev = pl.BlockSpec((hr, cw), lambda j, i: (jnp.maximum(i * per - 1, 0), ob + j))
    nxt = pl.BlockSpec((hr, cw), lambda j, i: (jnp.minimum((i + 1) * per, nrow * per - 1), ob + j))
    return main, prev, nxt


def _conv_fwd(p, w, name):
    t = p.shape[0]
    tm, cw = min(TM_EW, t), CW_EW
    nrow = t // tm
    cb_s, _, _ = _tile_specs(tm, cw, OFF_CB, nrow, 16)
    cc_s, cc_p, _ = _tile_specs(tm, cw, OFF_CC, nrow, 16)
    cx_s, cx_p, _ = _tile_specs(tm, cw, OFF_CX, nrow, 16)

    def body(cb_ref, cc_ref, ccp_ref, cx_ref, cxp_ref, w_ref, z_ref):
        u = _ld(cc_ref) * _ld(cx_ref)
        uh = _ld(ccp_ref) * _ld(cxp_ref) * _unless(pl.program_id(1) == 0)
        wv = w_ref[...]
        conv = wv[2:3, :] * u + wv[1:2, :] * _shift_down(u, uh, 1) + wv[0:1, :] * _shift_down(u, uh, 2)
        z_ref[...] = (_ld(cb_ref) * conv).astype(BF16)

    return pl.pallas_call(
        body, name=name, out_shape=jax.ShapeDtypeStruct((t, CONV_WIDTH), BF16),
        grid=(CONV_WIDTH // cw, nrow),
        in_specs=[cb_s, cc_s, cc_p, cx_s, cx_p, pl.BlockSpec((8, cw), lambda j, i: (0, j))],
        out_specs=pl.BlockSpec((tm, cw), lambda j, i: (i, j)),
        compiler_params=_params(("parallel", "arbitrary")),
    )(p, p, p, p, p, w)


def _acc_w(ref, vals):
    @pl.when(pl.program_id(1) == 0)
    def _():
        ref[...] = jnp.zeros_like(ref)

    for j, v in enumerate(vals):
        ref[j:j + 1, :] += jnp.sum(v, axis=0, keepdims=True)


def _conv_bwd(dz, p, w, dp, name):
    t = p.shape[0]
    tm, cw = min(TM_EW // 2, t), CONV_WIDTH
    nrow = t // tm
    dz_s, _, dz_n = _tile_specs(tm, cw, 0, nrow)
    cb_s, _, cb_n = _tile_specs(tm, cw, OFF_CB, nrow, 16)
    cc_s, cc_p, _ = _tile_specs(tm, cw, OFF_CC, nrow, 16)
    cx_s, cx_p, _ = _tile_specs(tm, cw, OFF_CX, nrow, 16)

    def body(dz_ref, dzn_ref, cb_ref, cbn_ref, cc_ref, ccp_ref, cx_ref, cxp_ref, w_ref, dp_in, dp_ref, dw_ref):
        dcb_ref = dp_ref.at[:, 0:cw]
        dcc_ref = dp_ref.at[:, cw:2 * cw]
        dcx_ref = dp_ref.at[:, 2 * cw:3 * cw]
        i = pl.program_id(1)
        ccv, cxv, dzv = _ld(cc_ref), _ld(cx_ref), dz_ref[...]
        u = ccv * cxv
        uh = _ld(ccp_ref) * _ld(cxp_ref) * _unless(i == 0)
        wv = w_ref[...]
        u1, u2 = _shift_down(u, uh, 1), _shift_down(u, uh, 2)
        conv = wv[2:3, :] * u + wv[1:2, :] * u1 + wv[0:1, :] * u2
        dcb_ref[...] = (dzv * conv).astype(BF16)
        dconv = dzv * _ld(cb_ref)
        dch = dzn_ref[...] * _ld(cbn_ref)[0:8, :] * _unless(i == nrow - 1)
        du = wv[2:3, :] * dconv + wv[1:2, :] * _shift_up(dconv, dch, 1) + wv[0:1, :] * _shift_up(dconv, dch, 2)
        dcc_ref[...] = (du * cxv).astype(BF16)
        dcx_ref[...] = (du * ccv).astype(BF16)
        _acc_w(dw_ref, (dconv * u2, dconv * u1, dconv * u))

    w_s = pl.BlockSpec((8, cw), lambda j, i: (0, j))
    return pl.pallas_call(
        body, name=name, out_shape=(_dp_shape(t), jax.ShapeDtypeStruct((8, CONV_WIDTH), F32)),
        grid=(1, nrow),
        in_specs=[dz_s, dz_n, cb_s, cb_n, cc_s, cc_p, cx_s, cx_p, w_s, _ANY],
        out_specs=(pl.BlockSpec((tm, 3 * cw), lambda j, i: (i, OFF_CB // (3 * cw))), w_s),
        input_output_aliases={9: 0},
        compiler_params=_params(("parallel", "arbitrary")),
    )(dz, dz, p, p, p, p, p, p, w, dp)


def _ffn_fwd(u, w, name):
    t = u.shape[0]
    tm, cw = min(TM_EW, t), CW_EW
    nrow = t // tm
    g_s, g_p, _ = _tile_specs(tm, cw, 0, nrow, 16)
    u_s, _, _ = _tile_specs(tm, cw, D_FF, nrow, 16)

    def body(g_ref, gp_ref, u_ref, w_ref, f_ref):
        gv = _ld(g_ref)
        gh = _ld(gp_ref) * _unless(pl.program_id(1) == 0)
        wv = w_ref[...]
        gc = wv[2:3, :] * gv + wv[1:2, :] * _shift_down(gv, gh, 1) + wv[0:1, :] * _shift_down(gv, gh, 2)
        f_ref[...] = (_gelu_and_grad(gc)[0] * _ld(u_ref)).astype(BF16)

    return pl.pallas_call(
        body, name=name, out_shape=jax.ShapeDtypeStruct((t, D_FF), BF16),
        grid=(D_FF // cw, nrow),
        in_specs=[g_s, g_p, u_s, pl.BlockSpec((8, cw), lambda j, i: (0, j))],
        out_specs=pl.BlockSpec((tm, cw), lambda j, i: (i, j)),
        compiler_params=_params(("parallel", "arbitrary")),
    )(u, u, u, w)


def _ffn_bwd(df, u, w, name):
    t = u.shape[0]
    tm, cw = min(TM_EW, t), CW_EW
    nrow = t // tm
    df_s, _, df_n = _tile_specs(tm, cw, 0, nrow)
    g_s, g_p, g_n = _tile_specs(tm, cw, 0, nrow, 16)
    u_s, _, u_n = _tile_specs(tm, cw, D_FF, nrow, 16)

    def body(df_ref, dfn_ref, g_ref, gp_ref, gn_ref, u_ref, un_ref, w_ref, dg_ref, du_ref, dw_ref):
        i = pl.program_id(1)
        gv, dfv, uv = _ld(g_ref), df_ref[...], _ld(u_ref)
        gh = _ld(gp_ref) * _unless(i == 0)
        wv = w_ref[...]
        g1, g2 = _shift_down(gv, gh, 1), _shift_down(gv, gh, 2)
        gc = wv[2:3, :] * gv + wv[1:2, :] * g1 + wv[0:1, :] * g2
        act, act_grad = _gelu_and_grad(gc)
        du_ref[...] = (dfv * act).astype(BF16)
        dgc = dfv * uv * act_grad
        gnv = _ld(gn_ref)[0:8, :]
        gtail = gv[tm - 8:tm, :]
        gcn = (wv[2:3, :] * gnv + wv[1:2, :] * _shift_down(gnv, gtail, 1) + wv[0:1, :] * _shift_down(gnv, gtail, 2))
        dgcn = dfn_ref[...] * _ld(un_ref)[0:8, :] * _gelu_and_grad(gcn)[1] * _unless(i == nrow - 1)
        dg = wv[2:3, :] * dgc + wv[1:2, :] * _shift_up(dgc, dgcn, 1) + wv[0:1, :] * _shift_up(dgc, dgcn, 2)
        dg_ref[...] = dg.astype(BF16)
        _acc_w(dw_ref, (dgc * g2, dgc * g1, dgc * gv))

    o_s = pl.BlockSpec((tm, cw), lambda j, i: (i, j))
    o_sh = jax.ShapeDtypeStruct((t, D_FF), BF16)
    w_s = pl.BlockSpec((8, cw), lambda j, i: (0, j))
    return pl.pallas_call(
        body, name=name, out_shape=(o_sh, o_sh, jax.ShapeDtypeStruct((8, D_FF), F32)),
        grid=(D_FF // cw, nrow),
        in_specs=[df_s, df_n, g_s, g_p, g_n, u_s, u_n, w_s],
        out_specs=(o_s, o_s, w_s),
        compiler_params=_params(("parallel", "arbitrary")),
    )(df, df, u, u, u, u, u, w)


def _merge_fwd(ya, yb, p, name):
    t = ya.shape[0]
    tm, cw = min(TM_EW, t), CW_EW
    y_s = pl.BlockSpec((tm, cw), lambda i, j: (i, j))

    def body(ya_ref, yb_ref, ga_ref, gb_ref, m_ref):
        m_ref[...] = (_sigmoid(_ld(ga_ref)) * ya_ref[...] + _sigmoid(_ld(gb_ref)) * yb_ref[...]).astype(BF16)

    return pl.pallas_call(
        body, name=name, out_shape=jax.ShapeDtypeStruct((t, D_MODEL), BF16),
        grid=(t // tm, D_MODEL // cw),
        in_specs=[y_s, y_s, pl.BlockSpec((tm, cw), lambda i, j: (i, OFF_GA // cw + j)),
                  pl.BlockSpec((tm, cw), lambda i, j: (i, OFF_GB // cw + j))],
        out_specs=y_s, compiler_params=_params(("parallel", "parallel")),
    )(ya, yb, p, p)


_ANY = pl.BlockSpec(memory_space=pl.ANY)


def _dp_shape(t):
    return jax.ShapeDtypeStruct((t, N_IN_PAD), BF16)


def _merge_bwd(dm, y, p, gate_off, dp, name):
    t = y.shape[0]
    tm, cw = min(TM_EW, t), CW_EW
    y_s = pl.BlockSpec((tm, cw), lambda i, j: (i, j))
    g_s = pl.BlockSpec((tm, cw), lambda i, j: (i, gate_off // cw + j))

    def body(dm_ref, y_ref, g_ref, *rest):
        dy_ref, dp_ref = rest[-2:]
        dmv = dm_ref[...]
        sg = _sigmoid(_ld(g_ref))
        dy_ref[...] = (dmv * sg).astype(BF16)
        dp_ref[...] = (dmv * y_ref[...] * sg * (1.0 - sg)).astype(BF16)

    extra = [] if dp is None else [dp]
    return pl.pallas_call(
        body, name=name, out_shape=(jax.ShapeDtypeStruct((t, D_MODEL), BF16), _dp_shape(t)),
        grid=(t // tm, D_MODEL // cw),
        in_specs=[y_s, y_s, g_s] + [_ANY] * len(extra),
        out_specs=(y_s, g_s), input_output_aliases={} if dp is None else {3: 1},
        compiler_params=_params(("parallel", "parallel")),
    )(dm, y, p, *extra)


def _tri(lower):
    r = lax.broadcasted_iota(jnp.int32, (CHUNK, CHUNK), 0)
    c = lax.broadcasted_iota(jnp.int32, (CHUNK, CHUNK), 1)
    return ((c <= r) if lower else (c >= r)).astype(F32)


def _eye_mask():
    r = lax.broadcasted_iota(jnp.int32, (GLA_DK, GLA_DK), 0)
    c = lax.broadcasted_iota(jnp.int32, (GLA_DK, GLA_DK), 1)
    return r == c


def _row_to_col(v):
    return jnp.sum(jnp.where(_eye_mask(), jnp.broadcast_to(v, (GLA_DK, GLA_DK)), 0.0), axis=1, keepdims=True)


def _col_to_row(v):
    return jnp.sum(jnp.where(_eye_mask(), jnp.broadcast_to(v, (GLA_DK, GLA_DK)), 0.0), axis=0, keepdims=True)


def _dot(a, b, dn):
    return lax.dot_general(a.astype(BF16), b.astype(BF16), (dn, ((), ())), preferred_element_type=F32)


_NN = ((1,), (0,))
_NT = ((1,), (1,))
_TN = ((0,), (0,))


def _gate_logits(lr_ref, wa_ref, ba_ref):
    return _dot(lr_ref[...], wa_ref[...], _NN) + ba_ref[...]


def _chunk_decay(la, tri):
    cum = lax.dot_general(tri, la, ((_NN), ((), ())), precision=lax.Precision.HIGHEST, preferred_element_type=F32)
    e = cum[CHUNK - 1:CHUNK, :]
    return cum, e, jnp.exp(e - cum)


def _gla_fwd(p, wa, ba, name):
    t = p.shape[0]
    rows = min(GLA_ROWS, t)
    cb = rows // CHUNK
    nc = t // CHUNK
    scale = GLA_DK ** -0.5

    def body(q_ref, k_ref, v_ref, lr_ref, wa_ref, ba_ref, o_ref, st_ref, s_scr):
        @pl.when(pl.program_id(0) == 0)
        def _():
            s_scr[...] = jnp.zeros_like(s_scr)

        la_all = _log_sigmoid(_gate_logits(lr_ref, wa_ref, ba_ref)) * (1.0 / GLA_TAU)
        tri = _tri(True)
        for ch in range(cb):
            rs = slice(ch * CHUNK, (ch + 1) * CHUNK)
            for h in range(GLA_HEADS):
                ks = slice(h * GLA_DK, (h + 1) * GLA_DK)
                vs = slice(h * GLA_DV, (h + 1) * GLA_DV)
                _, e, w = _chunk_decay(la_all[rs, ks], tri)
                kd = k_ref[rs, ks].astype(F32) * w
                s_new = _row_to_col(jnp.exp(e)) * s_scr[ks, :] + _dot(kd, v_ref[rs, vs], _TN)
                s_scr[ks, :] = s_new
                st_ref[ch, ks, :] = s_new
                o_ref[rs, vs] = _dot(q_ref[rs, ks].astype(F32) * scale, s_new, _NN)

    return pl.pallas_call(
        body, name=name,
        out_shape=(jax.ShapeDtypeStruct((t, GLA_V), F32), jax.ShapeDtypeStruct((nc, GLA_QK, GLA_DV), F32)),
        grid=(t // rows,),
        in_specs=[pl.BlockSpec((rows, GLA_QK), lambda i: (i, OFF_Q // GLA_QK)),
                  pl.BlockSpec((rows, GLA_QK), lambda i: (i, OFF_K // GLA_QK)),
                  pl.BlockSpec((rows, GLA_V), lambda i: (i, OFF_V // GLA_V)),
                  pl.BlockSpec((rows, LR_PAD), lambda i: (i, OFF_LR // LR_PAD)),
                  pl.BlockSpec((LR_PAD, GLA_QK), lambda i: (0, 0)),
                  pl.BlockSpec((1, GLA_QK), lambda i: (0, 0))],
        out_specs=(pl.BlockSpec((rows, GLA_V), lambda i: (i, 0)),
                   pl.BlockSpec((cb, GLA_QK, GLA_DV), lambda i: (i, 0, 0))),
        scratch_shapes=[pltpu.VMEM((GLA_QK, GLA_DV), F32)],
        compiler_params=_params(("arbitrary",)),
    )(p, p, p, p, wa, ba)


def _gla_bwd(do, p, st, wa, ba, dp, name):
    t = p.shape[0]
    rows = min(GLA_ROWS, t)
    cb = rows // CHUNK
    nb = t // rows
    scale = GLA_DK ** -0.5

    def rev(i):
        return nb - 1 - i

    def body(do_ref, q_ref, k_ref, v_ref, lr_ref, st_ref, stp_ref, wa_ref, ba_ref, dp_in,
             dp_ref, dlr_ref, dwa_ref, dba_ref, ds_scr, dz_scr):
        dq_ref = dp_ref.at[:, OFF_Q:OFF_Q + GLA_QK]
        dk_ref = dp_ref.at[:, OFF_K:OFF_K + GLA_QK]
        dv_ref = dp_ref.at[:, OFF_V:OFF_V + GLA_V]
        i = pl.program_id(0)

        @pl.when(i == 0)
        def _():
            ds_scr[...] = jnp.zeros_like(ds_scr)
            dwa_ref[...] = jnp.zeros_like(dwa_ref)
            dba_ref[...] = jnp.zeros_like(dba_ref)

        z_all = _gate_logits(lr_ref, wa_ref, ba_ref)
        la_all = _log_sigmoid(z_all) * (1.0 / GLA_TAU)
        tri, triu = _tri(True), _tri(False)
        last_row = lax.broadcasted_iota(jnp.int32, (CHUNK, GLA_DK), 0) == CHUNK - 1
        keep_prev = _unless(i == nb - 1)
        for ch in reversed(range(cb)):
            rs = slice(ch * CHUNK, (ch + 1) * CHUNK)
            for h in range(GLA_HEADS):
                ks = slice(h * GLA_DK, (h + 1) * GLA_DK)
                vs = slice(h * GLA_DV, (h + 1) * GLA_DV)
                _, e, w = _chunk_decay(la_all[rs, ks], tri)
                kd = k_ref[rs, ks].astype(F32) * w
                exp_e = jnp.exp(e)
                s_c = st_ref[ch, ks, :]
                if ch > 0:
                    s_p = st_ref[ch - 1, ks, :]
                else:
                    s_p = stp_ref[0, ks, :] * keep_prev
                do_c = do_ref[rs, vs]
                vv = v_ref[rs, vs]
                ds_tot = ds_scr[ks, :] + _dot(q_ref[rs, ks].astype(F32) * scale, do_c, _TN)
                dq_ref[rs, ks] = (_dot(do_c, s_c, _NT) * scale).astype(BF16)
                dkd = _dot(vv, ds_tot, _NT)
                dv_ref[rs, vs] = _dot(kd, ds_tot, _NN).astype(BF16)
                dexp_col = jnp.sum(ds_tot * s_p, axis=1, keepdims=True)
                ds_scr[ks, :] = _row_to_col(exp_e) * ds_tot
                dk_ref[rs, ks] = (dkd * w).astype(BF16)
                dwt = dkd * kd
                de = jnp.sum(dwt, axis=0, keepdims=True) + _col_to_row(dexp_col) * exp_e
                dcum = jnp.where(last_row, de - dwt, -dwt)
                da = lax.dot_general(triu, dcum, (_NN, ((), ())), precision=lax.Precision.HIGHEST,
                                     preferred_element_type=F32)
                dz_scr[rs, ks] = da * (1.0 / GLA_TAU) * _sigmoid(-z_all[rs, ks])
        dz = dz_scr[...]
        dlr_ref[...] = _dot(dz, wa_ref[...], _NT).astype(BF16)
        dwa_ref[...] += _dot(lr_ref[...], dz, _TN)
        dba_ref[...] += jnp.sum(dz, axis=0, keepdims=True)

    qkv = OFF_V + GLA_V
    return pl.pallas_call(
        body, name=name,
        out_shape=(_dp_shape(t), jax.ShapeDtypeStruct((t, LR_PAD), BF16),
                   jax.ShapeDtypeStruct((LR_PAD, GLA_QK), F32), jax.ShapeDtypeStruct((1, GLA_QK), F32)),
        grid=(nb,),
        in_specs=[pl.BlockSpec((rows, GLA_V), lambda i: (rev(i), 0)),
                  pl.BlockSpec((rows, GLA_QK), lambda i: (rev(i), OFF_Q // GLA_QK)),
                  pl.BlockSpec((rows, GLA_QK), lambda i: (rev(i), OFF_K // GLA_QK)),
                  pl.BlockSpec((rows, GLA_V), lambda i: (rev(i), OFF_V // GLA_V)),
                  pl.BlockSpec((rows, LR_PAD), lambda i: (rev(i), OFF_LR // LR_PAD)),
                  pl.BlockSpec((cb, GLA_QK, GLA_DV), lambda i: (rev(i), 0, 0)),
                  pl.BlockSpec((1, GLA_QK, GLA_DV), lambda i: (jnp.maximum(rev(i) * cb - 1, 0), 0, 0)),
                  pl.BlockSpec((LR_PAD, GLA_QK), lambda i: (0, 0)),
                  pl.BlockSpec((1, GLA_QK), lambda i: (0, 0)), _ANY],
        out_specs=(pl.BlockSpec((rows, qkv), lambda i: (rev(i), 0)),
                   pl.BlockSpec((rows, LR_PAD), lambda i: (rev(i), 0)),
                   pl.BlockSpec((LR_PAD, GLA_QK), lambda i: (0, 0)),
                   pl.BlockSpec((1, GLA_QK), lambda i: (0, 0))),
        input_output_aliases={9: 0},
        scratch_shapes=[pltpu.VMEM((GLA_QK, GLA_DV), F32), pltpu.VMEM((rows, GLA_QK), F32)],
        compiler_params=_params(("arbitrary",)),
    )(do, p, p, p, p, st, st, wa, ba, dp)


def _gla_out_fwd(o, p, gng, name):
    t = o.shape[0]
    tm = min(TM_EW, t)

    def body(o_ref, r_ref, g_ref, z_ref):
        gv = g_ref[...]
        for h in range(GLA_HEADS):
            vs = slice(h * GLA_DV, (h + 1) * GLA_DV)
            ov, rv = o_ref[:, vs], r_ref[:, vs].astype(F32)
            z_ref[:, vs] = ((ov * _rstd(ov) * gv) * (rv * _sigmoid(rv))).astype(BF16)

    return pl.pallas_call(
        body, name=name, out_shape=jax.ShapeDtypeStruct((t, GLA_V), BF16), grid=(t // tm,),
        in_specs=[pl.BlockSpec((tm, GLA_V), lambda i: (i, 0)),
                  pl.BlockSpec((tm, GLA_V), lambda i: (i, OFF_R // GLA_V)),
                  pl.BlockSpec((1, GLA_DV), lambda i: (0, 0))],
        out_specs=pl.BlockSpec((tm, GLA_V), lambda i: (i, 0)),
        compiler_params=_params(("parallel",)),
    )(o, p, gng)


def _gla_out_bwd(dz, o, p, gng, dp, name):
    t = o.shape[0]
    tm = min(TM_EW, t)

    def body(dz_ref, o_ref, r_ref, g_ref, dp_in, do_ref, dr_ref, dg_ref):
        @pl.when(pl.program_id(0) == 0)
        def _():
            dg_ref[...] = jnp.zeros_like(dg_ref)

        gv = g_ref[...]
        for h in range(GLA_HEADS):
            vs = slice(h * GLA_DV, (h + 1) * GLA_DV)
            ov, rv, dzv = o_ref[:, vs], r_ref[:, vs].astype(F32), dz_ref[:, vs]
            rs = _rstd(ov)
            oh = ov * rs
            sg = _sigmoid(rv)
            dr_ref[:, vs] = (dzv * (oh * gv) * (sg * (1.0 + rv * (1.0 - sg)))).astype(BF16)
            don = dzv * (rv * sg)
            dg_ref[...] += jnp.sum(don * oh, axis=0, keepdims=True)
            doh = don * gv
            do_ref[:, vs] = rs * (doh - oh * jnp.mean(doh * oh, axis=-1, keepdims=True))

    row = pl.BlockSpec((tm, GLA_V), lambda i: (i, 0))
    r_s = pl.BlockSpec((tm, GLA_V), lambda i: (i, OFF_R // GLA_V))
    return pl.pallas_call(
        body, name=name,
        out_shape=(jax.ShapeDtypeStruct((t, GLA_V), F32), _dp_shape(t), jax.ShapeDtypeStruct((1, GLA_DV), F32)),
        grid=(t // tm,),
        in_specs=[row, row, r_s, pl.BlockSpec((1, GLA_DV), lambda i: (0, 0)), _ANY],
        out_specs=(row, r_s, pl.BlockSpec((1, GLA_DV), lambda i: (0, 0))),
        input_output_aliases={4: 1},
        compiler_params=_params(("arbitrary",)),
    )(dz, o, p, gng, dp)


def _ada_fwd(c_all, w, b, layer, name):
    n = w.shape[2]
    tn = _pick(n, 512)

    def body(c_ref, w_ref, b_ref, o_ref):
        cv = c_ref[...]
        o_ref[...] = _dot(cv * _sigmoid(cv), w_ref[...], _NN) + b_ref[...]

    return pl.pallas_call(
        body, name=name, out_shape=jax.ShapeDtypeStruct((16, n), F32), grid=(n // tn,),
        in_specs=[pl.BlockSpec((16, D_MODEL), lambda j: (0, 0)),
                  pl.BlockSpec((None, D_MODEL, tn), lambda j: (layer, 0, j)),
                  pl.BlockSpec((1, tn), lambda j: (0, j))],
        out_specs=pl.BlockSpec((16, tn), lambda j: (0, j)),
        compiler_params=_params(("parallel",)),
    )(c_all, w, b)


def _ada_bwd(c_all, dmod, name):
    n = dmod.shape[2]
    tn = _pick(n, 512)

    def body(c_ref, d_ref, o_ref):
        cv = c_ref[...]
        o_ref[...] = _dot(cv * _sigmoid(cv), d_ref[...], _TN)

    return pl.pallas_call(
        body, name=name, out_shape=jax.ShapeDtypeStruct((DEPTH, D_MODEL, n), F32), grid=(DEPTH, n // tn),
        in_specs=[pl.BlockSpec((16, D_MODEL), lambda l, j: (0, 0)),
                  pl.BlockSpec((None, 16, tn), lambda l, j: (l, 0, j))],
        out_specs=pl.BlockSpec((None, D_MODEL, tn), lambda l, j: (l, 0, j)),
        compiler_params=_params(("parallel", "parallel")),
    )(c_all, dmod)


def _rows_tile(nrows, ncols, target_bytes):
    want = max(16, target_bytes // (4 * ncols))
    if nrows <= want:
        return nrows
    t = (want // 16) * 16
    while t >= 16:
        if nrows % t == 0:
            return t
        t -= 16
    return nrows


def _sum_chips(sent, landed, chip, name):
    _, nrows, ncols = sent[0].shape
    tr = _rows_tile(nrows, ncols, 2 << 20)
    nblk = nrows // tr

    def body(chip_ref, *refs):
        own, got, o_ref = refs[:DEPTH], refs[DEPTH:2 * DEPTH], refs[2 * DEPTH]
        me = chip_ref[0]
        for l in range(DEPTH):
            for m in range(N_CHIPS):
                @pl.when(jnp.logical_and(pl.program_id(0) == l, me == m))
                def _():
                    acc = None
                    for j in range(N_CHIPS):
                        val = (own[l][...] if j == m else got[l][j]).astype(F32)
                        acc = val if acc is None else acc + val
                    o_ref[...] = acc

    def rows_of(layer):
        return lambda l, i, chip_ref: jnp.where(l == layer, i, 0)

    own_specs = [pl.BlockSpec((None, tr, ncols), lambda l, i, chip_ref, r=rows_of(k): (chip_ref[0], r(l, i, chip_ref), 0))
                 for k in range(DEPTH)]
    got_specs = [pl.BlockSpec((N_CHIPS, tr, ncols), lambda l, i, chip_ref, r=rows_of(k): (0, r(l, i, chip_ref), 0))
                 for k in range(DEPTH)]
    return pl.pallas_call(
        body, name=name, out_shape=jax.ShapeDtypeStruct((DEPTH, nrows, ncols), F32),
        grid_spec=pltpu.PrefetchScalarGridSpec(
            num_scalar_prefetch=1, grid=(DEPTH, nblk), in_specs=own_specs + got_specs,
            out_specs=pl.BlockSpec((None, tr, ncols), lambda l, i, chip_ref: (l, i, 0))),
        compiler_params=_params(("arbitrary", "arbitrary")),
    )(chip, *sent, *landed)


def _adamw(w, m, v, ga, gb, name, tile=None):
    two = gb is not None
    c1 = 1.0 - ADAM_B1 ** ADAM_STEP
    c2 = 1.0 - ADAM_B2 ** ADAM_STEP

    def body(*refs):
        if two:
            w_ref, m_ref, v_ref, ga_ref, gb_ref, g_ref, d_ref, nm_ref, nv_ref = refs
            g = ga_ref[...] + gb_ref[...]
        else:
            w_ref, m_ref, v_ref, ga_ref, g_ref, d_ref, nm_ref, nv_ref = refs
            g = ga_ref[...]
        g_ref[...] = g
        nm = ADAM_B1 * m_ref[...] + (1.0 - ADAM_B1) * g
        nv = ADAM_B2 * v_ref[...] + (1.0 - ADAM_B2) * (g * g)
        nm_ref[...] = nm
        nv_ref[...] = nv
        d_ref[...] = -ADAM_LR * ((nm / c1) / (jnp.sqrt(nv / c2) + ADAM_EPS) + ADAM_WD * w_ref[...])

    if tile is None:
        nl, nrows, ncols = w.shape
        tr = _rows_tile(nrows, ncols, 1 << 20)
        blk = pl.BlockSpec((None, tr, ncols), lambda l, i: (l, i, 0))
        grid = (nl, nrows // tr)
    else:
        nrows, nl, ncols = w.shape
        rb, cb = tile
        blk = pl.BlockSpec((rb, nl, cb), lambda i, j: (i, 0, j))
        grid = (nrows // rb, ncols // cb)
    sh = jax.ShapeDtypeStruct(w.shape, F32)
    ins = [w, m, v, ga] + ([gb] if two else [])
    return pl.pallas_call(
        body, name=name, out_shape=(sh, sh, sh, sh), grid=grid,
        in_specs=[blk] * len(ins), out_specs=(blk, blk, blk, blk),
        compiler_params=_params(("parallel", "parallel")),
    )(*ins)


def _pad_rows(a, rows):
    return jnp.concatenate([a, jnp.zeros((rows - a.shape[0],) + a.shape[1:], a.dtype)], axis=0)


N_IN_CHIP = N_IN // N_CHIPS
_LR_LO = 3072 - N_IN_CHIP
_LR_HI = _LR_LO + GLA_LOWRANK


def _w_in_from_chips(a):
    return jnp.concatenate([a[0], a[1][:, :_LR_LO], a[1][:, _LR_HI:], a[2], a[3], a[1][:, _LR_LO:_LR_HI],
                            jnp.zeros((a.shape[1], LR_PAD - GLA_LOWRANK), a.dtype)], axis=1)


def _w_in_to_chips(w):
    s2 = 2 * N_IN_CHIP - GLA_LOWRANK
    s3 = s2 + N_IN_CHIP
    c1 = jnp.concatenate([w[:, N_IN_CHIP:3072], w[:, OFF_LR:OFF_LR + GLA_LOWRANK], w[:, 3072:s2]], axis=1)
    return jnp.stack([w[:, :N_IN_CHIP], c1, w[:, s2:s3], w[:, s3:OFF_LR]])


_BIG = ("w_in", "w_og", "w_oc", "w_o", "w_up", "w_dn")
_ROW_SHARDED = ("w_o", "w_dn")
_TWO_LEVEL = ("w_in", "w_up")


def kernel(x, c, w_ada, b_ada, norm_g, w_in, w_a2, b_a2, gla_norm_g, w_out_gla, conv_mix_w, w_out_conv, w_o, w_up, ffn_conv_w, w_down, loss_target, m_w_ada, m_b_ada, m_norm_g, m_w_in, m_w_a2, m_b_a2, m_gla_norm_g, m_w_out_gla, m_conv_mix_w, m_w_out_conv, m_w_o, m_w_up, m_ffn_conv_w, m_w_down, v_w_ada, v_b_ada, v_norm_g, v_w_in, v_w_a2, v_b_a2, v_gla_norm_g, v_w_out_gla, v_conv_mix_w, v_w_out_conv, v_w_o, v_w_up, v_ffn_conv_w, v_w_down):
    xi, yi, ci = lax.axis_index("x"), lax.axis_index("y"), lax.axis_index("c")
    chip = 2 * xi + yi
    dev = 2 * chip + ci
    chip_arr = jnp.reshape(chip, (1,)).astype(jnp.int32)
    xt = x[0]
    tgt = loss_target[0]

    c_all = _allgather8(jnp.broadcast_to(c, (8, D_MODEL)), "gather_c")[0][:, 0, :]
    c16 = _pad_rows(c_all, 16)
    sm_parts = [norm_g.reshape(-1), w_a2.reshape(-1), conv_mix_w.reshape(-1), ffn_conv_w.reshape(-1)]
    sm_sizes = [a.shape[0] for a in sm_parts]
    sm_flat = jnp.concatenate(sm_parts)
    sm_rows = -(-sm_flat.shape[0] // 128)
    sm_rows = -(-sm_rows // 8) * 8
    sm_flat = jnp.concatenate([sm_flat, jnp.zeros((sm_rows * 128 - sm_flat.shape[0],), F32)]).reshape(sm_rows, 128)
    sm_all = _allgather8(sm_flat, "gather_small")[0].reshape(N_DEV, -1)[0::2]
    offs = [0]
    for s in sm_sizes:
        offs.append(offs[-1] + s)

    def small_full(idx, shape):
        a = sm_all[:, offs[idx]:offs[idx + 1]].reshape((N_CHIPS,) + shape)
        a = jnp.moveaxis(a, 0, -2)
        return a.reshape(shape[:-1] + (N_CHIPS * shape[-1],))

    norm_g_f = small_full(0, (DEPTH, 4, 512))
    w_a2_f = small_full(1, (DEPTH, GLA_LOWRANK, 128))
    conv_w_f = small_full(2, (DEPTH, 3, 256))
    ffn_w_f = small_full(3, (DEPTH, 3, 1408))

    b_loc = lax.dynamic_slice(b_ada, (0, chip * 3072), (DEPTH, 3072))
    mod_loc = jnp.concatenate(
        [_ada_fwd(c16, w_ada, b_loc[l:l + 1], l, "ada_fwd")[:8] for l in range(DEPTH)], axis=0)
    mod_all = _allgather8(mod_loc, "gather_mod")[0][0::2]
    mods = []
    for l in range(DEPTH):
        row = lax.dynamic_slice(mod_all, (0, l * 8 + dev, 0), (N_CHIPS, 1, 3072)).reshape(1, 6 * D_MODEL)
        mods.append([row[:, k * D_MODEL:(k + 1) * D_MODEL] for k in range(6)])

    big = dict(w_in=w_in, w_og=w_out_gla, w_oc=w_out_conv, w_o=w_o, w_up=w_up, w_dn=w_down)
    gathers = {}
    tok = 0.0 * (mod_all[0, 0, 0] + sm_all[0, 0])
    for l in range(DEPTH):
        for k in _BIG:
            shard = (big[k][l] + tok).astype(BF16)
            if k in _TWO_LEVEL:
                shard = shard.reshape(2, shard.shape[0] // 2, shard.shape[1])
            land = lax.dynamic_update_slice(lax.empty((N_CHIPS,) + shard.shape, BF16), shard[None],
                                            (chip,) + (0,) * shard.ndim)
            *handle, token = _gather_start(land, "gather_start_%s_%d" % (k, l), k in _TWO_LEVEL)
            gathers[k, l] = tuple(handle)
            tok = token[0, 0]

    fills = {}

    def fill_early(k, l, after):
        land = _gather_wait(gathers[k, l], after, "gather_wait_%s_%d" % (k, l), True)
        *handle, token = _fill_start(land, "fill_start_%s_%d" % (k, l))
        fills[k, l] = tuple(handle)
        return token

    def gathered(k, l, after):
        if k in _TWO_LEVEL:
            if (k, l) not in fills:
                fill_early(k, l, after)
            full = _fill_wait(fills[k, l], after, "fill_wait_%s_%d" % (k, l))
            full = full.reshape(N_CHIPS, 2 * full.shape[2], full.shape[3])
        else:
            full = _gather_wait(gathers[k, l], after, "gather_wait_%s_%d" % (k, l))
        if k in _ROW_SHARDED:
            return full.reshape(N_CHIPS * full.shape[1], full.shape[2])
        return _w_in_from_chips(full) if k == "w_in" else full

    saved = []
    h = None
    xin = xt
    for l in range(DEPTH):
        sh1, sc1, g1, sh2, sc2, g2 = mods[l]
        gn = [norm_g_f[l, k][None] for k in range(4)]
        wa = _pad_rows(w_a2_f[l], LR_PAD)
        ba = b_a2[l][None]
        gng = gla_norm_g[l][None]
        cw8 = _pad_rows(conv_w_f[l], 8)
        fw8 = _pad_rows(ffn_w_f[l], 8)
        if l == 0:
            h = _pre_norm(xin, gn[0] + tok, sc1, sh1, "pre_norm")
        wi = gathered("w_in", l, h)
        p = _matmul(h, wi, "nn", BF16, "mm_in", tn=1152)
        o, st = _gla_fwd(p, wa, ba, "gla_fwd")
        za = _gla_out_fwd(o, p, gng, "gla_out_fwd")
        zb = _conv_fwd(p, cw8, "conv_fwd")
        wog, woc = gathered("w_og", l, zb), gathered("w_oc", l, zb)
        ya = _matmul(za, wog, "nn", F32, "mm_out_gla", tm=2048, b_chips=True)
        yb = _matmul(zb, woc, "nn", F32, "mm_out_conv", tm=2048, b_chips=True)
        mm = _merge_fwd(ya, yb, p, "merge_fwd")
        wo = gathered("w_o", l, fill_early("w_up", l, mm))
        y = _matmul(mm, wo, "nn", F32, "mm_o")
        x1, h2 = _post_pre(xin, y, g1, gn[1], gn[2], sc2, sh2, "post_pre")
        wup = gathered("w_up", l, h2)
        u = _matmul(h2, wup, "nn", BF16, "mm_up", tn=1408, b_chips=True)
        f = _ffn_fwd(u, fw8, "ffn_fwd")
        wdn = gathered("w_dn", l, fill_early("w_in", l + 1, f) if l + 1 < DEPTH else f)
        y2 = _matmul(f, wdn, "nn", F32, "mm_down", tm=1024, tn=2048, tk=1408)
        saved.append(dict(xin=xin, h=h, p=p, o=o, st=st, za=za, zb=zb, ya=ya, yb=yb, mm=mm, y=y, x1=x1, h2=h2,
                          u=u, f=f, y2=y2, wi=wi, wog=wog, woc=woc, wo=wo, wup=wup, wdn=wdn, wa=wa, ba=ba,
                          gng=gng, cw8=cw8, fw8=fw8, gn=gn, mod=mods[l]))
        if l + 1 < DEPTH:
            nsh1, nsc1 = mods[l + 1][0], mods[l + 1][1]
            xin, h = _post_pre(x1, y2, g2, gn[3], norm_g_f[l + 1, 0][None], nsc1, nsh1, "post_pre")
        else:
            dx, loss_tile = _post_loss(x1, y2, g2, gn[3], tgt, "post_loss")
    loss = lax.psum(loss_tile[0, 0], ("x", "y", "c"))

    scatters = {}

    def scatter(k, l, dw, after=None):
        if k in _ROW_SHARDED:
            send = dw.reshape(N_CHIPS, dw.shape[0] // N_CHIPS, dw.shape[1])
        else:
            send = _w_in_to_chips(dw) if k == "w_in" else dw
        *handle, token = _scatter_start(send, "scatter_start_%s_%d" % (k, l), after)
        scatters[k, l] = tuple(handle)
        return token[0, 0]

    sm = {k: [None] * DEPTH for k in ("dmod", "norm_g", "w_a2", "b_a2", "gng", "conv_w", "ffn_w")}
    for l in reversed(range(DEPTH)):
        s = saved[l]
        sh1, sc1, g1, sh2, sc2, g2 = s["mod"]
        gn = s["gn"]
        dy2, dg2, dgn3 = _post_bwd(dx, s["y2"], g2, gn[3], "post_bwd")
        tk = scatter("w_dn", l, _matmul(s["f"], dy2, "tn", BF16, "mm_down_dw", tm=512, tn=1024, tk=4096))
        df = _matmul(dy2, s["wdn"], "nt", F32, "mm_down_dx", tn=1408)
        dgate, dup, dfw = _ffn_bwd(df, s["u"], s["fw8"] + tk, "ffn_bwd")
        du = (dgate, dup)
        tk = scatter("w_up", l, _matmul(s["h2"], du, "tn", BF16, "mm_up_dw", tm=1024, tn=1408, tk=2048, out_chips=True))
        dh2 = _matmul(du, s["wup"], "nt", F32, "mm_up_dx", tm=1024, tn=2048, tk=1408, b_chips=True)
        dx1, dsh2, dsc2, dgn2 = _pre_bwd(dh2, s["x1"], dx, gn[2] + tk, sc2, "pre_bwd")
        dy, dg1, dgn1 = _post_bwd(dx1, s["y"], g1, gn[1], "post_bwd")
        tk = scatter("w_o", l, _matmul(s["mm"], dy, "tn", BF16, "mm_o_dw", tk=4096))
        dm = _matmul(dy, s["wo"], "nt", F32, "mm_o_dx")
        dya, dp = _merge_bwd(dm, s["ya"], s["p"], OFF_GA, None, "merge_bwd_a")
        dyb, dp = _merge_bwd(dm, s["yb"], s["p"], OFF_GB, dp, "merge_bwd_b")
        tk = tk + scatter("w_og", l, _matmul(s["za"], dya, "tn", BF16, "mm_out_gla_dw", tk=4096, out_chips=True))
        dza = _matmul(dya, s["wog"], "nt", F32, "mm_out_gla_dx", tm=2048, b_chips=True)
        do, dp, dgng = _gla_out_bwd(dza, s["o"], s["p"], s["gng"] + tk, dp, "gla_out_bwd")
        tk = scatter("w_oc", l, _matmul(s["zb"], dyb, "tn", BF16, "mm_out_conv_dw", tk=4096, out_chips=True))
        dzb = _matmul(dyb, s["woc"], "nt", F32, "mm_out_conv_dx", tm=2048, b_chips=True)
        dp, dcw = _conv_bwd(dzb, s["p"], s["cw8"] + tk, dp, "conv_bwd")
        dp, dlr, dwa, dba = _gla_bwd(do, s["p"], s["st"], s["wa"], s["ba"], dp, "gla_bwd")
        dp = lax.dynamic_update_slice(dp, dlr, (0, OFF_LR))
        dw_in = _matmul(s["h"], dp, "tn", BF16, "mm_in_dw", tm=512, tn=1152, tk=4096)
        tk = scatter("w_in", l, dw_in) if l > 0 else 0.0
        dh = _matmul(dp, s["wi"], "nt", F32, "mm_in_dx", tm=1024, tn=2048, tk=1152)
        dx, dsh1, dsc1, dgn0 = _pre_bwd(dh, s["xin"], dx1, gn[0] + tk, sc1, "pre_bwd")
        sm["dmod"][l] = jnp.concatenate([dsh1, dsc1, dg1, dsh2, dsc2, dg2], axis=1)[0]
        sm["norm_g"][l] = jnp.concatenate([dgn0, dgn1, dgn2, dgn3], axis=0)
        sm["w_a2"][l] = dwa[:GLA_LOWRANK]
        sm["b_a2"][l] = dba[0]
        sm["gng"][l] = dgng[0]
        sm["conv_w"][l] = dcw[:3]
        sm["ffn_w"][l] = dfw[:3]
    grad_x = dx[None]

    names = ("dmod", "norm_g", "w_a2", "b_a2", "gng", "conv_w", "ffn_w")
    parts = [jnp.stack(sm[k]).reshape(-1) for k in names]
    shapes = [jnp.stack(sm[k]).shape for k in names]
    sizes = [a.shape[0] for a in parts]
    flat = jnp.concatenate(parts)
    rows = -(-flat.shape[0] // 1024) * 8
    flat = jnp.concatenate([flat, jnp.zeros((rows * 128 - flat.shape[0],), F32)]).reshape(rows, 128)
    gath, tot = _allgather8(flat, "reduce_small")
    tk = scatter("w_in", 0, dw_in, after=tot)
    c16 = c16 + tk
    po = [0]
    for s_ in sizes:
        po.append(po[-1] + s_)
    tot = tot.reshape(-1)
    tot_of = {k: tot[po[i]:po[i + 1]].reshape(shapes[i]) for i, k in enumerate(names)}
    dmod_all = gath.reshape(N_DEV, -1)[:, po[0]:po[1]].reshape(N_DEV, DEPTH, 6 * D_MODEL)

    def chip_cols(a, width):
        return lax.dynamic_slice_in_dim(a, chip * width, width, axis=a.ndim - 1)

    dml = jnp.transpose(chip_cols(dmod_all, 3072), (1, 0, 2))
    dml = jnp.concatenate([dml, jnp.zeros_like(dml)], axis=1)
    g_w_ada = _ada_bwd(c16, dml, "ada_bwd")

    def upd(w, m, v, ga, gb, name):
        sh = w.shape
        as3 = sh if len(sh) == 3 else (1,) + sh
        outs = _adamw(w.reshape(as3), m.reshape(as3), v.reshape(as3), ga.reshape(as3),
                      None if gb is None else gb.reshape(as3), name)
        return [a.reshape(sh) for a in outs]

    res = {}
    res["w_ada"] = upd(w_ada, m_w_ada, v_w_ada, g_w_ada, None, "adamw")
    res["b_ada"] = upd(b_ada, m_b_ada, v_b_ada, tot_of["dmod"], None, "adamw")
    res["norm_g"] = upd(norm_g, m_norm_g, v_norm_g, chip_cols(tot_of["norm_g"], 512), None, "adamw")
    res["w_a2"] = upd(w_a2, m_w_a2, v_w_a2, chip_cols(tot_of["w_a2"], 128), None, "adamw")
    res["b_a2"] = upd(b_a2, m_b_a2, v_b_a2, tot_of["b_a2"], None, "adamw")
    res["gla_norm_g"] = upd(gla_norm_g, m_gla_norm_g, v_gla_norm_g, tot_of["gng"], None, "adamw")
    res["conv_mix_w"] = upd(conv_mix_w, m_conv_mix_w, v_conv_mix_w, chip_cols(tot_of["conv_w"], 256), None, "adamw")
    res["ffn_conv_w"] = upd(ffn_conv_w, m_ffn_conv_w, v_ffn_conv_w, chip_cols(tot_of["ffn_w"], 1408), None, "adamw")

    full_name = dict(w_in="w_in", w_og="w_out_gla", w_oc="w_out_conv", w_o="w_o", w_up="w_up", w_dn="w_down")
    state = dict(w_in=(w_in, m_w_in, v_w_in), w_og=(w_out_gla, m_w_out_gla, v_w_out_gla),
                 w_oc=(w_out_conv, m_w_out_conv, v_w_out_conv), w_o=(w_o, m_w_o, v_w_o),
                 w_up=(w_up, m_w_up, v_w_up), w_dn=(w_down, m_w_down, v_w_down))
    def finish(k, handle, after):
        plane, other = _sibling_wait(handle, after, "sibling_wait_" + k)
        if k == "w_in":
            outs = _adamw(*[jnp.transpose(a, (2, 0, 1)) for a in state[k]], plane, other, "adamw_w_in",
                          tile=(N_IN_CHIP // 4, D_MODEL // 8))
            res[full_name[k]] = [jnp.transpose(a, (1, 2, 0)) for a in outs]
        else:
            res[full_name[k]] = upd(*state[k], plane, other, "adamw")

    after = res["w_ada"][3]
    pending = None
    for k in ("w_dn", "w_up", "w_o", "w_og", "w_oc", "w_in"):
        done = [_scatter_wait(scatters[k, l], after, "scatter_wait_%s_%d" % (k, l)) for l in range(DEPTH)]
        plane = _sum_chips([d[0] for d in done], [d[1] for d in done], chip_arr, "sum_chips")
        if k == "w_in":
            plane = jnp.transpose(plane, (2, 0, 1))
        *handle, after = _sibling_start(plane, "sibling_start_" + k)
        if pending is not None:
            finish(*pending, plane)
        pending = (k, tuple(handle))
    finish(*pending, after)
    order = ("w_ada", "b_ada", "norm_g", "w_in", "w_a2", "b_a2", "gla_norm_g", "w_out_gla", "conv_mix_w",
             "w_out_conv", "w_o", "w_up", "ffn_conv_w", "w_down")
    return (loss, grad_x, *[res[k][0] for k in order], *[res[k][1] for k in order],
            *[res[k][2] for k in order], *[res[k][3] for k in order])
```

```python
import functools
import math

import jax
import jax.numpy as jnp
from jax import lax
from jax.experimental import pallas as pl
from jax.experimental.pallas import tpu as pltpu

F32 = jnp.float32
BF16 = jnp.bfloat16
MESH = pl.DeviceIdType.MESH

D_MODEL = 2048
DEPTH = 2
CHUNK = 64
GLA_HEADS = 4
GLA_DK = 128
GLA_DV = 256
GLA_QK = GLA_HEADS * GLA_DK
GLA_V = GLA_HEADS * GLA_DV
GLA_LOWRANK = 16
GLA_TAU = 16.0
CONV_WIDTH = 1024
D_FF = 5632
EPS = 1e-6
N_IN = 10256
LR_PAD = 128
N_IN_PAD = N_IN - GLA_LOWRANK + LR_PAD
OFF_Q, OFF_K, OFF_V, OFF_R = 0, 512, 1024, 2048
OFF_CB, OFF_CC, OFF_CX, OFF_GA, OFF_GB, OFF_LR = 3072, 4096, 5120, 6144, 8192, 10240

ADAM_LR = 0.001
ADAM_B1 = 0.9
ADAM_B2 = 0.999
ADAM_EPS = 1e-08
ADAM_WD = 0.01
ADAM_STEP = 10

N_CHIPS = 4
N_DEV = 8
VMEM_LIMIT = 56 * 1024 * 1024
TM_ROW = 256
TM_EW = 512
CW_EW = 512
GLA_ROWS = 256


def _params(sem=None):
    return pltpu.CompilerParams(dimension_semantics=sem, vmem_limit_bytes=VMEM_LIMIT)


def _sigmoid(v):
    return 1.0 / (1.0 + jnp.exp(-v))


def _log_sigmoid(v):
    return jnp.minimum(v, 0.0) - jnp.log(1.0 + jnp.exp(-jnp.abs(v)))


_GELU_C = math.sqrt(2.0 / math.pi)


def _gelu_and_grad(v):
    v2 = v * v
    t = jnp.tanh(_GELU_C * v * (1.0 + 0.044715 * v2))
    half = 0.5 * (1.0 + t)
    return v * half, half + (0.5 * _GELU_C) * v * (1.0 - t * t) * (1.0 + (3.0 * 0.044715) * v2)


def _ld(ref):
    return ref[...].astype(F32)


def _flip(a, d):
    return a + d - 2 * a * d


def _unless(cond):
    return jnp.where(cond, 0.0, 1.0).astype(F32)


def _allgather8(xv, name):
    r, cdim = xv.shape

    def body(x_ref, out_ref, sum_ref, send_sems, recv_sems):
        xi, yi, ci = lax.axis_index("x"), lax.axis_index("y"), lax.axis_index("c")
        me = 4 * xi + 2 * yi + ci
        out_ref[pl.ds(me, 1)] = x_ref[...][None]
        started = []
        for k in range(1, N_DEV):
            px, py, pc = _flip(xi, (k >> 2) & 1), _flip(yi, (k >> 1) & 1), _flip(ci, k & 1)
            cp = pltpu.make_async_remote_copy(
                src_ref=x_ref, dst_ref=out_ref.at[me], send_sem=send_sems.at[k - 1], recv_sem=recv_sems.at[k - 1],
                device_id=(px, py, pc), device_id_type=MESH)
            cp.start()
            started.append((cp, 4 * px + 2 * py + pc, k, (px, py, pc)))
        for cp, peer, k, pid in started:
            cp.wait_send()
            pltpu.make_async_remote_copy(
                src_ref=x_ref, dst_ref=out_ref.at[peer], send_sem=send_sems.at[k - 1], recv_sem=recv_sems.at[k - 1],
                device_id=pid, device_id_type=MESH).wait_recv()
        acc = out_ref[0]
        for d in range(1, N_DEV):
            acc = acc + out_ref[d]
        sum_ref[...] = acc

    return pl.pallas_call(
        body, name=name,
        out_shape=(jax.ShapeDtypeStruct((N_DEV, r, cdim), F32), jax.ShapeDtypeStruct((r, cdim), F32)),
        in_specs=[pl.BlockSpec(memory_space=pltpu.VMEM)],
        out_specs=(pl.BlockSpec(memory_space=pltpu.VMEM), pl.BlockSpec(memory_space=pltpu.VMEM)),
        scratch_shapes=[pltpu.SemaphoreType.DMA((N_DEV - 1,)), pltpu.SemaphoreType.DMA((N_DEV - 1,))],
        compiler_params=pltpu.CompilerParams(vmem_limit_bytes=VMEM_LIMIT),
    )(xv)


_HBM = pl.BlockSpec(memory_space=pltpu.HBM)
_SEM = pl.BlockSpec(memory_space=pltpu.SEMAPHORE)
_EFFECT = pltpu.SideEffectType.DATAFLOW_SIDE_EFFECTING
_CHIP_FLIPS = ((1, 0), (0, 1), (1, 1))


def _chip_copies(src_ref, land_ref, send_sems, recv_sems, scatter, halves=False):
    xi, yi, ci = lax.axis_index("x"), lax.axis_index("y"), lax.axis_index("c")
    me = 2 * xi + yi
    out = []

    def slot(j):
        return land_ref.at[j, ci] if halves else land_ref.at[j]

    for k, (dx, dy) in enumerate(_CHIP_FLIPS):
        px, py = _flip(xi, dx), _flip(yi, dy)
        peer = 2 * px + py
        src = src_ref.at[peer] if scatter else slot(me)
        mk = functools.partial(pltpu.make_async_remote_copy, src_ref=src, send_sem=send_sems.at[k],
                               recv_sem=recv_sems.at[k], device_id=(px, py, ci), device_id_type=MESH)
        out.append((mk(dst_ref=slot(me)), mk(dst_ref=slot(peer))))
    return out


def _fill_copies(land_ref, send_sems, recv_sems):
    xi, yi, ci = lax.axis_index("x"), lax.axis_index("y"), lax.axis_index("c")
    out = []
    for k, (dx, dy) in enumerate(_CHIP_FLIPS):
        peer = 2 * _flip(xi, dx) + _flip(yi, dy)
        mk = functools.partial(pltpu.make_async_remote_copy, src_ref=land_ref.at[peer, ci], send_sem=send_sems.at[k],
                               recv_sem=recv_sems.at[k], device_id=(xi, yi, 1 - ci), device_id_type=MESH)
        out.append((mk(dst_ref=land_ref.at[peer, ci]), mk(dst_ref=land_ref.at[peer, 1 - ci])))
    return out


def _fill_start(land, name):
    def body(land_ref, send_sems, recv_sems, land_thru, token):
        for mine, _ in _fill_copies(land_ref, send_sems, recv_sems):
            mine.start()
        token[...] = jnp.zeros_like(token)

    return pl.pallas_call(
        body, name=name,
        out_shape=(pltpu.SemaphoreType.DMA((3,)), pltpu.SemaphoreType.DMA((3,)), pltpu.HBM(land.shape, land.dtype),
                   jax.ShapeDtypeStruct((8, 128), F32)),
        in_specs=(_HBM,), out_specs=(_SEM, _SEM, _HBM, pl.BlockSpec(memory_space=pltpu.VMEM)),
        input_output_aliases={0: 2},
        compiler_params=pltpu.CompilerParams(has_side_effects=_EFFECT),
    )(land)


def _fill_wait(handle, after, name):
    send, recv, land_thru = handle

    def body(land_ref, send_sems, recv_sems, after_ref, land_out):
        for mine, theirs in _fill_copies(land_ref, send_sems, recv_sems):
            mine.wait_send()
            theirs.wait_recv()

    return pl.pallas_call(
        body, name=name, out_shape=pltpu.HBM(land_thru.shape, land_thru.dtype),
        in_specs=(_HBM, _SEM, _SEM, pl.BlockSpec(memory_space=pl.ANY)), out_specs=_HBM,
        input_output_aliases={0: 0},
        compiler_params=pltpu.CompilerParams(has_side_effects=_EFFECT),
    )(land_thru, send, recv, after)


def _gather_start(land, name, halves=False):
    def body(land_ref, send_sems, recv_sems, land_thru, token):
        for mine, _ in _chip_copies(None, land_ref, send_sems, recv_sems, False, halves):
            mine.start()
        token[...] = jnp.zeros_like(token)

    return pl.pallas_call(
        body, name=name,
        out_shape=(pltpu.SemaphoreType.DMA((3,)), pltpu.SemaphoreType.DMA((3,)), pltpu.HBM(land.shape, land.dtype),
                   jax.ShapeDtypeStruct((8, 128), F32)),
        in_specs=(_HBM,), out_specs=(_SEM, _SEM, _HBM, pl.BlockSpec(memory_space=pltpu.VMEM)),
        input_output_aliases={0: 2},
        compiler_params=pltpu.CompilerParams(has_side_effects=_EFFECT),
    )(pltpu.with_memory_space_constraint(land, pltpu.HBM))


def _gather_wait(handle, after, name, halves=False):
    send, recv, land_thru = handle

    def body(land_ref, send_sems, recv_sems, after_ref, land_out):
        for mine, theirs in _chip_copies(None, land_ref, send_sems, recv_sems, False, halves):
            mine.wait_send()
            theirs.wait_recv()

    return pl.pallas_call(
        body, name=name, out_shape=pltpu.HBM(land_thru.shape, land_thru.dtype),
        in_specs=(_HBM, _SEM, _SEM, pl.BlockSpec(memory_space=pl.ANY)), out_specs=_HBM,
        input_output_aliases={0: 0},
        compiler_params=pltpu.CompilerParams(has_side_effects=_EFFECT),
    )(land_thru, send, recv, after)


def _scatter_start(src, name, after=None):
    extra = [] if after is None else [after]

    def body(src_ref, land_ref, *rest):
        send_sems, recv_sems, src_thru, land_thru, token = rest[len(extra):]
        for mine, _ in _chip_copies(src_ref, land_ref, send_sems, recv_sems, True):
            mine.start()
        token[...] = jnp.zeros_like(token)

    return pl.pallas_call(
        body, name=name,
        out_shape=(pltpu.SemaphoreType.DMA((3,)), pltpu.SemaphoreType.DMA((3,)), pltpu.HBM(src.shape, src.dtype),
                   pltpu.HBM(src.shape, src.dtype), jax.ShapeDtypeStruct((8, 128), F32)),
        in_specs=(_HBM, _HBM) + (pl.BlockSpec(memory_space=pl.ANY),) * len(extra),
        out_specs=(_SEM, _SEM, _HBM, _HBM, pl.BlockSpec(memory_space=pltpu.VMEM)),
        input_output_aliases={0: 2, 1: 3},
        compiler_params=pltpu.CompilerParams(has_side_effects=_EFFECT),
    )(pltpu.with_memory_space_constraint(src, pltpu.HBM),
      pltpu.with_memory_space_constraint(lax.empty(src.shape, src.dtype), pltpu.HBM), *extra)


def _scatter_wait(handle, after, name):
    send, recv, src_thru, land_thru = handle

    def body(src_ref, land_ref, send_sems, recv_sems, after_ref, src_out, land_out):
        for mine, theirs in _chip_copies(src_ref, land_ref, send_sems, recv_sems, True):
            mine.wait_send()
            theirs.wait_recv()

    return pl.pallas_call(
        body, name=name,
        out_shape=(pltpu.HBM(src_thru.shape, src_thru.dtype), pltpu.HBM(land_thru.shape, land_thru.dtype)),
        in_specs=(_HBM, _HBM, _SEM, _SEM, pl.BlockSpec(memory_space=pl.ANY)), out_specs=(_HBM, _HBM),
        input_output_aliases={0: 0, 1: 1},
        compiler_params=pltpu.CompilerParams(has_side_effects=_EFFECT),
    )(src_thru, land_thru, send, recv, after)


def _sibling_copy(src_ref, land_ref, send_sems, recv_sems):
    xi, yi, ci = lax.axis_index("x"), lax.axis_index("y"), lax.axis_index("c")
    return pltpu.make_async_remote_copy(src_ref=src_ref, dst_ref=land_ref, send_sem=send_sems.at[0],
                                        recv_sem=recv_sems.at[0], device_id=(xi, yi, 1 - ci), device_id_type=MESH)


def _sibling_start(src, name):
    def body(src_ref, land_ref, send_sems, recv_sems, src_thru, land_thru, token):
        _sibling_copy(src_ref, land_ref, send_sems, recv_sems).start()
        token[...] = jnp.zeros_like(token)

    return pl.pallas_call(
        body, name=name,
        out_shape=(pltpu.SemaphoreType.DMA((1,)), pltpu.SemaphoreType.DMA((1,)), pltpu.HBM(src.shape, src.dtype),
                   pltpu.HBM(src.shape, src.dtype), jax.ShapeDtypeStruct((8, 128), F32)),
        in_specs=(_HBM, _HBM), out_specs=(_SEM, _SEM, _HBM, _HBM, pl.BlockSpec(memory_space=pltpu.VMEM)),
        input_output_aliases={0: 2, 1: 3},
        compiler_params=pltpu.CompilerParams(has_side_effects=_EFFECT),
    )(pltpu.with_memory_space_constraint(src, pltpu.HBM),
      pltpu.with_memory_space_constraint(lax.empty(src.shape, src.dtype), pltpu.HBM))


def _sibling_wait(handle, after, name):
    send, recv, src_thru, land_thru = handle

    def body(src_ref, land_ref, send_sems, recv_sems, after_ref, src_out, land_out):
        cp = _sibling_copy(src_ref, land_ref, send_sems, recv_sems)
        cp.wait_send()
        cp.wait_recv()

    return pl.pallas_call(
        body, name=name,
        out_shape=(pltpu.HBM(src_thru.shape, src_thru.dtype), pltpu.HBM(land_thru.shape, land_thru.dtype)),
        in_specs=(_HBM, _HBM, _SEM, _SEM, pl.BlockSpec(memory_space=pl.ANY)), out_specs=(_HBM, _HBM),
        input_output_aliases={0: 0, 1: 1},
        compiler_params=pltpu.CompilerParams(has_side_effects=_EFFECT),
    )(src_thru, land_thru, send, recv, after)


def _pick(dim, pref):
    if dim <= pref:
        return dim
    t = (pref // 128) * 128
    while t >= 128:
        if dim % t == 0:
            return t
        t -= 128
    return dim


def _matmul(a, b, dims, out_dtype, name, tm=512, tn=1024, tk=2048, out_chips=False, b_chips=False):
    a_parts = a if isinstance(a, tuple) else (a,)
    b_parts = b if isinstance(b, tuple) else (b,)
    na, nb = len(a_parts), len(b_parts)
    assert (na == 1 or dims == "nt") and (nb == 1 or dims == "tn")
    b_shape = (b_parts[0].shape[1], N_CHIPS * b_parts[0].shape[2]) if b_chips else b_parts[0].shape
    if dims == "nn":
        (m, kd), (_, n) = a_parts[0].shape, b_shape
    elif dims == "nt":
        (m, kd), (n, _) = a_parts[0].shape, b_shape
        kd = na * kd
    else:
        (kd, m), (_, n) = a_parts[0].shape, b_shape
        n = nb * n
    tm = _pick(m, tm)
    tn = _pick(n // N_CHIPS, tn) if (out_chips or (b_chips and dims == "nn")) else _pick(n // nb, tn)
    tk = _pick(kd // N_CHIPS, tk) if (b_chips and dims == "nt") else _pick(kd // na, tk)
    nk, nj = kd // tk, n // tn
    ka, jb = nk // na, nj // nb
    if out_chips:
        per_chip = n // N_CHIPS // tn
        out_shape = jax.ShapeDtypeStruct((N_CHIPS, m, n // N_CHIPS), out_dtype)
        out_spec = pl.BlockSpec((None, tm, tn), lambda j, i, k: (j // per_chip, i, j % per_chip))
    else:
        out_shape = jax.ShapeDtypeStruct((m, n), out_dtype)
        out_spec = pl.BlockSpec((tm, tn), lambda j, i, k: (i, j))
    def part_of(idx, first, count):
        return jnp.clip(idx - first, 0, count - 1)

    if dims == "nn":
        a_specs = [pl.BlockSpec((tm, tk), lambda j, i, k: (i, k))]
        b_specs = [pl.BlockSpec((tk, tn), lambda j, i, k: (k, j))]
        dn = (((1,), (0,)), ((), ()))
    elif dims == "nt":
        a_specs = [pl.BlockSpec((tm, tk), lambda j, i, k, p=p: (i, part_of(k, p * ka, ka))) for p in range(na)]
        b_specs = [pl.BlockSpec((tn, tk), lambda j, i, k: (j, k))]
        dn = (((1,), (1,)), ((), ()))
    else:
        a_specs = [pl.BlockSpec((tk, tm), lambda j, i, k: (k, i))]
        b_specs = [pl.BlockSpec((tk, tn), lambda j, i, k, p=p: (k, part_of(j, p * jb, jb))) for p in range(nb)]
        dn = (((0,), (0,)), ((), ()))
    if b_chips and dims == "nn":
        nper = n // N_CHIPS // tn
        b_specs = [pl.BlockSpec((None, tk, tn), lambda j, i, k: (j // nper, k, j % nper))]
    elif b_chips:
        kper = kd // N_CHIPS // tk
        b_specs = [pl.BlockSpec((None, tn, tk), lambda j, i, k: (k // kper, j, k % kper))]
    direct = nk == 1 or out_dtype == F32

    def body(*refs):
        a_refs, b_refs, o_ref = refs[:na], refs[na:na + nb], refs[na + nb]
        acc_ref = o_ref if direct else refs[na + nb + 1]
        j, k = pl.program_id(0), pl.program_id(2)

        def step(a_ref, b_ref):
            part = lax.dot_general(a_ref[...].astype(BF16), b_ref[...].astype(BF16), dn, preferred_element_type=F32)
            if nk == 1:
                o_ref[...] = part.astype(o_ref.dtype)
                return

            @pl.when(k == 0)
            def _():
                acc_ref[...] = part

            @pl.when(k > 0)
            def _():
                acc_ref[...] += part

            if not direct:
                @pl.when(k == nk - 1)
                def _():
                    o_ref[...] = acc_ref[...].astype(o_ref.dtype)

        if na == 1 and nb == 1:
            step(a_refs[0], b_refs[0])
        for p in range(na if na > 1 else 0):
            pl.when(jnp.logical_and(k >= p * ka, k < (p + 1) * ka))(functools.partial(step, a_refs[p], b_refs[0]))
        for p in range(nb if nb > 1 else 0):
            pl.when(jnp.logical_and(j >= p * jb, j < (p + 1) * jb))(functools.partial(step, a_refs[0], b_refs[p]))

    return pl.pallas_call(
        body, name=name, out_shape=out_shape,
        grid=(nj, m // tm, nk),
        in_specs=a_specs + b_specs,
        out_specs=out_spec,
        scratch_shapes=[] if direct else [pltpu.VMEM((tm, tn), F32)],
        compiler_params=_params(("parallel", "parallel", "arbitrary")),
    )(*a_parts, *b_parts)


def _rstd(v):
    return lax.rsqrt(jnp.mean(v * v, axis=-1, keepdims=True) + EPS)


def _row(tm):
    return pl.BlockSpec((tm, D_MODEL), lambda i: (i, 0))


_VEC = pl.BlockSpec((1, D_MODEL), lambda i: (0, 0))


def _pre_norm(x, gn, sc, sh, name):
    t = x.shape[0]
    tm = min(TM_ROW, t)

    def body(x_ref, gn_ref, sc_ref, sh_ref, h_ref):
        xv = x_ref[...]
        h_ref[...] = ((xv * _rstd(xv) * gn_ref[...]) * (1.0 + sc_ref[...]) + sh_ref[...]).astype(BF16)

    return pl.pallas_call(
        body, name=name, out_shape=jax.ShapeDtypeStruct((t, D_MODEL), BF16), grid=(t // tm,),
        in_specs=[_row(tm), _VEC, _VEC, _VEC], out_specs=_row(tm),
        compiler_params=_params(("parallel",)),
    )(x, gn, sc, sh)


def _post_pre(x, y, g, gnp, gn, sc, sh, name):
    t = x.shape[0]
    tm = min(TM_ROW, t)

    def body(x_ref, y_ref, g_ref, gnp_ref, gn_ref, sc_ref, sh_ref, x1_ref, h_ref):
        yv = y_ref[...]
        x1 = x_ref[...] + g_ref[...] * (yv * _rstd(yv) * gnp_ref[...])
        x1_ref[...] = x1
        h_ref[...] = ((x1 * _rstd(x1) * gn_ref[...]) * (1.0 + sc_ref[...]) + sh_ref[...]).astype(BF16)

    return pl.pallas_call(
        body, name=name,
        out_shape=(jax.ShapeDtypeStruct((t, D_MODEL), F32), jax.ShapeDtypeStruct((t, D_MODEL), BF16)),
        grid=(t // tm,),
        in_specs=[_row(tm), _row(tm), _VEC, _VEC, _VEC, _VEC, _VEC], out_specs=(_row(tm), _row(tm)),
        compiler_params=_params(("parallel",)),
    )(x, y, g, gnp, gn, sc, sh)


def _post_loss(x, y, g, gnp, tgt, name):
    t = x.shape[0]
    tm = min(TM_ROW, t)

    def body(x_ref, y_ref, g_ref, gnp_ref, t_ref, dx_ref, loss_ref):
        yv = y_ref[...]
        diff = x_ref[...] + g_ref[...] * (yv * _rstd(yv) * gnp_ref[...]) - t_ref[...]
        dx_ref[...] = diff * (1.0 / D_MODEL)
        part = (0.5 / D_MODEL) * jnp.sum(jnp.sum(diff * diff, axis=-1, keepdims=True), axis=0, keepdims=True)

        @pl.when(pl.program_id(0) == 0)
        def _():
            loss_ref[...] = jnp.zeros_like(loss_ref)

        loss_ref[...] += jnp.broadcast_to(part, loss_ref.shape)

    return pl.pallas_call(
        body, name=name,
        out_shape=(jax.ShapeDtypeStruct((t, D_MODEL), F32), jax.ShapeDtypeStruct((8, 128), F32)),
        grid=(t // tm,),
        in_specs=[_row(tm), _row(tm), _VEC, _VEC, _row(tm)],
        out_specs=(_row(tm), pl.BlockSpec((8, 128), lambda i: (0, 0))),
        compiler_params=_params(("arbitrary",)),
    )(x, y, g, gnp, tgt)


def _acc_rows(ref, val):
    @pl.when(pl.program_id(0) == 0)
    def _():
        ref[...] = jnp.zeros_like(ref)

    ref[...] += jnp.sum(val, axis=0, keepdims=True)


def _post_bwd(dxn, y, g, gnp, name):
    t = y.shape[0]
    tm = min(TM_ROW, t)

    def body(dx_ref, y_ref, g_ref, gnp_ref, dy_ref, dg_ref, dgn_ref):
        yv, dxv = y_ref[...], dx_ref[...]
        r = _rstd(yv)
        yh = yv * r
        _acc_rows(dg_ref, dxv * (yh * gnp_ref[...]))
        dn = dxv * g_ref[...]
        _acc_rows(dgn_ref, dn * yh)
        dyh = dn * gnp_ref[...]
        dy_ref[...] = (r * (dyh - yh * jnp.mean(dyh * yh, axis=-1, keepdims=True))).astype(BF16)

    return pl.pallas_call(
        body, name=name,
        out_shape=(jax.ShapeDtypeStruct((t, D_MODEL), BF16), jax.ShapeDtypeStruct((1, D_MODEL), F32),
                   jax.ShapeDtypeStruct((1, D_MODEL), F32)),
        grid=(t // tm,),
        in_specs=[_row(tm), _row(tm), _VEC, _VEC], out_specs=(_row(tm), _VEC, _VEC),
        compiler_params=_params(("arbitrary",)),
    )(dxn, y, g, gnp)


def _pre_bwd(dh, xin, dres, gn, sc, name):
    t = xin.shape[0]
    tm = min(TM_ROW, t)

    def body(dh_ref, x_ref, dres_ref, gn_ref, sc_ref, dx_ref, dsh_ref, dsc_ref, dgn_ref):
        xv, dhv = x_ref[...], dh_ref[...]
        r = _rstd(xv)
        xh = xv * r
        _acc_rows(dsh_ref, dhv)
        _acc_rows(dsc_ref, dhv * (xh * gn_ref[...]))
        dn = dhv * (1.0 + sc_ref[...])
        _acc_rows(dgn_ref, dn * xh)
        dxh = dn * gn_ref[...]
        dx_ref[...] = dres_ref[...] + r * (dxh - xh * jnp.mean(dxh * xh, axis=-1, keepdims=True))

    vec = jax.ShapeDtypeStruct((1, D_MODEL), F32)
    return pl.pallas_call(
        body, name=name, out_shape=(jax.ShapeDtypeStruct((t, D_MODEL), F32), vec, vec, vec),
        grid=(t // tm,),
        in_specs=[_row(tm), _row(tm), _row(tm), _VEC, _VEC], out_specs=(_row(tm), _VEC, _VEC, _VEC),
        compiler_params=_params(("arbitrary",)),
    )(dh, xin, dres, gn, sc)


def _fix_rows(v8, rows):
    idx = lax.broadcasted_iota(jnp.int32, v8.shape, 0)
    for j, val in rows:
        v8 = jnp.where(idx == j, jnp.broadcast_to(val, v8.shape), v8)
    return v8


def _shift_down(v, halo, s):
    hr, tm = halo.shape[0], v.shape[0]
    out = pltpu.roll(v, s, 0)
    if tm == 8:
        return _fix_rows(out, [(j, halo[hr - s + j:hr - s + j + 1, :]) for j in range(s)])
    head = _fix_rows(out[0:8, :], [(j, halo[hr - s + j:hr - s + j + 1, :]) for j in range(s)])
    return jnp.concatenate([head, out[8:, :]], axis=0)


def _shift_up(v, halo, s):
    tm = v.shape[0]
    out = pltpu.roll(v, tm - s, 0)
    tail = _fix_rows(out[tm - 8:, :], [(8 - s + j, halo[j:j + 1, :]) for j in range(s)])
    return jnp.concatenate([out[:tm - 8, :], tail], axis=0)


def _tile_specs(tm, cw, off, nrow, hr=8):
    ob = off // cw
    per = tm // hr
    main = pl.BlockSpec((tm, cw), lambda j, i: (i, ob + j))
    prev = pl.BlockSpec((hr, cw), lambda j, i: (jnp.maximum(i * per - 1, 0), ob + j))
    nxt = pl.BlockSpec((hr, cw), lambda j, i: (jnp.minimum((i + 1) * per, nrow * per - 1), ob + j))
    return main, prev, nxt


def _conv_fwd(p, w, name):
    t = p.shape[0]
    tm, cw = min(TM_EW, t), CW_EW
    nrow = t // tm
    cb_s, _, _ = _tile_specs(tm, cw, OFF_CB, nrow, 16)
    cc_s, cc_p, _ = _tile_specs(tm, cw, OFF_CC, nrow, 16)
    cx_s, cx_p, _ = _tile_specs(tm, cw, OFF_CX, nrow, 16)

    def body(cb_ref, cc_ref, ccp_ref, cx_ref, cxp_ref, w_ref, z_ref):
        u = _ld(cc_ref) * _ld(cx_ref)
        uh = _ld(ccp_ref) * _ld(cxp_ref) * _unless(pl.program_id(1) == 0)
        wv = w_ref[...]
        conv = wv[2:3, :] * u + wv[1:2, :] * _shift_down(u, uh, 1) + wv[0:1, :] * _shift_down(u, uh, 2)
        z_ref[...] = (_ld(cb_ref) * conv).astype(BF16)

    return pl.pallas_call(
        body, name=name, out_shape=jax.ShapeDtypeStruct((t, CONV_WIDTH), BF16),
        grid=(CONV_WIDTH // cw, nrow),
        in_specs=[cb_s, cc_s, cc_p, cx_s, cx_p, pl.BlockSpec((8, cw), lambda j, i: (0, j))],
        out_specs=pl.BlockSpec((tm, cw), lambda j, i: (i, j)),
        compiler_params=_params(("parallel", "arbitrary")),
    )(p, p, p, p, p, w)


def _acc_w(ref, vals):
    @pl.when(pl.program_id(1) == 0)
    def _():
        ref[...] = jnp.zeros_like(ref)

    for j, v in enumerate(vals):
        ref[j:j + 1, :] += jnp.sum(v, axis=0, keepdims=True)


def _conv_bwd(dz, p, w, dp, name):
    t = p.shape[0]
    tm, cw = min(TM_EW // 2, t), CONV_WIDTH
    nrow = t // tm
    dz_s, _, dz_n = _tile_specs(tm, cw, 0, nrow)
    cb_s, _, cb_n = _tile_specs(tm, cw, OFF_CB, nrow, 16)
    cc_s, cc_p, _ = _tile_specs(tm, cw, OFF_CC, nrow, 16)
    cx_s, cx_p, _ = _tile_specs(tm, cw, OFF_CX, nrow, 16)

    def body(dz_ref, dzn_ref, cb_ref, cbn_ref, cc_ref, ccp_ref, cx_ref, cxp_ref, w_ref, dp_in, dp_ref, dw_ref):
        dcb_ref = dp_ref.at[:, 0:cw]
        dcc_ref = dp_ref.at[:, cw:2 * cw]
        dcx_ref = dp_ref.at[:, 2 * cw:3 * cw]
        i = pl.program_id(1)
        ccv, cxv, dzv = _ld(cc_ref), _ld(cx_ref), dz_ref[...]
        u = ccv * cxv
        uh = _ld(ccp_ref) * _ld(cxp_ref) * _unless(i == 0)
        wv = w_ref[...]
        u1, u2 = _shift_down(u, uh, 1), _shift_down(u, uh, 2)
        conv = wv[2:3, :] * u + wv[1:2, :] * u1 + wv[0:1, :] * u2
        dcb_ref[...] = (dzv * conv).astype(BF16)
        dconv = dzv * _ld(cb_ref)
        dch = dzn_ref[...] * _ld(cbn_ref)[0:8, :] * _unless(i == nrow - 1)
        du = wv[2:3, :] * dconv + wv[1:2, :] * _shift_up(dconv, dch, 1) + wv[0:1, :] * _shift_up(dconv, dch, 2)
        dcc_ref[...] = (du * cxv).astype(BF16)
        dcx_ref[...] = (du * ccv).astype(BF16)
        _acc_w(dw_ref, (dconv * u2, dconv * u1, dconv * u))

    w_s = pl.BlockSpec((8, cw), lambda j, i: (0, j))
    return pl.pallas_call(
        body, name=name, out_shape=(_dp_shape(t), jax.ShapeDtypeStruct((8, CONV_WIDTH), F32)),
        grid=(1, nrow),
        in_specs=[dz_s, dz_n, cb_s, cb_n, cc_s, cc_p, cx_s, cx_p, w_s, _ANY],
        out_specs=(pl.BlockSpec((tm, 3 * cw), lambda j, i: (i, OFF_CB // (3 * cw))), w_s),
        input_output_aliases={9: 0},
        compiler_params=_params(("parallel", "arbitrary")),
    )(dz, dz, p, p, p, p, p, p, w, dp)


def _ffn_fwd(u, w, name):
    t = u.shape[0]
    tm, cw = min(TM_EW, t), CW_EW
    nrow = t // tm
    g_s, g_p, _ = _tile_specs(tm, cw, 0, nrow, 16)
    u_s, _, _ = _tile_specs(tm, cw, D_FF, nrow, 16)

    def body(g_ref, gp_ref, u_ref, w_ref, f_ref):
        gv = _ld(g_ref)
        gh = _ld(gp_ref) * _unless(pl.program_id(1) == 0)
        wv = w_ref[...]
        gc = wv[2:3, :] * gv + wv[1:2, :] * _shift_down(gv, gh, 1) + wv[0:1, :] * _shift_down(gv, gh, 2)
        f_ref[...] = (_gelu_and_grad(gc)[0] * _ld(u_ref)).astype(BF16)

    return pl.pallas_call(
        body, name=name, out_shape=jax.ShapeDtypeStruct((t, D_FF), BF16),
        grid=(D_FF // cw, nrow),
        in_specs=[g_s, g_p, u_s, pl.BlockSpec((8, cw), lambda j, i: (0, j))],
        out_specs=pl.BlockSpec((tm, cw), lambda j, i: (i, j)),
        compiler_params=_params(("parallel", "arbitrary")),
    )(u, u, u, w)


def _ffn_bwd(df, u, w, name):
    t = u.shape[0]
    tm, cw = min(TM_EW, t), CW_EW
    nrow = t // tm
    df_s, _, df_n = _tile_specs(tm, cw, 0, nrow)
    g_s, g_p, g_n = _tile_specs(tm, cw, 0, nrow, 16)
    u_s, _, u_n = _tile_specs(tm, cw, D_FF, nrow, 16)

    def body(df_ref, dfn_ref, g_ref, gp_ref, gn_ref, u_ref, un_ref, w_ref, dg_ref, du_ref, dw_ref):
        i = pl.program_id(1)
        gv, dfv, uv = _ld(g_ref), df_ref[...], _ld(u_ref)
        gh = _ld(gp_ref) * _unless(i == 0)
        wv = w_ref[...]
        g1, g2 = _shift_down(gv, gh, 1), _shift_down(gv, gh, 2)
        gc = wv[2:3, :] * gv + wv[1:2, :] * g1 + wv[0:1, :] * g2
        act, act_grad = _gelu_and_grad(gc)
        du_ref[...] = (dfv * act).astype(BF16)
        dgc = dfv * uv * act_grad
        gnv = _ld(gn_ref)[0:8, :]
        gtail = gv[tm - 8:tm, :]
        gcn = (wv[2:3, :] * gnv + wv[1:2, :] * _shift_down(gnv, gtail, 1) + wv[0:1, :] * _shift_down(gnv, gtail, 2))
        dgcn = dfn_ref[...] * _ld(un_ref)[0:8, :] * _gelu_and_grad(gcn)[1] * _unless(i == nrow - 1)
        dg = wv[2:3, :] * dgc + wv[1:2, :] * _shift_up(dgc, dgcn, 1) + wv[0:1, :] * _shift_up(dgc, dgcn, 2)
        dg_ref[...] = dg.astype(BF16)
        _acc_w(dw_ref, (dgc * g2, dgc * g1, dgc * gv))

    o_s = pl.BlockSpec((tm, cw), lambda j, i: (i, j))
    o_sh = jax.ShapeDtypeStruct((t, D_FF), BF16)
    w_s = pl.BlockSpec((8, cw), lambda j, i: (0, j))
    return pl.pallas_call(
        body, name=name, out_shape=(o_sh, o_sh, jax.ShapeDtypeStruct((8, D_FF), F32)),
        grid=(D_FF // cw, nrow),
        in_specs=[df_s, df_n, g_s, g_p, g_n, u_s, u_n, w_s],
        out_specs=(o_s, o_s, w_s),
        compiler_params=_params(("parallel", "arbitrary")),
    )(df, df, u, u, u, u, u, w)


def _merge_fwd(ya, yb, p, name):
    t = ya.shape[0]
    tm, cw = min(TM_EW, t), CW_EW
    y_s = pl.BlockSpec((tm, cw), lambda i, j: (i, j))

    def body(ya_ref, yb_ref, ga_ref, gb_ref, m_ref):
        m_ref[...] = (_sigmoid(_ld(ga_ref)) * ya_ref[...] + _sigmoid(_ld(gb_ref)) * yb_ref[...]).astype(BF16)

    return pl.pallas_call(
        body, name=name, out_shape=jax.ShapeDtypeStruct((t, D_MODEL), BF16),
        grid=(t // tm, D_MODEL // cw),
        in_specs=[y_s, y_s, pl.BlockSpec((tm, cw), lambda i, j: (i, OFF_GA // cw + j)),
                  pl.BlockSpec((tm, cw), lambda i, j: (i, OFF_GB // cw + j))],
        out_specs=y_s, compiler_params=_params(("parallel", "parallel")),
    )(ya, yb, p, p)


_ANY = pl.BlockSpec(memory_space=pl.ANY)


def _dp_shape(t):
    return jax.ShapeDtypeStruct((t, N_IN_PAD), BF16)


def _merge_bwd(dm, y, p, gate_off, dp, name):
    t = y.shape[0]
    tm, cw = min(TM_EW, t), CW_EW
    y_s = pl.BlockSpec((tm, cw), lambda i, j: (i, j))
    g_s = pl.BlockSpec((tm, cw), lambda i, j: (i, gate_off // cw + j))

    def body(dm_ref, y_ref, g_ref, *rest):
        dy_ref, dp_ref = rest[-2:]
        dmv = dm_ref[...]
        sg = _sigmoid(_ld(g_ref))
        dy_ref[...] = (dmv * sg).astype(BF16)
        dp_ref[...] = (dmv * y_ref[...] * sg * (1.0 - sg)).astype(BF16)

    extra = [] if dp is None else [dp]
    return pl.pallas_call(
        body, name=name, out_shape=(jax.ShapeDtypeStruct((t, D_MODEL), BF16), _dp_shape(t)),
        grid=(t // tm, D_MODEL // cw),
        in_specs=[y_s, y_s, g_s] + [_ANY] * len(extra),
        out_specs=(y_s, g_s), input_output_aliases={} if dp is None else {3: 1},
        compiler_params=_params(("parallel", "parallel")),
    )(dm, y, p, *extra)


def _tri(lower):
    r = lax.broadcasted_iota(jnp.int32, (CHUNK, CHUNK), 0)
    c = lax.broadcasted_iota(jnp.int32, (CHUNK, CHUNK), 1)
    return ((c <= r) if lower else (c >= r)).astype(F32)


def _eye_mask():
    r = lax.broadcasted_iota(jnp.int32, (GLA_DK, GLA_DK), 0)
    c = lax.broadcasted_iota(jnp.int32, (GLA_DK, GLA_DK), 1)
    return r == c


def _row_to_col(v):
    return jnp.sum(jnp.where(_eye_mask(), jnp.broadcast_to(v, (GLA_DK, GLA_DK)), 0.0), axis=1, keepdims=True)


def _col_to_row(v):
    return jnp.sum(jnp.where(_eye_mask(), jnp.broadcast_to(v, (GLA_DK, GLA_DK)), 0.0), axis=0, keepdims=True)


def _dot(a, b, dn):
    return lax.dot_general(a.astype(BF16), b.astype(BF16), (dn, ((), ())), preferred_element_type=F32)


_NN = ((1,), (0,))
_NT = ((1,), (1,))
_TN = ((0,), (0,))


def _gate_logits(lr_ref, wa_ref, ba_ref):
    return _dot(lr_ref[...], wa_ref[...], _NN) + ba_ref[...]


def _chunk_decay(la, tri):
    cum = lax.dot_general(tri, la, ((_NN), ((), ())), precision=lax.Precision.HIGHEST, preferred_element_type=F32)
    e = cum[CHUNK - 1:CHUNK, :]
    return cum, e, jnp.exp(e - cum)


def _gla_fwd(p, wa, ba, name):
    t = p.shape[0]
    rows = min(GLA_ROWS, t)
    cb = rows // CHUNK
    nc = t // CHUNK
    scale = GLA_DK ** -0.5

    def body(q_ref, k_ref, v_ref, lr_ref, wa_ref, ba_ref, o_ref, st_ref, s_scr):
        @pl.when(pl.program_id(0) == 0)
        def _():
            s_scr[...] = jnp.zeros_like(s_scr)

        la_all = _log_sigmoid(_gate_logits(lr_ref, wa_ref, ba_ref)) * (1.0 / GLA_TAU)
        tri = _tri(True)
        for ch in range(cb):
            rs = slice(ch * CHUNK, (ch + 1) * CHUNK)
            for h in range(GLA_HEADS):
                ks = slice(h * GLA_DK, (h + 1) * GLA_DK)
                vs = slice(h * GLA_DV, (h + 1) * GLA_DV)
                _, e, w = _chunk_decay(la_all[rs, ks], tri)
                kd = k_ref[rs, ks].astype(F32) * w
                s_new = _row_to_col(jnp.exp(e)) * s_scr[ks, :] + _dot(kd, v_ref[rs, vs], _TN)
                s_scr[ks, :] = s_new
                st_ref[ch, ks, :] = s_new
                o_ref[rs, vs] = _dot(q_ref[rs, ks].astype(F32) * scale, s_new, _NN)

    return pl.pallas_call(
        body, name=name,
        out_shape=(jax.ShapeDtypeStruct((t, GLA_V), F32), jax.ShapeDtypeStruct((nc, GLA_QK, GLA_DV), F32)),
        grid=(t // rows,),
        in_specs=[pl.BlockSpec((rows, GLA_QK), lambda i: (i, OFF_Q // GLA_QK)),
                  pl.BlockSpec((rows, GLA_QK), lambda i: (i, OFF_K // GLA_QK)),
                  pl.BlockSpec((rows, GLA_V), lambda i: (i, OFF_V // GLA_V)),
                  pl.BlockSpec((rows, LR_PAD), lambda i: (i, OFF_LR // LR_PAD)),
                  pl.BlockSpec((LR_PAD, GLA_QK), lambda i: (0, 0)),
                  pl.BlockSpec((1, GLA_QK), lambda i: (0, 0))],
        out_specs=(pl.BlockSpec((rows, GLA_V), lambda i: (i, 0)),
                   pl.BlockSpec((cb, GLA_QK, GLA_DV), lambda i: (i, 0, 0))),
        scratch_shapes=[pltpu.VMEM((GLA_QK, GLA_DV), F32)],
        compiler_params=_params(("arbitrary",)),
    )(p, p, p, p, wa, ba)


def _gla_bwd(do, p, st, wa, ba, dp, name):
    t = p.shape[0]
    rows = min(GLA_ROWS, t)
    cb = rows // CHUNK
    nb = t // rows
    scale = GLA_DK ** -0.5

    def rev(i):
        return nb - 1 - i

    def body(do_ref, q_ref, k_ref, v_ref, lr_ref, st_ref, stp_ref, wa_ref, ba_ref, dp_in,
             dp_ref, dlr_ref, dwa_ref, dba_ref, ds_scr, dz_scr):
        dq_ref = dp_ref.at[:, OFF_Q:OFF_Q + GLA_QK]
        dk_ref = dp_ref.at[:, OFF_K:OFF_K + GLA_QK]
        dv_ref = dp_ref.at[:, OFF_V:OFF_V + GLA_V]
        i = pl.program_id(0)

        @pl.when(i == 0)
        def _():
            ds_scr[...] = jnp.zeros_like(ds_scr)
            dwa_ref[...] = jnp.zeros_like(dwa_ref)
            dba_ref[...] = jnp.zeros_like(dba_ref)

        z_all = _gate_logits(lr_ref, wa_ref, ba_ref)
        la_all = _log_sigmoid(z_all) * (1.0 / GLA_TAU)
        tri, triu = _tri(True), _tri(False)
        last_row = lax.broadcasted_iota(jnp.int32, (CHUNK, GLA_DK), 0) == CHUNK - 1
        keep_prev = _unless(i == nb - 1)
        for ch in reversed(range(cb)):
            rs = slice(ch * CHUNK, (ch + 1) * CHUNK)
            for h in range(GLA_HEADS):
                ks = slice(h * GLA_DK, (h + 1) * GLA_DK)
                vs = slice(h * GLA_DV, (h + 1) * GLA_DV)
                _, e, w = _chunk_decay(la_all[rs, ks], tri)
                kd = k_ref[rs, ks].astype(F32) * w
                exp_e = jnp.exp(e)
                s_c = st_ref[ch, ks, :]
                if ch > 0:
                    s_p = st_ref[ch - 1, ks, :]
                else:
                    s_p = stp_ref[0, ks, :] * keep_prev
                do_c = do_ref[rs, vs]
                vv = v_ref[rs, vs]
                ds_tot = ds_scr[ks, :] + _dot(q_ref[rs, ks].astype(F32) * scale, do_c, _TN)
                dq_ref[rs, ks] = (_dot(do_c, s_c, _NT) * scale).astype(BF16)
                dkd = _dot(vv, ds_tot, _NT)
                dv_ref[rs, vs] = _dot(kd, ds_tot, _NN).astype(BF16)
                dexp_col = jnp.sum(ds_tot * s_p, axis=1, keepdims=True)
                ds_scr[ks, :] = _row_to_col(exp_e) * ds_tot
                dk_ref[rs, ks] = (dkd * w).astype(BF16)
                dwt = dkd * kd
                de = jnp.sum(dwt, axis=0, keepdims=True) + _col_to_row(dexp_col) * exp_e
                dcum = jnp.where(last_row, de - dwt, -dwt)
                da = lax.dot_general(triu, dcum, (_NN, ((), ())), precision=lax.Precision.HIGHEST,
                                     preferred_element_type=F32)
                dz_scr[rs, ks] = da * (1.0 / GLA_TAU) * _sigmoid(-z_all[rs, ks])
        dz = dz_scr[...]
        dlr_ref[...] = _dot(dz, wa_ref[...], _NT).astype(BF16)
        dwa_ref[...] += _dot(lr_ref[...], dz, _TN)
        dba_ref[...] += jnp.sum(dz, axis=0, keepdims=True)

    qkv = OFF_V + GLA_V
    return pl.pallas_call(
        body, name=name,
        out_shape=(_dp_shape(t), jax.ShapeDtypeStruct((t, LR_PAD), BF16),
                   jax.ShapeDtypeStruct((LR_PAD, GLA_QK), F32), jax.ShapeDtypeStruct((1, GLA_QK), F32)),
        grid=(nb,),
        in_specs=[pl.BlockSpec((rows, GLA_V), lambda i: (rev(i), 0)),
                  pl.BlockSpec((rows, GLA_QK), lambda i: (rev(i), OFF_Q // GLA_QK)),
                  pl.BlockSpec((rows, GLA_QK), lambda i: (rev(i), OFF_K // GLA_QK)),
                  pl.BlockSpec((rows, GLA_V), lambda i: (rev(i), OFF_V // GLA_V)),
                  pl.BlockSpec((rows, LR_PAD), lambda i: (rev(i), OFF_LR // LR_PAD)),
                  pl.BlockSpec((cb, GLA_QK, GLA_DV), lambda i: (rev(i), 0, 0)),
                  pl.BlockSpec((1, GLA_QK, GLA_DV), lambda i: (jnp.maximum(rev(i) * cb - 1, 0), 0, 0)),
                  pl.BlockSpec((LR_PAD, GLA_QK), lambda i: (0, 0)),
                  pl.BlockSpec((1, GLA_QK), lambda i: (0, 0)), _ANY],
        out_specs=(pl.BlockSpec((rows, qkv), lambda i: (rev(i), 0)),
                   pl.BlockSpec((rows, LR_PAD), lambda i: (rev(i), 0)),
                   pl.BlockSpec((LR_PAD, GLA_QK), lambda i: (0, 0)),
                   pl.BlockSpec((1, GLA_QK), lambda i: (0, 0))),
        input_output_aliases={9: 0},
        scratch_shapes=[pltpu.VMEM((GLA_QK, GLA_DV), F32), pltpu.VMEM((rows, GLA_QK), F32)],
        compiler_params=_params(("arbitrary",)),
    )(do, p, p, p, p, st, st, wa, ba, dp)


def _gla_out_fwd(o, p, gng, name):
    t = o.shape[0]
    tm = min(TM_EW, t)

    def body(o_ref, r_ref, g_ref, z_ref):
        gv = g_ref[...]
        for h in range(GLA_HEADS):
            vs = slice(h * GLA_DV, (h + 1) * GLA_DV)
            ov, rv = o_ref[:, vs], r_ref[:, vs].astype(F32)
            z_ref[:, vs] = ((ov * _rstd(ov) * gv) * (rv * _sigmoid(rv))).astype(BF16)

    return pl.pallas_call(
        body, name=name, out_shape=jax.ShapeDtypeStruct((t, GLA_V), BF16), grid=(t // tm,),
        in_specs=[pl.BlockSpec((tm, GLA_V), lambda i: (i, 0)),
                  pl.BlockSpec((tm, GLA_V), lambda i: (i, OFF_R // GLA_V)),
                  pl.BlockSpec((1, GLA_DV), lambda i: (0, 0))],
        out_specs=pl.BlockSpec((tm, GLA_V), lambda i: (i, 0)),
        compiler_params=_params(("parallel",)),
    )(o, p, gng)


def _gla_out_bwd(dz, o, p, gng, dp, name):
    t = o.shape[0]
    tm = min(TM_EW, t)

    def body(dz_ref, o_ref, r_ref, g_ref, dp_in, do_ref, dr_ref, dg_ref):
        @pl.when(pl.program_id(0) == 0)
        def _():
            dg_ref[...] = jnp.zeros_like(dg_ref)

        gv = g_ref[...]
        for h in range(GLA_HEADS):
            vs = slice(h * GLA_DV, (h + 1) * GLA_DV)
            ov, rv, dzv = o_ref[:, vs], r_ref[:, vs].astype(F32), dz_ref[:, vs]
            rs = _rstd(ov)
            oh = ov * rs
            sg = _sigmoid(rv)
            dr_ref[:, vs] = (dzv * (oh * gv) * (sg * (1.0 + rv * (1.0 - sg)))).astype(BF16)
            don = dzv * (rv * sg)
            dg_ref[...] += jnp.sum(don * oh, axis=0, keepdims=True)
            doh = don * gv
            do_ref[:, vs] = rs * (doh - oh * jnp.mean(doh * oh, axis=-1, keepdims=True))

    row = pl.BlockSpec((tm, GLA_V), lambda i: (i, 0))
    r_s = pl.BlockSpec((tm, GLA_V), lambda i: (i, OFF_R // GLA_V))
    return pl.pallas_call(
        body, name=name,
        out_shape=(jax.ShapeDtypeStruct((t, GLA_V), F32), _dp_shape(t), jax.ShapeDtypeStruct((1, GLA_DV), F32)),
        grid=(t // tm,),
        in_specs=[row, row, r_s, pl.BlockSpec((1, GLA_DV), lambda i: (0, 0)), _ANY],
        out_specs=(row, r_s, pl.BlockSpec((1, GLA_DV), lambda i: (0, 0))),
        input_output_aliases={4: 1},
        compiler_params=_params(("arbitrary",)),
    )(dz, o, p, gng, dp)


def _ada_fwd(c_all, w, b, layer, name):
    n = w.shape[2]
    tn = _pick(n, 512)

    def body(c_ref, w_ref, b_ref, o_ref):
        cv = c_ref[...]
        o_ref[...] = _dot(cv * _sigmoid(cv), w_ref[...], _NN) + b_ref[...]

    return pl.pallas_call(
        body, name=name, out_shape=jax.ShapeDtypeStruct((16, n), F32), grid=(n // tn,),
        in_specs=[pl.BlockSpec((16, D_MODEL), lambda j: (0, 0)),
                  pl.BlockSpec((None, D_MODEL, tn), lambda j: (layer, 0, j)),
                  pl.BlockSpec((1, tn), lambda j: (0, j))],
        out_specs=pl.BlockSpec((16, tn), lambda j: (0, j)),
        compiler_params=_params(("parallel",)),
    )(c_all, w, b)


def _ada_bwd(c_all, dmod, name):
    n = dmod.shape[2]
    tn = _pick(n, 512)

    def body(c_ref, d_ref, o_ref):
        cv = c_ref[...]
        o_ref[...] = _dot(cv * _sigmoid(cv), d_ref[...], _TN)

    return pl.pallas_call(
        body, name=name, out_shape=jax.ShapeDtypeStruct((DEPTH, D_MODEL, n), F32), grid=(DEPTH, n // tn),
        in_specs=[pl.BlockSpec((16, D_MODEL), lambda l, j: (0, 0)),
                  pl.BlockSpec((None, 16, tn), lambda l, j: (l, 0, j))],
        out_specs=pl.BlockSpec((None, D_MODEL, tn), lambda l, j: (l, 0, j)),
        compiler_params=_params(("parallel", "parallel")),
    )(c_all, dmod)


def _rows_tile(nrows, ncols, target_bytes):
    want = max(16, target_bytes // (4 * ncols))
    if nrows <= want:
        return nrows
    t = (want // 16) * 16
    while t >= 16:
        if nrows % t == 0:
            return t
        t -= 16
    return nrows


def _sum_chips(sent, landed, chip, name):
    _, nrows, ncols = sent[0].shape
    tr = _rows_tile(nrows, ncols, 2 << 20)
    nblk = nrows // tr

    def body(chip_ref, *refs):
        own, got, o_ref = refs[:DEPTH], refs[DEPTH:2 * DEPTH], refs[2 * DEPTH]
        me = chip_ref[0]
        for l in range(DEPTH):
            for m in range(N_CHIPS):
                @pl.when(jnp.logical_and(pl.program_id(0) == l, me == m))
                def _():
                    acc = None
                    for j in range(N_CHIPS):
                        val = (own[l][...] if j == m else got[l][j]).astype(F32)
                        acc = val if acc is None else acc + val
                    o_ref[...] = acc

    def rows_of(layer):
        return lambda l, i, chip_ref: jnp.where(l == layer, i, 0)

    own_specs = [pl.BlockSpec((None, tr, ncols), lambda l, i, chip_ref, r=rows_of(k): (chip_ref[0], r(l, i, chip_ref), 0))
                 for k in range(DEPTH)]
    got_specs = [pl.BlockSpec((N_CHIPS, tr, ncols), lambda l, i, chip_ref, r=rows_of(k): (0, r(l, i, chip_ref), 0))
                 for k in range(DEPTH)]
    return pl.pallas_call(
        body, name=name, out_shape=jax.ShapeDtypeStruct((DEPTH, nrows, ncols), F32),
        grid_spec=pltpu.PrefetchScalarGridSpec(
            num_scalar_prefetch=1, grid=(DEPTH, nblk), in_specs=own_specs + got_specs,
            out_specs=pl.BlockSpec((None, tr, ncols), lambda l, i, chip_ref: (l, i, 0))),
        compiler_params=_params(("arbitrary", "arbitrary")),
    )(chip, *sent, *landed)


def _adamw(w, m, v, ga, gb, name, tile=None):
    two = gb is not None
    c1 = 1.0 - ADAM_B1 ** ADAM_STEP
    c2 = 1.0 - ADAM_B2 ** ADAM_STEP

    def body(*refs):
        if two:
            w_ref, m_ref, v_ref, ga_ref, gb_ref, g_ref, d_ref, nm_ref, nv_ref = refs
            g = ga_ref[...] + gb_ref[...]
        else:
            w_ref, m_ref, v_ref, ga_ref, g_ref, d_ref, nm_ref, nv_ref = refs
            g = ga_ref[...]
        g_ref[...] = g
        nm = ADAM_B1 * m_ref[...] + (1.0 - ADAM_B1) * g
        nv = ADAM_B2 * v_ref[...] + (1.0 - ADAM_B2) * (g * g)
        nm_ref[...] = nm
        nv_ref[...] = nv
        d_ref[...] = -ADAM_LR * ((nm / c1) / (jnp.sqrt(nv / c2) + ADAM_EPS) + ADAM_WD * w_ref[...])

    if tile is None:
        nl, nrows, ncols = w.shape
        tr = _rows_tile(nrows, ncols, 1 << 20)
        blk = pl.BlockSpec((None, tr, ncols), lambda l, i: (l, i, 0))
        grid = (nl, nrows // tr)
    else:
        nrows, nl, ncols = w.shape
        rb, cb = tile
        blk = pl.BlockSpec((rb, nl, cb), lambda i, j: (i, 0, j))
        grid = (nrows // rb, ncols // cb)
    sh = jax.ShapeDtypeStruct(w.shape, F32)
    ins = [w, m, v, ga] + ([gb] if two else [])
    return pl.pallas_call(
        body, name=name, out_shape=(sh, sh, sh, sh), grid=grid,
        in_specs=[blk] * len(ins), out_specs=(blk, blk, blk, blk),
        compiler_params=_params(("parallel", "parallel")),
    )(*ins)


def _pad_rows(a, rows):
    return jnp.concatenate([a, jnp.zeros((rows - a.shape[0],) + a.shape[1:], a.dtype)], axis=0)


N_IN_CHIP = N_IN // N_CHIPS
_LR_LO = 3072 - N_IN_CHIP
_LR_HI = _LR_LO + GLA_LOWRANK


def _w_in_from_chips(a):
    return jnp.concatenate([a[0], a[1][:, :_LR_LO], a[1][:, _LR_HI:], a[2], a[3], a[1][:, _LR_LO:_LR_HI],
                            jnp.zeros((a.shape[1], LR_PAD - GLA_LOWRANK), a.dtype)], axis=1)


def _w_in_to_chips(w):
    s2 = 2 * N_IN_CHIP - GLA_LOWRANK
    s3 = s2 + N_IN_CHIP
    c1 = jnp.concatenate([w[:, N_IN_CHIP:3072], w[:, OFF_LR:OFF_LR + GLA_LOWRANK], w[:, 3072:s2]], axis=1)
    return jnp.stack([w[:, :N_IN_CHIP], c1, w[:, s2:s3], w[:, s3:OFF_LR]])


_BIG = ("w_in", "w_og", "w_oc", "w_o", "w_up", "w_dn")
_ROW_SHARDED = ("w_o", "w_dn")
_TWO_LEVEL = ("w_in", "w_up")


def kernel(x, c, w_ada, b_ada, norm_g, w_in, w_a2, b_a2, gla_norm_g, w_out_gla, conv_mix_w, w_out_conv, w_o, w_up, ffn_conv_w, w_down, loss_target, m_w_ada, m_b_ada, m_norm_g, m_w_in, m_w_a2, m_b_a2, m_gla_norm_g, m_w_out_gla, m_conv_mix_w, m_w_out_conv, m_w_o, m_w_up, m_ffn_conv_w, m_w_down, v_w_ada, v_b_ada, v_norm_g, v_w_in, v_w_a2, v_b_a2, v_gla_norm_g, v_w_out_gla, v_conv_mix_w, v_w_out_conv, v_w_o, v_w_up, v_ffn_conv_w, v_w_down):
    xi, yi, ci = lax.axis_index("x"), lax.axis_index("y"), lax.axis_index("c")
    chip = 2 * xi + yi
    dev = 2 * chip + ci
    chip_arr = jnp.reshape(chip, (1,)).astype(jnp.int32)
    xt = x[0]
    tgt = loss_target[0]

    c_all = _allgather8(jnp.broadcast_to(c, (8, D_MODEL)), "gather_c")[0][:, 0, :]
    c16 = _pad_rows(c_all, 16)
    sm_parts = [norm_g.reshape(-1), w_a2.reshape(-1), conv_mix_w.reshape(-1), ffn_conv_w.reshape(-1)]
    sm_sizes = [a.shape[0] for a in sm_parts]
    sm_flat = jnp.concatenate(sm_parts)
    sm_rows = -(-sm_flat.shape[0] // 128)
    sm_rows = -(-sm_rows // 8) * 8
    sm_flat = jnp.concatenate([sm_flat, jnp.zeros((sm_rows * 128 - sm_flat.shape[0],), F32)]).reshape(sm_rows, 128)
    sm_all = _allgather8(sm_flat, "gather_small")[0].reshape(N_DEV, -1)[0::2]
    offs = [0]
    for s in sm_sizes:
        offs.append(offs[-1] + s)

    def small_full(idx, shape):
        a = sm_all[:, offs[idx]:offs[idx + 1]].reshape((N_CHIPS,) + shape)
        a = jnp.moveaxis(a, 0, -2)
        return a.reshape(shape[:-1] + (N_CHIPS * shape[-1],))

    norm_g_f = small_full(0, (DEPTH, 4, 512))
    w_a2_f = small_full(1, (DEPTH, GLA_LOWRANK, 128))
    conv_w_f = small_full(2, (DEPTH, 3, 256))
    ffn_w_f = small_full(3, (DEPTH, 3, 1408))

    b_loc = lax.dynamic_slice(b_ada, (0, chip * 3072), (DEPTH, 3072))
    mod_loc = jnp.concatenate(
        [_ada_fwd(c16, w_ada, b_loc[l:l + 1], l, "ada_fwd")[:8] for l in range(DEPTH)], axis=0)
    mod_all = _allgather8(mod_loc, "gather_mod")[0][0::2]
    mods = []
    for l in range(DEPTH):
        row = lax.dynamic_slice(mod_all, (0, l * 8 + dev, 0), (N_CHIPS, 1, 3072)).reshape(1, 6 * D_MODEL)
        mods.append([row[:, k * D_MODEL:(k + 1) * D_MODEL] for k in range(6)])

    big = dict(w_in=w_in, w_og=w_out_gla, w_oc=w_out_conv, w_o=w_o, w_up=w_up, w_dn=w_down)
    gathers = {}
    def start_gathers(l, tok):
        for k in _BIG:
            shard = (big[k][l] + tok).astype(BF16)
            if k in _TWO_LEVEL:
                shard = shard.reshape(2, shard.shape[0] // 2, shard.shape[1])
            land = lax.dynamic_update_slice(lax.empty((N_CHIPS,) + shard.shape, BF16), shard[None],
                                            (chip,) + (0,) * shard.ndim)
            *handle, token = _gather_start(land, "gather_start_%s_%d" % (k, l), k in _TWO_LEVEL)
            gathers[k, l] = tuple(handle)
            tok = token[0, 0]
        return tok

    tok = start_gathers(0, 0.0 * (mod_all[0, 0, 0] + sm_all[0, 0]))

    fills = {}

    def fill_early(k, l, after):
        land = _gather_wait(gathers[k, l], after, "gather_wait_%s_%d" % (k, l), True)
        *handle, token = _fill_start(land, "fill_start_%s_%d" % (k, l))
        fills[k, l] = tuple(handle)
        return token

    def gathered(k, l, after):
        if k in _TWO_LEVEL:
            if (k, l) not in fills:
                fill_early(k, l, after)
            full = _fill_wait(fills[k, l], after, "fill_wait_%s_%d" % (k, l))
            full = full.reshape(N_CHIPS, 2 * full.shape[2], full.shape[3])
        else:
            full = _gather_wait(gathers[k, l], after, "gather_wait_%s_%d" % (k, l))
        if k in _ROW_SHARDED:
            return full.reshape(N_CHIPS * full.shape[1], full.shape[2])
        return _w_in_from_chips(full) if k == "w_in" else full

    saved = []
    h = None
    xin = xt
    for l in range(DEPTH):
        sh1, sc1, g1, sh2, sc2, g2 = mods[l]
        gn = [norm_g_f[l, k][None] for k in range(4)]
        wa = _pad_rows(w_a2_f[l], LR_PAD)
        ba = b_a2[l][None]
        gng = gla_norm_g[l][None]
        cw8 = _pad_rows(conv_w_f[l], 8)
        fw8 = _pad_rows(ffn_w_f[l], 8)
        if l == 0:
            h = _pre_norm(xin, gn[0] + tok, sc1, sh1, "pre_norm")
        wi = gathered("w_in", l, h)
        p = _matmul(h, wi, "nn", BF16, "mm_in", tn=1152)
        if l == 0:
            late = 0.0 * p[0, 0].astype(F32)
            for nl in range(1, DEPTH):
                late = start_gathers(nl, late)
            wa = wa + late
        o, st = _gla_fwd(p, wa, ba, "gla_fwd")
        za = _gla_out_fwd(o, p, gng, "gla_out_fwd")
        zb = _conv_fwd(p, cw8, "conv_fwd")
        wog, woc = gathered("w_og", l, zb), gathered("w_oc", l, zb)
        ya = _matmul(za, wog, "nn", F32, "mm_out_gla", tm=2048, b_chips=True)
        yb = _matmul(zb, woc, "nn", F32, "mm_out_conv", tm=2048, b_chips=True)
        mm = _merge_fwd(ya, yb, p, "merge_fwd")
        wo = gathered("w_o", l, fill_early("w_up", l, mm))
        y = _matmul(mm, wo, "nn", F32, "mm_o")
        x1, h2 = _post_pre(xin, y, g1, gn[1], gn[2], sc2, sh2, "post_pre")
        wup = gathered("w_up", l, h2)
        u = _matmul(h2, wup, "nn", BF16, "mm_up", tn=1408, b_chips=True)
        f = _ffn_fwd(u, fw8, "ffn_fwd")
        wdn = gathered("w_dn", l, fill_early("w_in", l + 1, f) if l + 1 < DEPTH else f)
        y2 = _matmul(f, wdn, "nn", F32, "mm_down", tm=1024, tn=2048, tk=1408)
        saved.append(dict(xin=xin, h=h, p=p, o=o, st=st, za=za, zb=zb, ya=ya, yb=yb, mm=mm, y=y, x1=x1, h2=h2,
                          u=u, f=f, y2=y2, wi=wi, wog=wog, woc=woc, wo=wo, wup=wup, wdn=wdn, wa=wa, ba=ba,
                          gng=gng, cw8=cw8, fw8=fw8, gn=gn, mod=mods[l]))
        if l + 1 < DEPTH:
            nsh1, nsc1 = mods[l + 1][0], mods[l + 1][1]
            xin, h = _post_pre(x1, y2, g2, gn[3], norm_g_f[l + 1, 0][None], nsc1, nsh1, "post_pre")
        else:
            dx, loss_tile = _post_loss(x1, y2, g2, gn[3], tgt, "post_loss")
    loss = lax.psum(loss_tile[0, 0], ("x", "y", "c"))

    scatters = {}

    def scatter(k, l, dw, after=None):
        if k in _ROW_SHARDED:
            send = dw.reshape(N_CHIPS, dw.shape[0] // N_CHIPS, dw.shape[1])
        else:
            send = _w_in_to_chips(dw) if k == "w_in" else dw
        *handle, token = _scatter_start(send, "scatter_start_%s_%d" % (k, l), after)
        scatters[k, l] = tuple(handle)
        return token[0, 0]

    sm = {k: [None] * DEPTH for k in ("dmod", "norm_g", "w_a2", "b_a2", "gng", "conv_w", "ffn_w")}
    for l in reversed(range(DEPTH)):
        s = saved[l]
        sh1, sc1, g1, sh2, sc2, g2 = s["mod"]
        gn = s["gn"]
        dy2, dg2, dgn3 = _post_bwd(dx, s["y2"], g2, gn[3], "post_bwd")
        tk = scatter("w_dn", l, _matmul(s["f"], dy2, "tn", BF16, "mm_down_dw", tm=512, tn=1024, tk=4096))
        df = _matmul(dy2, s["wdn"], "nt", F32, "mm_down_dx", tn=1408)
        dgate, dup, dfw = _ffn_bwd(df, s["u"], s["fw8"] + tk, "ffn_bwd")
        du = (dgate, dup)
        tk = scatter("w_up", l, _matmul(s["h2"], du, "tn", BF16, "mm_up_dw", tm=1024, tn=1408, tk=2048, out_chips=True))
        dh2 = _matmul(du, s["wup"], "nt", F32, "mm_up_dx", tm=1024, tn=2048, tk=1408, b_chips=True)
        dx1, dsh2, dsc2, dgn2 = _pre_bwd(dh2, s["x1"], dx, gn[2] + tk, sc2, "pre_bwd")
        dy, dg1, dgn1 = _post_bwd(dx1, s["y"], g1, gn[1], "post_bwd")
        tk = scatter("w_o", l, _matmul(s["mm"], dy, "tn", BF16, "mm_o_dw", tk=4096))
        dm = _matmul(dy, s["wo"], "nt", F32, "mm_o_dx")
        dya, dp = _merge_bwd(dm, s["ya"], s["p"], OFF_GA, None, "merge_bwd_a")
        dyb, dp = _merge_bwd(dm, s["yb"], s["p"], OFF_GB, dp, "merge_bwd_b")
        tk = tk + scatter("w_og", l, _matmul(s["za"], dya, "tn", BF16, "mm_out_gla_dw", tk=4096, out_chips=True))
        dza = _matmul(dya, s["wog"], "nt", F32, "mm_out_gla_dx", tm=2048, b_chips=True)
        do, dp, dgng = _gla_out_bwd(dza, s["o"], s["p"], s["gng"] + tk, dp, "gla_out_bwd")
        tk = scatter("w_oc", l, _matmul(s["zb"], dyb, "tn", BF16, "mm_out_conv_dw", tk=4096, out_chips=True))
        dzb = _matmul(dyb, s["woc"], "nt", F32, "mm_out_conv_dx", tm=2048, b_chips=True)
        dp, dcw = _conv_bwd(dzb, s["p"], s["cw8"] + tk, dp, "conv_bwd")
        dp, dlr, dwa, dba = _gla_bwd(do, s["p"], s["st"], s["wa"], s["ba"], dp, "gla_bwd")
        dp = lax.dynamic_update_slice(dp, dlr, (0, OFF_LR))
        dw_in = _matmul(s["h"], dp, "tn", BF16, "mm_in_dw", tm=512, tn=1152, tk=4096)
        tk = scatter("w_in", l, dw_in) if l > 0 else 0.0
        dh = _matmul(dp, s["wi"], "nt", F32, "mm_in_dx", tm=1024, tn=2048, tk=1152)
        dx, dsh1, dsc1, dgn0 = _pre_bwd(dh, s["xin"], dx1, gn[0] + tk, sc1, "pre_bwd")
        sm["dmod"][l] = jnp.concatenate([dsh1, dsc1, dg1, dsh2, dsc2, dg2], axis=1)[0]
        sm["norm_g"][l] = jnp.concatenate([dgn0, dgn1, dgn2, dgn3], axis=0)
        sm["w_a2"][l] = dwa[:GLA_LOWRANK]
        sm["b_a2"][l] = dba[0]
        sm["gng"][l] = dgng[0]
        sm["conv_w"][l] = dcw[:3]
        sm["ffn_w"][l] = dfw[:3]
    grad_x = dx[None]

    names = ("dmod", "norm_g", "w_a2", "b_a2", "gng", "conv_w", "ffn_w")
    parts = [jnp.stack(sm[k]).reshape(-1) for k in names]
    shapes = [jnp.stack(sm[k]).shape for k in names]
    sizes = [a.shape[0] for a in parts]
    flat = jnp.concatenate(parts)
    rows = -(-flat.shape[0] // 1024) * 8
    flat = jnp.concatenate([flat, jnp.zeros((rows * 128 - flat.shape[0],), F32)]).reshape(rows, 128)
    gath, tot = _allgather8(flat, "reduce_small")
    tk = scatter("w_in", 0, dw_in, after=tot)
    c16 = c16 + tk
    po = [0]
    for s_ in sizes:
        po.append(po[-1] + s_)
    tot = tot.reshape(-1)
    tot_of = {k: tot[po[i]:po[i + 1]].reshape(shapes[i]) for i, k in enumerate(names)}
    dmod_all = gath.reshape(N_DEV, -1)[:, po[0]:po[1]].reshape(N_DEV, DEPTH, 6 * D_MODEL)

    def chip_cols(a, width):
        return lax.dynamic_slice_in_dim(a, chip * width, width, axis=a.ndim - 1)

    dml = jnp.transpose(chip_cols(dmod_all, 3072), (1, 0, 2))
    dml = jnp.concatenate([dml, jnp.zeros_like(dml)], axis=1)
    g_w_ada = _ada_bwd(c16, dml, "ada_bwd")

    def upd(w, m, v, ga, gb, name):
        sh = w.shape
        as3 = sh if len(sh) == 3 else (1,) + sh
        outs = _adamw(w.reshape(as3), m.reshape(as3), v.reshape(as3), ga.reshape(as3),
                      None if gb is None else gb.reshape(as3), name)
        return [a.reshape(sh) for a in outs]

    res = {}
    res["w_ada"] = upd(w_ada, m_w_ada, v_w_ada, g_w_ada, None, "adamw")
    res["b_ada"] = upd(b_ada, m_b_ada, v_b_ada, tot_of["dmod"], None, "adamw")
    res["norm_g"] = upd(norm_g, m_norm_g, v_norm_g, chip_cols(tot_of["norm_g"], 512), None, "adamw")
    res["w_a2"] = upd(w_a2, m_w_a2, v_w_a2, chip_cols(tot_of["w_a2"], 128), None, "adamw")
    res["b_a2"] = upd(b_a2, m_b_a2, v_b_a2, tot_of["b_a2"], None, "adamw")
    res["gla_norm_g"] = upd(gla_norm_g, m_gla_norm_g, v_gla_norm_g, tot_of["gng"], None, "adamw")
    res["conv_mix_w"] = upd(conv_mix_w, m_conv_mix_w, v_conv_mix_w, chip_cols(tot_of["conv_w"], 256), None, "adamw")
    res["ffn_conv_w"] = upd(ffn_conv_w, m_ffn_conv_w, v_ffn_conv_w, chip_cols(tot_of["ffn_w"], 1408), None, "adamw")

    full_name = dict(w_in="w_in", w_og="w_out_gla", w_oc="w_out_conv", w_o="w_o", w_up="w_up", w_dn="w_down")
    state = dict(w_in=(w_in, m_w_in, v_w_in), w_og=(w_out_gla, m_w_out_gla, v_w_out_gla),
                 w_oc=(w_out_conv, m_w_out_conv, v_w_out_conv), w_o=(w_o, m_w_o, v_w_o),
                 w_up=(w_up, m_w_up, v_w_up), w_dn=(w_down, m_w_down, v_w_down))
    def finish(k, handle, after):
        plane, other = _sibling_wait(handle, after, "sibling_wait_" + k)
        if k == "w_in":
            outs = _adamw(*[jnp.transpose(a, (2, 0, 1)) for a in state[k]], plane, other, "adamw_w_in",
                          tile=(N_IN_CHIP // 4, D_MODEL // 8))
            res[full_name[k]] = [jnp.transpose(a, (1, 2, 0)) for a in outs]
        else:
            res[full_name[k]] = upd(*state[k], plane, other, "adamw")

    after = res["w_ada"][3]
    pending = None
    for k in ("w_dn", "w_up", "w_o", "w_og", "w_oc", "w_in"):
        done = [_scatter_wait(scatters[k, l], after, "scatter_wait_%s_%d" % (k, l)) for l in range(DEPTH)]
        plane = _sum_chips([d[0] for d in done], [d[1] for d in done], chip_arr, "sum_chips")
        if k == "w_in":
            plane = jnp.transpose(plane, (2, 0, 1))
        *handle, after = _sibling_start(plane, "sibling_start_" + k)
        if pending is not None:
            finish(*pending, plane)
        pending = (k, tuple(handle))
    finish(*pending, after)
    order = ("w_ada", "b_ada", "norm_g", "w_in", "w_a2", "b_a2", "gla_norm_g", "w_out_gla", "conv_mix_w",
             "w_out_conv", "w_o", "w_up", "ffn_conv_w", "w_down")
    return (loss, grad_x, *[res[k][0] for k in order], *[res[k][1] for k in order],
            *[res[k][2] for k in order], *[res[k][3] for k in order])
```
